```python
import jax, jax.numpy as jnp
from jax import lax
import numpy as np

D_MODEL = 1024
BATCH = 16
SEQ = 2048
DEPTH = 4

N_HEADS = 8
QK_NOPE_DIM = 128
QK_ROPE_DIM = 64
QK_DIM = QK_NOPE_DIM + QK_ROPE_DIM
V_HEAD_DIM = 128
Q_LORA_RANK = 384
KV_LORA_RANK = 256
ROPE_THETA = 10000.0
Q_BLOCK = 128
POOL_WINDOWS = (2, 4, 8, 16)
N_POOL_GROUPS = 4
POOL_GROUP_DIM = 128
POOL_DIM = N_POOL_GROUPS * POOL_GROUP_DIM
N_BRANCHES = 2
IN_DIM = POOL_DIM + Q_LORA_RANK + KV_LORA_RANK + QK_ROPE_DIM + N_BRANCHES * D_MODEL
SPLIT_POINTS = (POOL_DIM,
                POOL_DIM + Q_LORA_RANK,
                POOL_DIM + Q_LORA_RANK + KV_LORA_RANK,
                POOL_DIM + Q_LORA_RANK + KV_LORA_RANK + QK_ROPE_DIM)
D_FF = 2816
NORM_EPS = 1e-6

kernel_name = "macaron_gated_pool_mla_trunk"


def rms_norm(x, g):
    xf = x.astype(jnp.float32)
    y = xf * lax.rsqrt(jnp.mean(xf * xf, axis=-1, keepdims=True) + NORM_EPS)
    return (y * g.astype(jnp.float32)).astype(x.dtype)


def swiglu_ffn(h, w_up, w_down):
    gate, up = jnp.split(h @ w_up, 2, axis=-1)
    return (jax.nn.silu(gate) * up) @ w_down


def rope_tables(positions):
    inv_freq = ROPE_THETA ** (-jnp.arange(0, QK_ROPE_DIM, 2, dtype=jnp.float32) / QK_ROPE_DIM)
    ang = positions.astype(jnp.float32)[..., None] * inv_freq
    return jnp.cos(ang), jnp.sin(ang)


def apply_rope(x, cos, sin):
    xf = x.astype(jnp.float32)
    x1, x2 = jnp.split(xf, 2, axis=-1)
    out = jnp.concatenate([x1 * cos - x2 * sin, x2 * cos + x1 * sin], axis=-1)
    return out.astype(x.dtype)


def causal_multiscale_pool(xp, pool_maps, pool_scale):
    B, S, _ = xp.shape
    xg = xp.reshape(B, S, N_POOL_GROUPS, POOL_GROUP_DIM).astype(jnp.float32)
    csum = jnp.pad(jnp.cumsum(xg, axis=1), ((0, 0), (1, 0), (0, 0), (0, 0)))
    windows = jnp.array(POOL_WINDOWS, dtype=jnp.int32)
    t = jnp.arange(S, dtype=jnp.int32)[:, None]
    start = jnp.maximum(t + 1 - windows[None, :], 0)
    csum_start = csum[:, start, jnp.arange(N_POOL_GROUPS)[None, :]]
    count = jnp.minimum(t + 1, windows[None, :]).astype(jnp.float32)
    pooled = (csum[:, 1:] - csum_start) / count[None, :, :, None] - xg
    mixed = jnp.einsum('bsgc,gcd->bsgd', pooled.astype(xp.dtype), pool_maps)
    return mixed.reshape(B, S, POOL_DIM) * pool_scale


def mla_attention(q_lat, kv_lat, k_rope, cos, sin, q_norm, w_uq, kv_norm, w_ukv):
    B, S, _ = q_lat.shape
    q = (rms_norm(q_lat, q_norm) @ w_uq).reshape(B, S, N_HEADS, QK_DIM)
    q_nope, q_rope = q[..., :QK_NOPE_DIM], q[..., QK_NOPE_DIM:]
    kv = (rms_norm(kv_lat, kv_norm) @ w_ukv).reshape(B, S, N_HEADS, QK_NOPE_DIM + V_HEAD_DIM)
    k_nope, v = kv[..., :QK_NOPE_DIM], kv[..., QK_NOPE_DIM:]
    q_rope = apply_rope(q_rope, cos[:, :, None, :], sin[:, :, None, :])
    k_rope = apply_rope(k_rope, cos, sin)
    scale = QK_DIM ** -0.5
    outs = []
    for blk in range(S // Q_BLOCK):
        q0 = blk * Q_BLOCK
        k_end = q0 + Q_BLOCK
        s = (jnp.einsum('bqhd,bkhd->bhqk', q_nope[:, q0:k_end], k_nope[:, :k_end])
             + jnp.einsum('bqhr,bkr->bhqk', q_rope[:, q0:k_end], k_rope[:, :k_end]))
        s = s.astype(jnp.float32) * scale
        mask = (q0 + jnp.arange(Q_BLOCK))[:, None] >= jnp.arange(k_end)[None, :]
        s = jnp.where(mask[None, None], s, jnp.finfo(jnp.float32).min)
        p = jax.nn.softmax(s, axis=-1).astype(v.dtype)
        outs.append(jnp.einsum('bhqk,bkhd->bqhd', p, v[:, :k_end]))
    o = jnp.concatenate(outs, axis=1)
    return o.reshape(B, S, N_HEADS * V_HEAD_DIM)


def _fwd_setup_inputs(seed: int = 0) -> dict:
    key = jax.random.key(seed)
    ks = jax.random.split(key, 24)
    f32 = jnp.float32

    def w(k, shape, fan_in):
        return jax.random.normal(k, shape, f32) * (fan_in ** -0.5)

    def gain(k, shape):
        return 1.0 + 0.02 * jax.random.normal(k, shape, f32)

    L, D = DEPTH, D_MODEL
    offsets = jax.random.randint(ks[1], (BATCH, 1), 0, 4096, dtype=jnp.int32)
    positions = offsets + jnp.arange(SEQ, dtype=jnp.int32)[None, :]
    return {
        "x": jax.random.normal(ks[0], (BATCH, SEQ, D), f32),
        "positions": positions,
        "norm_ffn1": gain(ks[2], (L, D)),
        "ffn1_up": w(ks[3], (L, D, 2 * D_FF), D),
        "ffn1_down": w(ks[4], (L, D_FF, D), D_FF),
        "norm_mix": gain(ks[5], (L, D)),
        "w_in": w(ks[6], (L, D, IN_DIM), D),
        "b_gate": 0.02 * jax.random.normal(ks[7], (L, N_BRANCHES * D), f32),
        "pool_maps": w(ks[8], (L, N_POOL_GROUPS, POOL_GROUP_DIM, POOL_GROUP_DIM), POOL_GROUP_DIM),
        "pool_scale": 1.0 + 0.1 * jax.random.normal(ks[9], (L, POOL_DIM), f32),
        "w_pool_proj": w(ks[10], (L, POOL_DIM, D), POOL_DIM),
        "q_latent_norm": gain(ks[11], (L, Q_LORA_RANK)),
        "w_uq": w(ks[12], (L, Q_LORA_RANK, N_HEADS * QK_DIM), Q_LORA_RANK),
        "kv_latent_norm": gain(ks[13], (L, KV_LORA_RANK)),
        "w_ukv": w(ks[14], (L, KV_LORA_RANK, N_HEADS * (QK_NOPE_DIM + V_HEAD_DIM)), KV_LORA_RANK),
        "w_attn_proj": w(ks[15], (L, N_HEADS * V_HEAD_DIM, D), N_HEADS * V_HEAD_DIM),
        "w_out": w(ks[16], (L, D, D), D),
        "norm_ffn2": gain(ks[17], (L, D)),
        "ffn2_up": w(ks[18], (L, D, 2 * D_FF), D),
        "ffn2_down": w(ks[19], (L, D_FF, D), D_FF),
        "final_norm": gain(ks[20], (D,)),
    }


def _fwd_reference(x, positions, norm_ffn1, ffn1_up, ffn1_down, norm_mix, w_in, b_gate,
              pool_maps, pool_scale, w_pool_proj, q_latent_norm, w_uq, kv_latent_norm,
              w_ukv, w_attn_proj, w_out, norm_ffn2, ffn2_up, ffn2_down, final_norm):
    B, S, D = x.shape
    cos, sin = rope_tables(positions)
    for l in range(DEPTH):
        x = x + 0.5 * swiglu_ffn(rms_norm(x, norm_ffn1[l]), ffn1_up[l], ffn1_down[l])
        h = rms_norm(x, norm_mix[l])
        proj = h @ w_in[l]
        x_pool, q_lat, kv_lat, k_rope, gate_logits = jnp.split(proj, SPLIT_POINTS, axis=-1)
        gates = jax.nn.sigmoid((gate_logits + b_gate[l]).astype(jnp.float32)).astype(x.dtype)
        gates = gates.reshape(B, S, N_BRANCHES, D)
        branch_a = causal_multiscale_pool(x_pool, pool_maps[l], pool_scale[l]) @ w_pool_proj[l]
        branch_b = mla_attention(q_lat, kv_lat, k_rope, cos, sin, q_latent_norm[l], w_uq[l],
                                 kv_latent_norm[l], w_ukv[l]) @ w_attn_proj[l]
        merged = gates[:, :, 0] * branch_a + gates[:, :, 1] * branch_b
        x = x + merged @ w_out[l]
        x = x + 0.5 * swiglu_ffn(rms_norm(x, norm_ffn2[l]), ffn2_up[l], ffn2_down[l])
    return rms_norm(x, final_norm)


import jax as _jax
import jax.numpy as _jnp

TWIN_FORMAT = 'train_step'
FWD_PARAMS = ['x', 'positions', 'norm_ffn1', 'ffn1_up', 'ffn1_down', 'norm_mix', 'w_in', 'b_gate', 'pool_maps', 'pool_scale', 'w_pool_proj', 'q_latent_norm', 'w_uq', 'kv_latent_norm', 'w_ukv', 'w_attn_proj', 'w_out', 'norm_ffn2', 'ffn2_up', 'ffn2_down', 'final_norm']
TWIN_WEIGHTS = ['norm_ffn1', 'ffn1_up', 'ffn1_down', 'norm_mix', 'w_in', 'b_gate', 'pool_maps', 'pool_scale', 'w_pool_proj', 'q_latent_norm', 'w_uq', 'kv_latent_norm', 'w_ukv', 'w_attn_proj', 'w_out', 'norm_ffn2', 'ffn2_up', 'ffn2_down', 'final_norm']
TWIN_DIFF_INPUT = 'x'
TWIN_INPUTS = ['x', 'positions', 'norm_ffn1', 'ffn1_up', 'ffn1_down', 'norm_mix', 'w_in', 'b_gate', 'pool_maps', 'pool_scale', 'w_pool_proj', 'q_latent_norm', 'w_uq', 'kv_latent_norm', 'w_ukv', 'w_attn_proj', 'w_out', 'norm_ffn2', 'ffn2_up', 'ffn2_down', 'final_norm', 'loss_target', 'm_norm_ffn1', 'm_ffn1_up', 'm_ffn1_down', 'm_norm_mix', 'm_w_in', 'm_b_gate', 'm_pool_maps', 'm_pool_scale', 'm_w_pool_proj', 'm_q_latent_norm', 'm_w_uq', 'm_kv_latent_norm', 'm_w_ukv', 'm_w_attn_proj', 'm_w_out', 'm_norm_ffn2', 'm_ffn2_up', 'm_ffn2_down', 'm_final_norm', 'v_norm_ffn1', 'v_ffn1_up', 'v_ffn1_down', 'v_norm_mix', 'v_w_in', 'v_b_gate', 'v_pool_maps', 'v_pool_scale', 'v_w_pool_proj', 'v_q_latent_norm', 'v_w_uq', 'v_kv_latent_norm', 'v_w_ukv', 'v_w_attn_proj', 'v_w_out', 'v_norm_ffn2', 'v_ffn2_up', 'v_ffn2_down', 'v_final_norm']
TWIN_OUTPUTS = ['loss', 'grad_x', 'grad_norm_ffn1', 'grad_ffn1_up', 'grad_ffn1_down', 'grad_norm_mix', 'grad_w_in', 'grad_b_gate', 'grad_pool_maps', 'grad_pool_scale', 'grad_w_pool_proj', 'grad_q_latent_norm', 'grad_w_uq', 'grad_kv_latent_norm', 'grad_w_ukv', 'grad_w_attn_proj', 'grad_w_out', 'grad_norm_ffn2', 'grad_ffn2_up', 'grad_ffn2_down', 'grad_final_norm', 'delta_norm_ffn1', 'delta_ffn1_up', 'delta_ffn1_down', 'delta_norm_mix', 'delta_w_in', 'delta_b_gate', 'delta_pool_maps', 'delta_pool_scale', 'delta_w_pool_proj', 'delta_q_latent_norm', 'delta_w_uq', 'delta_kv_latent_norm', 'delta_w_ukv', 'delta_w_attn_proj', 'delta_w_out', 'delta_norm_ffn2', 'delta_ffn2_up', 'delta_ffn2_down', 'delta_final_norm', 'new_m_norm_ffn1', 'new_m_ffn1_up', 'new_m_ffn1_down', 'new_m_norm_mix', 'new_m_w_in', 'new_m_b_gate', 'new_m_pool_maps', 'new_m_pool_scale', 'new_m_w_pool_proj', 'new_m_q_latent_norm', 'new_m_w_uq', 'new_m_kv_latent_norm', 'new_m_w_ukv', 'new_m_w_attn_proj', 'new_m_w_out', 'new_m_norm_ffn2', 'new_m_ffn2_up', 'new_m_ffn2_down', 'new_m_final_norm', 'new_v_norm_ffn1', 'new_v_ffn1_up', 'new_v_ffn1_down', 'new_v_norm_mix', 'new_v_w_in', 'new_v_b_gate', 'new_v_pool_maps', 'new_v_pool_scale', 'new_v_w_pool_proj', 'new_v_q_latent_norm', 'new_v_w_uq', 'new_v_kv_latent_norm', 'new_v_w_ukv', 'new_v_w_attn_proj', 'new_v_w_out', 'new_v_norm_ffn2', 'new_v_ffn2_up', 'new_v_ffn2_down', 'new_v_final_norm']
TWIN_LEAF_KINDS = {'loss': 'loss', 'grad_x': 'grad_x', 'grad_norm_ffn1': 'grad_w', 'grad_ffn1_up': 'grad_w', 'grad_ffn1_down': 'grad_w', 'grad_norm_mix': 'grad_w', 'grad_w_in': 'grad_w', 'grad_b_gate': 'grad_w', 'grad_pool_maps': 'grad_w', 'grad_pool_scale': 'grad_w', 'grad_w_pool_proj': 'grad_w', 'grad_q_latent_norm': 'grad_w', 'grad_w_uq': 'grad_w', 'grad_kv_latent_norm': 'grad_w', 'grad_w_ukv': 'grad_w', 'grad_w_attn_proj': 'grad_w', 'grad_w_out': 'grad_w', 'grad_norm_ffn2': 'grad_w', 'grad_ffn2_up': 'grad_w', 'grad_ffn2_down': 'grad_w', 'grad_final_norm': 'grad_w', 'delta_norm_ffn1': 'delta_w', 'delta_ffn1_up': 'delta_w', 'delta_ffn1_down': 'delta_w', 'delta_norm_mix': 'delta_w', 'delta_w_in': 'delta_w', 'delta_b_gate': 'delta_w', 'delta_pool_maps': 'delta_w', 'delta_pool_scale': 'delta_w', 'delta_w_pool_proj': 'delta_w', 'delta_q_latent_norm': 'delta_w', 'delta_w_uq': 'delta_w', 'delta_kv_latent_norm': 'delta_w', 'delta_w_ukv': 'delta_w', 'delta_w_attn_proj': 'delta_w', 'delta_w_out': 'delta_w', 'delta_norm_ffn2': 'delta_w', 'delta_ffn2_up': 'delta_w', 'delta_ffn2_down': 'delta_w', 'delta_final_norm': 'delta_w', 'new_m_norm_ffn1': 'new_m', 'new_m_ffn1_up': 'new_m', 'new_m_ffn1_down': 'new_m', 'new_m_norm_mix': 'new_m', 'new_m_w_in': 'new_m', 'new_m_b_gate': 'new_m', 'new_m_pool_maps': 'new_m', 'new_m_pool_scale': 'new_m', 'new_m_w_pool_proj': 'new_m', 'new_m_q_latent_norm': 'new_m', 'new_m_w_uq': 'new_m', 'new_m_kv_latent_norm': 'new_m', 'new_m_w_ukv': 'new_m', 'new_m_w_attn_proj': 'new_m', 'new_m_w_out': 'new_m', 'new_m_norm_ffn2': 'new_m', 'new_m_ffn2_up': 'new_m', 'new_m_ffn2_down': 'new_m', 'new_m_final_norm': 'new_m', 'new_v_norm_ffn1': 'new_v', 'new_v_ffn1_up': 'new_v', 'new_v_ffn1_down': 'new_v', 'new_v_norm_mix': 'new_v', 'new_v_w_in': 'new_v', 'new_v_b_gate': 'new_v', 'new_v_pool_maps': 'new_v', 'new_v_pool_scale': 'new_v', 'new_v_w_pool_proj': 'new_v', 'new_v_q_latent_norm': 'new_v', 'new_v_w_uq': 'new_v', 'new_v_kv_latent_norm': 'new_v', 'new_v_w_ukv': 'new_v', 'new_v_w_attn_proj': 'new_v', 'new_v_w_out': 'new_v', 'new_v_norm_ffn2': 'new_v', 'new_v_ffn2_up': 'new_v', 'new_v_ffn2_down': 'new_v', 'new_v_final_norm': 'new_v'}


def _forward(args):
    return _fwd_reference(*[args[k] for k in FWD_PARAMS])


def _output_shape():
    out = _jax.eval_shape(lambda: _forward(_fwd_setup_inputs(0)))
    return out.shape, out.dtype

N_MICROBATCH = 1
ADAM_LR = 0.001
ADAM_B1 = 0.9
ADAM_B2 = 0.999
ADAM_EPS = 1e-08
ADAM_WD = 0.01
ADAM_STEP = 10
PER_EXAMPLE_BATCH_AXIS = {'x': 0, 'positions': 0, 'loss_target': 0}
SHARED_INPUTS = []
_WEIGHT_DTYPES = {'norm_ffn1': _jnp.float32, 'ffn1_up': _jnp.float32, 'ffn1_down': _jnp.float32, 'norm_mix': _jnp.float32, 'w_in': _jnp.float32, 'b_gate': _jnp.float32, 'pool_maps': _jnp.float32, 'pool_scale': _jnp.float32, 'w_pool_proj': _jnp.float32, 'q_latent_norm': _jnp.float32, 'w_uq': _jnp.float32, 'kv_latent_norm': _jnp.float32, 'w_ukv': _jnp.float32, 'w_attn_proj': _jnp.float32, 'w_out': _jnp.float32, 'norm_ffn2': _jnp.float32, 'ffn2_up': _jnp.float32, 'ffn2_down': _jnp.float32, 'final_norm': _jnp.float32}
MOMENT_SCALE = {'norm_ffn1': 7.037314e-02, 'ffn1_up': 2.994024e-02, 'ffn1_down': 4.881019e-02, 'norm_mix': 8.103910e-02, 'w_in': 4.528608e-02, 'b_gate': 2.001592e-02, 'pool_maps': 1.018663e-01, 'pool_scale': 9.999991e-02, 'w_pool_proj': 7.193097e-02, 'q_latent_norm': 2.340329e-02, 'w_uq': 1.193378e-02, 'kv_latent_norm': 4.755181e-02, 'w_ukv': 1.618188e-02, 'w_attn_proj': 1.932815e-02, 'w_out': 7.295430e-02, 'norm_ffn2': 6.668886e-02, 'ffn2_up': 2.585191e-02, 'ffn2_down': 4.220306e-02, 'final_norm': 3.203354e+01}


def _to_microbatches(a, axis):
    t = _jnp.moveaxis(a, axis, 0)
    t = t.reshape((N_MICROBATCH, t.shape[0] // N_MICROBATCH) + t.shape[1:])
    return _jnp.moveaxis(t, 1, axis + 1)


def setup_inputs(seed: int = 0) -> dict:
    inp = _fwd_setup_inputs(seed)
    key = _jax.random.fold_in(_jax.random.key(seed), 7919)
    shape, _ = _output_shape()
    out = dict(inp)
    out["loss_target"] = _jax.random.normal(_jax.random.fold_in(key, 0), shape, _jnp.float32)
    for i, name in enumerate(TWIN_WEIGHTS):
        w = inp[name].astype(_jnp.float32)
        if MOMENT_SCALE is None:
            s = _jnp.sqrt(_jnp.mean(_jnp.square(w)) + 1e-30)
        else:
            s = MOMENT_SCALE[name]
        km, kv = _jax.random.split(_jax.random.fold_in(key, i + 1))
        out[name] = w
        out["m_" + name] = s * _jax.random.normal(km, w.shape, _jnp.float32)
        out["v_" + name] = (s * s) * _jax.random.uniform(kv, w.shape, _jnp.float32, 0.5, 1.5)
    if N_MICROBATCH > 1:
        for name, axis in PER_EXAMPLE_BATCH_AXIS.items():
            out[name] = _to_microbatches(out[name], axis)
    return {'x': out['x'], 'positions': out['positions'], 'norm_ffn1': out['norm_ffn1'], 'ffn1_up': out['ffn1_up'], 'ffn1_down': out['ffn1_down'], 'norm_mix': out['norm_mix'], 'w_in': out['w_in'], 'b_gate': out['b_gate'], 'pool_maps': out['pool_maps'], 'pool_scale': out['pool_scale'], 'w_pool_proj': out['w_pool_proj'], 'q_latent_norm': out['q_latent_norm'], 'w_uq': out['w_uq'], 'kv_latent_norm': out['kv_latent_norm'], 'w_ukv': out['w_ukv'], 'w_attn_proj': out['w_attn_proj'], 'w_out': out['w_out'], 'norm_ffn2': out['norm_ffn2'], 'ffn2_up': out['ffn2_up'], 'ffn2_down': out['ffn2_down'], 'final_norm': out['final_norm'], 'loss_target': out['loss_target'], 'm_norm_ffn1': out['m_norm_ffn1'], 'm_ffn1_up': out['m_ffn1_up'], 'm_ffn1_down': out['m_ffn1_down'], 'm_norm_mix': out['m_norm_mix'], 'm_w_in': out['m_w_in'], 'm_b_gate': out['m_b_gate'], 'm_pool_maps': out['m_pool_maps'], 'm_pool_scale': out['m_pool_scale'], 'm_w_pool_proj': out['m_w_pool_proj'], 'm_q_latent_norm': out['m_q_latent_norm'], 'm_w_uq': out['m_w_uq'], 'm_kv_latent_norm': out['m_kv_latent_norm'], 'm_w_ukv': out['m_w_ukv'], 'm_w_attn_proj': out['m_w_attn_proj'], 'm_w_out': out['m_w_out'], 'm_norm_ffn2': out['m_norm_ffn2'], 'm_ffn2_up': out['m_ffn2_up'], 'm_ffn2_down': out['m_ffn2_down'], 'm_final_norm': out['m_final_norm'], 'v_norm_ffn1': out['v_norm_ffn1'], 'v_ffn1_up': out['v_ffn1_up'], 'v_ffn1_down': out['v_ffn1_down'], 'v_norm_mix': out['v_norm_mix'], 'v_w_in': out['v_w_in'], 'v_b_gate': out['v_b_gate'], 'v_pool_maps': out['v_pool_maps'], 'v_pool_scale': out['v_pool_scale'], 'v_w_pool_proj': out['v_w_pool_proj'], 'v_q_latent_norm': out['v_q_latent_norm'], 'v_w_uq': out['v_w_uq'], 'v_kv_latent_norm': out['v_kv_latent_norm'], 'v_w_ukv': out['v_w_ukv'], 'v_w_attn_proj': out['v_w_attn_proj'], 'v_w_out': out['v_w_out'], 'v_norm_ffn2': out['v_norm_ffn2'], 'v_ffn2_up': out['v_ffn2_up'], 'v_ffn2_down': out['v_ffn2_down'], 'v_final_norm': out['v_final_norm']}


def _loss(weights, diff, rest, loss_target):
    with _jax.named_scope("forward"):
        args = {**rest, TWIN_DIFF_INPUT: diff, **{k: w.astype(_WEIGHT_DTYPES[k]) for k, w in weights.items()}}
        y = _forward(args)
    with _jax.named_scope("loss_head"):
        err = _jnp.square(y.astype(_jnp.float32) - loss_target)
        return 0.5 * _jnp.sum(_jnp.mean(err, axis=-1)) if err.ndim else 0.5 * err


def _adamw(w, g, m, v):
    m = ADAM_B1 * m + (1.0 - ADAM_B1) * g
    v = ADAM_B2 * v + (1.0 - ADAM_B2) * _jnp.square(g)
    m_hat = m / (1.0 - ADAM_B1 ** ADAM_STEP)
    v_hat = v / (1.0 - ADAM_B2 ** ADAM_STEP)
    delta = -ADAM_LR * (m_hat / (_jnp.sqrt(v_hat) + ADAM_EPS) + ADAM_WD * w)
    return delta, m, v


def reference(x, positions, norm_ffn1, ffn1_up, ffn1_down, norm_mix, w_in, b_gate, pool_maps, pool_scale, w_pool_proj, q_latent_norm, w_uq, kv_latent_norm, w_ukv, w_attn_proj, w_out, norm_ffn2, ffn2_up, ffn2_down, final_norm, loss_target, m_norm_ffn1, m_ffn1_up, m_ffn1_down, m_norm_mix, m_w_in, m_b_gate, m_pool_maps, m_pool_scale, m_w_pool_proj, m_q_latent_norm, m_w_uq, m_kv_latent_norm, m_w_ukv, m_w_attn_proj, m_w_out, m_norm_ffn2, m_ffn2_up, m_ffn2_down, m_final_norm, v_norm_ffn1, v_ffn1_up, v_ffn1_down, v_norm_mix, v_w_in, v_b_gate, v_pool_maps, v_pool_scale, v_w_pool_proj, v_q_latent_norm, v_w_uq, v_kv_latent_norm, v_w_ukv, v_w_attn_proj, v_w_out, v_norm_ffn2, v_ffn2_up, v_ffn2_down, v_final_norm):
    given = dict(x=x, positions=positions, norm_ffn1=norm_ffn1, ffn1_up=ffn1_up, ffn1_down=ffn1_down, norm_mix=norm_mix, w_in=w_in, b_gate=b_gate, pool_maps=pool_maps, pool_scale=pool_scale, w_pool_proj=w_pool_proj, q_latent_norm=q_latent_norm, w_uq=w_uq, kv_latent_norm=kv_latent_norm, w_ukv=w_ukv, w_attn_proj=w_attn_proj, w_out=w_out, norm_ffn2=norm_ffn2, ffn2_up=ffn2_up, ffn2_down=ffn2_down, final_norm=final_norm, loss_target=loss_target, m_norm_ffn1=m_norm_ffn1, m_ffn1_up=m_ffn1_up, m_ffn1_down=m_ffn1_down, m_norm_mix=m_norm_mix, m_w_in=m_w_in, m_b_gate=m_b_gate, m_pool_maps=m_pool_maps, m_pool_scale=m_pool_scale, m_w_pool_proj=m_w_pool_proj, m_q_latent_norm=m_q_latent_norm, m_w_uq=m_w_uq, m_kv_latent_norm=m_kv_latent_norm, m_w_ukv=m_w_ukv, m_w_attn_proj=m_w_attn_proj, m_w_out=m_w_out, m_norm_ffn2=m_norm_ffn2, m_ffn2_up=m_ffn2_up, m_ffn2_down=m_ffn2_down, m_final_norm=m_final_norm, v_norm_ffn1=v_norm_ffn1, v_ffn1_up=v_ffn1_up, v_ffn1_down=v_ffn1_down, v_norm_mix=v_norm_mix, v_w_in=v_w_in, v_b_gate=v_b_gate, v_pool_maps=v_pool_maps, v_pool_scale=v_pool_scale, v_w_pool_proj=v_w_pool_proj, v_q_latent_norm=v_q_latent_norm, v_w_uq=v_w_uq, v_kv_latent_norm=v_kv_latent_norm, v_w_ukv=v_w_ukv, v_w_attn_proj=v_w_attn_proj, v_w_out=v_w_out, v_norm_ffn2=v_norm_ffn2, v_ffn2_up=v_ffn2_up, v_ffn2_down=v_ffn2_down, v_final_norm=v_final_norm)
    weights = {n: given[n] for n in TWIN_WEIGHTS}
    shared = {n: given[n] for n in SHARED_INPUTS}
    per_example = {n: given[n] for n in ['x', 'positions']}
    grad_fn = _jax.value_and_grad(_loss, argnums=(0, 1))

    def one_microbatch(ex, loss_target):
        ex = dict(ex)
        diff = ex.pop(TWIN_DIFF_INPUT)
        return grad_fn(weights, diff, {**shared, **ex}, loss_target)

    if N_MICROBATCH == 1:
        loss, (grad_w, grad_x) = one_microbatch(per_example, given["loss_target"])
    else:
        def body(carry, xs):
            loss_sum, grad_sum = carry
            l_k, (gw_k, gx_k) = one_microbatch(xs[0], xs[1])
            with _jax.named_scope("update"):
                return (loss_sum + l_k, _jax.tree.map(_jnp.add, grad_sum, gw_k)), gx_k

        init = (_jnp.zeros((), _jnp.float32), _jax.tree.map(_jnp.zeros_like, weights))
        (loss, grad_w), grad_x = _jax.lax.scan(body, init, (per_example, given["loss_target"]))
    with _jax.named_scope("update"):
        delta_w, new_m, new_v = {}, {}, {}
        for n in TWIN_WEIGHTS:
            delta_w[n], new_m[n], new_v[n] = _adamw(weights[n], grad_w[n], given["m_" + n], given["v_" + n])
    return (loss, grad_x, *[grad_w[n] for n in TWIN_WEIGHTS], *[delta_w[n] for n in TWIN_WEIGHTS],
            *[new_m[n] for n in TWIN_WEIGHTS], *[new_v[n] for n in TWIN_WEIGHTS])
```

```python
import functools

import jax
import jax.numpy as jnp
from jax import lax
from jax.experimental import pallas as pl
from jax.experimental.pallas import tpu as pltpu

F32 = jnp.float32
BF16 = jnp.bfloat16

N_HEADS = 8
QK_NOPE_DIM = 128
QK_ROPE_DIM = 64
QK_DIM = QK_NOPE_DIM + QK_ROPE_DIM
V_HEAD_DIM = 128
HEAD_PAD = 256
Q_LORA_RANK = 384
KV_LORA_RANK = 256
ROPE_THETA = 10000.0
POOL_WINDOWS = (2, 4, 8, 16)
N_POOL_GROUPS = 4
POOL_GROUP_DIM = 128
POOL_DIM = N_POOL_GROUPS * POOL_GROUP_DIM
LAT_DIM = 768
NORM_EPS = 1e-6
ADAM_LR = 0.001
ADAM_B1 = 0.9
ADAM_B2 = 0.999
ADAM_EPS = 1e-08
ADAM_WD = 0.01
ADAM_STEP = 10
NEG_INF = -1e30
LANES = 128
ATT_BLOCK = 512
VMEM_LIMIT = 48 * 1024 * 1024
MESH = pl.DeviceIdType.MESH

BIG = ("ffn1_up", "ffn1_down", "w_in", "w_pool_proj", "w_uq", "w_ukv", "w_attn_proj", "w_out",
       "ffn2_up", "ffn2_down")
ROW_SHARDED = ("ffn1_down", "w_attn_proj", "w_out", "ffn2_down")
SMALL = ("norm_ffn1", "norm_mix", "b_gate", "pool_maps", "pool_scale", "q_latent_norm",
         "kv_latent_norm", "norm_ffn2", "final_norm")
WEIGHTS = ("norm_ffn1", "ffn1_up", "ffn1_down", "norm_mix", "w_in", "b_gate", "pool_maps", "pool_scale",
           "w_pool_proj", "q_latent_norm", "w_uq", "kv_latent_norm", "w_ukv", "w_attn_proj", "w_out",
           "norm_ffn2", "ffn2_up", "ffn2_down", "final_norm")


def _pick(dim, cands):
    for c in cands:
        if c <= dim and dim % c == 0:
            return c
    return dim


def _cparams(sem=None, **kw):
    if sem is not None:
        kw["dimension_semantics"] = sem
    return pltpu.CompilerParams(vmem_limit_bytes=VMEM_LIMIT, **kw)


def _sigmoid(x):
    return 1.0 / (1.0 + jnp.exp(-x))


def mm(a, b, *, name, ta=False, tb=False, out_dtype=F32, res=None, alpha=1.0):
    if ta:
        K, M = a.shape
    else:
        M, K = a.shape
    if tb:
        N, K2 = b.shape
    else:
        K2, N = b.shape
    assert K == K2, (a.shape, b.shape, ta, tb)
    tm = _pick(M, (1024, 768, 512, 384, 256, 128))
    tn = _pick(N, (1024, 768, 512, 384, 256, 128))
    tk = _pick(K, (1024, 512, 256, 128))
    nk = K // tk
    dims = (((0 if ta else 1,), (1 if tb else 0,)), ((), ()))

    def body(*refs):
        if res is not None:
            a_ref, b_ref, r_ref, o_ref, acc = refs
        else:
            a_ref, b_ref, o_ref, acc = refs
        k = pl.program_id(2)

        @pl.when(k == 0)
        def _():
            acc[...] = jnp.zeros_like(acc)

        acc[...] += lax.dot_general(a_ref[...].astype(BF16), b_ref[...].astype(BF16), dims,
                                    preferred_element_type=F32)

        @pl.when(k == nk - 1)
        def _():
            r = acc[...]
            if alpha != 1.0:
                r = r * alpha
            if res is not None:
                r = r_ref[...].astype(F32) + r
            o_ref[...] = r.astype(out_dtype)

    a_spec = pl.BlockSpec((tk, tm), lambda i, j, k: (k, i)) if ta else pl.BlockSpec((tm, tk), lambda i, j, k: (i, k))
    b_spec = pl.BlockSpec((tn, tk), lambda i, j, k: (j, k)) if tb else pl.BlockSpec((tk, tn), lambda i, j, k: (k, j))
    o_spec = pl.BlockSpec((tm, tn), lambda i, j, k: (i, j))
    in_specs = [a_spec, b_spec]
    args = [a, b]
    if res is not None:
        in_specs.append(o_spec)
        args.append(res)
    return pl.pallas_call(
        body, name=name, grid=(M // tm, N // tn, nk), in_specs=in_specs, out_specs=o_spec,
        out_shape=jax.ShapeDtypeStruct((M, N), out_dtype),
        scratch_shapes=[pltpu.VMEM((tm, tn), F32)],
        compiler_params=_cparams(("parallel", "parallel", "arbitrary")),
    )(*args)


def _rows(T, width_bytes):
    cap = max(8, (2 * 1024 * 1024) // width_bytes)
    return _pick(T, tuple(c for c in (1024, 512, 256, 128, 64, 32, 16) if c <= cap))


def rms_fwd(x, g, *, name):
    T, D = x.shape
    tm = _rows(T, D * 4)

    def body(x_ref, g_ref, h_ref):
        xv = x_ref[...]
        r = lax.rsqrt(jnp.mean(xv * xv, axis=-1, keepdims=True) + NORM_EPS)
        h_ref[...] = (xv * r * g_ref[...]).astype(BF16)

    return pl.pallas_call(
        body, name=name, grid=(T // tm,),
        in_specs=[pl.BlockSpec((tm, D), lambda i: (i, 0)), pl.BlockSpec((1, D), lambda i: (0, 0))],
        out_specs=pl.BlockSpec((tm, D), lambda i: (i, 0)),
        out_shape=jax.ShapeDtypeStruct((T, D), BF16),
        compiler_params=_cparams(("parallel",)),
    )(x, g.reshape(1, D))


def _rms_bwd_math(xv, gv, dh):
    r = lax.rsqrt(jnp.mean(xv * xv, axis=-1, keepdims=True) + NORM_EPS)
    xn = xv * r
    dg = jnp.sum(dh * xn, axis=0, keepdims=True)
    dxn = dh * gv
    dx = r * (dxn - xn * jnp.mean(dxn * xn, axis=-1, keepdims=True))
    return dx, dg


def rms_bwd(x, g, dh, dres, *, name):
    T, D = x.shape
    tm = _rows(T, D * 4)

    def body(x_ref, g_ref, dh_ref, dres_ref, dx_ref, dg_ref):
        @pl.when(pl.program_id(0) == 0)
        def _():
            dg_ref[...] = jnp.zeros_like(dg_ref)

        dx, dg = _rms_bwd_math(x_ref[...], g_ref[...], dh_ref[...].astype(F32))
        dx_ref[...] = dres_ref[...] + dx
        dg_ref[...] += dg

    row = pl.BlockSpec((tm, D), lambda i: (i, 0))
    vec = pl.BlockSpec((1, D), lambda i: (0, 0))
    return pl.pallas_call(
        body, name=name, grid=(T // tm,), in_specs=[row, vec, row, row], out_specs=[row, vec],
        out_shape=[jax.ShapeDtypeStruct((T, D), F32), jax.ShapeDtypeStruct((1, D), F32)],
        compiler_params=_cparams(("arbitrary",)),
    )(x, g.reshape(1, D), dh, dres)


def swiglu_fwd(u, *, name):
    T, F2 = u.shape
    Fh = F2 // 2
    tm = _rows(T, Fh * 4)

    def body(g_ref, u_ref, a_ref):
        gv = g_ref[...].astype(F32)
        a_ref[...] = (gv * _sigmoid(gv) * u_ref[...].astype(F32)).astype(BF16)

    return pl.pallas_call(
        body, name=name, grid=(T // tm,),
        in_specs=[pl.BlockSpec((tm, Fh), lambda i: (i, 0)), pl.BlockSpec((tm, Fh), lambda i: (i, 1))],
        out_specs=pl.BlockSpec((tm, Fh), lambda i: (i, 0)),
        out_shape=jax.ShapeDtypeStruct((T, Fh), BF16),
        compiler_params=_cparams(("parallel",)),
    )(u, u)


def swiglu_bwd(u, da, *, name):
    T, F2 = u.shape
    Fh = F2 // 2
    tm = _rows(T, F2 * 4)

    def body(g_ref, u_ref, da_ref, du_ref):
        gv = g_ref[...].astype(F32)
        uv = u_ref[...].astype(F32)
        dav = da_ref[...].astype(F32)
        s = _sigmoid(gv)
        du_ref[:, :Fh] = (dav * uv * (s * (1.0 + gv * (1.0 - s)))).astype(BF16)
        du_ref[:, Fh:] = (dav * (gv * s)).astype(BF16)

    return pl.pallas_call(
        body, name=name, grid=(T // tm,),
        in_specs=[pl.BlockSpec((tm, Fh), lambda i: (i, 0)), pl.BlockSpec((tm, Fh), lambda i: (i, 1)),
                  pl.BlockSpec((tm, Fh), lambda i: (i, 0))],
        out_specs=pl.BlockSpec((tm, F2), lambda i: (i, 0)),
        out_shape=jax.ShapeDtypeStruct((T, F2), BF16),
        compiler_params=_cparams(("parallel",)),
    )(u, u, da)


def _rope(xv, cv, sv):
    half = QK_ROPE_DIM // 2
    lane = lax.broadcasted_iota(jnp.int32, xv.shape, 1)
    swapped = jnp.where(lane < half, pltpu.roll(xv, LANES - half, 1), pltpu.roll(xv, half, 1))
    return xv * cv + swapped * sv


def _rope_t(dy, cv, sv):
    half = QK_ROPE_DIM // 2
    ds = dy * sv
    lane = lax.broadcasted_iota(jnp.int32, dy.shape, 1)
    swapped = jnp.where(lane < half, pltpu.roll(ds, LANES - half, 1), pltpu.roll(ds, half, 1))
    return dy * cv + swapped


def lat_fwd(lat, qn_w, kvn_w, cs, sn, *, name):
    T = lat.shape[0]
    tm = _rows(T, LAT_DIM * 4)
    kv0 = Q_LORA_RANK
    kr0 = Q_LORA_RANK + KV_LORA_RANK

    def body(lat_ref, qw_ref, kw_ref, c_ref, s_ref, qn_ref, kvn_ref, kr_ref):
        ql = lat_ref[:, :kv0]
        r = lax.rsqrt(jnp.mean(ql * ql, axis=-1, keepdims=True) + NORM_EPS)
        qn_ref[...] = (ql * r * qw_ref[...]).astype(BF16)
        kl = lat_ref[:, kv0:kr0]
        r = lax.rsqrt(jnp.mean(kl * kl, axis=-1, keepdims=True) + NORM_EPS)
        kvn_ref[...] = (kl * r * kw_ref[...]).astype(BF16)
        kr_ref[...] = _rope(lat_ref[:, kr0:], c_ref[...], s_ref[...]).astype(BF16)

    return pl.pallas_call(
        body, name=name, grid=(T // tm,),
        in_specs=[pl.BlockSpec((tm, LAT_DIM), lambda i: (i, 0)),
                  pl.BlockSpec((1, Q_LORA_RANK), lambda i: (0, 0)),
                  pl.BlockSpec((1, KV_LORA_RANK), lambda i: (0, 0)),
                  pl.BlockSpec((tm, LANES), lambda i: (i, 0)), pl.BlockSpec((tm, LANES), lambda i: (i, 0))],
        out_specs=[pl.BlockSpec((tm, Q_LORA_RANK), lambda i: (i, 0)),
                   pl.BlockSpec((tm, KV_LORA_RANK), lambda i: (i, 0)),
                   pl.BlockSpec((tm, LANES), lambda i: (i, 0))],
        out_shape=[jax.ShapeDtypeStruct((T, Q_LORA_RANK), BF16), jax.ShapeDtypeStruct((T, KV_LORA_RANK), BF16),
                   jax.ShapeDtypeStruct((T, LANES), BF16)],
        compiler_params=_cparams(("parallel",)),
    )(lat, qn_w.reshape(1, -1), kvn_w.reshape(1, -1), cs, sn)


def lat_bwd(lat, qn_w, kvn_w, dqn, dkvn, dkr, cs, sn, *, name):
    T = lat.shape[0]
    tm = _rows(T, LAT_DIM * 4)
    kv0 = Q_LORA_RANK
    kr0 = Q_LORA_RANK + KV_LORA_RANK

    def body(lat_ref, qw_ref, kw_ref, dqn_ref, dkvn_ref, dkr_ref, c_ref, s_ref, dlat_ref, dqw_ref, dkw_ref):
        @pl.when(pl.program_id(0) == 0)
        def _():
            dqw_ref[...] = jnp.zeros_like(dqw_ref)
            dkw_ref[...] = jnp.zeros_like(dkw_ref)

        dx, dg = _rms_bwd_math(lat_ref[:, :kv0], qw_ref[...], dqn_ref[...])
        dlat_ref[:, :kv0] = dx.astype(BF16)
        dqw_ref[...] += dg
        dx, dg = _rms_bwd_math(lat_ref[:, kv0:kr0], kw_ref[...], dkvn_ref[...])
        dlat_ref[:, kv0:kr0] = dx.astype(BF16)
        dkw_ref[...] += dg
        dlat_ref[:, kr0:] = _rope_t(dkr_ref[...], c_ref[...], s_ref[...]).astype(BF16)

    row = lambda w: pl.BlockSpec((tm, w), lambda i: (i, 0))
    vec = lambda w: pl.BlockSpec((1, w), lambda i: (0, 0))
    return pl.pallas_call(
        body, name=name, grid=(T // tm,),
        in_specs=[row(LAT_DIM), vec(Q_LORA_RANK), vec(KV_LORA_RANK), row(Q_LORA_RANK), row(KV_LORA_RANK),
                  row(LANES), row(LANES), row(LANES)],
        out_specs=[row(LAT_DIM), vec(Q_LORA_RANK), vec(KV_LORA_RANK)],
        out_shape=[jax.ShapeDtypeStruct((T, LAT_DIM), BF16), jax.ShapeDtypeStruct((1, Q_LORA_RANK), F32),
                   jax.ShapeDtypeStruct((1, KV_LORA_RANK), F32)],
        compiler_params=_cparams(("arbitrary",)),
    )(lat, qn_w.reshape(1, -1), kvn_w.reshape(1, -1), dqn, dkvn, dkr, cs, sn)


def q_rope(q, cs, sn, *, transpose, name):
    T, W = q.shape
    tm = _rows(T, W * 4)
    fn = _rope_t if transpose else _rope

    def body(q_ref, c_ref, s_ref, o_ref):
        cv = c_ref[...]
        sv = s_ref[...]
        for h in range(N_HEADS):
            lo = h * HEAD_PAD
            o_ref[:, lo:lo + QK_NOPE_DIM] = q_ref[:, lo:lo + QK_NOPE_DIM].astype(BF16)
            o_ref[:, lo + QK_NOPE_DIM:lo + HEAD_PAD] = fn(q_ref[:, lo + QK_NOPE_DIM:lo + HEAD_PAD], cv, sv).astype(BF16)

    return pl.pallas_call(
        body, name=name, grid=(T // tm,),
        in_specs=[pl.BlockSpec((tm, W), lambda i: (i, 0)), pl.BlockSpec((tm, LANES), lambda i: (i, 0)),
                  pl.BlockSpec((tm, LANES), lambda i: (i, 0))],
        out_specs=pl.BlockSpec((tm, W), lambda i: (i, 0)),
        out_shape=jax.ShapeDtypeStruct((T, W), BF16),
        compiler_params=_cparams(("parallel",)),
    )(q, cs, sn)


def gate_fwd(gl, bg, ba, bb, *, name):
    T, D2 = gl.shape
    D = D2 // 2
    tm = _rows(T, D2 * 4)

    def body(gl_ref, bg_ref, ba_ref, bb_ref, m_ref):
        ga = _sigmoid(gl_ref[:, :D] + bg_ref[:, :D])
        gb = _sigmoid(gl_ref[:, D:] + bg_ref[:, D:])
        m_ref[...] = (ga * ba_ref[...] + gb * bb_ref[...]).astype(BF16)

    row = lambda w: pl.BlockSpec((tm, w), lambda i: (i, 0))
    return pl.pallas_call(
        body, name=name, grid=(T // tm,),
        in_specs=[row(D2), pl.BlockSpec((1, D2), lambda i: (0, 0)), row(D), row(D)],
        out_specs=row(D), out_shape=jax.ShapeDtypeStruct((T, D), BF16),
        compiler_params=_cparams(("parallel",)),
    )(gl, bg.reshape(1, D2), ba, bb)


def gate_bwd(dm, gl, bg, ba, bb, *, name):
    T, D2 = gl.shape
    D = D2 // 2
    tm = _rows(T, D2 * 4)

    def body(dm_ref, gl_ref, bg_ref, ba_ref, bb_ref, dba_ref, dbb_ref, dgl_ref, dbg_ref):
        @pl.when(pl.program_id(0) == 0)
        def _():
            dbg_ref[...] = jnp.zeros_like(dbg_ref)

        dmv = dm_ref[...]
        ga = _sigmoid(gl_ref[:, :D] + bg_ref[:, :D])
        gb = _sigmoid(gl_ref[:, D:] + bg_ref[:, D:])
        dba_ref[...] = (dmv * ga).astype(BF16)
        dbb_ref[...] = (dmv * gb).astype(BF16)
        dla = dmv * ba_ref[...] * ga * (1.0 - ga)
        dlb = dmv * bb_ref[...] * gb * (1.0 - gb)
        dgl_ref[:, :D] = dla.astype(BF16)
        dgl_ref[:, D:] = dlb.astype(BF16)
        dbg_ref[:, :D] += jnp.sum(dla, axis=0, keepdims=True)
        dbg_ref[:, D:] += jnp.sum(dlb, axis=0, keepdims=True)

    row = lambda w: pl.BlockSpec((tm, w), lambda i: (i, 0))
    vec = pl.BlockSpec((1, D2), lambda i: (0, 0))
    return pl.pallas_call(
        body, name=name, grid=(T // tm,),
        in_specs=[row(D), row(D2), vec, row(D), row(D)],
        out_specs=[row(D), row(D), row(D2), vec],
        out_shape=[jax.ShapeDtypeStruct((T, D), BF16), jax.ShapeDtypeStruct((T, D), BF16),
                   jax.ShapeDtypeStruct((T, D2), BF16), jax.ShapeDtypeStruct((1, D2), F32)],
        compiler_params=_cparams(("arbitrary",)),
    )(dm, gl, bg.reshape(1, D2), ba, bb)


def loss_head(x, gf, tgt, *, name):
    T, D = x.shape
    tm = _rows(T, D * 4)

    def body(x_ref, g_ref, t_ref, dx_ref, dg_ref, loss_ref):
        @pl.when(pl.program_id(0) == 0)
        def _():
            dg_ref[...] = jnp.zeros_like(dg_ref)
            loss_ref[...] = jnp.zeros_like(loss_ref)

        xv = x_ref[...]
        gv = g_ref[...]
        r = lax.rsqrt(jnp.mean(xv * xv, axis=-1, keepdims=True) + NORM_EPS)
        xn = xv * r
        err = xn * gv - t_ref[...]
        loss_ref[...] += 0.5 * jnp.sum(jnp.mean(err * err, axis=-1, keepdims=True))
        dy = err * (1.0 / D)
        dg_ref[...] += jnp.sum(dy * xn, axis=0, keepdims=True)
        dxn = dy * gv
        dx_ref[...] = r * (dxn - xn * jnp.mean(dxn * xn, axis=-1, keepdims=True))

    row = pl.BlockSpec((tm, D), lambda i: (i, 0))
    vec = pl.BlockSpec((1, D), lambda i: (0, 0))
    return pl.pallas_call(
        body, name=name, grid=(T // tm,), in_specs=[row, vec, row],
        out_specs=[row, vec, pl.BlockSpec((8, LANES), lambda i: (0, 0))],
        out_shape=[jax.ShapeDtypeStruct((T, D), F32), jax.ShapeDtypeStruct((1, D), F32),
                   jax.ShapeDtypeStruct((8, LANES), F32)],
        compiler_params=_cparams(("arbitrary",)),
    )(x, gf.reshape(1, D), tgt)


def _shift_rows(s, k, down):
    n = s.shape[0]
    t = lax.broadcasted_iota(jnp.int32, s.shape, 0)
    if down:
        return jnp.where(t >= k, pltpu.roll(s, k, 0), 0.0)
    return jnp.where(t < n - k, pltpu.roll(s, n - k, 0), 0.0)


def _window_sum(s, w, down):
    k = 1
    while k < w:
        s = s + _shift_rows(s, k, down)
        k *= 2
    return s


def _pool_count(shape, w):
    t = lax.broadcasted_iota(jnp.int32, shape, 0)
    return jnp.minimum(t + 1, w).astype(F32)


def pool_fwd(xp, maps, scale, B, *, name):
    T, P = xp.shape
    S = T // B
    G = POOL_GROUP_DIM

    def body(x_ref, m_ref, sc_ref, o_ref):
        for g, w in enumerate(POOL_WINDOWS):
            xg = x_ref[:, g * G:(g + 1) * G]
            pooled = _window_sum(xg, w, True) / _pool_count(xg.shape, w) - xg
            mixed = jnp.dot(pooled.astype(BF16), m_ref[g], preferred_element_type=F32)
            o_ref[:, g * G:(g + 1) * G] = (mixed * sc_ref[:, g * G:(g + 1) * G]).astype(BF16)

    return pl.pallas_call(
        body, name=name, grid=(B,),
        in_specs=[pl.BlockSpec((S, P), lambda b: (b, 0)), pl.BlockSpec((N_POOL_GROUPS, G, G), lambda b: (0, 0, 0)),
                  pl.BlockSpec((1, P), lambda b: (0, 0))],
        out_specs=pl.BlockSpec((S, P), lambda b: (b, 0)),
        out_shape=jax.ShapeDtypeStruct((T, P), BF16),
        compiler_params=_cparams(("parallel",)),
    )(xp, maps, scale.reshape(1, P))


def pool_bwd(xp, dmixed, maps, scale, B, *, name):
    T, P = xp.shape
    S = T // B
    G = POOL_GROUP_DIM

    def body(x_ref, dm_ref, m_ref, sc_ref, dx_ref, dmaps_ref, dsc_ref):
        @pl.when(pl.program_id(0) == 0)
        def _():
            dmaps_ref[...] = jnp.zeros_like(dmaps_ref)
            dsc_ref[...] = jnp.zeros_like(dsc_ref)

        for g, w in enumerate(POOL_WINDOWS):
            cols = slice(g * G, (g + 1) * G)
            xg = x_ref[:, cols]
            cnt = _pool_count(xg.shape, w)
            pooled = (_window_sum(xg, w, True) / cnt - xg).astype(BF16)
            mixed = jnp.dot(pooled, m_ref[g], preferred_element_type=F32)
            dmx = dm_ref[:, cols]
            dsc_ref[:, cols] += jnp.sum(dmx * mixed, axis=0, keepdims=True)
            dmp = (dmx * sc_ref[:, cols]).astype(BF16)
            dmaps_ref[g] += lax.dot_general(pooled, dmp, (((0,), (0,)), ((), ())), preferred_element_type=F32)
            dpooled = lax.dot_general(dmp, m_ref[g], (((1,), (1,)), ((), ())), preferred_element_type=F32)
            dx_ref[:, cols] = (_window_sum(dpooled / cnt, w, False) - dpooled).astype(BF16)

    blk = pl.BlockSpec((S, P), lambda b: (b, 0))
    mp = pl.BlockSpec((N_POOL_GROUPS, G, G), lambda b: (0, 0, 0))
    vec = pl.BlockSpec((1, P), lambda b: (0, 0))
    return pl.pallas_call(
        body, name=name, grid=(B,), in_specs=[blk, blk, mp, vec], out_specs=[blk, mp, vec],
        out_shape=[jax.ShapeDtypeStruct((T, P), BF16), jax.ShapeDtypeStruct((N_POOL_GROUPS, G, G), F32),
                   jax.ShapeDtypeStruct((1, P), F32)],
        compiler_params=_cparams(("arbitrary",)),
    )(xp, dmixed, maps, scale.reshape(1, P))


def _scores(q_ref, kv_ref, kr_ref, qi, ki, blk):
    k = jnp.concatenate([kv_ref[:, :QK_NOPE_DIM], kr_ref[...]], axis=1)
    s = lax.dot_general(q_ref[...], k, (((1,), (1,)), ((), ())), preferred_element_type=F32) * (QK_DIM ** -0.5)
    row = qi * blk + lax.broadcasted_iota(jnp.int32, s.shape, 0)
    col = ki * blk + lax.broadcasted_iota(jnp.int32, s.shape, 1)
    return jnp.where(row >= col, s, NEG_INF), k


def attn_fwd(q, kv, kr, B, *, name):
    T = q.shape[0]
    S = T // B
    blk = min(ATT_BLOCK, S)
    nb = S // blk
    H = N_HEADS

    def body(q_ref, kv_ref, kr_ref, o_ref, lse_ref, m_s, l_s, acc_s):
        qi = pl.program_id(2)
        ki = pl.program_id(3)

        @pl.when(ki == 0)
        def _():
            m_s[...] = jnp.full_like(m_s, NEG_INF)
            l_s[...] = jnp.zeros_like(l_s)
            acc_s[...] = jnp.zeros_like(acc_s)

        @pl.when(ki <= qi)
        def _():
            s, _ = _scores(q_ref, kv_ref, kr_ref, qi, ki, blk)
            m_prev = m_s[...]
            m_new = jnp.maximum(m_prev, jnp.max(s, axis=-1, keepdims=True))
            a = jnp.exp(m_prev - m_new)
            p = jnp.exp(s - m_new)
            l_s[...] = a * l_s[...] + jnp.sum(p, axis=-1, keepdims=True)
            acc_s[...] = a * acc_s[...] + jnp.dot(p.astype(BF16), kv_ref[:, QK_NOPE_DIM:], preferred_element_type=F32)
            m_s[...] = m_new

        @pl.when(ki == qi)
        def _():
            o_ref[...] = (acc_s[...] / l_s[...]).astype(BF16)
            lse_ref[0] = m_s[...] + jnp.log(l_s[...])

    return pl.pallas_call(
        body, name=name, grid=(B, H, nb, nb),
        in_specs=[pl.BlockSpec((blk, HEAD_PAD), lambda b, h, qi, ki: (b * nb + qi, h)),
                  pl.BlockSpec((blk, HEAD_PAD), lambda b, h, qi, ki: (b * nb + jnp.minimum(ki, qi), h)),
                  pl.BlockSpec((blk, LANES), lambda b, h, qi, ki: (b * nb + jnp.minimum(ki, qi), 0))],
        out_specs=[pl.BlockSpec((blk, V_HEAD_DIM), lambda b, h, qi, ki: (b * nb + qi, h)),
                   pl.BlockSpec((1, blk, 1), lambda b, h, qi, ki: (h, b * nb + qi, 0))],
        out_shape=[jax.ShapeDtypeStruct((T, H * V_HEAD_DIM), BF16), jax.ShapeDtypeStruct((H, T, 1), F32)],
        scratch_shapes=[pltpu.VMEM((blk, 1), F32), pltpu.VMEM((blk, 1), F32), pltpu.VMEM((blk, V_HEAD_DIM), F32)],
        compiler_params=_cparams(("parallel", "parallel", "parallel", "arbitrary")),
    )(q, kv, kr)


def attn_bwd_dq(q, kv, kr, o, do, lse, B, *, name):
    T = q.shape[0]
    S = T // B
    blk = min(ATT_BLOCK, S)
    nb = S // blk
    H = N_HEADS

    def body(q_ref, kv_ref, kr_ref, o_ref, do_ref, lse_ref, dq_ref, dl_ref, acc_s, dl_s):
        qi = pl.program_id(2)
        ki = pl.program_id(3)

        @pl.when(ki == 0)
        def _():
            acc_s[...] = jnp.zeros_like(acc_s)
            dl_s[...] = jnp.sum(do_ref[...].astype(F32) * o_ref[...].astype(F32), axis=-1, keepdims=True)

        @pl.when(ki <= qi)
        def _():
            s, k = _scores(q_ref, kv_ref, kr_ref, qi, ki, blk)
            p = jnp.exp(s - lse_ref[0])
            dp = lax.dot_general(do_ref[...], kv_ref[:, QK_NOPE_DIM:], (((1,), (1,)), ((), ())),
                                 preferred_element_type=F32)
            ds = (p * (dp - dl_s[...]) * (QK_DIM ** -0.5)).astype(BF16)
            acc_s[...] += jnp.dot(ds, k, preferred_element_type=F32)

        @pl.when(ki == qi)
        def _():
            dq_ref[...] = acc_s[...]
            dl_ref[0] = dl_s[...]

    qrow = lambda b, h, qi, ki: (b * nb + qi, h)
    krow = lambda b, h, qi, ki: (b * nb + jnp.minimum(ki, qi), h)
    stat = pl.BlockSpec((1, blk, 1), lambda b, h, qi, ki: (h, b * nb + qi, 0))
    return pl.pallas_call(
        body, name=name, grid=(B, H, nb, nb),
        in_specs=[pl.BlockSpec((blk, HEAD_PAD), qrow), pl.BlockSpec((blk, HEAD_PAD), krow),
                  pl.BlockSpec((blk, LANES), lambda b, h, qi, ki: (b * nb + jnp.minimum(ki, qi), 0)),
                  pl.BlockSpec((blk, V_HEAD_DIM), qrow), pl.BlockSpec((blk, V_HEAD_DIM), qrow), stat],
        out_specs=[pl.BlockSpec((blk, HEAD_PAD), qrow), stat],
        out_shape=[jax.ShapeDtypeStruct((T, H * HEAD_PAD), F32), jax.ShapeDtypeStruct((H, T, 1), F32)],
        scratch_shapes=[pltpu.VMEM((blk, HEAD_PAD), F32), pltpu.VMEM((blk, 1), F32)],
        compiler_params=_cparams(("parallel", "parallel", "parallel", "arbitrary")),
    )(q, kv, kr, o, do, lse)


def attn_bwd_dkv(q, kv, kr, do, lse, delta, B, *, name):
    T = q.shape[0]
    S = T // B
    blk = min(ATT_BLOCK, S)
    nb = S // blk
    H = N_HEADS

    def body(q_ref, kv_ref, kr_ref, do_ref, lse_ref, dl_ref, dkv_ref, dkr_ref, dk_s, dv_s):
        ki = pl.program_id(1)
        h = pl.program_id(2)
        qi = pl.program_id(3)

        @pl.when(qi == 0)
        def _():
            dk_s[...] = jnp.zeros_like(dk_s)
            dv_s[...] = jnp.zeros_like(dv_s)

        @pl.when(jnp.logical_and(qi == 0, h == 0))
        def _():
            dkr_ref[...] = jnp.zeros_like(dkr_ref)

        @pl.when(qi >= ki)
        def _():
            s, _ = _scores(q_ref, kv_ref, kr_ref, qi, ki, blk)
            p = jnp.exp(s - lse_ref[0])
            dov = do_ref[...]
            dv_s[...] += lax.dot_general(p.astype(BF16), dov, (((0,), (0,)), ((), ())), preferred_element_type=F32)
            dp = lax.dot_general(dov, kv_ref[:, QK_NOPE_DIM:], (((1,), (1,)), ((), ())), preferred_element_type=F32)
            ds = (p * (dp - dl_ref[0]) * (QK_DIM ** -0.5)).astype(BF16)
            dk_s[...] += lax.dot_general(ds, q_ref[...], (((0,), (0,)), ((), ())), preferred_element_type=F32)

        @pl.when(qi == nb - 1)
        def _():
            dkv_ref[:, :QK_NOPE_DIM] = dk_s[:, :QK_NOPE_DIM].astype(BF16)
            dkv_ref[:, QK_NOPE_DIM:] = dv_s[...].astype(BF16)
            dkr_ref[...] += dk_s[:, QK_NOPE_DIM:]

    qrow = lambda b, ki, h, qi: (b * nb + jnp.maximum(qi, ki), h)
    krow = lambda b, ki, h, qi: (b * nb + ki, h)
    stat = pl.BlockSpec((1, blk, 1), lambda b, ki, h, qi: (h, b * nb + jnp.maximum(qi, ki), 0))
    krs = pl.BlockSpec((blk, LANES), lambda b, ki, h, qi: (b * nb + ki, 0))
    return pl.pallas_call(
        body, name=name, grid=(B, nb, H, nb),
        in_specs=[pl.BlockSpec((blk, HEAD_PAD), qrow), pl.BlockSpec((blk, HEAD_PAD), krow), krs,
                  pl.BlockSpec((blk, V_HEAD_DIM), qrow), stat, stat],
        out_specs=[pl.BlockSpec((blk, HEAD_PAD), krow), krs],
        out_shape=[jax.ShapeDtypeStruct((T, H * HEAD_PAD), BF16), jax.ShapeDtypeStruct((T, LANES), F32)],
        scratch_shapes=[pltpu.VMEM((blk, HEAD_PAD), F32), pltpu.VMEM((blk, V_HEAD_DIM), F32)],
        compiler_params=_cparams(("parallel", "parallel", "arbitrary", "arbitrary")),
    )(q, kv, kr, do, lse, delta)


def adamw(w, g, m, v, *, name):
    R, C = w.shape
    cap = max(8, (1024 * 1024) // (C * 4))
    tr = _pick(R, tuple(c for c in (1024, 512, 256, 128, 64, 32, 16, 8) if c <= cap))
    c1 = 1.0 - ADAM_B1 ** ADAM_STEP
    c2 = 1.0 - ADAM_B2 ** ADAM_STEP

    def body(w_ref, g_ref, m_ref, v_ref, d_ref, nm_ref, nv_ref):
        gv = g_ref[...]
        mv = ADAM_B1 * m_ref[...] + (1.0 - ADAM_B1) * gv
        vv = ADAM_B2 * v_ref[...] + (1.0 - ADAM_B2) * (gv * gv)
        nm_ref[...] = mv
        nv_ref[...] = vv
        d_ref[...] = -ADAM_LR * ((mv / c1) / (jnp.sqrt(vv / c2) + ADAM_EPS) + ADAM_WD * w_ref[...])

    blk = pl.BlockSpec((tr, C), lambda i: (i, 0))
    sh = jax.ShapeDtypeStruct((R, C), F32)
    return pl.pallas_call(
        body, name=name, grid=(R // tr,), in_specs=[blk] * 4, out_specs=[blk] * 3, out_shape=[sh] * 3,
        compiler_params=_cparams(("parallel",)),
    )(w, g, m, v)


ANY = pl.BlockSpec(memory_space=pl.ANY)


def _place():
    x, y, c = lax.axis_index("x"), lax.axis_index("y"), lax.axis_index("c")
    others = [(1 - x, y), (x, 1 - y), (1 - x, 1 - y)]
    return x, y, c, others


def _remote(src, dst, ssem, rsem, dev):
    return pltpu.make_async_remote_copy(src_ref=src, dst_ref=dst, send_sem=ssem, recv_sem=rsem,
                                        device_id=dev, device_id_type=MESH)


def _half(ref_rows, c):
    hr = ref_rows // 2
    return pl.ds(pl.multiple_of(c * hr, 16), hr)


def all_gather_weights(shards):
    n = len(shards)

    def body(*refs):
        ins, outs = refs[:n], refs[n:2 * n]
        s_ici, r_ici, s_fwd, r_fwd, s_loc = refs[2 * n:]
        x, y, c, others = _place()
        q = 2 * x + y
        sib = (x, y, 1 - c)
        local, sent = [], []
        for i in range(n):
            loc = pltpu.make_async_copy(ins[i], outs[i].at[q], s_loc.at[i])
            loc.start()
            local.append(loc)
            mine = _half(ins[i].shape[1], c)
            for j, (ox, oy) in enumerate(others):
                cp = _remote(ins[i].at[:, mine, :], outs[i].at[q, :, mine, :], s_ici.at[3 * i + j], r_ici.at[3 * i + j],
                             (ox, oy, c))
                cp.start()
                sent.append(cp)
        for i in range(n):
            mine = _half(ins[i].shape[1], c)
            for j, (ox, oy) in enumerate(others):
                blk = outs[i].at[2 * ox + oy, :, mine, :]
                _remote(blk, blk, s_ici.at[3 * i + j], r_ici.at[3 * i + j], sib).wait_recv()
                fw = _remote(blk, blk, s_fwd.at[3 * i + j], r_fwd.at[3 * i + j], sib)
                fw.start()
                sent.append(fw)
        for i in range(n):
            theirs = _half(ins[i].shape[1], 1 - c)
            for j, (ox, oy) in enumerate(others):
                blk = outs[i].at[2 * ox + oy, :, theirs, :]
                _remote(blk, blk, s_fwd.at[3 * i + j], r_fwd.at[3 * i + j], sib).wait_recv()
        for cp in sent:
            cp.wait_send()
        for loc in local:
            loc.wait()

    return pl.pallas_call(
        body, name="all_gather_weights", in_specs=[ANY] * n, out_specs=[ANY] * n,
        out_shape=[jax.ShapeDtypeStruct((4,) + s.shape, s.dtype) for s in shards],
        scratch_shapes=[pltpu.SemaphoreType.DMA((3 * n,)), pltpu.SemaphoreType.DMA((3 * n,)),
                        pltpu.SemaphoreType.DMA((3 * n,)), pltpu.SemaphoreType.DMA((3 * n,)),
                        pltpu.SemaphoreType.DMA((n,))],
        compiler_params=pltpu.CompilerParams(has_side_effects=True),
    )(*shards)


def rs_swap_halves(grads):
    n = len(grads)

    def body(*refs):
        ins, outs = refs[:n], refs[n:2 * n]
        ssem, rsem = refs[2 * n:]
        x, y, c, _ = _place()
        sib = (x, y, 1 - c)
        cps = []
        for i in range(n):
            theirs = _half(ins[i].shape[2], 1 - c)
            cp = _remote(ins[i].at[:, :, theirs, :], outs[i], ssem.at[i], rsem.at[i], sib)
            cp.start()
            cps.append(cp)
        for cp in cps:
            cp.wait()

    return pl.pallas_call(
        body, name="rs_swap_halves", in_specs=[ANY] * n, out_specs=[ANY] * n,
        out_shape=[jax.ShapeDtypeStruct((4, g.shape[1], g.shape[2] // 2, g.shape[3]), g.dtype) for g in grads],
        scratch_shapes=[pltpu.SemaphoreType.DMA((n,)), pltpu.SemaphoreType.DMA((n,))],
        compiler_params=pltpu.CompilerParams(has_side_effects=True),
    )(*grads)


def rs_chip_sum(g, r1, core, *, name):
    _, L, r, cdim = g.shape
    hr = r // 2

    def body(c_ref, g_ref, r1_ref, o_ref):
        o_ref[...] = (g_ref[...].astype(F32) + r1_ref[...].astype(F32)).astype(BF16)

    return pl.pallas_call(
        body, name=name,
        grid_spec=pltpu.PrefetchScalarGridSpec(
            num_scalar_prefetch=1, grid=(4, L),
            in_specs=[pl.BlockSpec((1, 1, hr, cdim), lambda qq, l, c_ref: (qq, l, c_ref[0], 0)),
                      pl.BlockSpec((1, 1, hr, cdim), lambda qq, l, c_ref: (qq, l, 0, 0))],
            out_specs=pl.BlockSpec((1, 1, hr, cdim), lambda qq, l, c_ref: (qq, l, 0, 0))),
        out_shape=jax.ShapeDtypeStruct((4, L, hr, cdim), BF16),
        compiler_params=_cparams(("parallel", "parallel")),
    )(core, g, r1)


def rs_exchange(sums):
    n = len(sums)

    def body(*refs):
        ins, outs = refs[:n], refs[n:2 * n]
        ssem, rsem = refs[2 * n:]
        x, y, c, others = _place()
        cps = []
        for i in range(n):
            for j, (ox, oy) in enumerate(others):
                cp = _remote(ins[i].at[2 * ox + oy], outs[i].at[j], ssem.at[3 * i + j], rsem.at[3 * i + j], (ox, oy, c))
                cp.start()
                cps.append(cp)
        for cp in cps:
            cp.wait()

    return pl.pallas_call(
        body, name="rs_exchange", in_specs=[ANY] * n, out_specs=[ANY] * n,
        out_shape=[jax.ShapeDtypeStruct((3,) + s.shape[1:], s.dtype) for s in sums],
        scratch_shapes=[pltpu.SemaphoreType.DMA((3 * n,)), pltpu.SemaphoreType.DMA((3 * n,))],
        compiler_params=pltpu.CompilerParams(has_side_effects=True),
    )(*sums)


def rs_final_sum(g, r1, r2, place, *, name):
    _, L, r, cdim = g.shape
    hr = r // 2

    def body(p_ref, g_ref, r1_ref, a_ref, b_ref, d_ref, o_ref):
        acc = g_ref[...].astype(F32) + r1_ref[...].astype(F32)
        acc = acc + a_ref[...].astype(F32)
        acc = acc + b_ref[...].astype(F32)
        o_ref[...] = (acc + d_ref[...].astype(F32))[0]

    other = lambda j: pl.BlockSpec((1, 1, hr, cdim), lambda l, p_ref: (j, l, 0, 0))
    return pl.pallas_call(
        body, name=name,
        grid_spec=pltpu.PrefetchScalarGridSpec(
            num_scalar_prefetch=1, grid=(L,),
            in_specs=[pl.BlockSpec((1, 1, hr, cdim), lambda l, p_ref: (p_ref[0], l, p_ref[1], 0)),
                      pl.BlockSpec((1, 1, hr, cdim), lambda l, p_ref: (p_ref[0], l, 0, 0)),
                      other(0), other(1), other(2)],
            out_specs=pl.BlockSpec((1, hr, cdim), lambda l, p_ref: (l, 0, 0))),
        out_shape=jax.ShapeDtypeStruct((L, hr, cdim), F32),
        compiler_params=_cparams(("parallel",)),
    )(place, g, r1, r2, r2, r2)


def rs_join_halves(halves):
    n = len(halves)

    def body(*refs):
        ins, outs = refs[:n], refs[n:2 * n]
        ssem, rsem, lsem = refs[2 * n:]
        x, y, c, _ = _place()
        sib = (x, y, 1 - c)
        cps = []
        for i in range(n):
            mine = _half(outs[i].shape[1], c)
            loc = pltpu.make_async_copy(ins[i], outs[i].at[:, mine, :], lsem.at[i])
            loc.start()
            cp = _remote(ins[i], outs[i].at[:, mine, :], ssem.at[i], rsem.at[i], sib)
            cp.start()
            cps.append((loc, cp))
        for i, (loc, cp) in enumerate(cps):
            loc.wait()
            cp.wait_send()
            theirs = _half(outs[i].shape[1], 1 - c)
            _remote(ins[i], outs[i].at[:, theirs, :], ssem.at[i], rsem.at[i], sib).wait_recv()

    return pl.pallas_call(
        body, name="rs_join_halves", in_specs=[ANY] * n, out_specs=[ANY] * n,
        out_shape=[jax.ShapeDtypeStruct((h.shape[0], 2 * h.shape[1], h.shape[2]), h.dtype) for h in halves],
        scratch_shapes=[pltpu.SemaphoreType.DMA((n,)), pltpu.SemaphoreType.DMA((n,)), pltpu.SemaphoreType.DMA((n,))],
        compiler_params=pltpu.CompilerParams(has_side_effects=True),
    )(*halves)


def all_reduce_small(v):
    R = v.shape[0]

    def body(v_ref, o_ref, buf, ssem, rsem):
        x, y, c, _ = _place()
        me = 4 * x + 2 * y + c
        buf[me] = v_ref[...]
        cps = []
        for k in range(1, 8):
            fx, fy, fc = (k >> 2) & 1, (k >> 1) & 1, k & 1
            px = jnp.where(fx == 1, 1 - x, x)
            py = jnp.where(fy == 1, 1 - y, y)
            pc = jnp.where(fc == 1, 1 - c, c)
            cp = _remote(v_ref, buf.at[me], ssem.at[k - 1], rsem.at[k - 1], (px, py, pc))
            cp.start()
            cps.append(cp)
        for cp in cps:
            cp.wait()
        acc = buf[0]
        for d in range(1, 8):
            acc = acc + buf[d]
        o_ref[...] = acc

    vm = pl.BlockSpec(memory_space=pltpu.VMEM)
    return pl.pallas_call(
        body, name="all_reduce_small", in_specs=[vm], out_specs=vm,
        out_shape=jax.ShapeDtypeStruct((R, LANES), F32),
        scratch_shapes=[pltpu.VMEM((8, R, LANES), F32), pltpu.SemaphoreType.DMA((7,)), pltpu.SemaphoreType.DMA((7,))],
        compiler_params=pltpu.CompilerParams(vmem_limit_bytes=VMEM_LIMIT, has_side_effects=True),
    )(v)


def _to_stacked(name, full):
    R, C = full.shape
    if name in ROW_SHARDED:
        return full.reshape(4, R // 4, C)
    return jnp.transpose(full.reshape(R, 4, C // 4), (1, 0, 2))


def _from_stacked(name, st):
    _, r, c = st.shape
    if name in ROW_SHARDED:
        return st.reshape(4 * r, c)
    return jnp.transpose(st, (1, 0, 2)).reshape(r, 4 * c)


def _layer_weights(gathered, l):
    w = {n: _from_stacked(n, gathered[n][:, l]) for n in BIG}
    win = w.pop("w_in")
    D = win.shape[0]
    p0, p1, p2, p3 = POOL_DIM, POOL_DIM + Q_LORA_RANK, POOL_DIM + Q_LORA_RANK + KV_LORA_RANK, \
        POOL_DIM + Q_LORA_RANK + KV_LORA_RANK + QK_ROPE_DIM
    w["w_pool"] = win[:, :p0]
    w["w_lat"] = jnp.concatenate([win[:, p0:p3], jnp.zeros((D, LAT_DIM - (p3 - p0)), win.dtype)], axis=1)
    w["w_gate"] = win[:, p3:]
    uq = w["w_uq"].reshape(Q_LORA_RANK, N_HEADS, QK_DIM)
    w["w_uq"] = jnp.concatenate([uq, jnp.zeros((Q_LORA_RANK, N_HEADS, HEAD_PAD - QK_DIM), uq.dtype)],
                                axis=2).reshape(Q_LORA_RANK, N_HEADS * HEAD_PAD)
    return w


def _layer_grads_stacked(dw):
    out = {}
    lat = dw.pop("w_lat")
    dw["w_in"] = jnp.concatenate([dw.pop("w_pool"), lat[:, :Q_LORA_RANK + KV_LORA_RANK + QK_ROPE_DIM],
                                  dw.pop("w_gate")], axis=1)
    dw["w_uq"] = dw["w_uq"].reshape(Q_LORA_RANK, N_HEADS, HEAD_PAD)[:, :, :QK_DIM].reshape(Q_LORA_RANK, N_HEADS * QK_DIM)
    for n in BIG:
        out[n] = _to_stacked(n, dw[n])
    return out


def _rope_tables(positions):
    inv_freq = ROPE_THETA ** (-jnp.arange(0, QK_ROPE_DIM, 2, dtype=F32) / QK_ROPE_DIM)
    ang = positions.astype(F32).reshape(-1)[:, None] * inv_freq
    cos, sin = jnp.cos(ang), jnp.sin(ang)
    z = jnp.zeros((ang.shape[0], LANES - QK_ROPE_DIM), F32)
    return jnp.concatenate([cos, cos, z], axis=1), jnp.concatenate([-sin, sin, z], axis=1)


def _pack_small(vals):
    parts = []
    for n in SMALL:
        f = vals[n].reshape(-1).astype(F32)
        pad = (-f.shape[0]) % (8 * LANES)
        parts.append(jnp.pad(f, (0, pad)))
    return jnp.concatenate(parts).reshape(-1, LANES)


def _unpack_small(packed, like):
    flat = packed.reshape(-1)
    out, off = {}, 0
    for n in SMALL:
        size = like[n].size
        out[n] = flat[off:off + size].reshape(like[n].shape)
        off += size + ((-size) % (8 * LANES))
    return out


def _ffn_fwd(x, g, wu, wd, tag):
    h = rms_fwd(x, g, name=f"{tag}_norm")
    u = mm(h, wu, out_dtype=BF16, name=f"{tag}_up")
    a = swiglu_fwd(u, name=f"{tag}_act")
    y = mm(a, wd, res=x, alpha=0.5, name=f"{tag}_down")
    return y, (x, h, u, a)


def _ffn_bwd(dy, saved, g, wu, wd, tag):
    x, h, u, a = saved
    da = mm(dy, wd, tb=True, alpha=0.5, name=f"{tag}_down_dx")
    dwd = mm(a, dy, ta=True, alpha=0.5, out_dtype=BF16, name=f"{tag}_down_dw")
    du = swiglu_bwd(u, da, name=f"{tag}_act_bwd")
    dwu = mm(h, du, ta=True, out_dtype=BF16, name=f"{tag}_up_dw")
    dh = mm(du, wu, tb=True, name=f"{tag}_up_dx")
    dx, dg = rms_bwd(x, g, dh, dy, name=f"{tag}_norm_bwd")
    return dx, dg, dwu, dwd


def _mix_fwd(x, p, w, cs, sn, B):
    h = rms_fwd(x, p["norm_mix"], name="mix_norm")
    lat = mm(h, w["w_lat"], name="mix_lat")
    xp = mm(h, w["w_pool"], name="mix_pool_in")
    gl = mm(h, w["w_gate"], name="mix_gate_in")
    mixed = pool_fwd(xp, p["pool_maps"].astype(BF16), p["pool_scale"], B, name="pool_fwd")
    ba = mm(mixed, w["w_pool_proj"], name="mix_pool_proj")
    qn, kvn, kr = lat_fwd(lat, p["q_latent_norm"], p["kv_latent_norm"], cs, sn, name="lat_fwd")
    q = q_rope(mm(qn, w["w_uq"], name="mix_uq"), cs, sn, transpose=False, name="q_rope")
    kv = mm(kvn, w["w_ukv"], out_dtype=BF16, name="mix_ukv")
    o, lse = attn_fwd(q, kv, kr, B, name="attn_fwd")
    bb = mm(o, w["w_attn_proj"], name="mix_attn_proj")
    merged = gate_fwd(gl, p["b_gate"], ba, bb, name="gate_fwd")
    y = mm(merged, w["w_out"], res=x, name="mix_out")
    return y, (x, h, lat, xp, gl, mixed, ba, qn, kvn, kr, q, kv, o, lse, bb, merged)


def _mix_bwd(dy, saved, p, w, cs, sn, B):
    x, h, lat, xp, gl, mixed, ba, qn, kvn, kr, q, kv, o, lse, bb, merged = saved
    dw, ds = {}, {}
    dm = mm(dy, w["w_out"], tb=True, name="mix_out_dx")
    dw["w_out"] = mm(merged, dy, ta=True, out_dtype=BF16, name="mix_out_dw")
    dba, dbb, dgl, ds["b_gate"] = gate_bwd(dm, gl, p["b_gate"], ba, bb, name="gate_bwd")
    dw["w_attn_proj"] = mm(o, dbb, ta=True, out_dtype=BF16, name="mix_attn_proj_dw")
    do = mm(dbb, w["w_attn_proj"], tb=True, out_dtype=BF16, name="mix_attn_proj_dx")
    dw["w_pool_proj"] = mm(mixed, dba, ta=True, out_dtype=BF16, name="mix_pool_proj_dw")
    dmixed = mm(dba, w["w_pool_proj"], tb=True, name="mix_pool_proj_dx")
    dxp, ds["pool_maps"], ds["pool_scale"] = pool_bwd(xp, dmixed, p["pool_maps"].astype(BF16), p["pool_scale"], B,
                                                      name="pool_bwd")
    dq, delta = attn_bwd_dq(q, kv, kr, o, do, lse, B, name="attn_bwd_dq")
    dkv, dkr = attn_bwd_dkv(q, kv, kr, do, lse, delta, B, name="attn_bwd_dkv")
    dw["w_ukv"] = mm(kvn, dkv, ta=True, out_dtype=BF16, name="mix_ukv_dw")
    dkvn = mm(dkv, w["w_ukv"], tb=True, name="mix_ukv_dx")
    dqb = q_rope(dq, cs, sn, transpose=True, name="q_rope_bwd")
    dw["w_uq"] = mm(qn, dqb, ta=True, out_dtype=BF16, name="mix_uq_dw")
    dqn = mm(dqb, w["w_uq"], tb=True, name="mix_uq_dx")
    dlat, ds["q_latent_norm"], ds["kv_latent_norm"] = lat_bwd(lat, p["q_latent_norm"], p["kv_latent_norm"], dqn, dkvn,
                                                               dkr, cs, sn, name="lat_bwd")
    dw["w_lat"] = mm(h, dlat, ta=True, out_dtype=BF16, name="mix_lat_dw")
    dw["w_pool"] = mm(h, dxp, ta=True, out_dtype=BF16, name="mix_pool_in_dw")
    dw["w_gate"] = mm(h, dgl, ta=True, out_dtype=BF16, name="mix_gate_in_dw")
    dh = mm(dlat, w["w_lat"], tb=True, name="mix_lat_dx")
    dh = mm(dxp, w["w_pool"], tb=True, res=dh, name="mix_pool_in_dx")
    dh = mm(dgl, w["w_gate"], tb=True, res=dh, name="mix_gate_in_dx")
    dx, ds["norm_mix"] = rms_bwd(x, p["norm_mix"], dh, dy, name="mix_norm_bwd")
    return dx, dw, ds


def kernel(x, positions, norm_ffn1, ffn1_up, ffn1_down, norm_mix, w_in, b_gate, pool_maps, pool_scale, w_pool_proj, q_latent_norm, w_uq, kv_latent_norm, w_ukv, w_attn_proj, w_out, norm_ffn2, ffn2_up, ffn2_down, final_norm, loss_target, m_norm_ffn1, m_ffn1_up, m_ffn1_down, m_norm_mix, m_w_in, m_b_gate, m_pool_maps, m_pool_scale, m_w_pool_proj, m_q_latent_norm, m_w_uq, m_kv_latent_norm, m_w_ukv, m_w_attn_proj, m_w_out, m_norm_ffn2, m_ffn2_up, m_ffn2_down, m_final_norm, v_norm_ffn1, v_ffn1_up, v_ffn1_down, v_norm_mix, v_w_in, v_b_gate, v_pool_maps, v_pool_scale, v_w_pool_proj, v_q_latent_norm, v_w_uq, v_kv_latent_norm, v_w_ukv, v_w_attn_proj, v_w_out, v_norm_ffn2, v_ffn2_up, v_ffn2_down, v_final_norm):
    given = dict(locals())
    B, S, D = x.shape
    T = B * S
    L = norm_ffn1.shape[0]
    W = {n: given[n] for n in WEIGHTS}
    Mo = {n: given["m_" + n] for n in WEIGHTS}
    Vo = {n: given["v_" + n] for n in WEIGHTS}
    core = lax.axis_index("c").astype(jnp.int32)
    chip = (2 * lax.axis_index("x") + lax.axis_index("y")).astype(jnp.int32)

    gathered = dict(zip(BIG, all_gather_weights([W[n].astype(BF16) for n in BIG])))
    cs, sn = _rope_tables(positions)

    xs = x.reshape(T, D)
    saved = []
    for l in range(L):
        w = _layer_weights(gathered, l)
        p = {n: W[n][l] for n in SMALL if n != "final_norm"}
        xs, s1 = _ffn_fwd(xs, p["norm_ffn1"], w["ffn1_up"], w["ffn1_down"], "ffn1")
        xs, s2 = _mix_fwd(xs, p, w, cs, sn, B)
        xs, s3 = _ffn_fwd(xs, p["norm_ffn2"], w["ffn2_up"], w["ffn2_down"], "ffn2")
        saved.append((w, p, s1, s2, s3))

    dx, dfinal, loss_tile = loss_head(xs, final_norm, loss_target.reshape(T, D), name="loss_head")
    loss = lax.psum(loss_tile[0, 0], ("x", "y", "c"))

    big_layers, small_layers = [], []
    for l in reversed(range(L)):
        w, p, s1, s2, s3 = saved[l]
        dx, dg2, dwu2, dwd2 = _ffn_bwd(dx, s3, p["norm_ffn2"], w["ffn2_up"], w["ffn2_down"], "ffn2")
        dx, dw, ds = _mix_bwd(dx, s2, p, w, cs, sn, B)
        dx, dg1, dwu1, dwd1 = _ffn_bwd(dx, s1, p["norm_ffn1"], w["ffn1_up"], w["ffn1_down"], "ffn1")
        dw.update(ffn1_up=dwu1, ffn1_down=dwd1, ffn2_up=dwu2, ffn2_down=dwd2)
        ds.update(norm_ffn1=dg1, norm_ffn2=dg2)
        big_layers.append(_layer_grads_stacked(dw))
        small_layers.append(ds)
    big_layers.reverse()
    small_layers.reverse()

    parts = [jnp.stack([big_layers[l][n] for l in range(L)], axis=1) for n in BIG]
    r1 = rs_swap_halves(parts)
    core_arr = core.reshape(1)
    sums = [rs_chip_sum(g, a, core_arr, name=f"rs_chip_sum_{n}") for n, g, a in zip(BIG, parts, r1)]
    r2 = rs_exchange(sums)
    place = jnp.stack([chip, core])
    halves = [rs_final_sum(g, a, b, place, name=f"rs_final_sum_{n}") for n, g, a, b in zip(BIG, parts, r1, r2)]
    grads = dict(zip(BIG, rs_join_halves(halves)))

    small = {n: jnp.stack([small_layers[l][n].reshape(W[n].shape[1:]) for l in range(L)]) for n in SMALL
             if n != "final_norm"}
    small["final_norm"] = dfinal.reshape(final_norm.shape)
    grads.update(_unpack_small(all_reduce_small(_pack_small(small)), small))

    delta, new_m, new_v = {}, {}, {}
    for n in BIG:
        sh = W[n].shape
        two = lambda a: a.reshape(sh[0] * sh[1], sh[2])
        d, nm, nv = adamw(two(W[n]), two(grads[n]), two(Mo[n]), two(Vo[n]), name=f"adamw_{n}")
        delta[n], new_m[n], new_v[n] = d.reshape(sh), nm.reshape(sh), nv.reshape(sh)
    d, nm, nv = adamw(_pack_small(W), _pack_small(grads), _pack_small(Mo), _pack_small(Vo), name="adamw_small")
    delta.update(_unpack_small(d, W))
    new_m.update(_unpack_small(nm, W))
    new_v.update(_unpack_small(nv, W))

    return (loss, dx.reshape(B, S, D), *[grads[n] for n in WEIGHTS], *[delta[n] for n in WEIGHTS],
            *[new_m[n] for n in WEIGHTS], *[new_v[n] for n in WEIGHTS])
```

```python
import functools

import jax
import jax.numpy as jnp
from jax import lax
from jax.experimental import pallas as pl
from jax.experimental.pallas import tpu as pltpu

F32 = jnp.float32
BF16 = jnp.bfloat16

N_HEADS = 8
QK_NOPE_DIM = 128
QK_ROPE_DIM = 64
QK_DIM = QK_NOPE_DIM + QK_ROPE_DIM
V_HEAD_DIM = 128
HEAD_PAD = 256
Q_LORA_RANK = 384
KV_LORA_RANK = 256
ROPE_THETA = 10000.0
POOL_WINDOWS = (2, 4, 8, 16)
N_POOL_GROUPS = 4
POOL_GROUP_DIM = 128
POOL_DIM = N_POOL_GROUPS * POOL_GROUP_DIM
LAT_DIM = 768
NORM_EPS = 1e-6
ADAM_LR = 0.001
ADAM_B1 = 0.9
ADAM_B2 = 0.999
ADAM_EPS = 1e-08
ADAM_WD = 0.01
ADAM_STEP = 10
NEG_INF = -1e30
LANES = 128
ATT_BLOCK = 512
VMEM_LIMIT = 48 * 1024 * 1024
MESH = pl.DeviceIdType.MESH

BIG = ("ffn1_up", "ffn1_down", "w_in", "w_pool_proj", "w_uq", "w_ukv", "w_attn_proj", "w_out",
       "ffn2_up", "ffn2_down")
ROW_SHARDED = ("ffn1_down", "w_attn_proj", "w_out", "ffn2_down")
SMALL = ("norm_ffn1", "norm_mix", "b_gate", "pool_maps", "pool_scale", "q_latent_norm",
         "kv_latent_norm", "norm_ffn2", "final_norm")
WEIGHTS = ("norm_ffn1", "ffn1_up", "ffn1_down", "norm_mix", "w_in", "b_gate", "pool_maps", "pool_scale",
           "w_pool_proj", "q_latent_norm", "w_uq", "kv_latent_norm", "w_ukv", "w_attn_proj", "w_out",
           "norm_ffn2", "ffn2_up", "ffn2_down", "final_norm")


def _pick(dim, cands):
    for c in cands:
        if c <= dim and dim % c == 0:
            return c
    return dim


def _cparams(sem=None, **kw):
    if sem is not None:
        kw["dimension_semantics"] = sem
    return pltpu.CompilerParams(vmem_limit_bytes=VMEM_LIMIT, **kw)


def _sigmoid(x):
    return 1.0 / (1.0 + jnp.exp(-x))


def mm(a, b, *, name, ta=False, tb=False, out_dtype=F32, res=None, alpha=1.0, dep=None):
    if ta:
        K, M = a.shape
    else:
        M, K = a.shape
    if tb:
        N, K2 = b.shape
    else:
        K2, N = b.shape
    assert K == K2, (a.shape, b.shape, ta, tb)
    tm = _pick(M, (1024, 768, 512, 384, 256, 128))
    tn = _pick(N, (1024, 768, 512, 384, 256, 128))
    tk = _pick(K, (1024, 512, 256, 128))
    nk = K // tk
    dims = (((0 if ta else 1,), (1 if tb else 0,)), ((), ()))

    def body(*refs):
        a_ref, b_ref = refs[:2]
        r_ref = refs[2] if res is not None else None
        o_ref = refs[-2] if nk > 1 else refs[-1]

        def finish(r):
            if alpha != 1.0:
                r = r * alpha
            if res is not None:
                r = r_ref[...].astype(F32) + r
            o_ref[...] = r.astype(out_dtype)

        part = lax.dot_general(a_ref[...].astype(BF16), b_ref[...].astype(BF16), dims, preferred_element_type=F32)
        if nk == 1:
            finish(part)
            return
        acc = refs[-1]
        k = pl.program_id(2)

        @pl.when(k == 0)
        def _():
            acc[...] = part

        @pl.when(k > 0)
        def _():
            acc[...] += part

        @pl.when(k == nk - 1)
        def _():
            finish(acc[...])

    a_spec = pl.BlockSpec((tk, tm), lambda i, j, k: (k, i)) if ta else pl.BlockSpec((tm, tk), lambda i, j, k: (i, k))
    b_spec = pl.BlockSpec((tn, tk), lambda i, j, k: (j, k)) if tb else pl.BlockSpec((tk, tn), lambda i, j, k: (k, j))
    o_spec = pl.BlockSpec((tm, tn), lambda i, j, k: (i, j))
    in_specs = [a_spec, b_spec]
    args = [a, b]
    if res is not None:
        in_specs.append(o_spec)
        args.append(res)
    if dep is not None:
        in_specs.append(pl.BlockSpec((8, LANES), lambda i, j, k: (0, 0)))
        args.append(dep)
    return pl.pallas_call(
        body, name=name, grid=(M // tm, N // tn, nk), in_specs=in_specs, out_specs=o_spec,
        out_shape=jax.ShapeDtypeStruct((M, N), out_dtype),
        scratch_shapes=[pltpu.VMEM((tm, tn), F32)] if nk > 1 else [],
        compiler_params=_cparams(("parallel", "parallel", "arbitrary")),
    )(*args)


def _rows(T, width_bytes):
    cap = max(8, (2 * 1024 * 1024) // width_bytes)
    return _pick(T, tuple(c for c in (1024, 512, 256, 128, 64, 32, 16) if c <= cap))


def rms_fwd(x, g, *, name):
    T, D = x.shape
    tm = _rows(T, D * 4)

    def body(x_ref, g_ref, h_ref):
        xv = x_ref[...]
        r = lax.rsqrt(jnp.mean(xv * xv, axis=-1, keepdims=True) + NORM_EPS)
        h_ref[...] = (xv * r * g_ref[...]).astype(BF16)

    return pl.pallas_call(
        body, name=name, grid=(T // tm,),
        in_specs=[pl.BlockSpec((tm, D), lambda i: (i, 0)), pl.BlockSpec((1, D), lambda i: (0, 0))],
        out_specs=pl.BlockSpec((tm, D), lambda i: (i, 0)),
        out_shape=jax.ShapeDtypeStruct((T, D), BF16),
        compiler_params=_cparams(("parallel",)),
    )(x, g.reshape(1, D))


def _rms_bwd_math(xv, gv, dh):
    r = lax.rsqrt(jnp.mean(xv * xv, axis=-1, keepdims=True) + NORM_EPS)
    xn = xv * r
    dg = jnp.sum(dh * xn, axis=0, keepdims=True)
    dxn = dh * gv
    dx = r * (dxn - xn * jnp.mean(dxn * xn, axis=-1, keepdims=True))
    return dx, dg


def rms_bwd(x, g, dh, dres, *, name):
    T, D = x.shape
    tm = _rows(T, D * 4)

    def body(x_ref, g_ref, dh_ref, dres_ref, dx_ref, dg_ref):
        @pl.when(pl.program_id(0) == 0)
        def _():
            dg_ref[...] = jnp.zeros_like(dg_ref)

        dx, dg = _rms_bwd_math(x_ref[...], g_ref[...], dh_ref[...].astype(F32))
        dx_ref[...] = dres_ref[...] + dx
        dg_ref[...] += dg

    row = pl.BlockSpec((tm, D), lambda i: (i, 0))
    vec = pl.BlockSpec((1, D), lambda i: (0, 0))
    return pl.pallas_call(
        body, name=name, grid=(T // tm,), in_specs=[row, vec, row, row], out_specs=[row, vec],
        out_shape=[jax.ShapeDtypeStruct((T, D), F32), jax.ShapeDtypeStruct((1, D), F32)],
        compiler_params=_cparams(("arbitrary",)),
    )(x, g.reshape(1, D), dh, dres)


def swiglu_fwd(u, *, name):
    T, F2 = u.shape
    Fh = F2 // 2
    tm = _rows(T, Fh * 4)

    def body(g_ref, u_ref, a_ref):
        gv = g_ref[...].astype(F32)
        a_ref[...] = (gv * _sigmoid(gv) * u_ref[...].astype(F32)).astype(BF16)

    return pl.pallas_call(
        body, name=name, grid=(T // tm,),
        in_specs=[pl.BlockSpec((tm, Fh), lambda i: (i, 0)), pl.BlockSpec((tm, Fh), lambda i: (i, 1))],
        out_specs=pl.BlockSpec((tm, Fh), lambda i: (i, 0)),
        out_shape=jax.ShapeDtypeStruct((T, Fh), BF16),
        compiler_params=_cparams(("parallel",)),
    )(u, u)


def swiglu_bwd(u, da, *, name):
    T, F2 = u.shape
    Fh = F2 // 2
    tm = _rows(T, F2 * 4)

    def body(g_ref, u_ref, da_ref, du_ref):
        gv = g_ref[...].astype(F32)
        uv = u_ref[...].astype(F32)
        dav = da_ref[...].astype(F32)
        s = _sigmoid(gv)
        du_ref[:, :Fh] = (dav * uv * (s * (1.0 + gv * (1.0 - s)))).astype(BF16)
        du_ref[:, Fh:] = (dav * (gv * s)).astype(BF16)

    return pl.pallas_call(
        body, name=name, grid=(T // tm,),
        in_specs=[pl.BlockSpec((tm, Fh), lambda i: (i, 0)), pl.BlockSpec((tm, Fh), lambda i: (i, 1)),
                  pl.BlockSpec((tm, Fh), lambda i: (i, 0))],
        out_specs=pl.BlockSpec((tm, F2), lambda i: (i, 0)),
        out_shape=jax.ShapeDtypeStruct((T, F2), BF16),
        compiler_params=_cparams(("parallel",)),
    )(u, u, da)


def _rope(xv, cv, sv):
    half = QK_ROPE_DIM // 2
    lane = lax.broadcasted_iota(jnp.int32, xv.shape, 1)
    swapped = jnp.where(lane < half, pltpu.roll(xv, LANES - half, 1), pltpu.roll(xv, half, 1))
    return xv * cv + swapped * sv


def _rope_t(dy, cv, sv):
    half = QK_ROPE_DIM // 2
    ds = dy * sv
    lane = lax.broadcasted_iota(jnp.int32, dy.shape, 1)
    swapped = jnp.where(lane < half, pltpu.roll(ds, LANES - half, 1), pltpu.roll(ds, half, 1))
    return dy * cv + swapped


def lat_fwd(lat, qn_w, kvn_w, cs, sn, *, name):
    T = lat.shape[0]
    tm = _rows(T, LAT_DIM * 4)
    kv0 = Q_LORA_RANK
    kr0 = Q_LORA_RANK + KV_LORA_RANK

    def body(lat_ref, qw_ref, kw_ref, c_ref, s_ref, qn_ref, kvn_ref, kr_ref):
        ql = lat_ref[:, :kv0]
        r = lax.rsqrt(jnp.mean(ql * ql, axis=-1, keepdims=True) + NORM_EPS)
        qn_ref[...] = (ql * r * qw_ref[...]).astype(BF16)
        kl = lat_ref[:, kv0:kr0]
        r = lax.rsqrt(jnp.mean(kl * kl, axis=-1, keepdims=True) + NORM_EPS)
        kvn_ref[...] = (kl * r * kw_ref[...]).astype(BF16)
        kr_ref[...] = _rope(lat_ref[:, kr0:], c_ref[...], s_ref[...]).astype(BF16)

    return pl.pallas_call(
        body, name=name, grid=(T // tm,),
        in_specs=[pl.BlockSpec((tm, LAT_DIM), lambda i: (i, 0)),
                  pl.BlockSpec((1, Q_LORA_RANK), lambda i: (0, 0)),
                  pl.BlockSpec((1, KV_LORA_RANK), lambda i: (0, 0)),
                  pl.BlockSpec((tm, LANES), lambda i: (i, 0)), pl.BlockSpec((tm, LANES), lambda i: (i, 0))],
        out_specs=[pl.BlockSpec((tm, Q_LORA_RANK), lambda i: (i, 0)),
                   pl.BlockSpec((tm, KV_LORA_RANK), lambda i: (i, 0)),
                   pl.BlockSpec((tm, LANES), lambda i: (i, 0))],
        out_shape=[jax.ShapeDtypeStruct((T, Q_LORA_RANK), BF16), jax.ShapeDtypeStruct((T, KV_LORA_RANK), BF16),
                   jax.ShapeDtypeStruct((T, LANES), BF16)],
        compiler_params=_cparams(("parallel",)),
    )(lat, qn_w.reshape(1, -1), kvn_w.reshape(1, -1), cs, sn)


def lat_bwd(lat, qn_w, kvn_w, dqn, dkvn, dkr, cs, sn, *, name):
    T = lat.shape[0]
    tm = _rows(T, LAT_DIM * 4)
    kv0 = Q_LORA_RANK
    kr0 = Q_LORA_RANK + KV_LORA_RANK

    def body(lat_ref, qw_ref, kw_ref, dqn_ref, dkvn_ref, dkr_ref, c_ref, s_ref, dlat_ref, dqw_ref, dkw_ref):
        @pl.when(pl.program_id(0) == 0)
        def _():
            dqw_ref[...] = jnp.zeros_like(dqw_ref)
            dkw_ref[...] = jnp.zeros_like(dkw_ref)

        dx, dg = _rms_bwd_math(lat_ref[:, :kv0], qw_ref[...], dqn_ref[...])
        dlat_ref[:, :kv0] = dx.astype(BF16)
        dqw_ref[...] += dg
        dx, dg = _rms_bwd_math(lat_ref[:, kv0:kr0], kw_ref[...], dkvn_ref[...])
        dlat_ref[:, kv0:kr0] = dx.astype(BF16)
        dkw_ref[...] += dg
        dlat_ref[:, kr0:] = _rope_t(dkr_ref[...], c_ref[...], s_ref[...]).astype(BF16)

    row = lambda w: pl.BlockSpec((tm, w), lambda i: (i, 0))
    vec = lambda w: pl.BlockSpec((1, w), lambda i: (0, 0))
    return pl.pallas_call(
        body, name=name, grid=(T // tm,),
        in_specs=[row(LAT_DIM), vec(Q_LORA_RANK), vec(KV_LORA_RANK), row(Q_LORA_RANK), row(KV_LORA_RANK),
                  row(LANES), row(LANES), row(LANES)],
        out_specs=[row(LAT_DIM), vec(Q_LORA_RANK), vec(KV_LORA_RANK)],
        out_shape=[jax.ShapeDtypeStruct((T, LAT_DIM), BF16), jax.ShapeDtypeStruct((1, Q_LORA_RANK), F32),
                   jax.ShapeDtypeStruct((1, KV_LORA_RANK), F32)],
        compiler_params=_cparams(("arbitrary",)),
    )(lat, qn_w.reshape(1, -1), kvn_w.reshape(1, -1), dqn, dkvn, dkr, cs, sn)


def q_rope(q, cs, sn, *, transpose, name):
    T, W = q.shape
    tm = _rows(T, W * 4)
    fn = _rope_t if transpose else _rope
    scale = QK_DIM ** -0.5

    def body(q_ref, c_ref, s_ref, o_ref):
        cv = c_ref[...] * scale
        sv = s_ref[...] * scale
        for h in range(N_HEADS):
            lo = h * HEAD_PAD
            o_ref[:, lo:lo + QK_NOPE_DIM] = (q_ref[:, lo:lo + QK_NOPE_DIM] * scale).astype(BF16)
            o_ref[:, lo + QK_NOPE_DIM:lo + HEAD_PAD] = fn(q_ref[:, lo + QK_NOPE_DIM:lo + HEAD_PAD], cv, sv).astype(BF16)

    return pl.pallas_call(
        body, name=name, grid=(T // tm,),
        in_specs=[pl.BlockSpec((tm, W), lambda i: (i, 0)), pl.BlockSpec((tm, LANES), lambda i: (i, 0)),
                  pl.BlockSpec((tm, LANES), lambda i: (i, 0))],
        out_specs=pl.BlockSpec((tm, W), lambda i: (i, 0)),
        out_shape=jax.ShapeDtypeStruct((T, W), BF16),
        compiler_params=_cparams(("parallel",)),
    )(q, cs, sn)


def gate_fwd(gl, bg, ba, bb, *, name):
    T, D2 = gl.shape
    D = D2 // 2
    tm = _rows(T, D2 * 4)

    def body(gl_ref, bg_ref, ba_ref, bb_ref, m_ref):
        ga = _sigmoid(gl_ref[:, :D] + bg_ref[:, :D])
        gb = _sigmoid(gl_ref[:, D:] + bg_ref[:, D:])
        m_ref[...] = (ga * ba_ref[...] + gb * bb_ref[...]).astype(BF16)

    row = lambda w: pl.BlockSpec((tm, w), lambda i: (i, 0))
    return pl.pallas_call(
        body, name=name, grid=(T // tm,),
        in_specs=[row(D2), pl.BlockSpec((1, D2), lambda i: (0, 0)), row(D), row(D)],
        out_specs=row(D), out_shape=jax.ShapeDtypeStruct((T, D), BF16),
        compiler_params=_cparams(("parallel",)),
    )(gl, bg.reshape(1, D2), ba, bb)


def gate_bwd(dm, gl, bg, ba, bb, *, name):
    T, D2 = gl.shape
    D = D2 // 2
    tm = _rows(T, D2 * 4)

    def body(dm_ref, gl_ref, bg_ref, ba_ref, bb_ref, dba_ref, dbb_ref, dgl_ref, dbg_ref):
        @pl.when(pl.program_id(0) == 0)
        def _():
            dbg_ref[...] = jnp.zeros_like(dbg_ref)

        dmv = dm_ref[...]
        ga = _sigmoid(gl_ref[:, :D] + bg_ref[:, :D])
        gb = _sigmoid(gl_ref[:, D:] + bg_ref[:, D:])
        dba_ref[...] = (dmv * ga).astype(BF16)
        dbb_ref[...] = (dmv * gb).astype(BF16)
        dla = dmv * ba_ref[...] * ga * (1.0 - ga)
        dlb = dmv * bb_ref[...] * gb * (1.0 - gb)
        dgl_ref[:, :D] = dla.astype(BF16)
        dgl_ref[:, D:] = dlb.astype(BF16)
        dbg_ref[:, :D] += jnp.sum(dla, axis=0, keepdims=True)
        dbg_ref[:, D:] += jnp.sum(dlb, axis=0, keepdims=True)

    row = lambda w: pl.BlockSpec((tm, w), lambda i: (i, 0))
    vec = pl.BlockSpec((1, D2), lambda i: (0, 0))
    return pl.pallas_call(
        body, name=name, grid=(T // tm,),
        in_specs=[row(D), row(D2), vec, row(D), row(D)],
        out_specs=[row(D), row(D), row(D2), vec],
        out_shape=[jax.ShapeDtypeStruct((T, D), BF16), jax.ShapeDtypeStruct((T, D), BF16),
                   jax.ShapeDtypeStruct((T, D2), BF16), jax.ShapeDtypeStruct((1, D2), F32)],
        compiler_params=_cparams(("arbitrary",)),
    )(dm, gl, bg.reshape(1, D2), ba, bb)


def loss_head(x, gf, tgt, *, name):
    T, D = x.shape
    tm = _rows(T, D * 4)

    def body(x_ref, g_ref, t_ref, dx_ref, dg_ref, loss_ref):
        @pl.when(pl.program_id(0) == 0)
        def _():
            dg_ref[...] = jnp.zeros_like(dg_ref)
            loss_ref[...] = jnp.zeros_like(loss_ref)

        xv = x_ref[...]
        gv = g_ref[...]
        r = lax.rsqrt(jnp.mean(xv * xv, axis=-1, keepdims=True) + NORM_EPS)
        xn = xv * r
        err = xn * gv - t_ref[...]
        loss_ref[...] += 0.5 * jnp.sum(jnp.mean(err * err, axis=-1, keepdims=True))
        dy = err * (1.0 / D)
        dg_ref[...] += jnp.sum(dy * xn, axis=0, keepdims=True)
        dxn = dy * gv
        dx_ref[...] = r * (dxn - xn * jnp.mean(dxn * xn, axis=-1, keepdims=True))

    row = pl.BlockSpec((tm, D), lambda i: (i, 0))
    vec = pl.BlockSpec((1, D), lambda i: (0, 0))
    return pl.pallas_call(
        body, name=name, grid=(T // tm,), in_specs=[row, vec, row],
        out_specs=[row, vec, pl.BlockSpec((8, LANES), lambda i: (0, 0))],
        out_shape=[jax.ShapeDtypeStruct((T, D), F32), jax.ShapeDtypeStruct((1, D), F32),
                   jax.ShapeDtypeStruct((8, LANES), F32)],
        compiler_params=_cparams(("arbitrary",)),
    )(x, gf.reshape(1, D), tgt)


def _shift_rows(s, k, down):
    n = s.shape[0]
    t = lax.broadcasted_iota(jnp.int32, s.shape, 0)
    if down:
        return jnp.where(t >= k, pltpu.roll(s, k, 0), 0.0)
    return jnp.where(t < n - k, pltpu.roll(s, n - k, 0), 0.0)


def _window_sum(s, w, down):
    k = 1
    while k < w:
        s = s + _shift_rows(s, k, down)
        k *= 2
    return s


def _pool_count(shape, w):
    t = lax.broadcasted_iota(jnp.int32, shape, 0)
    return jnp.minimum(t + 1, w).astype(F32)


def pool_fwd(xp, maps, scale, B, *, name):
    T, P = xp.shape
    S = T // B
    G = POOL_GROUP_DIM

    def body(x_ref, m_ref, sc_ref, o_ref):
        for g, w in enumerate(POOL_WINDOWS):
            xg = x_ref[:, g * G:(g + 1) * G]
            pooled = _window_sum(xg, w, True) / _pool_count(xg.shape, w) - xg
            mixed = jnp.dot(pooled.astype(BF16), m_ref[g], preferred_element_type=F32)
            o_ref[:, g * G:(g + 1) * G] = (mixed * sc_ref[:, g * G:(g + 1) * G]).astype(BF16)

    return pl.pallas_call(
        body, name=name, grid=(B,),
        in_specs=[pl.BlockSpec((S, P), lambda b: (b, 0)), pl.BlockSpec((N_POOL_GROUPS, G, G), lambda b: (0, 0, 0)),
                  pl.BlockSpec((1, P), lambda b: (0, 0))],
        out_specs=pl.BlockSpec((S, P), lambda b: (b, 0)),
        out_shape=jax.ShapeDtypeStruct((T, P), BF16),
        compiler_params=_cparams(("parallel",)),
    )(xp, maps, scale.reshape(1, P))


def pool_bwd(xp, dmixed, maps, scale, B, *, name):
    T, P = xp.shape
    S = T // B
    G = POOL_GROUP_DIM

    def body(x_ref, dm_ref, m_ref, sc_ref, dx_ref, dmaps_ref, dsc_ref):
        @pl.when(pl.program_id(0) == 0)
        def _():
            dmaps_ref[...] = jnp.zeros_like(dmaps_ref)
            dsc_ref[...] = jnp.zeros_like(dsc_ref)

        for g, w in enumerate(POOL_WINDOWS):
            cols = slice(g * G, (g + 1) * G)
            xg = x_ref[:, cols]
            cnt = _pool_count(xg.shape, w)
            pooled = (_window_sum(xg, w, True) / cnt - xg).astype(BF16)
            mixed = jnp.dot(pooled, m_ref[g], preferred_element_type=F32)
            dmx = dm_ref[:, cols]
            dsc_ref[:, cols] += jnp.sum(dmx * mixed, axis=0, keepdims=True)
            dmp = (dmx * sc_ref[:, cols]).astype(BF16)
            dmaps_ref[g] += lax.dot_general(pooled, dmp, (((0,), (0,)), ((), ())), preferred_element_type=F32)
            dpooled = lax.dot_general(dmp, m_ref[g], (((1,), (1,)), ((), ())), preferred_element_type=F32)
            dx_ref[:, cols] = (_window_sum(dpooled / cnt, w, False) - dpooled).astype(BF16)

    blk = pl.BlockSpec((S, P), lambda b: (b, 0))
    mp = pl.BlockSpec((N_POOL_GROUPS, G, G), lambda b: (0, 0, 0))
    vec = pl.BlockSpec((1, P), lambda b: (0, 0))
    return pl.pallas_call(
        body, name=name, grid=(B,), in_specs=[blk, blk, mp, vec], out_specs=[blk, mp, vec],
        out_shape=[jax.ShapeDtypeStruct((T, P), BF16), jax.ShapeDtypeStruct((N_POOL_GROUPS, G, G), F32),
                   jax.ShapeDtypeStruct((1, P), F32)],
        compiler_params=_cparams(("arbitrary",)),
    )(xp, dmixed, maps, scale.reshape(1, P))


_NT = (((1,), (1,)), ((), ()))
_TN = (((0,), (0,)), ((), ()))


def _keys(kv_ref, kr_ref, rows):
    return jnp.concatenate([kv_ref[rows, :QK_NOPE_DIM], kr_ref[rows, :]], axis=1)


def _causal(s):
    row = lax.broadcasted_iota(jnp.int32, s.shape, 0)
    col = lax.broadcasted_iota(jnp.int32, s.shape, 1)
    return jnp.where(row >= col, s, NEG_INF)


def attn_fwd(q, kv, kr, B, *, name):
    T = q.shape[0]
    S = T // B
    blk = min(ATT_BLOCK, S)
    nb = S // blk
    H = N_HEADS

    def body(q_ref, kv_ref, kr_ref, o_ref, lse_ref):
        for qi in range(nb):
            rows = slice(qi * blk, (qi + 1) * blk)
            qb = q_ref[rows, :]
            sd = _causal(lax.dot_general(qb, _keys(kv_ref, kr_ref, rows), _NT, preferred_element_type=F32))
            m = jnp.max(sd, axis=-1, keepdims=True)
            if qi > 0:
                prev = slice(0, qi * blk)
                sp = lax.dot_general(qb, _keys(kv_ref, kr_ref, prev), _NT, preferred_element_type=F32)
                m = jnp.maximum(m, jnp.max(sp, axis=-1, keepdims=True))
            pd = jnp.exp(sd - m)
            l = jnp.sum(pd, axis=-1, keepdims=True)
            acc = jnp.dot(pd.astype(BF16), kv_ref[rows, QK_NOPE_DIM:], preferred_element_type=F32)
            if qi > 0:
                pp = jnp.exp(sp - m)
                l = l + jnp.sum(pp, axis=-1, keepdims=True)
                acc = acc + jnp.dot(pp.astype(BF16), kv_ref[prev, QK_NOPE_DIM:], preferred_element_type=F32)
            o_ref[rows, :] = (acc / l).astype(BF16)
            lse_ref[0, rows, :] = m + jnp.log(l)

    return pl.pallas_call(
        body, name=name, grid=(B, H),
        in_specs=[pl.BlockSpec((S, HEAD_PAD), lambda b, h: (b, h)), pl.BlockSpec((S, HEAD_PAD), lambda b, h: (b, h)),
                  pl.BlockSpec((S, LANES), lambda b, h: (b, 0))],
        out_specs=[pl.BlockSpec((S, V_HEAD_DIM), lambda b, h: (b, h)), pl.BlockSpec((1, S, 1), lambda b, h: (h, b, 0))],
        out_shape=[jax.ShapeDtypeStruct((T, H * V_HEAD_DIM), BF16), jax.ShapeDtypeStruct((H, T, 1), F32)],
        compiler_params=_cparams(("parallel", "parallel")),
    )(q, kv, kr)


def attn_bwd(q, kv, kr, o, do, lse, B, *, name):
    T = q.shape[0]
    S = T // B
    blk = min(ATT_BLOCK, S)
    nb = S // blk
    H = N_HEADS

    def body(q_ref, kv_ref, kr_ref, o_ref, do_ref, lse_ref, dq_ref, dkv_ref, dkr_ref, dk_s, dv_s):
        dk_s[...] = jnp.zeros_like(dk_s)
        dv_s[...] = jnp.zeros_like(dv_s)

        @pl.when(pl.program_id(1) == 0)
        def _():
            dkr_ref[...] = jnp.zeros_like(dkr_ref)

        for qi in range(nb):
            rows = slice(qi * blk, (qi + 1) * blk)
            qb = q_ref[rows, :]
            dob = do_ref[rows, :]
            delta = jnp.sum(dob.astype(F32) * o_ref[rows, :].astype(F32), axis=-1, keepdims=True)
            lse_b = lse_ref[0, rows, :]

            def part(ks, diagonal):
                k = _keys(kv_ref, kr_ref, ks)
                s = lax.dot_general(qb, k, _NT, preferred_element_type=F32)
                if diagonal:
                    s = _causal(s)
                p = jnp.exp(s - lse_b)
                dp = lax.dot_general(dob, kv_ref[ks, QK_NOPE_DIM:], _NT, preferred_element_type=F32)
                ds = (p * (dp - delta)).astype(BF16)
                dv_s[ks, :] += lax.dot_general(p.astype(BF16), dob, _TN, preferred_element_type=F32)
                dk_s[ks, :] += lax.dot_general(ds, qb, _TN, preferred_element_type=F32)
                return jnp.dot(ds, k, preferred_element_type=F32)

            dq = part(rows, True)
            if qi > 0:
                dq = dq + part(slice(0, qi * blk), False)
            dq_ref[rows, :] = dq

        dkv_ref[:, :QK_NOPE_DIM] = dk_s[:, :QK_NOPE_DIM].astype(BF16)
        dkv_ref[:, QK_NOPE_DIM:] = dv_s[...].astype(BF16)
        dkr_ref[...] += dk_s[:, QK_NOPE_DIM:]

    head = lambda w: pl.BlockSpec((S, w), lambda b, h: (b, h))
    shared = pl.BlockSpec((S, LANES), lambda b, h: (b, 0))
    return pl.pallas_call(
        body, name=name, grid=(B, H),
        in_specs=[head(HEAD_PAD), head(HEAD_PAD), shared, head(V_HEAD_DIM), head(V_HEAD_DIM),
                  pl.BlockSpec((1, S, 1), lambda b, h: (h, b, 0))],
        out_specs=[head(HEAD_PAD), head(HEAD_PAD), shared],
        out_shape=[jax.ShapeDtypeStruct((T, H * HEAD_PAD), F32), jax.ShapeDtypeStruct((T, H * HEAD_PAD), BF16),
                   jax.ShapeDtypeStruct((T, LANES), F32)],
        scratch_shapes=[pltpu.VMEM((S, HEAD_PAD), F32), pltpu.VMEM((S, V_HEAD_DIM), F32)],
        compiler_params=_cparams(("parallel", "arbitrary")),
    )(q, kv, kr, o, do, lse)


def adamw(w, g, m, v, *, name):
    R, C = w.shape
    cap = max(8, (1024 * 1024) // (C * 4))
    tr = _pick(R, tuple(c for c in (1024, 512, 256, 128, 64, 32, 16, 8) if c <= cap))
    c1 = 1.0 - ADAM_B1 ** ADAM_STEP
    c2 = 1.0 - ADAM_B2 ** ADAM_STEP

    def body(w_ref, g_ref, m_ref, v_ref, d_ref, nm_ref, nv_ref):
        gv = g_ref[...]
        mv = ADAM_B1 * m_ref[...] + (1.0 - ADAM_B1) * gv
        vv = ADAM_B2 * v_ref[...] + (1.0 - ADAM_B2) * (gv * gv)
        nm_ref[...] = mv
        nv_ref[...] = vv
        d_ref[...] = -ADAM_LR * ((mv / c1) / (jnp.sqrt(vv / c2) + ADAM_EPS) + ADAM_WD * w_ref[...])

    blk = pl.BlockSpec((tr, C), lambda i: (i, 0))
    sh = jax.ShapeDtypeStruct((R, C), F32)
    return pl.pallas_call(
        body, name=name, grid=(R // tr,), in_specs=[blk] * 4, out_specs=[blk] * 3, out_shape=[sh] * 3,
        compiler_params=_cparams(("parallel",)),
    )(w, g, m, v)


ANY = pl.BlockSpec(memory_space=pl.ANY)


def _place():
    x, y, c = lax.axis_index("x"), lax.axis_index("y"), lax.axis_index("c")
    others = [(1 - x, y), (x, 1 - y), (1 - x, 1 - y)]
    return x, y, c, others


def _remote(src, dst, ssem, rsem, dev):
    return pltpu.make_async_remote_copy(src_ref=src, dst_ref=dst, send_sem=ssem, recv_sem=rsem,
                                        device_id=dev, device_id_type=MESH)


def _half(ref_rows, c):
    hr = ref_rows // 2
    return pl.ds(pl.multiple_of(c * hr, 16), hr)


HBM = pl.BlockSpec(memory_space=pltpu.HBM)
SEMS = pl.BlockSpec(memory_space=pltpu.SEMAPHORE)
EFFECT = pltpu.SideEffectType.DATAFLOW_SIDE_EFFECTING


def exchange_begin(name, srcs, land_shapes, plan, ncopies):
    ns, nl = len(srcs), len(land_shapes)

    def body(*refs):
        ssem, rsem = refs[ns + nl], refs[ns + nl + 1]
        for k, (s, d, dev) in enumerate(plan(refs[:ns], refs[ns:ns + nl])):
            _remote(s, d, ssem.at[k], rsem.at[k], dev).start()
        refs[-1][...] = jnp.zeros_like(refs[-1])

    bufs = [pltpu.HBM(s.shape, s.dtype) for s in srcs] + [pltpu.HBM(s.shape, s.dtype) for s in land_shapes]
    args = [pltpu.with_memory_space_constraint(s, pltpu.HBM) for s in srcs]
    args += [pltpu.with_memory_space_constraint(lax.empty(s.shape, s.dtype), pltpu.HBM) for s in land_shapes]
    out = pl.pallas_call(
        body, name=name,
        out_shape=(pltpu.SemaphoreType.DMA((ncopies,)), pltpu.SemaphoreType.DMA((ncopies,)), *bufs,
                   jax.ShapeDtypeStruct((8, LANES), F32)),
        in_specs=[HBM] * (ns + nl),
        out_specs=(SEMS, SEMS, *([HBM] * (ns + nl)), pl.BlockSpec(memory_space=pltpu.VMEM)),
        input_output_aliases={i: 2 + i for i in range(ns + nl)},
        compiler_params=pltpu.CompilerParams(has_side_effects=EFFECT),
    )(*args)
    return (out[0], out[1], out[2:2 + ns], out[2 + ns:2 + ns + nl]), out[-1]


def exchange_end(name, handle, plan, after):
    ssem, rsem, srcs, lands = handle
    ns, nl = len(srcs), len(lands)

    def body(*refs):
        ssem_ref, rsem_ref = refs[ns + nl], refs[ns + nl + 1]
        for k, (s, d, dev) in enumerate(plan(refs[:ns], refs[ns:ns + nl])):
            cp = _remote(s, d, ssem_ref.at[k], rsem_ref.at[k], dev)
            cp.wait_send()
            cp.wait_recv()

    out = pl.pallas_call(
        body, name=name,
        out_shape=tuple(pltpu.HBM(s.shape, s.dtype) for s in (*srcs, *lands)),
        in_specs=[HBM] * (ns + nl) + [SEMS, SEMS, ANY], out_specs=tuple([HBM] * (ns + nl)),
        input_output_aliases={i: i for i in range(ns + nl)},
        compiler_params=pltpu.CompilerParams(has_side_effects=EFFECT),
    )(*srcs, *lands, ssem, rsem, after)
    return list(out[ns:])


def ag_plan(src_refs, land_refs):
    x, y, c, others = _place()
    plan = []
    for s, d in zip(src_refs, land_refs):
        mine = _half(s.shape[0], c)
        for ox, oy in others:
            plan.append((s.at[mine, :], d.at[2 * x + y, mine, :], (ox, oy, c)))
    return plan


def ag_forward(lands, *, name):
    n = len(lands)

    def body(*refs):
        ins, outs = refs[:n], refs[n:2 * n]
        ssem, rsem = refs[2 * n:]
        x, y, c, others = _place()
        sent = []
        for i in range(n):
            mine = _half(ins[i].shape[1], c)
            for j, (ox, oy) in enumerate(others):
                cp = _remote(ins[i].at[2 * ox + oy, mine, :], outs[i].at[2 * ox + oy, mine, :], ssem.at[3 * i + j],
                             rsem.at[3 * i + j], (x, y, 1 - c))
                cp.start()
                sent.append(cp)
        for cp in sent:
            cp.wait()

    return pl.pallas_call(
        body, name=name, in_specs=[ANY] * n, out_specs=[ANY] * n,
        out_shape=[jax.ShapeDtypeStruct(a.shape, a.dtype) for a in lands],
        input_output_aliases={i: i for i in range(n)},
        scratch_shapes=[pltpu.SemaphoreType.DMA((3 * n,)), pltpu.SemaphoreType.DMA((3 * n,))],
        compiler_params=pltpu.CompilerParams(has_side_effects=True),
    )(*lands)


def rs_swap_halves(grads, *, name):
    n = len(grads)

    def body(*refs):
        ins, outs = refs[:n], refs[n:2 * n]
        ssem, rsem = refs[2 * n:]
        x, y, c, _ = _place()
        cps = []
        for i in range(n):
            theirs = _half(ins[i].shape[1], 1 - c)
            cp = _remote(ins[i].at[:, theirs, :], outs[i], ssem.at[i], rsem.at[i], (x, y, 1 - c))
            cp.start()
            cps.append(cp)
        for cp in cps:
            cp.wait()

    return pl.pallas_call(
        body, name=name, in_specs=[ANY] * n, out_specs=[ANY] * n,
        out_shape=[jax.ShapeDtypeStruct((4, g.shape[1] // 2, g.shape[2]), g.dtype) for g in grads],
        scratch_shapes=[pltpu.SemaphoreType.DMA((n,)), pltpu.SemaphoreType.DMA((n,))],
        compiler_params=pltpu.CompilerParams(has_side_effects=True),
    )(*grads)


def rs_chip_sum(g, r1, core, *, name):
    _, r, cdim = g.shape
    hr = r // 2

    def body(c_ref, g_ref, r1_ref, o_ref):
        o_ref[...] = (g_ref[...].astype(F32) + r1_ref[...].astype(F32)).astype(BF16)

    return pl.pallas_call(
        body, name=name,
        grid_spec=pltpu.PrefetchScalarGridSpec(
            num_scalar_prefetch=1, grid=(4,),
            in_specs=[pl.BlockSpec((1, hr, cdim), lambda qq, c_ref: (qq, c_ref[0], 0)),
                      pl.BlockSpec((1, hr, cdim), lambda qq, c_ref: (qq, 0, 0))],
            out_specs=pl.BlockSpec((1, hr, cdim), lambda qq, c_ref: (qq, 0, 0))),
        out_shape=jax.ShapeDtypeStruct((4, hr, cdim), BF16),
        compiler_params=_cparams(("parallel",)),
    )(core, g, r1)


def rs_plan(src_refs, land_refs):
    x, y, c, others = _place()
    plan = []
    for s, d in zip(src_refs, land_refs):
        for j, (ox, oy) in enumerate(others):
            plan.append((s.at[2 * ox + oy], d.at[j], (ox, oy, c)))
    return plan


def rs_final_sum(g, r1, r2, place, acc, l, *, name):
    _, r, cdim = g.shape
    hr = r // 2
    ch = hr // 2

    def body(p_ref, g_ref, r1_ref, a_ref, b_ref, d_ref, acc_in, o_ref):
        s = g_ref[...].astype(F32) + r1_ref[...].astype(F32)
        s = s + a_ref[...].astype(F32)
        s = s + b_ref[...].astype(F32)
        o_ref[...] = s + d_ref[...].astype(F32)

    other = lambda j: pl.BlockSpec((1, ch, cdim), lambda t, p_ref: (j, t, 0))
    return pl.pallas_call(
        body, name=name,
        grid_spec=pltpu.PrefetchScalarGridSpec(
            num_scalar_prefetch=1, grid=(2,),
            in_specs=[pl.BlockSpec((1, ch, cdim), lambda t, p_ref: (p_ref[0], 2 * p_ref[1] + t, 0)),
                      pl.BlockSpec((1, ch, cdim), lambda t, p_ref: (p_ref[0], t, 0)),
                      other(0), other(1), other(2), ANY],
            out_specs=pl.BlockSpec((1, ch, cdim), lambda t, p_ref: (l, 2 * p_ref[1] + t, 0))),
        out_shape=jax.ShapeDtypeStruct(acc.shape, F32),
        input_output_aliases={6: 0},
        compiler_params=_cparams(("parallel",)),
    )(place, g, r1, r2, r2, r2, acc)


def rs_join_halves(grads):
    n = len(grads)

    def body(*refs):
        ins, outs = refs[:n], refs[n:2 * n]
        ssem, rsem = refs[2 * n:]
        x, y, c, _ = _place()
        cps = []
        for i in range(n):
            mine = _half(ins[i].shape[1], c)
            cp = _remote(ins[i].at[:, mine, :], outs[i].at[:, mine, :], ssem.at[i], rsem.at[i], (x, y, 1 - c))
            cp.start()
            cps.append(cp)
        for cp in cps:
            cp.wait()

    return pl.pallas_call(
        body, name="rs_join_halves", in_specs=[ANY] * n, out_specs=[ANY] * n,
        out_shape=[jax.ShapeDtypeStruct(g.shape, g.dtype) for g in grads],
        input_output_aliases={i: i for i in range(n)},
        scratch_shapes=[pltpu.SemaphoreType.DMA((n,)), pltpu.SemaphoreType.DMA((n,))],
        compiler_params=pltpu.CompilerParams(has_side_effects=True),
    )(*grads)


def all_reduce_small(v):
    R = v.shape[0]

    def body(v_ref, o_ref, buf, ssem, rsem):
        x, y, c, _ = _place()
        me = 4 * x + 2 * y + c
        buf[me] = v_ref[...]
        cps = []
        for k in range(1, 8):
            fx, fy, fc = (k >> 2) & 1, (k >> 1) & 1, k & 1
            px = jnp.where(fx == 1, 1 - x, x)
            py = jnp.where(fy == 1, 1 - y, y)
            pc = jnp.where(fc == 1, 1 - c, c)
            cp = _remote(v_ref, buf.at[me], ssem.at[k - 1], rsem.at[k - 1], (px, py, pc))
            cp.start()
            cps.append(cp)
        for cp in cps:
            cp.wait()
        acc = buf[0]
        for d in range(1, 8):
            acc = acc + buf[d]
        o_ref[...] = acc

    vm = pl.BlockSpec(memory_space=pltpu.VMEM)
    return pl.pallas_call(
        body, name="all_reduce_small", in_specs=[vm], out_specs=vm,
        out_shape=jax.ShapeDtypeStruct((R, LANES), F32),
        scratch_shapes=[pltpu.VMEM((8, R, LANES), F32), pltpu.SemaphoreType.DMA((7,)), pltpu.SemaphoreType.DMA((7,))],
        compiler_params=pltpu.CompilerParams(vmem_limit_bytes=VMEM_LIMIT, has_side_effects=True),
    )(v)


def _to_stacked(name, full):
    R, C = full.shape
    if name in ROW_SHARDED:
        return full.reshape(4, R // 4, C)
    return jnp.transpose(full.reshape(R, 4, C // 4), (1, 0, 2))


def _from_stacked(name, st):
    _, r, c = st.shape
    if name in ROW_SHARDED:
        return st.reshape(4 * r, c)
    return jnp.transpose(st, (1, 0, 2)).reshape(r, 4 * c)


def _layer_weights(lands, own, chip):
    w = {}
    for n in BIG:
        pieces = jnp.stack([jnp.where(chip == k, own[n], lands[n][k]) for k in range(4)])
        w[n] = _from_stacked(n, pieces)
    win = w.pop("w_in")
    D = win.shape[0]
    p0, p1, p2, p3 = POOL_DIM, POOL_DIM + Q_LORA_RANK, POOL_DIM + Q_LORA_RANK + KV_LORA_RANK, \
        POOL_DIM + Q_LORA_RANK + KV_LORA_RANK + QK_ROPE_DIM
    w["w_pool"] = win[:, :p0]
    w["w_lat"] = jnp.concatenate([win[:, p0:p3], jnp.zeros((D, LAT_DIM - (p3 - p0)), win.dtype)], axis=1)
    w["w_gate"] = win[:, p3:]
    uq = w["w_uq"].reshape(Q_LORA_RANK, N_HEADS, QK_DIM)
    w["w_uq"] = jnp.concatenate([uq, jnp.zeros((Q_LORA_RANK, N_HEADS, HEAD_PAD - QK_DIM), uq.dtype)],
                                axis=2).reshape(Q_LORA_RANK, N_HEADS * HEAD_PAD)
    return w


def _layer_grads_stacked(dw):
    out = {}
    lat = dw.pop("w_lat")
    dw["w_in"] = jnp.concatenate([dw.pop("w_pool"), lat[:, :Q_LORA_RANK + KV_LORA_RANK + QK_ROPE_DIM],
                                  dw.pop("w_gate")], axis=1)
    dw["w_uq"] = dw["w_uq"].reshape(Q_LORA_RANK, N_HEADS, HEAD_PAD)[:, :, :QK_DIM].reshape(Q_LORA_RANK, N_HEADS * QK_DIM)
    for n in BIG:
        out[n] = _to_stacked(n, dw[n])
    return out


def _rope_tables(positions):
    inv_freq = ROPE_THETA ** (-jnp.arange(0, QK_ROPE_DIM, 2, dtype=F32) / QK_ROPE_DIM)
    ang = positions.astype(F32).reshape(-1)[:, None] * inv_freq
    cos, sin = jnp.cos(ang), jnp.sin(ang)
    z = jnp.zeros((ang.shape[0], LANES - QK_ROPE_DIM), F32)
    return jnp.concatenate([cos, cos, z], axis=1), jnp.concatenate([-sin, sin, z], axis=1)


def _pack_small(vals):
    parts = []
    for n in SMALL:
        f = vals[n].reshape(-1).astype(F32)
        pad = (-f.shape[0]) % (8 * LANES)
        parts.append(jnp.pad(f, (0, pad)))
    return jnp.concatenate(parts).reshape(-1, LANES)


def _unpack_small(packed, like):
    flat = packed.reshape(-1)
    out, off = {}, 0
    for n in SMALL:
        size = like[n].size
        out[n] = flat[off:off + size].reshape(like[n].shape)
        off += size + ((-size) % (8 * LANES))
    return out


def _ffn_fwd(x, g, wu, wd, tag):
    h = rms_fwd(x, g, name=f"{tag}_norm")
    u = mm(h, wu, out_dtype=BF16, name=f"{tag}_up")
    a = swiglu_fwd(u, name=f"{tag}_act")
    y = mm(a, wd, res=x, alpha=0.5, name=f"{tag}_down")
    return y, (x, h, u, a)


def _ffn_bwd(dy, saved, g, wu, wd, tag, dep=None):
    x, h, u, a = saved
    da = mm(dy, wd, tb=True, alpha=0.5, dep=dep, name=f"{tag}_down_dx")
    dwd = mm(a, dy, ta=True, alpha=0.5, out_dtype=BF16, name=f"{tag}_down_dw")
    du = swiglu_bwd(u, da, name=f"{tag}_act_bwd")
    dwu = mm(h, du, ta=True, out_dtype=BF16, name=f"{tag}_up_dw")
    dh = mm(du, wu, tb=True, name=f"{tag}_up_dx")
    dx, dg = rms_bwd(x, g, dh, dy, name=f"{tag}_norm_bwd")
    return dx, dg, dwu, dwd


def _mix_fwd(x, p, w, cs, sn, B):
    h = rms_fwd(x, p["norm_mix"], name="mix_norm")
    lat = mm(h, w["w_lat"], name="mix_lat")
    xp = mm(h, w["w_pool"], name="mix_pool_in")
    gl = mm(h, w["w_gate"], name="mix_gate_in")
    mixed = pool_fwd(xp, p["pool_maps"].astype(BF16), p["pool_scale"], B, name="pool_fwd")
    ba = mm(mixed, w["w_pool_proj"], name="mix_pool_proj")
    qn, kvn, kr = lat_fwd(lat, p["q_latent_norm"], p["kv_latent_norm"], cs, sn, name="lat_fwd")
    q = q_rope(mm(qn, w["w_uq"], name="mix_uq"), cs, sn, transpose=False, name="q_rope")
    kv = mm(kvn, w["w_ukv"], out_dtype=BF16, name="mix_ukv")
    o, lse = attn_fwd(q, kv, kr, B, name="attn_fwd")
    bb = mm(o, w["w_attn_proj"], name="mix_attn_proj")
    merged = gate_fwd(gl, p["b_gate"], ba, bb, name="gate_fwd")
    y = mm(merged, w["w_out"], res=x, name="mix_out")
    return y, (x, h, lat, xp, gl, mixed, ba, qn, kvn, kr, q, kv, o, lse, bb, merged)


def _mix_bwd(dy, saved, p, w, cs, sn, B):
    x, h, lat, xp, gl, mixed, ba, qn, kvn, kr, q, kv, o, lse, bb, merged = saved
    dw, ds = {}, {}
    dm = mm(dy, w["w_out"], tb=True, name="mix_out_dx")
    dw["w_out"] = mm(merged, dy, ta=True, out_dtype=BF16, name="mix_out_dw")
    dba, dbb, dgl, ds["b_gate"] = gate_bwd(dm, gl, p["b_gate"], ba, bb, name="gate_bwd")
    dw["w_attn_proj"] = mm(o, dbb, ta=True, out_dtype=BF16, name="mix_attn_proj_dw")
    do = mm(dbb, w["w_attn_proj"], tb=True, out_dtype=BF16, name="mix_attn_proj_dx")
    dw["w_pool_proj"] = mm(mixed, dba, ta=True, out_dtype=BF16, name="mix_pool_proj_dw")
    dmixed = mm(dba, w["w_pool_proj"], tb=True, name="mix_pool_proj_dx")
    dxp, ds["pool_maps"], ds["pool_scale"] = pool_bwd(xp, dmixed, p["pool_maps"].astype(BF16), p["pool_scale"], B,
                                                      name="pool_bwd")
    dq, dkv, dkr = attn_bwd(q, kv, kr, o, do, lse, B, name="attn_bwd")
    dw["w_ukv"] = mm(kvn, dkv, ta=True, out_dtype=BF16, name="mix_ukv_dw")
    dkvn = mm(dkv, w["w_ukv"], tb=True, name="mix_ukv_dx")
    dqb = q_rope(dq, cs, sn, transpose=True, name="q_rope_bwd")
    dw["w_uq"] = mm(qn, dqb, ta=True, out_dtype=BF16, name="mix_uq_dw")
    dqn = mm(dqb, w["w_uq"], tb=True, name="mix_uq_dx")
    dlat, ds["q_latent_norm"], ds["kv_latent_norm"] = lat_bwd(lat, p["q_latent_norm"], p["kv_latent_norm"], dqn, dkvn,
                                                               dkr, cs, sn, name="lat_bwd")
    dw["w_lat"] = mm(h, dlat, ta=True, out_dtype=BF16, name="mix_lat_dw")
    dw["w_pool"] = mm(h, dxp, ta=True, out_dtype=BF16, name="mix_pool_in_dw")
    dw["w_gate"] = mm(h, dgl, ta=True, out_dtype=BF16, name="mix_gate_in_dw")
    dh = mm(dlat, w["w_lat"], tb=True, name="mix_lat_dx")
    dh = mm(dxp, w["w_pool"], tb=True, res=dh, name="mix_pool_in_dx")
    dh = mm(dgl, w["w_gate"], tb=True, res=dh, name="mix_gate_in_dx")
    dx, ds["norm_mix"] = rms_bwd(x, p["norm_mix"], dh, dy, name="mix_norm_bwd")
    return dx, dw, ds


def kernel(x, positions, norm_ffn1, ffn1_up, ffn1_down, norm_mix, w_in, b_gate, pool_maps, pool_scale, w_pool_proj, q_latent_norm, w_uq, kv_latent_norm, w_ukv, w_attn_proj, w_out, norm_ffn2, ffn2_up, ffn2_down, final_norm, loss_target, m_norm_ffn1, m_ffn1_up, m_ffn1_down, m_norm_mix, m_w_in, m_b_gate, m_pool_maps, m_pool_scale, m_w_pool_proj, m_q_latent_norm, m_w_uq, m_kv_latent_norm, m_w_ukv, m_w_attn_proj, m_w_out, m_norm_ffn2, m_ffn2_up, m_ffn2_down, m_final_norm, v_norm_ffn1, v_ffn1_up, v_ffn1_down, v_norm_mix, v_w_in, v_b_gate, v_pool_maps, v_pool_scale, v_w_pool_proj, v_q_latent_norm, v_w_uq, v_kv_latent_norm, v_w_ukv, v_w_attn_proj, v_w_out, v_norm_ffn2, v_ffn2_up, v_ffn2_down, v_final_norm):
    given = dict(locals())
    B, S, D = x.shape
    T = B * S
    L = norm_ffn1.shape[0]
    W = {n: given[n] for n in WEIGHTS}
    Mo = {n: given["m_" + n] for n in WEIGHTS}
    Vo = {n: given["v_" + n] for n in WEIGHTS}
    core = lax.axis_index("c").astype(jnp.int32)
    chip = (2 * lax.axis_index("x") + lax.axis_index("y")).astype(jnp.int32)

    core_arr = core.reshape(1)
    place = jnp.stack([chip, core])
    nbig = len(BIG)

    own = [{n: W[n][l].astype(BF16) for n in BIG} for l in range(L)]
    ag, tok = [], 0.0
    for l in range(L):
        lands = [jax.ShapeDtypeStruct((4,) + own[l][n].shape, BF16) for n in BIG]
        handle, t = exchange_begin(f"ag_start_{l}", [own[l][n] for n in BIG], lands, ag_plan, 3 * nbig)
        ag.append(handle)
        tok = tok + t[0, 0]
    cs, sn = _rope_tables(positions)

    xs = x.reshape(T, D) + tok
    saved = []
    for l in range(L):
        lands = ag_forward(exchange_end(f"ag_wait_{l}", ag[l], ag_plan, xs), name=f"ag_forward_{l}")
        w = _layer_weights(dict(zip(BIG, lands)), own[l], chip)
        p = {n: W[n][l] for n in SMALL if n != "final_norm"}
        xs, s1 = _ffn_fwd(xs, p["norm_ffn1"], w["ffn1_up"], w["ffn1_down"], "ffn1")
        xs, s2 = _mix_fwd(xs, p, w, cs, sn, B)
        xs, s3 = _ffn_fwd(xs, p["norm_ffn2"], w["ffn2_up"], w["ffn2_down"], "ffn2")
        saved.append((w, p, s1, s2, s3))

    dx, dfinal, loss_tile = loss_head(xs, final_norm, loss_target.reshape(T, D), name="loss_head")
    loss = lax.psum(loss_tile[0, 0], ("x", "y", "c"))

    small_layers, pending, dep = [], [], None
    for l in reversed(range(L)):
        w, p, s1, s2, s3 = saved[l]
        dx, dg2, dwu2, dwd2 = _ffn_bwd(dx, s3, p["norm_ffn2"], w["ffn2_up"], w["ffn2_down"], "ffn2", dep=dep)
        dx, dw, ds = _mix_bwd(dx, s2, p, w, cs, sn, B)
        dx, dg1, dwu1, dwd1 = _ffn_bwd(dx, s1, p["norm_ffn1"], w["ffn1_up"], w["ffn1_down"], "ffn1")
        dw.update(ffn1_up=dwu1, ffn1_down=dwd1, ffn2_up=dwu2, ffn2_down=dwd2)
        ds.update(norm_ffn1=dg1, norm_ffn2=dg2)
        small_layers.append(ds)
        stacked = _layer_grads_stacked(dw)
        parts = [stacked[n] for n in BIG]
        r1 = rs_swap_halves(parts, name=f"rs_swap_{l}")
        sums = [rs_chip_sum(g, a, core_arr, name=f"rs_chip_sum_{n}") for n, g, a in zip(BIG, parts, r1)]
        lands = [jax.ShapeDtypeStruct((3,) + s.shape[1:], BF16) for s in sums]
        handle, dep = exchange_begin(f"rs_start_{l}", sums, lands, rs_plan, 3 * nbig)
        pending.append((l, parts, r1, handle))
    small_layers.reverse()

    acc = {n: lax.empty(W[n].shape, F32) for n in BIG}
    for l, parts, r1, handle in pending:
        r2 = exchange_end(f"rs_wait_{l}", handle, rs_plan, dx)
        for n, g, a, b in zip(BIG, parts, r1, r2):
            acc[n] = rs_final_sum(g, a, b, place, acc[n], l, name=f"rs_final_sum_{n}")
    grads = dict(zip(BIG, rs_join_halves([acc[n] for n in BIG])))

    small = {n: jnp.stack([small_layers[l][n].reshape(W[n].shape[1:]) for l in range(L)]) for n in SMALL
             if n != "final_norm"}
    small["final_norm"] = dfinal.reshape(final_norm.shape)
    grads.update(_unpack_small(all_reduce_small(_pack_small(small)), small))

    delta, new_m, new_v = {}, {}, {}
    for n in BIG:
        sh = W[n].shape
        two = lambda a: a.reshape(sh[0] * sh[1], sh[2])
        d, nm, nv = adamw(two(W[n]), two(grads[n]), two(Mo[n]), two(Vo[n]), name=f"adamw_{n}")
        delta[n], new_m[n], new_v[n] = d.reshape(sh), nm.reshape(sh), nv.reshape(sh)
    d, nm, nv = adamw(_pack_small(W), _pack_small(grads), _pack_small(Mo), _pack_small(Vo), name="adamw_small")
    delta.update(_unpack_small(d, W))
    new_m.update(_unpack_small(nm, W))
    new_v.update(_unpack_small(nv, W))

    return (loss, dx.reshape(B, S, D), *[grads[n] for n in WEIGHTS], *[delta[n] for n in WEIGHTS],
            *[new_m[n] for n in WEIGHTS], *[new_v[n] for n in WEIGHTS])
```

```python
import functools

import jax
import jax.numpy as jnp
from jax import lax
from jax.experimental import pallas as pl
from jax.experimental.pallas import tpu as pltpu

F32 = jnp.float32
BF16 = jnp.bfloat16

N_HEADS = 8
QK_NOPE_DIM = 128
QK_ROPE_DIM = 64
QK_DIM = QK_NOPE_DIM + QK_ROPE_DIM
V_HEAD_DIM = 128
HEAD_PAD = 256
Q_LORA_RANK = 384
KV_LORA_RANK = 256
ROPE_THETA = 10000.0
POOL_WINDOWS = (2, 4, 8, 16)
N_POOL_GROUPS = 4
POOL_GROUP_DIM = 128
POOL_DIM = N_POOL_GROUPS * POOL_GROUP_DIM
LAT_DIM = 768
NORM_EPS = 1e-6
ADAM_LR = 0.001
ADAM_B1 = 0.9
ADAM_B2 = 0.999
ADAM_EPS = 1e-08
ADAM_WD = 0.01
ADAM_STEP = 10
NEG_INF = -1e30
LANES = 128
ATT_BLOCK = 512
VMEM_LIMIT = 48 * 1024 * 1024
MESH = pl.DeviceIdType.MESH
_NT = (((1,), (1,)), ((), ()))
_TN = (((0,), (0,)), ((), ()))

BIG = ("ffn1_up", "ffn1_down", "w_in", "w_pool_proj", "w_uq", "w_ukv", "w_attn_proj", "w_out",
       "ffn2_up", "ffn2_down")
ROW_SHARDED = ("ffn1_down", "w_attn_proj", "w_out", "ffn2_down")
SMALL = ("norm_ffn1", "norm_mix", "b_gate", "pool_maps", "pool_scale", "q_latent_norm",
         "kv_latent_norm", "norm_ffn2", "final_norm")
WEIGHTS = ("norm_ffn1", "ffn1_up", "ffn1_down", "norm_mix", "w_in", "b_gate", "pool_maps", "pool_scale",
           "w_pool_proj", "q_latent_norm", "w_uq", "kv_latent_norm", "w_ukv", "w_attn_proj", "w_out",
           "norm_ffn2", "ffn2_up", "ffn2_down", "final_norm")


def _pick(dim, cands):
    for c in cands:
        if c <= dim and dim % c == 0:
            return c
    return dim


def _cparams(sem=None, **kw):
    if sem is not None:
        kw["dimension_semantics"] = sem
    return pltpu.CompilerParams(vmem_limit_bytes=VMEM_LIMIT, **kw)


def _sigmoid(x):
    return 1.0 / (1.0 + jnp.exp(-x))


MM_TILE_BUDGET = 30 * 1024 * 1024
TILE_SIZES = (1408, 1024, 768, 512, 384, 256, 128)


def _mm_tiles(M, N, K, sa, sb, so, sr):
    tks = [K] if K <= 2816 else [t for t in (2816, 2048, 1408, 1024, 512, 256, 128) if K % t == 0]
    best = None
    for tk in tks:
        for tm in [t for t in TILE_SIZES if M % t == 0] or [M]:
            for tn in [t for t in TILE_SIZES if N % t == 0] or [N]:
                need = 2 * (tm * tk * sa + tk * tn * sb + tm * tn * (so + sr)) + (tm * tn * 4 if tk < K else 0)
                if need <= MM_TILE_BUDGET:
                    score = (tm * tn * tk, tk, tm)
                    if best is None or score > best[0]:
                        best = (score, (tm, tn, tk))
    assert best is not None, (M, N, K)
    return best[1]


def mm(a, b, *, name, ta=False, tb=False, out_dtype=F32, res=None, alpha=1.0, dep=None):
    if ta:
        K, M = a.shape
    else:
        M, K = a.shape
    if tb:
        N, K2 = b.shape
    else:
        K2, N = b.shape
    assert K == K2, (a.shape, b.shape, ta, tb)
    tm, tn, tk = _mm_tiles(M, N, K, a.dtype.itemsize, b.dtype.itemsize, jnp.dtype(out_dtype).itemsize,
                           0 if res is None else res.dtype.itemsize)
    nk = K // tk
    dims = (((0 if ta else 1,), (1 if tb else 0,)), ((), ()))

    def body(*refs):
        a_ref, b_ref = refs[:2]
        r_ref = refs[2] if res is not None else None
        o_ref = refs[-2] if nk > 1 else refs[-1]

        def finish(r):
            if alpha != 1.0:
                r = r * alpha
            if res is not None:
                r = r_ref[...].astype(F32) + r
            o_ref[...] = r.astype(out_dtype)

        part = lax.dot_general(a_ref[...].astype(BF16), b_ref[...].astype(BF16), dims, preferred_element_type=F32)
        if nk == 1:
            finish(part)
            return
        acc = refs[-1]
        k = pl.program_id(2)

        @pl.when(k == 0)
        def _():
            acc[...] = part

        @pl.when(k > 0)
        def _():
            acc[...] += part

        @pl.when(k == nk - 1)
        def _():
            finish(acc[...])

    a_spec = pl.BlockSpec((tk, tm), lambda i, j, k: (k, i)) if ta else pl.BlockSpec((tm, tk), lambda i, j, k: (i, k))
    b_spec = pl.BlockSpec((tn, tk), lambda i, j, k: (j, k)) if tb else pl.BlockSpec((tk, tn), lambda i, j, k: (k, j))
    o_spec = pl.BlockSpec((tm, tn), lambda i, j, k: (i, j))
    in_specs = [a_spec, b_spec]
    args = [a, b]
    if res is not None:
        in_specs.append(o_spec)
        args.append(res)
    if dep is not None:
        in_specs.append(pl.BlockSpec((8, LANES), lambda i, j, k: (0, 0)))
        args.append(dep)
    return pl.pallas_call(
        body, name=name, grid=(M // tm, N // tn, nk), in_specs=in_specs, out_specs=o_spec,
        out_shape=jax.ShapeDtypeStruct((M, N), out_dtype),
        scratch_shapes=[pltpu.VMEM((tm, tn), F32)] if nk > 1 else [],
        compiler_params=_cparams(("parallel", "parallel", "arbitrary")),
    )(*args)


def ffn_up_act(h, wu4, *, name):
    T, D = h.shape
    cq = wu4.shape[2]
    Fh = 2 * cq
    tm = _pick(T, (512, 256, 128))

    def body(h_ref, wg_ref, wu_ref, g_ref, u_ref, a_ref):
        hv = h_ref[...]
        gv = jnp.dot(hv, wg_ref[0], preferred_element_type=F32)
        uv = jnp.dot(hv, wu_ref[0], preferred_element_type=F32)
        g_ref[...] = gv.astype(BF16)
        u_ref[...] = uv.astype(BF16)
        a_ref[...] = (gv * _sigmoid(gv) * uv).astype(BF16)

    tile = pl.BlockSpec((tm, cq), lambda i, j: (i, j))
    sh = jax.ShapeDtypeStruct((T, Fh), BF16)
    return pl.pallas_call(
        body, name=name, grid=(T // tm, 2),
        in_specs=[pl.BlockSpec((tm, D), lambda i, j: (i, 0)), pl.BlockSpec((1, D, cq), lambda i, j: (j, 0, 0)),
                  pl.BlockSpec((1, D, cq), lambda i, j: (2 + j, 0, 0))],
        out_specs=[tile, tile, tile], out_shape=[sh, sh, sh],
        compiler_params=_cparams(("parallel", "parallel")),
    )(h, wu4, wu4)


def ffn_down_dx_act(dy, wd, g, u, *, dep=None, name):
    T, D = dy.shape
    Fh = wd.shape[0]
    cq = Fh // 2
    tm = _pick(T, (512, 256, 128))

    def body(dy_ref, wd_ref, g_ref, u_ref, *rest):
        dg_ref, du_ref = rest[-2:]
        da = 0.5 * lax.dot_general(dy_ref[...].astype(BF16), wd_ref[...], _NT, preferred_element_type=F32)
        gv = g_ref[...].astype(F32)
        uv = u_ref[...].astype(F32)
        s = _sigmoid(gv)
        dg_ref[...] = (da * uv * (s * (1.0 + gv * (1.0 - s)))).astype(BF16)
        du_ref[...] = (da * (gv * s)).astype(BF16)

    tile = pl.BlockSpec((tm, cq), lambda i, j: (i, j))
    sh = jax.ShapeDtypeStruct((T, Fh), BF16)
    in_specs = [pl.BlockSpec((tm, D), lambda i, j: (i, 0)), pl.BlockSpec((cq, D), lambda i, j: (j, 0)), tile, tile]
    args = [dy, wd, g, u]
    if dep is not None:
        in_specs.append(pl.BlockSpec((8, LANES), lambda i, j: (0, 0)))
        args.append(dep)
    return pl.pallas_call(
        body, name=name, grid=(T // tm, 2), in_specs=in_specs, out_specs=[tile, tile], out_shape=[sh, sh],
        compiler_params=_cparams(("parallel", "parallel")),
    )(*args)


def ffn_up_dw(h, dg, du, *, name):
    T, D = h.shape
    cq = dg.shape[1] // 2
    tk = _pick(T, (1024, 512, 256, 128))
    nk = T // tk

    def body(h_ref, dg_ref, du_ref, o_ref, acc):
        p = pl.program_id(0)
        k = pl.program_id(1)

        @pl.when(k == 0)
        def _():
            acc[...] = jnp.zeros_like(acc)

        @pl.when(p < 2)
        def _():
            acc[...] += lax.dot_general(h_ref[...], dg_ref[...], _TN, preferred_element_type=F32)

        @pl.when(p >= 2)
        def _():
            acc[...] += lax.dot_general(h_ref[...], du_ref[...], _TN, preferred_element_type=F32)

        @pl.when(k == nk - 1)
        def _():
            o_ref[0] = acc[...].astype(BF16)

    return pl.pallas_call(
        body, name=name, grid=(4, nk),
        in_specs=[pl.BlockSpec((tk, D), lambda p, k: (k, 0)),
                  pl.BlockSpec((tk, cq), lambda p, k: (k, jnp.minimum(p, 1))),
                  pl.BlockSpec((tk, cq), lambda p, k: (k, jnp.maximum(p - 2, 0)))],
        out_specs=pl.BlockSpec((1, D, cq), lambda p, k: (p, 0, 0)),
        out_shape=jax.ShapeDtypeStruct((4, D, cq), BF16),
        scratch_shapes=[pltpu.VMEM((D, cq), F32)],
        compiler_params=_cparams(("parallel", "arbitrary")),
    )(h, dg, du)


def ffn_up_dx(dg, du, wu4, *, name):
    T = dg.shape[0]
    _, D, cq = wu4.shape
    tm = _pick(T, (512, 256, 128))

    def body(dg_ref, du_ref, wg_ref, wu_ref, o_ref, acc):
        k = pl.program_id(1)
        part = lax.dot_general(dg_ref[...], wg_ref[0], _NT, preferred_element_type=F32)
        part = part + lax.dot_general(du_ref[...], wu_ref[0], _NT, preferred_element_type=F32)

        @pl.when(k == 0)
        def _():
            acc[...] = part

        @pl.when(k == 1)
        def _():
            o_ref[...] = acc[...] + part

    tile = pl.BlockSpec((tm, cq), lambda i, k: (i, k))
    return pl.pallas_call(
        body, name=name, grid=(T // tm, 2),
        in_specs=[tile, tile, pl.BlockSpec((1, D, cq), lambda i, k: (k, 0, 0)),
                  pl.BlockSpec((1, D, cq), lambda i, k: (2 + k, 0, 0))],
        out_specs=pl.BlockSpec((tm, D), lambda i, k: (i, 0)),
        out_shape=jax.ShapeDtypeStruct((T, D), F32),
        scratch_shapes=[pltpu.VMEM((tm, D), F32)],
        compiler_params=_cparams(("parallel", "arbitrary")),
    )(dg, du, wu4, wu4)


def _rows(T, width_bytes):
    cap = max(8, (2 * 1024 * 1024) // width_bytes)
    return _pick(T, tuple(c for c in (1024, 512, 256, 128, 64, 32, 16) if c <= cap))


def rms_fwd(x, g, *, name):
    T, D = x.shape
    tm = _rows(T, D * 4)

    def body(x_ref, g_ref, h_ref):
        xv = x_ref[...]
        r = lax.rsqrt(jnp.mean(xv * xv, axis=-1, keepdims=True) + NORM_EPS)
        h_ref[...] = (xv * r * g_ref[...]).astype(BF16)

    return pl.pallas_call(
        body, name=name, grid=(T // tm,),
        in_specs=[pl.BlockSpec((tm, D), lambda i: (i, 0)), pl.BlockSpec((1, D), lambda i: (0, 0))],
        out_specs=pl.BlockSpec((tm, D), lambda i: (i, 0)),
        out_shape=jax.ShapeDtypeStruct((T, D), BF16),
        compiler_params=_cparams(("parallel",)),
    )(x, g.reshape(1, D))


def _rms_bwd_math(xv, gv, dh):
    r = lax.rsqrt(jnp.mean(xv * xv, axis=-1, keepdims=True) + NORM_EPS)
    xn = xv * r
    dg = jnp.sum(dh * xn, axis=0, keepdims=True)
    dxn = dh * gv
    dx = r * (dxn - xn * jnp.mean(dxn * xn, axis=-1, keepdims=True))
    return dx, dg


def rms_bwd(x, g, dh, dres, *, name):
    T, D = x.shape
    tm = _rows(T, D * 4)

    def body(x_ref, g_ref, dh_ref, dres_ref, dx_ref, dg_ref):
        @pl.when(pl.program_id(0) == 0)
        def _():
            dg_ref[...] = jnp.zeros_like(dg_ref)

        dx, dg = _rms_bwd_math(x_ref[...], g_ref[...], dh_ref[...].astype(F32))
        dx_ref[...] = dres_ref[...] + dx
        dg_ref[...] += dg

    row = pl.BlockSpec((tm, D), lambda i: (i, 0))
    vec = pl.BlockSpec((1, D), lambda i: (0, 0))
    return pl.pallas_call(
        body, name=name, grid=(T // tm,), in_specs=[row, vec, row, row], out_specs=[row, vec],
        out_shape=[jax.ShapeDtypeStruct((T, D), F32), jax.ShapeDtypeStruct((1, D), F32)],
        compiler_params=_cparams(("arbitrary",)),
    )(x, g.reshape(1, D), dh, dres)


def _rope(xv, cv, sv):
    half = QK_ROPE_DIM // 2
    lane = lax.broadcasted_iota(jnp.int32, xv.shape, 1)
    swapped = jnp.where(lane < half, pltpu.roll(xv, LANES - half, 1), pltpu.roll(xv, half, 1))
    return xv * cv + swapped * sv


def _rope_t(dy, cv, sv):
    half = QK_ROPE_DIM // 2
    ds = dy * sv
    lane = lax.broadcasted_iota(jnp.int32, dy.shape, 1)
    swapped = jnp.where(lane < half, pltpu.roll(ds, LANES - half, 1), pltpu.roll(ds, half, 1))
    return dy * cv + swapped


def lat_fwd(lat, qn_w, kvn_w, cs, sn, *, name):
    T = lat.shape[0]
    tm = _rows(T, LAT_DIM * 4)
    kv0 = Q_LORA_RANK
    kr0 = Q_LORA_RANK + KV_LORA_RANK

    def body(lat_ref, qw_ref, kw_ref, c_ref, s_ref, qn_ref, kvn_ref, kr_ref):
        ql = lat_ref[:, :kv0]
        r = lax.rsqrt(jnp.mean(ql * ql, axis=-1, keepdims=True) + NORM_EPS)
        qn_ref[...] = (ql * r * qw_ref[...]).astype(BF16)
        kl = lat_ref[:, kv0:kr0]
        r = lax.rsqrt(jnp.mean(kl * kl, axis=-1, keepdims=True) + NORM_EPS)
        kvn_ref[...] = (kl * r * kw_ref[...]).astype(BF16)
        kr_ref[...] = _rope(lat_ref[:, kr0:], c_ref[...], s_ref[...]).astype(BF16)

    return pl.pallas_call(
        body, name=name, grid=(T // tm,),
        in_specs=[pl.BlockSpec((tm, LAT_DIM), lambda i: (i, 0)),
                  pl.BlockSpec((1, Q_LORA_RANK), lambda i: (0, 0)),
                  pl.BlockSpec((1, KV_LORA_RANK), lambda i: (0, 0)),
                  pl.BlockSpec((tm, LANES), lambda i: (i, 0)), pl.BlockSpec((tm, LANES), lambda i: (i, 0))],
        out_specs=[pl.BlockSpec((tm, Q_LORA_RANK), lambda i: (i, 0)),
                   pl.BlockSpec((tm, KV_LORA_RANK), lambda i: (i, 0)),
                   pl.BlockSpec((tm, LANES), lambda i: (i, 0))],
        out_shape=[jax.ShapeDtypeStruct((T, Q_LORA_RANK), BF16), jax.ShapeDtypeStruct((T, KV_LORA_RANK), BF16),
                   jax.ShapeDtypeStruct((T, LANES), BF16)],
        compiler_params=_cparams(("parallel",)),
    )(lat, qn_w.reshape(1, -1), kvn_w.reshape(1, -1), cs, sn)


def lat_bwd(lat, qn_w, kvn_w, dqn, dkvn, dkr, cs, sn, *, name):
    T = lat.shape[0]
    tm = _rows(T, LAT_DIM * 4)
    kv0 = Q_LORA_RANK
    kr0 = Q_LORA_RANK + KV_LORA_RANK

    def body(lat_ref, qw_ref, kw_ref, dqn_ref, dkvn_ref, dkr_ref, c_ref, s_ref, dlat_ref, dqw_ref, dkw_ref):
        @pl.when(pl.program_id(0) == 0)
        def _():
            dqw_ref[...] = jnp.zeros_like(dqw_ref)
            dkw_ref[...] = jnp.zeros_like(dkw_ref)

        dx, dg = _rms_bwd_math(lat_ref[:, :kv0], qw_ref[...], dqn_ref[...])
        dlat_ref[:, :kv0] = dx.astype(BF16)
        dqw_ref[...] += dg
        dx, dg = _rms_bwd_math(lat_ref[:, kv0:kr0], kw_ref[...], dkvn_ref[...])
        dlat_ref[:, kv0:kr0] = dx.astype(BF16)
        dkw_ref[...] += dg
        dlat_ref[:, kr0:] = _rope_t(dkr_ref[...], c_ref[...], s_ref[...]).astype(BF16)

    row = lambda w: pl.BlockSpec((tm, w), lambda i: (i, 0))
    vec = lambda w: pl.BlockSpec((1, w), lambda i: (0, 0))
    return pl.pallas_call(
        body, name=name, grid=(T // tm,),
        in_specs=[row(LAT_DIM), vec(Q_LORA_RANK), vec(KV_LORA_RANK), row(Q_LORA_RANK), row(KV_LORA_RANK),
                  row(LANES), row(LANES), row(LANES)],
        out_specs=[row(LAT_DIM), vec(Q_LORA_RANK), vec(KV_LORA_RANK)],
        out_shape=[jax.ShapeDtypeStruct((T, LAT_DIM), BF16), jax.ShapeDtypeStruct((1, Q_LORA_RANK), F32),
                   jax.ShapeDtypeStruct((1, KV_LORA_RANK), F32)],
        compiler_params=_cparams(("arbitrary",)),
    )(lat, qn_w.reshape(1, -1), kvn_w.reshape(1, -1), dqn, dkvn, dkr, cs, sn)


def q_rope(q, cs, sn, *, transpose, name):
    T, W = q.shape
    tm = _rows(T, W * 4)
    fn = _rope_t if transpose else _rope
    scale = QK_DIM ** -0.5

    def body(q_ref, c_ref, s_ref, o_ref):
        cv = c_ref[...] * scale
        sv = s_ref[...] * scale
        for h in range(N_HEADS):
            lo = h * HEAD_PAD
            o_ref[:, lo:lo + QK_NOPE_DIM] = (q_ref[:, lo:lo + QK_NOPE_DIM] * scale).astype(BF16)
            o_ref[:, lo + QK_NOPE_DIM:lo + HEAD_PAD] = fn(q_ref[:, lo + QK_NOPE_DIM:lo + HEAD_PAD], cv, sv).astype(BF16)

    return pl.pallas_call(
        body, name=name, grid=(T // tm,),
        in_specs=[pl.BlockSpec((tm, W), lambda i: (i, 0)), pl.BlockSpec((tm, LANES), lambda i: (i, 0)),
                  pl.BlockSpec((tm, LANES), lambda i: (i, 0))],
        out_specs=pl.BlockSpec((tm, W), lambda i: (i, 0)),
        out_shape=jax.ShapeDtypeStruct((T, W), BF16),
        compiler_params=_cparams(("parallel",)),
    )(q, cs, sn)


def gate_fwd(gl, bg, ba, bb, *, name):
    T, D2 = gl.shape
    D = D2 // 2
    tm = _rows(T, D2 * 4)

    def body(gl_ref, bg_ref, ba_ref, bb_ref, m_ref):
        ga = _sigmoid(gl_ref[:, :D] + bg_ref[:, :D])
        gb = _sigmoid(gl_ref[:, D:] + bg_ref[:, D:])
        m_ref[...] = (ga * ba_ref[...] + gb * bb_ref[...]).astype(BF16)

    row = lambda w: pl.BlockSpec((tm, w), lambda i: (i, 0))
    return pl.pallas_call(
        body, name=name, grid=(T // tm,),
        in_specs=[row(D2), pl.BlockSpec((1, D2), lambda i: (0, 0)), row(D), row(D)],
        out_specs=row(D), out_shape=jax.ShapeDtypeStruct((T, D), BF16),
        compiler_params=_cparams(("parallel",)),
    )(gl, bg.reshape(1, D2), ba, bb)


def gate_bwd(dm, gl, bg, ba, bb, *, name):
    T, D2 = gl.shape
    D = D2 // 2
    tm = _rows(T, D2 * 4)

    def body(dm_ref, gl_ref, bg_ref, ba_ref, bb_ref, dba_ref, dbb_ref, dgl_ref, dbg_ref):
        @pl.when(pl.program_id(0) == 0)
        def _():
            dbg_ref[...] = jnp.zeros_like(dbg_ref)

        dmv = dm_ref[...]
        ga = _sigmoid(gl_ref[:, :D] + bg_ref[:, :D])
        gb = _sigmoid(gl_ref[:, D:] + bg_ref[:, D:])
        dba_ref[...] = (dmv * ga).astype(BF16)
        dbb_ref[...] = (dmv * gb).astype(BF16)
        dla = dmv * ba_ref[...] * ga * (1.0 - ga)
        dlb = dmv * bb_ref[...] * gb * (1.0 - gb)
        dgl_ref[:, :D] = dla.astype(BF16)
        dgl_ref[:, D:] = dlb.astype(BF16)
        dbg_ref[:, :D] += jnp.sum(dla, axis=0, keepdims=True)
        dbg_ref[:, D:] += jnp.sum(dlb, axis=0, keepdims=True)

    row = lambda w: pl.BlockSpec((tm, w), lambda i: (i, 0))
    vec = pl.BlockSpec((1, D2), lambda i: (0, 0))
    return pl.pallas_call(
        body, name=name, grid=(T // tm,),
        in_specs=[row(D), row(D2), vec, row(D), row(D)],
        out_specs=[row(D), row(D), row(D2), vec],
        out_shape=[jax.ShapeDtypeStruct((T, D), BF16), jax.ShapeDtypeStruct((T, D), BF16),
                   jax.ShapeDtypeStruct((T, D2), BF16), jax.ShapeDtypeStruct((1, D2), F32)],
        compiler_params=_cparams(("arbitrary",)),
    )(dm, gl, bg.reshape(1, D2), ba, bb)


def loss_head(x, gf, tgt, *, name):
    T, D = x.shape
    tm = _rows(T, D * 4)

    def body(x_ref, g_ref, t_ref, dx_ref, dg_ref, loss_ref):
        @pl.when(pl.program_id(0) == 0)
        def _():
            dg_ref[...] = jnp.zeros_like(dg_ref)
            loss_ref[...] = jnp.zeros_like(loss_ref)

        xv = x_ref[...]
        gv = g_ref[...]
        r = lax.rsqrt(jnp.mean(xv * xv, axis=-1, keepdims=True) + NORM_EPS)
        xn = xv * r
        err = xn * gv - t_ref[...]
        loss_ref[...] += 0.5 * jnp.sum(jnp.mean(err * err, axis=-1, keepdims=True))
        dy = err * (1.0 / D)
        dg_ref[...] += jnp.sum(dy * xn, axis=0, keepdims=True)
        dxn = dy * gv
        dx_ref[...] = r * (dxn - xn * jnp.mean(dxn * xn, axis=-1, keepdims=True))

    row = pl.BlockSpec((tm, D), lambda i: (i, 0))
    vec = pl.BlockSpec((1, D), lambda i: (0, 0))
    return pl.pallas_call(
        body, name=name, grid=(T // tm,), in_specs=[row, vec, row],
        out_specs=[row, vec, pl.BlockSpec((8, LANES), lambda i: (0, 0))],
        out_shape=[jax.ShapeDtypeStruct((T, D), F32), jax.ShapeDtypeStruct((1, D), F32),
                   jax.ShapeDtypeStruct((8, LANES), F32)],
        compiler_params=_cparams(("arbitrary",)),
    )(x, gf.reshape(1, D), tgt)


def _shift_rows(s, k, down):
    n = s.shape[0]
    t = lax.broadcasted_iota(jnp.int32, s.shape, 0)
    if down:
        return jnp.where(t >= k, pltpu.roll(s, k, 0), 0.0)
    return jnp.where(t < n - k, pltpu.roll(s, n - k, 0), 0.0)


def _window_sum(s, w, down):
    k = 1
    while k < w:
        s = s + _shift_rows(s, k, down)
        k *= 2
    return s


def _pool_count(shape, w):
    t = lax.broadcasted_iota(jnp.int32, shape, 0)
    return jnp.minimum(t + 1, w).astype(F32)


def pool_fwd(xp, maps, scale, B, *, name):
    T, P = xp.shape
    S = T // B
    G = POOL_GROUP_DIM

    def body(x_ref, m_ref, sc_ref, o_ref):
        for g, w in enumerate(POOL_WINDOWS):
            xg = x_ref[:, g * G:(g + 1) * G]
            pooled = _window_sum(xg, w, True) / _pool_count(xg.shape, w) - xg
            mixed = jnp.dot(pooled.astype(BF16), m_ref[g], preferred_element_type=F32)
            o_ref[:, g * G:(g + 1) * G] = (mixed * sc_ref[:, g * G:(g + 1) * G]).astype(BF16)

    return pl.pallas_call(
        body, name=name, grid=(B,),
        in_specs=[pl.BlockSpec((S, P), lambda b: (b, 0)), pl.BlockSpec((N_POOL_GROUPS, G, G), lambda b: (0, 0, 0)),
                  pl.BlockSpec((1, P), lambda b: (0, 0))],
        out_specs=pl.BlockSpec((S, P), lambda b: (b, 0)),
        out_shape=jax.ShapeDtypeStruct((T, P), BF16),
        compiler_params=_cparams(("parallel",)),
    )(xp, maps, scale.reshape(1, P))


def pool_bwd(xp, dmixed, maps, scale, B, *, name):
    T, P = xp.shape
    S = T // B
    G = POOL_GROUP_DIM

    def body(x_ref, dm_ref, m_ref, sc_ref, dx_ref, dmaps_ref, dsc_ref):
        @pl.when(pl.program_id(0) == 0)
        def _():
            dmaps_ref[...] = jnp.zeros_like(dmaps_ref)
            dsc_ref[...] = jnp.zeros_like(dsc_ref)

        for g, w in enumerate(POOL_WINDOWS):
            cols = slice(g * G, (g + 1) * G)
            xg = x_ref[:, cols]
            cnt = _pool_count(xg.shape, w)
            pooled = (_window_sum(xg, w, True) / cnt - xg).astype(BF16)
            mixed = jnp.dot(pooled, m_ref[g], preferred_element_type=F32)
            dmx = dm_ref[:, cols]
            dsc_ref[:, cols] += jnp.sum(dmx * mixed, axis=0, keepdims=True)
            dmp = (dmx * sc_ref[:, cols]).astype(BF16)
            dmaps_ref[g] += lax.dot_general(pooled, dmp, (((0,), (0,)), ((), ())), preferred_element_type=F32)
            dpooled = lax.dot_general(dmp, m_ref[g], (((1,), (1,)), ((), ())), preferred_element_type=F32)
            dx_ref[:, cols] = (_window_sum(dpooled / cnt, w, False) - dpooled).astype(BF16)

    blk = pl.BlockSpec((S, P), lambda b: (b, 0))
    mp = pl.BlockSpec((N_POOL_GROUPS, G, G), lambda b: (0, 0, 0))
    vec = pl.BlockSpec((1, P), lambda b: (0, 0))
    return pl.pallas_call(
        body, name=name, grid=(B,), in_specs=[blk, blk, mp, vec], out_specs=[blk, mp, vec],
        out_shape=[jax.ShapeDtypeStruct((T, P), BF16), jax.ShapeDtypeStruct((N_POOL_GROUPS, G, G), F32),
                   jax.ShapeDtypeStruct((1, P), F32)],
        compiler_params=_cparams(("arbitrary",)),
    )(xp, dmixed, maps, scale.reshape(1, P))


def _keys(kv_ref, kr_ref, rows):
    return jnp.concatenate([kv_ref[rows, :QK_NOPE_DIM], kr_ref[rows, :]], axis=1)


def _causal(s):
    row = lax.broadcasted_iota(jnp.int32, s.shape, 0)
    col = lax.broadcasted_iota(jnp.int32, s.shape, 1)
    return jnp.where(row >= col, s, NEG_INF)


def attn_fwd(q, kv, kr, B, *, name):
    T = q.shape[0]
    S = T // B
    blk = min(ATT_BLOCK, S)
    nb = S // blk
    H = N_HEADS

    def body(q_ref, kv_ref, kr_ref, o_ref, lse_ref):
        for qi in range(nb):
            rows = slice(qi * blk, (qi + 1) * blk)
            qb = q_ref[rows, :]
            sd = _causal(lax.dot_general(qb, _keys(kv_ref, kr_ref, rows), _NT, preferred_element_type=F32))
            m = jnp.max(sd, axis=-1, keepdims=True)
            if qi > 0:
                prev = slice(0, qi * blk)
                sp = lax.dot_general(qb, _keys(kv_ref, kr_ref, prev), _NT, preferred_element_type=F32)
                m = jnp.maximum(m, jnp.max(sp, axis=-1, keepdims=True))
            pd = jnp.exp(sd - m)
            l = jnp.sum(pd, axis=-1, keepdims=True)
            acc = jnp.dot(pd.astype(BF16), kv_ref[rows, QK_NOPE_DIM:], preferred_element_type=F32)
            if qi > 0:
                pp = jnp.exp(sp - m)
                l = l + jnp.sum(pp, axis=-1, keepdims=True)
                acc = acc + jnp.dot(pp.astype(BF16), kv_ref[prev, QK_NOPE_DIM:], preferred_element_type=F32)
            o_ref[rows, :] = (acc / l).astype(BF16)
            lse_ref[0, rows, :] = m + jnp.log(l)

    return pl.pallas_call(
        body, name=name, grid=(B, H),
        in_specs=[pl.BlockSpec((S, HEAD_PAD), lambda b, h: (b, h)), pl.BlockSpec((S, HEAD_PAD), lambda b, h: (b, h)),
                  pl.BlockSpec((S, LANES), lambda b, h: (b, 0))],
        out_specs=[pl.BlockSpec((S, V_HEAD_DIM), lambda b, h: (b, h)), pl.BlockSpec((1, S, 1), lambda b, h: (h, b, 0))],
        out_shape=[jax.ShapeDtypeStruct((T, H * V_HEAD_DIM), BF16), jax.ShapeDtypeStruct((H, T, 1), F32)],
        compiler_params=_cparams(("parallel", "parallel")),
    )(q, kv, kr)


def attn_bwd(q, kv, kr, o, do, lse, B, *, name):
    T = q.shape[0]
    S = T // B
    blk = min(ATT_BLOCK, S)
    nb = S // blk
    H = N_HEADS

    def body(q_ref, kv_ref, kr_ref, o_ref, do_ref, lse_ref, dq_ref, dkv_ref, dkr_ref, dk_s, dv_s):
        dk_s[...] = jnp.zeros_like(dk_s)
        dv_s[...] = jnp.zeros_like(dv_s)

        @pl.when(pl.program_id(1) == 0)
        def _():
            dkr_ref[...] = jnp.zeros_like(dkr_ref)

        for qi in range(nb):
            rows = slice(qi * blk, (qi + 1) * blk)
            qb = q_ref[rows, :]
            dob = do_ref[rows, :]
            delta = jnp.sum(dob.astype(F32) * o_ref[rows, :].astype(F32), axis=-1, keepdims=True)
            lse_b = lse_ref[0, rows, :]

            def part(ks, diagonal):
                k = _keys(kv_ref, kr_ref, ks)
                s = lax.dot_general(qb, k, _NT, preferred_element_type=F32)
                if diagonal:
                    s = _causal(s)
                p = jnp.exp(s - lse_b)
                dp = lax.dot_general(dob, kv_ref[ks, QK_NOPE_DIM:], _NT, preferred_element_type=F32)
                ds = (p * (dp - delta)).astype(BF16)
                dv_s[ks, :] += lax.dot_general(p.astype(BF16), dob, _TN, preferred_element_type=F32)
                dk_s[ks, :] += lax.dot_general(ds, qb, _TN, preferred_element_type=F32)
                return jnp.dot(ds, k, preferred_element_type=F32)

            dq = part(rows, True)
            if qi > 0:
                dq = dq + part(slice(0, qi * blk), False)
            dq_ref[rows, :] = dq

        dkv_ref[:, :QK_NOPE_DIM] = dk_s[:, :QK_NOPE_DIM].astype(BF16)
        dkv_ref[:, QK_NOPE_DIM:] = dv_s[...].astype(BF16)
        dkr_ref[...] += dk_s[:, QK_NOPE_DIM:]

    head = lambda w: pl.BlockSpec((S, w), lambda b, h: (b, h))
    shared = pl.BlockSpec((S, LANES), lambda b, h: (b, 0))
    return pl.pallas_call(
        body, name=name, grid=(B, H),
        in_specs=[head(HEAD_PAD), head(HEAD_PAD), shared, head(V_HEAD_DIM), head(V_HEAD_DIM),
                  pl.BlockSpec((1, S, 1), lambda b, h: (h, b, 0))],
        out_specs=[head(HEAD_PAD), head(HEAD_PAD), shared],
        out_shape=[jax.ShapeDtypeStruct((T, H * HEAD_PAD), F32), jax.ShapeDtypeStruct((T, H * HEAD_PAD), BF16),
                   jax.ShapeDtypeStruct((T, LANES), F32)],
        scratch_shapes=[pltpu.VMEM((S, HEAD_PAD), F32), pltpu.VMEM((S, V_HEAD_DIM), F32)],
        compiler_params=_cparams(("parallel", "arbitrary")),
    )(q, kv, kr, o, do, lse)


def adamw(w, g, m, v, *, name):
    R, C = w.shape
    cap = max(8, (1024 * 1024) // (C * 4))
    tr = _pick(R, tuple(c for c in (1024, 512, 256, 128, 64, 32, 16, 8) if c <= cap))
    c1 = 1.0 - ADAM_B1 ** ADAM_STEP
    c2 = 1.0 - ADAM_B2 ** ADAM_STEP

    def body(w_ref, g_ref, m_ref, v_ref, d_ref, nm_ref, nv_ref):
        gv = g_ref[...]
        mv = ADAM_B1 * m_ref[...] + (1.0 - ADAM_B1) * gv
        vv = ADAM_B2 * v_ref[...] + (1.0 - ADAM_B2) * (gv * gv)
        nm_ref[...] = mv
        nv_ref[...] = vv
        d_ref[...] = -ADAM_LR * ((mv / c1) / (jnp.sqrt(vv / c2) + ADAM_EPS) + ADAM_WD * w_ref[...])

    blk = pl.BlockSpec((tr, C), lambda i: (i, 0))
    sh = jax.ShapeDtypeStruct((R, C), F32)
    return pl.pallas_call(
        body, name=name, grid=(R // tr,), in_specs=[blk] * 4, out_specs=[blk] * 3, out_shape=[sh] * 3,
        compiler_params=_cparams(("parallel",)),
    )(w, g, m, v)


ANY = pl.BlockSpec(memory_space=pl.ANY)


def _place():
    x, y, c = lax.axis_index("x"), lax.axis_index("y"), lax.axis_index("c")
    others = [(1 - x, y), (x, 1 - y), (1 - x, 1 - y)]
    return x, y, c, others


def _remote(src, dst, ssem, rsem, dev):
    return pltpu.make_async_remote_copy(src_ref=src, dst_ref=dst, send_sem=ssem, recv_sem=rsem,
                                        device_id=dev, device_id_type=MESH)


def _half(ref_rows, c):
    hr = ref_rows // 2
    return pl.ds(pl.multiple_of(c * hr, 16), hr)


HBM = pl.BlockSpec(memory_space=pltpu.HBM)
SEMS = pl.BlockSpec(memory_space=pltpu.SEMAPHORE)
EFFECT = pltpu.SideEffectType.DATAFLOW_SIDE_EFFECTING


def exchange_begin(name, srcs, land_shapes, plan, ncopies):
    ns, nl = len(srcs), len(land_shapes)

    def body(*refs):
        ssem, rsem = refs[ns + nl], refs[ns + nl + 1]
        for k, (s, d, dev) in enumerate(plan(refs[:ns], refs[ns:ns + nl])):
            _remote(s, d, ssem.at[k], rsem.at[k], dev).start()
        refs[-1][...] = jnp.zeros_like(refs[-1])

    bufs = [pltpu.HBM(s.shape, s.dtype) for s in srcs] + [pltpu.HBM(s.shape, s.dtype) for s in land_shapes]
    args = [pltpu.with_memory_space_constraint(s, pltpu.HBM) for s in srcs]
    args += [pltpu.with_memory_space_constraint(lax.empty(s.shape, s.dtype), pltpu.HBM) for s in land_shapes]
    out = pl.pallas_call(
        body, name=name,
        out_shape=(pltpu.SemaphoreType.DMA((ncopies,)), pltpu.SemaphoreType.DMA((ncopies,)), *bufs,
                   jax.ShapeDtypeStruct((8, LANES), F32)),
        in_specs=[HBM] * (ns + nl),
        out_specs=(SEMS, SEMS, *([HBM] * (ns + nl)), pl.BlockSpec(memory_space=pltpu.VMEM)),
        input_output_aliases={i: 2 + i for i in range(ns + nl)},
        compiler_params=pltpu.CompilerParams(has_side_effects=EFFECT),
    )(*args)
    return (out[0], out[1], out[2:2 + ns], out[2 + ns:2 + ns + nl]), out[-1]


def exchange_end(name, handle, plan, after):
    ssem, rsem, srcs, lands = handle
    ns, nl = len(srcs), len(lands)

    def body(*refs):
        ssem_ref, rsem_ref = refs[ns + nl], refs[ns + nl + 1]
        for k, (s, d, dev) in enumerate(plan(refs[:ns], refs[ns:ns + nl])):
            cp = _remote(s, d, ssem_ref.at[k], rsem_ref.at[k], dev)
            cp.wait_send()
            cp.wait_recv()

    out = pl.pallas_call(
        body, name=name,
        out_shape=tuple(pltpu.HBM(s.shape, s.dtype) for s in (*srcs, *lands)),
        in_specs=[HBM] * (ns + nl) + [SEMS, SEMS, ANY], out_specs=tuple([HBM] * (ns + nl)),
        input_output_aliases={i: i for i in range(ns + nl)},
        compiler_params=pltpu.CompilerParams(has_side_effects=EFFECT),
    )(*srcs, *lands, ssem, rsem, after)
    return list(out[:ns]), list(out[ns:])


def ag_plan(src_refs, land_refs):
    x, y, c, others = _place()
    plan = []
    for s, d in zip(src_refs, land_refs):
        mine = _half(s.shape[0], c)
        for ox, oy in others:
            plan.append((s.at[mine, :], d.at[2 * x + y, mine, :], (ox, oy, c)))
    return plan


def ag_forward(lands, *, name):
    n = len(lands)

    def body(*refs):
        ins, outs = refs[:n], refs[n:2 * n]
        ssem, rsem = refs[2 * n:]
        x, y, c, others = _place()
        sent = []
        for i in range(n):
            mine = _half(ins[i].shape[1], c)
            for j, (ox, oy) in enumerate(others):
                cp = _remote(ins[i].at[2 * ox + oy, mine, :], outs[i].at[2 * ox + oy, mine, :], ssem.at[3 * i + j],
                             rsem.at[3 * i + j], (x, y, 1 - c))
                cp.start()
                sent.append(cp)
        for cp in sent:
            cp.wait()

    return pl.pallas_call(
        body, name=name, in_specs=[ANY] * n, out_specs=[ANY] * n,
        out_shape=[jax.ShapeDtypeStruct(a.shape, a.dtype) for a in lands],
        input_output_aliases={i: i for i in range(n)},
        scratch_shapes=[pltpu.SemaphoreType.DMA((3 * n,)), pltpu.SemaphoreType.DMA((3 * n,))],
        compiler_params=pltpu.CompilerParams(has_side_effects=True),
    )(*lands)


def place_own(lands, own, chip, *, name):
    n = len(lands)
    steps = 4

    def body(chip_ref, *refs):
        for i in range(n):
            refs[2 * n + i][0] = refs[i][...]

    in_specs = [pl.BlockSpec((o.shape[0] // steps, o.shape[1]), lambda t, q: (t, 0)) for o in own] + [ANY] * n
    out_specs = [pl.BlockSpec((1, o.shape[0] // steps, o.shape[1]), lambda t, q: (q[0], t, 0)) for o in own]
    return pl.pallas_call(
        body, name=name,
        grid_spec=pltpu.PrefetchScalarGridSpec(num_scalar_prefetch=1, grid=(steps,), in_specs=in_specs,
                                               out_specs=out_specs),
        out_shape=[jax.ShapeDtypeStruct(a.shape, a.dtype) for a in lands],
        input_output_aliases={1 + n + i: i for i in range(n)},
        compiler_params=_cparams(("parallel",)),
    )(chip, *own, *lands)


def rs_swap_halves(grads, *, name):
    n = len(grads)

    def body(*refs):
        ins, outs = refs[:n], refs[n:2 * n]
        ssem, rsem = refs[2 * n:]
        x, y, c, _ = _place()
        cps = []
        for i in range(n):
            theirs = _half(ins[i].shape[1], 1 - c)
            cp = _remote(ins[i].at[:, theirs, :], outs[i], ssem.at[i], rsem.at[i], (x, y, 1 - c))
            cp.start()
            cps.append(cp)
        for cp in cps:
            cp.wait()

    return pl.pallas_call(
        body, name=name, in_specs=[ANY] * n, out_specs=[ANY] * n,
        out_shape=[jax.ShapeDtypeStruct((4, g.shape[1] // 2, g.shape[2]), g.dtype) for g in grads],
        scratch_shapes=[pltpu.SemaphoreType.DMA((n,)), pltpu.SemaphoreType.DMA((n,))],
        compiler_params=pltpu.CompilerParams(has_side_effects=True),
    )(*grads)


def rs_chip_sum(g, r1, core, *, name):
    _, r, cdim = g.shape
    hr = r // 2

    def body(c_ref, g_ref, r1_ref, o_ref):
        o_ref[...] = (g_ref[...].astype(F32) + r1_ref[...].astype(F32)).astype(BF16)

    return pl.pallas_call(
        body, name=name,
        grid_spec=pltpu.PrefetchScalarGridSpec(
            num_scalar_prefetch=1, grid=(4,),
            in_specs=[pl.BlockSpec((1, hr, cdim), lambda qq, c_ref: (qq, c_ref[0], 0)),
                      pl.BlockSpec((1, hr, cdim), lambda qq, c_ref: (qq, 0, 0))],
            out_specs=pl.BlockSpec((1, hr, cdim), lambda qq, c_ref: (qq, 0, 0))),
        out_shape=jax.ShapeDtypeStruct((4, hr, cdim), BF16),
        compiler_params=_cparams(("parallel",)),
    )(core, g, r1)


def rs_plan(src_refs, land_refs):
    x, y, c, others = _place()
    plan = []
    for s, d in zip(src_refs, land_refs):
        for j, (ox, oy) in enumerate(others):
            plan.append((s.at[2 * ox + oy], d.at[j], (ox, oy, c)))
    return plan


def rs_final_sum(g, r1, r2, place, acc, l, *, name):
    _, r, cdim = g.shape
    hr = r // 2
    ch = hr // 2

    def body(p_ref, g_ref, r1_ref, a_ref, b_ref, d_ref, acc_in, o_ref):
        s = g_ref[...].astype(F32) + r1_ref[...].astype(F32)
        s = s + a_ref[...].astype(F32)
        s = s + b_ref[...].astype(F32)
        o_ref[...] = s + d_ref[...].astype(F32)

    other = lambda j: pl.BlockSpec((1, ch, cdim), lambda t, p_ref: (j, t, 0))
    return pl.pallas_call(
        body, name=name,
        grid_spec=pltpu.PrefetchScalarGridSpec(
            num_scalar_prefetch=1, grid=(2,),
            in_specs=[pl.BlockSpec((1, ch, cdim), lambda t, p_ref: (p_ref[0], 2 * p_ref[1] + t, 0)),
                      pl.BlockSpec((1, ch, cdim), lambda t, p_ref: (p_ref[0], t, 0)),
                      other(0), other(1), other(2), ANY],
            out_specs=pl.BlockSpec((1, ch, cdim), lambda t, p_ref: (l, 2 * p_ref[1] + t, 0))),
        out_shape=jax.ShapeDtypeStruct(acc.shape, F32),
        input_output_aliases={6: 0},
        compiler_params=_cparams(("parallel",)),
    )(place, g, r1, r2, r2, r2, acc)


def rs_join_halves(grads):
    n = len(grads)

    def body(*refs):
        ins, outs = refs[:n], refs[n:2 * n]
        ssem, rsem = refs[2 * n:]
        x, y, c, _ = _place()
        cps = []
        for i in range(n):
            mine = _half(ins[i].shape[1], c)
            cp = _remote(ins[i].at[:, mine, :], outs[i].at[:, mine, :], ssem.at[i], rsem.at[i], (x, y, 1 - c))
            cp.start()
            cps.append(cp)
        for cp in cps:
            cp.wait()

    return pl.pallas_call(
        body, name="rs_join_halves", in_specs=[ANY] * n, out_specs=[ANY] * n,
        out_shape=[jax.ShapeDtypeStruct(g.shape, g.dtype) for g in grads],
        input_output_aliases={i: i for i in range(n)},
        scratch_shapes=[pltpu.SemaphoreType.DMA((n,)), pltpu.SemaphoreType.DMA((n,))],
        compiler_params=pltpu.CompilerParams(has_side_effects=True),
    )(*grads)


def all_reduce_small(v):
    R = v.shape[0]

    def body(v_ref, o_ref, buf, ssem, rsem):
        x, y, c, _ = _place()
        me = 4 * x + 2 * y + c
        buf[me] = v_ref[...]
        cps = []
        for k in range(1, 8):
            fx, fy, fc = (k >> 2) & 1, (k >> 1) & 1, k & 1
            px = jnp.where(fx == 1, 1 - x, x)
            py = jnp.where(fy == 1, 1 - y, y)
            pc = jnp.where(fc == 1, 1 - c, c)
            cp = _remote(v_ref, buf.at[me], ssem.at[k - 1], rsem.at[k - 1], (px, py, pc))
            cp.start()
            cps.append(cp)
        for cp in cps:
            cp.wait()
        acc = buf[0]
        for d in range(1, 8):
            acc = acc + buf[d]
        o_ref[...] = acc

    vm = pl.BlockSpec(memory_space=pltpu.VMEM)
    return pl.pallas_call(
        body, name="all_reduce_small", in_specs=[vm], out_specs=vm,
        out_shape=jax.ShapeDtypeStruct((R, LANES), F32),
        scratch_shapes=[pltpu.VMEM((8, R, LANES), F32), pltpu.SemaphoreType.DMA((7,)), pltpu.SemaphoreType.DMA((7,))],
        compiler_params=pltpu.CompilerParams(vmem_limit_bytes=VMEM_LIMIT, has_side_effects=True),
    )(v)


def _to_stacked(name, full):
    R, C = full.shape
    if name in ROW_SHARDED:
        return full.reshape(4, R // 4, C)
    return jnp.transpose(full.reshape(R, 4, C // 4), (1, 0, 2))


def _from_stacked(name, st):
    _, r, c = st.shape
    if name in ROW_SHARDED:
        return st.reshape(4 * r, c)
    return jnp.transpose(st, (1, 0, 2)).reshape(r, 4 * c)


UP_PIECES = ("ffn1_up", "ffn2_up")


def _layer_weights(lands):
    w = {n: lands[n] if n in UP_PIECES else _from_stacked(n, lands[n]) for n in BIG}
    win = w.pop("w_in")
    D = win.shape[0]
    p0, p1, p2, p3 = POOL_DIM, POOL_DIM + Q_LORA_RANK, POOL_DIM + Q_LORA_RANK + KV_LORA_RANK, \
        POOL_DIM + Q_LORA_RANK + KV_LORA_RANK + QK_ROPE_DIM
    w["w_pool"] = win[:, :p0]
    w["w_lat"] = jnp.concatenate([win[:, p0:p3], jnp.zeros((D, LAT_DIM - (p3 - p0)), win.dtype)], axis=1)
    w["w_gate"] = win[:, p3:]
    uq = w["w_uq"].reshape(Q_LORA_RANK, N_HEADS, QK_DIM)
    w["w_uq"] = jnp.concatenate([uq, jnp.zeros((Q_LORA_RANK, N_HEADS, HEAD_PAD - QK_DIM), uq.dtype)],
                                axis=2).reshape(Q_LORA_RANK, N_HEADS * HEAD_PAD)
    return w


def _layer_grads_stacked(dw):
    out = {}
    lat = dw.pop("w_lat")
    dw["w_in"] = jnp.concatenate([dw.pop("w_pool"), lat[:, :Q_LORA_RANK + KV_LORA_RANK + QK_ROPE_DIM],
                                  dw.pop("w_gate")], axis=1)
    dw["w_uq"] = dw["w_uq"].reshape(Q_LORA_RANK, N_HEADS, HEAD_PAD)[:, :, :QK_DIM].reshape(Q_LORA_RANK, N_HEADS * QK_DIM)
    for n in BIG:
        out[n] = dw[n] if n in UP_PIECES else _to_stacked(n, dw[n])
    return out


def _rope_tables(positions):
    inv_freq = ROPE_THETA ** (-jnp.arange(0, QK_ROPE_DIM, 2, dtype=F32) / QK_ROPE_DIM)
    ang = positions.astype(F32).reshape(-1)[:, None] * inv_freq
    cos, sin = jnp.cos(ang), jnp.sin(ang)
    z = jnp.zeros((ang.shape[0], LANES - QK_ROPE_DIM), F32)
    return jnp.concatenate([cos, cos, z], axis=1), jnp.concatenate([-sin, sin, z], axis=1)


def _pack_small(vals):
    parts = []
    for n in SMALL:
        f = vals[n].reshape(-1).astype(F32)
        pad = (-f.shape[0]) % (8 * LANES)
        parts.append(jnp.pad(f, (0, pad)))
    return jnp.concatenate(parts).reshape(-1, LANES)


def _unpack_small(packed, like):
    flat = packed.reshape(-1)
    out, off = {}, 0
    for n in SMALL:
        size = like[n].size
        out[n] = flat[off:off + size].reshape(like[n].shape)
        off += size + ((-size) % (8 * LANES))
    return out


def _ffn_fwd(x, g, wu4, wd, tag):
    h = rms_fwd(x, g, name=f"{tag}_norm")
    gate, up, a = ffn_up_act(h, wu4, name=f"{tag}_up_act")
    y = mm(a, wd, res=x, alpha=0.5, name=f"{tag}_down")
    return y, (x, h, gate, up, a)


def _ffn_bwd(dy, saved, g, wu4, wd, tag, dep=None):
    x, h, gate, up, a = saved
    dgate, dup = ffn_down_dx_act(dy, wd, gate, up, dep=dep, name=f"{tag}_down_dx_act")
    dwd = mm(a, dy, ta=True, alpha=0.5, out_dtype=BF16, name=f"{tag}_down_dw")
    dwu4 = ffn_up_dw(h, dgate, dup, name=f"{tag}_up_dw")
    dh = ffn_up_dx(dgate, dup, wu4, name=f"{tag}_up_dx")
    dx, dg = rms_bwd(x, g, dh, dy, name=f"{tag}_norm_bwd")
    return dx, dg, dwu4, dwd


def _mix_fwd(x, p, w, cs, sn, B):
    h = rms_fwd(x, p["norm_mix"], name="mix_norm")
    lat = mm(h, w["w_lat"], name="mix_lat")
    xp = mm(h, w["w_pool"], name="mix_pool_in")
    gl = mm(h, w["w_gate"], name="mix_gate_in")
    mixed = pool_fwd(xp, p["pool_maps"].astype(BF16), p["pool_scale"], B, name="pool_fwd")
    ba = mm(mixed, w["w_pool_proj"], name="mix_pool_proj")
    qn, kvn, kr = lat_fwd(lat, p["q_latent_norm"], p["kv_latent_norm"], cs, sn, name="lat_fwd")
    q = q_rope(mm(qn, w["w_uq"], name="mix_uq"), cs, sn, transpose=False, name="q_rope")
    kv = mm(kvn, w["w_ukv"], out_dtype=BF16, name="mix_ukv")
    o, lse = attn_fwd(q, kv, kr, B, name="attn_fwd")
    bb = mm(o, w["w_attn_proj"], name="mix_attn_proj")
    merged = gate_fwd(gl, p["b_gate"], ba, bb, name="gate_fwd")
    y = mm(merged, w["w_out"], res=x, name="mix_out")
    return y, (x, h, lat, xp, gl, mixed, ba, qn, kvn, kr, q, kv, o, lse, bb, merged)


def _mix_bwd(dy, saved, p, w, cs, sn, B):
    x, h, lat, xp, gl, mixed, ba, qn, kvn, kr, q, kv, o, lse, bb, merged = saved
    dw, ds = {}, {}
    dm = mm(dy, w["w_out"], tb=True, name="mix_out_dx")
    dw["w_out"] = mm(merged, dy, ta=True, out_dtype=BF16, name="mix_out_dw")
    dba, dbb, dgl, ds["b_gate"] = gate_bwd(dm, gl, p["b_gate"], ba, bb, name="gate_bwd")
    dw["w_attn_proj"] = mm(o, dbb, ta=True, out_dtype=BF16, name="mix_attn_proj_dw")
    do = mm(dbb, w["w_attn_proj"], tb=True, out_dtype=BF16, name="mix_attn_proj_dx")
    dw["w_pool_proj"] = mm(mixed, dba, ta=True, out_dtype=BF16, name="mix_pool_proj_dw")
    dmixed = mm(dba, w["w_pool_proj"], tb=True, name="mix_pool_proj_dx")
    dxp, ds["pool_maps"], ds["pool_scale"] = pool_bwd(xp, dmixed, p["pool_maps"].astype(BF16), p["pool_scale"], B,
                                                      name="pool_bwd")
    dq, dkv, dkr = attn_bwd(q, kv, kr, o, do, lse, B, name="attn_bwd")
    dw["w_ukv"] = mm(kvn, dkv, ta=True, out_dtype=BF16, name="mix_ukv_dw")
    dkvn = mm(dkv, w["w_ukv"], tb=True, name="mix_ukv_dx")
    dqb = q_rope(dq, cs, sn, transpose=True, name="q_rope_bwd")
    dw["w_uq"] = mm(qn, dqb, ta=True, out_dtype=BF16, name="mix_uq_dw")
    dqn = mm(dqb, w["w_uq"], tb=True, name="mix_uq_dx")
    dlat, ds["q_latent_norm"], ds["kv_latent_norm"] = lat_bwd(lat, p["q_latent_norm"], p["kv_latent_norm"], dqn, dkvn,
                                                               dkr, cs, sn, name="lat_bwd")
    dw["w_lat"] = mm(h, dlat, ta=True, out_dtype=BF16, name="mix_lat_dw")
    dw["w_pool"] = mm(h, dxp, ta=True, out_dtype=BF16, name="mix_pool_in_dw")
    dw["w_gate"] = mm(h, dgl, ta=True, out_dtype=BF16, name="mix_gate_in_dw")
    dh = mm(dlat, w["w_lat"], tb=True, name="mix_lat_dx")
    dh = mm(dxp, w["w_pool"], tb=True, res=dh, name="mix_pool_in_dx")
    dh = mm(dgl, w["w_gate"], tb=True, res=dh, name="mix_gate_in_dx")
    dx, ds["norm_mix"] = rms_bwd(x, p["norm_mix"], dh, dy, name="mix_norm_bwd")
    return dx, dw, ds


def kernel(x, positions, norm_ffn1, ffn1_up, ffn1_down, norm_mix, w_in, b_gate, pool_maps, pool_scale, w_pool_proj, q_latent_norm, w_uq, kv_latent_norm, w_ukv, w_attn_proj, w_out, norm_ffn2, ffn2_up, ffn2_down, final_norm, loss_target, m_norm_ffn1, m_ffn1_up, m_ffn1_down, m_norm_mix, m_w_in, m_b_gate, m_pool_maps, m_pool_scale, m_w_pool_proj, m_q_latent_norm, m_w_uq, m_kv_latent_norm, m_w_ukv, m_w_attn_proj, m_w_out, m_norm_ffn2, m_ffn2_up, m_ffn2_down, m_final_norm, v_norm_ffn1, v_ffn1_up, v_ffn1_down, v_norm_mix, v_w_in, v_b_gate, v_pool_maps, v_pool_scale, v_w_pool_proj, v_q_latent_norm, v_w_uq, v_kv_latent_norm, v_w_ukv, v_w_attn_proj, v_w_out, v_norm_ffn2, v_ffn2_up, v_ffn2_down, v_final_norm):
    given = dict(locals())
    B, S, D = x.shape
    T = B * S
    L = norm_ffn1.shape[0]
    W = {n: given[n] for n in WEIGHTS}
    Mo = {n: given["m_" + n] for n in WEIGHTS}
    Vo = {n: given["v_" + n] for n in WEIGHTS}
    core = lax.axis_index("c").astype(jnp.int32)
    chip = (2 * lax.axis_index("x") + lax.axis_index("y")).astype(jnp.int32)

    core_arr = core.reshape(1)
    place = jnp.stack([chip, core])
    nbig = len(BIG)

    own = [{n: W[n][l].astype(BF16) for n in BIG} for l in range(L)]
    ag, tok = [], 0.0
    for l in range(L):
        lands = [jax.ShapeDtypeStruct((4,) + own[l][n].shape, BF16) for n in BIG]
        handle, t = exchange_begin(f"ag_start_{l}", [own[l][n] for n in BIG], lands, ag_plan, 3 * nbig)
        ag.append(handle)
        tok = tok + t[0, 0]
    cs, sn = _rope_tables(positions)

    xs = x.reshape(T, D) + tok
    saved = []
    for l in range(L):
        mine, lands = exchange_end(f"ag_wait_{l}", ag[l], ag_plan, xs)
        lands = ag_forward(lands, name=f"ag_forward_{l}")
        lands = place_own(lands, mine, chip.reshape(1), name=f"place_own_{l}")
        w = _layer_weights(dict(zip(BIG, lands)))
        p = {n: W[n][l] for n in SMALL if n != "final_norm"}
        xs, s1 = _ffn_fwd(xs, p["norm_ffn1"], w["ffn1_up"], w["ffn1_down"], "ffn1")
        xs, s2 = _mix_fwd(xs, p, w, cs, sn, B)
        xs, s3 = _ffn_fwd(xs, p["norm_ffn2"], w["ffn2_up"], w["ffn2_down"], "ffn2")
        saved.append((w, p, s1, s2, s3))

    dx, dfinal, loss_tile = loss_head(xs, final_norm, loss_target.reshape(T, D), name="loss_head")
    loss = lax.psum(loss_tile[0, 0], ("x", "y", "c"))

    small_layers, pending, dep = [], [], None
    for l in reversed(range(L)):
        w, p, s1, s2, s3 = saved[l]
        dx, dg2, dwu2, dwd2 = _ffn_bwd(dx, s3, p["norm_ffn2"], w["ffn2_up"], w["ffn2_down"], "ffn2", dep=dep)
        dx, dw, ds = _mix_bwd(dx, s2, p, w, cs, sn, B)
        dx, dg1, dwu1, dwd1 = _ffn_bwd(dx, s1, p["norm_ffn1"], w["ffn1_up"], w["ffn1_down"], "ffn1")
        dw.update(ffn1_up=dwu1, ffn1_down=dwd1, ffn2_up=dwu2, ffn2_down=dwd2)
        ds.update(norm_ffn1=dg1, norm_ffn2=dg2)
        small_layers.append(ds)
        stacked = _layer_grads_stacked(dw)
        parts = [stacked[n] for n in BIG]
        r1 = rs_swap_halves(parts, name=f"rs_swap_{l}")
        sums = [rs_chip_sum(g, a, core_arr, name=f"rs_chip_sum_{n}") for n, g, a in zip(BIG, parts, r1)]
        lands = [jax.ShapeDtypeStruct((3,) + s.shape[1:], BF16) for s in sums]
        handle, dep = exchange_begin(f"rs_start_{l}", sums, lands, rs_plan, 3 * nbig)
        pending.append((l, parts, r1, handle))
    small_layers.reverse()

    acc = {n: lax.empty(W[n].shape, F32) for n in BIG}
    for l, parts, r1, handle in pending:
        _, r2 = exchange_end(f"rs_wait_{l}", handle, rs_plan, dx)
        for n, g, a, b in zip(BIG, parts, r1, r2):
            acc[n] = rs_final_sum(g, a, b, place, acc[n], l, name=f"rs_final_sum_{n}")
    grads = dict(zip(BIG, rs_join_halves([acc[n] for n in BIG])))

    small = {n: jnp.stack([small_layers[l][n].reshape(W[n].shape[1:]) for l in range(L)]) for n in SMALL
             if n != "final_norm"}
    small["final_norm"] = dfinal.reshape(final_norm.shape)
    grads.update(_unpack_small(all_reduce_small(_pack_small(small)), small))

    delta, new_m, new_v = {}, {}, {}
    for n in BIG:
        sh = W[n].shape
        two = lambda a: a.reshape(sh[0] * sh[1], sh[2])
        d, nm, nv = adamw(two(W[n]), two(grads[n]), two(Mo[n]), two(Vo[n]), name=f"adamw_{n}")
        delta[n], new_m[n], new_v[n] = d.reshape(sh), nm.reshape(sh), nv.reshape(sh)
    d, nm, nv = adamw(_pack_small(W), _pack_small(grads), _pack_small(Mo), _pack_small(Vo), name="adamw_small")
    delta.update(_unpack_small(d, W))
    new_m.update(_unpack_small(nm, W))
    new_v.update(_unpack_small(nv, W))

    return (loss, dx.reshape(B, S, D), *[grads[n] for n in WEIGHTS], *[delta[n] for n in WEIGHTS],
            *[new_m[n] for n in WEIGHTS], *[new_v[n] for n in WEIGHTS])
```

```python
import functools

import jax
import jax.numpy as jnp
from jax import lax
from jax.experimental import pallas as pl
from jax.experimental.pallas import tpu as pltpu

F32 = jnp.float32
BF16 = jnp.bfloat16

N_HEADS = 8
QK_NOPE_DIM = 128
QK_ROPE_DIM = 64
QK_DIM = QK_NOPE_DIM + QK_ROPE_DIM
V_HEAD_DIM = 128
HEAD_PAD = 256
Q_LORA_RANK = 384
KV_LORA_RANK = 256
ROPE_THETA = 10000.0
POOL_WINDOWS = (2, 4, 8, 16)
N_POOL_GROUPS = 4
POOL_GROUP_DIM = 128
POOL_DIM = N_POOL_GROUPS * POOL_GROUP_DIM
LAT_DIM = 768
NORM_EPS = 1e-6
ADAM_LR = 0.001
ADAM_B1 = 0.9
ADAM_B2 = 0.999
ADAM_EPS = 1e-08
ADAM_WD = 0.01
ADAM_STEP = 10
NEG_INF = -1e30
LANES = 128
ATT_BLOCK = 512
VMEM_LIMIT = 48 * 1024 * 1024
MESH = pl.DeviceIdType.MESH
_NT = (((1,), (1,)), ((), ()))
_TN = (((0,), (0,)), ((), ()))

BIG = ("ffn1_up", "ffn1_down", "w_in", "w_pool_proj", "w_uq", "w_ukv", "w_attn_proj", "w_out",
       "ffn2_up", "ffn2_down")
ROW_SHARDED = ("ffn1_down", "w_attn_proj", "w_out", "ffn2_down")
SMALL = ("norm_ffn1", "norm_mix", "b_gate", "pool_maps", "pool_scale", "q_latent_norm",
         "kv_latent_norm", "norm_ffn2", "final_norm")
WEIGHTS = ("norm_ffn1", "ffn1_up", "ffn1_down", "norm_mix", "w_in", "b_gate", "pool_maps", "pool_scale",
           "w_pool_proj", "q_latent_norm", "w_uq", "kv_latent_norm", "w_ukv", "w_attn_proj", "w_out",
           "norm_ffn2", "ffn2_up", "ffn2_down", "final_norm")


def _pick(dim, cands):
    for c in cands:
        if c <= dim and dim % c == 0:
            return c
    return dim


def _cparams(sem=None, **kw):
    if sem is not None:
        kw["dimension_semantics"] = sem
    return pltpu.CompilerParams(vmem_limit_bytes=VMEM_LIMIT, **kw)


def _sigmoid(x):
    return 1.0 / (1.0 + jnp.exp(-x))


MM_TILE_BUDGET = 30 * 1024 * 1024
TILE_SIZES = (1408, 1024, 768, 512, 384, 256, 128)


def _mm_tiles(M, N, K, sa, sb, so, sr):
    tks = [K] if K <= 2816 else [t for t in (2816, 2048, 1408, 1024, 512, 256, 128) if K % t == 0]
    best = None
    for tk in tks:
        for tm in [t for t in TILE_SIZES if M % t == 0] or [M]:
            for tn in [t for t in TILE_SIZES if N % t == 0] or [N]:
                need = 2 * (tm * tk * sa + tk * tn * sb + tm * tn * (so + sr)) + (tm * tn * 4 if tk < K else 0)
                if need <= MM_TILE_BUDGET:
                    score = (tm * tn * tk, tk, tm)
                    if best is None or score > best[0]:
                        best = (score, (tm, tn, tk))
    assert best is not None, (M, N, K)
    return best[1]


def mm(a, b, *, name, ta=False, tb=False, out_dtype=F32, res=None, alpha=1.0, dep=None):
    if ta:
        K, M = a.shape
    else:
        M, K = a.shape
    if tb:
        N, K2 = b.shape
    else:
        K2, N = b.shape
    assert K == K2, (a.shape, b.shape, ta, tb)
    tm, tn, tk = _mm_tiles(M, N, K, a.dtype.itemsize, b.dtype.itemsize, jnp.dtype(out_dtype).itemsize,
                           0 if res is None else res.dtype.itemsize)
    nk = K // tk
    dims = (((0 if ta else 1,), (1 if tb else 0,)), ((), ()))

    def body(*refs):
        a_ref, b_ref = refs[:2]
        r_ref = refs[2] if res is not None else None
        o_ref = refs[-2] if nk > 1 else refs[-1]

        def finish(r):
            if alpha != 1.0:
                r = r * alpha
            if res is not None:
                r = r_ref[...].astype(F32) + r
            o_ref[...] = r.astype(out_dtype)

        part = lax.dot_general(a_ref[...].astype(BF16), b_ref[...].astype(BF16), dims, preferred_element_type=F32)
        if nk == 1:
            finish(part)
            return
        acc = refs[-1]
        k = pl.program_id(2)

        @pl.when(k == 0)
        def _():
            acc[...] = part

        @pl.when(k > 0)
        def _():
            acc[...] += part

        @pl.when(k == nk - 1)
        def _():
            finish(acc[...])

    a_spec = pl.BlockSpec((tk, tm), lambda i, j, k: (k, i)) if ta else pl.BlockSpec((tm, tk), lambda i, j, k: (i, k))
    b_spec = pl.BlockSpec((tn, tk), lambda i, j, k: (j, k)) if tb else pl.BlockSpec((tk, tn), lambda i, j, k: (k, j))
    o_spec = pl.BlockSpec((tm, tn), lambda i, j, k: (i, j))
    in_specs = [a_spec, b_spec]
    args = [a, b]
    if res is not None:
        in_specs.append(o_spec)
        args.append(res)
    if dep is not None:
        in_specs.append(pl.BlockSpec((8, LANES), lambda i, j, k: (0, 0)))
        args.append(dep)
    return pl.pallas_call(
        body, name=name, grid=(M // tm, N // tn, nk), in_specs=in_specs, out_specs=o_spec,
        out_shape=jax.ShapeDtypeStruct((M, N), out_dtype),
        scratch_shapes=[pltpu.VMEM((tm, tn), F32)] if nk > 1 else [],
        compiler_params=_cparams(("parallel", "parallel", "arbitrary")),
    )(*args)


MXU_COLS = 256


def _col_chunks(n):
    return [(lo, min(lo + MXU_COLS, n)) for lo in range(0, n, MXU_COLS)]


def ffn_up_act(h, wu4, *, name):
    T, D = h.shape
    cq = wu4.shape[2]
    Fh = 2 * cq
    tm = _pick(T, (512, 256, 128))

    def body(h_ref, wg_ref, wu_ref, g_ref, u_ref, a_ref):
        hv = h_ref[...]
        for lo, hi in _col_chunks(cq):
            gv = jnp.dot(hv, wg_ref[0, :, lo:hi], preferred_element_type=F32)
            uv = jnp.dot(hv, wu_ref[0, :, lo:hi], preferred_element_type=F32)
            g_ref[:, lo:hi] = gv.astype(BF16)
            u_ref[:, lo:hi] = uv.astype(BF16)
            a_ref[:, lo:hi] = (gv * _sigmoid(gv) * uv).astype(BF16)

    tile = pl.BlockSpec((tm, cq), lambda i, j: (i, j))
    sh = jax.ShapeDtypeStruct((T, Fh), BF16)
    return pl.pallas_call(
        body, name=name, grid=(T // tm, 2),
        in_specs=[pl.BlockSpec((tm, D), lambda i, j: (i, 0)), pl.BlockSpec((1, D, cq), lambda i, j: (j, 0, 0)),
                  pl.BlockSpec((1, D, cq), lambda i, j: (2 + j, 0, 0))],
        out_specs=[tile, tile, tile], out_shape=[sh, sh, sh],
        compiler_params=_cparams(("parallel", "parallel")),
    )(h, wu4, wu4)


def ffn_down_dx_act(dy, wd, g, u, *, dep=None, name):
    T, D = dy.shape
    Fh = wd.shape[0]
    cq = Fh // 2
    tm = _pick(T, (512, 256, 128))

    def body(dy_ref, wd_ref, g_ref, u_ref, *rest):
        dg_ref, du_ref = rest[-2:]
        dyv = dy_ref[...].astype(BF16)
        for lo, hi in _col_chunks(cq):
            da = 0.5 * lax.dot_general(dyv, wd_ref[lo:hi, :], _NT, preferred_element_type=F32)
            gv = g_ref[:, lo:hi].astype(F32)
            uv = u_ref[:, lo:hi].astype(F32)
            s = _sigmoid(gv)
            dg_ref[:, lo:hi] = (da * uv * (s * (1.0 + gv * (1.0 - s)))).astype(BF16)
            du_ref[:, lo:hi] = (da * (gv * s)).astype(BF16)

    tile = pl.BlockSpec((tm, cq), lambda i, j: (i, j))
    sh = jax.ShapeDtypeStruct((T, Fh), BF16)
    in_specs = [pl.BlockSpec((tm, D), lambda i, j: (i, 0)), pl.BlockSpec((cq, D), lambda i, j: (j, 0)), tile, tile]
    args = [dy, wd, g, u]
    if dep is not None:
        in_specs.append(pl.BlockSpec((8, LANES), lambda i, j: (0, 0)))
        args.append(dep)
    return pl.pallas_call(
        body, name=name, grid=(T // tm, 2), in_specs=in_specs, out_specs=[tile, tile], out_shape=[sh, sh],
        compiler_params=_cparams(("parallel", "parallel")),
    )(*args)


def ffn_up_dw(h, dg, du, *, name):
    T, D = h.shape
    cq = dg.shape[1] // 2
    tk = _pick(T, (1024, 512, 256, 128))
    nk = T // tk

    def body(h_ref, dg_ref, du_ref, o_ref, acc):
        p = pl.program_id(0)
        k = pl.program_id(1)

        @pl.when(k == 0)
        def _():
            acc[...] = jnp.zeros_like(acc)

        @pl.when(p < 2)
        def _():
            acc[...] += lax.dot_general(h_ref[...], dg_ref[...], _TN, preferred_element_type=F32)

        @pl.when(p >= 2)
        def _():
            acc[...] += lax.dot_general(h_ref[...], du_ref[...], _TN, preferred_element_type=F32)

        @pl.when(k == nk - 1)
        def _():
            o_ref[0] = acc[...].astype(BF16)

    return pl.pallas_call(
        body, name=name, grid=(4, nk),
        in_specs=[pl.BlockSpec((tk, D), lambda p, k: (k, 0)),
                  pl.BlockSpec((tk, cq), lambda p, k: (k, jnp.minimum(p, 1))),
                  pl.BlockSpec((tk, cq), lambda p, k: (k, jnp.maximum(p - 2, 0)))],
        out_specs=pl.BlockSpec((1, D, cq), lambda p, k: (p, 0, 0)),
        out_shape=jax.ShapeDtypeStruct((4, D, cq), BF16),
        scratch_shapes=[pltpu.VMEM((D, cq), F32)],
        compiler_params=_cparams(("parallel", "arbitrary")),
    )(h, dg, du)


def ffn_up_dx(dg, du, wu4, *, name):
    T = dg.shape[0]
    _, D, cq = wu4.shape
    tm = _pick(T, (512, 256, 128))

    def body(dg_ref, du_ref, wg_ref, wu_ref, o_ref, acc):
        k = pl.program_id(1)
        part = lax.dot_general(dg_ref[...], wg_ref[0], _NT, preferred_element_type=F32)
        part = part + lax.dot_general(du_ref[...], wu_ref[0], _NT, preferred_element_type=F32)

        @pl.when(k == 0)
        def _():
            acc[...] = part

        @pl.when(k == 1)
        def _():
            o_ref[...] = acc[...] + part

    tile = pl.BlockSpec((tm, cq), lambda i, k: (i, k))
    return pl.pallas_call(
        body, name=name, grid=(T // tm, 2),
        in_specs=[tile, tile, pl.BlockSpec((1, D, cq), lambda i, k: (k, 0, 0)),
                  pl.BlockSpec((1, D, cq), lambda i, k: (2 + k, 0, 0))],
        out_specs=pl.BlockSpec((tm, D), lambda i, k: (i, 0)),
        out_shape=jax.ShapeDtypeStruct((T, D), F32),
        scratch_shapes=[pltpu.VMEM((tm, D), F32)],
        compiler_params=_cparams(("parallel", "arbitrary")),
    )(dg, du, wu4, wu4)


def _rows(T, width_bytes):
    cap = max(8, (2 * 1024 * 1024) // width_bytes)
    return _pick(T, tuple(c for c in (1024, 512, 256, 128, 64, 32, 16) if c <= cap))


def rms_fwd(x, g, *, name, dep=None):
    T, D = x.shape
    tm = _rows(T, D * 4)

    def body(x_ref, g_ref, *rest):
        xv = x_ref[...]
        r = lax.rsqrt(jnp.mean(xv * xv, axis=-1, keepdims=True) + NORM_EPS)
        rest[-1][...] = (xv * r * g_ref[...]).astype(BF16)

    in_specs = [pl.BlockSpec((tm, D), lambda i: (i, 0)), pl.BlockSpec((1, D), lambda i: (0, 0))]
    args = [x, g.reshape(1, D)]
    if dep is not None:
        in_specs.append(pl.BlockSpec((8, LANES), lambda i: (0, 0)))
        args.append(dep)
    return pl.pallas_call(
        body, name=name, grid=(T // tm,), in_specs=in_specs,
        out_specs=pl.BlockSpec((tm, D), lambda i: (i, 0)),
        out_shape=jax.ShapeDtypeStruct((T, D), BF16),
        compiler_params=_cparams(("parallel",)),
    )(*args)


def _rms_bwd_math(xv, gv, dh):
    r = lax.rsqrt(jnp.mean(xv * xv, axis=-1, keepdims=True) + NORM_EPS)
    xn = xv * r
    dg = jnp.sum(dh * xn, axis=0, keepdims=True)
    dxn = dh * gv
    dx = r * (dxn - xn * jnp.mean(dxn * xn, axis=-1, keepdims=True))
    return dx, dg


def rms_bwd(x, g, dh, dres, *, name):
    T, D = x.shape
    tm = _rows(T, D * 4)

    def body(x_ref, g_ref, dh_ref, dres_ref, dx_ref, dg_ref):
        @pl.when(pl.program_id(0) == 0)
        def _():
            dg_ref[...] = jnp.zeros_like(dg_ref)

        dx, dg = _rms_bwd_math(x_ref[...], g_ref[...], dh_ref[...].astype(F32))
        dx_ref[...] = dres_ref[...] + dx
        dg_ref[...] += dg

    row = pl.BlockSpec((tm, D), lambda i: (i, 0))
    vec = pl.BlockSpec((1, D), lambda i: (0, 0))
    return pl.pallas_call(
        body, name=name, grid=(T // tm,), in_specs=[row, vec, row, row], out_specs=[row, vec],
        out_shape=[jax.ShapeDtypeStruct((T, D), F32), jax.ShapeDtypeStruct((1, D), F32)],
        compiler_params=_cparams(("arbitrary",)),
    )(x, g.reshape(1, D), dh, dres)


def _rope(xv, cv, sv):
    half = QK_ROPE_DIM // 2
    lane = lax.broadcasted_iota(jnp.int32, xv.shape, 1)
    swapped = jnp.where(lane < half, pltpu.roll(xv, LANES - half, 1), pltpu.roll(xv, half, 1))
    return xv * cv + swapped * sv


def _rope_t(dy, cv, sv):
    half = QK_ROPE_DIM // 2
    ds = dy * sv
    lane = lax.broadcasted_iota(jnp.int32, dy.shape, 1)
    swapped = jnp.where(lane < half, pltpu.roll(ds, LANES - half, 1), pltpu.roll(ds, half, 1))
    return dy * cv + swapped


def lat_fwd(lat, qn_w, kvn_w, cs, sn, *, name):
    T = lat.shape[0]
    tm = _rows(T, LAT_DIM * 4)
    kv0 = Q_LORA_RANK
    kr0 = Q_LORA_RANK + KV_LORA_RANK

    def body(lat_ref, qw_ref, kw_ref, c_ref, s_ref, qn_ref, kvn_ref, kr_ref):
        ql = lat_ref[:, :kv0]
        r = lax.rsqrt(jnp.mean(ql * ql, axis=-1, keepdims=True) + NORM_EPS)
        qn_ref[...] = (ql * r * qw_ref[...]).astype(BF16)
        kl = lat_ref[:, kv0:kr0]
        r = lax.rsqrt(jnp.mean(kl * kl, axis=-1, keepdims=True) + NORM_EPS)
        kvn_ref[...] = (kl * r * kw_ref[...]).astype(BF16)
        kr_ref[...] = _rope(lat_ref[:, kr0:], c_ref[...], s_ref[...]).astype(BF16)

    return pl.pallas_call(
        body, name=name, grid=(T // tm,),
        in_specs=[pl.BlockSpec((tm, LAT_DIM), lambda i: (i, 0)),
                  pl.BlockSpec((1, Q_LORA_RANK), lambda i: (0, 0)),
                  pl.BlockSpec((1, KV_LORA_RANK), lambda i: (0, 0)),
                  pl.BlockSpec((tm, LANES), lambda i: (i, 0)), pl.BlockSpec((tm, LANES), lambda i: (i, 0))],
        out_specs=[pl.BlockSpec((tm, Q_LORA_RANK), lambda i: (i, 0)),
                   pl.BlockSpec((tm, KV_LORA_RANK), lambda i: (i, 0)),
                   pl.BlockSpec((tm, LANES), lambda i: (i, 0))],
        out_shape=[jax.ShapeDtypeStruct((T, Q_LORA_RANK), BF16), jax.ShapeDtypeStruct((T, KV_LORA_RANK), BF16),
                   jax.ShapeDtypeStruct((T, LANES), BF16)],
        compiler_params=_cparams(("parallel",)),
    )(lat, qn_w.reshape(1, -1), kvn_w.reshape(1, -1), cs, sn)


def lat_bwd(lat, qn_w, kvn_w, dqn, dkvn, dkr, cs, sn, *, name):
    T = lat.shape[0]
    tm = _rows(T, LAT_DIM * 4)
    kv0 = Q_LORA_RANK
    kr0 = Q_LORA_RANK + KV_LORA_RANK

    def body(lat_ref, qw_ref, kw_ref, dqn_ref, dkvn_ref, dkr_ref, c_ref, s_ref, dlat_ref, dqw_ref, dkw_ref):
        @pl.when(pl.program_id(0) == 0)
        def _():
            dqw_ref[...] = jnp.zeros_like(dqw_ref)
            dkw_ref[...] = jnp.zeros_like(dkw_ref)

        dx, dg = _rms_bwd_math(lat_ref[:, :kv0], qw_ref[...], dqn_ref[...])
        dlat_ref[:, :kv0] = dx.astype(BF16)
        dqw_ref[...] += dg
        dx, dg = _rms_bwd_math(lat_ref[:, kv0:kr0], kw_ref[...], dkvn_ref[...])
        dlat_ref[:, kv0:kr0] = dx.astype(BF16)
        dkw_ref[...] += dg
        dlat_ref[:, kr0:] = _rope_t(dkr_ref[...], c_ref[...], s_ref[...]).astype(BF16)

    row = lambda w: pl.BlockSpec((tm, w), lambda i: (i, 0))
    vec = lambda w: pl.BlockSpec((1, w), lambda i: (0, 0))
    return pl.pallas_call(
        body, name=name, grid=(T // tm,),
        in_specs=[row(LAT_DIM), vec(Q_LORA_RANK), vec(KV_LORA_RANK), row(Q_LORA_RANK), row(KV_LORA_RANK),
                  row(LANES), row(LANES), row(LANES)],
        out_specs=[row(LAT_DIM), vec(Q_LORA_RANK), vec(KV_LORA_RANK)],
        out_shape=[jax.ShapeDtypeStruct((T, LAT_DIM), BF16), jax.ShapeDtypeStruct((1, Q_LORA_RANK), F32),
                   jax.ShapeDtypeStruct((1, KV_LORA_RANK), F32)],
        compiler_params=_cparams(("arbitrary",)),
    )(lat, qn_w.reshape(1, -1), kvn_w.reshape(1, -1), dqn, dkvn, dkr, cs, sn)


def q_rope(q, cs, sn, *, transpose, name):
    T, W = q.shape
    tm = _rows(T, W * 4)
    fn = _rope_t if transpose else _rope
    scale = QK_DIM ** -0.5

    def body(q_ref, c_ref, s_ref, o_ref):
        cv = c_ref[...] * scale
        sv = s_ref[...] * scale
        for h in range(N_HEADS):
            lo = h * HEAD_PAD
            o_ref[:, lo:lo + QK_NOPE_DIM] = (q_ref[:, lo:lo + QK_NOPE_DIM] * scale).astype(BF16)
            o_ref[:, lo + QK_NOPE_DIM:lo + HEAD_PAD] = fn(q_ref[:, lo + QK_NOPE_DIM:lo + HEAD_PAD], cv, sv).astype(BF16)

    return pl.pallas_call(
        body, name=name, grid=(T // tm,),
        in_specs=[pl.BlockSpec((tm, W), lambda i: (i, 0)), pl.BlockSpec((tm, LANES), lambda i: (i, 0)),
                  pl.BlockSpec((tm, LANES), lambda i: (i, 0))],
        out_specs=pl.BlockSpec((tm, W), lambda i: (i, 0)),
        out_shape=jax.ShapeDtypeStruct((T, W), BF16),
        compiler_params=_cparams(("parallel",)),
    )(q, cs, sn)


def gate_fwd(gl, bg, ba, bb, *, name):
    T, D2 = gl.shape
    D = D2 // 2
    tm = _rows(T, D2 * 4)

    def body(gl_ref, bg_ref, ba_ref, bb_ref, m_ref):
        ga = _sigmoid(gl_ref[:, :D] + bg_ref[:, :D])
        gb = _sigmoid(gl_ref[:, D:] + bg_ref[:, D:])
        m_ref[...] = (ga * ba_ref[...] + gb * bb_ref[...]).astype(BF16)

    row = lambda w: pl.BlockSpec((tm, w), lambda i: (i, 0))
    return pl.pallas_call(
        body, name=name, grid=(T // tm,),
        in_specs=[row(D2), pl.BlockSpec((1, D2), lambda i: (0, 0)), row(D), row(D)],
        out_specs=row(D), out_shape=jax.ShapeDtypeStruct((T, D), BF16),
        compiler_params=_cparams(("parallel",)),
    )(gl, bg.reshape(1, D2), ba, bb)


def gate_bwd(dm, gl, bg, ba, bb, *, name):
    T, D2 = gl.shape
    D = D2 // 2
    tm = _rows(T, D2 * 4)

    def body(dm_ref, gl_ref, bg_ref, ba_ref, bb_ref, dba_ref, dbb_ref, dgl_ref, dbg_ref):
        @pl.when(pl.program_id(0) == 0)
        def _():
            dbg_ref[...] = jnp.zeros_like(dbg_ref)

        dmv = dm_ref[...]
        ga = _sigmoid(gl_ref[:, :D] + bg_ref[:, :D])
        gb = _sigmoid(gl_ref[:, D:] + bg_ref[:, D:])
        dba_ref[...] = (dmv * ga).astype(BF16)
        dbb_ref[...] = (dmv * gb).astype(BF16)
        dla = dmv * ba_ref[...] * ga * (1.0 - ga)
        dlb = dmv * bb_ref[...] * gb * (1.0 - gb)
        dgl_ref[:, :D] = dla.astype(BF16)
        dgl_ref[:, D:] = dlb.astype(BF16)
        dbg_ref[:, :D] += jnp.sum(dla, axis=0, keepdims=True)
        dbg_ref[:, D:] += jnp.sum(dlb, axis=0, keepdims=True)

    row = lambda w: pl.BlockSpec((tm, w), lambda i: (i, 0))
    vec = pl.BlockSpec((1, D2), lambda i: (0, 0))
    return pl.pallas_call(
        body, name=name, grid=(T // tm,),
        in_specs=[row(D), row(D2), vec, row(D), row(D)],
        out_specs=[row(D), row(D), row(D2), vec],
        out_shape=[jax.ShapeDtypeStruct((T, D), BF16), jax.ShapeDtypeStruct((T, D), BF16),
                   jax.ShapeDtypeStruct((T, D2), BF16), jax.ShapeDtypeStruct((1, D2), F32)],
        compiler_params=_cparams(("arbitrary",)),
    )(dm, gl, bg.reshape(1, D2), ba, bb)


def loss_head(x, gf, tgt, *, name):
    T, D = x.shape
    tm = _rows(T, D * 4)

    def body(x_ref, g_ref, t_ref, dx_ref, dg_ref, loss_ref):
        @pl.when(pl.program_id(0) == 0)
        def _():
            dg_ref[...] = jnp.zeros_like(dg_ref)
            loss_ref[...] = jnp.zeros_like(loss_ref)

        xv = x_ref[...]
        gv = g_ref[...]
        r = lax.rsqrt(jnp.mean(xv * xv, axis=-1, keepdims=True) + NORM_EPS)
        xn = xv * r
        err = xn * gv - t_ref[...]
        loss_ref[...] += 0.5 * jnp.sum(jnp.mean(err * err, axis=-1, keepdims=True))
        dy = err * (1.0 / D)
        dg_ref[...] += jnp.sum(dy * xn, axis=0, keepdims=True)
        dxn = dy * gv
        dx_ref[...] = r * (dxn - xn * jnp.mean(dxn * xn, axis=-1, keepdims=True))

    row = pl.BlockSpec((tm, D), lambda i: (i, 0))
    vec = pl.BlockSpec((1, D), lambda i: (0, 0))
    return pl.pallas_call(
        body, name=name, grid=(T // tm,), in_specs=[row, vec, row],
        out_specs=[row, vec, pl.BlockSpec((8, LANES), lambda i: (0, 0))],
        out_shape=[jax.ShapeDtypeStruct((T, D), F32), jax.ShapeDtypeStruct((1, D), F32),
                   jax.ShapeDtypeStruct((8, LANES), F32)],
        compiler_params=_cparams(("arbitrary",)),
    )(x, gf.reshape(1, D), tgt)


def _shift_rows(s, k, down):
    n = s.shape[0]
    t = lax.broadcasted_iota(jnp.int32, s.shape, 0)
    if down:
        return jnp.where(t >= k, pltpu.roll(s, k, 0), 0.0)
    return jnp.where(t < n - k, pltpu.roll(s, n - k, 0), 0.0)


def _window_sum(s, w, down):
    k = 1
    while k < w:
        s = s + _shift_rows(s, k, down)
        k *= 2
    return s


def _pool_count(shape, w):
    t = lax.broadcasted_iota(jnp.int32, shape, 0)
    return jnp.minimum(t + 1, w).astype(F32)


def pool_fwd(xp, maps, scale, B, *, name):
    T, P = xp.shape
    S = T // B
    G = POOL_GROUP_DIM

    def body(x_ref, m_ref, sc_ref, o_ref):
        for g, w in enumerate(POOL_WINDOWS):
            xg = x_ref[:, g * G:(g + 1) * G]
            pooled = _window_sum(xg, w, True) / _pool_count(xg.shape, w) - xg
            mixed = jnp.dot(pooled.astype(BF16), m_ref[g], preferred_element_type=F32)
            o_ref[:, g * G:(g + 1) * G] = (mixed * sc_ref[:, g * G:(g + 1) * G]).astype(BF16)

    return pl.pallas_call(
        body, name=name, grid=(B,),
        in_specs=[pl.BlockSpec((S, P), lambda b: (b, 0)), pl.BlockSpec((N_POOL_GROUPS, G, G), lambda b: (0, 0, 0)),
                  pl.BlockSpec((1, P), lambda b: (0, 0))],
        out_specs=pl.BlockSpec((S, P), lambda b: (b, 0)),
        out_shape=jax.ShapeDtypeStruct((T, P), BF16),
        compiler_params=_cparams(("parallel",)),
    )(xp, maps, scale.reshape(1, P))


def pool_bwd(xp, dmixed, maps, scale, B, *, name):
    T, P = xp.shape
    S = T // B
    G = POOL_GROUP_DIM

    def body(x_ref, dm_ref, m_ref, sc_ref, dx_ref, dmaps_ref, dsc_ref):
        @pl.when(pl.program_id(0) == 0)
        def _():
            dmaps_ref[...] = jnp.zeros_like(dmaps_ref)
            dsc_ref[...] = jnp.zeros_like(dsc_ref)

        for g, w in enumerate(POOL_WINDOWS):
            cols = slice(g * G, (g + 1) * G)
            xg = x_ref[:, cols]
            cnt = _pool_count(xg.shape, w)
            pooled = (_window_sum(xg, w, True) / cnt - xg).astype(BF16)
            mixed = jnp.dot(pooled, m_ref[g], preferred_element_type=F32)
            dmx = dm_ref[:, cols]
            dsc_ref[:, cols] += jnp.sum(dmx * mixed, axis=0, keepdims=True)
            dmp = (dmx * sc_ref[:, cols]).astype(BF16)
            dmaps_ref[g] += lax.dot_general(pooled, dmp, (((0,), (0,)), ((), ())), preferred_element_type=F32)
            dpooled = lax.dot_general(dmp, m_ref[g], (((1,), (1,)), ((), ())), preferred_element_type=F32)
            dx_ref[:, cols] = (_window_sum(dpooled / cnt, w, False) - dpooled).astype(BF16)

    blk = pl.BlockSpec((S, P), lambda b: (b, 0))
    mp = pl.BlockSpec((N_POOL_GROUPS, G, G), lambda b: (0, 0, 0))
    vec = pl.BlockSpec((1, P), lambda b: (0, 0))
    return pl.pallas_call(
        body, name=name, grid=(B,), in_specs=[blk, blk, mp, vec], out_specs=[blk, mp, vec],
        out_shape=[jax.ShapeDtypeStruct((T, P), BF16), jax.ShapeDtypeStruct((N_POOL_GROUPS, G, G), F32),
                   jax.ShapeDtypeStruct((1, P), F32)],
        compiler_params=_cparams(("arbitrary",)),
    )(xp, dmixed, maps, scale.reshape(1, P))


def _keys(kv_ref, kr_ref, rows):
    return jnp.concatenate([kv_ref[rows, :QK_NOPE_DIM], kr_ref[rows, :]], axis=1)


def _causal(s):
    row = lax.broadcasted_iota(jnp.int32, s.shape, 0)
    col = lax.broadcasted_iota(jnp.int32, s.shape, 1)
    return jnp.where(row >= col, s, NEG_INF)


def attn_fwd(q, kv, kr, B, *, name):
    T = q.shape[0]
    S = T // B
    blk = min(ATT_BLOCK, S)
    nb = S // blk
    H = N_HEADS

    def body(q_ref, kv_ref, kr_ref, o_ref, lse_ref):
        for qi in range(nb):
            rows = slice(qi * blk, (qi + 1) * blk)
            qb = q_ref[rows, :]
            sd = _causal(lax.dot_general(qb, _keys(kv_ref, kr_ref, rows), _NT, preferred_element_type=F32))
            m = jnp.max(sd, axis=-1, keepdims=True)
            if qi > 0:
                prev = slice(0, qi * blk)
                sp = lax.dot_general(qb, _keys(kv_ref, kr_ref, prev), _NT, preferred_element_type=F32)
                m = jnp.maximum(m, jnp.max(sp, axis=-1, keepdims=True))
            pd = jnp.exp(sd - m)
            l = jnp.sum(pd, axis=-1, keepdims=True)
            acc = jnp.dot(pd.astype(BF16), kv_ref[rows, QK_NOPE_DIM:], preferred_element_type=F32)
            if qi > 0:
                pp = jnp.exp(sp - m)
                l = l + jnp.sum(pp, axis=-1, keepdims=True)
                acc = acc + jnp.dot(pp.astype(BF16), kv_ref[prev, QK_NOPE_DIM:], preferred_element_type=F32)
            o_ref[rows, :] = (acc / l).astype(BF16)
            lse_ref[0, rows, :] = m + jnp.log(l)

    return pl.pallas_call(
        body, name=name, grid=(B, H),
        in_specs=[pl.BlockSpec((S, HEAD_PAD), lambda b, h: (b, h)), pl.BlockSpec((S, HEAD_PAD), lambda b, h: (b, h)),
                  pl.BlockSpec((S, LANES), lambda b, h: (b, 0))],
        out_specs=[pl.BlockSpec((S, V_HEAD_DIM), lambda b, h: (b, h)), pl.BlockSpec((1, S, 1), lambda b, h: (h, b, 0))],
        out_shape=[jax.ShapeDtypeStruct((T, H * V_HEAD_DIM), BF16), jax.ShapeDtypeStruct((H, T, 1), F32)],
        compiler_params=_cparams(("parallel", "parallel")),
    )(q, kv, kr)


def attn_bwd(q, kv, kr, o, do, lse, B, *, name):
    T = q.shape[0]
    S = T // B
    blk = min(ATT_BLOCK, S)
    nb = S // blk
    H = N_HEADS

    def body(q_ref, kv_ref, kr_ref, o_ref, do_ref, lse_ref, dq_ref, dkv_ref, dkr_ref, dk_s, dv_s):
        dk_s[...] = jnp.zeros_like(dk_s)
        dv_s[...] = jnp.zeros_like(dv_s)

        @pl.when(pl.program_id(1) == 0)
        def _():
            dkr_ref[...] = jnp.zeros_like(dkr_ref)

        for qi in range(nb):
            rows = slice(qi * blk, (qi + 1) * blk)
            qb = q_ref[rows, :]
            dob = do_ref[rows, :]
            delta = jnp.sum(dob.astype(F32) * o_ref[rows, :].astype(F32), axis=-1, keepdims=True)
            lse_b = lse_ref[0, rows, :]

            def part(ks, diagonal):
                k = _keys(kv_ref, kr_ref, ks)
                s = lax.dot_general(qb, k, _NT, preferred_element_type=F32)
                if diagonal:
                    s = _causal(s)
                p = jnp.exp(s - lse_b)
                dp = lax.dot_general(dob, kv_ref[ks, QK_NOPE_DIM:], _NT, preferred_element_type=F32)
                ds = (p * (dp - delta)).astype(BF16)
                dv_s[ks, :] += lax.dot_general(p.astype(BF16), dob, _TN, preferred_element_type=F32)
                dk_s[ks, :] += lax.dot_general(ds, qb, _TN, preferred_element_type=F32)
                return jnp.dot(ds, k, preferred_element_type=F32)

            dq = part(rows, True)
            if qi > 0:
                dq = dq + part(slice(0, qi * blk), False)
            dq_ref[rows, :] = dq

        dkv_ref[:, :QK_NOPE_DIM] = dk_s[:, :QK_NOPE_DIM].astype(BF16)
        dkv_ref[:, QK_NOPE_DIM:] = dv_s[...].astype(BF16)
        dkr_ref[...] += dk_s[:, QK_NOPE_DIM:]

    head = lambda w: pl.BlockSpec((S, w), lambda b, h: (b, h))
    shared = pl.BlockSpec((S, LANES), lambda b, h: (b, 0))
    return pl.pallas_call(
        body, name=name, grid=(B, H),
        in_specs=[head(HEAD_PAD), head(HEAD_PAD), shared, head(V_HEAD_DIM), head(V_HEAD_DIM),
                  pl.BlockSpec((1, S, 1), lambda b, h: (h, b, 0))],
        out_specs=[head(HEAD_PAD), head(HEAD_PAD), shared],
        out_shape=[jax.ShapeDtypeStruct((T, H * HEAD_PAD), F32), jax.ShapeDtypeStruct((T, H * HEAD_PAD), BF16),
                   jax.ShapeDtypeStruct((T, LANES), F32)],
        scratch_shapes=[pltpu.VMEM((S, HEAD_PAD), F32), pltpu.VMEM((S, V_HEAD_DIM), F32)],
        compiler_params=_cparams(("parallel", "arbitrary")),
    )(q, kv, kr, o, do, lse)


def adamw(w, g, m, v, *, name):
    R, C = w.shape
    cap = max(8, (1024 * 1024) // (C * 4))
    tr = _pick(R, tuple(c for c in (1024, 512, 256, 128, 64, 32, 16, 8) if c <= cap))
    c1 = 1.0 - ADAM_B1 ** ADAM_STEP
    c2 = 1.0 - ADAM_B2 ** ADAM_STEP

    def body(w_ref, g_ref, m_ref, v_ref, d_ref, nm_ref, nv_ref):
        gv = g_ref[...]
        mv = ADAM_B1 * m_ref[...] + (1.0 - ADAM_B1) * gv
        vv = ADAM_B2 * v_ref[...] + (1.0 - ADAM_B2) * (gv * gv)
        nm_ref[...] = mv
        nv_ref[...] = vv
        d_ref[...] = -ADAM_LR * ((mv / c1) / (jnp.sqrt(vv / c2) + ADAM_EPS) + ADAM_WD * w_ref[...])

    blk = pl.BlockSpec((tr, C), lambda i: (i, 0))
    sh = jax.ShapeDtypeStruct((R, C), F32)
    return pl.pallas_call(
        body, name=name, grid=(R // tr,), in_specs=[blk] * 4, out_specs=[blk] * 3, out_shape=[sh] * 3,
        compiler_params=_cparams(("parallel",)),
    )(w, g, m, v)


ANY = pl.BlockSpec(memory_space=pl.ANY)


def _place():
    x, y, c = lax.axis_index("x"), lax.axis_index("y"), lax.axis_index("c")
    others = [(1 - x, y), (x, 1 - y), (1 - x, 1 - y)]
    return x, y, c, others


def _remote(src, dst, ssem, rsem, dev):
    return pltpu.make_async_remote_copy(src_ref=src, dst_ref=dst, send_sem=ssem, recv_sem=rsem,
                                        device_id=dev, device_id_type=MESH)


def _half(ref_rows, c):
    hr = ref_rows // 2
    return pl.ds(pl.multiple_of(c * hr, 16), hr)


HBM = pl.BlockSpec(memory_space=pltpu.HBM)
SEMS = pl.BlockSpec(memory_space=pltpu.SEMAPHORE)
EFFECT = pltpu.SideEffectType.DATAFLOW_SIDE_EFFECTING


def exchange_begin(name, srcs, land_shapes, plan, ncopies, after=None):
    ns, nl = len(srcs), len(land_shapes)
    nin = ns + nl + (0 if after is None else 1)

    def body(*refs):
        ssem, rsem = refs[nin], refs[nin + 1]
        for k, (s, d, dev) in enumerate(plan(refs[:ns], refs[ns:ns + nl])):
            _remote(s, d, ssem.at[k], rsem.at[k], dev).start()
        refs[-1][...] = jnp.zeros_like(refs[-1])

    bufs = [pltpu.HBM(s.shape, s.dtype) for s in srcs] + [pltpu.HBM(s.shape, s.dtype) for s in land_shapes]
    args = [pltpu.with_memory_space_constraint(s, pltpu.HBM) for s in srcs]
    args += [pltpu.with_memory_space_constraint(lax.empty(s.shape, s.dtype), pltpu.HBM) for s in land_shapes]
    if after is not None:
        args.append(after)
    out = pl.pallas_call(
        body, name=name,
        out_shape=(pltpu.SemaphoreType.DMA((ncopies,)), pltpu.SemaphoreType.DMA((ncopies,)), *bufs,
                   jax.ShapeDtypeStruct((8, LANES), F32)),
        in_specs=[HBM] * (ns + nl) + ([] if after is None else [ANY]),
        out_specs=(SEMS, SEMS, *([HBM] * (ns + nl)), pl.BlockSpec(memory_space=pltpu.VMEM)),
        input_output_aliases={i: 2 + i for i in range(ns + nl)},
        compiler_params=pltpu.CompilerParams(has_side_effects=EFFECT),
    )(*args)
    return (out[0], out[1], out[2:2 + ns], out[2 + ns:2 + ns + nl]), out[-1]


def exchange_end(name, handle, plan, after):
    ssem, rsem, srcs, lands = handle
    ns, nl = len(srcs), len(lands)

    def body(*refs):
        ssem_ref, rsem_ref = refs[ns + nl], refs[ns + nl + 1]
        for k, (s, d, dev) in enumerate(plan(refs[:ns], refs[ns:ns + nl])):
            cp = _remote(s, d, ssem_ref.at[k], rsem_ref.at[k], dev)
            cp.wait_send()
            cp.wait_recv()

    out = pl.pallas_call(
        body, name=name,
        out_shape=tuple(pltpu.HBM(s.shape, s.dtype) for s in (*srcs, *lands)),
        in_specs=[HBM] * (ns + nl) + [SEMS, SEMS, ANY], out_specs=tuple([HBM] * (ns + nl)),
        input_output_aliases={i: i for i in range(ns + nl)},
        compiler_params=pltpu.CompilerParams(has_side_effects=EFFECT),
    )(*srcs, *lands, ssem, rsem, after)
    return list(out[:ns]), list(out[ns:])


def ag_plan(src_refs, land_refs):
    x, y, c, others = _place()
    plan = []
    for s, d in zip(src_refs, land_refs):
        mine = _half(s.shape[0], c)
        for ox, oy in others:
            plan.append((s.at[mine, :], d.at[2 * x + y, mine, :], (ox, oy, c)))
    return plan


def ag_forward(lands, *, name):
    n = len(lands)

    def body(*refs):
        ins, outs = refs[:n], refs[n:2 * n]
        ssem, rsem = refs[2 * n:]
        x, y, c, others = _place()
        sent = []
        for i in range(n):
            mine = _half(ins[i].shape[1], c)
            for j, (ox, oy) in enumerate(others):
                cp = _remote(ins[i].at[2 * ox + oy, mine, :], outs[i].at[2 * ox + oy, mine, :], ssem.at[3 * i + j],
                             rsem.at[3 * i + j], (x, y, 1 - c))
                cp.start()
                sent.append(cp)
        for cp in sent:
            cp.wait()

    return pl.pallas_call(
        body, name=name, in_specs=[ANY] * n, out_specs=[ANY] * n,
        out_shape=[jax.ShapeDtypeStruct(a.shape, a.dtype) for a in lands],
        input_output_aliases={i: i for i in range(n)},
        scratch_shapes=[pltpu.SemaphoreType.DMA((3 * n,)), pltpu.SemaphoreType.DMA((3 * n,))],
        compiler_params=pltpu.CompilerParams(has_side_effects=True),
    )(*lands)


def place_own(lands, own, chip, *, name):
    n = len(lands)
    steps = 4

    def body(chip_ref, *refs):
        for i in range(n):
            refs[2 * n + i][0] = refs[i][...]

    in_specs = [pl.BlockSpec((o.shape[0] // steps, o.shape[1]), lambda t, q: (t, 0)) for o in own] + [ANY] * n
    out_specs = [pl.BlockSpec((1, o.shape[0] // steps, o.shape[1]), lambda t, q: (q[0], t, 0)) for o in own]
    return pl.pallas_call(
        body, name=name,
        grid_spec=pltpu.PrefetchScalarGridSpec(num_scalar_prefetch=1, grid=(steps,), in_specs=in_specs,
                                               out_specs=out_specs),
        out_shape=[jax.ShapeDtypeStruct(a.shape, a.dtype) for a in lands],
        input_output_aliases={1 + n + i: i for i in range(n)},
        compiler_params=_cparams(("parallel",)),
    )(chip, *own, *lands)


def rs_swap_halves(grads, *, name):
    n = len(grads)

    def body(*refs):
        ins, outs = refs[:n], refs[n:2 * n]
        ssem, rsem = refs[2 * n:]
        x, y, c, _ = _place()
        cps = []
        for i in range(n):
            theirs = _half(ins[i].shape[1], 1 - c)
            cp = _remote(ins[i].at[:, theirs, :], outs[i], ssem.at[i], rsem.at[i], (x, y, 1 - c))
            cp.start()
            cps.append(cp)
        for cp in cps:
            cp.wait()

    return pl.pallas_call(
        body, name=name, in_specs=[ANY] * n, out_specs=[ANY] * n,
        out_shape=[jax.ShapeDtypeStruct((4, g.shape[1] // 2, g.shape[2]), g.dtype) for g in grads],
        scratch_shapes=[pltpu.SemaphoreType.DMA((n,)), pltpu.SemaphoreType.DMA((n,))],
        compiler_params=pltpu.CompilerParams(has_side_effects=True),
    )(*grads)


def rs_chip_sum(g, r1, core, *, name):
    _, r, cdim = g.shape
    hr = r // 2

    def body(c_ref, g_ref, r1_ref, o_ref):
        o_ref[...] = (g_ref[...].astype(F32) + r1_ref[...].astype(F32)).astype(BF16)

    return pl.pallas_call(
        body, name=name,
        grid_spec=pltpu.PrefetchScalarGridSpec(
            num_scalar_prefetch=1, grid=(4,),
            in_specs=[pl.BlockSpec((1, hr, cdim), lambda qq, c_ref: (qq, c_ref[0], 0)),
                      pl.BlockSpec((1, hr, cdim), lambda qq, c_ref: (qq, 0, 0))],
            out_specs=pl.BlockSpec((1, hr, cdim), lambda qq, c_ref: (qq, 0, 0))),
        out_shape=jax.ShapeDtypeStruct((4, hr, cdim), BF16),
        compiler_params=_cparams(("parallel",)),
    )(core, g, r1)


def rs_plan(src_refs, land_refs):
    x, y, c, others = _place()
    plan = []
    for s, d in zip(src_refs, land_refs):
        for j, (ox, oy) in enumerate(others):
            plan.append((s.at[2 * ox + oy], d.at[j], (ox, oy, c)))
    return plan


def rs_final_sum(g, r1, r2, place, acc, l, *, name):
    _, r, cdim = g.shape
    hr = r // 2
    ch = hr // 2

    def body(p_ref, g_ref, r1_ref, a_ref, b_ref, d_ref, acc_in, o_ref):
        s = g_ref[...].astype(F32) + r1_ref[...].astype(F32)
        s = s + a_ref[...].astype(F32)
        s = s + b_ref[...].astype(F32)
        o_ref[...] = s + d_ref[...].astype(F32)

    other = lambda j: pl.BlockSpec((1, ch, cdim), lambda t, p_ref: (j, t, 0))
    return pl.pallas_call(
        body, name=name,
        grid_spec=pltpu.PrefetchScalarGridSpec(
            num_scalar_prefetch=1, grid=(2,),
            in_specs=[pl.BlockSpec((1, ch, cdim), lambda t, p_ref: (p_ref[0], 2 * p_ref[1] + t, 0)),
                      pl.BlockSpec((1, ch, cdim), lambda t, p_ref: (p_ref[0], t, 0)),
                      other(0), other(1), other(2), ANY],
            out_specs=pl.BlockSpec((1, ch, cdim), lambda t, p_ref: (l, 2 * p_ref[1] + t, 0))),
        out_shape=jax.ShapeDtypeStruct(acc.shape, F32),
        input_output_aliases={6: 0},
        compiler_params=_cparams(("parallel",)),
    )(place, g, r1, r2, r2, r2, acc)


def rs_join_halves(grads, *, name):
    n = len(grads)

    def body(*refs):
        ins, outs = refs[:n], refs[n:2 * n]
        ssem, rsem = refs[2 * n:]
        x, y, c, _ = _place()
        cps = []
        for i in range(n):
            mine = _half(ins[i].shape[1], c)
            cp = _remote(ins[i].at[:, mine, :], outs[i].at[:, mine, :], ssem.at[i], rsem.at[i], (x, y, 1 - c))
            cp.start()
            cps.append(cp)
        for cp in cps:
            cp.wait()

    return pl.pallas_call(
        body, name=name, in_specs=[ANY] * n, out_specs=[ANY] * n,
        out_shape=[jax.ShapeDtypeStruct(g.shape, g.dtype) for g in grads],
        input_output_aliases={i: i for i in range(n)},
        scratch_shapes=[pltpu.SemaphoreType.DMA((n,)), pltpu.SemaphoreType.DMA((n,))],
        compiler_params=pltpu.CompilerParams(has_side_effects=True),
    )(*grads)


def all_reduce_small(v):
    R = v.shape[0]

    def body(v_ref, o_ref, buf, ssem, rsem):
        x, y, c, _ = _place()
        me = 4 * x + 2 * y + c
        buf[me] = v_ref[...]
        cps = []
        for k in range(1, 8):
            fx, fy, fc = (k >> 2) & 1, (k >> 1) & 1, k & 1
            px = jnp.where(fx == 1, 1 - x, x)
            py = jnp.where(fy == 1, 1 - y, y)
            pc = jnp.where(fc == 1, 1 - c, c)
            cp = _remote(v_ref, buf.at[me], ssem.at[k - 1], rsem.at[k - 1], (px, py, pc))
            cp.start()
            cps.append(cp)
        for cp in cps:
            cp.wait()
        acc = buf[0]
        for d in range(1, 8):
            acc = acc + buf[d]
        o_ref[...] = acc

    vm = pl.BlockSpec(memory_space=pltpu.VMEM)
    return pl.pallas_call(
        body, name="all_reduce_small", in_specs=[vm], out_specs=vm,
        out_shape=jax.ShapeDtypeStruct((R, LANES), F32),
        scratch_shapes=[pltpu.VMEM((8, R, LANES), F32), pltpu.SemaphoreType.DMA((7,)), pltpu.SemaphoreType.DMA((7,))],
        compiler_params=pltpu.CompilerParams(vmem_limit_bytes=VMEM_LIMIT, has_side_effects=True),
    )(v)


def _to_stacked(name, full):
    R, C = full.shape
    if name in ROW_SHARDED:
        return full.reshape(4, R // 4, C)
    return jnp.transpose(full.reshape(R, 4, C // 4), (1, 0, 2))


def _from_stacked(name, st):
    _, r, c = st.shape
    if name in ROW_SHARDED:
        return st.reshape(4 * r, c)
    return jnp.transpose(st, (1, 0, 2)).reshape(r, 4 * c)


UP_PIECES = ("ffn1_up", "ffn2_up")


def _layer_weights(lands):
    w = {n: lands[n] if n in UP_PIECES else _from_stacked(n, lands[n]) for n in lands}
    if "w_in" not in w:
        return w
    win = w.pop("w_in")
    D = win.shape[0]
    p0, p1, p2, p3 = POOL_DIM, POOL_DIM + Q_LORA_RANK, POOL_DIM + Q_LORA_RANK + KV_LORA_RANK, \
        POOL_DIM + Q_LORA_RANK + KV_LORA_RANK + QK_ROPE_DIM
    w["w_pool"] = win[:, :p0]
    w["w_lat"] = jnp.concatenate([win[:, p0:p3], jnp.zeros((D, LAT_DIM - (p3 - p0)), win.dtype)], axis=1)
    w["w_gate"] = win[:, p3:]
    uq = w["w_uq"].reshape(Q_LORA_RANK, N_HEADS, QK_DIM)
    w["w_uq"] = jnp.concatenate([uq, jnp.zeros((Q_LORA_RANK, N_HEADS, HEAD_PAD - QK_DIM), uq.dtype)],
                                axis=2).reshape(Q_LORA_RANK, N_HEADS * HEAD_PAD)
    return w


def _layer_grads_stacked(dw):
    dw = dict(dw)
    if "w_lat" in dw:
        lat = dw.pop("w_lat")
        dw["w_in"] = jnp.concatenate([dw.pop("w_pool"), lat[:, :Q_LORA_RANK + KV_LORA_RANK + QK_ROPE_DIM],
                                      dw.pop("w_gate")], axis=1)
        dw["w_uq"] = dw["w_uq"].reshape(Q_LORA_RANK, N_HEADS, HEAD_PAD)[:, :, :QK_DIM].reshape(Q_LORA_RANK,
                                                                                                 N_HEADS * QK_DIM)
    return {n: dw[n] if n in UP_PIECES else _to_stacked(n, dw[n]) for n in BIG if n in dw}


def _rope_tables(positions):
    inv_freq = ROPE_THETA ** (-jnp.arange(0, QK_ROPE_DIM, 2, dtype=F32) / QK_ROPE_DIM)
    ang = positions.astype(F32).reshape(-1)[:, None] * inv_freq
    cos, sin = jnp.cos(ang), jnp.sin(ang)
    z = jnp.zeros((ang.shape[0], LANES - QK_ROPE_DIM), F32)
    return jnp.concatenate([cos, cos, z], axis=1), jnp.concatenate([-sin, sin, z], axis=1)


def _pack_small(vals):
    parts = []
    for n in SMALL:
        f = vals[n].reshape(-1).astype(F32)
        pad = (-f.shape[0]) % (8 * LANES)
        parts.append(jnp.pad(f, (0, pad)))
    return jnp.concatenate(parts).reshape(-1, LANES)


def _unpack_small(packed, like):
    flat = packed.reshape(-1)
    out, off = {}, 0
    for n in SMALL:
        size = like[n].size
        out[n] = flat[off:off + size].reshape(like[n].shape)
        off += size + ((-size) % (8 * LANES))
    return out


def _ffn_fwd(x, g, wu4, wd, tag, dep=None):
    h = rms_fwd(x, g, dep=dep, name=f"{tag}_norm")
    gate, up, a = ffn_up_act(h, wu4, name=f"{tag}_up_act")
    y = mm(a, wd, res=x, alpha=0.5, name=f"{tag}_down")
    return y, (x, h, gate, up, a)


def _ffn_bwd(dy, saved, g, wu4, wd, tag, dep=None):
    x, h, gate, up, a = saved
    dgate, dup = ffn_down_dx_act(dy, wd, gate, up, dep=dep, name=f"{tag}_down_dx_act")
    dwd = mm(a, dy, ta=True, alpha=0.5, out_dtype=BF16, name=f"{tag}_down_dw")
    dwu4 = ffn_up_dw(h, dgate, dup, name=f"{tag}_up_dw")
    dh = ffn_up_dx(dgate, dup, wu4, name=f"{tag}_up_dx")
    dx, dg = rms_bwd(x, g, dh, dy, name=f"{tag}_norm_bwd")
    return dx, dg, dwu4, dwd


def _mix_fwd(x, p, w, cs, sn, B, dep=None):
    h = rms_fwd(x, p["norm_mix"], dep=dep, name="mix_norm")
    lat = mm(h, w["w_lat"], name="mix_lat")
    xp = mm(h, w["w_pool"], name="mix_pool_in")
    gl = mm(h, w["w_gate"], name="mix_gate_in")
    mixed = pool_fwd(xp, p["pool_maps"].astype(BF16), p["pool_scale"], B, name="pool_fwd")
    ba = mm(mixed, w["w_pool_proj"], name="mix_pool_proj")
    qn, kvn, kr = lat_fwd(lat, p["q_latent_norm"], p["kv_latent_norm"], cs, sn, name="lat_fwd")
    q = q_rope(mm(qn, w["w_uq"], name="mix_uq"), cs, sn, transpose=False, name="q_rope")
    kv = mm(kvn, w["w_ukv"], out_dtype=BF16, name="mix_ukv")
    o, lse = attn_fwd(q, kv, kr, B, name="attn_fwd")
    bb = mm(o, w["w_attn_proj"], name="mix_attn_proj")
    merged = gate_fwd(gl, p["b_gate"], ba, bb, name="gate_fwd")
    y = mm(merged, w["w_out"], res=x, name="mix_out")
    return y, (x, h, lat, xp, gl, mixed, ba, qn, kvn, kr, q, kv, o, lse, bb, merged)


def _mix_bwd(dy, saved, p, w, cs, sn, B):
    x, h, lat, xp, gl, mixed, ba, qn, kvn, kr, q, kv, o, lse, bb, merged = saved
    dw, ds = {}, {}
    dm = mm(dy, w["w_out"], tb=True, name="mix_out_dx")
    dw["w_out"] = mm(merged, dy, ta=True, out_dtype=BF16, name="mix_out_dw")
    dba, dbb, dgl, ds["b_gate"] = gate_bwd(dm, gl, p["b_gate"], ba, bb, name="gate_bwd")
    dw["w_attn_proj"] = mm(o, dbb, ta=True, out_dtype=BF16, name="mix_attn_proj_dw")
    do = mm(dbb, w["w_attn_proj"], tb=True, out_dtype=BF16, name="mix_attn_proj_dx")
    dw["w_pool_proj"] = mm(mixed, dba, ta=True, out_dtype=BF16, name="mix_pool_proj_dw")
    dmixed = mm(dba, w["w_pool_proj"], tb=True, name="mix_pool_proj_dx")
    dxp, ds["pool_maps"], ds["pool_scale"] = pool_bwd(xp, dmixed, p["pool_maps"].astype(BF16), p["pool_scale"], B,
                                                      name="pool_bwd")
    dq, dkv, dkr = attn_bwd(q, kv, kr, o, do, lse, B, name="attn_bwd")
    dw["w_ukv"] = mm(kvn, dkv, ta=True, out_dtype=BF16, name="mix_ukv_dw")
    dkvn = mm(dkv, w["w_ukv"], tb=True, name="mix_ukv_dx")
    dqb = q_rope(dq, cs, sn, transpose=True, name="q_rope_bwd")
    dw["w_uq"] = mm(qn, dqb, ta=True, out_dtype=BF16, name="mix_uq_dw")
    dqn = mm(dqb, w["w_uq"], tb=True, name="mix_uq_dx")
    dlat, ds["q_latent_norm"], ds["kv_latent_norm"] = lat_bwd(lat, p["q_latent_norm"], p["kv_latent_norm"], dqn, dkvn,
                                                               dkr, cs, sn, name="lat_bwd")
    dw["w_lat"] = mm(h, dlat, ta=True, out_dtype=BF16, name="mix_lat_dw")
    dw["w_pool"] = mm(h, dxp, ta=True, out_dtype=BF16, name="mix_pool_in_dw")
    dw["w_gate"] = mm(h, dgl, ta=True, out_dtype=BF16, name="mix_gate_in_dw")
    dh = mm(dlat, w["w_lat"], tb=True, name="mix_lat_dx")
    dh = mm(dxp, w["w_pool"], tb=True, res=dh, name="mix_pool_in_dx")
    dh = mm(dgl, w["w_gate"], tb=True, res=dh, name="mix_gate_in_dx")
    dx, ds["norm_mix"] = rms_bwd(x, p["norm_mix"], dh, dy, name="mix_norm_bwd")
    return dx, dw, ds


def kernel(x, positions, norm_ffn1, ffn1_up, ffn1_down, norm_mix, w_in, b_gate, pool_maps, pool_scale, w_pool_proj, q_latent_norm, w_uq, kv_latent_norm, w_ukv, w_attn_proj, w_out, norm_ffn2, ffn2_up, ffn2_down, final_norm, loss_target, m_norm_ffn1, m_ffn1_up, m_ffn1_down, m_norm_mix, m_w_in, m_b_gate, m_pool_maps, m_pool_scale, m_w_pool_proj, m_q_latent_norm, m_w_uq, m_kv_latent_norm, m_w_ukv, m_w_attn_proj, m_w_out, m_norm_ffn2, m_ffn2_up, m_ffn2_down, m_final_norm, v_norm_ffn1, v_ffn1_up, v_ffn1_down, v_norm_mix, v_w_in, v_b_gate, v_pool_maps, v_pool_scale, v_w_pool_proj, v_q_latent_norm, v_w_uq, v_kv_latent_norm, v_w_ukv, v_w_attn_proj, v_w_out, v_norm_ffn2, v_ffn2_up, v_ffn2_down, v_final_norm):
    given = dict(locals())
    B, S, D = x.shape
    T = B * S
    L = norm_ffn1.shape[0]
    W = {n: given[n] for n in WEIGHTS}
    Mo = {n: given["m_" + n] for n in WEIGHTS}
    Vo = {n: given["v_" + n] for n in WEIGHTS}
    core = lax.axis_index("c").astype(jnp.int32)
    chip = (2 * lax.axis_index("x") + lax.axis_index("y")).astype(jnp.int32)

    core_arr = core.reshape(1)
    chip_arr = chip.reshape(1)
    place = jnp.stack([chip, core])
    first = ("ffn1_up", "ffn1_down")
    rest = tuple(n for n in BIG if n not in first)

    own = [{n: W[n][l].astype(BF16) for n in BIG} for l in range(L)]

    def ag_begin(l, names, tag, after=None):
        lands = [jax.ShapeDtypeStruct((4,) + own[l][n].shape, BF16) for n in names]
        return exchange_begin(f"ag_start_{tag}", [own[l][n] for n in names], lands, ag_plan, 3 * len(names), after)

    def ag_finish(handle, names, tag, after):
        mine, lands = exchange_end(f"ag_wait_{tag}", handle, ag_plan, after)
        lands = ag_forward(lands, name=f"ag_forward_{tag}")
        return _layer_weights(dict(zip(names, place_own(lands, mine, chip_arr, name=f"place_own_{tag}"))))

    h_first, t1 = ag_begin(0, first, "0a")
    h_rest, t2 = ag_begin(0, rest, "0b")
    cs, sn = _rope_tables(positions)
    xs = x.reshape(T, D) + (t1[0, 0] + t2[0, 0])
    saved, handle = [], None
    for l in range(L):
        p = {n: W[n][l] for n in SMALL if n != "final_norm"}
        dep = None
        if l == 0:
            w = ag_finish(h_first, first, "0a", xs)
        else:
            w = ag_finish(handle, BIG, str(l), xs)
            if l + 1 < L:
                handle, dep = ag_begin(l + 1, BIG, str(l + 1), after=xs)
        xs, s1 = _ffn_fwd(xs, p["norm_ffn1"], w["ffn1_up"], w["ffn1_down"], "ffn1", dep=dep)
        dep = None
        if l == 0:
            w.update(ag_finish(h_rest, rest, "0b", xs))
            if L > 1:
                handle, dep = ag_begin(1, BIG, "1", after=xs)
        xs, s2 = _mix_fwd(xs, p, w, cs, sn, B, dep=dep)
        xs, s3 = _ffn_fwd(xs, p["norm_ffn2"], w["ffn2_up"], w["ffn2_down"], "ffn2")
        saved.append((w, p, s1, s2, s3))

    dx, dfinal, loss_tile = loss_head(xs, final_norm, loss_target.reshape(T, D), name="loss_head")
    loss = lax.psum(loss_tile[0, 0], ("x", "y", "c"))

    def rs_begin(dw, tag):
        stacked = _layer_grads_stacked(dw)
        names = tuple(stacked)
        parts = [stacked[n] for n in names]
        r1 = rs_swap_halves(parts, name=f"rs_swap_{tag}")
        sums = [rs_chip_sum(g, a, core_arr, name=f"rs_chip_sum_{n}") for n, g, a in zip(names, parts, r1)]
        lands = [jax.ShapeDtypeStruct((3,) + s.shape[1:], BF16) for s in sums]
        handle, token = exchange_begin(f"rs_start_{tag}", sums, lands, rs_plan, 3 * len(names))
        return (names, parts, r1, handle), token

    acc = {n: lax.empty(W[n].shape, F32) for n in BIG}

    def rs_finish(l, pending, tag, after):
        names, parts, r1, handle = pending
        _, r2 = exchange_end(f"rs_wait_{tag}", handle, rs_plan, after)
        for n, g, a, b in zip(names, parts, r1, r2):
            acc[n] = rs_final_sum(g, a, b, place, acc[n], l, name=f"rs_final_sum_{n}")

    small_layers, pending, dep = [], [], None
    for l in reversed(range(L)):
        w, p, s1, s2, s3 = saved[l]
        dx, dg2, dwu2, dwd2 = _ffn_bwd(dx, s3, p["norm_ffn2"], w["ffn2_up"], w["ffn2_down"], "ffn2", dep=dep)
        dx, dw, ds = _mix_bwd(dx, s2, p, w, cs, sn, B)
        dw.update(ffn2_up=dwu2, ffn2_down=dwd2)
        dep = None
        if l == 0:
            early, dep = rs_begin(dw, "0b")
            dw = {}
        dx, dg1, dwu1, dwd1 = _ffn_bwd(dx, s1, p["norm_ffn1"], w["ffn1_up"], w["ffn1_down"], "ffn1", dep=dep)
        dw.update(ffn1_up=dwu1, ffn1_down=dwd1)
        ds.update(norm_ffn1=dg1, norm_ffn2=dg2)
        small_layers.append(ds)
        last, dep = rs_begin(dw, "0a" if l == 0 else str(l))
        if l > 0:
            pending.append((l, last))
    small_layers.reverse()

    small = {n: jnp.stack([small_layers[l][n].reshape(W[n].shape[1:]) for l in range(L)]) for n in SMALL
             if n != "final_norm"}
    small["final_norm"] = dfinal.reshape(final_norm.shape)
    grads = _unpack_small(all_reduce_small(_pack_small(small)), small)
    delta, new_m, new_v = {}, {}, {}
    d, nm, nv = adamw(_pack_small(W), _pack_small(grads), _pack_small(Mo), _pack_small(Vo), name="adamw_small")
    delta.update(_unpack_small(d, W))
    new_m.update(_unpack_small(nm, W))
    new_v.update(_unpack_small(nv, W))

    def update(names, tag):
        joined = rs_join_halves([acc[n] for n in names], name=f"rs_join_{tag}")
        for n, g in zip(names, joined):
            sh = W[n].shape
            two = lambda a: a.reshape(sh[0] * sh[1], sh[2])
            d, nm, nv = adamw(two(W[n]), two(g), two(Mo[n]), two(Vo[n]), name=f"adamw_{n}")
            grads[n], delta[n], new_m[n], new_v[n] = g, d.reshape(sh), nm.reshape(sh), nv.reshape(sh)
        return d

    for l, item in pending:
        rs_finish(l, item, str(l), d)
    rs_finish(0, early, "0b", d)
    d = update(rest, "rest")
    rs_finish(0, last, "0a", d)
    update(first, "first")

    return (loss, dx.reshape(B, S, D), *[grads[n] for n in WEIGHTS], *[delta[n] for n in WEIGHTS],
            *[new_m[n] for n in WEIGHTS], *[new_v[n] for n in WEIGHTS])
```

```python
import functools

import jax
import jax.numpy as jnp
from jax import lax
from jax.experimental import pallas as pl
from jax.experimental.pallas import tpu as pltpu

F32 = jnp.float32
BF16 = jnp.bfloat16

N_HEADS = 8
QK_NOPE_DIM = 128
QK_ROPE_DIM = 64
QK_DIM = QK_NOPE_DIM + QK_ROPE_DIM
V_HEAD_DIM = 128
HEAD_PAD = 256
Q_LORA_RANK = 384
KV_LORA_RANK = 256
ROPE_THETA = 10000.0
POOL_WINDOWS = (2, 4, 8, 16)
N_POOL_GROUPS = 4
POOL_GROUP_DIM = 128
POOL_DIM = N_POOL_GROUPS * POOL_GROUP_DIM
LAT_DIM = 768
NORM_EPS = 1e-6
ADAM_LR = 0.001
ADAM_B1 = 0.9
ADAM_B2 = 0.999
ADAM_EPS = 1e-08
ADAM_WD = 0.01
ADAM_STEP = 10
NEG_INF = -1e30
LANES = 128
ATT_BLOCK = 512
VMEM_LIMIT = 48 * 1024 * 1024
MESH = pl.DeviceIdType.MESH
_NT = (((1,), (1,)), ((), ()))
_TN = (((0,), (0,)), ((), ()))

BIG = ("ffn1_up", "ffn1_down", "w_in", "w_pool_proj", "w_uq", "w_ukv", "w_attn_proj", "w_out",
       "ffn2_up", "ffn2_down")
ROW_SHARDED = ("ffn1_down", "w_attn_proj", "w_out", "ffn2_down")
SMALL = ("norm_ffn1", "norm_mix", "b_gate", "pool_maps", "pool_scale", "q_latent_norm",
         "kv_latent_norm", "norm_ffn2", "final_norm")
WEIGHTS = ("norm_ffn1", "ffn1_up", "ffn1_down", "norm_mix", "w_in", "b_gate", "pool_maps", "pool_scale",
           "w_pool_proj", "q_latent_norm", "w_uq", "kv_latent_norm", "w_ukv", "w_attn_proj", "w_out",
           "norm_ffn2", "ffn2_up", "ffn2_down", "final_norm")


def _pick(dim, cands):
    for c in cands:
        if c <= dim and dim % c == 0:
            return c
    return dim


def _cparams(sem=None, **kw):
    if sem is not None:
        kw["dimension_semantics"] = sem
    return pltpu.CompilerParams(vmem_limit_bytes=VMEM_LIMIT, **kw)


def _sigmoid(x):
    return 1.0 / (1.0 + jnp.exp(-x))


MM_TILE_BUDGET = 30 * 1024 * 1024
TILE_SIZES = (1408, 1024, 768, 512, 384, 256, 128)


V7X_MXU_FLOPS = 9.0e14
V7X_HBM_BYTES = 2.5e12
GRID_STEP_S = 0.35e-6


def _mm_tiles(M, N, K, sa, sb, so, sr):
    tks = [K] if K <= 2816 else [t for t in (2816, 2048, 1408, 1024, 512, 256, 128) if K % t == 0]
    best = None
    for tk in tks:
        for tm in [t for t in TILE_SIZES if M % t == 0] or [M]:
            for tn in [t for t in TILE_SIZES if N % t == 0] or [N]:
                need = 2 * (tm * tk * sa + tk * tn * sb + tm * tn * (so + sr)) + (tm * tn * 4 if tk < K else 0)
                if need > MM_TILE_BUDGET:
                    continue
                ni, nj, nk = M // tm, N // tn, K // tk
                a_bytes = M * K * sa * (nj if nk > 1 else 1)
                b_bytes = K * N * sb * (1 if nj == 1 and nk == 1 else ni)
                traffic = a_bytes + b_bytes + M * N * (so + sr) + (M * N * 8 * nk if nk > 1 else 0)
                t = max(2.0 * M * N * K / V7X_MXU_FLOPS, traffic / V7X_HBM_BYTES) + ni * nj * nk * GRID_STEP_S
                if best is None or t < best[0]:
                    best = (t, (tm, tn, tk))
    assert best is not None, (M, N, K)
    return best[1]


def mm(a, b, *, name, ta=False, tb=False, out_dtype=F32, res=None, alpha=1.0, dep=None):
    if ta:
        K, M = a.shape
    else:
        M, K = a.shape
    if tb:
        N, K2 = b.shape
    else:
        K2, N = b.shape
    assert K == K2, (a.shape, b.shape, ta, tb)
    tm, tn, tk = _mm_tiles(M, N, K, a.dtype.itemsize, b.dtype.itemsize, jnp.dtype(out_dtype).itemsize,
                           0 if res is None else res.dtype.itemsize)
    nk = K // tk
    dims = (((0 if ta else 1,), (1 if tb else 0,)), ((), ()))

    def body(*refs):
        a_ref, b_ref = refs[:2]
        r_ref = refs[2] if res is not None else None
        o_ref = refs[-2] if nk > 1 else refs[-1]

        def finish(r):
            if alpha != 1.0:
                r = r * alpha
            if res is not None:
                r = r_ref[...].astype(F32) + r
            o_ref[...] = r.astype(out_dtype)

        part = lax.dot_general(a_ref[...].astype(BF16), b_ref[...].astype(BF16), dims, preferred_element_type=F32)
        if nk == 1:
            finish(part)
            return
        acc = refs[-1]
        k = pl.program_id(2)

        @pl.when(k == 0)
        def _():
            acc[...] = part

        @pl.when(k > 0)
        def _():
            acc[...] += part

        @pl.when(k == nk - 1)
        def _():
            finish(acc[...])

    a_spec = pl.BlockSpec((tk, tm), lambda i, j, k: (k, i)) if ta else pl.BlockSpec((tm, tk), lambda i, j, k: (i, k))
    b_spec = pl.BlockSpec((tn, tk), lambda i, j, k: (j, k)) if tb else pl.BlockSpec((tk, tn), lambda i, j, k: (k, j))
    o_spec = pl.BlockSpec((tm, tn), lambda i, j, k: (i, j))
    in_specs = [a_spec, b_spec]
    args = [a, b]
    if res is not None:
        in_specs.append(o_spec)
        args.append(res)
    if dep is not None:
        in_specs.append(pl.BlockSpec((8, LANES), lambda i, j, k: (0, 0)))
        args.append(dep)
    return pl.pallas_call(
        body, name=name, grid=(M // tm, N // tn, nk), in_specs=in_specs, out_specs=o_spec,
        out_shape=jax.ShapeDtypeStruct((M, N), out_dtype),
        scratch_shapes=[pltpu.VMEM((tm, tn), F32)] if nk > 1 else [],
        compiler_params=_cparams(("parallel", "parallel", "arbitrary")),
    )(*args)


MXU_COLS = 256


def _col_chunks(n):
    return [(lo, min(lo + MXU_COLS, n)) for lo in range(0, n, MXU_COLS)]


def ffn_up_act(h, wu4, *, name):
    T, D = h.shape
    cq = wu4.shape[2]
    Fh = 2 * cq
    tm = _pick(T, (512, 256, 128))

    def body(h_ref, wg_ref, wu_ref, g_ref, u_ref, a_ref):
        hv = h_ref[...]
        for lo, hi in _col_chunks(cq):
            gv = jnp.dot(hv, wg_ref[0, :, lo:hi], preferred_element_type=F32)
            uv = jnp.dot(hv, wu_ref[0, :, lo:hi], preferred_element_type=F32)
            g_ref[:, lo:hi] = gv.astype(BF16)
            u_ref[:, lo:hi] = uv.astype(BF16)
            a_ref[:, lo:hi] = (gv * _sigmoid(gv) * uv).astype(BF16)

    tile = pl.BlockSpec((tm, cq), lambda j, i: (i, j))
    sh = jax.ShapeDtypeStruct((T, Fh), BF16)
    return pl.pallas_call(
        body, name=name, grid=(2, T // tm),
        in_specs=[pl.BlockSpec((tm, D), lambda j, i: (i, 0)), pl.BlockSpec((1, D, cq), lambda j, i: (j, 0, 0)),
                  pl.BlockSpec((1, D, cq), lambda j, i: (2 + j, 0, 0))],
        out_specs=[tile, tile, tile], out_shape=[sh, sh, sh],
        compiler_params=_cparams(("parallel", "parallel")),
    )(h, wu4, wu4)


def ffn_down_dx_act(dy, wd, g, u, *, dep=None, name):
    T, D = dy.shape
    Fh = wd.shape[0]
    cq = Fh // 2
    tm = _pick(T, (512, 256, 128))

    def body(dy_ref, wd_ref, g_ref, u_ref, *rest):
        dg_ref, du_ref = rest[-2:]
        dyv = dy_ref[...].astype(BF16)
        for lo, hi in _col_chunks(cq):
            da = 0.5 * lax.dot_general(dyv, wd_ref[lo:hi, :], _NT, preferred_element_type=F32)
            gv = g_ref[:, lo:hi].astype(F32)
            uv = u_ref[:, lo:hi].astype(F32)
            s = _sigmoid(gv)
            dg_ref[:, lo:hi] = (da * uv * (s * (1.0 + gv * (1.0 - s)))).astype(BF16)
            du_ref[:, lo:hi] = (da * (gv * s)).astype(BF16)

    tile = pl.BlockSpec((tm, cq), lambda j, i: (i, j))
    sh = jax.ShapeDtypeStruct((T, Fh), BF16)
    in_specs = [pl.BlockSpec((tm, D), lambda j, i: (i, 0)), pl.BlockSpec((cq, D), lambda j, i: (j, 0)), tile, tile]
    args = [dy, wd, g, u]
    if dep is not None:
        in_specs.append(pl.BlockSpec((8, LANES), lambda j, i: (0, 0)))
        args.append(dep)
    return pl.pallas_call(
        body, name=name, grid=(2, T // tm), in_specs=in_specs, out_specs=[tile, tile], out_shape=[sh, sh],
        compiler_params=_cparams(("parallel", "parallel")),
    )(*args)


def ffn_up_dw(h, dg, du, *, name):
    T, D = h.shape
    cq = dg.shape[1] // 2
    tk = _pick(T, (1024, 512, 256, 128))
    nk = T // tk

    def body(h_ref, dg_ref, du_ref, o_ref, acc):
        p = pl.program_id(0)
        k = pl.program_id(1)

        @pl.when(k == 0)
        def _():
            acc[...] = jnp.zeros_like(acc)

        @pl.when(p < 2)
        def _():
            acc[...] += lax.dot_general(h_ref[...], dg_ref[...], _TN, preferred_element_type=F32)

        @pl.when(p >= 2)
        def _():
            acc[...] += lax.dot_general(h_ref[...], du_ref[...], _TN, preferred_element_type=F32)

        @pl.when(k == nk - 1)
        def _():
            o_ref[0] = acc[...].astype(BF16)

    return pl.pallas_call(
        body, name=name, grid=(4, nk),
        in_specs=[pl.BlockSpec((tk, D), lambda p, k: (k, 0)),
                  pl.BlockSpec((tk, cq), lambda p, k: (jnp.where(p < 2, k, 0), jnp.minimum(p, 1))),
                  pl.BlockSpec((tk, cq), lambda p, k: (jnp.where(p < 2, 0, k), jnp.maximum(p - 2, 0)))],
        out_specs=pl.BlockSpec((1, D, cq), lambda p, k: (p, 0, 0)),
        out_shape=jax.ShapeDtypeStruct((4, D, cq), BF16),
        scratch_shapes=[pltpu.VMEM((D, cq), F32)],
        compiler_params=_cparams(("parallel", "arbitrary")),
    )(h, dg, du)


def ffn_up_dx(dg, du, wu4, *, name):
    T = dg.shape[0]
    _, D, cq = wu4.shape
    tm = _pick(T, (1024, 512, 256, 128))

    def body(dg_ref, du_ref, wg_ref, wu_ref, o_ref, acc):
        k = pl.program_id(1)
        part = lax.dot_general(dg_ref[...], wg_ref[0], _NT, preferred_element_type=F32)
        part = part + lax.dot_general(du_ref[...], wu_ref[0], _NT, preferred_element_type=F32)

        @pl.when(k == 0)
        def _():
            acc[...] = part

        @pl.when(k == 1)
        def _():
            o_ref[...] = acc[...] + part

    tile = pl.BlockSpec((tm, cq), lambda i, k: (i, k))
    return pl.pallas_call(
        body, name=name, grid=(T // tm, 2),
        in_specs=[tile, tile, pl.BlockSpec((1, D, cq), lambda i, k: (k, 0, 0)),
                  pl.BlockSpec((1, D, cq), lambda i, k: (2 + k, 0, 0))],
        out_specs=pl.BlockSpec((tm, D), lambda i, k: (i, 0)),
        out_shape=jax.ShapeDtypeStruct((T, D), F32),
        scratch_shapes=[pltpu.VMEM((tm, D), F32)],
        compiler_params=_cparams(("parallel", "arbitrary")),
    )(dg, du, wu4, wu4)


def _rows(T, width_bytes):
    cap = max(8, (2 * 1024 * 1024) // width_bytes)
    return _pick(T, tuple(c for c in (1024, 512, 256, 128, 64, 32, 16) if c <= cap))


def rms_fwd(x, g, *, name, dep=None):
    T, D = x.shape
    tm = _rows(T, D * 4)

    def body(x_ref, g_ref, *rest):
        xv = x_ref[...]
        r = lax.rsqrt(jnp.mean(xv * xv, axis=-1, keepdims=True) + NORM_EPS)
        rest[-1][...] = (xv * r * g_ref[...]).astype(BF16)

    in_specs = [pl.BlockSpec((tm, D), lambda i: (i, 0)), pl.BlockSpec((1, D), lambda i: (0, 0))]
    args = [x, g.reshape(1, D)]
    if dep is not None:
        in_specs.append(pl.BlockSpec((8, LANES), lambda i: (0, 0)))
        args.append(dep)
    return pl.pallas_call(
        body, name=name, grid=(T // tm,), in_specs=in_specs,
        out_specs=pl.BlockSpec((tm, D), lambda i: (i, 0)),
        out_shape=jax.ShapeDtypeStruct((T, D), BF16),
        compiler_params=_cparams(("parallel",)),
    )(*args)


def _rms_bwd_math(xv, gv, dh):
    r = lax.rsqrt(jnp.mean(xv * xv, axis=-1, keepdims=True) + NORM_EPS)
    xn = xv * r
    dg = jnp.sum(dh * xn, axis=0, keepdims=True)
    dxn = dh * gv
    dx = r * (dxn - xn * jnp.mean(dxn * xn, axis=-1, keepdims=True))
    return dx, dg


def rms_bwd(x, g, dh, dres, *, name):
    T, D = x.shape
    tm = _rows(T, D * 4)

    def body(x_ref, g_ref, dh_ref, dres_ref, dx_ref, dg_ref):
        @pl.when(pl.program_id(0) == 0)
        def _():
            dg_ref[...] = jnp.zeros_like(dg_ref)

        dx, dg = _rms_bwd_math(x_ref[...], g_ref[...], dh_ref[...].astype(F32))
        dx_ref[...] = dres_ref[...] + dx
        dg_ref[...] += dg

    row = pl.BlockSpec((tm, D), lambda i: (i, 0))
    vec = pl.BlockSpec((1, D), lambda i: (0, 0))
    return pl.pallas_call(
        body, name=name, grid=(T // tm,), in_specs=[row, vec, row, row], out_specs=[row, vec],
        out_shape=[jax.ShapeDtypeStruct((T, D), F32), jax.ShapeDtypeStruct((1, D), F32)],
        compiler_params=_cparams(("arbitrary",)),
    )(x, g.reshape(1, D), dh, dres)


def _rope(xv, cv, sv):
    half = QK_ROPE_DIM // 2
    lane = lax.broadcasted_iota(jnp.int32, xv.shape, 1)
    swapped = jnp.where(lane < half, pltpu.roll(xv, LANES - half, 1), pltpu.roll(xv, half, 1))
    return xv * cv + swapped * sv


def _rope_t(dy, cv, sv):
    half = QK_ROPE_DIM // 2
    ds = dy * sv
    lane = lax.broadcasted_iota(jnp.int32, dy.shape, 1)
    swapped = jnp.where(lane < half, pltpu.roll(ds, LANES - half, 1), pltpu.roll(ds, half, 1))
    return dy * cv + swapped


def lat_fwd(lat, qn_w, kvn_w, cs, sn, *, name):
    T = lat.shape[0]
    tm = _rows(T, LAT_DIM * 4)
    kv0 = Q_LORA_RANK
    kr0 = Q_LORA_RANK + KV_LORA_RANK

    def body(lat_ref, qw_ref, kw_ref, c_ref, s_ref, qn_ref, kvn_ref, kr_ref):
        ql = lat_ref[:, :kv0]
        r = lax.rsqrt(jnp.mean(ql * ql, axis=-1, keepdims=True) + NORM_EPS)
        qn_ref[...] = (ql * r * qw_ref[...]).astype(BF16)
        kl = lat_ref[:, kv0:kr0]
        r = lax.rsqrt(jnp.mean(kl * kl, axis=-1, keepdims=True) + NORM_EPS)
        kvn_ref[...] = (kl * r * kw_ref[...]).astype(BF16)
        kr_ref[...] = _rope(lat_ref[:, kr0:], c_ref[...], s_ref[...]).astype(BF16)

    return pl.pallas_call(
        body, name=name, grid=(T // tm,),
        in_specs=[pl.BlockSpec((tm, LAT_DIM), lambda i: (i, 0)),
                  pl.BlockSpec((1, Q_LORA_RANK), lambda i: (0, 0)),
                  pl.BlockSpec((1, KV_LORA_RANK), lambda i: (0, 0)),
                  pl.BlockSpec((tm, LANES), lambda i: (i, 0)), pl.BlockSpec((tm, LANES), lambda i: (i, 0))],
        out_specs=[pl.BlockSpec((tm, Q_LORA_RANK), lambda i: (i, 0)),
                   pl.BlockSpec((tm, KV_LORA_RANK), lambda i: (i, 0)),
                   pl.BlockSpec((tm, LANES), lambda i: (i, 0))],
        out_shape=[jax.ShapeDtypeStruct((T, Q_LORA_RANK), BF16), jax.ShapeDtypeStruct((T, KV_LORA_RANK), BF16),
                   jax.ShapeDtypeStruct((T, LANES), BF16)],
        compiler_params=_cparams(("parallel",)),
    )(lat, qn_w.reshape(1, -1), kvn_w.reshape(1, -1), cs, sn)


def lat_bwd(lat, qn_w, kvn_w, dqn, dkvn, dkr, cs, sn, *, name):
    T = lat.shape[0]
    tm = _rows(T, LAT_DIM * 4)
    kv0 = Q_LORA_RANK
    kr0 = Q_LORA_RANK + KV_LORA_RANK

    def body(lat_ref, qw_ref, kw_ref, dqn_ref, dkvn_ref, dkr_ref, c_ref, s_ref, dlat_ref, dqw_ref, dkw_ref):
        @pl.when(pl.program_id(0) == 0)
        def _():
            dqw_ref[...] = jnp.zeros_like(dqw_ref)
            dkw_ref[...] = jnp.zeros_like(dkw_ref)

        dx, dg = _rms_bwd_math(lat_ref[:, :kv0], qw_ref[...], dqn_ref[...])
        dlat_ref[:, :kv0] = dx.astype(BF16)
        dqw_ref[...] += dg
        dx, dg = _rms_bwd_math(lat_ref[:, kv0:kr0], kw_ref[...], dkvn_ref[...])
        dlat_ref[:, kv0:kr0] = dx.astype(BF16)
        dkw_ref[...] += dg
        dlat_ref[:, kr0:] = _rope_t(dkr_ref[...], c_ref[...], s_ref[...]).astype(BF16)

    row = lambda w: pl.BlockSpec((tm, w), lambda i: (i, 0))
    vec = lambda w: pl.BlockSpec((1, w), lambda i: (0, 0))
    return pl.pallas_call(
        body, name=name, grid=(T // tm,),
        in_specs=[row(LAT_DIM), vec(Q_LORA_RANK), vec(KV_LORA_RANK), row(Q_LORA_RANK), row(KV_LORA_RANK),
                  row(LANES), row(LANES), row(LANES)],
        out_specs=[row(LAT_DIM), vec(Q_LORA_RANK), vec(KV_LORA_RANK)],
        out_shape=[jax.ShapeDtypeStruct((T, LAT_DIM), BF16), jax.ShapeDtypeStruct((1, Q_LORA_RANK), F32),
                   jax.ShapeDtypeStruct((1, KV_LORA_RANK), F32)],
        compiler_params=_cparams(("arbitrary",)),
    )(lat, qn_w.reshape(1, -1), kvn_w.reshape(1, -1), dqn, dkvn, dkr, cs, sn)


def attn_proj_gate(o, wap, gl, bg, ba, *, name):
    T, D2 = gl.shape
    D = D2 // 2
    tm = _pick(T, (256, 128))

    def body(o_ref, w_ref, gl_ref, bg_ref, ba_ref, bb_ref, m_ref):
        bb = jnp.dot(o_ref[...], w_ref[...], preferred_element_type=F32)
        bb_ref[...] = bb
        ga = _sigmoid(gl_ref[:, :D] + bg_ref[:, :D])
        gb = _sigmoid(gl_ref[:, D:] + bg_ref[:, D:])
        m_ref[...] = (ga * ba_ref[...] + gb * bb).astype(BF16)

    row = lambda w: pl.BlockSpec((tm, w), lambda i: (i, 0))
    return pl.pallas_call(
        body, name=name, grid=(T // tm,),
        in_specs=[row(o.shape[1]), pl.BlockSpec(wap.shape, lambda i: (0, 0)), row(D2),
                  pl.BlockSpec((1, D2), lambda i: (0, 0)), row(D)],
        out_specs=[row(D), row(D)],
        out_shape=[jax.ShapeDtypeStruct((T, D), F32), jax.ShapeDtypeStruct((T, D), BF16)],
        compiler_params=_cparams(("parallel",)),
    )(o, wap, gl, bg.reshape(1, D2), ba)


def out_dx_gate(dy, wo, gl, bg, ba, bb, *, name):
    T, D2 = gl.shape
    D = D2 // 2
    tm = _pick(T, (256, 128))

    def body(dy_ref, w_ref, gl_ref, bg_ref, ba_ref, bb_ref, dba_ref, dbb_ref, dgl_ref, dbg_ref):
        @pl.when(pl.program_id(0) == 0)
        def _():
            dbg_ref[...] = jnp.zeros_like(dbg_ref)

        dmv = lax.dot_general(dy_ref[...].astype(BF16), w_ref[...], _NT, preferred_element_type=F32)
        ga = _sigmoid(gl_ref[:, :D] + bg_ref[:, :D])
        gb = _sigmoid(gl_ref[:, D:] + bg_ref[:, D:])
        dba_ref[...] = (dmv * ga).astype(BF16)
        dbb_ref[...] = (dmv * gb).astype(BF16)
        dla = dmv * ba_ref[...] * ga * (1.0 - ga)
        dlb = dmv * bb_ref[...] * gb * (1.0 - gb)
        dgl_ref[:, :D] = dla.astype(BF16)
        dgl_ref[:, D:] = dlb.astype(BF16)
        dbg_ref[:, :D] += jnp.sum(dla, axis=0, keepdims=True)
        dbg_ref[:, D:] += jnp.sum(dlb, axis=0, keepdims=True)

    row = lambda w: pl.BlockSpec((tm, w), lambda i: (i, 0))
    vec = pl.BlockSpec((1, D2), lambda i: (0, 0))
    return pl.pallas_call(
        body, name=name, grid=(T // tm,),
        in_specs=[row(D), pl.BlockSpec(wo.shape, lambda i: (0, 0)), row(D2), vec, row(D), row(D)],
        out_specs=[row(D), row(D), row(D2), vec],
        out_shape=[jax.ShapeDtypeStruct((T, D), BF16), jax.ShapeDtypeStruct((T, D), BF16),
                   jax.ShapeDtypeStruct((T, D2), BF16), jax.ShapeDtypeStruct((1, D2), F32)],
        compiler_params=_cparams(("arbitrary",)),
    )(dy, wo, gl, bg.reshape(1, D2), ba, bb)


def loss_head(x, gf, tgt, *, name):
    T, D = x.shape
    tm = _rows(T, D * 4)

    def body(x_ref, g_ref, t_ref, dx_ref, dg_ref, loss_ref):
        @pl.when(pl.program_id(0) == 0)
        def _():
            dg_ref[...] = jnp.zeros_like(dg_ref)
            loss_ref[...] = jnp.zeros_like(loss_ref)

        xv = x_ref[...]
        gv = g_ref[...]
        r = lax.rsqrt(jnp.mean(xv * xv, axis=-1, keepdims=True) + NORM_EPS)
        xn = xv * r
        err = xn * gv - t_ref[...]
        loss_ref[...] += 0.5 * jnp.sum(jnp.mean(err * err, axis=-1, keepdims=True))
        dy = err * (1.0 / D)
        dg_ref[...] += jnp.sum(dy * xn, axis=0, keepdims=True)
        dxn = dy * gv
        dx_ref[...] = r * (dxn - xn * jnp.mean(dxn * xn, axis=-1, keepdims=True))

    row = pl.BlockSpec((tm, D), lambda i: (i, 0))
    vec = pl.BlockSpec((1, D), lambda i: (0, 0))
    return pl.pallas_call(
        body, name=name, grid=(T // tm,), in_specs=[row, vec, row],
        out_specs=[row, vec, pl.BlockSpec((8, LANES), lambda i: (0, 0))],
        out_shape=[jax.ShapeDtypeStruct((T, D), F32), jax.ShapeDtypeStruct((1, D), F32),
                   jax.ShapeDtypeStruct((8, LANES), F32)],
        compiler_params=_cparams(("arbitrary",)),
    )(x, gf.reshape(1, D), tgt)


def _shift_rows(s, k, down):
    n = s.shape[0]
    t = lax.broadcasted_iota(jnp.int32, s.shape, 0)
    if down:
        return jnp.where(t >= k, pltpu.roll(s, k, 0), 0.0)
    return jnp.where(t < n - k, pltpu.roll(s, n - k, 0), 0.0)


def _window_sum(s, w, down):
    k = 1
    while k < w:
        s = s + _shift_rows(s, k, down)
        k *= 2
    return s


def _pool_count(shape, w):
    t = lax.broadcasted_iota(jnp.int32, shape, 0)
    return jnp.minimum(t + 1, w).astype(F32)


def pool_fwd(xp, maps, scale, B, *, name):
    T, P = xp.shape
    S = T // B
    G = POOL_GROUP_DIM

    def body(x_ref, m_ref, sc_ref, o_ref):
        for g, w in enumerate(POOL_WINDOWS):
            xg = x_ref[:, g * G:(g + 1) * G]
            pooled = _window_sum(xg, w, True) / _pool_count(xg.shape, w) - xg
            mixed = jnp.dot(pooled.astype(BF16), m_ref[g], preferred_element_type=F32)
            o_ref[:, g * G:(g + 1) * G] = (mixed * sc_ref[:, g * G:(g + 1) * G]).astype(BF16)

    return pl.pallas_call(
        body, name=name, grid=(B,),
        in_specs=[pl.BlockSpec((S, P), lambda b: (b, 0)), pl.BlockSpec((N_POOL_GROUPS, G, G), lambda b: (0, 0, 0)),
                  pl.BlockSpec((1, P), lambda b: (0, 0))],
        out_specs=pl.BlockSpec((S, P), lambda b: (b, 0)),
        out_shape=jax.ShapeDtypeStruct((T, P), BF16),
        compiler_params=_cparams(("parallel",)),
    )(xp, maps, scale.reshape(1, P))


def pool_bwd(xp, dmixed, maps, scale, B, *, name):
    T, P = xp.shape
    S = T // B
    G = POOL_GROUP_DIM

    def body(x_ref, dm_ref, m_ref, sc_ref, dx_ref, dmaps_ref, dsc_ref):
        @pl.when(pl.program_id(0) == 0)
        def _():
            dmaps_ref[...] = jnp.zeros_like(dmaps_ref)
            dsc_ref[...] = jnp.zeros_like(dsc_ref)

        for g, w in enumerate(POOL_WINDOWS):
            cols = slice(g * G, (g + 1) * G)
            xg = x_ref[:, cols]
            cnt = _pool_count(xg.shape, w)
            pooled = (_window_sum(xg, w, True) / cnt - xg).astype(BF16)
            mixed = jnp.dot(pooled, m_ref[g], preferred_element_type=F32)
            dmx = dm_ref[:, cols]
            dsc_ref[:, cols] += jnp.sum(dmx * mixed, axis=0, keepdims=True)
            dmp = (dmx * sc_ref[:, cols]).astype(BF16)
            dmaps_ref[g] += lax.dot_general(pooled, dmp, (((0,), (0,)), ((), ())), preferred_element_type=F32)
            dpooled = lax.dot_general(dmp, m_ref[g], (((1,), (1,)), ((), ())), preferred_element_type=F32)
            dx_ref[:, cols] = (_window_sum(dpooled / cnt, w, False) - dpooled).astype(BF16)

    blk = pl.BlockSpec((S, P), lambda b: (b, 0))
    mp = pl.BlockSpec((N_POOL_GROUPS, G, G), lambda b: (0, 0, 0))
    vec = pl.BlockSpec((1, P), lambda b: (0, 0))
    return pl.pallas_call(
        body, name=name, grid=(B,), in_specs=[blk, blk, mp, vec], out_specs=[blk, mp, vec],
        out_shape=[jax.ShapeDtypeStruct((T, P), BF16), jax.ShapeDtypeStruct((N_POOL_GROUPS, G, G), F32),
                   jax.ShapeDtypeStruct((1, P), F32)],
        compiler_params=_cparams(("arbitrary",)),
    )(xp, dmixed, maps, scale.reshape(1, P))


def _keys(kv_ref, kr_ref, rows):
    return jnp.concatenate([kv_ref[rows, :QK_NOPE_DIM], kr_ref[rows, :]], axis=1)


def _causal(s):
    row = lax.broadcasted_iota(jnp.int32, s.shape, 0)
    col = lax.broadcasted_iota(jnp.int32, s.shape, 1)
    return jnp.where(row >= col, s, NEG_INF)


def attn_fwd(q, kv, kr, cs, sn, B, *, name):
    T = q.shape[0]
    S = T // B
    blk = min(ATT_BLOCK, S)
    nb = S // blk
    H = N_HEADS
    scale = QK_DIM ** -0.5

    def body(q_ref, kv_ref, kr_ref, c_ref, s_ref, o_ref, lse_ref, qs_ref):
        qs_ref[:, :QK_NOPE_DIM] = (q_ref[:, :QK_NOPE_DIM] * scale).astype(BF16)
        qs_ref[:, QK_NOPE_DIM:] = _rope(q_ref[:, QK_NOPE_DIM:], c_ref[...] * scale, s_ref[...] * scale).astype(BF16)
        for qi in range(nb):
            rows = slice(qi * blk, (qi + 1) * blk)
            qb = qs_ref[rows, :]
            sd = _causal(lax.dot_general(qb, _keys(kv_ref, kr_ref, rows), _NT, preferred_element_type=F32))
            m = jnp.max(sd, axis=-1, keepdims=True)
            if qi > 0:
                prev = slice(0, qi * blk)
                sp = lax.dot_general(qb, _keys(kv_ref, kr_ref, prev), _NT, preferred_element_type=F32)
                m = jnp.maximum(m, jnp.max(sp, axis=-1, keepdims=True))
            pd = jnp.exp(sd - m)
            l = jnp.sum(pd, axis=-1, keepdims=True)
            acc = jnp.dot(pd.astype(BF16), kv_ref[rows, QK_NOPE_DIM:], preferred_element_type=F32)
            if qi > 0:
                pp = jnp.exp(sp - m)
                l = l + jnp.sum(pp, axis=-1, keepdims=True)
                acc = acc + jnp.dot(pp.astype(BF16), kv_ref[prev, QK_NOPE_DIM:], preferred_element_type=F32)
            o_ref[rows, :] = (acc / l).astype(BF16)
            lse_ref[0, rows, :] = m + jnp.log(l)

    head = pl.BlockSpec((S, HEAD_PAD), lambda b, h: (b, h))
    shared = pl.BlockSpec((S, LANES), lambda b, h: (b, 0))
    return pl.pallas_call(
        body, name=name, grid=(B, H),
        in_specs=[head, head, shared, shared, shared],
        out_specs=[pl.BlockSpec((S, V_HEAD_DIM), lambda b, h: (b, h)), pl.BlockSpec((1, S, 1), lambda b, h: (h, b, 0)),
                   head],
        out_shape=[jax.ShapeDtypeStruct((T, H * V_HEAD_DIM), BF16), jax.ShapeDtypeStruct((H, T, 1), F32),
                   jax.ShapeDtypeStruct((T, H * HEAD_PAD), BF16)],
        compiler_params=_cparams(("parallel", "parallel")),
    )(q, kv, kr, cs, sn)


def attn_bwd(q, kv, kr, o, do, lse, cs, sn, B, *, name):
    T = q.shape[0]
    S = T // B
    blk = min(ATT_BLOCK, S)
    nb = S // blk
    H = N_HEADS
    scale = QK_DIM ** -0.5

    def body(q_ref, kv_ref, kr_ref, o_ref, do_ref, lse_ref, c_ref, s_ref, dq_ref, dkv_ref, dkr_ref, dk_s, dv_s):
        dk_s[...] = jnp.zeros_like(dk_s)
        dv_s[...] = jnp.zeros_like(dv_s)

        @pl.when(pl.program_id(1) == 0)
        def _():
            dkr_ref[...] = jnp.zeros_like(dkr_ref)

        for qi in range(nb):
            rows = slice(qi * blk, (qi + 1) * blk)
            qb = q_ref[rows, :]
            dob = do_ref[rows, :]
            delta = jnp.sum(dob.astype(F32) * o_ref[rows, :].astype(F32), axis=-1, keepdims=True)
            lse_b = lse_ref[0, rows, :]

            def part(ks, diagonal):
                k = _keys(kv_ref, kr_ref, ks)
                s = lax.dot_general(qb, k, _NT, preferred_element_type=F32)
                if diagonal:
                    s = _causal(s)
                p = jnp.exp(s - lse_b)
                dp = lax.dot_general(dob, kv_ref[ks, QK_NOPE_DIM:], _NT, preferred_element_type=F32)
                ds = (p * (dp - delta)).astype(BF16)
                dv_s[ks, :] += lax.dot_general(p.astype(BF16), dob, _TN, preferred_element_type=F32)
                dk_s[ks, :] += lax.dot_general(ds, qb, _TN, preferred_element_type=F32)
                return jnp.dot(ds, k, preferred_element_type=F32)

            dq = part(rows, True)
            if qi > 0:
                dq = dq + part(slice(0, qi * blk), False)
            dq_ref[rows, :QK_NOPE_DIM] = (dq[:, :QK_NOPE_DIM] * scale).astype(BF16)
            dq_ref[rows, QK_NOPE_DIM:] = _rope_t(dq[:, QK_NOPE_DIM:], c_ref[rows, :] * scale,
                                                 s_ref[rows, :] * scale).astype(BF16)

        dkv_ref[:, :QK_NOPE_DIM] = dk_s[:, :QK_NOPE_DIM].astype(BF16)
        dkv_ref[:, QK_NOPE_DIM:] = dv_s[...].astype(BF16)
        dkr_ref[...] += dk_s[:, QK_NOPE_DIM:]

    head = lambda w: pl.BlockSpec((S, w), lambda b, h: (b, h))
    shared = pl.BlockSpec((S, LANES), lambda b, h: (b, 0))
    return pl.pallas_call(
        body, name=name, grid=(B, H),
        in_specs=[head(HEAD_PAD), head(HEAD_PAD), shared, head(V_HEAD_DIM), head(V_HEAD_DIM),
                  pl.BlockSpec((1, S, 1), lambda b, h: (h, b, 0)), shared, shared],
        out_specs=[head(HEAD_PAD), head(HEAD_PAD), shared],
        out_shape=[jax.ShapeDtypeStruct((T, H * HEAD_PAD), BF16), jax.ShapeDtypeStruct((T, H * HEAD_PAD), BF16),
                   jax.ShapeDtypeStruct((T, LANES), F32)],
        scratch_shapes=[pltpu.VMEM((S, HEAD_PAD), F32), pltpu.VMEM((S, V_HEAD_DIM), F32)],
        compiler_params=_cparams(("parallel", "arbitrary")),
    )(q, kv, kr, o, do, lse, cs, sn)


def adamw(w, g, m, v, *, name, dep=None):
    R, C = w.shape
    cap = max(8, (1024 * 1024) // (C * 4))
    tr = _pick(R, tuple(c for c in (1024, 512, 256, 128, 64, 32, 16, 8) if c <= cap))
    c1 = 1.0 - ADAM_B1 ** ADAM_STEP
    c2 = 1.0 - ADAM_B2 ** ADAM_STEP

    def body(w_ref, g_ref, m_ref, v_ref, *rest):
        d_ref, nm_ref, nv_ref = rest[-3:]
        gv = g_ref[...]
        mv = ADAM_B1 * m_ref[...] + (1.0 - ADAM_B1) * gv
        vv = ADAM_B2 * v_ref[...] + (1.0 - ADAM_B2) * (gv * gv)
        nm_ref[...] = mv
        nv_ref[...] = vv
        d_ref[...] = -ADAM_LR * ((mv / c1) / (jnp.sqrt(vv / c2) + ADAM_EPS) + ADAM_WD * w_ref[...])

    blk = pl.BlockSpec((tr, C), lambda i: (i, 0))
    sh = jax.ShapeDtypeStruct((R, C), F32)
    extra = [] if dep is None else [dep]
    return pl.pallas_call(
        body, name=name, grid=(R // tr,), in_specs=[blk] * 4 + [ANY] * len(extra), out_specs=[blk] * 3,
        out_shape=[sh] * 3, compiler_params=_cparams(("parallel",)),
    )(w, g, m, v, *extra)


ANY = pl.BlockSpec(memory_space=pl.ANY)


def _place():
    x, y, c = lax.axis_index("x"), lax.axis_index("y"), lax.axis_index("c")
    others = [(1 - x, y), (x, 1 - y), (1 - x, 1 - y)]
    return x, y, c, others


def _remote(src, dst, ssem, rsem, dev):
    return pltpu.make_async_remote_copy(src_ref=src, dst_ref=dst, send_sem=ssem, recv_sem=rsem,
                                        device_id=dev, device_id_type=MESH)


def _half(ref_rows, c):
    hr = ref_rows // 2
    return pl.ds(pl.multiple_of(c * hr, 16), hr)


HBM = pl.BlockSpec(memory_space=pltpu.HBM)
SEMS = pl.BlockSpec(memory_space=pltpu.SEMAPHORE)
EFFECT = pltpu.SideEffectType.DATAFLOW_SIDE_EFFECTING


def exchange_begin(name, srcs, land_shapes, plan, ncopies, after=None):
    ns, nl = len(srcs), len(land_shapes)
    nin = ns + nl + (0 if after is None else 1)

    def body(*refs):
        ssem, rsem = refs[nin], refs[nin + 1]
        for k, (s, d, dev) in enumerate(plan(refs[:ns], refs[ns:ns + nl])):
            _remote(s, d, ssem.at[k], rsem.at[k], dev).start()
        refs[-1][...] = jnp.zeros_like(refs[-1])

    bufs = [pltpu.HBM(s.shape, s.dtype) for s in srcs] + [pltpu.HBM(s.shape, s.dtype) for s in land_shapes]
    args = [pltpu.with_memory_space_constraint(s, pltpu.HBM) for s in srcs]
    args += [pltpu.with_memory_space_constraint(lax.empty(s.shape, s.dtype), pltpu.HBM) for s in land_shapes]
    if after is not None:
        args.append(after)
    out = pl.pallas_call(
        body, name=name,
        out_shape=(pltpu.SemaphoreType.DMA((ncopies,)), pltpu.SemaphoreType.DMA((ncopies,)), *bufs,
                   jax.ShapeDtypeStruct((8, LANES), F32)),
        in_specs=[HBM] * (ns + nl) + ([] if after is None else [ANY]),
        out_specs=(SEMS, SEMS, *([HBM] * (ns + nl)), pl.BlockSpec(memory_space=pltpu.VMEM)),
        input_output_aliases={i: 2 + i for i in range(ns + nl)},
        compiler_params=pltpu.CompilerParams(has_side_effects=EFFECT),
    )(*args)
    return (out[0], out[1], out[2:2 + ns], out[2 + ns:2 + ns + nl]), out[-1]


def exchange_end(name, handle, plan, after):
    ssem, rsem, srcs, lands = handle
    ns, nl = len(srcs), len(lands)

    def body(*refs):
        ssem_ref, rsem_ref = refs[ns + nl], refs[ns + nl + 1]
        for k, (s, d, dev) in enumerate(plan(refs[:ns], refs[ns:ns + nl])):
            cp = _remote(s, d, ssem_ref.at[k], rsem_ref.at[k], dev)
            cp.wait_send()
            cp.wait_recv()

    out = pl.pallas_call(
        body, name=name,
        out_shape=tuple(pltpu.HBM(s.shape, s.dtype) for s in (*srcs, *lands)),
        in_specs=[HBM] * (ns + nl) + [SEMS, SEMS, ANY], out_specs=tuple([HBM] * (ns + nl)),
        input_output_aliases={i: i for i in range(ns + nl)},
        compiler_params=pltpu.CompilerParams(has_side_effects=EFFECT),
    )(*srcs, *lands, ssem, rsem, after)
    return list(out[:ns]), list(out[ns:])


def ag_plan(src_refs, land_refs):
    x, y, c, others = _place()
    plan = []
    for s, d in zip(src_refs, land_refs):
        mine = _half(s.shape[0], c)
        for ox, oy in others:
            plan.append((s.at[mine, :], d.at[2 * x + y, mine, :], (ox, oy, c)))
    return plan


def ag_forward(lands, *, name):
    n = len(lands)

    def body(*refs):
        ins, outs = refs[:n], refs[n:2 * n]
        ssem, rsem = refs[2 * n:]
        x, y, c, others = _place()
        sent = []
        for i in range(n):
            mine = _half(ins[i].shape[1], c)
            for j, (ox, oy) in enumerate(others):
                cp = _remote(ins[i].at[2 * ox + oy, mine, :], outs[i].at[2 * ox + oy, mine, :], ssem.at[3 * i + j],
                             rsem.at[3 * i + j], (x, y, 1 - c))
                cp.start()
                sent.append(cp)
        for cp in sent:
            cp.wait()

    return pl.pallas_call(
        body, name=name, in_specs=[ANY] * n, out_specs=[ANY] * n,
        out_shape=[jax.ShapeDtypeStruct(a.shape, a.dtype) for a in lands],
        input_output_aliases={i: i for i in range(n)},
        scratch_shapes=[pltpu.SemaphoreType.DMA((3 * n,)), pltpu.SemaphoreType.DMA((3 * n,))],
        compiler_params=pltpu.CompilerParams(has_side_effects=True),
    )(*lands)


def place_own(lands, own, chip, *, name):
    n = len(lands)
    steps = 4

    def body(chip_ref, *refs):
        for i in range(n):
            refs[2 * n + i][0] = refs[i][...]

    in_specs = [pl.BlockSpec((o.shape[0] // steps, o.shape[1]), lambda t, q: (t, 0)) for o in own] + [ANY] * n
    out_specs = [pl.BlockSpec((1, o.shape[0] // steps, o.shape[1]), lambda t, q: (q[0], t, 0)) for o in own]
    return pl.pallas_call(
        body, name=name,
        grid_spec=pltpu.PrefetchScalarGridSpec(num_scalar_prefetch=1, grid=(steps,), in_specs=in_specs,
                                               out_specs=out_specs),
        out_shape=[jax.ShapeDtypeStruct(a.shape, a.dtype) for a in lands],
        input_output_aliases={1 + n + i: i for i in range(n)},
        compiler_params=_cparams(("parallel",)),
    )(chip, *own, *lands)


def rs_swap_halves(grads, *, name):
    n = len(grads)

    def body(*refs):
        ins, outs = refs[:n], refs[n:2 * n]
        ssem, rsem = refs[2 * n:]
        x, y, c, _ = _place()
        cps = []
        for i in range(n):
            theirs = _half(ins[i].shape[1], 1 - c)
            cp = _remote(ins[i].at[:, theirs, :], outs[i], ssem.at[i], rsem.at[i], (x, y, 1 - c))
            cp.start()
            cps.append(cp)
        for cp in cps:
            cp.wait()

    return pl.pallas_call(
        body, name=name, in_specs=[ANY] * n, out_specs=[ANY] * n,
        out_shape=[jax.ShapeDtypeStruct((4, g.shape[1] // 2, g.shape[2]), g.dtype) for g in grads],
        scratch_shapes=[pltpu.SemaphoreType.DMA((n,)), pltpu.SemaphoreType.DMA((n,))],
        compiler_params=pltpu.CompilerParams(has_side_effects=True),
    )(*grads)


def rs_chip_sum(g, r1, core, *, name):
    _, r, cdim = g.shape
    hr = r // 2

    def body(c_ref, g_ref, r1_ref, o_ref):
        o_ref[...] = (g_ref[...].astype(F32) + r1_ref[...].astype(F32)).astype(BF16)

    return pl.pallas_call(
        body, name=name,
        grid_spec=pltpu.PrefetchScalarGridSpec(
            num_scalar_prefetch=1, grid=(4,),
            in_specs=[pl.BlockSpec((1, hr, cdim), lambda qq, c_ref: (qq, c_ref[0], 0)),
                      pl.BlockSpec((1, hr, cdim), lambda qq, c_ref: (qq, 0, 0))],
            out_specs=pl.BlockSpec((1, hr, cdim), lambda qq, c_ref: (qq, 0, 0))),
        out_shape=jax.ShapeDtypeStruct((4, hr, cdim), BF16),
        compiler_params=_cparams(("parallel",)),
    )(core, g, r1)


def rs_plan(src_refs, land_refs):
    x, y, c, others = _place()
    plan = []
    for s, d in zip(src_refs, land_refs):
        for j, (ox, oy) in enumerate(others):
            plan.append((s.at[2 * ox + oy], d.at[j], (ox, oy, c)))
    return plan


def rs_final_sum(g, r1, r2, place, acc, l, *, name):
    _, r, cdim = g.shape
    hr = r // 2
    ch = hr // 2

    def body(p_ref, g_ref, r1_ref, a_ref, b_ref, d_ref, acc_in, o_ref):
        s = g_ref[...].astype(F32) + r1_ref[...].astype(F32)
        s = s + a_ref[...].astype(F32)
        s = s + b_ref[...].astype(F32)
        o_ref[...] = s + d_ref[...].astype(F32)

    other = lambda j: pl.BlockSpec((1, ch, cdim), lambda t, p_ref: (j, t, 0))
    return pl.pallas_call(
        body, name=name,
        grid_spec=pltpu.PrefetchScalarGridSpec(
            num_scalar_prefetch=1, grid=(2,),
            in_specs=[pl.BlockSpec((1, ch, cdim), lambda t, p_ref: (p_ref[0], 2 * p_ref[1] + t, 0)),
                      pl.BlockSpec((1, ch, cdim), lambda t, p_ref: (p_ref[0], t, 0)),
                      other(0), other(1), other(2), ANY],
            out_specs=pl.BlockSpec((1, ch, cdim), lambda t, p_ref: (l, 2 * p_ref[1] + t, 0))),
        out_shape=jax.ShapeDtypeStruct(acc.shape, F32),
        input_output_aliases={6: 0},
        compiler_params=_cparams(("parallel",)),
    )(place, g, r1, r2, r2, r2, acc)


def rs_join_halves(grads, *, name):
    n = len(grads)

    def body(*refs):
        ins, outs = refs[:n], refs[n:2 * n]
        ssem, rsem = refs[2 * n:]
        x, y, c, _ = _place()
        cps = []
        for i in range(n):
            mine = _half(ins[i].shape[1], c)
            cp = _remote(ins[i].at[:, mine, :], outs[i].at[:, mine, :], ssem.at[i], rsem.at[i], (x, y, 1 - c))
            cp.start()
            cps.append(cp)
        for cp in cps:
            cp.wait()

    return pl.pallas_call(
        body, name=name, in_specs=[ANY] * n, out_specs=[ANY] * n,
        out_shape=[jax.ShapeDtypeStruct(g.shape, g.dtype) for g in grads],
        input_output_aliases={i: i for i in range(n)},
        scratch_shapes=[pltpu.SemaphoreType.DMA((n,)), pltpu.SemaphoreType.DMA((n,))],
        compiler_params=pltpu.CompilerParams(has_side_effects=True),
    )(*grads)


def all_reduce_small(v):
    R = v.shape[0]

    def body(v_ref, o_ref, buf, ssem, rsem):
        x, y, c, _ = _place()
        me = 4 * x + 2 * y + c
        buf[me] = v_ref[...]
        cps = []
        for k in range(1, 8):
            fx, fy, fc = (k >> 2) & 1, (k >> 1) & 1, k & 1
            px = jnp.where(fx == 1, 1 - x, x)
            py = jnp.where(fy == 1, 1 - y, y)
            pc = jnp.where(fc == 1, 1 - c, c)
            cp = _remote(v_ref, buf.at[me], ssem.at[k - 1], rsem.at[k - 1], (px, py, pc))
            cp.start()
            cps.append(cp)
        for cp in cps:
            cp.wait()
        acc = buf[0]
        for d in range(1, 8):
            acc = acc + buf[d]
        o_ref[...] = acc

    vm = pl.BlockSpec(memory_space=pltpu.VMEM)
    return pl.pallas_call(
        body, name="all_reduce_small", in_specs=[vm], out_specs=vm,
        out_shape=jax.ShapeDtypeStruct((R, LANES), F32),
        scratch_shapes=[pltpu.VMEM((8, R, LANES), F32), pltpu.SemaphoreType.DMA((7,)), pltpu.SemaphoreType.DMA((7,))],
        compiler_params=pltpu.CompilerParams(vmem_limit_bytes=VMEM_LIMIT, has_side_effects=True),
    )(v)


def _to_stacked(name, full):
    R, C = full.shape
    if name in ROW_SHARDED:
        return full.reshape(4, R // 4, C)
    return jnp.transpose(full.reshape(R, 4, C // 4), (1, 0, 2))


def _from_stacked(name, st):
    _, r, c = st.shape
    if name in ROW_SHARDED:
        return st.reshape(4 * r, c)
    return jnp.transpose(st, (1, 0, 2)).reshape(r, 4 * c)


UP_PIECES = ("ffn1_up", "ffn2_up")


def _layer_weights(lands):
    w = {n: lands[n] if n in UP_PIECES else _from_stacked(n, lands[n]) for n in lands}
    if "w_in" not in w:
        return w
    win = w.pop("w_in")
    D = win.shape[0]
    p0, p1, p2, p3 = POOL_DIM, POOL_DIM + Q_LORA_RANK, POOL_DIM + Q_LORA_RANK + KV_LORA_RANK, \
        POOL_DIM + Q_LORA_RANK + KV_LORA_RANK + QK_ROPE_DIM
    w["w_pool"] = win[:, :p0]
    w["w_lat"] = jnp.concatenate([win[:, p0:p3], jnp.zeros((D, LAT_DIM - (p3 - p0)), win.dtype)], axis=1)
    w["w_gate"] = win[:, p3:]
    uq = w["w_uq"].reshape(Q_LORA_RANK, N_HEADS, QK_DIM)
    w["w_uq"] = jnp.concatenate([uq, jnp.zeros((Q_LORA_RANK, N_HEADS, HEAD_PAD - QK_DIM), uq.dtype)],
                                axis=2).reshape(Q_LORA_RANK, N_HEADS * HEAD_PAD)
    return w


def _layer_grads_stacked(dw):
    dw = dict(dw)
    if "w_lat" in dw:
        lat = dw.pop("w_lat")
        dw["w_in"] = jnp.concatenate([dw.pop("w_pool"), lat[:, :Q_LORA_RANK + KV_LORA_RANK + QK_ROPE_DIM],
                                      dw.pop("w_gate")], axis=1)
        dw["w_uq"] = dw["w_uq"].reshape(Q_LORA_RANK, N_HEADS, HEAD_PAD)[:, :, :QK_DIM].reshape(Q_LORA_RANK,
                                                                                                 N_HEADS * QK_DIM)
    return {n: dw[n] if n in UP_PIECES else _to_stacked(n, dw[n]) for n in BIG if n in dw}


def _rope_tables(positions):
    inv_freq = ROPE_THETA ** (-jnp.arange(0, QK_ROPE_DIM, 2, dtype=F32) / QK_ROPE_DIM)
    ang = positions.astype(F32).reshape(-1)[:, None] * inv_freq
    cos, sin = jnp.cos(ang), jnp.sin(ang)
    z = jnp.zeros((ang.shape[0], LANES - QK_ROPE_DIM), F32)
    return jnp.concatenate([cos, cos, z], axis=1), jnp.concatenate([-sin, sin, z], axis=1)


def _pack_small(vals):
    parts = []
    for n in SMALL:
        f = vals[n].reshape(-1).astype(F32)
        pad = (-f.shape[0]) % (8 * LANES)
        parts.append(jnp.pad(f, (0, pad)))
    return jnp.concatenate(parts).reshape(-1, LANES)


def _unpack_small(packed, like):
    flat = packed.reshape(-1)
    out, off = {}, 0
    for n in SMALL:
        size = like[n].size
        out[n] = flat[off:off + size].reshape(like[n].shape)
        off += size + ((-size) % (8 * LANES))
    return out


def _ffn_fwd(x, g, wu4, wd, tag, dep=None):
    h = rms_fwd(x, g, dep=dep, name=f"{tag}_norm")
    gate, up, a = ffn_up_act(h, wu4, name=f"{tag}_up_act")
    y = mm(a, wd, res=x, alpha=0.5, name=f"{tag}_down")
    return y, (x, h, gate, up, a)


def _ffn_bwd(dy, saved, g, wu4, wd, tag, dep=None):
    x, h, gate, up, a = saved
    dgate, dup = ffn_down_dx_act(dy, wd, gate, up, dep=dep, name=f"{tag}_down_dx_act")
    dwd = mm(a, dy, ta=True, alpha=0.5, out_dtype=BF16, name=f"{tag}_down_dw")
    dwu4 = ffn_up_dw(h, dgate, dup, name=f"{tag}_up_dw")
    dh = ffn_up_dx(dgate, dup, wu4, name=f"{tag}_up_dx")
    dx, dg = rms_bwd(x, g, dh, dy, name=f"{tag}_norm_bwd")
    return dx, dg, dwu4, dwd


def _mix_fwd(x, p, w, cs, sn, B, dep=None):
    h = rms_fwd(x, p["norm_mix"], dep=dep, name="mix_norm")
    lat = mm(h, w["w_lat"], name="mix_lat")
    xp = mm(h, w["w_pool"], name="mix_pool_in")
    gl = mm(h, w["w_gate"], name="mix_gate_in")
    mixed = pool_fwd(xp, p["pool_maps"].astype(BF16), p["pool_scale"], B, name="pool_fwd")
    ba = mm(mixed, w["w_pool_proj"], name="mix_pool_proj")
    qn, kvn, kr = lat_fwd(lat, p["q_latent_norm"], p["kv_latent_norm"], cs, sn, name="lat_fwd")
    kv = mm(kvn, w["w_ukv"], out_dtype=BF16, name="mix_ukv")
    o, lse, q = attn_fwd(mm(qn, w["w_uq"], name="mix_uq"), kv, kr, cs, sn, B, name="attn_fwd")
    bb, merged = attn_proj_gate(o, w["w_attn_proj"], gl, p["b_gate"], ba, name="mix_attn_proj_gate")
    y = mm(merged, w["w_out"], res=x, name="mix_out")
    return y, (x, h, lat, xp, gl, mixed, ba, qn, kvn, kr, q, kv, o, lse, bb, merged)


def _mix_bwd(dy, saved, p, w, cs, sn, B):
    x, h, lat, xp, gl, mixed, ba, qn, kvn, kr, q, kv, o, lse, bb, merged = saved
    dw, ds = {}, {}
    dw["w_out"] = mm(merged, dy, ta=True, out_dtype=BF16, name="mix_out_dw")
    dba, dbb, dgl, ds["b_gate"] = out_dx_gate(dy, w["w_out"], gl, p["b_gate"], ba, bb, name="mix_out_dx_gate")
    dw["w_attn_proj"] = mm(o, dbb, ta=True, out_dtype=BF16, name="mix_attn_proj_dw")
    do = mm(dbb, w["w_attn_proj"], tb=True, out_dtype=BF16, name="mix_attn_proj_dx")
    dw["w_pool_proj"] = mm(mixed, dba, ta=True, out_dtype=BF16, name="mix_pool_proj_dw")
    dmixed = mm(dba, w["w_pool_proj"], tb=True, name="mix_pool_proj_dx")
    dxp, ds["pool_maps"], ds["pool_scale"] = pool_bwd(xp, dmixed, p["pool_maps"].astype(BF16), p["pool_scale"], B,
                                                      name="pool_bwd")
    dqb, dkv, dkr = attn_bwd(q, kv, kr, o, do, lse, cs, sn, B, name="attn_bwd")
    dw["w_ukv"] = mm(kvn, dkv, ta=True, out_dtype=BF16, name="mix_ukv_dw")
    dkvn = mm(dkv, w["w_ukv"], tb=True, name="mix_ukv_dx")
    dw["w_uq"] = mm(qn, dqb, ta=True, out_dtype=BF16, name="mix_uq_dw")
    dqn = mm(dqb, w["w_uq"], tb=True, name="mix_uq_dx")
    dlat, ds["q_latent_norm"], ds["kv_latent_norm"] = lat_bwd(lat, p["q_latent_norm"], p["kv_latent_norm"], dqn, dkvn,
                                                               dkr, cs, sn, name="lat_bwd")
    dw["w_lat"] = mm(h, dlat, ta=True, out_dtype=BF16, name="mix_lat_dw")
    dw["w_pool"] = mm(h, dxp, ta=True, out_dtype=BF16, name="mix_pool_in_dw")
    dw["w_gate"] = mm(h, dgl, ta=True, out_dtype=BF16, name="mix_gate_in_dw")
    dh = mm(dlat, w["w_lat"], tb=True, name="mix_lat_dx")
    dh = mm(dxp, w["w_pool"], tb=True, res=dh, name="mix_pool_in_dx")
    dh = mm(dgl, w["w_gate"], tb=True, res=dh, name="mix_gate_in_dx")
    dx, ds["norm_mix"] = rms_bwd(x, p["norm_mix"], dh, dy, name="mix_norm_bwd")
    return dx, dw, ds


def kernel(x, positions, norm_ffn1, ffn1_up, ffn1_down, norm_mix, w_in, b_gate, pool_maps, pool_scale, w_pool_proj, q_latent_norm, w_uq, kv_latent_norm, w_ukv, w_attn_proj, w_out, norm_ffn2, ffn2_up, ffn2_down, final_norm, loss_target, m_norm_ffn1, m_ffn1_up, m_ffn1_down, m_norm_mix, m_w_in, m_b_gate, m_pool_maps, m_pool_scale, m_w_pool_proj, m_q_latent_norm, m_w_uq, m_kv_latent_norm, m_w_ukv, m_w_attn_proj, m_w_out, m_norm_ffn2, m_ffn2_up, m_ffn2_down, m_final_norm, v_norm_ffn1, v_ffn1_up, v_ffn1_down, v_norm_mix, v_w_in, v_b_gate, v_pool_maps, v_pool_scale, v_w_pool_proj, v_q_latent_norm, v_w_uq, v_kv_latent_norm, v_w_ukv, v_w_attn_proj, v_w_out, v_norm_ffn2, v_ffn2_up, v_ffn2_down, v_final_norm):
    given = dict(locals())
    B, S, D = x.shape
    T = B * S
    L = norm_ffn1.shape[0]
    W = {n: given[n] for n in WEIGHTS}
    Mo = {n: given["m_" + n] for n in WEIGHTS}
    Vo = {n: given["v_" + n] for n in WEIGHTS}
    core = lax.axis_index("c").astype(jnp.int32)
    chip = (2 * lax.axis_index("x") + lax.axis_index("y")).astype(jnp.int32)

    core_arr = core.reshape(1)
    chip_arr = chip.reshape(1)
    place = jnp.stack([chip, core])
    first = ("ffn1_up", "ffn1_down")
    rest = tuple(n for n in BIG if n not in first)

    own = [{n: W[n][l].astype(BF16) for n in BIG} for l in range(L)]

    def ag_begin(l, names, tag, after=None):
        lands = [jax.ShapeDtypeStruct((4,) + own[l][n].shape, BF16) for n in names]
        return exchange_begin(f"ag_start_{tag}", [own[l][n] for n in names], lands, ag_plan, 3 * len(names), after)

    def ag_finish(handle, names, tag, after):
        mine, lands = exchange_end(f"ag_wait_{tag}", handle, ag_plan, after)
        lands = ag_forward(lands, name=f"ag_forward_{tag}")
        return _layer_weights(dict(zip(names, place_own(lands, mine, chip_arr, name=f"place_own_{tag}"))))

    h_first, t1 = ag_begin(0, first, "0a")
    cs, sn = _rope_tables(positions)
    xs = x.reshape(T, D) + t1[0, 0]
    saved, handle = [], None
    for l in range(L):
        p = {n: W[n][l] for n in SMALL if n != "final_norm"}
        dep = None
        if l == 0:
            w = ag_finish(h_first, first, "0a", xs)
            h_rest, dep = ag_begin(0, rest, "0b", after=w["ffn1_down"])
        else:
            w = ag_finish(handle, BIG, str(l), xs)
            if l + 1 < L:
                handle, dep = ag_begin(l + 1, BIG, str(l + 1), after=xs)
        xs, s1 = _ffn_fwd(xs, p["norm_ffn1"], w["ffn1_up"], w["ffn1_down"], "ffn1", dep=dep)
        dep = None
        if l == 0:
            w.update(ag_finish(h_rest, rest, "0b", xs))
            if L > 1:
                handle, dep = ag_begin(1, BIG, "1", after=xs)
        xs, s2 = _mix_fwd(xs, p, w, cs, sn, B, dep=dep)
        xs, s3 = _ffn_fwd(xs, p["norm_ffn2"], w["ffn2_up"], w["ffn2_down"], "ffn2")
        saved.append((w, p, s1, s2, s3))

    dx, dfinal, loss_tile = loss_head(xs, final_norm, loss_target.reshape(T, D), name="loss_head")
    loss = lax.psum(loss_tile[0, 0], ("x", "y", "c"))

    def rs_begin(dw, tag):
        stacked = _layer_grads_stacked(dw)
        names = tuple(stacked)
        parts = [stacked[n] for n in names]
        r1 = rs_swap_halves(parts, name=f"rs_swap_{tag}")
        sums = [rs_chip_sum(g, a, core_arr, name=f"rs_chip_sum_{n}") for n, g, a in zip(names, parts, r1)]
        lands = [jax.ShapeDtypeStruct((3,) + s.shape[1:], BF16) for s in sums]
        handle, token = exchange_begin(f"rs_start_{tag}", sums, lands, rs_plan, 3 * len(names))
        return (names, parts, r1, handle), token

    acc = {n: lax.empty(W[n].shape, F32) for n in BIG}

    def rs_finish(l, pending, tag, after):
        names, parts, r1, handle = pending
        _, r2 = exchange_end(f"rs_wait_{tag}", handle, rs_plan, after)
        for n, g, a, b in zip(names, parts, r1, r2):
            acc[n] = rs_final_sum(g, a, b, place, acc[n], l, name=f"rs_final_sum_{n}")

    small_layers, pending, dep = [], [], None
    for l in reversed(range(L)):
        w, p, s1, s2, s3 = saved[l]
        dx, dg2, dwu2, dwd2 = _ffn_bwd(dx, s3, p["norm_ffn2"], w["ffn2_up"], w["ffn2_down"], "ffn2", dep=dep)
        dx, dw, ds = _mix_bwd(dx, s2, p, w, cs, sn, B)
        dw.update(ffn2_up=dwu2, ffn2_down=dwd2)
        dep = None
        if l == 0:
            early, dep = rs_begin(dw, "0b")
            dw = {}
        dx, dg1, dwu1, dwd1 = _ffn_bwd(dx, s1, p["norm_ffn1"], w["ffn1_up"], w["ffn1_down"], "ffn1", dep=dep)
        dw.update(ffn1_up=dwu1, ffn1_down=dwd1)
        ds.update(norm_ffn1=dg1, norm_ffn2=dg2)
        small_layers.append(ds)
        last, dep = rs_begin(dw, "0a" if l == 0 else str(l))
        if l > 0:
            pending.append((l, last))
    small_layers.reverse()

    small = {n: jnp.stack([small_layers[l][n].reshape(W[n].shape[1:]) for l in range(L)]) for n in SMALL
             if n != "final_norm"}
    small["final_norm"] = dfinal.reshape(final_norm.shape)
    grads = _unpack_small(all_reduce_small(_pack_small(small) + dep[0, 0]), small)
    delta, new_m, new_v = {}, {}, {}
    d, nm, nv = adamw(_pack_small(W), _pack_small(grads), _pack_small(Mo), _pack_small(Vo), name="adamw_small")
    delta.update(_unpack_small(d, W))
    new_m.update(_unpack_small(nm, W))
    new_v.update(_unpack_small(nv, W))

    def update(names, tag, d):
        joined = rs_join_halves([acc[n] for n in names], name=f"rs_join_{tag}")
        for n, g in zip(names, joined):
            sh = W[n].shape
            two = lambda a: a.reshape(sh[0] * sh[1], sh[2])
            d, nm, nv = adamw(two(W[n]), two(g), two(Mo[n]), two(Vo[n]), dep=d, name=f"adamw_{n}")
            grads[n], delta[n], new_m[n], new_v[n] = g, d.reshape(sh), nm.reshape(sh), nv.reshape(sh)
        return d

    for l, item in pending:
        rs_finish(l, item, str(l), d)
    rs_finish(0, early, "0b", d)
    d = update(rest, "rest", d)
    rs_finish(0, last, "0a", d)
    update(first, "first", d)

    return (loss, dx.reshape(B, S, D), *[grads[n] for n in WEIGHTS], *[delta[n] for n in WEIGHTS],
            *[new_m[n] for n in WEIGHTS], *[new_v[n] for n in WEIGHTS])
```

```python
import functools

import jax
import jax.numpy as jnp
from jax import lax
from jax.experimental import pallas as pl
from jax.experimental.pallas import tpu as pltpu

F32 = jnp.float32
BF16 = jnp.bfloat16

N_HEADS = 8
QK_NOPE_DIM = 128
QK_ROPE_DIM = 64
QK_DIM = QK_NOPE_DIM + QK_ROPE_DIM
V_HEAD_DIM = 128
HEAD_PAD = 256
Q_LORA_RANK = 384
KV_LORA_RANK = 256
ROPE_THETA = 10000.0
POOL_WINDOWS = (2, 4, 8, 16)
N_POOL_GROUPS = 4
POOL_GROUP_DIM = 128
POOL_DIM = N_POOL_GROUPS * POOL_GROUP_DIM
LAT_DIM = 768
NORM_EPS = 1e-6
ADAM_LR = 0.001
ADAM_B1 = 0.9
ADAM_B2 = 0.999
ADAM_EPS = 1e-08
ADAM_WD = 0.01
ADAM_STEP = 10
NEG_INF = -1e30
LANES = 128
ATT_BLOCK = 512
VMEM_LIMIT = 48 * 1024 * 1024
MESH = pl.DeviceIdType.MESH
_NT = (((1,), (1,)), ((), ()))
_TN = (((0,), (0,)), ((), ()))

BIG = ("ffn1_up", "ffn1_down", "w_in", "w_pool_proj", "w_uq", "w_ukv", "w_attn_proj", "w_out",
       "ffn2_up", "ffn2_down")
ROW_SHARDED = ("ffn1_down", "w_attn_proj", "w_out", "ffn2_down")
SMALL = ("norm_ffn1", "norm_mix", "b_gate", "pool_maps", "pool_scale", "q_latent_norm",
         "kv_latent_norm", "norm_ffn2", "final_norm")
WEIGHTS = ("norm_ffn1", "ffn1_up", "ffn1_down", "norm_mix", "w_in", "b_gate", "pool_maps", "pool_scale",
           "w_pool_proj", "q_latent_norm", "w_uq", "kv_latent_norm", "w_ukv", "w_attn_proj", "w_out",
           "norm_ffn2", "ffn2_up", "ffn2_down", "final_norm")


def _pick(dim, cands):
    for c in cands:
        if c <= dim and dim % c == 0:
            return c
    return dim


def _cparams(sem=None, **kw):
    if sem is not None:
        kw["dimension_semantics"] = sem
    return pltpu.CompilerParams(vmem_limit_bytes=VMEM_LIMIT, **kw)


def _sigmoid(x):
    return 1.0 / (1.0 + jnp.exp(-x))


MM_TILE_BUDGET = 30 * 1024 * 1024
TILE_SIZES = (1408, 1024, 768, 512, 384, 256, 128)


V7X_MXU_FLOPS = 9.0e14
V7X_HBM_BYTES = 2.5e12
GRID_STEP_S = 0.35e-6


def _mm_tiles(M, N, K, sa, sb, so, sr):
    tks = [K] if K <= 2816 else [t for t in (2816, 2048, 1408, 1024, 512, 256, 128) if K % t == 0]
    best = None
    for tk in tks:
        for tm in [t for t in TILE_SIZES if M % t == 0] or [M]:
            for tn in [t for t in TILE_SIZES if N % t == 0] or [N]:
                need = 2 * (tm * tk * sa + tk * tn * sb + tm * tn * (so + sr)) + (tm * tn * 4 if tk < K else 0)
                if need > MM_TILE_BUDGET:
                    continue
                ni, nj, nk = M // tm, N // tn, K // tk
                a_bytes = M * K * sa * (nj if nk > 1 else 1)
                b_bytes = K * N * sb * (1 if nj == 1 and nk == 1 else ni)
                traffic = a_bytes + b_bytes + M * N * (so + sr) + (M * N * 8 * nk if nk > 1 else 0)
                t = max(2.0 * M * N * K / V7X_MXU_FLOPS, traffic / V7X_HBM_BYTES) + ni * nj * nk * GRID_STEP_S
                if best is None or t < best[0]:
                    best = (t, (tm, tn, tk))
    assert best is not None, (M, N, K)
    return best[1]


def mm(a, b, *, name, ta=False, tb=False, out_dtype=F32, res=None, alpha=1.0, dep=None):
    if ta:
        K, M = a.shape
    else:
        M, K = a.shape
    if tb:
        N, K2 = b.shape
    else:
        K2, N = b.shape
    assert K == K2, (a.shape, b.shape, ta, tb)
    tm, tn, tk = _mm_tiles(M, N, K, a.dtype.itemsize, b.dtype.itemsize, jnp.dtype(out_dtype).itemsize,
                           0 if res is None else res.dtype.itemsize)
    nk = K // tk
    dims = (((0 if ta else 1,), (1 if tb else 0,)), ((), ()))

    def body(*refs):
        a_ref, b_ref = refs[:2]
        r_ref = refs[2] if res is not None else None
        o_ref = refs[-2] if nk > 1 else refs[-1]

        def finish(r):
            if alpha != 1.0:
                r = r * alpha
            if res is not None:
                r = r_ref[...].astype(F32) + r
            o_ref[...] = r.astype(out_dtype)

        part = lax.dot_general(a_ref[...].astype(BF16), b_ref[...].astype(BF16), dims, preferred_element_type=F32)
        if nk == 1:
            finish(part)
            return
        acc = refs[-1]
        k = pl.program_id(2)

        @pl.when(k == 0)
        def _():
            acc[...] = part

        @pl.when(k > 0)
        def _():
            acc[...] += part

        @pl.when(k == nk - 1)
        def _():
            finish(acc[...])

    a_spec = pl.BlockSpec((tk, tm), lambda i, j, k: (k, i)) if ta else pl.BlockSpec((tm, tk), lambda i, j, k: (i, k))
    b_spec = pl.BlockSpec((tn, tk), lambda i, j, k: (j, k)) if tb else pl.BlockSpec((tk, tn), lambda i, j, k: (k, j))
    o_spec = pl.BlockSpec((tm, tn), lambda i, j, k: (i, j))
    in_specs = [a_spec, b_spec]
    args = [a, b]
    if res is not None:
        in_specs.append(o_spec)
        args.append(res)
    if dep is not None:
        in_specs.append(pl.BlockSpec((8, LANES), lambda i, j, k: (0, 0)))
        args.append(dep)
    return pl.pallas_call(
        body, name=name, grid=(M // tm, N // tn, nk), in_specs=in_specs, out_specs=o_spec,
        out_shape=jax.ShapeDtypeStruct((M, N), out_dtype),
        scratch_shapes=[pltpu.VMEM((tm, tn), F32)] if nk > 1 else [],
        compiler_params=_cparams(("parallel", "parallel", "arbitrary")),
    )(*args)


MXU_COLS = 256


def _col_chunks(n):
    return [(lo, min(lo + MXU_COLS, n)) for lo in range(0, n, MXU_COLS)]


def ffn_up_act(h, wu4, *, name):
    T, D = h.shape
    cq = wu4.shape[2]
    Fh = 2 * cq
    tm = _pick(T, (512, 256, 128))

    def body(h_ref, wg_ref, wu_ref, g_ref, u_ref, a_ref):
        hv = h_ref[...]
        for lo, hi in _col_chunks(cq):
            gv = jnp.dot(hv, wg_ref[0, :, lo:hi], preferred_element_type=F32)
            uv = jnp.dot(hv, wu_ref[0, :, lo:hi], preferred_element_type=F32)
            g_ref[:, lo:hi] = gv.astype(BF16)
            u_ref[:, lo:hi] = uv.astype(BF16)
            a_ref[:, lo:hi] = (gv * _sigmoid(gv) * uv).astype(BF16)

    tile = pl.BlockSpec((tm, cq), lambda j, i: (i, j))
    sh = jax.ShapeDtypeStruct((T, Fh), BF16)
    return pl.pallas_call(
        body, name=name, grid=(2, T // tm),
        in_specs=[pl.BlockSpec((tm, D), lambda j, i: (i, 0)), pl.BlockSpec((1, D, cq), lambda j, i: (j, 0, 0)),
                  pl.BlockSpec((1, D, cq), lambda j, i: (2 + j, 0, 0))],
        out_specs=[tile, tile, tile], out_shape=[sh, sh, sh],
        compiler_params=_cparams(("parallel", "parallel")),
    )(h, wu4, wu4)


def ffn_down_dx_act(dy, wd, g, u, *, dep=None, name):
    T, D = dy.shape
    Fh = wd.shape[0]
    cq = Fh // 2
    tm = _pick(T, (512, 256, 128))

    def body(dy_ref, wd_ref, g_ref, u_ref, *rest):
        dg_ref, du_ref = rest[-2:]
        dyv = dy_ref[...].astype(BF16)
        for lo, hi in _col_chunks(cq):
            da = 0.5 * lax.dot_general(dyv, wd_ref[lo:hi, :], _NT, preferred_element_type=F32)
            gv = g_ref[:, lo:hi].astype(F32)
            uv = u_ref[:, lo:hi].astype(F32)
            s = _sigmoid(gv)
            dg_ref[:, lo:hi] = (da * uv * (s * (1.0 + gv * (1.0 - s)))).astype(BF16)
            du_ref[:, lo:hi] = (da * (gv * s)).astype(BF16)

    tile = pl.BlockSpec((tm, cq), lambda j, i: (i, j))
    sh = jax.ShapeDtypeStruct((T, Fh), BF16)
    in_specs = [pl.BlockSpec((tm, D), lambda j, i: (i, 0)), pl.BlockSpec((cq, D), lambda j, i: (j, 0)), tile, tile]
    args = [dy, wd, g, u]
    if dep is not None:
        in_specs.append(pl.BlockSpec((8, LANES), lambda j, i: (0, 0)))
        args.append(dep)
    return pl.pallas_call(
        body, name=name, grid=(2, T // tm), in_specs=in_specs, out_specs=[tile, tile], out_shape=[sh, sh],
        compiler_params=_cparams(("parallel", "parallel")),
    )(*args)


def ffn_up_dw(h, dg, du, *, name):
    T, D = h.shape
    cq = dg.shape[1] // 2
    tk = _pick(T, (1024, 512, 256, 128))
    nk = T // tk

    def body(h_ref, dg_ref, du_ref, o_ref, acc):
        p = pl.program_id(0)
        k = pl.program_id(1)

        @pl.when(k == 0)
        def _():
            acc[...] = jnp.zeros_like(acc)

        @pl.when(p < 2)
        def _():
            acc[...] += lax.dot_general(h_ref[...], dg_ref[...], _TN, preferred_element_type=F32)

        @pl.when(p >= 2)
        def _():
            acc[...] += lax.dot_general(h_ref[...], du_ref[...], _TN, preferred_element_type=F32)

        @pl.when(k == nk - 1)
        def _():
            o_ref[0] = acc[...].astype(BF16)

    return pl.pallas_call(
        body, name=name, grid=(4, nk),
        in_specs=[pl.BlockSpec((tk, D), lambda p, k: (k, 0)),
                  pl.BlockSpec((tk, cq), lambda p, k: (jnp.where(p < 2, k, 0), jnp.minimum(p, 1))),
                  pl.BlockSpec((tk, cq), lambda p, k: (jnp.where(p < 2, 0, k), jnp.maximum(p - 2, 0)))],
        out_specs=pl.BlockSpec((1, D, cq), lambda p, k: (p, 0, 0)),
        out_shape=jax.ShapeDtypeStruct((4, D, cq), BF16),
        scratch_shapes=[pltpu.VMEM((D, cq), F32)],
        compiler_params=_cparams(("parallel", "arbitrary")),
    )(h, dg, du)


def ffn_up_dx_norm(dg, du, wu4, x, g, dy, *, name):
    T = dg.shape[0]
    _, D, cq = wu4.shape
    tm = _pick(T, (512, 256, 128))

    def body(dg_ref, du_ref, wg_ref, wu_ref, x_ref, g_ref, dy_ref, dx_ref, dgain_ref, acc):
        i = pl.program_id(0)
        k = pl.program_id(1)
        part = lax.dot_general(dg_ref[...], wg_ref[0], _NT, preferred_element_type=F32)
        part = part + lax.dot_general(du_ref[...], wu_ref[0], _NT, preferred_element_type=F32)

        @pl.when(jnp.logical_and(i == 0, k == 0))
        def _():
            dgain_ref[...] = jnp.zeros_like(dgain_ref)

        @pl.when(k == 0)
        def _():
            acc[...] = part

        @pl.when(k == 1)
        def _():
            dx, dgain = _rms_bwd_math(x_ref[...], g_ref[...], acc[...] + part)
            dx_ref[...] = dy_ref[...] + dx
            dgain_ref[...] += dgain

    tile = pl.BlockSpec((tm, cq), lambda i, k: (i, k))
    row = pl.BlockSpec((tm, D), lambda i, k: (i, 0))
    vec = pl.BlockSpec((1, D), lambda i, k: (0, 0))
    return pl.pallas_call(
        body, name=name, grid=(T // tm, 2),
        in_specs=[tile, tile, pl.BlockSpec((1, D, cq), lambda i, k: (k, 0, 0)),
                  pl.BlockSpec((1, D, cq), lambda i, k: (2 + k, 0, 0)), row, vec, row],
        out_specs=[row, vec],
        out_shape=[jax.ShapeDtypeStruct((T, D), F32), jax.ShapeDtypeStruct((1, D), F32)],
        scratch_shapes=[pltpu.VMEM((tm, D), F32)],
        compiler_params=_cparams(("arbitrary", "arbitrary")),
    )(dg, du, wu4, wu4, x, g.reshape(1, D), dy)


def mix_in(h, w_lat, w_pool, w_gate, *, name):
    T, D = h.shape
    tm = _pick(T, (512, 256, 128))

    def body(h_ref, wl_ref, wp_ref, wg_ref, lat_ref, xp_ref, gl_ref):
        hv = h_ref[...]
        lat_ref[...] = jnp.dot(hv, wl_ref[...], preferred_element_type=F32)
        xp_ref[...] = jnp.dot(hv, wp_ref[...], preferred_element_type=F32)
        gl_ref[...] = jnp.dot(hv, wg_ref[...], preferred_element_type=F32)

    whole = lambda a: pl.BlockSpec(a.shape, lambda i: (0, 0))
    out = lambda a: pl.BlockSpec((tm, a.shape[1]), lambda i: (i, 0))
    return pl.pallas_call(
        body, name=name, grid=(T // tm,),
        in_specs=[pl.BlockSpec((tm, D), lambda i: (i, 0)), whole(w_lat), whole(w_pool), whole(w_gate)],
        out_specs=[out(w_lat), out(w_pool), out(w_gate)],
        out_shape=[jax.ShapeDtypeStruct((T, a.shape[1]), F32) for a in (w_lat, w_pool, w_gate)],
        compiler_params=_cparams(("parallel",)),
    )(h, w_lat, w_pool, w_gate)


def mix_in_dx_norm(dlat, dxp, dgl, w_lat, w_pool, w_gate, x, g, dy, *, name):
    T, D = x.shape
    tm = _pick(T, (512, 256, 128))

    def body(dlat_ref, dxp_ref, dgl_ref, wl_ref, wp_ref, wg_ref, x_ref, g_ref, dy_ref, dx_ref, dgain_ref):
        @pl.when(pl.program_id(0) == 0)
        def _():
            dgain_ref[...] = jnp.zeros_like(dgain_ref)

        dh = lax.dot_general(dlat_ref[...], wl_ref[...], _NT, preferred_element_type=F32)
        dh = dh + lax.dot_general(dxp_ref[...], wp_ref[...], _NT, preferred_element_type=F32)
        dh = dh + lax.dot_general(dgl_ref[...], wg_ref[...], _NT, preferred_element_type=F32)
        dx, dgain = _rms_bwd_math(x_ref[...], g_ref[...], dh)
        dx_ref[...] = dy_ref[...] + dx
        dgain_ref[...] += dgain

    row = lambda a: pl.BlockSpec((tm, a.shape[1]), lambda i: (i, 0))
    whole = lambda a: pl.BlockSpec(a.shape, lambda i: (0, 0))
    vec = pl.BlockSpec((1, D), lambda i: (0, 0))
    return pl.pallas_call(
        body, name=name, grid=(T // tm,),
        in_specs=[row(dlat), row(dxp), row(dgl), whole(w_lat), whole(w_pool), whole(w_gate), row(x), vec, row(dy)],
        out_specs=[row(x), vec],
        out_shape=[jax.ShapeDtypeStruct((T, D), F32), jax.ShapeDtypeStruct((1, D), F32)],
        compiler_params=_cparams(("arbitrary",)),
    )(dlat, dxp, dgl, w_lat, w_pool, w_gate, x, g.reshape(1, D), dy)


def _rows(T, width_bytes):
    cap = max(8, (2 * 1024 * 1024) // width_bytes)
    return _pick(T, tuple(c for c in (1024, 512, 256, 128, 64, 32, 16) if c <= cap))


def rms_fwd(x, g, *, name, dep=None):
    T, D = x.shape
    tm = _rows(T, D * 4)

    def body(x_ref, g_ref, *rest):
        xv = x_ref[...]
        r = lax.rsqrt(jnp.mean(xv * xv, axis=-1, keepdims=True) + NORM_EPS)
        rest[-1][...] = (xv * r * g_ref[...]).astype(BF16)

    in_specs = [pl.BlockSpec((tm, D), lambda i: (i, 0)), pl.BlockSpec((1, D), lambda i: (0, 0))]
    args = [x, g.reshape(1, D)]
    if dep is not None:
        in_specs.append(pl.BlockSpec((8, LANES), lambda i: (0, 0)))
        args.append(dep)
    return pl.pallas_call(
        body, name=name, grid=(T // tm,), in_specs=in_specs,
        out_specs=pl.BlockSpec((tm, D), lambda i: (i, 0)),
        out_shape=jax.ShapeDtypeStruct((T, D), BF16),
        compiler_params=_cparams(("parallel",)),
    )(*args)


def _rms_bwd_math(xv, gv, dh):
    r = lax.rsqrt(jnp.mean(xv * xv, axis=-1, keepdims=True) + NORM_EPS)
    xn = xv * r
    dg = jnp.sum(dh * xn, axis=0, keepdims=True)
    dxn = dh * gv
    dx = r * (dxn - xn * jnp.mean(dxn * xn, axis=-1, keepdims=True))
    return dx, dg


def _rope(xv, cv, sv):
    half = QK_ROPE_DIM // 2
    lane = lax.broadcasted_iota(jnp.int32, xv.shape, 1)
    swapped = jnp.where(lane < half, pltpu.roll(xv, LANES - half, 1), pltpu.roll(xv, half, 1))
    return xv * cv + swapped * sv


def _rope_t(dy, cv, sv):
    half = QK_ROPE_DIM // 2
    ds = dy * sv
    lane = lax.broadcasted_iota(jnp.int32, dy.shape, 1)
    swapped = jnp.where(lane < half, pltpu.roll(ds, LANES - half, 1), pltpu.roll(ds, half, 1))
    return dy * cv + swapped


def lat_fwd(lat, qn_w, kvn_w, cs, sn, *, name):
    T = lat.shape[0]
    tm = _rows(T, LAT_DIM * 4)
    kv0 = Q_LORA_RANK
    kr0 = Q_LORA_RANK + KV_LORA_RANK

    def body(lat_ref, qw_ref, kw_ref, c_ref, s_ref, qn_ref, kvn_ref, kr_ref):
        ql = lat_ref[:, :kv0]
        r = lax.rsqrt(jnp.mean(ql * ql, axis=-1, keepdims=True) + NORM_EPS)
        qn_ref[...] = (ql * r * qw_ref[...]).astype(BF16)
        kl = lat_ref[:, kv0:kr0]
        r = lax.rsqrt(jnp.mean(kl * kl, axis=-1, keepdims=True) + NORM_EPS)
        kvn_ref[...] = (kl * r * kw_ref[...]).astype(BF16)
        kr_ref[...] = _rope(lat_ref[:, kr0:], c_ref[...], s_ref[...]).astype(BF16)

    return pl.pallas_call(
        body, name=name, grid=(T // tm,),
        in_specs=[pl.BlockSpec((tm, LAT_DIM), lambda i: (i, 0)),
                  pl.BlockSpec((1, Q_LORA_RANK), lambda i: (0, 0)),
                  pl.BlockSpec((1, KV_LORA_RANK), lambda i: (0, 0)),
                  pl.BlockSpec((tm, LANES), lambda i: (i, 0)), pl.BlockSpec((tm, LANES), lambda i: (i, 0))],
        out_specs=[pl.BlockSpec((tm, Q_LORA_RANK), lambda i: (i, 0)),
                   pl.BlockSpec((tm, KV_LORA_RANK), lambda i: (i, 0)),
                   pl.BlockSpec((tm, LANES), lambda i: (i, 0))],
        out_shape=[jax.ShapeDtypeStruct((T, Q_LORA_RANK), BF16), jax.ShapeDtypeStruct((T, KV_LORA_RANK), BF16),
                   jax.ShapeDtypeStruct((T, LANES), BF16)],
        compiler_params=_cparams(("parallel",)),
    )(lat, qn_w.reshape(1, -1), kvn_w.reshape(1, -1), cs, sn)


def lat_bwd(lat, qn_w, kvn_w, dqn, dkvn, dkr, cs, sn, *, name):
    T = lat.shape[0]
    tm = _rows(T, LAT_DIM * 4)
    kv0 = Q_LORA_RANK
    kr0 = Q_LORA_RANK + KV_LORA_RANK

    def body(lat_ref, qw_ref, kw_ref, dqn_ref, dkvn_ref, dkr_ref, c_ref, s_ref, dlat_ref, dqw_ref, dkw_ref):
        @pl.when(pl.program_id(0) == 0)
        def _():
            dqw_ref[...] = jnp.zeros_like(dqw_ref)
            dkw_ref[...] = jnp.zeros_like(dkw_ref)

        dx, dg = _rms_bwd_math(lat_ref[:, :kv0], qw_ref[...], dqn_ref[...])
        dlat_ref[:, :kv0] = dx.astype(BF16)
        dqw_ref[...] += dg
        dx, dg = _rms_bwd_math(lat_ref[:, kv0:kr0], kw_ref[...], dkvn_ref[...])
        dlat_ref[:, kv0:kr0] = dx.astype(BF16)
        dkw_ref[...] += dg
        dlat_ref[:, kr0:] = _rope_t(dkr_ref[...], c_ref[...], s_ref[...]).astype(BF16)

    row = lambda w: pl.BlockSpec((tm, w), lambda i: (i, 0))
    vec = lambda w: pl.BlockSpec((1, w), lambda i: (0, 0))
    return pl.pallas_call(
        body, name=name, grid=(T // tm,),
        in_specs=[row(LAT_DIM), vec(Q_LORA_RANK), vec(KV_LORA_RANK), row(Q_LORA_RANK), row(KV_LORA_RANK),
                  row(LANES), row(LANES), row(LANES)],
        out_specs=[row(LAT_DIM), vec(Q_LORA_RANK), vec(KV_LORA_RANK)],
        out_shape=[jax.ShapeDtypeStruct((T, LAT_DIM), BF16), jax.ShapeDtypeStruct((1, Q_LORA_RANK), F32),
                   jax.ShapeDtypeStruct((1, KV_LORA_RANK), F32)],
        compiler_params=_cparams(("arbitrary",)),
    )(lat, qn_w.reshape(1, -1), kvn_w.reshape(1, -1), dqn, dkvn, dkr, cs, sn)


def attn_proj_gate(o, wap, gl, bg, ba, *, name):
    T, D2 = gl.shape
    D = D2 // 2
    tm = _pick(T, (256, 128))

    def body(o_ref, w_ref, gl_ref, bg_ref, ba_ref, bb_ref, m_ref):
        bb = jnp.dot(o_ref[...], w_ref[...], preferred_element_type=F32)
        bb_ref[...] = bb
        ga = _sigmoid(gl_ref[:, :D] + bg_ref[:, :D])
        gb = _sigmoid(gl_ref[:, D:] + bg_ref[:, D:])
        m_ref[...] = (ga * ba_ref[...] + gb * bb).astype(BF16)

    row = lambda w: pl.BlockSpec((tm, w), lambda i: (i, 0))
    return pl.pallas_call(
        body, name=name, grid=(T // tm,),
        in_specs=[row(o.shape[1]), pl.BlockSpec(wap.shape, lambda i: (0, 0)), row(D2),
                  pl.BlockSpec((1, D2), lambda i: (0, 0)), row(D)],
        out_specs=[row(D), row(D)],
        out_shape=[jax.ShapeDtypeStruct((T, D), F32), jax.ShapeDtypeStruct((T, D), BF16)],
        compiler_params=_cparams(("parallel",)),
    )(o, wap, gl, bg.reshape(1, D2), ba)


def out_dx_gate(dy, wo, gl, bg, ba, bb, *, name):
    T, D2 = gl.shape
    D = D2 // 2
    tm = _pick(T, (256, 128))

    def body(dy_ref, w_ref, gl_ref, bg_ref, ba_ref, bb_ref, dba_ref, dbb_ref, dgl_ref, dbg_ref):
        @pl.when(pl.program_id(0) == 0)
        def _():
            dbg_ref[...] = jnp.zeros_like(dbg_ref)

        dmv = lax.dot_general(dy_ref[...].astype(BF16), w_ref[...], _NT, preferred_element_type=F32)
        ga = _sigmoid(gl_ref[:, :D] + bg_ref[:, :D])
        gb = _sigmoid(gl_ref[:, D:] + bg_ref[:, D:])
        dba_ref[...] = (dmv * ga).astype(BF16)
        dbb_ref[...] = (dmv * gb).astype(BF16)
        dla = dmv * ba_ref[...] * ga * (1.0 - ga)
        dlb = dmv * bb_ref[...] * gb * (1.0 - gb)
        dgl_ref[:, :D] = dla.astype(BF16)
        dgl_ref[:, D:] = dlb.astype(BF16)
        dbg_ref[:, :D] += jnp.sum(dla, axis=0, keepdims=True)
        dbg_ref[:, D:] += jnp.sum(dlb, axis=0, keepdims=True)

    row = lambda w: pl.BlockSpec((tm, w), lambda i: (i, 0))
    vec = pl.BlockSpec((1, D2), lambda i: (0, 0))
    return pl.pallas_call(
        body, name=name, grid=(T // tm,),
        in_specs=[row(D), pl.BlockSpec(wo.shape, lambda i: (0, 0)), row(D2), vec, row(D), row(D)],
        out_specs=[row(D), row(D), row(D2), vec],
        out_shape=[jax.ShapeDtypeStruct((T, D), BF16), jax.ShapeDtypeStruct((T, D), BF16),
                   jax.ShapeDtypeStruct((T, D2), BF16), jax.ShapeDtypeStruct((1, D2), F32)],
        compiler_params=_cparams(("arbitrary",)),
    )(dy, wo, gl, bg.reshape(1, D2), ba, bb)


def loss_head(x, gf, tgt, *, name):
    T, D = x.shape
    tm = _rows(T, D * 4)

    def body(x_ref, g_ref, t_ref, dx_ref, dg_ref, loss_ref):
        @pl.when(pl.program_id(0) == 0)
        def _():
            dg_ref[...] = jnp.zeros_like(dg_ref)
            loss_ref[...] = jnp.zeros_like(loss_ref)

        xv = x_ref[...]
        gv = g_ref[...]
        r = lax.rsqrt(jnp.mean(xv * xv, axis=-1, keepdims=True) + NORM_EPS)
        xn = xv * r
        err = xn * gv - t_ref[...]
        loss_ref[...] += 0.5 * jnp.sum(jnp.mean(err * err, axis=-1, keepdims=True))
        dy = err * (1.0 / D)
        dg_ref[...] += jnp.sum(dy * xn, axis=0, keepdims=True)
        dxn = dy * gv
        dx_ref[...] = r * (dxn - xn * jnp.mean(dxn * xn, axis=-1, keepdims=True))

    row = pl.BlockSpec((tm, D), lambda i: (i, 0))
    vec = pl.BlockSpec((1, D), lambda i: (0, 0))
    return pl.pallas_call(
        body, name=name, grid=(T // tm,), in_specs=[row, vec, row],
        out_specs=[row, vec, pl.BlockSpec((8, LANES), lambda i: (0, 0))],
        out_shape=[jax.ShapeDtypeStruct((T, D), F32), jax.ShapeDtypeStruct((1, D), F32),
                   jax.ShapeDtypeStruct((8, LANES), F32)],
        compiler_params=_cparams(("arbitrary",)),
    )(x, gf.reshape(1, D), tgt)


def _shift_rows(s, k, down):
    n = s.shape[0]
    t = lax.broadcasted_iota(jnp.int32, s.shape, 0)
    if down:
        return jnp.where(t >= k, pltpu.roll(s, k, 0), 0.0)
    return jnp.where(t < n - k, pltpu.roll(s, n - k, 0), 0.0)


def _window_sum(s, w, down):
    k = 1
    while k < w:
        s = s + _shift_rows(s, k, down)
        k *= 2
    return s


def _pool_count(shape, w):
    t = lax.broadcasted_iota(jnp.int32, shape, 0)
    return jnp.minimum(t + 1, w).astype(F32)


def pool_fwd(xp, maps, scale, B, *, name):
    T, P = xp.shape
    S = T // B
    G = POOL_GROUP_DIM

    def body(x_ref, m_ref, sc_ref, o_ref):
        for g, w in enumerate(POOL_WINDOWS):
            xg = x_ref[:, g * G:(g + 1) * G]
            pooled = _window_sum(xg, w, True) / _pool_count(xg.shape, w) - xg
            mixed = jnp.dot(pooled.astype(BF16), m_ref[g], preferred_element_type=F32)
            o_ref[:, g * G:(g + 1) * G] = (mixed * sc_ref[:, g * G:(g + 1) * G]).astype(BF16)

    return pl.pallas_call(
        body, name=name, grid=(B,),
        in_specs=[pl.BlockSpec((S, P), lambda b: (b, 0)), pl.BlockSpec((N_POOL_GROUPS, G, G), lambda b: (0, 0, 0)),
                  pl.BlockSpec((1, P), lambda b: (0, 0))],
        out_specs=pl.BlockSpec((S, P), lambda b: (b, 0)),
        out_shape=jax.ShapeDtypeStruct((T, P), BF16),
        compiler_params=_cparams(("parallel",)),
    )(xp, maps, scale.reshape(1, P))


def pool_bwd(xp, dmixed, maps, scale, B, *, name):
    T, P = xp.shape
    S = T // B
    G = POOL_GROUP_DIM

    def body(x_ref, dm_ref, m_ref, sc_ref, dx_ref, dmaps_ref, dsc_ref):
        @pl.when(pl.program_id(0) == 0)
        def _():
            dmaps_ref[...] = jnp.zeros_like(dmaps_ref)
            dsc_ref[...] = jnp.zeros_like(dsc_ref)

        for g, w in enumerate(POOL_WINDOWS):
            cols = slice(g * G, (g + 1) * G)
            xg = x_ref[:, cols]
            cnt = _pool_count(xg.shape, w)
            pooled = (_window_sum(xg, w, True) / cnt - xg).astype(BF16)
            mixed = jnp.dot(pooled, m_ref[g], preferred_element_type=F32)
            dmx = dm_ref[:, cols]
            dsc_ref[:, cols] += jnp.sum(dmx * mixed, axis=0, keepdims=True)
            dmp = (dmx * sc_ref[:, cols]).astype(BF16)
            dmaps_ref[g] += lax.dot_general(pooled, dmp, (((0,), (0,)), ((), ())), preferred_element_type=F32)
            dpooled = lax.dot_general(dmp, m_ref[g], (((1,), (1,)), ((), ())), preferred_element_type=F32)
            dx_ref[:, cols] = (_window_sum(dpooled / cnt, w, False) - dpooled).astype(BF16)

    blk = pl.BlockSpec((S, P), lambda b: (b, 0))
    mp = pl.BlockSpec((N_POOL_GROUPS, G, G), lambda b: (0, 0, 0))
    vec = pl.BlockSpec((1, P), lambda b: (0, 0))
    return pl.pallas_call(
        body, name=name, grid=(B,), in_specs=[blk, blk, mp, vec], out_specs=[blk, mp, vec],
        out_shape=[jax.ShapeDtypeStruct((T, P), BF16), jax.ShapeDtypeStruct((N_POOL_GROUPS, G, G), F32),
                   jax.ShapeDtypeStruct((1, P), F32)],
        compiler_params=_cparams(("arbitrary",)),
    )(xp, dmixed, maps, scale.reshape(1, P))


def _keys(kv_ref, kr_ref, rows):
    return jnp.concatenate([kv_ref[rows, :QK_NOPE_DIM], kr_ref[rows, :]], axis=1)


def _causal(s):
    row = lax.broadcasted_iota(jnp.int32, s.shape, 0)
    col = lax.broadcasted_iota(jnp.int32, s.shape, 1)
    return jnp.where(row >= col, s, NEG_INF)


def attn_fwd(q, kv, kr, cs, sn, B, *, name):
    T = q.shape[0]
    S = T // B
    blk = min(ATT_BLOCK, S)
    nb = S // blk
    H = N_HEADS
    scale = QK_DIM ** -0.5

    def body(q_ref, kv_ref, kr_ref, c_ref, s_ref, o_ref, lse_ref, qs_ref):
        qs_ref[:, :QK_NOPE_DIM] = (q_ref[:, :QK_NOPE_DIM] * scale).astype(BF16)
        qs_ref[:, QK_NOPE_DIM:] = _rope(q_ref[:, QK_NOPE_DIM:], c_ref[...] * scale, s_ref[...] * scale).astype(BF16)
        for qi in range(nb):
            rows = slice(qi * blk, (qi + 1) * blk)
            qb = qs_ref[rows, :]
            sd = _causal(lax.dot_general(qb, _keys(kv_ref, kr_ref, rows), _NT, preferred_element_type=F32))
            m = jnp.max(sd, axis=-1, keepdims=True)
            if qi > 0:
                prev = slice(0, qi * blk)
                sp = lax.dot_general(qb, _keys(kv_ref, kr_ref, prev), _NT, preferred_element_type=F32)
                m = jnp.maximum(m, jnp.max(sp, axis=-1, keepdims=True))
            pd = jnp.exp(sd - m)
            l = jnp.sum(pd, axis=-1, keepdims=True)
            acc = jnp.dot(pd.astype(BF16), kv_ref[rows, QK_NOPE_DIM:], preferred_element_type=F32)
            if qi > 0:
                pp = jnp.exp(sp - m)
                l = l + jnp.sum(pp, axis=-1, keepdims=True)
                acc = acc + jnp.dot(pp.astype(BF16), kv_ref[prev, QK_NOPE_DIM:], preferred_element_type=F32)
            o_ref[rows, :] = (acc / l).astype(BF16)
            lse_ref[0, rows, :] = m + jnp.log(l)

    head = pl.BlockSpec((S, HEAD_PAD), lambda b, h: (b, h))
    shared = pl.BlockSpec((S, LANES), lambda b, h: (b, 0))
    return pl.pallas_call(
        body, name=name, grid=(B, H),
        in_specs=[head, head, shared, shared, shared],
        out_specs=[pl.BlockSpec((S, V_HEAD_DIM), lambda b, h: (b, h)), pl.BlockSpec((1, S, 1), lambda b, h: (h, b, 0)),
                   head],
        out_shape=[jax.ShapeDtypeStruct((T, H * V_HEAD_DIM), BF16), jax.ShapeDtypeStruct((H, T, 1), F32),
                   jax.ShapeDtypeStruct((T, H * HEAD_PAD), BF16)],
        compiler_params=_cparams(("parallel", "parallel")),
    )(q, kv, kr, cs, sn)


def attn_bwd(q, kv, kr, o, do, lse, cs, sn, B, *, name):
    T = q.shape[0]
    S = T // B
    blk = min(ATT_BLOCK, S)
    nb = S // blk
    H = N_HEADS
    scale = QK_DIM ** -0.5

    def body(q_ref, kv_ref, kr_ref, o_ref, do_ref, lse_ref, c_ref, s_ref, dq_ref, dkv_ref, dkr_ref, dk_s, dv_s):
        dk_s[...] = jnp.zeros_like(dk_s)
        dv_s[...] = jnp.zeros_like(dv_s)

        @pl.when(pl.program_id(1) == 0)
        def _():
            dkr_ref[...] = jnp.zeros_like(dkr_ref)

        for qi in range(nb):
            rows = slice(qi * blk, (qi + 1) * blk)
            qb = q_ref[rows, :]
            dob = do_ref[rows, :]
            delta = jnp.sum(dob.astype(F32) * o_ref[rows, :].astype(F32), axis=-1, keepdims=True)
            lse_b = lse_ref[0, rows, :]

            def part(ks, diagonal):
                k = _keys(kv_ref, kr_ref, ks)
                s = lax.dot_general(qb, k, _NT, preferred_element_type=F32)
                if diagonal:
                    s = _causal(s)
                p = jnp.exp(s - lse_b)
                dp = lax.dot_general(dob, kv_ref[ks, QK_NOPE_DIM:], _NT, preferred_element_type=F32)
                ds = (p * (dp - delta)).astype(BF16)
                dv_s[ks, :] += lax.dot_general(p.astype(BF16), dob, _TN, preferred_element_type=F32)
                dk_s[ks, :] += lax.dot_general(ds, qb, _TN, preferred_element_type=F32)
                return jnp.dot(ds, k, preferred_element_type=F32)

            dq = part(rows, True)
            if qi > 0:
                dq = dq + part(slice(0, qi * blk), False)
            dq_ref[rows, :QK_NOPE_DIM] = (dq[:, :QK_NOPE_DIM] * scale).astype(BF16)
            dq_ref[rows, QK_NOPE_DIM:] = _rope_t(dq[:, QK_NOPE_DIM:], c_ref[rows, :] * scale,
                                                 s_ref[rows, :] * scale).astype(BF16)

        dkv_ref[:, :QK_NOPE_DIM] = dk_s[:, :QK_NOPE_DIM].astype(BF16)
        dkv_ref[:, QK_NOPE_DIM:] = dv_s[...].astype(BF16)
        dkr_ref[...] += dk_s[:, QK_NOPE_DIM:]

    head = lambda w: pl.BlockSpec((S, w), lambda b, h: (b, h))
    shared = pl.BlockSpec((S, LANES), lambda b, h: (b, 0))
    return pl.pallas_call(
        body, name=name, grid=(B, H),
        in_specs=[head(HEAD_PAD), head(HEAD_PAD), shared, head(V_HEAD_DIM), head(V_HEAD_DIM),
                  pl.BlockSpec((1, S, 1), lambda b, h: (h, b, 0)), shared, shared],
        out_specs=[head(HEAD_PAD), head(HEAD_PAD), shared],
        out_shape=[jax.ShapeDtypeStruct((T, H * HEAD_PAD), BF16), jax.ShapeDtypeStruct((T, H * HEAD_PAD), BF16),
                   jax.ShapeDtypeStruct((T, LANES), F32)],
        scratch_shapes=[pltpu.VMEM((S, HEAD_PAD), F32), pltpu.VMEM((S, V_HEAD_DIM), F32)],
        compiler_params=_cparams(("parallel", "arbitrary")),
    )(q, kv, kr, o, do, lse, cs, sn)


def adamw(w, g, m, v, *, name, dep=None):
    R, C = w.shape
    cap = max(8, (1024 * 1024) // (C * 4))
    tr = _pick(R, tuple(c for c in (1024, 512, 256, 128, 64, 32, 16, 8) if c <= cap))
    c1 = 1.0 - ADAM_B1 ** ADAM_STEP
    c2 = 1.0 - ADAM_B2 ** ADAM_STEP

    def body(w_ref, g_ref, m_ref, v_ref, *rest):
        d_ref, nm_ref, nv_ref = rest[-3:]
        gv = g_ref[...]
        mv = ADAM_B1 * m_ref[...] + (1.0 - ADAM_B1) * gv
        vv = ADAM_B2 * v_ref[...] + (1.0 - ADAM_B2) * (gv * gv)
        nm_ref[...] = mv
        nv_ref[...] = vv
        d_ref[...] = -ADAM_LR * ((mv / c1) / (jnp.sqrt(vv / c2) + ADAM_EPS) + ADAM_WD * w_ref[...])

    blk = pl.BlockSpec((tr, C), lambda i: (i, 0))
    sh = jax.ShapeDtypeStruct((R, C), F32)
    extra = [] if dep is None else [dep]
    return pl.pallas_call(
        body, name=name, grid=(R // tr,), in_specs=[blk] * 4 + [ANY] * len(extra), out_specs=[blk] * 3,
        out_shape=[sh] * 3, compiler_params=_cparams(("parallel",)),
    )(w, g, m, v, *extra)


ANY = pl.BlockSpec(memory_space=pl.ANY)


def _place():
    x, y, c = lax.axis_index("x"), lax.axis_index("y"), lax.axis_index("c")
    others = [(1 - x, y), (x, 1 - y), (1 - x, 1 - y)]
    return x, y, c, others


def _remote(src, dst, ssem, rsem, dev):
    return pltpu.make_async_remote_copy(src_ref=src, dst_ref=dst, send_sem=ssem, recv_sem=rsem,
                                        device_id=dev, device_id_type=MESH)


def _half(ref_rows, c):
    hr = ref_rows // 2
    return pl.ds(pl.multiple_of(c * hr, 16), hr)


HBM = pl.BlockSpec(memory_space=pltpu.HBM)
SEMS = pl.BlockSpec(memory_space=pltpu.SEMAPHORE)
EFFECT = pltpu.SideEffectType.DATAFLOW_SIDE_EFFECTING


def exchange_begin(name, srcs, land_shapes, plan, ncopies, after=None):
    ns, nl = len(srcs), len(land_shapes)
    nin = ns + nl + (0 if after is None else 1)

    def body(*refs):
        ssem, rsem = refs[nin], refs[nin + 1]
        for k, (s, d, dev) in enumerate(plan(refs[:ns], refs[ns:ns + nl])):
            _remote(s, d, ssem.at[k], rsem.at[k], dev).start()
        refs[-1][...] = jnp.zeros_like(refs[-1])

    bufs = [pltpu.HBM(s.shape, s.dtype) for s in srcs] + [pltpu.HBM(s.shape, s.dtype) for s in land_shapes]
    args = [pltpu.with_memory_space_constraint(s, pltpu.HBM) for s in srcs]
    args += [pltpu.with_memory_space_constraint(lax.empty(s.shape, s.dtype), pltpu.HBM) for s in land_shapes]
    if after is not None:
        args.append(after)
    out = pl.pallas_call(
        body, name=name,
        out_shape=(pltpu.SemaphoreType.DMA((ncopies,)), pltpu.SemaphoreType.DMA((ncopies,)), *bufs,
                   jax.ShapeDtypeStruct((8, LANES), F32)),
        in_specs=[HBM] * (ns + nl) + ([] if after is None else [ANY]),
        out_specs=(SEMS, SEMS, *([HBM] * (ns + nl)), pl.BlockSpec(memory_space=pltpu.VMEM)),
        input_output_aliases={i: 2 + i for i in range(ns + nl)},
        compiler_params=pltpu.CompilerParams(has_side_effects=EFFECT),
    )(*args)
    return (out[0], out[1], out[2:2 + ns], out[2 + ns:2 + ns + nl]), out[-1]


def exchange_end(name, handle, plan, after):
    ssem, rsem, srcs, lands = handle
    ns, nl = len(srcs), len(lands)

    def body(*refs):
        ssem_ref, rsem_ref = refs[ns + nl], refs[ns + nl + 1]
        for k, (s, d, dev) in enumerate(plan(refs[:ns], refs[ns:ns + nl])):
            cp = _remote(s, d, ssem_ref.at[k], rsem_ref.at[k], dev)
            cp.wait_send()
            cp.wait_recv()

    out = pl.pallas_call(
        body, name=name,
        out_shape=tuple(pltpu.HBM(s.shape, s.dtype) for s in (*srcs, *lands)),
        in_specs=[HBM] * (ns + nl) + [SEMS, SEMS, ANY], out_specs=tuple([HBM] * (ns + nl)),
        input_output_aliases={i: i for i in range(ns + nl)},
        compiler_params=pltpu.CompilerParams(has_side_effects=EFFECT),
    )(*srcs, *lands, ssem, rsem, after)
    return list(out[:ns]), list(out[ns:])


def ag_plan(src_refs, land_refs):
    x, y, c, others = _place()
    plan = []
    for s, d in zip(src_refs, land_refs):
        mine = _half(s.shape[0], c)
        for ox, oy in others:
            plan.append((s.at[mine, :], d.at[2 * x + y, mine, :], (ox, oy, c)))
    return plan


def ag_forward(lands, *, name):
    n = len(lands)

    def body(*refs):
        ins, outs = refs[:n], refs[n:2 * n]
        ssem, rsem = refs[2 * n:]
        x, y, c, others = _place()
        sent = []
        for i in range(n):
            mine = _half(ins[i].shape[1], c)
            for j, (ox, oy) in enumerate(others):
                cp = _remote(ins[i].at[2 * ox + oy, mine, :], outs[i].at[2 * ox + oy, mine, :], ssem.at[3 * i + j],
                             rsem.at[3 * i + j], (x, y, 1 - c))
                cp.start()
                sent.append(cp)
        for cp in sent:
            cp.wait()

    return pl.pallas_call(
        body, name=name, in_specs=[ANY] * n, out_specs=[ANY] * n,
        out_shape=[jax.ShapeDtypeStruct(a.shape, a.dtype) for a in lands],
        input_output_aliases={i: i for i in range(n)},
        scratch_shapes=[pltpu.SemaphoreType.DMA((3 * n,)), pltpu.SemaphoreType.DMA((3 * n,))],
        compiler_params=pltpu.CompilerParams(has_side_effects=True),
    )(*lands)


def place_own(lands, own, chip, *, name):
    n = len(lands)
    steps = 4

    def body(chip_ref, *refs):
        for i in range(n):
            refs[2 * n + i][0] = refs[i][...]

    in_specs = [pl.BlockSpec((o.shape[0] // steps, o.shape[1]), lambda t, q: (t, 0)) for o in own] + [ANY] * n
    out_specs = [pl.BlockSpec((1, o.shape[0] // steps, o.shape[1]), lambda t, q: (q[0], t, 0)) for o in own]
    return pl.pallas_call(
        body, name=name,
        grid_spec=pltpu.PrefetchScalarGridSpec(num_scalar_prefetch=1, grid=(steps,), in_specs=in_specs,
                                               out_specs=out_specs),
        out_shape=[jax.ShapeDtypeStruct(a.shape, a.dtype) for a in lands],
        input_output_aliases={1 + n + i: i for i in range(n)},
        compiler_params=_cparams(("parallel",)),
    )(chip, *own, *lands)


def rs_swap_halves(grads, *, name):
    n = len(grads)

    def body(*refs):
        ins, outs = refs[:n], refs[n:2 * n]
        ssem, rsem = refs[2 * n:]
        x, y, c, _ = _place()
        cps = []
        for i in range(n):
            theirs = _half(ins[i].shape[1], 1 - c)
            cp = _remote(ins[i].at[:, theirs, :], outs[i], ssem.at[i], rsem.at[i], (x, y, 1 - c))
            cp.start()
            cps.append(cp)
        for cp in cps:
            cp.wait()

    return pl.pallas_call(
        body, name=name, in_specs=[ANY] * n, out_specs=[ANY] * n,
        out_shape=[jax.ShapeDtypeStruct((4, g.shape[1] // 2, g.shape[2]), g.dtype) for g in grads],
        scratch_shapes=[pltpu.SemaphoreType.DMA((n,)), pltpu.SemaphoreType.DMA((n,))],
        compiler_params=pltpu.CompilerParams(has_side_effects=True),
    )(*grads)


def rs_chip_sum(g, r1, core, *, name):
    _, r, cdim = g.shape
    hr = r // 2

    def body(c_ref, g_ref, r1_ref, o_ref):
        o_ref[...] = (g_ref[...].astype(F32) + r1_ref[...].astype(F32)).astype(BF16)

    return pl.pallas_call(
        body, name=name,
        grid_spec=pltpu.PrefetchScalarGridSpec(
            num_scalar_prefetch=1, grid=(4,),
            in_specs=[pl.BlockSpec((1, hr, cdim), lambda qq, c_ref: (qq, c_ref[0], 0)),
                      pl.BlockSpec((1, hr, cdim), lambda qq, c_ref: (qq, 0, 0))],
            out_specs=pl.BlockSpec((1, hr, cdim), lambda qq, c_ref: (qq, 0, 0))),
        out_shape=jax.ShapeDtypeStruct((4, hr, cdim), BF16),
        compiler_params=_cparams(("parallel",)),
    )(core, g, r1)


def rs_plan(src_refs, land_refs):
    x, y, c, others = _place()
    plan = []
    for s, d in zip(src_refs, land_refs):
        for j, (ox, oy) in enumerate(others):
            plan.append((s.at[2 * ox + oy], d.at[j], (ox, oy, c)))
    return plan


def rs_final_sum(g, r1, r2, place, acc, l, *, name):
    _, r, cdim = g.shape
    hr = r // 2
    ch = hr // 2

    def body(p_ref, g_ref, r1_ref, a_ref, b_ref, d_ref, acc_in, o_ref):
        s = g_ref[...].astype(F32) + r1_ref[...].astype(F32)
        s = s + a_ref[...].astype(F32)
        s = s + b_ref[...].astype(F32)
        o_ref[...] = s + d_ref[...].astype(F32)

    other = lambda j: pl.BlockSpec((1, ch, cdim), lambda t, p_ref: (j, t, 0))
    return pl.pallas_call(
        body, name=name,
        grid_spec=pltpu.PrefetchScalarGridSpec(
            num_scalar_prefetch=1, grid=(2,),
            in_specs=[pl.BlockSpec((1, ch, cdim), lambda t, p_ref: (p_ref[0], 2 * p_ref[1] + t, 0)),
                      pl.BlockSpec((1, ch, cdim), lambda t, p_ref: (p_ref[0], t, 0)),
                      other(0), other(1), other(2), ANY],
            out_specs=pl.BlockSpec((1, ch, cdim), lambda t, p_ref: (l, 2 * p_ref[1] + t, 0))),
        out_shape=jax.ShapeDtypeStruct(acc.shape, F32),
        input_output_aliases={6: 0},
        compiler_params=_cparams(("parallel",)),
    )(place, g, r1, r2, r2, r2, acc)


def rs_join_halves(grads, *, name):
    n = len(grads)

    def body(*refs):
        ins, outs = refs[:n], refs[n:2 * n]
        ssem, rsem = refs[2 * n:]
        x, y, c, _ = _place()
        cps = []
        for i in range(n):
            mine = _half(ins[i].shape[1], c)
            cp = _remote(ins[i].at[:, mine, :], outs[i].at[:, mine, :], ssem.at[i], rsem.at[i], (x, y, 1 - c))
            cp.start()
            cps.append(cp)
        for cp in cps:
            cp.wait()

    return pl.pallas_call(
        body, name=name, in_specs=[ANY] * n, out_specs=[ANY] * n,
        out_shape=[jax.ShapeDtypeStruct(g.shape, g.dtype) for g in grads],
        input_output_aliases={i: i for i in range(n)},
        scratch_shapes=[pltpu.SemaphoreType.DMA((n,)), pltpu.SemaphoreType.DMA((n,))],
        compiler_params=pltpu.CompilerParams(has_side_effects=True),
    )(*grads)


def all_reduce_small(v):
    R = v.shape[0]

    def body(v_ref, o_ref, buf, ssem, rsem):
        x, y, c, _ = _place()
        me = 4 * x + 2 * y + c
        buf[me] = v_ref[...]
        cps = []
        for k in range(1, 8):
            fx, fy, fc = (k >> 2) & 1, (k >> 1) & 1, k & 1
            px = jnp.where(fx == 1, 1 - x, x)
            py = jnp.where(fy == 1, 1 - y, y)
            pc = jnp.where(fc == 1, 1 - c, c)
            cp = _remote(v_ref, buf.at[me], ssem.at[k - 1], rsem.at[k - 1], (px, py, pc))
            cp.start()
            cps.append(cp)
        for cp in cps:
            cp.wait()
        acc = buf[0]
        for d in range(1, 8):
            acc = acc + buf[d]
        o_ref[...] = acc

    vm = pl.BlockSpec(memory_space=pltpu.VMEM)
    return pl.pallas_call(
        body, name="all_reduce_small", in_specs=[vm], out_specs=vm,
        out_shape=jax.ShapeDtypeStruct((R, LANES), F32),
        scratch_shapes=[pltpu.VMEM((8, R, LANES), F32), pltpu.SemaphoreType.DMA((7,)), pltpu.SemaphoreType.DMA((7,))],
        compiler_params=pltpu.CompilerParams(vmem_limit_bytes=VMEM_LIMIT, has_side_effects=True),
    )(v)


def _to_stacked(name, full):
    R, C = full.shape
    if name in ROW_SHARDED:
        return full.reshape(4, R // 4, C)
    return jnp.transpose(full.reshape(R, 4, C // 4), (1, 0, 2))


def _from_stacked(name, st):
    _, r, c = st.shape
    if name in ROW_SHARDED:
        return st.reshape(4 * r, c)
    return jnp.transpose(st, (1, 0, 2)).reshape(r, 4 * c)


UP_PIECES = ("ffn1_up", "ffn2_up")


def _layer_weights(lands):
    w = {n: lands[n] if n in UP_PIECES else _from_stacked(n, lands[n]) for n in lands}
    if "w_in" not in w:
        return w
    win = w.pop("w_in")
    D = win.shape[0]
    p0, p1, p2, p3 = POOL_DIM, POOL_DIM + Q_LORA_RANK, POOL_DIM + Q_LORA_RANK + KV_LORA_RANK, \
        POOL_DIM + Q_LORA_RANK + KV_LORA_RANK + QK_ROPE_DIM
    w["w_pool"] = win[:, :p0]
    w["w_lat"] = jnp.concatenate([win[:, p0:p3], jnp.zeros((D, LAT_DIM - (p3 - p0)), win.dtype)], axis=1)
    w["w_gate"] = win[:, p3:]
    uq = w["w_uq"].reshape(Q_LORA_RANK, N_HEADS, QK_DIM)
    w["w_uq"] = jnp.concatenate([uq, jnp.zeros((Q_LORA_RANK, N_HEADS, HEAD_PAD - QK_DIM), uq.dtype)],
                                axis=2).reshape(Q_LORA_RANK, N_HEADS * HEAD_PAD)
    return w


def _layer_grads_stacked(dw):
    dw = dict(dw)
    if "w_lat" in dw:
        lat = dw.pop("w_lat")
        dw["w_in"] = jnp.concatenate([dw.pop("w_pool"), lat[:, :Q_LORA_RANK + KV_LORA_RANK + QK_ROPE_DIM],
                                      dw.pop("w_gate")], axis=1)
        dw["w_uq"] = dw["w_uq"].reshape(Q_LORA_RANK, N_HEADS, HEAD_PAD)[:, :, :QK_DIM].reshape(Q_LORA_RANK,
                                                                                                 N_HEADS * QK_DIM)
    return {n: dw[n] if n in UP_PIECES else _to_stacked(n, dw[n]) for n in BIG if n in dw}


def _rope_tables(positions):
    inv_freq = ROPE_THETA ** (-jnp.arange(0, QK_ROPE_DIM, 2, dtype=F32) / QK_ROPE_DIM)
    ang = positions.astype(F32).reshape(-1)[:, None] * inv_freq
    cos, sin = jnp.cos(ang), jnp.sin(ang)
    z = jnp.zeros((ang.shape[0], LANES - QK_ROPE_DIM), F32)
    return jnp.concatenate([cos, cos, z], axis=1), jnp.concatenate([-sin, sin, z], axis=1)


def _pack_small(vals):
    parts = []
    for n in SMALL:
        f = vals[n].reshape(-1).astype(F32)
        pad = (-f.shape[0]) % (8 * LANES)
        parts.append(jnp.pad(f, (0, pad)))
    return jnp.concatenate(parts).reshape(-1, LANES)


def _unpack_small(packed, like):
    flat = packed.reshape(-1)
    out, off = {}, 0
    for n in SMALL:
        size = like[n].size
        out[n] = flat[off:off + size].reshape(like[n].shape)
        off += size + ((-size) % (8 * LANES))
    return out


def _ffn_fwd(x, g, wu4, wd, tag, dep=None):
    h = rms_fwd(x, g, dep=dep, name=f"{tag}_norm")
    gate, up, a = ffn_up_act(h, wu4, name=f"{tag}_up_act")
    y = mm(a, wd, res=x, alpha=0.5, name=f"{tag}_down")
    return y, (x, h, gate, up, a)


def _ffn_bwd(dy, saved, g, wu4, wd, tag, dep=None):
    x, h, gate, up, a = saved
    dgate, dup = ffn_down_dx_act(dy, wd, gate, up, dep=dep, name=f"{tag}_down_dx_act")
    dwd = mm(a, dy, ta=True, alpha=0.5, out_dtype=BF16, name=f"{tag}_down_dw")
    dwu4 = ffn_up_dw(h, dgate, dup, name=f"{tag}_up_dw")
    dx, dg = ffn_up_dx_norm(dgate, dup, wu4, x, g, dy, name=f"{tag}_up_dx_norm")
    return dx, dg, dwu4, dwd


def _mix_fwd(x, p, w, cs, sn, B, dep=None):
    h = rms_fwd(x, p["norm_mix"], dep=dep, name="mix_norm")
    lat, xp, gl = mix_in(h, w["w_lat"], w["w_pool"], w["w_gate"], name="mix_in")
    mixed = pool_fwd(xp, p["pool_maps"].astype(BF16), p["pool_scale"], B, name="pool_fwd")
    ba = mm(mixed, w["w_pool_proj"], name="mix_pool_proj")
    qn, kvn, kr = lat_fwd(lat, p["q_latent_norm"], p["kv_latent_norm"], cs, sn, name="lat_fwd")
    kv = mm(kvn, w["w_ukv"], out_dtype=BF16, name="mix_ukv")
    o, lse, q = attn_fwd(mm(qn, w["w_uq"], name="mix_uq"), kv, kr, cs, sn, B, name="attn_fwd")
    bb, merged = attn_proj_gate(o, w["w_attn_proj"], gl, p["b_gate"], ba, name="mix_attn_proj_gate")
    y = mm(merged, w["w_out"], res=x, name="mix_out")
    return y, (x, h, lat, xp, gl, mixed, ba, qn, kvn, kr, q, kv, o, lse, bb, merged)


def _mix_bwd(dy, saved, p, w, cs, sn, B):
    x, h, lat, xp, gl, mixed, ba, qn, kvn, kr, q, kv, o, lse, bb, merged = saved
    dw, ds = {}, {}
    dw["w_out"] = mm(merged, dy, ta=True, out_dtype=BF16, name="mix_out_dw")
    dba, dbb, dgl, ds["b_gate"] = out_dx_gate(dy, w["w_out"], gl, p["b_gate"], ba, bb, name="mix_out_dx_gate")
    dw["w_attn_proj"] = mm(o, dbb, ta=True, out_dtype=BF16, name="mix_attn_proj_dw")
    do = mm(dbb, w["w_attn_proj"], tb=True, out_dtype=BF16, name="mix_attn_proj_dx")
    dw["w_pool_proj"] = mm(mixed, dba, ta=True, out_dtype=BF16, name="mix_pool_proj_dw")
    dmixed = mm(dba, w["w_pool_proj"], tb=True, name="mix_pool_proj_dx")
    dxp, ds["pool_maps"], ds["pool_scale"] = pool_bwd(xp, dmixed, p["pool_maps"].astype(BF16), p["pool_scale"], B,
                                                      name="pool_bwd")
    dqb, dkv, dkr = attn_bwd(q, kv, kr, o, do, lse, cs, sn, B, name="attn_bwd")
    dw["w_ukv"] = mm(kvn, dkv, ta=True, out_dtype=BF16, name="mix_ukv_dw")
    dkvn = mm(dkv, w["w_ukv"], tb=True, name="mix_ukv_dx")
    dw["w_uq"] = mm(qn, dqb, ta=True, out_dtype=BF16, name="mix_uq_dw")
    dqn = mm(dqb, w["w_uq"], tb=True, name="mix_uq_dx")
    dlat, ds["q_latent_norm"], ds["kv_latent_norm"] = lat_bwd(lat, p["q_latent_norm"], p["kv_latent_norm"], dqn, dkvn,
                                                               dkr, cs, sn, name="lat_bwd")
    dw["w_lat"] = mm(h, dlat, ta=True, out_dtype=BF16, name="mix_lat_dw")
    dw["w_pool"] = mm(h, dxp, ta=True, out_dtype=BF16, name="mix_pool_in_dw")
    dw["w_gate"] = mm(h, dgl, ta=True, out_dtype=BF16, name="mix_gate_in_dw")
    dx, ds["norm_mix"] = mix_in_dx_norm(dlat, dxp, dgl, w["w_lat"], w["w_pool"], w["w_gate"], x, p["norm_mix"], dy,
                                        name="mix_in_dx_norm")
    return dx, dw, ds


def kernel(x, positions, norm_ffn1, ffn1_up, ffn1_down, norm_mix, w_in, b_gate, pool_maps, pool_scale, w_pool_proj, q_latent_norm, w_uq, kv_latent_norm, w_ukv, w_attn_proj, w_out, norm_ffn2, ffn2_up, ffn2_down, final_norm, loss_target, m_norm_ffn1, m_ffn1_up, m_ffn1_down, m_norm_mix, m_w_in, m_b_gate, m_pool_maps, m_pool_scale, m_w_pool_proj, m_q_latent_norm, m_w_uq, m_kv_latent_norm, m_w_ukv, m_w_attn_proj, m_w_out, m_norm_ffn2, m_ffn2_up, m_ffn2_down, m_final_norm, v_norm_ffn1, v_ffn1_up, v_ffn1_down, v_norm_mix, v_w_in, v_b_gate, v_pool_maps, v_pool_scale, v_w_pool_proj, v_q_latent_norm, v_w_uq, v_kv_latent_norm, v_w_ukv, v_w_attn_proj, v_w_out, v_norm_ffn2, v_ffn2_up, v_ffn2_down, v_final_norm):
    given = dict(locals())
    B, S, D = x.shape
    T = B * S
    L = norm_ffn1.shape[0]
    W = {n: given[n] for n in WEIGHTS}
    Mo = {n: given["m_" + n] for n in WEIGHTS}
    Vo = {n: given["v_" + n] for n in WEIGHTS}
    core = lax.axis_index("c").astype(jnp.int32)
    chip = (2 * lax.axis_index("x") + lax.axis_index("y")).astype(jnp.int32)

    core_arr = core.reshape(1)
    chip_arr = chip.reshape(1)
    place = jnp.stack([chip, core])
    first = ("ffn1_up", "ffn1_down")
    rest = tuple(n for n in BIG if n not in first)

    own = [{n: W[n][l].astype(BF16) for n in BIG} for l in range(L)]

    def ag_begin(l, names, tag, after=None):
        lands = [jax.ShapeDtypeStruct((4,) + own[l][n].shape, BF16) for n in names]
        return exchange_begin(f"ag_start_{tag}", [own[l][n] for n in names], lands, ag_plan, 3 * len(names), after)

    def ag_finish(handle, names, tag, after):
        mine, lands = exchange_end(f"ag_wait_{tag}", handle, ag_plan, after)
        lands = ag_forward(lands, name=f"ag_forward_{tag}")
        return _layer_weights(dict(zip(names, place_own(lands, mine, chip_arr, name=f"place_own_{tag}"))))

    h_first, t1 = ag_begin(0, first, "0a")
    cs, sn = _rope_tables(positions)
    xs = x.reshape(T, D) + t1[0, 0]
    saved, handle = [], None
    for l in range(L):
        p = {n: W[n][l] for n in SMALL if n != "final_norm"}
        dep = None
        if l == 0:
            w = ag_finish(h_first, first, "0a", xs)
            h_rest, dep = ag_begin(0, rest, "0b", after=w["ffn1_down"])
        else:
            w = ag_finish(handle, BIG, str(l), xs)
            if l + 1 < L:
                handle, dep = ag_begin(l + 1, BIG, str(l + 1), after=xs)
        xs, s1 = _ffn_fwd(xs, p["norm_ffn1"], w["ffn1_up"], w["ffn1_down"], "ffn1", dep=dep)
        dep = None
        if l == 0:
            w.update(ag_finish(h_rest, rest, "0b", xs))
            if L > 1:
                handle, dep = ag_begin(1, BIG, "1", after=xs)
        xs, s2 = _mix_fwd(xs, p, w, cs, sn, B, dep=dep)
        xs, s3 = _ffn_fwd(xs, p["norm_ffn2"], w["ffn2_up"], w["ffn2_down"], "ffn2")
        saved.append((w, p, s1, s2, s3))

    dx, dfinal, loss_tile = loss_head(xs, final_norm, loss_target.reshape(T, D), name="loss_head")
    loss = lax.psum(loss_tile[0, 0], ("x", "y", "c"))

    def rs_begin(dw, tag):
        stacked = _layer_grads_stacked(dw)
        names = tuple(stacked)
        parts = [stacked[n] for n in names]
        r1 = rs_swap_halves(parts, name=f"rs_swap_{tag}")
        sums = [rs_chip_sum(g, a, core_arr, name=f"rs_chip_sum_{n}") for n, g, a in zip(names, parts, r1)]
        lands = [jax.ShapeDtypeStruct((3,) + s.shape[1:], BF16) for s in sums]
        handle, token = exchange_begin(f"rs_start_{tag}", sums, lands, rs_plan, 3 * len(names))
        return (names, parts, r1, handle), token

    acc = {n: lax.empty(W[n].shape, F32) for n in BIG}

    def rs_finish(l, pending, tag, after):
        names, parts, r1, handle = pending
        _, r2 = exchange_end(f"rs_wait_{tag}", handle, rs_plan, after)
        for n, g, a, b in zip(names, parts, r1, r2):
            acc[n] = rs_final_sum(g, a, b, place, acc[n], l, name=f"rs_final_sum_{n}")

    small_layers, pending, dep = [], [], None
    for l in reversed(range(L)):
        w, p, s1, s2, s3 = saved[l]
        dx, dg2, dwu2, dwd2 = _ffn_bwd(dx, s3, p["norm_ffn2"], w["ffn2_up"], w["ffn2_down"], "ffn2", dep=dep)
        dx, dw, ds = _mix_bwd(dx, s2, p, w, cs, sn, B)
        dw.update(ffn2_up=dwu2, ffn2_down=dwd2)
        dep = None
        if l == 0:
            early, dep = rs_begin(dw, "0b")
            dw = {}
        dx, dg1, dwu1, dwd1 = _ffn_bwd(dx, s1, p["norm_ffn1"], w["ffn1_up"], w["ffn1_down"], "ffn1", dep=dep)
        dw.update(ffn1_up=dwu1, ffn1_down=dwd1)
        ds.update(norm_ffn1=dg1, norm_ffn2=dg2)
        small_layers.append(ds)
        last, dep = rs_begin(dw, "0a" if l == 0 else str(l))
        if l > 0:
            pending.append((l, last))
    small_layers.reverse()

    small = {n: jnp.stack([small_layers[l][n].reshape(W[n].shape[1:]) for l in range(L)]) for n in SMALL
             if n != "final_norm"}
    small["final_norm"] = dfinal.reshape(final_norm.shape)
    grads = _unpack_small(all_reduce_small(_pack_small(small) + dep[0, 0]), small)
    delta, new_m, new_v = {}, {}, {}
    d, nm, nv = adamw(_pack_small(W), _pack_small(grads), _pack_small(Mo), _pack_small(Vo), name="adamw_small")
    delta.update(_unpack_small(d, W))
    new_m.update(_unpack_small(nm, W))
    new_v.update(_unpack_small(nv, W))

    def update(names, tag, d):
        joined = rs_join_halves([acc[n] for n in names], name=f"rs_join_{tag}")
        for n, g in zip(names, joined):
            sh = W[n].shape
            two = lambda a: a.reshape(sh[0] * sh[1], sh[2])
            d, nm, nv = adamw(two(W[n]), two(g), two(Mo[n]), two(Vo[n]), dep=d, name=f"adamw_{n}")
            grads[n], delta[n], new_m[n], new_v[n] = g, d.reshape(sh), nm.reshape(sh), nv.reshape(sh)
        return d

    for l, item in pending:
        rs_finish(l, item, str(l), d)
    rs_finish(0, early, "0b", d)
    d = update(rest, "rest", d)
    rs_finish(0, last, "0a", d)
    update(first, "first", d)

    return (loss, dx.reshape(B, S, D), *[grads[n] for n in WEIGHTS], *[delta[n] for n in WEIGHTS],
            *[new_m[n] for n in WEIGHTS], *[new_v[n] for n in WEIGHTS])
```

```python
import functools

import jax
import jax.numpy as jnp
from jax import lax
from jax.experimental import pallas as pl
from jax.experimental.pallas import tpu as pltpu

F32 = jnp.float32
BF16 = jnp.bfloat16

N_HEADS = 8
QK_NOPE_DIM = 128
QK_ROPE_DIM = 64
QK_DIM = QK_NOPE_DIM + QK_ROPE_DIM
V_HEAD_DIM = 128
HEAD_PAD = 256
Q_LORA_RANK = 384
KV_LORA_RANK = 256
ROPE_THETA = 10000.0
POOL_WINDOWS = (2, 4, 8, 16)
N_POOL_GROUPS = 4
POOL_GROUP_DIM = 128
POOL_DIM = N_POOL_GROUPS * POOL_GROUP_DIM
LAT_DIM = 768
NORM_EPS = 1e-6
ADAM_LR = 0.001
ADAM_B1 = 0.9
ADAM_B2 = 0.999
ADAM_EPS = 1e-08
ADAM_WD = 0.01
ADAM_STEP = 10
NEG_INF = -1e30
LANES = 128
ATT_BLOCK = 512
VMEM_LIMIT = 48 * 1024 * 1024
MESH = pl.DeviceIdType.MESH
_NT = (((1,), (1,)), ((), ()))
_TN = (((0,), (0,)), ((), ()))

BIG = ("ffn1_up", "ffn1_down", "w_in", "w_pool_proj", "w_uq", "w_ukv", "w_attn_proj", "w_out",
       "ffn2_up", "ffn2_down")
ROW_SHARDED = ("ffn1_down", "w_attn_proj", "w_out", "ffn2_down")
SMALL = ("norm_ffn1", "norm_mix", "b_gate", "pool_maps", "pool_scale", "q_latent_norm",
         "kv_latent_norm", "norm_ffn2", "final_norm")
WEIGHTS = ("norm_ffn1", "ffn1_up", "ffn1_down", "norm_mix", "w_in", "b_gate", "pool_maps", "pool_scale",
           "w_pool_proj", "q_latent_norm", "w_uq", "kv_latent_norm", "w_ukv", "w_attn_proj", "w_out",
           "norm_ffn2", "ffn2_up", "ffn2_down", "final_norm")


def _pick(dim, cands):
    for c in cands:
        if c <= dim and dim % c == 0:
            return c
    return dim


def _cparams(sem=None, **kw):
    if sem is not None:
        kw["dimension_semantics"] = sem
    return pltpu.CompilerParams(vmem_limit_bytes=VMEM_LIMIT, **kw)


def _sigmoid(x):
    return 1.0 / (1.0 + jnp.exp(-x))


MM_TILE_BUDGET = 30 * 1024 * 1024
TILE_SIZES = (1408, 1024, 768, 512, 384, 256, 128)


V7X_MXU_FLOPS = 9.0e14
V7X_HBM_BYTES = 2.5e12
GRID_STEP_S = 0.35e-6


def _mm_tiles(M, N, K, sa, sb, so, sr):
    tks = [K] if K <= 2816 else [t for t in (2816, 2048, 1408, 1024, 512, 256, 128) if K % t == 0]
    best = None
    for tk in tks:
        for tm in [t for t in TILE_SIZES if M % t == 0] or [M]:
            for tn in [t for t in TILE_SIZES if N % t == 0] or [N]:
                need = 2 * (tm * tk * sa + tk * tn * sb + tm * tn * (so + sr)) + (tm * tn * 4 if tk < K else 0)
                if need > MM_TILE_BUDGET:
                    continue
                ni, nj, nk = M // tm, N // tn, K // tk
                a_bytes = M * K * sa * (nj if nk > 1 else 1)
                b_bytes = K * N * sb * (1 if nj == 1 and nk == 1 else ni)
                traffic = a_bytes + b_bytes + M * N * (so + sr) + (M * N * 8 * nk if nk > 1 else 0)
                t = max(2.0 * M * N * K / V7X_MXU_FLOPS, traffic / V7X_HBM_BYTES) + ni * nj * nk * GRID_STEP_S
                if best is None or t < best[0]:
                    best = (t, (tm, tn, tk))
    assert best is not None, (M, N, K)
    return best[1]


def mm(a, b, *, name, ta=False, tb=False, out_dtype=F32, res=None, alpha=1.0, dep=None):
    if ta:
        K, M = a.shape
    else:
        M, K = a.shape
    if tb:
        N, K2 = b.shape
    else:
        K2, N = b.shape
    assert K == K2, (a.shape, b.shape, ta, tb)
    tm, tn, tk = _mm_tiles(M, N, K, a.dtype.itemsize, b.dtype.itemsize, jnp.dtype(out_dtype).itemsize,
                           0 if res is None else res.dtype.itemsize)
    nk = K // tk
    dims = (((0 if ta else 1,), (1 if tb else 0,)), ((), ()))

    def body(*refs):
        a_ref, b_ref = refs[:2]
        r_ref = refs[2] if res is not None else None
        o_ref = refs[-2] if nk > 1 else refs[-1]

        def finish(r):
            if alpha != 1.0:
                r = r * alpha
            if res is not None:
                r = r_ref[...].astype(F32) + r
            o_ref[...] = r.astype(out_dtype)

        part = lax.dot_general(a_ref[...].astype(BF16), b_ref[...].astype(BF16), dims, preferred_element_type=F32)
        if nk == 1:
            finish(part)
            return
        acc = refs[-1]
        k = pl.program_id(2)

        @pl.when(k == 0)
        def _():
            acc[...] = part

        @pl.when(k > 0)
        def _():
            acc[...] += part

        @pl.when(k == nk - 1)
        def _():
            finish(acc[...])

    a_spec = pl.BlockSpec((tk, tm), lambda i, j, k: (k, i)) if ta else pl.BlockSpec((tm, tk), lambda i, j, k: (i, k))
    b_spec = pl.BlockSpec((tn, tk), lambda i, j, k: (j, k)) if tb else pl.BlockSpec((tk, tn), lambda i, j, k: (k, j))
    o_spec = pl.BlockSpec((tm, tn), lambda i, j, k: (i, j))
    in_specs = [a_spec, b_spec]
    args = [a, b]
    if res is not None:
        in_specs.append(o_spec)
        args.append(res)
    if dep is not None:
        in_specs.append(pl.BlockSpec((8, LANES), lambda i, j, k: (0, 0)))
        args.append(dep)
    return pl.pallas_call(
        body, name=name, grid=(M // tm, N // tn, nk), in_specs=in_specs, out_specs=o_spec,
        out_shape=jax.ShapeDtypeStruct((M, N), out_dtype),
        scratch_shapes=[pltpu.VMEM((tm, tn), F32)] if nk > 1 else [],
        compiler_params=_cparams(("parallel", "parallel", "arbitrary")),
    )(*args)


MXU_COLS = 256


def _col_chunks(n):
    return [(lo, min(lo + MXU_COLS, n)) for lo in range(0, n, MXU_COLS)]


def ffn_up_act(h, wu4, *, name):
    T, D = h.shape
    cq = wu4.shape[2]
    Fh = 2 * cq
    tm = _pick(T, (512, 256, 128))

    def body(h_ref, wg_ref, wu_ref, g_ref, u_ref, a_ref):
        hv = h_ref[...]
        for lo, hi in _col_chunks(cq):
            gv = jnp.dot(hv, wg_ref[0, :, lo:hi], preferred_element_type=F32)
            uv = jnp.dot(hv, wu_ref[0, :, lo:hi], preferred_element_type=F32)
            g_ref[:, lo:hi] = gv.astype(BF16)
            u_ref[:, lo:hi] = uv.astype(BF16)
            a_ref[:, lo:hi] = (gv * _sigmoid(gv) * uv).astype(BF16)

    tile = pl.BlockSpec((tm, cq), lambda j, i: (i, j))
    sh = jax.ShapeDtypeStruct((T, Fh), BF16)
    return pl.pallas_call(
        body, name=name, grid=(2, T // tm),
        in_specs=[pl.BlockSpec((tm, D), lambda j, i: (i, 0)), pl.BlockSpec((1, D, cq), lambda j, i: (j, 0, 0)),
                  pl.BlockSpec((1, D, cq), lambda j, i: (2 + j, 0, 0))],
        out_specs=[tile, tile, tile], out_shape=[sh, sh, sh],
        compiler_params=_cparams(("parallel", "parallel")),
    )(h, wu4, wu4)


def ffn_down_dx_act(dy, wd, g, u, *, dep=None, name):
    T, D = dy.shape
    Fh = wd.shape[0]
    cq = Fh // 2
    tm = _pick(T, (512, 256, 128))

    def body(dy_ref, wd_ref, g_ref, u_ref, *rest):
        dg_ref, du_ref = rest[-2:]
        dyv = dy_ref[...].astype(BF16)
        for lo, hi in _col_chunks(cq):
            da = 0.5 * lax.dot_general(dyv, wd_ref[lo:hi, :], _NT, preferred_element_type=F32)
            gv = g_ref[:, lo:hi].astype(F32)
            uv = u_ref[:, lo:hi].astype(F32)
            s = _sigmoid(gv)
            dg_ref[:, lo:hi] = (da * uv * (s * (1.0 + gv * (1.0 - s)))).astype(BF16)
            du_ref[:, lo:hi] = (da * (gv * s)).astype(BF16)

    tile = pl.BlockSpec((tm, cq), lambda j, i: (i, j))
    sh = jax.ShapeDtypeStruct((T, Fh), BF16)
    in_specs = [pl.BlockSpec((tm, D), lambda j, i: (i, 0)), pl.BlockSpec((cq, D), lambda j, i: (j, 0)), tile, tile]
    args = [dy, wd, g, u]
    if dep is not None:
        in_specs.append(pl.BlockSpec((8, LANES), lambda j, i: (0, 0)))
        args.append(dep)
    return pl.pallas_call(
        body, name=name, grid=(2, T // tm), in_specs=in_specs, out_specs=[tile, tile], out_shape=[sh, sh],
        compiler_params=_cparams(("parallel", "parallel")),
    )(*args)


def ffn_up_dw(h, dg, du, *, name):
    T, D = h.shape
    cq = dg.shape[1] // 2
    tk = _pick(T, (1024, 512, 256, 128))
    nk = T // tk

    def body(h_ref, dg_ref, du_ref, o_ref, acc):
        p = pl.program_id(0)
        k = pl.program_id(1)

        @pl.when(k == 0)
        def _():
            acc[...] = jnp.zeros_like(acc)

        @pl.when(p < 2)
        def _():
            acc[...] += lax.dot_general(h_ref[...], dg_ref[...], _TN, preferred_element_type=F32)

        @pl.when(p >= 2)
        def _():
            acc[...] += lax.dot_general(h_ref[...], du_ref[...], _TN, preferred_element_type=F32)

        @pl.when(k == nk - 1)
        def _():
            o_ref[0] = acc[...].astype(BF16)

    return pl.pallas_call(
        body, name=name, grid=(4, nk),
        in_specs=[pl.BlockSpec((tk, D), lambda p, k: (k, 0)),
                  pl.BlockSpec((tk, cq), lambda p, k: (jnp.where(p < 2, k, 0), jnp.minimum(p, 1))),
                  pl.BlockSpec((tk, cq), lambda p, k: (jnp.where(p < 2, 0, k), jnp.maximum(p - 2, 0)))],
        out_specs=pl.BlockSpec((1, D, cq), lambda p, k: (p, 0, 0)),
        out_shape=jax.ShapeDtypeStruct((4, D, cq), BF16),
        scratch_shapes=[pltpu.VMEM((D, cq), F32)],
        compiler_params=_cparams(("parallel", "arbitrary")),
    )(h, dg, du)


def ffn_up_dx_norm(dg, du, wu4, x, g, dy, *, name):
    T = dg.shape[0]
    _, D, cq = wu4.shape
    tm = _pick(T, (512, 256, 128))

    def body(dg_ref, du_ref, wg_ref, wu_ref, x_ref, g_ref, dy_ref, dx_ref, dgain_ref, acc):
        i = pl.program_id(0)
        k = pl.program_id(1)
        part = lax.dot_general(dg_ref[...], wg_ref[0], _NT, preferred_element_type=F32)
        part = part + lax.dot_general(du_ref[...], wu_ref[0], _NT, preferred_element_type=F32)

        @pl.when(jnp.logical_and(i == 0, k == 0))
        def _():
            dgain_ref[...] = jnp.zeros_like(dgain_ref)

        @pl.when(k == 0)
        def _():
            acc[...] = part

        @pl.when(k == 1)
        def _():
            dx, dgain = _rms_bwd_math(x_ref[...], g_ref[...], acc[...] + part)
            dx_ref[...] = dy_ref[...] + dx
            dgain_ref[...] += dgain

    tile = pl.BlockSpec((tm, cq), lambda i, k: (i, k))
    row = pl.BlockSpec((tm, D), lambda i, k: (i, 0))
    vec = pl.BlockSpec((1, D), lambda i, k: (0, 0))
    return pl.pallas_call(
        body, name=name, grid=(T // tm, 2),
        in_specs=[tile, tile, pl.BlockSpec((1, D, cq), lambda i, k: (k, 0, 0)),
                  pl.BlockSpec((1, D, cq), lambda i, k: (2 + k, 0, 0)), row, vec, row],
        out_specs=[row, vec],
        out_shape=[jax.ShapeDtypeStruct((T, D), F32), jax.ShapeDtypeStruct((1, D), F32)],
        scratch_shapes=[pltpu.VMEM((tm, D), F32)],
        compiler_params=_cparams(("arbitrary", "arbitrary")),
    )(dg, du, wu4, wu4, x, g.reshape(1, D), dy)


def mix_in(h, w_lat, w_pool, w_gate, *, name):
    T, D = h.shape
    tm = _pick(T, (512, 256, 128))

    def body(h_ref, wl_ref, wp_ref, wg_ref, lat_ref, xp_ref, gl_ref):
        hv = h_ref[...]
        lat_ref[...] = jnp.dot(hv, wl_ref[...], preferred_element_type=F32)
        xp_ref[...] = jnp.dot(hv, wp_ref[...], preferred_element_type=F32)
        gl_ref[...] = jnp.dot(hv, wg_ref[...], preferred_element_type=F32).astype(BF16)

    whole = lambda a: pl.BlockSpec(a.shape, lambda i: (0, 0))
    out = lambda a: pl.BlockSpec((tm, a.shape[1]), lambda i: (i, 0))
    return pl.pallas_call(
        body, name=name, grid=(T // tm,),
        in_specs=[pl.BlockSpec((tm, D), lambda i: (i, 0)), whole(w_lat), whole(w_pool), whole(w_gate)],
        out_specs=[out(w_lat), out(w_pool), out(w_gate)],
        out_shape=[jax.ShapeDtypeStruct((T, w_lat.shape[1]), F32), jax.ShapeDtypeStruct((T, w_pool.shape[1]), F32),
                   jax.ShapeDtypeStruct((T, w_gate.shape[1]), BF16)],
        compiler_params=_cparams(("parallel",)),
    )(h, w_lat, w_pool, w_gate)


def mix_in_dx_norm(dlat, dxp, dgl, w_lat, w_pool, w_gate, x, g, dy, *, name):
    T, D = x.shape
    tm = _pick(T, (512, 256, 128))

    def body(dlat_ref, dxp_ref, dgl_ref, wl_ref, wp_ref, wg_ref, x_ref, g_ref, dy_ref, dx_ref, dgain_ref):
        @pl.when(pl.program_id(0) == 0)
        def _():
            dgain_ref[...] = jnp.zeros_like(dgain_ref)

        dh = lax.dot_general(dlat_ref[...], wl_ref[...], _NT, preferred_element_type=F32)
        dh = dh + lax.dot_general(dxp_ref[...], wp_ref[...], _NT, preferred_element_type=F32)
        dh = dh + lax.dot_general(dgl_ref[...], wg_ref[...], _NT, preferred_element_type=F32)
        dx, dgain = _rms_bwd_math(x_ref[...], g_ref[...], dh)
        dx_ref[...] = dy_ref[...] + dx
        dgain_ref[...] += dgain

    row = lambda a: pl.BlockSpec((tm, a.shape[1]), lambda i: (i, 0))
    whole = lambda a: pl.BlockSpec(a.shape, lambda i: (0, 0))
    vec = pl.BlockSpec((1, D), lambda i: (0, 0))
    return pl.pallas_call(
        body, name=name, grid=(T // tm,),
        in_specs=[row(dlat), row(dxp), row(dgl), whole(w_lat), whole(w_pool), whole(w_gate), row(x), vec, row(dy)],
        out_specs=[row(x), vec],
        out_shape=[jax.ShapeDtypeStruct((T, D), F32), jax.ShapeDtypeStruct((1, D), F32)],
        compiler_params=_cparams(("arbitrary",)),
    )(dlat, dxp, dgl, w_lat, w_pool, w_gate, x, g.reshape(1, D), dy)


def _rows(T, width_bytes):
    cap = max(8, (2 * 1024 * 1024) // width_bytes)
    return _pick(T, tuple(c for c in (1024, 512, 256, 128, 64, 32, 16) if c <= cap))


def rms_fwd(x, g, *, name, dep=None):
    T, D = x.shape
    tm = _rows(T, D * 4)

    def body(x_ref, g_ref, *rest):
        xv = x_ref[...]
        r = lax.rsqrt(jnp.mean(xv * xv, axis=-1, keepdims=True) + NORM_EPS)
        rest[-1][...] = (xv * r * g_ref[...]).astype(BF16)

    in_specs = [pl.BlockSpec((tm, D), lambda i: (i, 0)), pl.BlockSpec((1, D), lambda i: (0, 0))]
    args = [x, g.reshape(1, D)]
    if dep is not None:
        in_specs.append(pl.BlockSpec((8, LANES), lambda i: (0, 0)))
        args.append(dep)
    return pl.pallas_call(
        body, name=name, grid=(T // tm,), in_specs=in_specs,
        out_specs=pl.BlockSpec((tm, D), lambda i: (i, 0)),
        out_shape=jax.ShapeDtypeStruct((T, D), BF16),
        compiler_params=_cparams(("parallel",)),
    )(*args)


def _rms_bwd_math(xv, gv, dh):
    r = lax.rsqrt(jnp.mean(xv * xv, axis=-1, keepdims=True) + NORM_EPS)
    xn = xv * r
    dg = jnp.sum(dh * xn, axis=0, keepdims=True)
    dxn = dh * gv
    dx = r * (dxn - xn * jnp.mean(dxn * xn, axis=-1, keepdims=True))
    return dx, dg


def _rope(xv, cv, sv):
    half = QK_ROPE_DIM // 2
    lane = lax.broadcasted_iota(jnp.int32, xv.shape, 1)
    swapped = jnp.where(lane < half, pltpu.roll(xv, LANES - half, 1), pltpu.roll(xv, half, 1))
    return xv * cv + swapped * sv


def _rope_t(dy, cv, sv):
    half = QK_ROPE_DIM // 2
    ds = dy * sv
    lane = lax.broadcasted_iota(jnp.int32, dy.shape, 1)
    swapped = jnp.where(lane < half, pltpu.roll(ds, LANES - half, 1), pltpu.roll(ds, half, 1))
    return dy * cv + swapped


def lat_fwd(lat, qn_w, kvn_w, cs, sn, *, name):
    T = lat.shape[0]
    tm = _rows(T, LAT_DIM * 4)
    kv0 = Q_LORA_RANK
    kr0 = Q_LORA_RANK + KV_LORA_RANK

    def body(lat_ref, qw_ref, kw_ref, c_ref, s_ref, qn_ref, kvn_ref, kr_ref):
        ql = lat_ref[:, :kv0]
        r = lax.rsqrt(jnp.mean(ql * ql, axis=-1, keepdims=True) + NORM_EPS)
        qn_ref[...] = (ql * r * qw_ref[...]).astype(BF16)
        kl = lat_ref[:, kv0:kr0]
        r = lax.rsqrt(jnp.mean(kl * kl, axis=-1, keepdims=True) + NORM_EPS)
        kvn_ref[...] = (kl * r * kw_ref[...]).astype(BF16)
        kr_ref[...] = _rope(lat_ref[:, kr0:], c_ref[...], s_ref[...]).astype(BF16)

    return pl.pallas_call(
        body, name=name, grid=(T // tm,),
        in_specs=[pl.BlockSpec((tm, LAT_DIM), lambda i: (i, 0)),
                  pl.BlockSpec((1, Q_LORA_RANK), lambda i: (0, 0)),
                  pl.BlockSpec((1, KV_LORA_RANK), lambda i: (0, 0)),
                  pl.BlockSpec((tm, LANES), lambda i: (i, 0)), pl.BlockSpec((tm, LANES), lambda i: (i, 0))],
        out_specs=[pl.BlockSpec((tm, Q_LORA_RANK), lambda i: (i, 0)),
                   pl.BlockSpec((tm, KV_LORA_RANK), lambda i: (i, 0)),
                   pl.BlockSpec((tm, LANES), lambda i: (i, 0))],
        out_shape=[jax.ShapeDtypeStruct((T, Q_LORA_RANK), BF16), jax.ShapeDtypeStruct((T, KV_LORA_RANK), BF16),
                   jax.ShapeDtypeStruct((T, LANES), BF16)],
        compiler_params=_cparams(("parallel",)),
    )(lat, qn_w.reshape(1, -1), kvn_w.reshape(1, -1), cs, sn)


def lat_bwd(lat, qn_w, kvn_w, dqn, dkvn, dkr, cs, sn, *, name):
    T = lat.shape[0]
    tm = _rows(T, LAT_DIM * 4)
    kv0 = Q_LORA_RANK
    kr0 = Q_LORA_RANK + KV_LORA_RANK

    def body(lat_ref, qw_ref, kw_ref, dqn_ref, dkvn_ref, dkr_ref, c_ref, s_ref, dlat_ref, dqw_ref, dkw_ref):
        @pl.when(pl.program_id(0) == 0)
        def _():
            dqw_ref[...] = jnp.zeros_like(dqw_ref)
            dkw_ref[...] = jnp.zeros_like(dkw_ref)

        dx, dg = _rms_bwd_math(lat_ref[:, :kv0], qw_ref[...], dqn_ref[...])
        dlat_ref[:, :kv0] = dx.astype(BF16)
        dqw_ref[...] += dg
        dx, dg = _rms_bwd_math(lat_ref[:, kv0:kr0], kw_ref[...], dkvn_ref[...])
        dlat_ref[:, kv0:kr0] = dx.astype(BF16)
        dkw_ref[...] += dg
        dlat_ref[:, kr0:] = _rope_t(dkr_ref[...], c_ref[...], s_ref[...]).astype(BF16)

    row = lambda w: pl.BlockSpec((tm, w), lambda i: (i, 0))
    vec = lambda w: pl.BlockSpec((1, w), lambda i: (0, 0))
    return pl.pallas_call(
        body, name=name, grid=(T // tm,),
        in_specs=[row(LAT_DIM), vec(Q_LORA_RANK), vec(KV_LORA_RANK), row(Q_LORA_RANK), row(KV_LORA_RANK),
                  row(LANES), row(LANES), row(LANES)],
        out_specs=[row(LAT_DIM), vec(Q_LORA_RANK), vec(KV_LORA_RANK)],
        out_shape=[jax.ShapeDtypeStruct((T, LAT_DIM), BF16), jax.ShapeDtypeStruct((1, Q_LORA_RANK), F32),
                   jax.ShapeDtypeStruct((1, KV_LORA_RANK), F32)],
        compiler_params=_cparams(("arbitrary",)),
    )(lat, qn_w.reshape(1, -1), kvn_w.reshape(1, -1), dqn, dkvn, dkr, cs, sn)


def attn_proj_gate(o, wap, gl, bg, ba, *, name):
    T, D2 = gl.shape
    D = D2 // 2
    tm = _pick(T, (512, 256, 128))

    def body(o_ref, w_ref, gl_ref, bg_ref, ba_ref, bb_ref, m_ref):
        bb = jnp.dot(o_ref[...], w_ref[...], preferred_element_type=F32)
        bb_ref[...] = bb.astype(BF16)
        ga = _sigmoid(gl_ref[:, :D].astype(F32) + bg_ref[:, :D])
        gb = _sigmoid(gl_ref[:, D:].astype(F32) + bg_ref[:, D:])
        m_ref[...] = (ga * ba_ref[...].astype(F32) + gb * bb).astype(BF16)

    row = lambda w: pl.BlockSpec((tm, w), lambda i: (i, 0))
    return pl.pallas_call(
        body, name=name, grid=(T // tm,),
        in_specs=[row(o.shape[1]), pl.BlockSpec(wap.shape, lambda i: (0, 0)), row(D2),
                  pl.BlockSpec((1, D2), lambda i: (0, 0)), row(D)],
        out_specs=[row(D), row(D)],
        out_shape=[jax.ShapeDtypeStruct((T, D), BF16), jax.ShapeDtypeStruct((T, D), BF16)],
        compiler_params=_cparams(("parallel",)),
    )(o, wap, gl, bg.reshape(1, D2), ba)


def out_dx_gate(dy, wo, gl, bg, ba, bb, *, name):
    T, D2 = gl.shape
    D = D2 // 2
    tm = _pick(T, (512, 256, 128))

    def body(dy_ref, w_ref, gl_ref, bg_ref, ba_ref, bb_ref, dba_ref, dbb_ref, dgl_ref, dbg_ref):
        @pl.when(pl.program_id(0) == 0)
        def _():
            dbg_ref[...] = jnp.zeros_like(dbg_ref)

        dmv = lax.dot_general(dy_ref[...].astype(BF16), w_ref[...], _NT, preferred_element_type=F32)
        ga = _sigmoid(gl_ref[:, :D].astype(F32) + bg_ref[:, :D])
        gb = _sigmoid(gl_ref[:, D:].astype(F32) + bg_ref[:, D:])
        dba_ref[...] = (dmv * ga).astype(BF16)
        dbb_ref[...] = (dmv * gb).astype(BF16)
        dla = dmv * ba_ref[...].astype(F32) * ga * (1.0 - ga)
        dlb = dmv * bb_ref[...].astype(F32) * gb * (1.0 - gb)
        dgl_ref[:, :D] = dla.astype(BF16)
        dgl_ref[:, D:] = dlb.astype(BF16)
        dbg_ref[:, :D] += jnp.sum(dla, axis=0, keepdims=True)
        dbg_ref[:, D:] += jnp.sum(dlb, axis=0, keepdims=True)

    row = lambda w: pl.BlockSpec((tm, w), lambda i: (i, 0))
    vec = pl.BlockSpec((1, D2), lambda i: (0, 0))
    return pl.pallas_call(
        body, name=name, grid=(T // tm,),
        in_specs=[row(D), pl.BlockSpec(wo.shape, lambda i: (0, 0)), row(D2), vec, row(D), row(D)],
        out_specs=[row(D), row(D), row(D2), vec],
        out_shape=[jax.ShapeDtypeStruct((T, D), BF16), jax.ShapeDtypeStruct((T, D), BF16),
                   jax.ShapeDtypeStruct((T, D2), BF16), jax.ShapeDtypeStruct((1, D2), F32)],
        compiler_params=_cparams(("arbitrary",)),
    )(dy, wo, gl, bg.reshape(1, D2), ba, bb)


def loss_head(x, gf, tgt, *, name):
    T, D = x.shape
    tm = _rows(T, D * 4)

    def body(x_ref, g_ref, t_ref, dx_ref, dg_ref, loss_ref):
        @pl.when(pl.program_id(0) == 0)
        def _():
            dg_ref[...] = jnp.zeros_like(dg_ref)
            loss_ref[...] = jnp.zeros_like(loss_ref)

        xv = x_ref[...]
        gv = g_ref[...]
        r = lax.rsqrt(jnp.mean(xv * xv, axis=-1, keepdims=True) + NORM_EPS)
        xn = xv * r
        err = xn * gv - t_ref[...]
        loss_ref[...] += 0.5 * jnp.sum(jnp.mean(err * err, axis=-1, keepdims=True))
        dy = err * (1.0 / D)
        dg_ref[...] += jnp.sum(dy * xn, axis=0, keepdims=True)
        dxn = dy * gv
        dx_ref[...] = r * (dxn - xn * jnp.mean(dxn * xn, axis=-1, keepdims=True))

    row = pl.BlockSpec((tm, D), lambda i: (i, 0))
    vec = pl.BlockSpec((1, D), lambda i: (0, 0))
    return pl.pallas_call(
        body, name=name, grid=(T // tm,), in_specs=[row, vec, row],
        out_specs=[row, vec, pl.BlockSpec((8, LANES), lambda i: (0, 0))],
        out_shape=[jax.ShapeDtypeStruct((T, D), F32), jax.ShapeDtypeStruct((1, D), F32),
                   jax.ShapeDtypeStruct((8, LANES), F32)],
        compiler_params=_cparams(("arbitrary",)),
    )(x, gf.reshape(1, D), tgt)


def _shift_rows(s, k, down):
    n = s.shape[0]
    t = lax.broadcasted_iota(jnp.int32, s.shape, 0)
    if down:
        return jnp.where(t >= k, pltpu.roll(s, k, 0), 0.0)
    return jnp.where(t < n - k, pltpu.roll(s, n - k, 0), 0.0)


def _window_sum(s, w, down):
    k = 1
    while k < w:
        s = s + _shift_rows(s, k, down)
        k *= 2
    return s


def _pool_count(shape, w):
    t = lax.broadcasted_iota(jnp.int32, shape, 0)
    return jnp.minimum(t + 1, w).astype(F32)


def pool_fwd(xp, maps, scale, B, *, name):
    T, P = xp.shape
    S = T // B
    G = POOL_GROUP_DIM

    def body(x_ref, m_ref, sc_ref, o_ref):
        for g, w in enumerate(POOL_WINDOWS):
            xg = x_ref[:, g * G:(g + 1) * G]
            pooled = _window_sum(xg, w, True) / _pool_count(xg.shape, w) - xg
            mixed = jnp.dot(pooled.astype(BF16), m_ref[g], preferred_element_type=F32)
            o_ref[:, g * G:(g + 1) * G] = (mixed * sc_ref[:, g * G:(g + 1) * G]).astype(BF16)

    return pl.pallas_call(
        body, name=name, grid=(B,),
        in_specs=[pl.BlockSpec((S, P), lambda b: (b, 0)), pl.BlockSpec((N_POOL_GROUPS, G, G), lambda b: (0, 0, 0)),
                  pl.BlockSpec((1, P), lambda b: (0, 0))],
        out_specs=pl.BlockSpec((S, P), lambda b: (b, 0)),
        out_shape=jax.ShapeDtypeStruct((T, P), BF16),
        compiler_params=_cparams(("parallel",)),
    )(xp, maps, scale.reshape(1, P))


def pool_bwd(xp, dmixed, maps, scale, B, *, name):
    T, P = xp.shape
    S = T // B
    G = POOL_GROUP_DIM

    def body(x_ref, dm_ref, m_ref, sc_ref, dx_ref, dmaps_ref, dsc_ref):
        @pl.when(pl.program_id(0) == 0)
        def _():
            dmaps_ref[...] = jnp.zeros_like(dmaps_ref)
            dsc_ref[...] = jnp.zeros_like(dsc_ref)

        for g, w in enumerate(POOL_WINDOWS):
            cols = slice(g * G, (g + 1) * G)
            xg = x_ref[:, cols]
            cnt = _pool_count(xg.shape, w)
            pooled = (_window_sum(xg, w, True) / cnt - xg).astype(BF16)
            mixed = jnp.dot(pooled, m_ref[g], preferred_element_type=F32)
            dmx = dm_ref[:, cols]
            dsc_ref[:, cols] += jnp.sum(dmx * mixed, axis=0, keepdims=True)
            dmp = (dmx * sc_ref[:, cols]).astype(BF16)
            dmaps_ref[g] += lax.dot_general(pooled, dmp, (((0,), (0,)), ((), ())), preferred_element_type=F32)
            dpooled = lax.dot_general(dmp, m_ref[g], (((1,), (1,)), ((), ())), preferred_element_type=F32)
            dx_ref[:, cols] = (_window_sum(dpooled / cnt, w, False) - dpooled).astype(BF16)

    blk = pl.BlockSpec((S, P), lambda b: (b, 0))
    mp = pl.BlockSpec((N_POOL_GROUPS, G, G), lambda b: (0, 0, 0))
    vec = pl.BlockSpec((1, P), lambda b: (0, 0))
    return pl.pallas_call(
        body, name=name, grid=(B,), in_specs=[blk, blk, mp, vec], out_specs=[blk, mp, vec],
        out_shape=[jax.ShapeDtypeStruct((T, P), BF16), jax.ShapeDtypeStruct((N_POOL_GROUPS, G, G), F32),
                   jax.ShapeDtypeStruct((1, P), F32)],
        compiler_params=_cparams(("arbitrary",)),
    )(xp, dmixed, maps, scale.reshape(1, P))


def _keys(kv_ref, kr_ref, rows):
    return jnp.concatenate([kv_ref[rows, :QK_NOPE_DIM], kr_ref[rows, :]], axis=1)


def _causal(s):
    row = lax.broadcasted_iota(jnp.int32, s.shape, 0)
    col = lax.broadcasted_iota(jnp.int32, s.shape, 1)
    return jnp.where(row >= col, s, NEG_INF)


def attn_fwd(q, kv, kr, cs, sn, B, *, name):
    T = q.shape[0]
    S = T // B
    blk = min(ATT_BLOCK, S)
    nb = S // blk
    H = N_HEADS
    scale = QK_DIM ** -0.5

    def body(q_ref, kv_ref, kr_ref, c_ref, s_ref, o_ref, lse_ref, qs_ref):
        qs_ref[:, :QK_NOPE_DIM] = (q_ref[:, :QK_NOPE_DIM] * scale).astype(BF16)
        qs_ref[:, QK_NOPE_DIM:] = _rope(q_ref[:, QK_NOPE_DIM:], c_ref[...] * scale, s_ref[...] * scale).astype(BF16)
        for qi in range(nb):
            rows = slice(qi * blk, (qi + 1) * blk)
            qb = qs_ref[rows, :]
            sd = _causal(lax.dot_general(qb, _keys(kv_ref, kr_ref, rows), _NT, preferred_element_type=F32))
            m = jnp.max(sd, axis=-1, keepdims=True)
            if qi > 0:
                prev = slice(0, qi * blk)
                sp = lax.dot_general(qb, _keys(kv_ref, kr_ref, prev), _NT, preferred_element_type=F32)
                m = jnp.maximum(m, jnp.max(sp, axis=-1, keepdims=True))
            pd = jnp.exp(sd - m)
            l = jnp.sum(pd, axis=-1, keepdims=True)
            acc = jnp.dot(pd.astype(BF16), kv_ref[rows, QK_NOPE_DIM:], preferred_element_type=F32)
            if qi > 0:
                pp = jnp.exp(sp - m)
                l = l + jnp.sum(pp, axis=-1, keepdims=True)
                acc = acc + jnp.dot(pp.astype(BF16), kv_ref[prev, QK_NOPE_DIM:], preferred_element_type=F32)
            o_ref[rows, :] = (acc / l).astype(BF16)
            lse_ref[0, rows, :] = m + jnp.log(l)

    head = pl.BlockSpec((S, HEAD_PAD), lambda b, h: (b, h))
    shared = pl.BlockSpec((S, LANES), lambda b, h: (b, 0))
    return pl.pallas_call(
        body, name=name, grid=(B, H),
        in_specs=[head, head, shared, shared, shared],
        out_specs=[pl.BlockSpec((S, V_HEAD_DIM), lambda b, h: (b, h)), pl.BlockSpec((1, S, 1), lambda b, h: (h, b, 0)),
                   head],
        out_shape=[jax.ShapeDtypeStruct((T, H * V_HEAD_DIM), BF16), jax.ShapeDtypeStruct((H, T, 1), F32),
                   jax.ShapeDtypeStruct((T, H * HEAD_PAD), BF16)],
        compiler_params=_cparams(("parallel", "parallel")),
    )(q, kv, kr, cs, sn)


def attn_bwd(q, kv, kr, o, do, lse, cs, sn, B, *, name):
    T = q.shape[0]
    S = T // B
    blk = min(ATT_BLOCK, S)
    nb = S // blk
    H = N_HEADS
    scale = QK_DIM ** -0.5

    def body(q_ref, kv_ref, kr_ref, o_ref, do_ref, lse_ref, c_ref, s_ref, dq_ref, dkv_ref, dkr_ref, dk_s, dv_s):
        dk_s[...] = jnp.zeros_like(dk_s)
        dv_s[...] = jnp.zeros_like(dv_s)

        @pl.when(pl.program_id(1) == 0)
        def _():
            dkr_ref[...] = jnp.zeros_like(dkr_ref)

        for qi in range(nb):
            rows = slice(qi * blk, (qi + 1) * blk)
            qb = q_ref[rows, :]
            dob = do_ref[rows, :]
            delta = jnp.sum(dob.astype(F32) * o_ref[rows, :].astype(F32), axis=-1, keepdims=True)
            lse_b = lse_ref[0, rows, :]

            def part(ks, diagonal):
                k = _keys(kv_ref, kr_ref, ks)
                s = lax.dot_general(qb, k, _NT, preferred_element_type=F32)
                if diagonal:
                    s = _causal(s)
                p = jnp.exp(s - lse_b)
                dp = lax.dot_general(dob, kv_ref[ks, QK_NOPE_DIM:], _NT, preferred_element_type=F32)
                ds = (p * (dp - delta)).astype(BF16)
                dv_s[ks, :] += lax.dot_general(p.astype(BF16), dob, _TN, preferred_element_type=F32)
                dk_s[ks, :] += lax.dot_general(ds, qb, _TN, preferred_element_type=F32)
                return jnp.dot(ds, k, preferred_element_type=F32)

            dq = part(rows, True)
            if qi > 0:
                dq = dq + part(slice(0, qi * blk), False)
            dq_ref[rows, :QK_NOPE_DIM] = (dq[:, :QK_NOPE_DIM] * scale).astype(BF16)
            dq_ref[rows, QK_NOPE_DIM:] = _rope_t(dq[:, QK_NOPE_DIM:], c_ref[rows, :] * scale,
                                                 s_ref[rows, :] * scale).astype(BF16)

        dkv_ref[:, :QK_NOPE_DIM] = dk_s[:, :QK_NOPE_DIM].astype(BF16)
        dkv_ref[:, QK_NOPE_DIM:] = dv_s[...].astype(BF16)
        dkr_ref[...] += dk_s[:, QK_NOPE_DIM:]

    head = lambda w: pl.BlockSpec((S, w), lambda b, h: (b, h))
    shared = pl.BlockSpec((S, LANES), lambda b, h: (b, 0))
    return pl.pallas_call(
        body, name=name, grid=(B, H),
        in_specs=[head(HEAD_PAD), head(HEAD_PAD), shared, head(V_HEAD_DIM), head(V_HEAD_DIM),
                  pl.BlockSpec((1, S, 1), lambda b, h: (h, b, 0)), shared, shared],
        out_specs=[head(HEAD_PAD), head(HEAD_PAD), shared],
        out_shape=[jax.ShapeDtypeStruct((T, H * HEAD_PAD), BF16), jax.ShapeDtypeStruct((T, H * HEAD_PAD), BF16),
                   jax.ShapeDtypeStruct((T, LANES), F32)],
        scratch_shapes=[pltpu.VMEM((S, HEAD_PAD), F32), pltpu.VMEM((S, V_HEAD_DIM), F32)],
        compiler_params=_cparams(("parallel", "arbitrary")),
    )(q, kv, kr, o, do, lse, cs, sn)


def adamw(w, g, m, v, *, name, dep=None, copy_g=False):
    R, C = w.shape
    cap = max(8, (1024 * 1024) // (C * 4))
    tr = _pick(R, tuple(c for c in (1024, 512, 256, 128, 64, 32, 16, 8) if c <= cap))
    c1 = 1.0 - ADAM_B1 ** ADAM_STEP
    c2 = 1.0 - ADAM_B2 ** ADAM_STEP
    nout = 4 if copy_g else 3

    def body(w_ref, g_ref, m_ref, v_ref, *rest):
        d_ref, nm_ref, nv_ref = rest[-3:]
        gv = g_ref[...]
        if copy_g:
            rest[-4][...] = gv
        mv = ADAM_B1 * m_ref[...] + (1.0 - ADAM_B1) * gv
        vv = ADAM_B2 * v_ref[...] + (1.0 - ADAM_B2) * (gv * gv)
        nm_ref[...] = mv
        nv_ref[...] = vv
        d_ref[...] = -ADAM_LR * ((mv / c1) / (jnp.sqrt(vv / c2) + ADAM_EPS) + ADAM_WD * w_ref[...])

    blk = pl.BlockSpec((tr, C), lambda i: (i, 0))
    sh = jax.ShapeDtypeStruct((R, C), F32)
    extra = [] if dep is None else [dep]
    return pl.pallas_call(
        body, name=name, grid=(R // tr,), in_specs=[blk] * 4 + [ANY] * len(extra), out_specs=[blk] * nout,
        out_shape=[sh] * nout, compiler_params=_cparams(("parallel",)),
    )(w, g, m, v, *extra)


ANY = pl.BlockSpec(memory_space=pl.ANY)


def _place():
    x, y, c = lax.axis_index("x"), lax.axis_index("y"), lax.axis_index("c")
    others = [(1 - x, y), (x, 1 - y), (1 - x, 1 - y)]
    return x, y, c, others


def _remote(src, dst, ssem, rsem, dev):
    return pltpu.make_async_remote_copy(src_ref=src, dst_ref=dst, send_sem=ssem, recv_sem=rsem,
                                        device_id=dev, device_id_type=MESH)


def _half(ref_rows, c):
    hr = ref_rows // 2
    return pl.ds(pl.multiple_of(c * hr, 16), hr)


HBM = pl.BlockSpec(memory_space=pltpu.HBM)
SEMS = pl.BlockSpec(memory_space=pltpu.SEMAPHORE)
EFFECT = pltpu.SideEffectType.DATAFLOW_SIDE_EFFECTING


def exchange_begin(name, srcs, land_shapes, plan, ncopies, after=None):
    ns, nl = len(srcs), len(land_shapes)
    nin = ns + nl + (0 if after is None else 1)

    def body(*refs):
        ssem, rsem = refs[nin], refs[nin + 1]
        for k, (s, d, dev) in enumerate(plan(refs[:ns], refs[ns:ns + nl])):
            _remote(s, d, ssem.at[k], rsem.at[k], dev).start()
        refs[-1][...] = jnp.zeros_like(refs[-1])

    bufs = [pltpu.HBM(s.shape, s.dtype) for s in srcs] + [pltpu.HBM(s.shape, s.dtype) for s in land_shapes]
    args = [pltpu.with_memory_space_constraint(s, pltpu.HBM) for s in srcs]
    args += [pltpu.with_memory_space_constraint(lax.empty(s.shape, s.dtype), pltpu.HBM) for s in land_shapes]
    if after is not None:
        args.append(after)
    out = pl.pallas_call(
        body, name=name,
        out_shape=(pltpu.SemaphoreType.DMA((ncopies,)), pltpu.SemaphoreType.DMA((ncopies,)), *bufs,
                   jax.ShapeDtypeStruct((8, LANES), F32)),
        in_specs=[HBM] * (ns + nl) + ([] if after is None else [ANY]),
        out_specs=(SEMS, SEMS, *([HBM] * (ns + nl)), pl.BlockSpec(memory_space=pltpu.VMEM)),
        input_output_aliases={i: 2 + i for i in range(ns + nl)},
        compiler_params=pltpu.CompilerParams(has_side_effects=EFFECT),
    )(*args)
    return (out[0], out[1], out[2:2 + ns], out[2 + ns:2 + ns + nl]), out[-1]


def exchange_end(name, handle, plan, after):
    ssem, rsem, srcs, lands = handle
    ns, nl = len(srcs), len(lands)

    def body(*refs):
        ssem_ref, rsem_ref = refs[ns + nl], refs[ns + nl + 1]
        for k, (s, d, dev) in enumerate(plan(refs[:ns], refs[ns:ns + nl])):
            cp = _remote(s, d, ssem_ref.at[k], rsem_ref.at[k], dev)
            cp.wait_send()
            cp.wait_recv()

    out = pl.pallas_call(
        body, name=name,
        out_shape=tuple(pltpu.HBM(s.shape, s.dtype) for s in (*srcs, *lands)),
        in_specs=[HBM] * (ns + nl) + [SEMS, SEMS, ANY], out_specs=tuple([HBM] * (ns + nl)),
        input_output_aliases={i: i for i in range(ns + nl)},
        compiler_params=pltpu.CompilerParams(has_side_effects=EFFECT),
    )(*srcs, *lands, ssem, rsem, after)
    return list(out[:ns]), list(out[ns:])


def ag_plan(src_refs, land_refs):
    x, y, c, others = _place()
    plan = []
    for s, d in zip(src_refs, land_refs):
        mine = _half(s.shape[0], c)
        for ox, oy in others:
            plan.append((s.at[mine, :], d.at[2 * x + y, mine, :], (ox, oy, c)))
    return plan


def ag_forward(lands, *, name):
    n = len(lands)

    def body(*refs):
        ins, outs = refs[:n], refs[n:2 * n]
        ssem, rsem = refs[2 * n:]
        x, y, c, others = _place()
        sent = []
        for i in range(n):
            mine = _half(ins[i].shape[1], c)
            for j, (ox, oy) in enumerate(others):
                cp = _remote(ins[i].at[2 * ox + oy, mine, :], outs[i].at[2 * ox + oy, mine, :], ssem.at[3 * i + j],
                             rsem.at[3 * i + j], (x, y, 1 - c))
                cp.start()
                sent.append(cp)
        for cp in sent:
            cp.wait()

    return pl.pallas_call(
        body, name=name, in_specs=[ANY] * n, out_specs=[ANY] * n,
        out_shape=[jax.ShapeDtypeStruct(a.shape, a.dtype) for a in lands],
        input_output_aliases={i: i for i in range(n)},
        scratch_shapes=[pltpu.SemaphoreType.DMA((3 * n,)), pltpu.SemaphoreType.DMA((3 * n,))],
        compiler_params=pltpu.CompilerParams(has_side_effects=True),
    )(*lands)


def place_own(lands, own, chip, *, name):
    n = len(lands)
    steps = 4

    def body(chip_ref, *refs):
        for i in range(n):
            refs[2 * n + i][0] = refs[i][...]

    in_specs = [pl.BlockSpec((o.shape[0] // steps, o.shape[1]), lambda t, q: (t, 0)) for o in own] + [ANY] * n
    out_specs = [pl.BlockSpec((1, o.shape[0] // steps, o.shape[1]), lambda t, q: (q[0], t, 0)) for o in own]
    return pl.pallas_call(
        body, name=name,
        grid_spec=pltpu.PrefetchScalarGridSpec(num_scalar_prefetch=1, grid=(steps,), in_specs=in_specs,
                                               out_specs=out_specs),
        out_shape=[jax.ShapeDtypeStruct(a.shape, a.dtype) for a in lands],
        input_output_aliases={1 + n + i: i for i in range(n)},
        compiler_params=_cparams(("parallel",)),
    )(chip, *own, *lands)


def rs_swap_halves(grads, *, name):
    n = len(grads)

    def body(*refs):
        ins, outs = refs[:n], refs[n:2 * n]
        ssem, rsem = refs[2 * n:]
        x, y, c, _ = _place()
        cps = []
        for i in range(n):
            theirs = _half(ins[i].shape[1], 1 - c)
            cp = _remote(ins[i].at[:, theirs, :], outs[i], ssem.at[i], rsem.at[i], (x, y, 1 - c))
            cp.start()
            cps.append(cp)
        for cp in cps:
            cp.wait()

    return pl.pallas_call(
        body, name=name, in_specs=[ANY] * n, out_specs=[ANY] * n,
        out_shape=[jax.ShapeDtypeStruct((4, g.shape[1] // 2, g.shape[2]), g.dtype) for g in grads],
        scratch_shapes=[pltpu.SemaphoreType.DMA((n,)), pltpu.SemaphoreType.DMA((n,))],
        compiler_params=pltpu.CompilerParams(has_side_effects=True),
    )(*grads)


def rs_chip_sum(g, r1, core, *, name):
    _, r, cdim = g.shape
    hr = r // 2

    def body(c_ref, g_ref, r1_ref, o_ref):
        o_ref[...] = (g_ref[...].astype(F32) + r1_ref[...].astype(F32)).astype(BF16)

    return pl.pallas_call(
        body, name=name,
        grid_spec=pltpu.PrefetchScalarGridSpec(
            num_scalar_prefetch=1, grid=(4,),
            in_specs=[pl.BlockSpec((1, hr, cdim), lambda qq, c_ref: (qq, c_ref[0], 0)),
                      pl.BlockSpec((1, hr, cdim), lambda qq, c_ref: (qq, 0, 0))],
            out_specs=pl.BlockSpec((1, hr, cdim), lambda qq, c_ref: (qq, 0, 0))),
        out_shape=jax.ShapeDtypeStruct((4, hr, cdim), BF16),
        compiler_params=_cparams(("parallel",)),
    )(core, g, r1)


def rs_plan(src_refs, land_refs):
    x, y, c, others = _place()
    plan = []
    for s, d in zip(src_refs, land_refs):
        for j, (ox, oy) in enumerate(others):
            plan.append((s.at[2 * ox + oy], d.at[j], (ox, oy, c)))
    return plan


def rs_final_sum(g, r1, r2, place, acc, l, *, name):
    _, r, cdim = g.shape
    hr = r // 2
    ch = hr // 2

    def body(p_ref, g_ref, r1_ref, a_ref, b_ref, d_ref, acc_in, o_ref):
        s = g_ref[...].astype(F32) + r1_ref[...].astype(F32)
        s = s + a_ref[...].astype(F32)
        s = s + b_ref[...].astype(F32)
        o_ref[...] = s + d_ref[...].astype(F32)

    other = lambda j: pl.BlockSpec((1, ch, cdim), lambda t, p_ref: (j, t, 0))
    return pl.pallas_call(
        body, name=name,
        grid_spec=pltpu.PrefetchScalarGridSpec(
            num_scalar_prefetch=1, grid=(2,),
            in_specs=[pl.BlockSpec((1, ch, cdim), lambda t, p_ref: (p_ref[0], 2 * p_ref[1] + t, 0)),
                      pl.BlockSpec((1, ch, cdim), lambda t, p_ref: (p_ref[0], t, 0)),
                      other(0), other(1), other(2), ANY],
            out_specs=pl.BlockSpec((1, ch, cdim), lambda t, p_ref: (l, 2 * p_ref[1] + t, 0))),
        out_shape=jax.ShapeDtypeStruct(acc.shape, F32),
        input_output_aliases={6: 0},
        compiler_params=_cparams(("parallel",)),
    )(place, g, r1, r2, r2, r2, acc)


def rs_join_halves(grads, *, name):
    n = len(grads)

    def body(*refs):
        ins, outs = refs[:n], refs[n:2 * n]
        ssem, rsem = refs[2 * n:]
        x, y, c, _ = _place()
        cps = []
        for i in range(n):
            mine = _half(ins[i].shape[1], c)
            cp = _remote(ins[i].at[:, mine, :], outs[i].at[:, mine, :], ssem.at[i], rsem.at[i], (x, y, 1 - c))
            cp.start()
            cps.append(cp)
        for cp in cps:
            cp.wait()

    return pl.pallas_call(
        body, name=name, in_specs=[ANY] * n, out_specs=[ANY] * n,
        out_shape=[jax.ShapeDtypeStruct(g.shape, g.dtype) for g in grads],
        input_output_aliases={i: i for i in range(n)},
        scratch_shapes=[pltpu.SemaphoreType.DMA((n,)), pltpu.SemaphoreType.DMA((n,))],
        compiler_params=pltpu.CompilerParams(has_side_effects=True),
    )(*grads)


def all_reduce_small(v):
    R = v.shape[0]
    rs = R // 8

    def body(v_ref, o_ref, buf, ssem1, rsem1, ssem2, rsem2):
        x, y, c, _ = _place()
        me = 4 * x + 2 * y + c
        mine = pl.ds(pl.multiple_of(me * rs, 8), rs)
        peers = []
        for k in range(1, 8):
            px = jnp.where((k >> 2) & 1 == 1, 1 - x, x)
            py = jnp.where((k >> 1) & 1 == 1, 1 - y, y)
            pc = jnp.where(k & 1 == 1, 1 - c, c)
            peers.append((px, py, pc))
        buf[me] = v_ref[mine, :]
        cps = []
        for k, (px, py, pc) in enumerate(peers):
            theirs = pl.ds(pl.multiple_of((4 * px + 2 * py + pc) * rs, 8), rs)
            cp = _remote(v_ref.at[theirs, :], buf.at[me], ssem1.at[k], rsem1.at[k], (px, py, pc))
            cp.start()
            cps.append(cp)
        for cp in cps:
            cp.wait()
        acc = buf[0]
        for d in range(1, 8):
            acc = acc + buf[d]
        o_ref[mine, :] = acc
        cps = []
        for k, peer in enumerate(peers):
            cp = _remote(o_ref.at[mine, :], o_ref.at[mine, :], ssem2.at[k], rsem2.at[k], peer)
            cp.start()
            cps.append(cp)
        for cp in cps:
            cp.wait()

    vm = pl.BlockSpec(memory_space=pltpu.VMEM)
    return pl.pallas_call(
        body, name="all_reduce_small", in_specs=[vm], out_specs=vm,
        out_shape=jax.ShapeDtypeStruct((R, LANES), F32),
        scratch_shapes=[pltpu.VMEM((8, rs, LANES), F32)] + [pltpu.SemaphoreType.DMA((7,))] * 4,
        compiler_params=pltpu.CompilerParams(vmem_limit_bytes=VMEM_LIMIT, has_side_effects=True),
    )(v)


def _to_stacked(name, full):
    R, C = full.shape
    if name in ROW_SHARDED:
        return full.reshape(4, R // 4, C)
    return jnp.transpose(full.reshape(R, 4, C // 4), (1, 0, 2))


def _from_stacked(name, st):
    _, r, c = st.shape
    if name in ROW_SHARDED:
        return st.reshape(4 * r, c)
    return jnp.transpose(st, (1, 0, 2)).reshape(r, 4 * c)


UP_PIECES = ("ffn1_up", "ffn2_up")


def _layer_weights(lands):
    w = {n: lands[n] if n in UP_PIECES else _from_stacked(n, lands[n]) for n in lands}
    if "w_in" not in w:
        return w
    win = w.pop("w_in")
    D = win.shape[0]
    p0, p1, p2, p3 = POOL_DIM, POOL_DIM + Q_LORA_RANK, POOL_DIM + Q_LORA_RANK + KV_LORA_RANK, \
        POOL_DIM + Q_LORA_RANK + KV_LORA_RANK + QK_ROPE_DIM
    w["w_pool"] = win[:, :p0]
    w["w_lat"] = jnp.concatenate([win[:, p0:p3], jnp.zeros((D, LAT_DIM - (p3 - p0)), win.dtype)], axis=1)
    w["w_gate"] = win[:, p3:]
    uq = w["w_uq"].reshape(Q_LORA_RANK, N_HEADS, QK_DIM)
    w["w_uq"] = jnp.concatenate([uq, jnp.zeros((Q_LORA_RANK, N_HEADS, HEAD_PAD - QK_DIM), uq.dtype)],
                                axis=2).reshape(Q_LORA_RANK, N_HEADS * HEAD_PAD)
    return w


def _layer_grads_stacked(dw):
    dw = dict(dw)
    if "w_lat" in dw:
        lat = dw.pop("w_lat")
        dw["w_in"] = jnp.concatenate([dw.pop("w_pool"), lat[:, :Q_LORA_RANK + KV_LORA_RANK + QK_ROPE_DIM],
                                      dw.pop("w_gate")], axis=1)
        dw["w_uq"] = dw["w_uq"].reshape(Q_LORA_RANK, N_HEADS, HEAD_PAD)[:, :, :QK_DIM].reshape(Q_LORA_RANK,
                                                                                                 N_HEADS * QK_DIM)
    return {n: dw[n] if n in UP_PIECES else _to_stacked(n, dw[n]) for n in BIG if n in dw}


def _rope_tables(positions):
    inv_freq = ROPE_THETA ** (-jnp.arange(0, QK_ROPE_DIM, 2, dtype=F32) / QK_ROPE_DIM)
    ang = positions.astype(F32).reshape(-1)[:, None] * inv_freq
    cos, sin = jnp.cos(ang), jnp.sin(ang)
    z = jnp.zeros((ang.shape[0], LANES - QK_ROPE_DIM), F32)
    return jnp.concatenate([cos, cos, z], axis=1), jnp.concatenate([-sin, sin, z], axis=1)


def _pack_small(vals):
    parts, total = [], 0
    for n in SMALL:
        f = vals[n].reshape(-1).astype(F32)
        pad = (-f.shape[0]) % (8 * LANES)
        parts.append(jnp.pad(f, (0, pad)))
        total += f.shape[0] + pad
    parts.append(jnp.zeros(((-total) % (64 * LANES),), F32))
    return jnp.concatenate(parts).reshape(-1, LANES)


def _unpack_small(packed, like):
    flat = packed.reshape(-1)
    out, off = {}, 0
    for n in SMALL:
        size = like[n].size
        out[n] = flat[off:off + size].reshape(like[n].shape)
        off += size + ((-size) % (8 * LANES))
    return out


def _ffn_fwd(x, g, wu4, wd, tag, dep=None):
    h = rms_fwd(x, g, dep=dep, name=f"{tag}_norm")
    gate, up, a = ffn_up_act(h, wu4, name=f"{tag}_up_act")
    y = mm(a, wd, res=x, alpha=0.5, name=f"{tag}_down")
    return y, (x, h, gate, up, a)


def _ffn_bwd(dy, saved, g, wu4, wd, tag, dep=None):
    x, h, gate, up, a = saved
    dgate, dup = ffn_down_dx_act(dy, wd, gate, up, dep=dep, name=f"{tag}_down_dx_act")
    dwd = mm(a, dy, ta=True, alpha=0.5, out_dtype=BF16, name=f"{tag}_down_dw")
    dwu4 = ffn_up_dw(h, dgate, dup, name=f"{tag}_up_dw")
    dx, dg = ffn_up_dx_norm(dgate, dup, wu4, x, g, dy, name=f"{tag}_up_dx_norm")
    return dx, dg, dwu4, dwd


def _mix_fwd(x, p, w, cs, sn, B, dep=None):
    h = rms_fwd(x, p["norm_mix"], dep=dep, name="mix_norm")
    lat, xp, gl = mix_in(h, w["w_lat"], w["w_pool"], w["w_gate"], name="mix_in")
    mixed = pool_fwd(xp, p["pool_maps"].astype(BF16), p["pool_scale"], B, name="pool_fwd")
    ba = mm(mixed, w["w_pool_proj"], out_dtype=BF16, name="mix_pool_proj")
    qn, kvn, kr = lat_fwd(lat, p["q_latent_norm"], p["kv_latent_norm"], cs, sn, name="lat_fwd")
    kv = mm(kvn, w["w_ukv"], out_dtype=BF16, name="mix_ukv")
    o, lse, q = attn_fwd(mm(qn, w["w_uq"], name="mix_uq"), kv, kr, cs, sn, B, name="attn_fwd")
    bb, merged = attn_proj_gate(o, w["w_attn_proj"], gl, p["b_gate"], ba, name="mix_attn_proj_gate")
    y = mm(merged, w["w_out"], res=x, name="mix_out")
    return y, (x, h, lat, xp, gl, mixed, ba, qn, kvn, kr, q, kv, o, lse, bb, merged)


def _mix_bwd(dy, saved, p, w, cs, sn, B):
    x, h, lat, xp, gl, mixed, ba, qn, kvn, kr, q, kv, o, lse, bb, merged = saved
    dw, ds = {}, {}
    dw["w_out"] = mm(merged, dy, ta=True, out_dtype=BF16, name="mix_out_dw")
    dba, dbb, dgl, ds["b_gate"] = out_dx_gate(dy, w["w_out"], gl, p["b_gate"], ba, bb, name="mix_out_dx_gate")
    dw["w_attn_proj"] = mm(o, dbb, ta=True, out_dtype=BF16, name="mix_attn_proj_dw")
    do = mm(dbb, w["w_attn_proj"], tb=True, out_dtype=BF16, name="mix_attn_proj_dx")
    dw["w_pool_proj"] = mm(mixed, dba, ta=True, out_dtype=BF16, name="mix_pool_proj_dw")
    dmixed = mm(dba, w["w_pool_proj"], tb=True, name="mix_pool_proj_dx")
    dxp, ds["pool_maps"], ds["pool_scale"] = pool_bwd(xp, dmixed, p["pool_maps"].astype(BF16), p["pool_scale"], B,
                                                      name="pool_bwd")
    dqb, dkv, dkr = attn_bwd(q, kv, kr, o, do, lse, cs, sn, B, name="attn_bwd")
    dw["w_ukv"] = mm(kvn, dkv, ta=True, out_dtype=BF16, name="mix_ukv_dw")
    dkvn = mm(dkv, w["w_ukv"], tb=True, name="mix_ukv_dx")
    dw["w_uq"] = mm(qn, dqb, ta=True, out_dtype=BF16, name="mix_uq_dw")
    dqn = mm(dqb, w["w_uq"], tb=True, name="mix_uq_dx")
    dlat, ds["q_latent_norm"], ds["kv_latent_norm"] = lat_bwd(lat, p["q_latent_norm"], p["kv_latent_norm"], dqn, dkvn,
                                                               dkr, cs, sn, name="lat_bwd")
    dw["w_lat"] = mm(h, dlat, ta=True, out_dtype=BF16, name="mix_lat_dw")
    dw["w_pool"] = mm(h, dxp, ta=True, out_dtype=BF16, name="mix_pool_in_dw")
    dw["w_gate"] = mm(h, dgl, ta=True, out_dtype=BF16, name="mix_gate_in_dw")
    dx, ds["norm_mix"] = mix_in_dx_norm(dlat, dxp, dgl, w["w_lat"], w["w_pool"], w["w_gate"], x, p["norm_mix"], dy,
                                        name="mix_in_dx_norm")
    return dx, dw, ds


def kernel(x, positions, norm_ffn1, ffn1_up, ffn1_down, norm_mix, w_in, b_gate, pool_maps, pool_scale, w_pool_proj, q_latent_norm, w_uq, kv_latent_norm, w_ukv, w_attn_proj, w_out, norm_ffn2, ffn2_up, ffn2_down, final_norm, loss_target, m_norm_ffn1, m_ffn1_up, m_ffn1_down, m_norm_mix, m_w_in, m_b_gate, m_pool_maps, m_pool_scale, m_w_pool_proj, m_q_latent_norm, m_w_uq, m_kv_latent_norm, m_w_ukv, m_w_attn_proj, m_w_out, m_norm_ffn2, m_ffn2_up, m_ffn2_down, m_final_norm, v_norm_ffn1, v_ffn1_up, v_ffn1_down, v_norm_mix, v_w_in, v_b_gate, v_pool_maps, v_pool_scale, v_w_pool_proj, v_q_latent_norm, v_w_uq, v_kv_latent_norm, v_w_ukv, v_w_attn_proj, v_w_out, v_norm_ffn2, v_ffn2_up, v_ffn2_down, v_final_norm):
    given = dict(locals())
    B, S, D = x.shape
    T = B * S
    L = norm_ffn1.shape[0]
    W = {n: given[n] for n in WEIGHTS}
    Mo = {n: given["m_" + n] for n in WEIGHTS}
    Vo = {n: given["v_" + n] for n in WEIGHTS}
    core = lax.axis_index("c").astype(jnp.int32)
    chip = (2 * lax.axis_index("x") + lax.axis_index("y")).astype(jnp.int32)

    core_arr = core.reshape(1)
    chip_arr = chip.reshape(1)
    place = jnp.stack([chip, core])
    first = ("ffn1_up", "ffn1_down")
    rest = tuple(n for n in BIG if n not in first)

    own = [{n: W[n][l].astype(BF16) for n in BIG} for l in range(L)]

    def ag_begin(l, names, tag, after=None):
        lands = [jax.ShapeDtypeStruct((4,) + own[l][n].shape, BF16) for n in names]
        return exchange_begin(f"ag_start_{tag}", [own[l][n] for n in names], lands, ag_plan, 3 * len(names), after)

    def ag_finish(handle, names, tag, after):
        mine, lands = exchange_end(f"ag_wait_{tag}", handle, ag_plan, after)
        lands = ag_forward(lands, name=f"ag_forward_{tag}")
        return _layer_weights(dict(zip(names, place_own(lands, mine, chip_arr, name=f"place_own_{tag}"))))

    h_first, t1 = ag_begin(0, first, "0a")
    cs, sn = _rope_tables(positions)
    xs = x.reshape(T, D) + t1[0, 0]
    saved, handle = [], None
    for l in range(L):
        p = {n: W[n][l] for n in SMALL if n != "final_norm"}
        dep = None
        if l == 0:
            w = ag_finish(h_first, first, "0a", xs)
            h_rest, dep = ag_begin(0, rest, "0b", after=w["ffn1_down"])
        else:
            w = ag_finish(handle, BIG, str(l), xs)
            if l + 1 < L:
                handle, dep = ag_begin(l + 1, BIG, str(l + 1), after=xs)
        xs, s1 = _ffn_fwd(xs, p["norm_ffn1"], w["ffn1_up"], w["ffn1_down"], "ffn1", dep=dep)
        dep = None
        if l == 0:
            w.update(ag_finish(h_rest, rest, "0b", xs))
            if L > 1:
                handle, dep = ag_begin(1, BIG, "1", after=xs)
        xs, s2 = _mix_fwd(xs, p, w, cs, sn, B, dep=dep)
        xs, s3 = _ffn_fwd(xs, p["norm_ffn2"], w["ffn2_up"], w["ffn2_down"], "ffn2")
        saved.append((w, p, s1, s2, s3))

    dx, dfinal, loss_tile = loss_head(xs, final_norm, loss_target.reshape(T, D), name="loss_head")
    loss = lax.psum(loss_tile[0, 0], ("x", "y", "c"))

    def rs_begin(dw, tag):
        stacked = _layer_grads_stacked(dw)
        names = tuple(stacked)
        parts = [stacked[n] for n in names]
        r1 = rs_swap_halves(parts, name=f"rs_swap_{tag}")
        sums = [rs_chip_sum(g, a, core_arr, name=f"rs_chip_sum_{n}") for n, g, a in zip(names, parts, r1)]
        lands = [jax.ShapeDtypeStruct((3,) + s.shape[1:], BF16) for s in sums]
        handle, token = exchange_begin(f"rs_start_{tag}", sums, lands, rs_plan, 3 * len(names))
        return (names, parts, r1, handle), token

    acc = {n: lax.empty(W[n].shape, F32) for n in BIG}

    def rs_finish(l, pending, tag, after):
        names, parts, r1, handle = pending
        _, r2 = exchange_end(f"rs_wait_{tag}", handle, rs_plan, after)
        for n, g, a, b in zip(names, parts, r1, r2):
            acc[n] = rs_final_sum(g, a, b, place, acc[n], l, name=f"rs_final_sum_{n}")

    small_layers, pending, dep = [], [], None
    for l in reversed(range(L)):
        w, p, s1, s2, s3 = saved[l]
        dx, dg2, dwu2, dwd2 = _ffn_bwd(dx, s3, p["norm_ffn2"], w["ffn2_up"], w["ffn2_down"], "ffn2", dep=dep)
        dx, dw, ds = _mix_bwd(dx, s2, p, w, cs, sn, B)
        dw.update(ffn2_up=dwu2, ffn2_down=dwd2)
        dep = None
        if l == 0:
            early, dep = rs_begin(dw, "0b")
            dw = {}
        dx, dg1, dwu1, dwd1 = _ffn_bwd(dx, s1, p["norm_ffn1"], w["ffn1_up"], w["ffn1_down"], "ffn1", dep=dep)
        dw.update(ffn1_up=dwu1, ffn1_down=dwd1)
        ds.update(norm_ffn1=dg1, norm_ffn2=dg2)
        small_layers.append(ds)
        last, dep = rs_begin(dw, "0a" if l == 0 else str(l))
        if l > 0:
            pending.append((l, last))
    small_layers.reverse()

    small = {n: jnp.stack([small_layers[l][n].reshape(W[n].shape[1:]) for l in range(L)]) for n in SMALL
             if n != "final_norm"}
    small["final_norm"] = dfinal.reshape(final_norm.shape)
    grads = _unpack_small(all_reduce_small(_pack_small(small) + dep[0, 0]), small)
    delta, new_m, new_v = {}, {}, {}
    d, nm, nv = adamw(_pack_small(W), _pack_small(grads), _pack_small(Mo), _pack_small(Vo), name="adamw_small")
    delta.update(_unpack_small(d, W))
    new_m.update(_unpack_small(nm, W))
    new_v.update(_unpack_small(nv, W))

    def update(names, tag, d):
        joined = rs_join_halves([acc[n] for n in names], name=f"rs_join_{tag}")
        for n, g in zip(names, joined):
            flip = (lambda a: jnp.swapaxes(a, 1, 2)) if n == "w_in" else (lambda a: a)
            sh = flip(W[n]).shape
            two = lambda a: flip(a).reshape(sh[0] * sh[1], sh[2])
            back = lambda a: flip(a.reshape(sh))
            gc, d, nm, nv = adamw(two(W[n]), two(g), two(Mo[n]), two(Vo[n]), dep=d, copy_g=True, name=f"adamw_{n}")
            grads[n], delta[n], new_m[n], new_v[n] = back(gc), back(d), back(nm), back(nv)
        return d

    for l, item in pending:
        rs_finish(l, item, str(l), d)
    rs_finish(0, early, "0b", d)
    d = update(rest, "rest", d)
    rs_finish(0, last, "0a", d)
    update(first, "first", d)

    return (loss, dx.reshape(B, S, D), *[grads[n] for n in WEIGHTS], *[delta[n] for n in WEIGHTS],
            *[new_m[n] for n in WEIGHTS], *[new_v[n] for n in WEIGHTS])
```

```python
import functools

import jax
import jax.numpy as jnp
from jax import lax
from jax.experimental import pallas as pl
from jax.experimental.pallas import tpu as pltpu

F32 = jnp.float32
BF16 = jnp.bfloat16

N_HEADS = 8
QK_NOPE_DIM = 128
QK_ROPE_DIM = 64
QK_DIM = QK_NOPE_DIM + QK_ROPE_DIM
V_HEAD_DIM = 128
HEAD_PAD = 256
Q_LORA_RANK = 384
KV_LORA_RANK = 256
ROPE_THETA = 10000.0
POOL_WINDOWS = (2, 4, 8, 16)
N_POOL_GROUPS = 4
POOL_GROUP_DIM = 128
POOL_DIM = N_POOL_GROUPS * POOL_GROUP_DIM
LAT_DIM = 768
NORM_EPS = 1e-6
ADAM_LR = 0.001
ADAM_B1 = 0.9
ADAM_B2 = 0.999
ADAM_EPS = 1e-08
ADAM_WD = 0.01
ADAM_STEP = 10
NEG_INF = -1e30
LANES = 128
ATT_BLOCK = 512
VMEM_LIMIT = 48 * 1024 * 1024
MESH = pl.DeviceIdType.MESH
_NT = (((1,), (1,)), ((), ()))
_TN = (((0,), (0,)), ((), ()))

BIG = ("ffn1_up", "ffn1_down", "w_in", "w_pool_proj", "w_uq", "w_ukv", "w_attn_proj", "w_out",
       "ffn2_up", "ffn2_down")
ROW_SHARDED = ("ffn1_down", "w_attn_proj", "w_out", "ffn2_down")
SMALL = ("norm_ffn1", "norm_mix", "b_gate", "pool_maps", "pool_scale", "q_latent_norm",
         "kv_latent_norm", "norm_ffn2", "final_norm")
WEIGHTS = ("norm_ffn1", "ffn1_up", "ffn1_down", "norm_mix", "w_in", "b_gate", "pool_maps", "pool_scale",
           "w_pool_proj", "q_latent_norm", "w_uq", "kv_latent_norm", "w_ukv", "w_attn_proj", "w_out",
           "norm_ffn2", "ffn2_up", "ffn2_down", "final_norm")


def _pick(dim, cands):
    for c in cands:
        if c <= dim and dim % c == 0:
            return c
    return dim


def _cparams(sem=None, **kw):
    if sem is not None:
        kw["dimension_semantics"] = sem
    return pltpu.CompilerParams(vmem_limit_bytes=VMEM_LIMIT, **kw)


def _hbm_call(body, **kw):
    call = pl.pallas_call(body, **kw)

    def run(*args):
        pin = lambda a: a.ndim >= 2 and a.dtype in (F32, BF16) and a.shape != (8, LANES)
        return call(*[pltpu.with_memory_space_constraint(a, pltpu.HBM) if pin(a) else a for a in args])

    return run


def _sigmoid(x):
    return 0.5 * jnp.tanh(0.5 * x) + 0.5


MM_TILE_BUDGET = 30 * 1024 * 1024
TILE_SIZES = (1408, 1024, 768, 512, 384, 256, 128)


V7X_MXU_FLOPS = 9.0e14
V7X_HBM_BYTES = 2.5e12
GRID_STEP_S = 0.35e-6


def _mm_tiles(M, N, K, sa, sb, so, sr):
    tks = [K] if K <= 2816 else [t for t in (2816, 2048, 1408, 1024, 512, 256, 128) if K % t == 0]
    best = None
    for tk in tks:
        for tm in [t for t in TILE_SIZES if M % t == 0] or [M]:
            for tn in [t for t in TILE_SIZES if N % t == 0] or [N]:
                need = 2 * (tm * tk * sa + tk * tn * sb + tm * tn * (so + sr)) + (tm * tn * 4 if tk < K else 0)
                if need > MM_TILE_BUDGET:
                    continue
                ni, nj, nk = M // tm, N // tn, K // tk
                a_bytes = M * K * sa * (nj if nk > 1 else 1)
                b_bytes = K * N * sb * (1 if nj == 1 and nk == 1 else ni)
                traffic = a_bytes + b_bytes + M * N * (so + sr) + (M * N * 8 * nk if nk > 1 else 0)
                t = max(2.0 * M * N * K / V7X_MXU_FLOPS, traffic / V7X_HBM_BYTES) + ni * nj * nk * GRID_STEP_S
                if best is None or t < best[0]:
                    best = (t, (tm, tn, tk))
    assert best is not None, (M, N, K)
    return best[1]


def mm(a, b, *, name, ta=False, tb=False, out_dtype=F32, res=None, alpha=1.0, dep=None):
    if ta:
        K, M = a.shape
    else:
        M, K = a.shape
    if tb:
        N, K2 = b.shape
    else:
        K2, N = b.shape
    assert K == K2, (a.shape, b.shape, ta, tb)
    tm, tn, tk = _mm_tiles(M, N, K, a.dtype.itemsize, b.dtype.itemsize, jnp.dtype(out_dtype).itemsize,
                           0 if res is None else res.dtype.itemsize)
    nk = K // tk
    dims = (((0 if ta else 1,), (1 if tb else 0,)), ((), ()))

    def body(*refs):
        a_ref, b_ref = refs[:2]
        r_ref = refs[2] if res is not None else None
        o_ref = refs[-2] if nk > 1 else refs[-1]

        def finish(r):
            if alpha != 1.0:
                r = r * alpha
            if res is not None:
                r = r_ref[...].astype(F32) + r
            o_ref[...] = r.astype(out_dtype)

        part = lax.dot_general(a_ref[...].astype(BF16), b_ref[...].astype(BF16), dims, preferred_element_type=F32)
        if nk == 1:
            finish(part)
            return
        acc = refs[-1]
        k = pl.program_id(2)

        @pl.when(k == 0)
        def _():
            acc[...] = part

        @pl.when(k > 0)
        def _():
            acc[...] += part

        @pl.when(k == nk - 1)
        def _():
            finish(acc[...])

    a_spec = pl.BlockSpec((tk, tm), lambda i, j, k: (k, i)) if ta else pl.BlockSpec((tm, tk), lambda i, j, k: (i, k))
    b_spec = pl.BlockSpec((tn, tk), lambda i, j, k: (j, k)) if tb else pl.BlockSpec((tk, tn), lambda i, j, k: (k, j))
    o_spec = pl.BlockSpec((tm, tn), lambda i, j, k: (i, j))
    in_specs = [a_spec, b_spec]
    args = [a, b]
    if res is not None:
        in_specs.append(o_spec)
        args.append(res)
    if dep is not None:
        in_specs.append(pl.BlockSpec((8, LANES), lambda i, j, k: (0, 0)))
        args.append(dep)
    return _hbm_call(
        body, name=name, grid=(M // tm, N // tn, nk), in_specs=in_specs, out_specs=o_spec,
        out_shape=jax.ShapeDtypeStruct((M, N), out_dtype),
        scratch_shapes=[pltpu.VMEM((tm, tn), F32)] if nk > 1 else [],
        compiler_params=_cparams(("parallel", "parallel", "arbitrary")),
    )(*args)


MXU_COLS = 256


def _col_chunks(n):
    return [(lo, min(lo + MXU_COLS, n)) for lo in range(0, n, MXU_COLS)]


def ffn_up_act(h, wu4, *, name):
    T, D = h.shape
    cq = wu4.shape[2]
    Fh = 2 * cq
    tm = _pick(T, (512, 256, 128))

    def body(h_ref, wg_ref, wu_ref, g_ref, u_ref, a_ref):
        hv = h_ref[...]
        for lo, hi in _col_chunks(cq):
            gv = jnp.dot(hv, wg_ref[0, :, lo:hi], preferred_element_type=F32)
            uv = jnp.dot(hv, wu_ref[0, :, lo:hi], preferred_element_type=F32)
            g_ref[:, lo:hi] = gv.astype(BF16)
            u_ref[:, lo:hi] = uv.astype(BF16)
            a_ref[:, lo:hi] = (gv * _sigmoid(gv) * uv).astype(BF16)

    tile = pl.BlockSpec((tm, cq), lambda j, i: (i, j))
    sh = jax.ShapeDtypeStruct((T, Fh), BF16)
    return _hbm_call(
        body, name=name, grid=(2, T // tm),
        in_specs=[pl.BlockSpec((tm, D), lambda j, i: (i, 0)), pl.BlockSpec((1, D, cq), lambda j, i: (j, 0, 0)),
                  pl.BlockSpec((1, D, cq), lambda j, i: (2 + j, 0, 0))],
        out_specs=[tile, tile, tile], out_shape=[sh, sh, sh],
        compiler_params=_cparams(("parallel", "parallel")),
    )(h, wu4, wu4)


def ffn_down_dx_act(dy, wd, g, u, *, dep=None, name):
    T, D = dy.shape
    Fh = wd.shape[0]
    cq = Fh // 2
    tm = _pick(T, (512, 256, 128))

    def body(dy_ref, wd_ref, g_ref, u_ref, *rest):
        dg_ref, du_ref = rest[-2:]
        dyv = dy_ref[...].astype(BF16)
        for lo, hi in _col_chunks(cq):
            da = 0.5 * lax.dot_general(dyv, wd_ref[lo:hi, :], _NT, preferred_element_type=F32)
            gv = g_ref[:, lo:hi].astype(F32)
            uv = u_ref[:, lo:hi].astype(F32)
            s = _sigmoid(gv)
            dg_ref[:, lo:hi] = (da * uv * (s * (1.0 + gv * (1.0 - s)))).astype(BF16)
            du_ref[:, lo:hi] = (da * (gv * s)).astype(BF16)

    tile = pl.BlockSpec((tm, cq), lambda j, i: (i, j))
    sh = jax.ShapeDtypeStruct((T, Fh), BF16)
    in_specs = [pl.BlockSpec((tm, D), lambda j, i: (i, 0)), pl.BlockSpec((cq, D), lambda j, i: (j, 0)), tile, tile]
    args = [dy, wd, g, u]
    if dep is not None:
        in_specs.append(pl.BlockSpec((8, LANES), lambda j, i: (0, 0)))
        args.append(dep)
    return _hbm_call(
        body, name=name, grid=(2, T // tm), in_specs=in_specs, out_specs=[tile, tile], out_shape=[sh, sh],
        compiler_params=_cparams(("parallel", "parallel")),
    )(*args)


def ffn_up_dw(h, dg, du, *, name):
    T, D = h.shape
    cq = dg.shape[1] // 2
    tk = _pick(T, (1024, 512, 256, 128))
    nk = T // tk

    def body(h_ref, dg_ref, du_ref, o_ref, acc):
        p = pl.program_id(0)
        k = pl.program_id(1)

        @pl.when(k == 0)
        def _():
            acc[...] = jnp.zeros_like(acc)

        @pl.when(p < 2)
        def _():
            acc[...] += lax.dot_general(h_ref[...], dg_ref[...], _TN, preferred_element_type=F32)

        @pl.when(p >= 2)
        def _():
            acc[...] += lax.dot_general(h_ref[...], du_ref[...], _TN, preferred_element_type=F32)

        @pl.when(k == nk - 1)
        def _():
            o_ref[0] = acc[...].astype(BF16)

    return _hbm_call(
        body, name=name, grid=(4, nk),
        in_specs=[pl.BlockSpec((tk, D), lambda p, k: (k, 0)),
                  pl.BlockSpec((tk, cq), lambda p, k: (jnp.where(p < 2, k, 0), jnp.minimum(p, 1))),
                  pl.BlockSpec((tk, cq), lambda p, k: (jnp.where(p < 2, 0, k), jnp.maximum(p - 2, 0)))],
        out_specs=pl.BlockSpec((1, D, cq), lambda p, k: (p, 0, 0)),
        out_shape=jax.ShapeDtypeStruct((4, D, cq), BF16),
        scratch_shapes=[pltpu.VMEM((D, cq), F32)],
        compiler_params=_cparams(("parallel", "arbitrary")),
    )(h, dg, du)


def ffn_up_dx_norm(dg, du, wu4, x, g, dy, *, name):
    T = dg.shape[0]
    _, D, cq = wu4.shape
    tm = _pick(T, (512, 256, 128))

    def body(dg_ref, du_ref, wg_ref, wu_ref, x_ref, g_ref, dy_ref, dx_ref, dgain_ref, acc):
        i = pl.program_id(0)
        k = pl.program_id(1)
        part = lax.dot_general(dg_ref[...], wg_ref[0], _NT, preferred_element_type=F32)
        part = part + lax.dot_general(du_ref[...], wu_ref[0], _NT, preferred_element_type=F32)

        @pl.when(jnp.logical_and(i == 0, k == 0))
        def _():
            dgain_ref[...] = jnp.zeros_like(dgain_ref)

        @pl.when(k == 0)
        def _():
            acc[...] = part

        @pl.when(k == 1)
        def _():
            dx, dgain = _rms_bwd_math(x_ref[...], g_ref[...], acc[...] + part)
            dx_ref[...] = dy_ref[...] + dx
            dgain_ref[...] += dgain

    tile = pl.BlockSpec((tm, cq), lambda i, k: (i, k))
    row = pl.BlockSpec((tm, D), lambda i, k: (i, 0))
    vec = pl.BlockSpec((1, D), lambda i, k: (0, 0))
    return _hbm_call(
        body, name=name, grid=(T // tm, 2),
        in_specs=[tile, tile, pl.BlockSpec((1, D, cq), lambda i, k: (k, 0, 0)),
                  pl.BlockSpec((1, D, cq), lambda i, k: (2 + k, 0, 0)), row, vec, row],
        out_specs=[row, vec],
        out_shape=[jax.ShapeDtypeStruct((T, D), F32), jax.ShapeDtypeStruct((1, D), F32)],
        scratch_shapes=[pltpu.VMEM((tm, D), F32)],
        compiler_params=_cparams(("arbitrary", "arbitrary")),
    )(dg, du, wu4, wu4, x, g.reshape(1, D), dy)


def mix_in(h, w_lat, w_pool, w_gate, *, name):
    T, D = h.shape
    tm = _pick(T, (512, 256, 128))

    def body(h_ref, wl_ref, wp_ref, wg_ref, lat_ref, xp_ref, gl_ref):
        hv = h_ref[...]
        lat_ref[...] = jnp.dot(hv, wl_ref[...], preferred_element_type=F32)
        xp_ref[...] = jnp.dot(hv, wp_ref[...], preferred_element_type=F32)
        gl_ref[...] = jnp.dot(hv, wg_ref[...], preferred_element_type=F32).astype(BF16)

    whole = lambda a: pl.BlockSpec(a.shape, lambda i: (0, 0))
    out = lambda a: pl.BlockSpec((tm, a.shape[1]), lambda i: (i, 0))
    return _hbm_call(
        body, name=name, grid=(T // tm,),
        in_specs=[pl.BlockSpec((tm, D), lambda i: (i, 0)), whole(w_lat), whole(w_pool), whole(w_gate)],
        out_specs=[out(w_lat), out(w_pool), out(w_gate)],
        out_shape=[jax.ShapeDtypeStruct((T, w_lat.shape[1]), F32), jax.ShapeDtypeStruct((T, w_pool.shape[1]), F32),
                   jax.ShapeDtypeStruct((T, w_gate.shape[1]), BF16)],
        compiler_params=_cparams(("parallel",)),
    )(h, w_lat, w_pool, w_gate)


def mix_in_dx_norm(dlat, dxp, dgl, w_lat, w_pool, w_gate, x, g, dy, *, name):
    T, D = x.shape
    tm = _pick(T, (512, 256, 128))

    def body(dlat_ref, dxp_ref, dgl_ref, wl_ref, wp_ref, wg_ref, x_ref, g_ref, dy_ref, dx_ref, dgain_ref):
        @pl.when(pl.program_id(0) == 0)
        def _():
            dgain_ref[...] = jnp.zeros_like(dgain_ref)

        dh = lax.dot_general(dlat_ref[...], wl_ref[...], _NT, preferred_element_type=F32)
        dh = dh + lax.dot_general(dxp_ref[...], wp_ref[...], _NT, preferred_element_type=F32)
        dh = dh + lax.dot_general(dgl_ref[...], wg_ref[...], _NT, preferred_element_type=F32)
        dx, dgain = _rms_bwd_math(x_ref[...], g_ref[...], dh)
        dx_ref[...] = dy_ref[...] + dx
        dgain_ref[...] += dgain

    row = lambda a: pl.BlockSpec((tm, a.shape[1]), lambda i: (i, 0))
    whole = lambda a: pl.BlockSpec(a.shape, lambda i: (0, 0))
    vec = pl.BlockSpec((1, D), lambda i: (0, 0))
    return _hbm_call(
        body, name=name, grid=(T // tm,),
        in_specs=[row(dlat), row(dxp), row(dgl), whole(w_lat), whole(w_pool), whole(w_gate), row(x), vec, row(dy)],
        out_specs=[row(x), vec],
        out_shape=[jax.ShapeDtypeStruct((T, D), F32), jax.ShapeDtypeStruct((1, D), F32)],
        compiler_params=_cparams(("arbitrary",)),
    )(dlat, dxp, dgl, w_lat, w_pool, w_gate, x, g.reshape(1, D), dy)


def _rows(T, width_bytes):
    cap = max(8, (2 * 1024 * 1024) // width_bytes)
    return _pick(T, tuple(c for c in (1024, 512, 256, 128, 64, 32, 16) if c <= cap))


def rms_fwd(x, g, *, name, dep=None):
    T, D = x.shape
    tm = _rows(T, D * 4)

    def body(x_ref, g_ref, *rest):
        xv = x_ref[...]
        r = lax.rsqrt(jnp.mean(xv * xv, axis=-1, keepdims=True) + NORM_EPS)
        rest[-1][...] = (xv * r * g_ref[...]).astype(BF16)

    in_specs = [pl.BlockSpec((tm, D), lambda i: (i, 0)), pl.BlockSpec((1, D), lambda i: (0, 0))]
    args = [x, g.reshape(1, D)]
    if dep is not None:
        in_specs.append(pl.BlockSpec((8, LANES), lambda i: (0, 0)))
        args.append(dep)
    return _hbm_call(
        body, name=name, grid=(T // tm,), in_specs=in_specs,
        out_specs=pl.BlockSpec((tm, D), lambda i: (i, 0)),
        out_shape=jax.ShapeDtypeStruct((T, D), BF16),
        compiler_params=_cparams(("parallel",)),
    )(*args)


def _rms_bwd_math(xv, gv, dh):
    r = lax.rsqrt(jnp.mean(xv * xv, axis=-1, keepdims=True) + NORM_EPS)
    xn = xv * r
    dg = jnp.sum(dh * xn, axis=0, keepdims=True)
    dxn = dh * gv
    dx = r * (dxn - xn * jnp.mean(dxn * xn, axis=-1, keepdims=True))
    return dx, dg


def _rope(xv, cv, sv):
    half = QK_ROPE_DIM // 2
    lane = lax.broadcasted_iota(jnp.int32, xv.shape, 1)
    swapped = jnp.where(lane < half, pltpu.roll(xv, LANES - half, 1), pltpu.roll(xv, half, 1))
    return xv * cv + swapped * sv


def _rope_t(dy, cv, sv):
    half = QK_ROPE_DIM // 2
    ds = dy * sv
    lane = lax.broadcasted_iota(jnp.int32, dy.shape, 1)
    swapped = jnp.where(lane < half, pltpu.roll(ds, LANES - half, 1), pltpu.roll(ds, half, 1))
    return dy * cv + swapped


def lat_fwd(lat, qn_w, kvn_w, cs, sn, *, name):
    T = lat.shape[0]
    tm = _rows(T, LAT_DIM * 4)
    kv0 = Q_LORA_RANK
    kr0 = Q_LORA_RANK + KV_LORA_RANK

    def body(lat_ref, qw_ref, kw_ref, c_ref, s_ref, qn_ref, kvn_ref, kr_ref):
        ql = lat_ref[:, :kv0]
        r = lax.rsqrt(jnp.mean(ql * ql, axis=-1, keepdims=True) + NORM_EPS)
        qn_ref[...] = (ql * r * qw_ref[...]).astype(BF16)
        kl = lat_ref[:, kv0:kr0]
        r = lax.rsqrt(jnp.mean(kl * kl, axis=-1, keepdims=True) + NORM_EPS)
        kvn_ref[...] = (kl * r * kw_ref[...]).astype(BF16)
        kr_ref[...] = _rope(lat_ref[:, kr0:], c_ref[...], s_ref[...]).astype(BF16)

    return _hbm_call(
        body, name=name, grid=(T // tm,),
        in_specs=[pl.BlockSpec((tm, LAT_DIM), lambda i: (i, 0)),
                  pl.BlockSpec((1, Q_LORA_RANK), lambda i: (0, 0)),
                  pl.BlockSpec((1, KV_LORA_RANK), lambda i: (0, 0)),
                  pl.BlockSpec((tm, LANES), lambda i: (i, 0)), pl.BlockSpec((tm, LANES), lambda i: (i, 0))],
        out_specs=[pl.BlockSpec((tm, Q_LORA_RANK), lambda i: (i, 0)),
                   pl.BlockSpec((tm, KV_LORA_RANK), lambda i: (i, 0)),
                   pl.BlockSpec((tm, LANES), lambda i: (i, 0))],
        out_shape=[jax.ShapeDtypeStruct((T, Q_LORA_RANK), BF16), jax.ShapeDtypeStruct((T, KV_LORA_RANK), BF16),
                   jax.ShapeDtypeStruct((T, LANES), BF16)],
        compiler_params=_cparams(("parallel",)),
    )(lat, qn_w.reshape(1, -1), kvn_w.reshape(1, -1), cs, sn)


def lat_bwd(lat, qn_w, kvn_w, dqn, dkvn, dkr, cs, sn, *, name):
    T = lat.shape[0]
    tm = _rows(T, LAT_DIM * 4)
    kv0 = Q_LORA_RANK
    kr0 = Q_LORA_RANK + KV_LORA_RANK

    def body(lat_ref, qw_ref, kw_ref, dqn_ref, dkvn_ref, dkr_ref, c_ref, s_ref, dlat_ref, dqw_ref, dkw_ref):
        @pl.when(pl.program_id(0) == 0)
        def _():
            dqw_ref[...] = jnp.zeros_like(dqw_ref)
            dkw_ref[...] = jnp.zeros_like(dkw_ref)

        dx, dg = _rms_bwd_math(lat_ref[:, :kv0], qw_ref[...], dqn_ref[...])
        dlat_ref[:, :kv0] = dx.astype(BF16)
        dqw_ref[...] += dg
        dx, dg = _rms_bwd_math(lat_ref[:, kv0:kr0], kw_ref[...], dkvn_ref[...])
        dlat_ref[:, kv0:kr0] = dx.astype(BF16)
        dkw_ref[...] += dg
        dlat_ref[:, kr0:] = _rope_t(dkr_ref[...], c_ref[...], s_ref[...]).astype(BF16)

    row = lambda w: pl.BlockSpec((tm, w), lambda i: (i, 0))
    vec = lambda w: pl.BlockSpec((1, w), lambda i: (0, 0))
    return _hbm_call(
        body, name=name, grid=(T // tm,),
        in_specs=[row(LAT_DIM), vec(Q_LORA_RANK), vec(KV_LORA_RANK), row(Q_LORA_RANK), row(KV_LORA_RANK),
                  row(LANES), row(LANES), row(LANES)],
        out_specs=[row(LAT_DIM), vec(Q_LORA_RANK), vec(KV_LORA_RANK)],
        out_shape=[jax.ShapeDtypeStruct((T, LAT_DIM), BF16), jax.ShapeDtypeStruct((1, Q_LORA_RANK), F32),
                   jax.ShapeDtypeStruct((1, KV_LORA_RANK), F32)],
        compiler_params=_cparams(("arbitrary",)),
    )(lat, qn_w.reshape(1, -1), kvn_w.reshape(1, -1), dqn, dkvn, dkr, cs, sn)


def attn_proj_gate(o, wap, gl, bg, ba, *, name):
    T, D2 = gl.shape
    D = D2 // 2
    tm = _pick(T, (512, 256, 128))

    def body(o_ref, w_ref, gl_ref, bg_ref, ba_ref, bb_ref, m_ref):
        bb = jnp.dot(o_ref[...], w_ref[...], preferred_element_type=F32)
        bb_ref[...] = bb.astype(BF16)
        ga = _sigmoid(gl_ref[:, :D].astype(F32) + bg_ref[:, :D])
        gb = _sigmoid(gl_ref[:, D:].astype(F32) + bg_ref[:, D:])
        m_ref[...] = (ga * ba_ref[...].astype(F32) + gb * bb).astype(BF16)

    row = lambda w: pl.BlockSpec((tm, w), lambda i: (i, 0))
    return _hbm_call(
        body, name=name, grid=(T // tm,),
        in_specs=[row(o.shape[1]), pl.BlockSpec(wap.shape, lambda i: (0, 0)), row(D2),
                  pl.BlockSpec((1, D2), lambda i: (0, 0)), row(D)],
        out_specs=[row(D), row(D)],
        out_shape=[jax.ShapeDtypeStruct((T, D), BF16), jax.ShapeDtypeStruct((T, D), BF16)],
        compiler_params=_cparams(("parallel",)),
    )(o, wap, gl, bg.reshape(1, D2), ba)


def out_dx_gate(dy, wo, gl, bg, ba, bb, *, name):
    T, D2 = gl.shape
    D = D2 // 2
    tm = _pick(T, (512, 256, 128))

    def body(dy_ref, w_ref, gl_ref, bg_ref, ba_ref, bb_ref, dba_ref, dbb_ref, dgl_ref, dbg_ref):
        @pl.when(pl.program_id(0) == 0)
        def _():
            dbg_ref[...] = jnp.zeros_like(dbg_ref)

        dmv = lax.dot_general(dy_ref[...].astype(BF16), w_ref[...], _NT, preferred_element_type=F32)
        ga = _sigmoid(gl_ref[:, :D].astype(F32) + bg_ref[:, :D])
        gb = _sigmoid(gl_ref[:, D:].astype(F32) + bg_ref[:, D:])
        dba_ref[...] = (dmv * ga).astype(BF16)
        dbb_ref[...] = (dmv * gb).astype(BF16)
        dla = dmv * ba_ref[...].astype(F32) * ga * (1.0 - ga)
        dlb = dmv * bb_ref[...].astype(F32) * gb * (1.0 - gb)
        dgl_ref[:, :D] = dla.astype(BF16)
        dgl_ref[:, D:] = dlb.astype(BF16)
        dbg_ref[:, :D] += jnp.sum(dla, axis=0, keepdims=True)
        dbg_ref[:, D:] += jnp.sum(dlb, axis=0, keepdims=True)

    row = lambda w: pl.BlockSpec((tm, w), lambda i: (i, 0))
    vec = pl.BlockSpec((1, D2), lambda i: (0, 0))
    return _hbm_call(
        body, name=name, grid=(T // tm,),
        in_specs=[row(D), pl.BlockSpec(wo.shape, lambda i: (0, 0)), row(D2), vec, row(D), row(D)],
        out_specs=[row(D), row(D), row(D2), vec],
        out_shape=[jax.ShapeDtypeStruct((T, D), BF16), jax.ShapeDtypeStruct((T, D), BF16),
                   jax.ShapeDtypeStruct((T, D2), BF16), jax.ShapeDtypeStruct((1, D2), F32)],
        compiler_params=_cparams(("arbitrary",)),
    )(dy, wo, gl, bg.reshape(1, D2), ba, bb)


def loss_head(x, gf, tgt, *, name):
    T, D = x.shape
    tm = _rows(T, D * 4)

    def body(x_ref, g_ref, t_ref, dx_ref, dg_ref, loss_ref):
        @pl.when(pl.program_id(0) == 0)
        def _():
            dg_ref[...] = jnp.zeros_like(dg_ref)
            loss_ref[...] = jnp.zeros_like(loss_ref)

        xv = x_ref[...]
        gv = g_ref[...]
        r = lax.rsqrt(jnp.mean(xv * xv, axis=-1, keepdims=True) + NORM_EPS)
        xn = xv * r
        err = xn * gv - t_ref[...]
        loss_ref[...] += 0.5 * jnp.sum(jnp.mean(err * err, axis=-1, keepdims=True))
        dy = err * (1.0 / D)
        dg_ref[...] += jnp.sum(dy * xn, axis=0, keepdims=True)
        dxn = dy * gv
        dx_ref[...] = r * (dxn - xn * jnp.mean(dxn * xn, axis=-1, keepdims=True))

    row = pl.BlockSpec((tm, D), lambda i: (i, 0))
    vec = pl.BlockSpec((1, D), lambda i: (0, 0))
    return _hbm_call(
        body, name=name, grid=(T // tm,), in_specs=[row, vec, row],
        out_specs=[row, vec, pl.BlockSpec((8, LANES), lambda i: (0, 0))],
        out_shape=[jax.ShapeDtypeStruct((T, D), F32), jax.ShapeDtypeStruct((1, D), F32),
                   jax.ShapeDtypeStruct((8, LANES), F32)],
        compiler_params=_cparams(("arbitrary",)),
    )(x, gf.reshape(1, D), tgt)


def _shift_rows(s, k, down):
    n = s.shape[0]
    t = lax.broadcasted_iota(jnp.int32, s.shape, 0)
    if down:
        return jnp.where(t >= k, pltpu.roll(s, k, 0), 0.0)
    return jnp.where(t < n - k, pltpu.roll(s, n - k, 0), 0.0)


def _window_sum(s, w, down):
    k = 1
    while k < w:
        s = s + _shift_rows(s, k, down)
        k *= 2
    return s


def _pool_count(shape, w):
    t = lax.broadcasted_iota(jnp.int32, shape, 0)
    return jnp.minimum(t + 1, w).astype(F32)


def pool_fwd(xp, maps, scale, B, *, name):
    T, P = xp.shape
    S = T // B
    G = POOL_GROUP_DIM

    def body(x_ref, m_ref, sc_ref, o_ref):
        for g, w in enumerate(POOL_WINDOWS):
            xg = x_ref[:, g * G:(g + 1) * G]
            pooled = _window_sum(xg, w, True) / _pool_count(xg.shape, w) - xg
            mixed = jnp.dot(pooled.astype(BF16), m_ref[g], preferred_element_type=F32)
            o_ref[:, g * G:(g + 1) * G] = (mixed * sc_ref[:, g * G:(g + 1) * G]).astype(BF16)

    return _hbm_call(
        body, name=name, grid=(B,),
        in_specs=[pl.BlockSpec((S, P), lambda b: (b, 0)), pl.BlockSpec((N_POOL_GROUPS, G, G), lambda b: (0, 0, 0)),
                  pl.BlockSpec((1, P), lambda b: (0, 0))],
        out_specs=pl.BlockSpec((S, P), lambda b: (b, 0)),
        out_shape=jax.ShapeDtypeStruct((T, P), BF16),
        compiler_params=_cparams(("parallel",)),
    )(xp, maps, scale.reshape(1, P))


def pool_bwd(xp, dmixed, maps, scale, B, *, name):
    T, P = xp.shape
    S = T // B
    G = POOL_GROUP_DIM

    def body(x_ref, dm_ref, m_ref, sc_ref, dx_ref, dmaps_ref, dsc_ref):
        @pl.when(pl.program_id(0) == 0)
        def _():
            dmaps_ref[...] = jnp.zeros_like(dmaps_ref)
            dsc_ref[...] = jnp.zeros_like(dsc_ref)

        for g, w in enumerate(POOL_WINDOWS):
            cols = slice(g * G, (g + 1) * G)
            xg = x_ref[:, cols]
            cnt = _pool_count(xg.shape, w)
            pooled = (_window_sum(xg, w, True) / cnt - xg).astype(BF16)
            mixed = jnp.dot(pooled, m_ref[g], preferred_element_type=F32)
            dmx = dm_ref[:, cols]
            dsc_ref[:, cols] += jnp.sum(dmx * mixed, axis=0, keepdims=True)
            dmp = (dmx * sc_ref[:, cols]).astype(BF16)
            dmaps_ref[g] += lax.dot_general(pooled, dmp, (((0,), (0,)), ((), ())), preferred_element_type=F32)
            dpooled = lax.dot_general(dmp, m_ref[g], (((1,), (1,)), ((), ())), preferred_element_type=F32)
            dx_ref[:, cols] = (_window_sum(dpooled / cnt, w, False) - dpooled).astype(BF16)

    blk = pl.BlockSpec((S, P), lambda b: (b, 0))
    mp = pl.BlockSpec((N_POOL_GROUPS, G, G), lambda b: (0, 0, 0))
    vec = pl.BlockSpec((1, P), lambda b: (0, 0))
    return _hbm_call(
        body, name=name, grid=(B,), in_specs=[blk, blk, mp, vec], out_specs=[blk, mp, vec],
        out_shape=[jax.ShapeDtypeStruct((T, P), BF16), jax.ShapeDtypeStruct((N_POOL_GROUPS, G, G), F32),
                   jax.ShapeDtypeStruct((1, P), F32)],
        compiler_params=_cparams(("arbitrary",)),
    )(xp, dmixed, maps, scale.reshape(1, P))


def _keys(kv_ref, kr_ref, rows):
    return jnp.concatenate([kv_ref[rows, :QK_NOPE_DIM], kr_ref[rows, :]], axis=1)


def _causal(s):
    row = lax.broadcasted_iota(jnp.int32, s.shape, 0)
    col = lax.broadcasted_iota(jnp.int32, s.shape, 1)
    return jnp.where(row >= col, s, NEG_INF)


def attn_fwd(q, kv, kr, cs, sn, B, *, name):
    T = q.shape[0]
    S = T // B
    blk = min(ATT_BLOCK, S)
    nb = S // blk
    H = N_HEADS
    scale = QK_DIM ** -0.5

    def body(q_ref, kv_ref, kr_ref, c_ref, s_ref, o_ref, lse_ref, qs_ref):
        qs_ref[:, :QK_NOPE_DIM] = (q_ref[:, :QK_NOPE_DIM] * scale).astype(BF16)
        qs_ref[:, QK_NOPE_DIM:] = _rope(q_ref[:, QK_NOPE_DIM:], c_ref[...] * scale, s_ref[...] * scale).astype(BF16)
        for qi in range(nb):
            rows = slice(qi * blk, (qi + 1) * blk)
            qb = qs_ref[rows, :]
            sd = _causal(lax.dot_general(qb, _keys(kv_ref, kr_ref, rows), _NT, preferred_element_type=F32))
            m = jnp.max(sd, axis=-1, keepdims=True)
            if qi > 0:
                prev = slice(0, qi * blk)
                sp = lax.dot_general(qb, _keys(kv_ref, kr_ref, prev), _NT, preferred_element_type=F32)
                m = jnp.maximum(m, jnp.max(sp, axis=-1, keepdims=True))
            pd = jnp.exp(sd - m)
            l = jnp.sum(pd, axis=-1, keepdims=True)
            acc = jnp.dot(pd.astype(BF16), kv_ref[rows, QK_NOPE_DIM:], preferred_element_type=F32)
            if qi > 0:
                pp = jnp.exp(sp - m)
                l = l + jnp.sum(pp, axis=-1, keepdims=True)
                acc = acc + jnp.dot(pp.astype(BF16), kv_ref[prev, QK_NOPE_DIM:], preferred_element_type=F32)
            o_ref[rows, :] = (acc / l).astype(BF16)
            lse_ref[0, rows, :] = m + jnp.log(l)

    head = pl.BlockSpec((S, HEAD_PAD), lambda b, h: (b, h))
    shared = pl.BlockSpec((S, LANES), lambda b, h: (b, 0))
    return _hbm_call(
        body, name=name, grid=(B, H),
        in_specs=[head, head, shared, shared, shared],
        out_specs=[pl.BlockSpec((S, V_HEAD_DIM), lambda b, h: (b, h)), pl.BlockSpec((1, S, 1), lambda b, h: (h, b, 0)),
                   head],
        out_shape=[jax.ShapeDtypeStruct((T, H * V_HEAD_DIM), BF16), jax.ShapeDtypeStruct((H, T, 1), F32),
                   jax.ShapeDtypeStruct((T, H * HEAD_PAD), BF16)],
        compiler_params=_cparams(("parallel", "parallel")),
    )(q, kv, kr, cs, sn)


def attn_bwd(q, kv, kr, o, do, lse, cs, sn, B, *, name):
    T = q.shape[0]
    S = T // B
    blk = min(ATT_BLOCK, S)
    nb = S // blk
    H = N_HEADS
    scale = QK_DIM ** -0.5

    def body(q_ref, kv_ref, kr_ref, o_ref, do_ref, lse_ref, c_ref, s_ref, dq_ref, dkv_ref, dkr_ref, dk_s, dv_s):
        dk_s[...] = jnp.zeros_like(dk_s)
        dv_s[...] = jnp.zeros_like(dv_s)

        @pl.when(pl.program_id(1) == 0)
        def _():
            dkr_ref[...] = jnp.zeros_like(dkr_ref)

        for qi in range(nb):
            rows = slice(qi * blk, (qi + 1) * blk)
            qb = q_ref[rows, :]
            dob = do_ref[rows, :]
            delta = jnp.sum(dob.astype(F32) * o_ref[rows, :].astype(F32), axis=-1, keepdims=True)
            lse_b = lse_ref[0, rows, :]

            def part(ks, diagonal):
                k = _keys(kv_ref, kr_ref, ks)
                s = lax.dot_general(qb, k, _NT, preferred_element_type=F32)
                if diagonal:
                    s = _causal(s)
                p = jnp.exp(s - lse_b)
                dp = lax.dot_general(dob, kv_ref[ks, QK_NOPE_DIM:], _NT, preferred_element_type=F32)
                ds = (p * (dp - delta)).astype(BF16)
                dv_s[ks, :] += lax.dot_general(p.astype(BF16), dob, _TN, preferred_element_type=F32)
                dk_s[ks, :] += lax.dot_general(ds, qb, _TN, preferred_element_type=F32)
                return jnp.dot(ds, k, preferred_element_type=F32)

            dq = part(rows, True)
            if qi > 0:
                dq = dq + part(slice(0, qi * blk), False)
            dq_ref[rows, :QK_NOPE_DIM] = (dq[:, :QK_NOPE_DIM] * scale).astype(BF16)
            dq_ref[rows, QK_NOPE_DIM:] = _rope_t(dq[:, QK_NOPE_DIM:], c_ref[rows, :] * scale,
                                                 s_ref[rows, :] * scale).astype(BF16)

        dkv_ref[:, :QK_NOPE_DIM] = dk_s[:, :QK_NOPE_DIM].astype(BF16)
        dkv_ref[:, QK_NOPE_DIM:] = dv_s[...].astype(BF16)
        dkr_ref[...] += dk_s[:, QK_NOPE_DIM:]

    head = lambda w: pl.BlockSpec((S, w), lambda b, h: (b, h))
    shared = pl.BlockSpec((S, LANES), lambda b, h: (b, 0))
    return _hbm_call(
        body, name=name, grid=(B, H),
        in_specs=[head(HEAD_PAD), head(HEAD_PAD), shared, head(V_HEAD_DIM), head(V_HEAD_DIM),
                  pl.BlockSpec((1, S, 1), lambda b, h: (h, b, 0)), shared, shared],
        out_specs=[head(HEAD_PAD), head(HEAD_PAD), shared],
        out_shape=[jax.ShapeDtypeStruct((T, H * HEAD_PAD), BF16), jax.ShapeDtypeStruct((T, H * HEAD_PAD), BF16),
                   jax.ShapeDtypeStruct((T, LANES), F32)],
        scratch_shapes=[pltpu.VMEM((S, HEAD_PAD), F32), pltpu.VMEM((S, V_HEAD_DIM), F32)],
        compiler_params=_cparams(("parallel", "arbitrary")),
    )(q, kv, kr, o, do, lse, cs, sn)


def adamw(w, g, m, v, *, name, dep=None, copy_g=False):
    R, C = w.shape
    cap = max(8, (1024 * 1024) // (C * 4))
    tr = _pick(R, tuple(c for c in (1024, 512, 256, 128, 64, 32, 16, 8) if c <= cap))
    c1 = 1.0 - ADAM_B1 ** ADAM_STEP
    c2 = 1.0 - ADAM_B2 ** ADAM_STEP
    nout = 4 if copy_g else 3

    def body(w_ref, g_ref, m_ref, v_ref, *rest):
        d_ref, nm_ref, nv_ref = rest[-3:]
        gv = g_ref[...]
        if copy_g:
            rest[-4][...] = gv
        mv = ADAM_B1 * m_ref[...] + (1.0 - ADAM_B1) * gv
        vv = ADAM_B2 * v_ref[...] + (1.0 - ADAM_B2) * (gv * gv)
        nm_ref[...] = mv
        nv_ref[...] = vv
        d_ref[...] = -ADAM_LR * ((mv / c1) / (jnp.sqrt(vv / c2) + ADAM_EPS) + ADAM_WD * w_ref[...])

    blk = pl.BlockSpec((tr, C), lambda i: (i, 0))
    sh = jax.ShapeDtypeStruct((R, C), F32)
    extra = [] if dep is None else [dep]
    return _hbm_call(
        body, name=name, grid=(R // tr,), in_specs=[blk] * 4 + [ANY] * len(extra), out_specs=[blk] * nout,
        out_shape=[sh] * nout, compiler_params=_cparams(("parallel",)),
    )(w, g, m, v, *extra)


ANY = pl.BlockSpec(memory_space=pl.ANY)


def _place():
    x, y, c = lax.axis_index("x"), lax.axis_index("y"), lax.axis_index("c")
    others = [(1 - x, y), (x, 1 - y), (1 - x, 1 - y)]
    return x, y, c, others


def _remote(src, dst, ssem, rsem, dev):
    return pltpu.make_async_remote_copy(src_ref=src, dst_ref=dst, send_sem=ssem, recv_sem=rsem,
                                        device_id=dev, device_id_type=MESH)


def _half(ref_rows, c):
    hr = ref_rows // 2
    return pl.ds(pl.multiple_of(c * hr, 16), hr)


HBM = pl.BlockSpec(memory_space=pltpu.HBM)
SEMS = pl.BlockSpec(memory_space=pltpu.SEMAPHORE)
EFFECT = pltpu.SideEffectType.DATAFLOW_SIDE_EFFECTING


def exchange_begin(name, srcs, land_shapes, plan, ncopies, after=None):
    ns, nl = len(srcs), len(land_shapes)
    nin = ns + nl + (0 if after is None else 1)

    def body(*refs):
        ssem, rsem = refs[nin], refs[nin + 1]
        for k, (s, d, dev) in enumerate(plan(refs[:ns], refs[ns:ns + nl])):
            _remote(s, d, ssem.at[k], rsem.at[k], dev).start()
        refs[-1][...] = jnp.zeros_like(refs[-1])

    bufs = [pltpu.HBM(s.shape, s.dtype) for s in srcs] + [pltpu.HBM(s.shape, s.dtype) for s in land_shapes]
    args = [pltpu.with_memory_space_constraint(s, pltpu.HBM) for s in srcs]
    args += [pltpu.with_memory_space_constraint(lax.empty(s.shape, s.dtype), pltpu.HBM) for s in land_shapes]
    if after is not None:
        args.append(after)
    out = pl.pallas_call(
        body, name=name,
        out_shape=(pltpu.SemaphoreType.DMA((ncopies,)), pltpu.SemaphoreType.DMA((ncopies,)), *bufs,
                   jax.ShapeDtypeStruct((8, LANES), F32)),
        in_specs=[HBM] * (ns + nl) + ([] if after is None else [ANY]),
        out_specs=(SEMS, SEMS, *([HBM] * (ns + nl)), pl.BlockSpec(memory_space=pltpu.VMEM)),
        input_output_aliases={i: 2 + i for i in range(ns + nl)},
        compiler_params=pltpu.CompilerParams(has_side_effects=EFFECT),
    )(*args)
    return (out[0], out[1], out[2:2 + ns], out[2 + ns:2 + ns + nl]), out[-1]


def exchange_end(name, handle, plan, after):
    ssem, rsem, srcs, lands = handle
    ns, nl = len(srcs), len(lands)

    def body(*refs):
        ssem_ref, rsem_ref = refs[ns + nl], refs[ns + nl + 1]
        for k, (s, d, dev) in enumerate(plan(refs[:ns], refs[ns:ns + nl])):
            cp = _remote(s, d, ssem_ref.at[k], rsem_ref.at[k], dev)
            cp.wait_send()
            cp.wait_recv()

    out = pl.pallas_call(
        body, name=name,
        out_shape=tuple(pltpu.HBM(s.shape, s.dtype) for s in (*srcs, *lands)),
        in_specs=[HBM] * (ns + nl) + [SEMS, SEMS, ANY], out_specs=tuple([HBM] * (ns + nl)),
        input_output_aliases={i: i for i in range(ns + nl)},
        compiler_params=pltpu.CompilerParams(has_side_effects=EFFECT),
    )(*srcs, *lands, ssem, rsem, after)
    return list(out[:ns]), list(out[ns:])


def ag_plan(src_refs, land_refs):
    x, y, c, others = _place()
    plan = []
    for s, d in zip(src_refs, land_refs):
        mine = _half(s.shape[0], c)
        for ox, oy in others:
            plan.append((s.at[mine, :], d.at[2 * x + y, mine, :], (ox, oy, c)))
    return plan


def ag_forward_plan(src_refs, land_refs):
    x, y, c, others = _place()
    plan = []
    for s in src_refs:
        mine = _half(s.shape[1], c)
        for ox, oy in others:
            blk = s.at[2 * ox + oy, mine, :]
            plan.append((blk, blk, (x, y, 1 - c)))
    return plan


def rs_swap_plan(src_refs, land_refs):
    x, y, c, _ = _place()
    return [(s.at[:, _half(s.shape[1], 1 - c), :], d, (x, y, 1 - c)) for s, d in zip(src_refs, land_refs)]


def ag_forward(lands, *, name):
    n = len(lands)

    def body(*refs):
        ins, outs = refs[:n], refs[n:2 * n]
        ssem, rsem = refs[2 * n:]
        x, y, c, others = _place()
        sent = []
        for i in range(n):
            mine = _half(ins[i].shape[1], c)
            for j, (ox, oy) in enumerate(others):
                cp = _remote(ins[i].at[2 * ox + oy, mine, :], outs[i].at[2 * ox + oy, mine, :], ssem.at[3 * i + j],
                             rsem.at[3 * i + j], (x, y, 1 - c))
                cp.start()
                sent.append(cp)
        for cp in sent:
            cp.wait()

    return _hbm_call(
        body, name=name, in_specs=[ANY] * n, out_specs=[ANY] * n,
        out_shape=[jax.ShapeDtypeStruct(a.shape, a.dtype) for a in lands],
        input_output_aliases={i: i for i in range(n)},
        scratch_shapes=[pltpu.SemaphoreType.DMA((3 * n,)), pltpu.SemaphoreType.DMA((3 * n,))],
        compiler_params=pltpu.CompilerParams(has_side_effects=True),
    )(*lands)


def place_own(lands, own, chip, *, name):
    n = len(lands)
    steps = 4

    def body(chip_ref, *refs):
        for i in range(n):
            refs[2 * n + i][0] = refs[i][...]

    in_specs = [pl.BlockSpec((o.shape[0] // steps, o.shape[1]), lambda t, q: (t, 0)) for o in own] + [ANY] * n
    out_specs = [pl.BlockSpec((1, o.shape[0] // steps, o.shape[1]), lambda t, q: (q[0], t, 0)) for o in own]
    return _hbm_call(
        body, name=name,
        grid_spec=pltpu.PrefetchScalarGridSpec(num_scalar_prefetch=1, grid=(steps,), in_specs=in_specs,
                                               out_specs=out_specs),
        out_shape=[jax.ShapeDtypeStruct(a.shape, a.dtype) for a in lands],
        input_output_aliases={1 + n + i: i for i in range(n)},
        compiler_params=_cparams(("parallel",)),
    )(chip, *own, *lands)


def rs_swap_halves(grads, *, name):
    n = len(grads)

    def body(*refs):
        ins, outs = refs[:n], refs[n:2 * n]
        ssem, rsem = refs[2 * n:]
        x, y, c, _ = _place()
        cps = []
        for i in range(n):
            theirs = _half(ins[i].shape[1], 1 - c)
            cp = _remote(ins[i].at[:, theirs, :], outs[i], ssem.at[i], rsem.at[i], (x, y, 1 - c))
            cp.start()
            cps.append(cp)
        for cp in cps:
            cp.wait()

    return _hbm_call(
        body, name=name, in_specs=[ANY] * n, out_specs=[ANY] * n,
        out_shape=[jax.ShapeDtypeStruct((4, g.shape[1] // 2, g.shape[2]), g.dtype) for g in grads],
        scratch_shapes=[pltpu.SemaphoreType.DMA((n,)), pltpu.SemaphoreType.DMA((n,))],
        compiler_params=pltpu.CompilerParams(has_side_effects=True),
    )(*grads)


def rs_chip_sum(g, r1, core, *, name):
    _, r, cdim = g.shape
    hr = r // 2

    def body(c_ref, g_ref, r1_ref, o_ref):
        o_ref[...] = (g_ref[...].astype(F32) + r1_ref[...].astype(F32)).astype(BF16)

    return _hbm_call(
        body, name=name,
        grid_spec=pltpu.PrefetchScalarGridSpec(
            num_scalar_prefetch=1, grid=(4,),
            in_specs=[pl.BlockSpec((1, hr, cdim), lambda qq, c_ref: (qq, c_ref[0], 0)),
                      pl.BlockSpec((1, hr, cdim), lambda qq, c_ref: (qq, 0, 0))],
            out_specs=pl.BlockSpec((1, hr, cdim), lambda qq, c_ref: (qq, 0, 0))),
        out_shape=jax.ShapeDtypeStruct((4, hr, cdim), BF16),
        compiler_params=_cparams(("parallel",)),
    )(core, g, r1)


def rs_plan(src_refs, land_refs):
    x, y, c, others = _place()
    plan = []
    for s, d in zip(src_refs, land_refs):
        for j, (ox, oy) in enumerate(others):
            plan.append((s.at[2 * ox + oy], d.at[j], (ox, oy, c)))
    return plan


def rs_final_sum(g, r1, r2, place, acc, l, *, name):
    _, r, cdim = g.shape
    hr = r // 2
    ch = hr // 2

    def body(p_ref, g_ref, r1_ref, a_ref, b_ref, d_ref, acc_in, o_ref):
        s = g_ref[...].astype(F32) + r1_ref[...].astype(F32)
        s = s + a_ref[...].astype(F32)
        s = s + b_ref[...].astype(F32)
        o_ref[...] = s + d_ref[...].astype(F32)

    other = lambda j: pl.BlockSpec((1, ch, cdim), lambda t, p_ref: (j, t, 0))
    return _hbm_call(
        body, name=name,
        grid_spec=pltpu.PrefetchScalarGridSpec(
            num_scalar_prefetch=1, grid=(2,),
            in_specs=[pl.BlockSpec((1, ch, cdim), lambda t, p_ref: (p_ref[0], 2 * p_ref[1] + t, 0)),
                      pl.BlockSpec((1, ch, cdim), lambda t, p_ref: (p_ref[0], t, 0)),
                      other(0), other(1), other(2), ANY],
            out_specs=pl.BlockSpec((1, ch, cdim), lambda t, p_ref: (l, 2 * p_ref[1] + t, 0))),
        out_shape=jax.ShapeDtypeStruct(acc.shape, F32),
        input_output_aliases={6: 0},
        compiler_params=_cparams(("parallel",)),
    )(place, g, r1, r2, r2, r2, acc)


def rs_join_halves(grads, *, name):
    n = len(grads)

    def body(*refs):
        ins, outs = refs[:n], refs[n:2 * n]
        ssem, rsem = refs[2 * n:]
        x, y, c, _ = _place()
        cps = []
        for i in range(n):
            mine = _half(ins[i].shape[1], c)
            cp = _remote(ins[i].at[:, mine, :], outs[i].at[:, mine, :], ssem.at[i], rsem.at[i], (x, y, 1 - c))
            cp.start()
            cps.append(cp)
        for cp in cps:
            cp.wait()

    return _hbm_call(
        body, name=name, in_specs=[ANY] * n, out_specs=[ANY] * n,
        out_shape=[jax.ShapeDtypeStruct(g.shape, g.dtype) for g in grads],
        input_output_aliases={i: i for i in range(n)},
        scratch_shapes=[pltpu.SemaphoreType.DMA((n,)), pltpu.SemaphoreType.DMA((n,))],
        compiler_params=pltpu.CompilerParams(has_side_effects=True),
    )(*grads)


def all_reduce_small(v):
    R = v.shape[0]
    rs = R // 8

    def body(v_ref, o_ref, buf, ssem1, rsem1, ssem2, rsem2):
        x, y, c, _ = _place()
        me = 4 * x + 2 * y + c
        mine = pl.ds(pl.multiple_of(me * rs, 8), rs)
        peers = []
        for k in range(1, 8):
            px = jnp.where((k >> 2) & 1 == 1, 1 - x, x)
            py = jnp.where((k >> 1) & 1 == 1, 1 - y, y)
            pc = jnp.where(k & 1 == 1, 1 - c, c)
            peers.append((px, py, pc))
        buf[me] = v_ref[mine, :]
        cps = []
        for k, (px, py, pc) in enumerate(peers):
            theirs = pl.ds(pl.multiple_of((4 * px + 2 * py + pc) * rs, 8), rs)
            cp = _remote(v_ref.at[theirs, :], buf.at[me], ssem1.at[k], rsem1.at[k], (px, py, pc))
            cp.start()
            cps.append(cp)
        for cp in cps:
            cp.wait()
        acc = buf[0]
        for d in range(1, 8):
            acc = acc + buf[d]
        o_ref[mine, :] = acc
        cps = []
        for k, peer in enumerate(peers):
            cp = _remote(o_ref.at[mine, :], o_ref.at[mine, :], ssem2.at[k], rsem2.at[k], peer)
            cp.start()
            cps.append(cp)
        for cp in cps:
            cp.wait()

    vm = pl.BlockSpec(memory_space=pltpu.VMEM)
    return pl.pallas_call(
        body, name="all_reduce_small", in_specs=[vm], out_specs=vm,
        out_shape=jax.ShapeDtypeStruct((R, LANES), F32),
        scratch_shapes=[pltpu.VMEM((8, rs, LANES), F32)] + [pltpu.SemaphoreType.DMA((7,))] * 4,
        compiler_params=pltpu.CompilerParams(vmem_limit_bytes=VMEM_LIMIT, has_side_effects=True),
    )(v)


def _to_stacked(name, full):
    R, C = full.shape
    if name in ROW_SHARDED:
        return full.reshape(4, R // 4, C)
    return jnp.transpose(full.reshape(R, 4, C // 4), (1, 0, 2))


def _from_stacked(name, st):
    _, r, c = st.shape
    if name in ROW_SHARDED:
        return st.reshape(4 * r, c)
    return jnp.transpose(st, (1, 0, 2)).reshape(r, 4 * c)


UP_PIECES = ("ffn1_up", "ffn2_up")


def _layer_weights(lands):
    w = {n: lands[n] if n in UP_PIECES else _from_stacked(n, lands[n]) for n in lands}
    if "w_in" not in w:
        return w
    win = w.pop("w_in")
    D = win.shape[0]
    p0, p1, p2, p3 = POOL_DIM, POOL_DIM + Q_LORA_RANK, POOL_DIM + Q_LORA_RANK + KV_LORA_RANK, \
        POOL_DIM + Q_LORA_RANK + KV_LORA_RANK + QK_ROPE_DIM
    w["w_pool"] = win[:, :p0]
    w["w_lat"] = jnp.concatenate([win[:, p0:p3], jnp.zeros((D, LAT_DIM - (p3 - p0)), win.dtype)], axis=1)
    w["w_gate"] = win[:, p3:]
    uq = w["w_uq"].reshape(Q_LORA_RANK, N_HEADS, QK_DIM)
    w["w_uq"] = jnp.concatenate([uq, jnp.zeros((Q_LORA_RANK, N_HEADS, HEAD_PAD - QK_DIM), uq.dtype)],
                                axis=2).reshape(Q_LORA_RANK, N_HEADS * HEAD_PAD)
    return w


def _layer_grads_stacked(dw):
    dw = dict(dw)
    if "w_lat" in dw:
        lat = dw.pop("w_lat")
        dw["w_in"] = jnp.concatenate([dw.pop("w_pool"), lat[:, :Q_LORA_RANK + KV_LORA_RANK + QK_ROPE_DIM],
                                      dw.pop("w_gate")], axis=1)
        dw["w_uq"] = dw["w_uq"].reshape(Q_LORA_RANK, N_HEADS, HEAD_PAD)[:, :, :QK_DIM].reshape(Q_LORA_RANK,
                                                                                                 N_HEADS * QK_DIM)
    return {n: dw[n] if n in UP_PIECES else _to_stacked(n, dw[n]) for n in BIG if n in dw}


def _rope_tables(positions):
    inv_freq = ROPE_THETA ** (-jnp.arange(0, QK_ROPE_DIM, 2, dtype=F32) / QK_ROPE_DIM)
    ang = positions.astype(F32).reshape(-1)[:, None] * inv_freq
    cos, sin = jnp.cos(ang), jnp.sin(ang)
    z = jnp.zeros((ang.shape[0], LANES - QK_ROPE_DIM), F32)
    return jnp.concatenate([cos, cos, z], axis=1), jnp.concatenate([-sin, sin, z], axis=1)


def _pack_small(vals):
    parts, total = [], 0
    for n in SMALL:
        f = vals[n].reshape(-1).astype(F32)
        pad = (-f.shape[0]) % (8 * LANES)
        parts.append(jnp.pad(f, (0, pad)))
        total += f.shape[0] + pad
    parts.append(jnp.zeros(((-total) % (64 * LANES),), F32))
    return jnp.concatenate(parts).reshape(-1, LANES)


def _unpack_small(packed, like):
    flat = packed.reshape(-1)
    out, off = {}, 0
    for n in SMALL:
        size = like[n].size
        out[n] = flat[off:off + size].reshape(like[n].shape)
        off += size + ((-size) % (8 * LANES))
    return out


def _ffn_fwd(x, g, wu4, wd, tag, dep=None):
    h = rms_fwd(x, g, dep=dep, name=f"{tag}_norm")
    gate, up, a = ffn_up_act(h, wu4, name=f"{tag}_up_act")
    y = mm(a, wd, res=x, alpha=0.5, name=f"{tag}_down")
    return y, (x, h, gate, up, a)


def _ffn_bwd(dy, saved, g, wu4, wd, tag, dep=None):
    x, h, gate, up, a = saved
    dgate, dup = ffn_down_dx_act(dy, wd, gate, up, dep=dep, name=f"{tag}_down_dx_act")
    dwd = mm(a, dy, ta=True, alpha=0.5, out_dtype=BF16, name=f"{tag}_down_dw")
    dwu4 = ffn_up_dw(h, dgate, dup, name=f"{tag}_up_dw")
    dx, dg = ffn_up_dx_norm(dgate, dup, wu4, x, g, dy, name=f"{tag}_up_dx_norm")
    return dx, dg, dwu4, dwd


def _mix_fwd(x, p, w, cs, sn, B, dep=None):
    h = rms_fwd(x, p["norm_mix"], dep=dep, name="mix_norm")
    lat, xp, gl = mix_in(h, w["w_lat"], w["w_pool"], w["w_gate"], name="mix_in")
    mixed = pool_fwd(xp, p["pool_maps"].astype(BF16), p["pool_scale"], B, name="pool_fwd")
    ba = mm(mixed, w["w_pool_proj"], out_dtype=BF16, name="mix_pool_proj")
    qn, kvn, kr = lat_fwd(lat, p["q_latent_norm"], p["kv_latent_norm"], cs, sn, name="lat_fwd")
    kv = mm(kvn, w["w_ukv"], out_dtype=BF16, name="mix_ukv")
    o, lse, q = attn_fwd(mm(qn, w["w_uq"], name="mix_uq"), kv, kr, cs, sn, B, name="attn_fwd")
    bb, merged = attn_proj_gate(o, w["w_attn_proj"], gl, p["b_gate"], ba, name="mix_attn_proj_gate")
    y = mm(merged, w["w_out"], res=x, name="mix_out")
    return y, (x, h, lat, xp, gl, mixed, ba, qn, kvn, kr, q, kv, o, lse, bb, merged)


def _mix_bwd(dy, saved, p, w, cs, sn, B, dep=None):
    x, h, lat, xp, gl, mixed, ba, qn, kvn, kr, q, kv, o, lse, bb, merged = saved
    dw, ds = {}, {}
    dw["w_out"] = mm(merged, dy, ta=True, out_dtype=BF16, dep=dep, name="mix_out_dw")
    dba, dbb, dgl, ds["b_gate"] = out_dx_gate(dy, w["w_out"], gl, p["b_gate"], ba, bb, name="mix_out_dx_gate")
    dw["w_attn_proj"] = mm(o, dbb, ta=True, out_dtype=BF16, name="mix_attn_proj_dw")
    do = mm(dbb, w["w_attn_proj"], tb=True, out_dtype=BF16, name="mix_attn_proj_dx")
    dw["w_pool_proj"] = mm(mixed, dba, ta=True, out_dtype=BF16, name="mix_pool_proj_dw")
    dmixed = mm(dba, w["w_pool_proj"], tb=True, name="mix_pool_proj_dx")
    dxp, ds["pool_maps"], ds["pool_scale"] = pool_bwd(xp, dmixed, p["pool_maps"].astype(BF16), p["pool_scale"], B,
                                                      name="pool_bwd")
    dqb, dkv, dkr = attn_bwd(q, kv, kr, o, do, lse, cs, sn, B, name="attn_bwd")
    dw["w_ukv"] = mm(kvn, dkv, ta=True, out_dtype=BF16, name="mix_ukv_dw")
    dkvn = mm(dkv, w["w_ukv"], tb=True, name="mix_ukv_dx")
    dw["w_uq"] = mm(qn, dqb, ta=True, out_dtype=BF16, name="mix_uq_dw")
    dqn = mm(dqb, w["w_uq"], tb=True, name="mix_uq_dx")
    dlat, ds["q_latent_norm"], ds["kv_latent_norm"] = lat_bwd(lat, p["q_latent_norm"], p["kv_latent_norm"], dqn, dkvn,
                                                               dkr, cs, sn, name="lat_bwd")
    dw["w_lat"] = mm(h, dlat, ta=True, out_dtype=BF16, name="mix_lat_dw")
    dw["w_pool"] = mm(h, dxp, ta=True, out_dtype=BF16, name="mix_pool_in_dw")
    dw["w_gate"] = mm(h, dgl, ta=True, out_dtype=BF16, name="mix_gate_in_dw")
    dx, ds["norm_mix"] = mix_in_dx_norm(dlat, dxp, dgl, w["w_lat"], w["w_pool"], w["w_gate"], x, p["norm_mix"], dy,
                                        name="mix_in_dx_norm")
    return dx, dw, ds


def kernel(x, positions, norm_ffn1, ffn1_up, ffn1_down, norm_mix, w_in, b_gate, pool_maps, pool_scale, w_pool_proj, q_latent_norm, w_uq, kv_latent_norm, w_ukv, w_attn_proj, w_out, norm_ffn2, ffn2_up, ffn2_down, final_norm, loss_target, m_norm_ffn1, m_ffn1_up, m_ffn1_down, m_norm_mix, m_w_in, m_b_gate, m_pool_maps, m_pool_scale, m_w_pool_proj, m_q_latent_norm, m_w_uq, m_kv_latent_norm, m_w_ukv, m_w_attn_proj, m_w_out, m_norm_ffn2, m_ffn2_up, m_ffn2_down, m_final_norm, v_norm_ffn1, v_ffn1_up, v_ffn1_down, v_norm_mix, v_w_in, v_b_gate, v_pool_maps, v_pool_scale, v_w_pool_proj, v_q_latent_norm, v_w_uq, v_kv_latent_norm, v_w_ukv, v_w_attn_proj, v_w_out, v_norm_ffn2, v_ffn2_up, v_ffn2_down, v_final_norm):
    given = dict(locals())
    B, S, D = x.shape
    T = B * S
    L = norm_ffn1.shape[0]
    W = {n: given[n] for n in WEIGHTS}
    Mo = {n: given["m_" + n] for n in WEIGHTS}
    Vo = {n: given["v_" + n] for n in WEIGHTS}
    core = lax.axis_index("c").astype(jnp.int32)
    chip = (2 * lax.axis_index("x") + lax.axis_index("y")).astype(jnp.int32)

    core_arr = core.reshape(1)
    chip_arr = chip.reshape(1)
    place = jnp.stack([chip, core])
    first = ("ffn1_up", "ffn1_down")
    rest = tuple(n for n in BIG if n not in first)

    own = [{n: W[n][l].astype(BF16) for n in BIG} for l in range(L)]

    def ag_begin(l, names, tag, after=None):
        lands = [jax.ShapeDtypeStruct((4,) + own[l][n].shape, BF16) for n in names]
        return exchange_begin(f"ag_start_{tag}", [own[l][n] for n in names], lands, ag_plan, 3 * len(names), after)

    def ag_finish(handle, names, tag, after):
        mine, lands = exchange_end(f"ag_wait_{tag}", handle, ag_plan, after)
        lands = ag_forward(lands, name=f"ag_forward_{tag}")
        return _layer_weights(dict(zip(names, place_own(lands, mine, chip_arr, name=f"place_own_{tag}"))))

    h_first, t1 = ag_begin(0, first, "0a")
    cs, sn = _rope_tables(positions)
    xs = x.reshape(T, D) + t1[0, 0]
    saved, ici, handed = [], None, None
    for l in range(L):
        p = {n: W[n][l] for n in SMALL if n != "final_norm"}
        dep = None
        if l == 0:
            w = ag_finish(h_first, first, "0a", xs)
            h_rest, dep = ag_begin(0, rest, "0b", after=w["ffn1_down"])
        else:
            mine, handle = handed
            lands, _ = exchange_end(f"ag_forward_wait_{l}", handle, ag_forward_plan, xs)
            w = _layer_weights(dict(zip(BIG, place_own(lands, mine, chip_arr, name=f"place_own_{l}"))))
        xs, s1 = _ffn_fwd(xs, p["norm_ffn1"], w["ffn1_up"], w["ffn1_down"], "ffn1", dep=dep)
        dep = None
        if l == 0:
            w.update(ag_finish(h_rest, rest, "0b", xs))
            if L > 1:
                ici, dep = ag_begin(1, BIG, "1", after=xs)
        xs, s2 = _mix_fwd(xs, p, w, cs, sn, B, dep=dep)
        dep = None
        if l + 1 < L:
            mine, lands = exchange_end(f"ag_wait_{l + 1}", ici, ag_plan, xs)
            handle, dep = exchange_begin(f"ag_forward_start_{l + 1}", lands, [], ag_forward_plan, 3 * len(BIG))
            handed = (mine, handle)
            if l + 2 < L:
                ici, dep = ag_begin(l + 2, BIG, str(l + 2), after=dep)
        xs, s3 = _ffn_fwd(xs, p["norm_ffn2"], w["ffn2_up"], w["ffn2_down"], "ffn2", dep=dep)
        saved.append((w, p, s1, s2, s3))

    dx, dfinal, loss_tile = loss_head(xs, final_norm, loss_target.reshape(T, D), name="loss_head")
    loss = lax.psum(loss_tile[0, 0], ("x", "y", "c"))

    def rs_begin(dw, tag):
        stacked = _layer_grads_stacked(dw)
        names = tuple(stacked)
        parts = [stacked[n] for n in names]
        r1 = rs_swap_halves(parts, name=f"rs_swap_{tag}")
        sums = [rs_chip_sum(g, a, core_arr, name=f"rs_chip_sum_{n}") for n, g, a in zip(names, parts, r1)]
        lands = [jax.ShapeDtypeStruct((3,) + s.shape[1:], BF16) for s in sums]
        handle, token = exchange_begin(f"rs_start_{tag}", sums, lands, rs_plan, 3 * len(names))
        return (names, parts, r1, handle), token

    acc = {n: lax.empty(W[n].shape, F32) for n in BIG}

    def rs_finish(l, pending, tag, after):
        names, parts, r1, handle = pending
        _, r2 = exchange_end(f"rs_wait_{tag}", handle, rs_plan, after)
        for n, g, a, b in zip(names, parts, r1, r2):
            acc[n] = rs_final_sum(g, a, b, place, acc[n], l, name=f"rs_final_sum_{n}")

    small_layers, pending, swapping, dep = [], [], None, None
    for l in reversed(range(L)):
        w, p, s1, s2, s3 = saved[l]
        dx, dg2, dwu2, dwd2 = _ffn_bwd(dx, s3, p["norm_ffn2"], w["ffn2_up"], w["ffn2_down"], "ffn2", dep=dep)
        dep = None
        if swapping is not None:
            above, names, handle = swapping
            parts, r1 = exchange_end(f"rs_swap_wait_{above}", handle, rs_swap_plan, dx)
            sums = [rs_chip_sum(g, a, core_arr, name=f"rs_chip_sum_{n}") for n, g, a in zip(names, parts, r1)]
            lands = [jax.ShapeDtypeStruct((3,) + s.shape[1:], BF16) for s in sums]
            handle, dep = exchange_begin(f"rs_start_{above}", sums, lands, rs_plan, 3 * len(names))
            pending.append((above, (names, parts, r1, handle)))
            swapping = None
        dx, dw, ds = _mix_bwd(dx, s2, p, w, cs, sn, B, dep=dep)
        dw.update(ffn2_up=dwu2, ffn2_down=dwd2)
        dep = None
        if l == 0:
            early, dep = rs_begin(dw, "0b")
            dw = {}
        dx, dg1, dwu1, dwd1 = _ffn_bwd(dx, s1, p["norm_ffn1"], w["ffn1_up"], w["ffn1_down"], "ffn1", dep=dep)
        dw.update(ffn1_up=dwu1, ffn1_down=dwd1)
        ds.update(norm_ffn1=dg1, norm_ffn2=dg2)
        small_layers.append(ds)
        if l == 0:
            last, dep = rs_begin(dw, "0a")
        else:
            stacked = _layer_grads_stacked(dw)
            halves = [jax.ShapeDtypeStruct((4, g.shape[1] // 2, g.shape[2]), BF16) for g in stacked.values()]
            handle, dep = exchange_begin(f"rs_swap_start_{l}", list(stacked.values()), halves, rs_swap_plan,
                                         len(stacked))
            swapping = (l, tuple(stacked), handle)
    small_layers.reverse()

    small = {n: jnp.stack([small_layers[l][n].reshape(W[n].shape[1:]) for l in range(L)]) for n in SMALL
             if n != "final_norm"}
    small["final_norm"] = dfinal.reshape(final_norm.shape)
    grads = _unpack_small(all_reduce_small(_pack_small(small) + dep[0, 0]), small)
    delta, new_m, new_v = {}, {}, {}
    d, nm, nv = adamw(_pack_small(W), _pack_small(grads), _pack_small(Mo), _pack_small(Vo), name="adamw_small")
    delta.update(_unpack_small(d, W))
    new_m.update(_unpack_small(nm, W))
    new_v.update(_unpack_small(nv, W))

    def update(names, tag, d):
        joined = rs_join_halves([acc[n] for n in names], name=f"rs_join_{tag}")
        for n, g in zip(names, joined):
            flip = (lambda a: jnp.swapaxes(a, 1, 2)) if n == "w_in" else (lambda a: a)
            sh = flip(W[n]).shape
            two = lambda a: flip(a).reshape(sh[0] * sh[1], sh[2])
            back = lambda a: flip(a.reshape(sh))
            gc, d, nm, nv = adamw(two(W[n]), two(g), two(Mo[n]), two(Vo[n]), dep=d, copy_g=True, name=f"adamw_{n}")
            grads[n], delta[n], new_m[n], new_v[n] = back(gc), back(d), back(nm), back(nv)
        return d

    for l, item in pending:
        rs_finish(l, item, str(l), d)
    rs_finish(0, early, "0b", d)
    d = update(rest, "rest", d)
    rs_finish(0, last, "0a", d)
    update(first, "first", d)

    return (loss, dx.reshape(B, S, D), *[grads[n] for n in WEIGHTS], *[delta[n] for n in WEIGHTS],
            *[new_m[n] for n in WEIGHTS], *[new_v[n] for n in WEIGHTS])
```

```python
import functools

import jax
import jax.numpy as jnp
from jax import lax
from jax.experimental import pallas as pl
from jax.experimental.pallas import tpu as pltpu

F32 = jnp.float32
BF16 = jnp.bfloat16

N_HEADS = 8
QK_NOPE_DIM = 128
QK_ROPE_DIM = 64
QK_DIM = QK_NOPE_DIM + QK_ROPE_DIM
V_HEAD_DIM = 128
HEAD_PAD = 256
Q_LORA_RANK = 384
KV_LORA_RANK = 256
ROPE_THETA = 10000.0
POOL_WINDOWS = (2, 4, 8, 16)
N_POOL_GROUPS = 4
POOL_GROUP_DIM = 128
POOL_DIM = N_POOL_GROUPS * POOL_GROUP_DIM
LAT_DIM = 768
NORM_EPS = 1e-6
ADAM_LR = 0.001
ADAM_B1 = 0.9
ADAM_B2 = 0.999
ADAM_EPS = 1e-08
ADAM_WD = 0.01
ADAM_STEP = 10
NEG_INF = -1e30
LANES = 128
ATT_BLOCK = 512
VMEM_LIMIT = 48 * 1024 * 1024
MESH = pl.DeviceIdType.MESH
_NT = (((1,), (1,)), ((), ()))
_TN = (((0,), (0,)), ((), ()))

BIG = ("ffn1_up", "ffn1_down", "w_in", "w_pool_proj", "w_uq", "w_ukv", "w_attn_proj", "w_out",
       "ffn2_up", "ffn2_down")
ROW_SHARDED = ("ffn1_down", "w_attn_proj", "w_out", "ffn2_down")
SMALL = ("norm_ffn1", "norm_mix", "b_gate", "pool_maps", "pool_scale", "q_latent_norm",
         "kv_latent_norm", "norm_ffn2", "final_norm")
WEIGHTS = ("norm_ffn1", "ffn1_up", "ffn1_down", "norm_mix", "w_in", "b_gate", "pool_maps", "pool_scale",
           "w_pool_proj", "q_latent_norm", "w_uq", "kv_latent_norm", "w_ukv", "w_attn_proj", "w_out",
           "norm_ffn2", "ffn2_up", "ffn2_down", "final_norm")


def _pick(dim, cands):
    for c in cands:
        if c <= dim and dim % c == 0:
            return c
    return dim


def _cparams(sem=None, **kw):
    if sem is not None:
        kw["dimension_semantics"] = sem
    return pltpu.CompilerParams(vmem_limit_bytes=VMEM_LIMIT, **kw)


def _sigmoid(x):
    return 0.5 * jnp.tanh(0.5 * x) + 0.5


MM_TILE_BUDGET = 30 * 1024 * 1024
TILE_SIZES = (1408, 1024, 768, 512, 384, 256, 128)


V7X_MXU_FLOPS = 9.0e14
V7X_HBM_BYTES = 2.5e12
GRID_STEP_S = 0.35e-6


def _mm_tiles(M, N, K, sa, sb, so, sr):
    tks = [K] if K <= 2816 else [t for t in (2816, 2048, 1408, 1024, 512, 256, 128) if K % t == 0]
    best = None
    for tk in tks:
        for tm in [t for t in TILE_SIZES if M % t == 0] or [M]:
            for tn in [t for t in TILE_SIZES if N % t == 0] or [N]:
                need = 2 * (tm * tk * sa + tk * tn * sb + tm * tn * (so + sr)) + (tm * tn * 4 if tk < K else 0)
                if need > MM_TILE_BUDGET:
                    continue
                ni, nj, nk = M // tm, N // tn, K // tk
                a_bytes = M * K * sa * (nj if nk > 1 else 1)
                b_bytes = K * N * sb * (1 if nj == 1 and nk == 1 else ni)
                traffic = a_bytes + b_bytes + M * N * (so + sr) + (M * N * 8 * nk if nk > 1 else 0)
                t = max(2.0 * M * N * K / V7X_MXU_FLOPS, traffic / V7X_HBM_BYTES) + ni * nj * nk * GRID_STEP_S
                if best is None or t < best[0]:
                    best = (t, (tm, tn, tk))
    assert best is not None, (M, N, K)
    return best[1]


def mm(a, b, *, name, ta=False, tb=False, out_dtype=F32, res=None, alpha=1.0, dep=None):
    if ta:
        K, M = a.shape
    else:
        M, K = a.shape
    if tb:
        N, K2 = b.shape
    else:
        K2, N = b.shape
    assert K == K2, (a.shape, b.shape, ta, tb)
    tm, tn, tk = _mm_tiles(M, N, K, a.dtype.itemsize, b.dtype.itemsize, jnp.dtype(out_dtype).itemsize,
                           0 if res is None else res.dtype.itemsize)
    nk = K // tk
    dims = (((0 if ta else 1,), (1 if tb else 0,)), ((), ()))

    def body(*refs):
        a_ref, b_ref = refs[:2]
        r_ref = refs[2] if res is not None else None
        o_ref = refs[-2] if nk > 1 else refs[-1]

        def finish(r):
            if alpha != 1.0:
                r = r * alpha
            if res is not None:
                r = r_ref[...].astype(F32) + r
            o_ref[...] = r.astype(out_dtype)

        part = lax.dot_general(a_ref[...].astype(BF16), b_ref[...].astype(BF16), dims, preferred_element_type=F32)
        if nk == 1:
            finish(part)
            return
        acc = refs[-1]
        k = pl.program_id(2)

        @pl.when(k == 0)
        def _():
            acc[...] = part

        @pl.when(k > 0)
        def _():
            acc[...] += part

        @pl.when(k == nk - 1)
        def _():
            finish(acc[...])

    a_spec = pl.BlockSpec((tk, tm), lambda i, j, k: (k, i)) if ta else pl.BlockSpec((tm, tk), lambda i, j, k: (i, k))
    b_spec = pl.BlockSpec((tn, tk), lambda i, j, k: (j, k)) if tb else pl.BlockSpec((tk, tn), lambda i, j, k: (k, j))
    o_spec = pl.BlockSpec((tm, tn), lambda i, j, k: (i, j))
    in_specs = [a_spec, b_spec]
    args = [a, b]
    if res is not None:
        in_specs.append(o_spec)
        args.append(res)
    if dep is not None:
        in_specs.append(pl.BlockSpec((8, LANES), lambda i, j, k: (0, 0)))
        args.append(dep)
    return pl.pallas_call(
        body, name=name, grid=(M // tm, N // tn, nk), in_specs=in_specs, out_specs=o_spec,
        out_shape=jax.ShapeDtypeStruct((M, N), out_dtype),
        scratch_shapes=[pltpu.VMEM((tm, tn), F32)] if nk > 1 else [],
        compiler_params=_cparams(("parallel", "parallel", "arbitrary")),
    )(*args)


MXU_COLS = 256


def _col_chunks(n):
    return [(lo, min(lo + MXU_COLS, n)) for lo in range(0, n, MXU_COLS)]


def ffn_up_act(h, wu4, *, name):
    T, D = h.shape
    cq = wu4.shape[2]
    Fh = 2 * cq
    tm = _pick(T, (512, 256, 128))

    def body(h_ref, wg_ref, wu_ref, g_ref, u_ref, a_ref):
        hv = h_ref[...]
        for lo, hi in _col_chunks(cq):
            gv = jnp.dot(hv, wg_ref[0, :, lo:hi], preferred_element_type=F32)
            uv = jnp.dot(hv, wu_ref[0, :, lo:hi], preferred_element_type=F32)
            g_ref[:, lo:hi] = gv.astype(BF16)
            u_ref[:, lo:hi] = uv.astype(BF16)
            a_ref[:, lo:hi] = (gv * _sigmoid(gv) * uv).astype(BF16)

    tile = pl.BlockSpec((tm, cq), lambda j, i: (i, j))
    sh = jax.ShapeDtypeStruct((T, Fh), BF16)
    return pl.pallas_call(
        body, name=name, grid=(2, T // tm),
        in_specs=[pl.BlockSpec((tm, D), lambda j, i: (i, 0)), pl.BlockSpec((1, D, cq), lambda j, i: (j, 0, 0)),
                  pl.BlockSpec((1, D, cq), lambda j, i: (2 + j, 0, 0))],
        out_specs=[tile, tile, tile], out_shape=[sh, sh, sh],
        compiler_params=_cparams(("parallel", "parallel")),
    )(h, wu4, wu4)


def ffn_down_dx_act(dy, wd, g, u, *, dep=None, name):
    T, D = dy.shape
    Fh = wd.shape[0]
    cq = Fh // 2
    tm = _pick(T, (512, 256, 128))

    def body(dy_ref, wd_ref, g_ref, u_ref, *rest):
        dg_ref, du_ref = rest[-2:]
        dyv = dy_ref[...].astype(BF16)
        for lo, hi in _col_chunks(cq):
            da = 0.5 * lax.dot_general(dyv, wd_ref[lo:hi, :], _NT, preferred_element_type=F32)
            gv = g_ref[:, lo:hi].astype(F32)
            uv = u_ref[:, lo:hi].astype(F32)
            s = _sigmoid(gv)
            dg_ref[:, lo:hi] = (da * uv * (s * (1.0 + gv * (1.0 - s)))).astype(BF16)
            du_ref[:, lo:hi] = (da * (gv * s)).astype(BF16)

    tile = pl.BlockSpec((tm, cq), lambda j, i: (i, j))
    sh = jax.ShapeDtypeStruct((T, Fh), BF16)
    in_specs = [pl.BlockSpec((tm, D), lambda j, i: (i, 0)), pl.BlockSpec((cq, D), lambda j, i: (j, 0)), tile, tile]
    args = [dy, wd, g, u]
    if dep is not None:
        in_specs.append(pl.BlockSpec((8, LANES), lambda j, i: (0, 0)))
        args.append(dep)
    return pl.pallas_call(
        body, name=name, grid=(2, T // tm), in_specs=in_specs, out_specs=[tile, tile], out_shape=[sh, sh],
        compiler_params=_cparams(("parallel", "parallel")),
    )(*args)


def ffn_up_dw(h, dg, du, *, name):
    T, D = h.shape
    cq = dg.shape[1] // 2
    tk = _pick(T, (1024, 512, 256, 128))
    nk = T // tk

    def body(h_ref, dg_ref, du_ref, o_ref, acc):
        p = pl.program_id(0)
        k = pl.program_id(1)

        @pl.when(k == 0)
        def _():
            acc[...] = jnp.zeros_like(acc)

        @pl.when(p < 2)
        def _():
            acc[...] += lax.dot_general(h_ref[...], dg_ref[...], _TN, preferred_element_type=F32)

        @pl.when(p >= 2)
        def _():
            acc[...] += lax.dot_general(h_ref[...], du_ref[...], _TN, preferred_element_type=F32)

        @pl.when(k == nk - 1)
        def _():
            o_ref[0] = acc[...].astype(BF16)

    return pl.pallas_call(
        body, name=name, grid=(4, nk),
        in_specs=[pl.BlockSpec((tk, D), lambda p, k: (k, 0)),
                  pl.BlockSpec((tk, cq), lambda p, k: (jnp.where(p < 2, k, 0), jnp.minimum(p, 1))),
                  pl.BlockSpec((tk, cq), lambda p, k: (jnp.where(p < 2, 0, k), jnp.maximum(p - 2, 0)))],
        out_specs=pl.BlockSpec((1, D, cq), lambda p, k: (p, 0, 0)),
        out_shape=jax.ShapeDtypeStruct((4, D, cq), BF16),
        scratch_shapes=[pltpu.VMEM((D, cq), F32)],
        compiler_params=_cparams(("parallel", "arbitrary")),
    )(h, dg, du)


def ffn_up_dx_norm(dg, du, wu4, x, g, dy, *, name):
    T = dg.shape[0]
    _, D, cq = wu4.shape
    tm = _pick(T, (512, 256, 128))

    def body(dg_ref, du_ref, wg_ref, wu_ref, x_ref, g_ref, dy_ref, dx_ref, dgain_ref, acc):
        i = pl.program_id(0)
        k = pl.program_id(1)
        part = lax.dot_general(dg_ref[...], wg_ref[0], _NT, preferred_element_type=F32)
        part = part + lax.dot_general(du_ref[...], wu_ref[0], _NT, preferred_element_type=F32)

        @pl.when(jnp.logical_and(i == 0, k == 0))
        def _():
            dgain_ref[...] = jnp.zeros_like(dgain_ref)

        @pl.when(k == 0)
        def _():
            acc[...] = part

        @pl.when(k == 1)
        def _():
            dx, dgain = _rms_bwd_math(x_ref[...], g_ref[...], acc[...] + part)
            dx_ref[...] = dy_ref[...] + dx
            dgain_ref[...] += dgain

    tile = pl.BlockSpec((tm, cq), lambda i, k: (i, k))
    row = pl.BlockSpec((tm, D), lambda i, k: (i, 0))
    vec = pl.BlockSpec((1, D), lambda i, k: (0, 0))
    return pl.pallas_call(
        body, name=name, grid=(T // tm, 2),
        in_specs=[tile, tile, pl.BlockSpec((1, D, cq), lambda i, k: (k, 0, 0)),
                  pl.BlockSpec((1, D, cq), lambda i, k: (2 + k, 0, 0)), row, vec, row],
        out_specs=[row, vec],
        out_shape=[jax.ShapeDtypeStruct((T, D), F32), jax.ShapeDtypeStruct((1, D), F32)],
        scratch_shapes=[pltpu.VMEM((tm, D), F32)],
        compiler_params=_cparams(("arbitrary", "arbitrary")),
    )(dg, du, wu4, wu4, x, g.reshape(1, D), dy)


def mix_in(h, w_lat, w_pool, w_gate, *, name):
    T, D = h.shape
    tm = _pick(T, (512, 256, 128))

    def body(h_ref, wl_ref, wp_ref, wg_ref, lat_ref, xp_ref, gl_ref):
        hv = h_ref[...]
        lat_ref[...] = jnp.dot(hv, wl_ref[...], preferred_element_type=F32)
        xp_ref[...] = jnp.dot(hv, wp_ref[...], preferred_element_type=F32)
        gl_ref[...] = jnp.dot(hv, wg_ref[...], preferred_element_type=F32).astype(BF16)

    whole = lambda a: pl.BlockSpec(a.shape, lambda i: (0, 0))
    out = lambda a: pl.BlockSpec((tm, a.shape[1]), lambda i: (i, 0))
    return pl.pallas_call(
        body, name=name, grid=(T // tm,),
        in_specs=[pl.BlockSpec((tm, D), lambda i: (i, 0)), whole(w_lat), whole(w_pool), whole(w_gate)],
        out_specs=[out(w_lat), out(w_pool), out(w_gate)],
        out_shape=[jax.ShapeDtypeStruct((T, w_lat.shape[1]), F32), jax.ShapeDtypeStruct((T, w_pool.shape[1]), F32),
                   jax.ShapeDtypeStruct((T, w_gate.shape[1]), BF16)],
        compiler_params=_cparams(("parallel",)),
    )(h, w_lat, w_pool, w_gate)


def mix_in_dx_norm(dlat, dxp, dgl, w_lat, w_pool, w_gate, x, g, dy, *, name):
    T, D = x.shape
    tm = _pick(T, (512, 256, 128))

    def body(dlat_ref, dxp_ref, dgl_ref, wl_ref, wp_ref, wg_ref, x_ref, g_ref, dy_ref, dx_ref, dgain_ref):
        @pl.when(pl.program_id(0) == 0)
        def _():
            dgain_ref[...] = jnp.zeros_like(dgain_ref)

        dh = lax.dot_general(dlat_ref[...], wl_ref[...], _NT, preferred_element_type=F32)
        dh = dh + lax.dot_general(dxp_ref[...], wp_ref[...], _NT, preferred_element_type=F32)
        dh = dh + lax.dot_general(dgl_ref[...], wg_ref[...], _NT, preferred_element_type=F32)
        dx, dgain = _rms_bwd_math(x_ref[...], g_ref[...], dh)
        dx_ref[...] = dy_ref[...] + dx
        dgain_ref[...] += dgain

    row = lambda a: pl.BlockSpec((tm, a.shape[1]), lambda i: (i, 0))
    whole = lambda a: pl.BlockSpec(a.shape, lambda i: (0, 0))
    vec = pl.BlockSpec((1, D), lambda i: (0, 0))
    return pl.pallas_call(
        body, name=name, grid=(T // tm,),
        in_specs=[row(dlat), row(dxp), row(dgl), whole(w_lat), whole(w_pool), whole(w_gate), row(x), vec, row(dy)],
        out_specs=[row(x), vec],
        out_shape=[jax.ShapeDtypeStruct((T, D), F32), jax.ShapeDtypeStruct((1, D), F32)],
        compiler_params=_cparams(("arbitrary",)),
    )(dlat, dxp, dgl, w_lat, w_pool, w_gate, x, g.reshape(1, D), dy)


def _rows(T, width_bytes):
    cap = max(8, (2 * 1024 * 1024) // width_bytes)
    return _pick(T, tuple(c for c in (1024, 512, 256, 128, 64, 32, 16) if c <= cap))


def rms_fwd(x, g, *, name, dep=None):
    T, D = x.shape
    tm = _rows(T, D * 4)

    def body(x_ref, g_ref, *rest):
        xv = x_ref[...]
        r = lax.rsqrt(jnp.mean(xv * xv, axis=-1, keepdims=True) + NORM_EPS)
        rest[-1][...] = (xv * r * g_ref[...]).astype(BF16)

    in_specs = [pl.BlockSpec((tm, D), lambda i: (i, 0)), pl.BlockSpec((1, D), lambda i: (0, 0))]
    args = [x, g.reshape(1, D)]
    if dep is not None:
        in_specs.append(pl.BlockSpec((8, LANES), lambda i: (0, 0)))
        args.append(dep)
    return pl.pallas_call(
        body, name=name, grid=(T // tm,), in_specs=in_specs,
        out_specs=pl.BlockSpec((tm, D), lambda i: (i, 0)),
        out_shape=jax.ShapeDtypeStruct((T, D), BF16),
        compiler_params=_cparams(("parallel",)),
    )(*args)


def _rms_bwd_math(xv, gv, dh):
    r = lax.rsqrt(jnp.mean(xv * xv, axis=-1, keepdims=True) + NORM_EPS)
    xn = xv * r
    dg = jnp.sum(dh * xn, axis=0, keepdims=True)
    dxn = dh * gv
    dx = r * (dxn - xn * jnp.mean(dxn * xn, axis=-1, keepdims=True))
    return dx, dg


def _rope(xv, cv, sv):
    half = QK_ROPE_DIM // 2
    lane = lax.broadcasted_iota(jnp.int32, xv.shape, 1)
    swapped = jnp.where(lane < half, pltpu.roll(xv, LANES - half, 1), pltpu.roll(xv, half, 1))
    return xv * cv + swapped * sv


def _rope_t(dy, cv, sv):
    half = QK_ROPE_DIM // 2
    ds = dy * sv
    lane = lax.broadcasted_iota(jnp.int32, dy.shape, 1)
    swapped = jnp.where(lane < half, pltpu.roll(ds, LANES - half, 1), pltpu.roll(ds, half, 1))
    return dy * cv + swapped


def lat_fwd(lat, qn_w, kvn_w, cs, sn, *, name):
    T = lat.shape[0]
    tm = _rows(T, LAT_DIM * 4)
    kv0 = Q_LORA_RANK
    kr0 = Q_LORA_RANK + KV_LORA_RANK

    def body(lat_ref, qw_ref, kw_ref, c_ref, s_ref, qn_ref, kvn_ref, kr_ref):
        ql = lat_ref[:, :kv0]
        r = lax.rsqrt(jnp.mean(ql * ql, axis=-1, keepdims=True) + NORM_EPS)
        qn_ref[...] = (ql * r * qw_ref[...]).astype(BF16)
        kl = lat_ref[:, kv0:kr0]
        r = lax.rsqrt(jnp.mean(kl * kl, axis=-1, keepdims=True) + NORM_EPS)
        kvn_ref[...] = (kl * r * kw_ref[...]).astype(BF16)
        kr_ref[...] = _rope(lat_ref[:, kr0:], c_ref[...], s_ref[...]).astype(BF16)

    return pl.pallas_call(
        body, name=name, grid=(T // tm,),
        in_specs=[pl.BlockSpec((tm, LAT_DIM), lambda i: (i, 0)),
                  pl.BlockSpec((1, Q_LORA_RANK), lambda i: (0, 0)),
                  pl.BlockSpec((1, KV_LORA_RANK), lambda i: (0, 0)),
                  pl.BlockSpec((tm, LANES), lambda i: (i, 0)), pl.BlockSpec((tm, LANES), lambda i: (i, 0))],
        out_specs=[pl.BlockSpec((tm, Q_LORA_RANK), lambda i: (i, 0)),
                   pl.BlockSpec((tm, KV_LORA_RANK), lambda i: (i, 0)),
                   pl.BlockSpec((tm, LANES), lambda i: (i, 0))],
        out_shape=[jax.ShapeDtypeStruct((T, Q_LORA_RANK), BF16), jax.ShapeDtypeStruct((T, KV_LORA_RANK), BF16),
                   jax.ShapeDtypeStruct((T, LANES), BF16)],
        compiler_params=_cparams(("parallel",)),
    )(lat, qn_w.reshape(1, -1), kvn_w.reshape(1, -1), cs, sn)


def lat_bwd(lat, qn_w, kvn_w, dqn, dkvn, dkr, cs, sn, *, name):
    T = lat.shape[0]
    tm = _rows(T, LAT_DIM * 4)
    kv0 = Q_LORA_RANK
    kr0 = Q_LORA_RANK + KV_LORA_RANK

    def body(lat_ref, qw_ref, kw_ref, dqn_ref, dkvn_ref, dkr_ref, c_ref, s_ref, dlat_ref, dqw_ref, dkw_ref):
        @pl.when(pl.program_id(0) == 0)
        def _():
            dqw_ref[...] = jnp.zeros_like(dqw_ref)
            dkw_ref[...] = jnp.zeros_like(dkw_ref)

        dx, dg = _rms_bwd_math(lat_ref[:, :kv0], qw_ref[...], dqn_ref[...])
        dlat_ref[:, :kv0] = dx.astype(BF16)
        dqw_ref[...] += dg
        dx, dg = _rms_bwd_math(lat_ref[:, kv0:kr0], kw_ref[...], dkvn_ref[...])
        dlat_ref[:, kv0:kr0] = dx.astype(BF16)
        dkw_ref[...] += dg
        dlat_ref[:, kr0:] = _rope_t(dkr_ref[...], c_ref[...], s_ref[...]).astype(BF16)

    row = lambda w: pl.BlockSpec((tm, w), lambda i: (i, 0))
    vec = lambda w: pl.BlockSpec((1, w), lambda i: (0, 0))
    return pl.pallas_call(
        body, name=name, grid=(T // tm,),
        in_specs=[row(LAT_DIM), vec(Q_LORA_RANK), vec(KV_LORA_RANK), row(Q_LORA_RANK), row(KV_LORA_RANK),
                  row(LANES), row(LANES), row(LANES)],
        out_specs=[row(LAT_DIM), vec(Q_LORA_RANK), vec(KV_LORA_RANK)],
        out_shape=[jax.ShapeDtypeStruct((T, LAT_DIM), BF16), jax.ShapeDtypeStruct((1, Q_LORA_RANK), F32),
                   jax.ShapeDtypeStruct((1, KV_LORA_RANK), F32)],
        compiler_params=_cparams(("arbitrary",)),
    )(lat, qn_w.reshape(1, -1), kvn_w.reshape(1, -1), dqn, dkvn, dkr, cs, sn)


def attn_proj_gate(o, wap, gl, bg, ba, *, name):
    T, D2 = gl.shape
    D = D2 // 2
    tm = _pick(T, (512, 256, 128))

    def body(o_ref, w_ref, gl_ref, bg_ref, ba_ref, bb_ref, m_ref):
        bb = jnp.dot(o_ref[...], w_ref[...], preferred_element_type=F32)
        bb_ref[...] = bb.astype(BF16)
        ga = _sigmoid(gl_ref[:, :D].astype(F32) + bg_ref[:, :D])
        gb = _sigmoid(gl_ref[:, D:].astype(F32) + bg_ref[:, D:])
        m_ref[...] = (ga * ba_ref[...].astype(F32) + gb * bb).astype(BF16)

    row = lambda w: pl.BlockSpec((tm, w), lambda i: (i, 0))
    return pl.pallas_call(
        body, name=name, grid=(T // tm,),
        in_specs=[row(o.shape[1]), pl.BlockSpec(wap.shape, lambda i: (0, 0)), row(D2),
                  pl.BlockSpec((1, D2), lambda i: (0, 0)), row(D)],
        out_specs=[row(D), row(D)],
        out_shape=[jax.ShapeDtypeStruct((T, D), BF16), jax.ShapeDtypeStruct((T, D), BF16)],
        compiler_params=_cparams(("parallel",)),
    )(o, wap, gl, bg.reshape(1, D2), ba)


def out_dx_gate(dy, wo, gl, bg, ba, bb, *, name):
    T, D2 = gl.shape
    D = D2 // 2
    tm = _pick(T, (512, 256, 128))

    def body(dy_ref, w_ref, gl_ref, bg_ref, ba_ref, bb_ref, dba_ref, dbb_ref, dgl_ref, dbg_ref):
        @pl.when(pl.program_id(0) == 0)
        def _():
            dbg_ref[...] = jnp.zeros_like(dbg_ref)

        dmv = lax.dot_general(dy_ref[...].astype(BF16), w_ref[...], _NT, preferred_element_type=F32)
        ga = _sigmoid(gl_ref[:, :D].astype(F32) + bg_ref[:, :D])
        gb = _sigmoid(gl_ref[:, D:].astype(F32) + bg_ref[:, D:])
        dba_ref[...] = (dmv * ga).astype(BF16)
        dbb_ref[...] = (dmv * gb).astype(BF16)
        dla = dmv * ba_ref[...].astype(F32) * ga * (1.0 - ga)
        dlb = dmv * bb_ref[...].astype(F32) * gb * (1.0 - gb)
        dgl_ref[:, :D] = dla.astype(BF16)
        dgl_ref[:, D:] = dlb.astype(BF16)
        dbg_ref[:, :D] += jnp.sum(dla, axis=0, keepdims=True)
        dbg_ref[:, D:] += jnp.sum(dlb, axis=0, keepdims=True)

    row = lambda w: pl.BlockSpec((tm, w), lambda i: (i, 0))
    vec = pl.BlockSpec((1, D2), lambda i: (0, 0))
    return pl.pallas_call(
        body, name=name, grid=(T // tm,),
        in_specs=[row(D), pl.BlockSpec(wo.shape, lambda i: (0, 0)), row(D2), vec, row(D), row(D)],
        out_specs=[row(D), row(D), row(D2), vec],
        out_shape=[jax.ShapeDtypeStruct((T, D), BF16), jax.ShapeDtypeStruct((T, D), BF16),
                   jax.ShapeDtypeStruct((T, D2), BF16), jax.ShapeDtypeStruct((1, D2), F32)],
        compiler_params=_cparams(("arbitrary",)),
    )(dy, wo, gl, bg.reshape(1, D2), ba, bb)


def loss_head(x, gf, tgt, *, name):
    T, D = x.shape
    tm = _rows(T, D * 4)

    def body(x_ref, g_ref, t_ref, dx_ref, dg_ref, loss_ref):
        @pl.when(pl.program_id(0) == 0)
        def _():
            dg_ref[...] = jnp.zeros_like(dg_ref)
            loss_ref[...] = jnp.zeros_like(loss_ref)

        xv = x_ref[...]
        gv = g_ref[...]
        r = lax.rsqrt(jnp.mean(xv * xv, axis=-1, keepdims=True) + NORM_EPS)
        xn = xv * r
        err = xn * gv - t_ref[...]
        loss_ref[...] += 0.5 * jnp.sum(jnp.mean(err * err, axis=-1, keepdims=True))
        dy = err * (1.0 / D)
        dg_ref[...] += jnp.sum(dy * xn, axis=0, keepdims=True)
        dxn = dy * gv
        dx_ref[...] = r * (dxn - xn * jnp.mean(dxn * xn, axis=-1, keepdims=True))

    row = pl.BlockSpec((tm, D), lambda i: (i, 0))
    vec = pl.BlockSpec((1, D), lambda i: (0, 0))
    return pl.pallas_call(
        body, name=name, grid=(T // tm,), in_specs=[row, vec, row],
        out_specs=[row, vec, pl.BlockSpec((8, LANES), lambda i: (0, 0))],
        out_shape=[jax.ShapeDtypeStruct((T, D), F32), jax.ShapeDtypeStruct((1, D), F32),
                   jax.ShapeDtypeStruct((8, LANES), F32)],
        compiler_params=_cparams(("arbitrary",)),
    )(x, gf.reshape(1, D), tgt)


def _shift_rows(s, k, down):
    n = s.shape[0]
    t = lax.broadcasted_iota(jnp.int32, s.shape, 0)
    if down:
        return jnp.where(t >= k, pltpu.roll(s, k, 0), 0.0)
    return jnp.where(t < n - k, pltpu.roll(s, n - k, 0), 0.0)


def _window_sum(s, w, down):
    k = 1
    while k < w:
        s = s + _shift_rows(s, k, down)
        k *= 2
    return s


def _pool_count(shape, w):
    t = lax.broadcasted_iota(jnp.int32, shape, 0)
    return jnp.minimum(t + 1, w).astype(F32)


def pool_fwd(xp, maps, scale, B, *, name):
    T, P = xp.shape
    S = T // B
    G = POOL_GROUP_DIM

    def body(x_ref, m_ref, sc_ref, o_ref):
        for g, w in enumerate(POOL_WINDOWS):
            xg = x_ref[:, g * G:(g + 1) * G]
            pooled = _window_sum(xg, w, True) / _pool_count(xg.shape, w) - xg
            mixed = jnp.dot(pooled.astype(BF16), m_ref[g], preferred_element_type=F32)
            o_ref[:, g * G:(g + 1) * G] = (mixed * sc_ref[:, g * G:(g + 1) * G]).astype(BF16)

    return pl.pallas_call(
        body, name=name, grid=(B,),
        in_specs=[pl.BlockSpec((S, P), lambda b: (b, 0)), pl.BlockSpec((N_POOL_GROUPS, G, G), lambda b: (0, 0, 0)),
                  pl.BlockSpec((1, P), lambda b: (0, 0))],
        out_specs=pl.BlockSpec((S, P), lambda b: (b, 0)),
        out_shape=jax.ShapeDtypeStruct((T, P), BF16),
        compiler_params=_cparams(("parallel",)),
    )(xp, maps, scale.reshape(1, P))


def pool_bwd(xp, dmixed, maps, scale, B, *, name):
    T, P = xp.shape
    S = T // B
    G = POOL_GROUP_DIM

    def body(x_ref, dm_ref, m_ref, sc_ref, dx_ref, dmaps_ref, dsc_ref):
        @pl.when(pl.program_id(0) == 0)
        def _():
            dmaps_ref[...] = jnp.zeros_like(dmaps_ref)
            dsc_ref[...] = jnp.zeros_like(dsc_ref)

        for g, w in enumerate(POOL_WINDOWS):
            cols = slice(g * G, (g + 1) * G)
            xg = x_ref[:, cols]
            cnt = _pool_count(xg.shape, w)
            pooled = (_window_sum(xg, w, True) / cnt - xg).astype(BF16)
            mixed = jnp.dot(pooled, m_ref[g], preferred_element_type=F32)
            dmx = dm_ref[:, cols]
            dsc_ref[:, cols] += jnp.sum(dmx * mixed, axis=0, keepdims=True)
            dmp = (dmx * sc_ref[:, cols]).astype(BF16)
            dmaps_ref[g] += lax.dot_general(pooled, dmp, (((0,), (0,)), ((), ())), preferred_element_type=F32)
            dpooled = lax.dot_general(dmp, m_ref[g], (((1,), (1,)), ((), ())), preferred_element_type=F32)
            dx_ref[:, cols] = (_window_sum(dpooled / cnt, w, False) - dpooled).astype(BF16)

    blk = pl.BlockSpec((S, P), lambda b: (b, 0))
    mp = pl.BlockSpec((N_POOL_GROUPS, G, G), lambda b: (0, 0, 0))
    vec = pl.BlockSpec((1, P), lambda b: (0, 0))
    return pl.pallas_call(
        body, name=name, grid=(B,), in_specs=[blk, blk, mp, vec], out_specs=[blk, mp, vec],
        out_shape=[jax.ShapeDtypeStruct((T, P), BF16), jax.ShapeDtypeStruct((N_POOL_GROUPS, G, G), F32),
                   jax.ShapeDtypeStruct((1, P), F32)],
        compiler_params=_cparams(("arbitrary",)),
    )(xp, dmixed, maps, scale.reshape(1, P))


def _keys(kv_ref, kr_ref, rows):
    return jnp.concatenate([kv_ref[rows, :QK_NOPE_DIM], kr_ref[rows, :]], axis=1)


def _causal(s):
    row = lax.broadcasted_iota(jnp.int32, s.shape, 0)
    col = lax.broadcasted_iota(jnp.int32, s.shape, 1)
    return jnp.where(row >= col, s, NEG_INF)


def attn_fwd(q, kv, kr, cs, sn, B, *, name):
    T = q.shape[0]
    S = T // B
    blk = min(ATT_BLOCK, S)
    nb = S // blk
    H = N_HEADS
    scale = QK_DIM ** -0.5

    def body(q_ref, kv_ref, kr_ref, c_ref, s_ref, o_ref, lse_ref, qs_ref):
        qs_ref[:, :QK_NOPE_DIM] = (q_ref[:, :QK_NOPE_DIM] * scale).astype(BF16)
        qs_ref[:, QK_NOPE_DIM:] = _rope(q_ref[:, QK_NOPE_DIM:], c_ref[...] * scale, s_ref[...] * scale).astype(BF16)
        for qi in range(nb):
            rows = slice(qi * blk, (qi + 1) * blk)
            qb = qs_ref[rows, :]
            sd = _causal(lax.dot_general(qb, _keys(kv_ref, kr_ref, rows), _NT, preferred_element_type=F32))
            m = jnp.max(sd, axis=-1, keepdims=True)
            if qi > 0:
                prev = slice(0, qi * blk)
                sp = lax.dot_general(qb, _keys(kv_ref, kr_ref, prev), _NT, preferred_element_type=F32)
                m = jnp.maximum(m, jnp.max(sp, axis=-1, keepdims=True))
            pd = jnp.exp(sd - m)
            l = jnp.sum(pd, axis=-1, keepdims=True)
            acc = jnp.dot(pd.astype(BF16), kv_ref[rows, QK_NOPE_DIM:], preferred_element_type=F32)
            if qi > 0:
                pp = jnp.exp(sp - m)
                l = l + jnp.sum(pp, axis=-1, keepdims=True)
                acc = acc + jnp.dot(pp.astype(BF16), kv_ref[prev, QK_NOPE_DIM:], preferred_element_type=F32)
            o_ref[rows, :] = (acc / l).astype(BF16)
            lse_ref[0, rows, :] = m + jnp.log(l)

    head = pl.BlockSpec((S, HEAD_PAD), lambda b, h: (b, h))
    shared = pl.BlockSpec((S, LANES), lambda b, h: (b, 0))
    return pl.pallas_call(
        body, name=name, grid=(B, H),
        in_specs=[head, head, shared, shared, shared],
        out_specs=[pl.BlockSpec((S, V_HEAD_DIM), lambda b, h: (b, h)), pl.BlockSpec((1, S, 1), lambda b, h: (h, b, 0)),
                   head],
        out_shape=[jax.ShapeDtypeStruct((T, H * V_HEAD_DIM), BF16), jax.ShapeDtypeStruct((H, T, 1), F32),
                   jax.ShapeDtypeStruct((T, H * HEAD_PAD), BF16)],
        compiler_params=_cparams(("parallel", "parallel")),
    )(q, kv, kr, cs, sn)


def attn_bwd(q, kv, kr, o, do, lse, cs, sn, B, *, name):
    T = q.shape[0]
    S = T // B
    blk = min(ATT_BLOCK, S)
    nb = S // blk
    H = N_HEADS
    scale = QK_DIM ** -0.5

    def body(q_ref, kv_ref, kr_ref, o_ref, do_ref, lse_ref, c_ref, s_ref, dq_ref, dkv_ref, dkr_ref, dk_s, dv_s):
        dk_s[...] = jnp.zeros_like(dk_s)
        dv_s[...] = jnp.zeros_like(dv_s)

        @pl.when(pl.program_id(1) == 0)
        def _():
            dkr_ref[...] = jnp.zeros_like(dkr_ref)

        for qi in range(nb):
            rows = slice(qi * blk, (qi + 1) * blk)
            qb = q_ref[rows, :]
            dob = do_ref[rows, :]
            delta = jnp.sum(dob.astype(F32) * o_ref[rows, :].astype(F32), axis=-1, keepdims=True)
            lse_b = lse_ref[0, rows, :]

            def part(ks, diagonal):
                k = _keys(kv_ref, kr_ref, ks)
                s = lax.dot_general(qb, k, _NT, preferred_element_type=F32)
                if diagonal:
                    s = _causal(s)
                p = jnp.exp(s - lse_b)
                dp = lax.dot_general(dob, kv_ref[ks, QK_NOPE_DIM:], _NT, preferred_element_type=F32)
                ds = (p * (dp - delta)).astype(BF16)
                dv_s[ks, :] += lax.dot_general(p.astype(BF16), dob, _TN, preferred_element_type=F32)
                dk_s[ks, :] += lax.dot_general(ds, qb, _TN, preferred_element_type=F32)
                return jnp.dot(ds, k, preferred_element_type=F32)

            dq = part(rows, True)
            if qi > 0:
                dq = dq + part(slice(0, qi * blk), False)
            dq_ref[rows, :QK_NOPE_DIM] = (dq[:, :QK_NOPE_DIM] * scale).astype(BF16)
            dq_ref[rows, QK_NOPE_DIM:] = _rope_t(dq[:, QK_NOPE_DIM:], c_ref[rows, :] * scale,
                                                 s_ref[rows, :] * scale).astype(BF16)

        dkv_ref[:, :QK_NOPE_DIM] = dk_s[:, :QK_NOPE_DIM].astype(BF16)
        dkv_ref[:, QK_NOPE_DIM:] = dv_s[...].astype(BF16)
        dkr_ref[...] += dk_s[:, QK_NOPE_DIM:]

    head = lambda w: pl.BlockSpec((S, w), lambda b, h: (b, h))
    shared = pl.BlockSpec((S, LANES), lambda b, h: (b, 0))
    return pl.pallas_call(
        body, name=name, grid=(B, H),
        in_specs=[head(HEAD_PAD), head(HEAD_PAD), shared, head(V_HEAD_DIM), head(V_HEAD_DIM),
                  pl.BlockSpec((1, S, 1), lambda b, h: (h, b, 0)), shared, shared],
        out_specs=[head(HEAD_PAD), head(HEAD_PAD), shared],
        out_shape=[jax.ShapeDtypeStruct((T, H * HEAD_PAD), BF16), jax.ShapeDtypeStruct((T, H * HEAD_PAD), BF16),
                   jax.ShapeDtypeStruct((T, LANES), F32)],
        scratch_shapes=[pltpu.VMEM((S, HEAD_PAD), F32), pltpu.VMEM((S, V_HEAD_DIM), F32)],
        compiler_params=_cparams(("parallel", "arbitrary")),
    )(q, kv, kr, o, do, lse, cs, sn)


def adamw(w, g, m, v, *, name, dep=None, copy_g=False):
    R, C = w.shape
    cap = max(8, (1024 * 1024) // (C * 4))
    tr = _pick(R, tuple(c for c in (1024, 512, 256, 128, 64, 32, 16, 8) if c <= cap))
    c1 = 1.0 - ADAM_B1 ** ADAM_STEP
    c2 = 1.0 - ADAM_B2 ** ADAM_STEP
    nout = 4 if copy_g else 3

    def body(w_ref, g_ref, m_ref, v_ref, *rest):
        d_ref, nm_ref, nv_ref = rest[-3:]
        gv = g_ref[...]
        if copy_g:
            rest[-4][...] = gv
        mv = ADAM_B1 * m_ref[...] + (1.0 - ADAM_B1) * gv
        vv = ADAM_B2 * v_ref[...] + (1.0 - ADAM_B2) * (gv * gv)
        nm_ref[...] = mv
        nv_ref[...] = vv
        d_ref[...] = -ADAM_LR * ((mv / c1) / (jnp.sqrt(vv / c2) + ADAM_EPS) + ADAM_WD * w_ref[...])

    blk = pl.BlockSpec((tr, C), lambda i: (i, 0))
    sh = jax.ShapeDtypeStruct((R, C), F32)
    extra = [] if dep is None else [dep]
    return pl.pallas_call(
        body, name=name, grid=(R // tr,), in_specs=[blk] * 4 + [ANY] * len(extra), out_specs=[blk] * nout,
        out_shape=[sh] * nout, compiler_params=_cparams(("parallel",)),
    )(w, g, m, v, *extra)


ANY = pl.BlockSpec(memory_space=pl.ANY)


def _place():
    x, y, c = lax.axis_index("x"), lax.axis_index("y"), lax.axis_index("c")
    others = [(1 - x, y), (x, 1 - y), (1 - x, 1 - y)]
    return x, y, c, others


def _remote(src, dst, ssem, rsem, dev):
    return pltpu.make_async_remote_copy(src_ref=src, dst_ref=dst, send_sem=ssem, recv_sem=rsem,
                                        device_id=dev, device_id_type=MESH)


def _half(ref_rows, c):
    hr = ref_rows // 2
    return pl.ds(pl.multiple_of(c * hr, 16), hr)


HBM = pl.BlockSpec(memory_space=pltpu.HBM)
SEMS = pl.BlockSpec(memory_space=pltpu.SEMAPHORE)
EFFECT = pltpu.SideEffectType.DATAFLOW_SIDE_EFFECTING


def exchange_begin(name, srcs, land_shapes, plan, ncopies, after=None):
    ns, nl = len(srcs), len(land_shapes)
    nin = ns + nl + (0 if after is None else 1)

    def body(*refs):
        ssem, rsem = refs[nin], refs[nin + 1]
        for k, (s, d, dev) in enumerate(plan(refs[:ns], refs[ns:ns + nl])):
            _remote(s, d, ssem.at[k], rsem.at[k], dev).start()
        refs[-1][...] = jnp.zeros_like(refs[-1])

    bufs = [pltpu.HBM(s.shape, s.dtype) for s in srcs] + [pltpu.HBM(s.shape, s.dtype) for s in land_shapes]
    args = [pltpu.with_memory_space_constraint(s, pltpu.HBM) for s in srcs]
    args += [pltpu.with_memory_space_constraint(lax.empty(s.shape, s.dtype), pltpu.HBM) for s in land_shapes]
    if after is not None:
        args.append(after)
    out = pl.pallas_call(
        body, name=name,
        out_shape=(pltpu.SemaphoreType.DMA((ncopies,)), pltpu.SemaphoreType.DMA((ncopies,)), *bufs,
                   jax.ShapeDtypeStruct((8, LANES), F32)),
        in_specs=[HBM] * (ns + nl) + ([] if after is None else [ANY]),
        out_specs=(SEMS, SEMS, *([HBM] * (ns + nl)), pl.BlockSpec(memory_space=pltpu.VMEM)),
        input_output_aliases={i: 2 + i for i in range(ns + nl)},
        compiler_params=pltpu.CompilerParams(has_side_effects=EFFECT),
    )(*args)
    return (out[0], out[1], out[2:2 + ns], out[2 + ns:2 + ns + nl]), out[-1]


def exchange_end(name, handle, plan, after):
    ssem, rsem, srcs, lands = handle
    ns, nl = len(srcs), len(lands)

    def body(*refs):
        ssem_ref, rsem_ref = refs[ns + nl], refs[ns + nl + 1]
        for k, (s, d, dev) in enumerate(plan(refs[:ns], refs[ns:ns + nl])):
            cp = _remote(s, d, ssem_ref.at[k], rsem_ref.at[k], dev)
            cp.wait_send()
            cp.wait_recv()

    out = pl.pallas_call(
        body, name=name,
        out_shape=tuple(pltpu.HBM(s.shape, s.dtype) for s in (*srcs, *lands)),
        in_specs=[HBM] * (ns + nl) + [SEMS, SEMS, ANY], out_specs=tuple([HBM] * (ns + nl)),
        input_output_aliases={i: i for i in range(ns + nl)},
        compiler_params=pltpu.CompilerParams(has_side_effects=EFFECT),
    )(*srcs, *lands, ssem, rsem, after)
    return list(out[:ns]), list(out[ns:])


def ag_plan(src_refs, land_refs):
    x, y, c, others = _place()
    plan = []
    for s, d in zip(src_refs, land_refs):
        mine = _half(s.shape[0], c)
        for ox, oy in others:
            plan.append((s.at[mine, :], d.at[2 * x + y, mine, :], (ox, oy, c)))
    return plan


def ag_forward_plan(src_refs, land_refs):
    x, y, c, others = _place()
    plan = []
    for s in src_refs:
        mine = _half(s.shape[1], c)
        for ox, oy in others:
            blk = s.at[2 * ox + oy, mine, :]
            plan.append((blk, blk, (x, y, 1 - c)))
    return plan


def rs_swap_plan(src_refs, land_refs):
    x, y, c, _ = _place()
    return [(s.at[:, _half(s.shape[1], 1 - c), :], d, (x, y, 1 - c)) for s, d in zip(src_refs, land_refs)]


def ag_forward(lands, *, name):
    n = len(lands)

    def body(*refs):
        ins, outs = refs[:n], refs[n:2 * n]
        ssem, rsem = refs[2 * n:]
        x, y, c, others = _place()
        sent = []
        for i in range(n):
            mine = _half(ins[i].shape[1], c)
            for j, (ox, oy) in enumerate(others):
                cp = _remote(ins[i].at[2 * ox + oy, mine, :], outs[i].at[2 * ox + oy, mine, :], ssem.at[3 * i + j],
                             rsem.at[3 * i + j], (x, y, 1 - c))
                cp.start()
                sent.append(cp)
        for cp in sent:
            cp.wait()

    return pl.pallas_call(
        body, name=name, in_specs=[ANY] * n, out_specs=[ANY] * n,
        out_shape=[jax.ShapeDtypeStruct(a.shape, a.dtype) for a in lands],
        input_output_aliases={i: i for i in range(n)},
        scratch_shapes=[pltpu.SemaphoreType.DMA((3 * n,)), pltpu.SemaphoreType.DMA((3 * n,))],
        compiler_params=pltpu.CompilerParams(has_side_effects=True),
    )(*lands)


def place_own(lands, own, chip, *, name):
    n = len(lands)
    steps = 4

    def body(chip_ref, *refs):
        for i in range(n):
            refs[2 * n + i][0] = refs[i][...]

    in_specs = [pl.BlockSpec((o.shape[0] // steps, o.shape[1]), lambda t, q: (t, 0)) for o in own] + [ANY] * n
    out_specs = [pl.BlockSpec((1, o.shape[0] // steps, o.shape[1]), lambda t, q: (q[0], t, 0)) for o in own]
    return pl.pallas_call(
        body, name=name,
        grid_spec=pltpu.PrefetchScalarGridSpec(num_scalar_prefetch=1, grid=(steps,), in_specs=in_specs,
                                               out_specs=out_specs),
        out_shape=[jax.ShapeDtypeStruct(a.shape, a.dtype) for a in lands],
        input_output_aliases={1 + n + i: i for i in range(n)},
        compiler_params=_cparams(("parallel",)),
    )(chip, *own, *lands)


def rs_swap_halves(grads, *, name, after=None):
    n = len(grads)
    extra = [] if after is None else [after]
    nin = n + len(extra)

    def body(*refs):
        ins, outs = refs[:n], refs[nin:nin + n]
        ssem, rsem = refs[nin + n:]
        x, y, c, _ = _place()
        cps = []
        for i in range(n):
            theirs = _half(ins[i].shape[1], 1 - c)
            cp = _remote(ins[i].at[:, theirs, :], outs[i], ssem.at[i], rsem.at[i], (x, y, 1 - c))
            cp.start()
            cps.append(cp)
        for cp in cps:
            cp.wait()

    return pl.pallas_call(
        body, name=name, in_specs=[ANY] * nin, out_specs=[ANY] * n,
        out_shape=[jax.ShapeDtypeStruct((4, g.shape[1] // 2, g.shape[2]), g.dtype) for g in grads],
        scratch_shapes=[pltpu.SemaphoreType.DMA((n,)), pltpu.SemaphoreType.DMA((n,))],
        compiler_params=pltpu.CompilerParams(has_side_effects=True),
    )(*grads, *extra)


def rs_chip_sum(g, r1, core, *, name):
    _, r, cdim = g.shape
    hr = r // 2

    def body(c_ref, g_ref, r1_ref, o_ref):
        o_ref[...] = (g_ref[...].astype(F32) + r1_ref[...].astype(F32)).astype(BF16)

    return pl.pallas_call(
        body, name=name,
        grid_spec=pltpu.PrefetchScalarGridSpec(
            num_scalar_prefetch=1, grid=(4,),
            in_specs=[pl.BlockSpec((1, hr, cdim), lambda qq, c_ref: (qq, c_ref[0], 0)),
                      pl.BlockSpec((1, hr, cdim), lambda qq, c_ref: (qq, 0, 0))],
            out_specs=pl.BlockSpec((1, hr, cdim), lambda qq, c_ref: (qq, 0, 0))),
        out_shape=jax.ShapeDtypeStruct((4, hr, cdim), BF16),
        compiler_params=_cparams(("parallel",)),
    )(core, g, r1)


def rs_plan(src_refs, land_refs):
    x, y, c, others = _place()
    plan = []
    for s, d in zip(src_refs, land_refs):
        for j, (ox, oy) in enumerate(others):
            plan.append((s.at[2 * ox + oy], d.at[j], (ox, oy, c)))
    return plan


def rs_final_sum(g, r1, r2, place, acc, l, *, name):
    _, r, cdim = g.shape
    hr = r // 2
    ch = hr // 2

    def body(p_ref, g_ref, r1_ref, a_ref, b_ref, d_ref, acc_in, o_ref):
        s = g_ref[...].astype(F32) + r1_ref[...].astype(F32)
        s = s + a_ref[...].astype(F32)
        s = s + b_ref[...].astype(F32)
        o_ref[...] = s + d_ref[...].astype(F32)

    other = lambda j: pl.BlockSpec((1, ch, cdim), lambda t, p_ref: (j, t, 0))
    return pl.pallas_call(
        body, name=name,
        grid_spec=pltpu.PrefetchScalarGridSpec(
            num_scalar_prefetch=1, grid=(2,),
            in_specs=[pl.BlockSpec((1, ch, cdim), lambda t, p_ref: (p_ref[0], 2 * p_ref[1] + t, 0)),
                      pl.BlockSpec((1, ch, cdim), lambda t, p_ref: (p_ref[0], t, 0)),
                      other(0), other(1), other(2), ANY],
            out_specs=pl.BlockSpec((1, ch, cdim), lambda t, p_ref: (l, 2 * p_ref[1] + t, 0))),
        out_shape=jax.ShapeDtypeStruct(acc.shape, F32),
        input_output_aliases={6: 0},
        compiler_params=_cparams(("parallel",)),
    )(place, g, r1, r2, r2, r2, acc)


def rs_join_halves(grads, *, name):
    n = len(grads)

    def body(*refs):
        ins, outs = refs[:n], refs[n:2 * n]
        ssem, rsem = refs[2 * n:]
        x, y, c, _ = _place()
        cps = []
        for i in range(n):
            mine = _half(ins[i].shape[1], c)
            cp = _remote(ins[i].at[:, mine, :], outs[i].at[:, mine, :], ssem.at[i], rsem.at[i], (x, y, 1 - c))
            cp.start()
            cps.append(cp)
        for cp in cps:
            cp.wait()

    return pl.pallas_call(
        body, name=name, in_specs=[ANY] * n, out_specs=[ANY] * n,
        out_shape=[jax.ShapeDtypeStruct(g.shape, g.dtype) for g in grads],
        input_output_aliases={i: i for i in range(n)},
        scratch_shapes=[pltpu.SemaphoreType.DMA((n,)), pltpu.SemaphoreType.DMA((n,))],
        compiler_params=pltpu.CompilerParams(has_side_effects=True),
    )(*grads)


def all_reduce_small(v):
    R = v.shape[0]
    rs = R // 8

    def body(v_ref, o_ref, buf, ssem1, rsem1, ssem2, rsem2):
        x, y, c, _ = _place()
        me = 4 * x + 2 * y + c
        mine = pl.ds(pl.multiple_of(me * rs, 8), rs)
        peers = []
        for k in range(1, 8):
            px = jnp.where((k >> 2) & 1 == 1, 1 - x, x)
            py = jnp.where((k >> 1) & 1 == 1, 1 - y, y)
            pc = jnp.where(k & 1 == 1, 1 - c, c)
            peers.append((px, py, pc))
        buf[me] = v_ref[mine, :]
        cps = []
        for k, (px, py, pc) in enumerate(peers):
            theirs = pl.ds(pl.multiple_of((4 * px + 2 * py + pc) * rs, 8), rs)
            cp = _remote(v_ref.at[theirs, :], buf.at[me], ssem1.at[k], rsem1.at[k], (px, py, pc))
            cp.start()
            cps.append(cp)
        for cp in cps:
            cp.wait()
        acc = buf[0]
        for d in range(1, 8):
            acc = acc + buf[d]
        o_ref[mine, :] = acc
        cps = []
        for k, peer in enumerate(peers):
            cp = _remote(o_ref.at[mine, :], o_ref.at[mine, :], ssem2.at[k], rsem2.at[k], peer)
            cp.start()
            cps.append(cp)
        for cp in cps:
            cp.wait()

    vm = pl.BlockSpec(memory_space=pltpu.VMEM)
    return pl.pallas_call(
        body, name="all_reduce_small", in_specs=[vm], out_specs=vm,
        out_shape=jax.ShapeDtypeStruct((R, LANES), F32),
        scratch_shapes=[pltpu.VMEM((8, rs, LANES), F32)] + [pltpu.SemaphoreType.DMA((7,))] * 4,
        compiler_params=pltpu.CompilerParams(vmem_limit_bytes=VMEM_LIMIT, has_side_effects=True),
    )(v)


def _to_stacked(name, full):
    R, C = full.shape
    if name in ROW_SHARDED:
        return full.reshape(4, R // 4, C)
    return jnp.transpose(full.reshape(R, 4, C // 4), (1, 0, 2))


def _from_stacked(name, st):
    _, r, c = st.shape
    if name in ROW_SHARDED:
        return st.reshape(4 * r, c)
    return jnp.transpose(st, (1, 0, 2)).reshape(r, 4 * c)


UP_PIECES = ("ffn1_up", "ffn2_up")


def _layer_weights(lands):
    w = {n: lands[n] if n in UP_PIECES else _from_stacked(n, lands[n]) for n in lands}
    if "w_in" not in w:
        return w
    win = w.pop("w_in")
    D = win.shape[0]
    p0, p1, p2, p3 = POOL_DIM, POOL_DIM + Q_LORA_RANK, POOL_DIM + Q_LORA_RANK + KV_LORA_RANK, \
        POOL_DIM + Q_LORA_RANK + KV_LORA_RANK + QK_ROPE_DIM
    w["w_pool"] = win[:, :p0]
    w["w_lat"] = jnp.concatenate([win[:, p0:p3], jnp.zeros((D, LAT_DIM - (p3 - p0)), win.dtype)], axis=1)
    w["w_gate"] = win[:, p3:]
    uq = w["w_uq"].reshape(Q_LORA_RANK, N_HEADS, QK_DIM)
    w["w_uq"] = jnp.concatenate([uq, jnp.zeros((Q_LORA_RANK, N_HEADS, HEAD_PAD - QK_DIM), uq.dtype)],
                                axis=2).reshape(Q_LORA_RANK, N_HEADS * HEAD_PAD)
    return w


def _layer_grads_stacked(dw):
    dw = dict(dw)
    if "w_lat" in dw:
        lat = dw.pop("w_lat")
        dw["w_in"] = jnp.concatenate([dw.pop("w_pool"), lat[:, :Q_LORA_RANK + KV_LORA_RANK + QK_ROPE_DIM],
                                      dw.pop("w_gate")], axis=1)
        dw["w_uq"] = dw["w_uq"].reshape(Q_LORA_RANK, N_HEADS, HEAD_PAD)[:, :, :QK_DIM].reshape(Q_LORA_RANK,
                                                                                                 N_HEADS * QK_DIM)
    return {n: dw[n] if n in UP_PIECES else _to_stacked(n, dw[n]) for n in BIG if n in dw}


def _rope_tables(positions):
    inv_freq = ROPE_THETA ** (-jnp.arange(0, QK_ROPE_DIM, 2, dtype=F32) / QK_ROPE_DIM)
    ang = positions.astype(F32).reshape(-1)[:, None] * inv_freq
    cos, sin = jnp.cos(ang), jnp.sin(ang)
    z = jnp.zeros((ang.shape[0], LANES - QK_ROPE_DIM), F32)
    return jnp.concatenate([cos, cos, z], axis=1), jnp.concatenate([-sin, sin, z], axis=1)


def _pack_small(vals):
    parts, total = [], 0
    for n in SMALL:
        f = vals[n].reshape(-1).astype(F32)
        pad = (-f.shape[0]) % (8 * LANES)
        parts.append(jnp.pad(f, (0, pad)))
        total += f.shape[0] + pad
    parts.append(jnp.zeros(((-total) % (64 * LANES),), F32))
    return jnp.concatenate(parts).reshape(-1, LANES)


def _unpack_small(packed, like):
    flat = packed.reshape(-1)
    out, off = {}, 0
    for n in SMALL:
        size = like[n].size
        out[n] = flat[off:off + size].reshape(like[n].shape)
        off += size + ((-size) % (8 * LANES))
    return out


def _ffn_fwd(x, g, wu4, wd, tag, dep=None):
    h = rms_fwd(x, g, dep=dep, name=f"{tag}_norm")
    gate, up, a = ffn_up_act(h, wu4, name=f"{tag}_up_act")
    y = mm(a, wd, res=x, alpha=0.5, name=f"{tag}_down")
    return y, (x, h, gate, up, a)


def _ffn_bwd(dy, saved, g, wu4, wd, tag, dep=None):
    x, h, gate, up, a = saved
    dgate, dup = ffn_down_dx_act(dy, wd, gate, up, dep=dep, name=f"{tag}_down_dx_act")
    dwd = mm(a, dy, ta=True, alpha=0.5, out_dtype=BF16, name=f"{tag}_down_dw")
    dwu4 = ffn_up_dw(h, dgate, dup, name=f"{tag}_up_dw")
    dx, dg = ffn_up_dx_norm(dgate, dup, wu4, x, g, dy, name=f"{tag}_up_dx_norm")
    return dx, dg, dwu4, dwd


def _mix_fwd(x, p, w, cs, sn, B, dep=None):
    h = rms_fwd(x, p["norm_mix"], dep=dep, name="mix_norm")
    lat, xp, gl = mix_in(h, w["w_lat"], w["w_pool"], w["w_gate"], name="mix_in")
    mixed = pool_fwd(xp, p["pool_maps"].astype(BF16), p["pool_scale"], B, name="pool_fwd")
    ba = mm(mixed, w["w_pool_proj"], out_dtype=BF16, name="mix_pool_proj")
    qn, kvn, kr = lat_fwd(lat, p["q_latent_norm"], p["kv_latent_norm"], cs, sn, name="lat_fwd")
    kv = mm(kvn, w["w_ukv"], out_dtype=BF16, name="mix_ukv")
    o, lse, q = attn_fwd(mm(qn, w["w_uq"], name="mix_uq"), kv, kr, cs, sn, B, name="attn_fwd")
    bb, merged = attn_proj_gate(o, w["w_attn_proj"], gl, p["b_gate"], ba, name="mix_attn_proj_gate")
    y = mm(merged, w["w_out"], res=x, name="mix_out")
    return y, (x, h, lat, xp, gl, mixed, ba, qn, kvn, kr, q, kv, o, lse, bb, merged)


def _mix_bwd(dy, saved, p, w, cs, sn, B, dep=None):
    x, h, lat, xp, gl, mixed, ba, qn, kvn, kr, q, kv, o, lse, bb, merged = saved
    dw, ds = {}, {}
    dw["w_out"] = mm(merged, dy, ta=True, out_dtype=BF16, dep=dep, name="mix_out_dw")
    dba, dbb, dgl, ds["b_gate"] = out_dx_gate(dy, w["w_out"], gl, p["b_gate"], ba, bb, name="mix_out_dx_gate")
    dw["w_attn_proj"] = mm(o, dbb, ta=True, out_dtype=BF16, name="mix_attn_proj_dw")
    do = mm(dbb, w["w_attn_proj"], tb=True, out_dtype=BF16, name="mix_attn_proj_dx")
    dw["w_pool_proj"] = mm(mixed, dba, ta=True, out_dtype=BF16, name="mix_pool_proj_dw")
    dmixed = mm(dba, w["w_pool_proj"], tb=True, name="mix_pool_proj_dx")
    dxp, ds["pool_maps"], ds["pool_scale"] = pool_bwd(xp, dmixed, p["pool_maps"].astype(BF16), p["pool_scale"], B,
                                                      name="pool_bwd")
    dqb, dkv, dkr = attn_bwd(q, kv, kr, o, do, lse, cs, sn, B, name="attn_bwd")
    dw["w_ukv"] = mm(kvn, dkv, ta=True, out_dtype=BF16, name="mix_ukv_dw")
    dkvn = mm(dkv, w["w_ukv"], tb=True, name="mix_ukv_dx")
    dw["w_uq"] = mm(qn, dqb, ta=True, out_dtype=BF16, name="mix_uq_dw")
    dqn = mm(dqb, w["w_uq"], tb=True, name="mix_uq_dx")
    dlat, ds["q_latent_norm"], ds["kv_latent_norm"] = lat_bwd(lat, p["q_latent_norm"], p["kv_latent_norm"], dqn, dkvn,
                                                               dkr, cs, sn, name="lat_bwd")
    dw["w_lat"] = mm(h, dlat, ta=True, out_dtype=BF16, name="mix_lat_dw")
    dw["w_pool"] = mm(h, dxp, ta=True, out_dtype=BF16, name="mix_pool_in_dw")
    dw["w_gate"] = mm(h, dgl, ta=True, out_dtype=BF16, name="mix_gate_in_dw")
    dx, ds["norm_mix"] = mix_in_dx_norm(dlat, dxp, dgl, w["w_lat"], w["w_pool"], w["w_gate"], x, p["norm_mix"], dy,
                                        name="mix_in_dx_norm")
    return dx, dw, ds


def kernel(x, positions, norm_ffn1, ffn1_up, ffn1_down, norm_mix, w_in, b_gate, pool_maps, pool_scale, w_pool_proj, q_latent_norm, w_uq, kv_latent_norm, w_ukv, w_attn_proj, w_out, norm_ffn2, ffn2_up, ffn2_down, final_norm, loss_target, m_norm_ffn1, m_ffn1_up, m_ffn1_down, m_norm_mix, m_w_in, m_b_gate, m_pool_maps, m_pool_scale, m_w_pool_proj, m_q_latent_norm, m_w_uq, m_kv_latent_norm, m_w_ukv, m_w_attn_proj, m_w_out, m_norm_ffn2, m_ffn2_up, m_ffn2_down, m_final_norm, v_norm_ffn1, v_ffn1_up, v_ffn1_down, v_norm_mix, v_w_in, v_b_gate, v_pool_maps, v_pool_scale, v_w_pool_proj, v_q_latent_norm, v_w_uq, v_kv_latent_norm, v_w_ukv, v_w_attn_proj, v_w_out, v_norm_ffn2, v_ffn2_up, v_ffn2_down, v_final_norm):
    given = dict(locals())
    B, S, D = x.shape
    T = B * S
    L = norm_ffn1.shape[0]
    W = {n: given[n] for n in WEIGHTS}
    Mo = {n: given["m_" + n] for n in WEIGHTS}
    Vo = {n: given["v_" + n] for n in WEIGHTS}
    core = lax.axis_index("c").astype(jnp.int32)
    chip = (2 * lax.axis_index("x") + lax.axis_index("y")).astype(jnp.int32)

    core_arr = core.reshape(1)
    chip_arr = chip.reshape(1)
    place = jnp.stack([chip, core])
    first = ("ffn1_up", "ffn1_down")
    rest = tuple(n for n in BIG if n not in first)

    own = [{n: W[n][l].astype(BF16) for n in BIG} for l in range(L)]

    def ag_begin(l, names, tag, after=None):
        lands = [jax.ShapeDtypeStruct((4,) + own[l][n].shape, BF16) for n in names]
        return exchange_begin(f"ag_start_{tag}", [own[l][n] for n in names], lands, ag_plan, 3 * len(names), after)

    def ag_finish(handle, names, tag, after):
        mine, lands = exchange_end(f"ag_wait_{tag}", handle, ag_plan, after)
        lands = ag_forward(lands, name=f"ag_forward_{tag}")
        return _layer_weights(dict(zip(names, place_own(lands, mine, chip_arr, name=f"place_own_{tag}"))))

    h_first, t1 = ag_begin(0, first, "0a")
    cs, sn = _rope_tables(positions)
    xs = x.reshape(T, D) + t1[0, 0]
    saved, ici, handed = [], None, None
    for l in range(L):
        p = {n: W[n][l] for n in SMALL if n != "final_norm"}
        dep = None
        if l == 0:
            w = ag_finish(h_first, first, "0a", xs)
            h_rest, dep = ag_begin(0, rest, "0b", after=w["ffn1_down"])
        else:
            mine, handle = handed
            lands, _ = exchange_end(f"ag_forward_wait_{l}", handle, ag_forward_plan, xs)
            w = _layer_weights(dict(zip(BIG, place_own(lands, mine, chip_arr, name=f"place_own_{l}"))))
        xs, s1 = _ffn_fwd(xs, p["norm_ffn1"], w["ffn1_up"], w["ffn1_down"], "ffn1", dep=dep)
        dep = None
        if l == 0:
            w.update(ag_finish(h_rest, rest, "0b", xs))
            if L > 1:
                ici, dep = ag_begin(1, BIG, "1", after=xs)
        xs, s2 = _mix_fwd(xs, p, w, cs, sn, B, dep=dep)
        dep = None
        if l + 1 < L:
            mine, lands = exchange_end(f"ag_wait_{l + 1}", ici, ag_plan, xs)
            handle, dep = exchange_begin(f"ag_forward_start_{l + 1}", lands, [], ag_forward_plan, 3 * len(BIG))
            handed = (mine, handle)
            if l + 2 < L:
                ici, dep = ag_begin(l + 2, BIG, str(l + 2), after=dep)
        xs, s3 = _ffn_fwd(xs, p["norm_ffn2"], w["ffn2_up"], w["ffn2_down"], "ffn2", dep=dep)
        saved.append((w, p, s1, s2, s3))

    dx, dfinal, loss_tile = loss_head(xs, final_norm, loss_target.reshape(T, D), name="loss_head")
    loss = lax.psum(loss_tile[0, 0], ("x", "y", "c"))

    def rs_begin(dw, tag, after=None):
        stacked = _layer_grads_stacked(dw)
        names = tuple(stacked)
        parts = [stacked[n] for n in names]
        r1 = rs_swap_halves(parts, after=after, name=f"rs_swap_{tag}")
        sums = [rs_chip_sum(g, a, core_arr, name=f"rs_chip_sum_{n}") for n, g, a in zip(names, parts, r1)]
        lands = [jax.ShapeDtypeStruct((3,) + s.shape[1:], BF16) for s in sums]
        handle, token = exchange_begin(f"rs_start_{tag}", sums, lands, rs_plan, 3 * len(names))
        return (names, parts, r1, handle), token

    acc = {n: lax.empty(W[n].shape, F32) for n in BIG}

    def rs_finish(l, pending, tag, after):
        names, parts, r1, handle = pending
        _, r2 = exchange_end(f"rs_wait_{tag}", handle, rs_plan, after)
        for n, g, a, b in zip(names, parts, r1, r2):
            acc[n] = rs_final_sum(g, a, b, place, acc[n], l, name=f"rs_final_sum_{n}")

    small_layers, pending, swapping, dep = [], [], None, None
    for l in reversed(range(L)):
        w, p, s1, s2, s3 = saved[l]
        dx, dg2, dwu2, dwd2 = _ffn_bwd(dx, s3, p["norm_ffn2"], w["ffn2_up"], w["ffn2_down"], "ffn2", dep=dep)
        dep = None
        if swapping is not None:
            above, names, handle = swapping
            parts, r1 = exchange_end(f"rs_swap_wait_{above}", handle, rs_swap_plan, dx)
            sums = [rs_chip_sum(g, a, core_arr, name=f"rs_chip_sum_{n}") for n, g, a in zip(names, parts, r1)]
            lands = [jax.ShapeDtypeStruct((3,) + s.shape[1:], BF16) for s in sums]
            handle, dep = exchange_begin(f"rs_start_{above}", sums, lands, rs_plan, 3 * len(names))
            pending.append((above, (names, parts, r1, handle)))
            swapping = None
        dx, dw, ds = _mix_bwd(dx, s2, p, w, cs, sn, B, dep=dep)
        dw.update(ffn2_up=dwu2, ffn2_down=dwd2)
        dep = None
        if l == 0:
            early, dep = rs_begin(dw, "0b")
            dw = {}
        dx, dg1, dwu1, dwd1 = _ffn_bwd(dx, s1, p["norm_ffn1"], w["ffn1_up"], w["ffn1_down"], "ffn1", dep=dep)
        dw.update(ffn1_up=dwu1, ffn1_down=dwd1)
        ds.update(norm_ffn1=dg1, norm_ffn2=dg2)
        small_layers.append(ds)
        if l == 0:
            last_dw = dw
        else:
            stacked = _layer_grads_stacked(dw)
            halves = [jax.ShapeDtypeStruct((4, g.shape[1] // 2, g.shape[2]), BF16) for g in stacked.values()]
            handle, dep = exchange_begin(f"rs_swap_start_{l}", list(stacked.values()), halves, rs_swap_plan,
                                         len(stacked))
            swapping = (l, tuple(stacked), handle)
    small_layers.reverse()

    small = {n: jnp.stack([small_layers[l][n].reshape(W[n].shape[1:]) for l in range(L)]) for n in SMALL
             if n != "final_norm"}
    small["final_norm"] = dfinal.reshape(final_norm.shape)
    reduced = all_reduce_small(_pack_small(small))
    grads = _unpack_small(reduced, small)
    last, dep = rs_begin(last_dw, "0a", after=reduced)
    delta, new_m, new_v = {}, {}, {}
    d, nm, nv = adamw(_pack_small(W), reduced, _pack_small(Mo), _pack_small(Vo), dep=dep, name="adamw_small")
    delta.update(_unpack_small(d, W))
    new_m.update(_unpack_small(nm, W))
    new_v.update(_unpack_small(nv, W))

    def update(names, tag, d):
        joined = rs_join_halves([acc[n] for n in names], name=f"rs_join_{tag}")
        for n, g in zip(names, joined):
            flip = (lambda a: jnp.swapaxes(a, 1, 2)) if n == "w_in" else (lambda a: a)
            sh = flip(W[n]).shape
            two = lambda a: flip(a).reshape(sh[0] * sh[1], sh[2])
            back = lambda a: flip(a.reshape(sh))
            gc, d, nm, nv = adamw(two(W[n]), two(g), two(Mo[n]), two(Vo[n]), dep=d, copy_g=True, name=f"adamw_{n}")
            grads[n], delta[n], new_m[n], new_v[n] = back(gc), back(d), back(nm), back(nv)
        return d

    for l, item in pending:
        rs_finish(l, item, str(l), d)
    rs_finish(0, early, "0b", d)
    d = update(rest, "rest", d)
    rs_finish(0, last, "0a", d)
    update(first, "first", d)

    return (loss, dx.reshape(B, S, D), *[grads[n] for n in WEIGHTS], *[delta[n] for n in WEIGHTS],
            *[new_m[n] for n in WEIGHTS], *[new_v[n] for n in WEIGHTS])
```

```python
import functools

import jax
import jax.numpy as jnp
from jax import lax
from jax.experimental import pallas as pl
from jax.experimental.pallas import tpu as pltpu

F32 = jnp.float32
BF16 = jnp.bfloat16

N_HEADS = 8
QK_NOPE_DIM = 128
QK_ROPE_DIM = 64
QK_DIM = QK_NOPE_DIM + QK_ROPE_DIM
V_HEAD_DIM = 128
HEAD_PAD = 256
Q_LORA_RANK = 384
KV_LORA_RANK = 256
ROPE_THETA = 10000.0
POOL_WINDOWS = (2, 4, 8, 16)
N_POOL_GROUPS = 4
POOL_GROUP_DIM = 128
POOL_DIM = N_POOL_GROUPS * POOL_GROUP_DIM
LAT_DIM = 768
NORM_EPS = 1e-6
ADAM_LR = 0.001
ADAM_B1 = 0.9
ADAM_B2 = 0.999
ADAM_EPS = 1e-08
ADAM_WD = 0.01
ADAM_STEP = 10
NEG_INF = -1e30
LANES = 128
ATT_BLOCK = 512
VMEM_LIMIT = 48 * 1024 * 1024
MESH = pl.DeviceIdType.MESH
_NT = (((1,), (1,)), ((), ()))
_TN = (((0,), (0,)), ((), ()))

BIG = ("ffn1_up", "ffn1_down", "w_in", "w_pool_proj", "w_uq", "w_ukv", "w_attn_proj", "w_out",
       "ffn2_up", "ffn2_down")
ROW_SHARDED = ("ffn1_down", "w_attn_proj", "w_out", "ffn2_down")
SMALL = ("norm_ffn1", "norm_mix", "b_gate", "pool_maps", "pool_scale", "q_latent_norm",
         "kv_latent_norm", "norm_ffn2", "final_norm")
PACKED = tuple(n for n in SMALL if n != "pool_maps")
WEIGHTS = ("norm_ffn1", "ffn1_up", "ffn1_down", "norm_mix", "w_in", "b_gate", "pool_maps", "pool_scale",
           "w_pool_proj", "q_latent_norm", "w_uq", "kv_latent_norm", "w_ukv", "w_attn_proj", "w_out",
           "norm_ffn2", "ffn2_up", "ffn2_down", "final_norm")


def _pick(dim, cands):
    for c in cands:
        if c <= dim and dim % c == 0:
            return c
    return dim


def _cparams(sem=None, **kw):
    if sem is not None:
        kw["dimension_semantics"] = sem
    return pltpu.CompilerParams(vmem_limit_bytes=VMEM_LIMIT, **kw)


def _sigmoid(x):
    return 0.5 * jnp.tanh(0.5 * x) + 0.5


MM_TILE_BUDGET = 30 * 1024 * 1024
TILE_SIZES = (1408, 1024, 768, 512, 384, 256, 128)


V7X_MXU_FLOPS = 9.0e14
V7X_HBM_BYTES = 2.5e12
GRID_STEP_S = 0.35e-6


def _mm_tiles(M, N, K, sa, sb, so, sr):
    tks = [K] if K <= 2816 else [t for t in (2816, 2048, 1408, 1024, 512, 256, 128) if K % t == 0]
    best = None
    for tk in tks:
        for tm in [t for t in TILE_SIZES if M % t == 0] or [M]:
            for tn in [t for t in TILE_SIZES if N % t == 0] or [N]:
                need = 2 * (tm * tk * sa + tk * tn * sb + tm * tn * (so + sr)) + (tm * tn * 4 if tk < K else 0)
                if need > MM_TILE_BUDGET:
                    continue
                ni, nj, nk = M // tm, N // tn, K // tk
                a_bytes = M * K * sa * (nj if nk > 1 else 1)
                b_bytes = K * N * sb * (1 if nj == 1 and nk == 1 else ni)
                traffic = a_bytes + b_bytes + M * N * (so + sr) + (M * N * 8 * nk if nk > 1 else 0)
                t = max(2.0 * M * N * K / V7X_MXU_FLOPS, traffic / V7X_HBM_BYTES) + ni * nj * nk * GRID_STEP_S
                if best is None or t < best[0]:
                    best = (t, (tm, tn, tk))
    assert best is not None, (M, N, K)
    return best[1]


def mm(a, b, *, name, ta=False, tb=False, out_dtype=F32, res=None, alpha=1.0, dep=None):
    if ta:
        K, M = a.shape
    else:
        M, K = a.shape
    if tb:
        N, K2 = b.shape
    else:
        K2, N = b.shape
    assert K == K2, (a.shape, b.shape, ta, tb)
    tm, tn, tk = _mm_tiles(M, N, K, a.dtype.itemsize, b.dtype.itemsize, jnp.dtype(out_dtype).itemsize,
                           0 if res is None else res.dtype.itemsize)
    nk = K // tk
    dims = (((0 if ta else 1,), (1 if tb else 0,)), ((), ()))

    def body(*refs):
        a_ref, b_ref = refs[:2]
        r_ref = refs[2] if res is not None else None
        o_ref = refs[-2] if nk > 1 else refs[-1]

        def finish(r):
            if alpha != 1.0:
                r = r * alpha
            if res is not None:
                r = r_ref[...].astype(F32) + r
            o_ref[...] = r.astype(out_dtype)

        part = lax.dot_general(a_ref[...].astype(BF16), b_ref[...].astype(BF16), dims, preferred_element_type=F32)
        if nk == 1:
            finish(part)
            return
        acc = refs[-1]
        k = pl.program_id(2)

        @pl.when(k == 0)
        def _():
            acc[...] = part

        @pl.when(k > 0)
        def _():
            acc[...] += part

        @pl.when(k == nk - 1)
        def _():
            finish(acc[...])

    a_spec = pl.BlockSpec((tk, tm), lambda i, j, k: (k, i)) if ta else pl.BlockSpec((tm, tk), lambda i, j, k: (i, k))
    b_spec = pl.BlockSpec((tn, tk), lambda i, j, k: (j, k)) if tb else pl.BlockSpec((tk, tn), lambda i, j, k: (k, j))
    o_spec = pl.BlockSpec((tm, tn), lambda i, j, k: (i, j))
    in_specs = [a_spec, b_spec]
    args = [a, b]
    if res is not None:
        in_specs.append(o_spec)
        args.append(res)
    if dep is not None:
        in_specs.append(pl.BlockSpec((8, LANES), lambda i, j, k: (0, 0)))
        args.append(dep)
    return pl.pallas_call(
        body, name=name, grid=(M // tm, N // tn, nk), in_specs=in_specs, out_specs=o_spec,
        out_shape=jax.ShapeDtypeStruct((M, N), out_dtype),
        scratch_shapes=[pltpu.VMEM((tm, tn), F32)] if nk > 1 else [],
        compiler_params=_cparams(("parallel", "parallel", "arbitrary")),
    )(*args)


MXU_COLS = 256


def _col_chunks(n):
    return [(lo, min(lo + MXU_COLS, n)) for lo in range(0, n, MXU_COLS)]


def ffn_up_act(h, wu4, *, name):
    T, D = h.shape
    cq = wu4.shape[2]
    Fh = 2 * cq
    tm = _pick(T, (512, 256, 128))

    def body(h_ref, wg_ref, wu_ref, g_ref, u_ref, a_ref):
        hv = h_ref[...]
        for lo, hi in _col_chunks(cq):
            gv = jnp.dot(hv, wg_ref[0, :, lo:hi], preferred_element_type=F32)
            uv = jnp.dot(hv, wu_ref[0, :, lo:hi], preferred_element_type=F32)
            g_ref[:, lo:hi] = gv.astype(BF16)
            u_ref[:, lo:hi] = uv.astype(BF16)
            a_ref[:, lo:hi] = (gv * _sigmoid(gv) * uv).astype(BF16)

    tile = pl.BlockSpec((tm, cq), lambda j, i: (i, j))
    sh = jax.ShapeDtypeStruct((T, Fh), BF16)
    return pl.pallas_call(
        body, name=name, grid=(2, T // tm),
        in_specs=[pl.BlockSpec((tm, D), lambda j, i: (i, 0)), pl.BlockSpec((1, D, cq), lambda j, i: (j, 0, 0)),
                  pl.BlockSpec((1, D, cq), lambda j, i: (2 + j, 0, 0))],
        out_specs=[tile, tile, tile], out_shape=[sh, sh, sh],
        compiler_params=_cparams(("parallel", "parallel")),
    )(h, wu4, wu4)


def ffn_down_dx_act(dy, wd, g, u, *, dep=None, name):
    T, D = dy.shape
    Fh = wd.shape[0]
    cq = Fh // 2
    tm = _pick(T, (512, 256, 128))

    def body(dy_ref, wd_ref, g_ref, u_ref, *rest):
        dg_ref, du_ref = rest[-2:]
        dyv = dy_ref[...].astype(BF16)
        for lo, hi in _col_chunks(cq):
            da = 0.5 * lax.dot_general(dyv, wd_ref[lo:hi, :], _NT, preferred_element_type=F32)
            gv = g_ref[:, lo:hi].astype(F32)
            uv = u_ref[:, lo:hi].astype(F32)
            s = _sigmoid(gv)
            dg_ref[:, lo:hi] = (da * uv * (s * (1.0 + gv * (1.0 - s)))).astype(BF16)
            du_ref[:, lo:hi] = (da * (gv * s)).astype(BF16)

    tile = pl.BlockSpec((tm, cq), lambda j, i: (i, j))
    sh = jax.ShapeDtypeStruct((T, Fh), BF16)
    in_specs = [pl.BlockSpec((tm, D), lambda j, i: (i, 0)), pl.BlockSpec((cq, D), lambda j, i: (j, 0)), tile, tile]
    args = [dy, wd, g, u]
    if dep is not None:
        in_specs.append(pl.BlockSpec((8, LANES), lambda j, i: (0, 0)))
        args.append(dep)
    return pl.pallas_call(
        body, name=name, grid=(2, T // tm), in_specs=in_specs, out_specs=[tile, tile], out_shape=[sh, sh],
        compiler_params=_cparams(("parallel", "parallel")),
    )(*args)


def ffn_up_dw(h, dg, du, *, name):
    T, D = h.shape
    cq = dg.shape[1] // 2
    tk = _pick(T, (1024, 512, 256, 128))
    nk = T // tk

    def body(h_ref, dg_ref, du_ref, o_ref, acc):
        p = pl.program_id(0)
        k = pl.program_id(1)

        @pl.when(k == 0)
        def _():
            acc[...] = jnp.zeros_like(acc)

        @pl.when(p < 2)
        def _():
            acc[...] += lax.dot_general(h_ref[...], dg_ref[...], _TN, preferred_element_type=F32)

        @pl.when(p >= 2)
        def _():
            acc[...] += lax.dot_general(h_ref[...], du_ref[...], _TN, preferred_element_type=F32)

        @pl.when(k == nk - 1)
        def _():
            o_ref[0] = acc[...].astype(BF16)

    return pl.pallas_call(
        body, name=name, grid=(4, nk),
        in_specs=[pl.BlockSpec((tk, D), lambda p, k: (k, 0)),
                  pl.BlockSpec((tk, cq), lambda p, k: (jnp.where(p < 2, k, 0), jnp.minimum(p, 1))),
                  pl.BlockSpec((tk, cq), lambda p, k: (jnp.where(p < 2, 0, k), jnp.maximum(p - 2, 0)))],
        out_specs=pl.BlockSpec((1, D, cq), lambda p, k: (p, 0, 0)),
        out_shape=jax.ShapeDtypeStruct((4, D, cq), BF16),
        scratch_shapes=[pltpu.VMEM((D, cq), F32)],
        compiler_params=_cparams(("parallel", "arbitrary")),
    )(h, dg, du)


def ffn_up_dx_norm(dg, du, wu4, x, g, dy, *, name):
    T = dg.shape[0]
    _, D, cq = wu4.shape
    tm = _pick(T, (512, 256, 128))

    def body(dg_ref, du_ref, wg_ref, wu_ref, x_ref, g_ref, dy_ref, dx_ref, dgain_ref, acc):
        i = pl.program_id(0)
        k = pl.program_id(1)
        part = lax.dot_general(dg_ref[...], wg_ref[0], _NT, preferred_element_type=F32)
        part = part + lax.dot_general(du_ref[...], wu_ref[0], _NT, preferred_element_type=F32)

        @pl.when(jnp.logical_and(i == 0, k == 0))
        def _():
            dgain_ref[...] = jnp.zeros_like(dgain_ref)

        @pl.when(k == 0)
        def _():
            acc[...] = part

        @pl.when(k == 1)
        def _():
            dx, dgain = _rms_bwd_math(x_ref[...], g_ref[...], acc[...] + part)
            dx_ref[...] = dy_ref[...] + dx
            dgain_ref[...] += dgain

    tile = pl.BlockSpec((tm, cq), lambda i, k: (i, k))
    row = pl.BlockSpec((tm, D), lambda i, k: (i, 0))
    vec = pl.BlockSpec((1, D), lambda i, k: (0, 0))
    return pl.pallas_call(
        body, name=name, grid=(T // tm, 2),
        in_specs=[tile, tile, pl.BlockSpec((1, D, cq), lambda i, k: (k, 0, 0)),
                  pl.BlockSpec((1, D, cq), lambda i, k: (2 + k, 0, 0)), row, vec, row],
        out_specs=[row, vec],
        out_shape=[jax.ShapeDtypeStruct((T, D), F32), jax.ShapeDtypeStruct((1, D), F32)],
        scratch_shapes=[pltpu.VMEM((tm, D), F32)],
        compiler_params=_cparams(("arbitrary", "arbitrary")),
    )(dg, du, wu4, wu4, x, g.reshape(1, D), dy)


def mix_in(h, w_lat, w_pool, w_gate, *, name):
    T, D = h.shape
    tm = _pick(T, (512, 256, 128))

    def body(h_ref, wl_ref, wp_ref, wg_ref, lat_ref, xp_ref, gl_ref):
        hv = h_ref[...]
        lat_ref[...] = jnp.dot(hv, wl_ref[...], preferred_element_type=F32)
        xp_ref[...] = jnp.dot(hv, wp_ref[...], preferred_element_type=F32)
        gl_ref[...] = jnp.dot(hv, wg_ref[...], preferred_element_type=F32).astype(BF16)

    whole = lambda a: pl.BlockSpec(a.shape, lambda i: (0, 0))
    out = lambda a: pl.BlockSpec((tm, a.shape[1]), lambda i: (i, 0))
    return pl.pallas_call(
        body, name=name, grid=(T // tm,),
        in_specs=[pl.BlockSpec((tm, D), lambda i: (i, 0)), whole(w_lat), whole(w_pool), whole(w_gate)],
        out_specs=[out(w_lat), out(w_pool), out(w_gate)],
        out_shape=[jax.ShapeDtypeStruct((T, w_lat.shape[1]), F32), jax.ShapeDtypeStruct((T, w_pool.shape[1]), F32),
                   jax.ShapeDtypeStruct((T, w_gate.shape[1]), BF16)],
        compiler_params=_cparams(("parallel",)),
    )(h, w_lat, w_pool, w_gate)


def mix_in_dx_norm(dlat, dxp, dgl, w_lat, w_pool, w_gate, x, g, dy, *, name):
    T, D = x.shape
    tm = _pick(T, (512, 256, 128))

    def body(dlat_ref, dxp_ref, dgl_ref, wl_ref, wp_ref, wg_ref, x_ref, g_ref, dy_ref, dx_ref, dgain_ref):
        @pl.when(pl.program_id(0) == 0)
        def _():
            dgain_ref[...] = jnp.zeros_like(dgain_ref)

        dh = lax.dot_general(dlat_ref[...], wl_ref[...], _NT, preferred_element_type=F32)
        dh = dh + lax.dot_general(dxp_ref[...], wp_ref[...], _NT, preferred_element_type=F32)
        dh = dh + lax.dot_general(dgl_ref[...], wg_ref[...], _NT, preferred_element_type=F32)
        dx, dgain = _rms_bwd_math(x_ref[...], g_ref[...], dh)
        dx_ref[...] = dy_ref[...] + dx
        dgain_ref[...] += dgain

    row = lambda a: pl.BlockSpec((tm, a.shape[1]), lambda i: (i, 0))
    whole = lambda a: pl.BlockSpec(a.shape, lambda i: (0, 0))
    vec = pl.BlockSpec((1, D), lambda i: (0, 0))
    return pl.pallas_call(
        body, name=name, grid=(T // tm,),
        in_specs=[row(dlat), row(dxp), row(dgl), whole(w_lat), whole(w_pool), whole(w_gate), row(x), vec, row(dy)],
        out_specs=[row(x), vec],
        out_shape=[jax.ShapeDtypeStruct((T, D), F32), jax.ShapeDtypeStruct((1, D), F32)],
        compiler_params=_cparams(("arbitrary",)),
    )(dlat, dxp, dgl, w_lat, w_pool, w_gate, x, g.reshape(1, D), dy)


def _rows(T, width_bytes):
    cap = max(8, (2 * 1024 * 1024) // width_bytes)
    return _pick(T, tuple(c for c in (1024, 512, 256, 128, 64, 32, 16) if c <= cap))


def rms_fwd(x, g, *, name, dep=None):
    T, D = x.shape
    tm = _rows(T, D * 4)

    def body(x_ref, g_ref, *rest):
        xv = x_ref[...]
        r = lax.rsqrt(jnp.mean(xv * xv, axis=-1, keepdims=True) + NORM_EPS)
        rest[-1][...] = (xv * r * g_ref[...]).astype(BF16)

    in_specs = [pl.BlockSpec((tm, D), lambda i: (i, 0)), pl.BlockSpec((1, D), lambda i: (0, 0))]
    args = [x, g.reshape(1, D)]
    if dep is not None:
        in_specs.append(pl.BlockSpec((8, LANES), lambda i: (0, 0)))
        args.append(dep)
    return pl.pallas_call(
        body, name=name, grid=(T // tm,), in_specs=in_specs,
        out_specs=pl.BlockSpec((tm, D), lambda i: (i, 0)),
        out_shape=jax.ShapeDtypeStruct((T, D), BF16),
        compiler_params=_cparams(("parallel",)),
    )(*args)


def _rms_bwd_math(xv, gv, dh):
    r = lax.rsqrt(jnp.mean(xv * xv, axis=-1, keepdims=True) + NORM_EPS)
    xn = xv * r
    dg = jnp.sum(dh * xn, axis=0, keepdims=True)
    dxn = dh * gv
    dx = r * (dxn - xn * jnp.mean(dxn * xn, axis=-1, keepdims=True))
    return dx, dg


def _rope(xv, cv, sv):
    half = QK_ROPE_DIM // 2
    lane = lax.broadcasted_iota(jnp.int32, xv.shape, 1)
    swapped = jnp.where(lane < half, pltpu.roll(xv, LANES - half, 1), pltpu.roll(xv, half, 1))
    return xv * cv + swapped * sv


def _rope_t(dy, cv, sv):
    half = QK_ROPE_DIM // 2
    ds = dy * sv
    lane = lax.broadcasted_iota(jnp.int32, dy.shape, 1)
    swapped = jnp.where(lane < half, pltpu.roll(ds, LANES - half, 1), pltpu.roll(ds, half, 1))
    return dy * cv + swapped


def lat_fwd(lat, qn_w, kvn_w, cs, sn, *, name):
    T = lat.shape[0]
    tm = _rows(T, LAT_DIM * 4)
    kv0 = Q_LORA_RANK
    kr0 = Q_LORA_RANK + KV_LORA_RANK

    def body(lat_ref, qw_ref, kw_ref, c_ref, s_ref, qn_ref, kvn_ref, kr_ref):
        ql = lat_ref[:, :kv0]
        r = lax.rsqrt(jnp.mean(ql * ql, axis=-1, keepdims=True) + NORM_EPS)
        qn_ref[...] = (ql * r * qw_ref[...]).astype(BF16)
        kl = lat_ref[:, kv0:kr0]
        r = lax.rsqrt(jnp.mean(kl * kl, axis=-1, keepdims=True) + NORM_EPS)
        kvn_ref[...] = (kl * r * kw_ref[...]).astype(BF16)
        kr_ref[...] = _rope(lat_ref[:, kr0:], c_ref[...], s_ref[...]).astype(BF16)

    return pl.pallas_call(
        body, name=name, grid=(T // tm,),
        in_specs=[pl.BlockSpec((tm, LAT_DIM), lambda i: (i, 0)),
                  pl.BlockSpec((1, Q_LORA_RANK), lambda i: (0, 0)),
                  pl.BlockSpec((1, KV_LORA_RANK), lambda i: (0, 0)),
                  pl.BlockSpec((tm, LANES), lambda i: (i, 0)), pl.BlockSpec((tm, LANES), lambda i: (i, 0))],
        out_specs=[pl.BlockSpec((tm, Q_LORA_RANK), lambda i: (i, 0)),
                   pl.BlockSpec((tm, KV_LORA_RANK), lambda i: (i, 0)),
                   pl.BlockSpec((tm, LANES), lambda i: (i, 0))],
        out_shape=[jax.ShapeDtypeStruct((T, Q_LORA_RANK), BF16), jax.ShapeDtypeStruct((T, KV_LORA_RANK), BF16),
                   jax.ShapeDtypeStruct((T, LANES), BF16)],
        compiler_params=_cparams(("parallel",)),
    )(lat, qn_w.reshape(1, -1), kvn_w.reshape(1, -1), cs, sn)


def lat_bwd(lat, qn_w, kvn_w, dqn, dkvn, dkr, cs, sn, *, name):
    T = lat.shape[0]
    tm = _rows(T, LAT_DIM * 4)
    kv0 = Q_LORA_RANK
    kr0 = Q_LORA_RANK + KV_LORA_RANK

    def body(lat_ref, qw_ref, kw_ref, dqn_ref, dkvn_ref, dkr_ref, c_ref, s_ref, dlat_ref, dqw_ref, dkw_ref):
        @pl.when(pl.program_id(0) == 0)
        def _():
            dqw_ref[...] = jnp.zeros_like(dqw_ref)
            dkw_ref[...] = jnp.zeros_like(dkw_ref)

        dx, dg = _rms_bwd_math(lat_ref[:, :kv0], qw_ref[...], dqn_ref[...])
        dlat_ref[:, :kv0] = dx.astype(BF16)
        dqw_ref[...] += dg
        dx, dg = _rms_bwd_math(lat_ref[:, kv0:kr0], kw_ref[...], dkvn_ref[...])
        dlat_ref[:, kv0:kr0] = dx.astype(BF16)
        dkw_ref[...] += dg
        dlat_ref[:, kr0:] = _rope_t(dkr_ref[...], c_ref[...], s_ref[...]).astype(BF16)

    row = lambda w: pl.BlockSpec((tm, w), lambda i: (i, 0))
    vec = lambda w: pl.BlockSpec((1, w), lambda i: (0, 0))
    return pl.pallas_call(
        body, name=name, grid=(T // tm,),
        in_specs=[row(LAT_DIM), vec(Q_LORA_RANK), vec(KV_LORA_RANK), row(Q_LORA_RANK), row(KV_LORA_RANK),
                  row(LANES), row(LANES), row(LANES)],
        out_specs=[row(LAT_DIM), vec(Q_LORA_RANK), vec(KV_LORA_RANK)],
        out_shape=[jax.ShapeDtypeStruct((T, LAT_DIM), BF16), jax.ShapeDtypeStruct((1, Q_LORA_RANK), F32),
                   jax.ShapeDtypeStruct((1, KV_LORA_RANK), F32)],
        compiler_params=_cparams(("arbitrary",)),
    )(lat, qn_w.reshape(1, -1), kvn_w.reshape(1, -1), dqn, dkvn, dkr, cs, sn)


def attn_proj_gate(o, wap, gl, bg, ba, *, name):
    T, D2 = gl.shape
    D = D2 // 2
    tm = _pick(T, (512, 256, 128))

    def body(o_ref, w_ref, gl_ref, bg_ref, ba_ref, bb_ref, m_ref):
        bb = jnp.dot(o_ref[...], w_ref[...], preferred_element_type=F32)
        bb_ref[...] = bb.astype(BF16)
        ga = _sigmoid(gl_ref[:, :D].astype(F32) + bg_ref[:, :D])
        gb = _sigmoid(gl_ref[:, D:].astype(F32) + bg_ref[:, D:])
        m_ref[...] = (ga * ba_ref[...].astype(F32) + gb * bb).astype(BF16)

    row = lambda w: pl.BlockSpec((tm, w), lambda i: (i, 0))
    return pl.pallas_call(
        body, name=name, grid=(T // tm,),
        in_specs=[row(o.shape[1]), pl.BlockSpec(wap.shape, lambda i: (0, 0)), row(D2),
                  pl.BlockSpec((1, D2), lambda i: (0, 0)), row(D)],
        out_specs=[row(D), row(D)],
        out_shape=[jax.ShapeDtypeStruct((T, D), BF16), jax.ShapeDtypeStruct((T, D), BF16)],
        compiler_params=_cparams(("parallel",)),
    )(o, wap, gl, bg.reshape(1, D2), ba)


def out_dx_gate(dy, wo, gl, bg, ba, bb, *, name):
    T, D2 = gl.shape
    D = D2 // 2
    tm = _pick(T, (512, 256, 128))

    def body(dy_ref, w_ref, gl_ref, bg_ref, ba_ref, bb_ref, dba_ref, dbb_ref, dgl_ref, dbg_ref):
        @pl.when(pl.program_id(0) == 0)
        def _():
            dbg_ref[...] = jnp.zeros_like(dbg_ref)

        dmv = lax.dot_general(dy_ref[...].astype(BF16), w_ref[...], _NT, preferred_element_type=F32)
        ga = _sigmoid(gl_ref[:, :D].astype(F32) + bg_ref[:, :D])
        gb = _sigmoid(gl_ref[:, D:].astype(F32) + bg_ref[:, D:])
        dba_ref[...] = (dmv * ga).astype(BF16)
        dbb_ref[...] = (dmv * gb).astype(BF16)
        dla = dmv * ba_ref[...].astype(F32) * ga * (1.0 - ga)
        dlb = dmv * bb_ref[...].astype(F32) * gb * (1.0 - gb)
        dgl_ref[:, :D] = dla.astype(BF16)
        dgl_ref[:, D:] = dlb.astype(BF16)
        dbg_ref[:, :D] += jnp.sum(dla, axis=0, keepdims=True)
        dbg_ref[:, D:] += jnp.sum(dlb, axis=0, keepdims=True)

    row = lambda w: pl.BlockSpec((tm, w), lambda i: (i, 0))
    vec = pl.BlockSpec((1, D2), lambda i: (0, 0))
    return pl.pallas_call(
        body, name=name, grid=(T // tm,),
        in_specs=[row(D), pl.BlockSpec(wo.shape, lambda i: (0, 0)), row(D2), vec, row(D), row(D)],
        out_specs=[row(D), row(D), row(D2), vec],
        out_shape=[jax.ShapeDtypeStruct((T, D), BF16), jax.ShapeDtypeStruct((T, D), BF16),
                   jax.ShapeDtypeStruct((T, D2), BF16), jax.ShapeDtypeStruct((1, D2), F32)],
        compiler_params=_cparams(("arbitrary",)),
    )(dy, wo, gl, bg.reshape(1, D2), ba, bb)


def loss_head(x, gf, tgt, *, name):
    T, D = x.shape
    tm = _rows(T, D * 4)

    def body(x_ref, g_ref, t_ref, dx_ref, dg_ref, loss_ref):
        @pl.when(pl.program_id(0) == 0)
        def _():
            dg_ref[...] = jnp.zeros_like(dg_ref)
            loss_ref[...] = jnp.zeros_like(loss_ref)

        xv = x_ref[...]
        gv = g_ref[...]
        r = lax.rsqrt(jnp.mean(xv * xv, axis=-1, keepdims=True) + NORM_EPS)
        xn = xv * r
        err = xn * gv - t_ref[...]
        loss_ref[...] += 0.5 * jnp.sum(jnp.mean(err * err, axis=-1, keepdims=True))
        dy = err * (1.0 / D)
        dg_ref[...] += jnp.sum(dy * xn, axis=0, keepdims=True)
        dxn = dy * gv
        dx_ref[...] = r * (dxn - xn * jnp.mean(dxn * xn, axis=-1, keepdims=True))

    row = pl.BlockSpec((tm, D), lambda i: (i, 0))
    vec = pl.BlockSpec((1, D), lambda i: (0, 0))
    return pl.pallas_call(
        body, name=name, grid=(T // tm,), in_specs=[row, vec, row],
        out_specs=[row, vec, pl.BlockSpec((8, LANES), lambda i: (0, 0))],
        out_shape=[jax.ShapeDtypeStruct((T, D), F32), jax.ShapeDtypeStruct((1, D), F32),
                   jax.ShapeDtypeStruct((8, LANES), F32)],
        compiler_params=_cparams(("arbitrary",)),
    )(x, gf.reshape(1, D), tgt)


def _shift_rows(s, k, down):
    n = s.shape[0]
    t = lax.broadcasted_iota(jnp.int32, s.shape, 0)
    if down:
        return jnp.where(t >= k, pltpu.roll(s, k, 0), 0.0)
    return jnp.where(t < n - k, pltpu.roll(s, n - k, 0), 0.0)


def _window_sum(s, w, down):
    k = 1
    while k < w:
        s = s + _shift_rows(s, k, down)
        k *= 2
    return s


def _pool_count(shape, w):
    t = lax.broadcasted_iota(jnp.int32, shape, 0)
    return jnp.minimum(t + 1, w).astype(F32)


def pool_fwd(xp, maps, scale, B, *, name):
    T, P = xp.shape
    S = T // B
    G = POOL_GROUP_DIM

    def body(x_ref, m_ref, sc_ref, o_ref):
        for g, w in enumerate(POOL_WINDOWS):
            xg = x_ref[:, g * G:(g + 1) * G]
            pooled = _window_sum(xg, w, True) / _pool_count(xg.shape, w) - xg
            mixed = jnp.dot(pooled.astype(BF16), m_ref[g], preferred_element_type=F32)
            o_ref[:, g * G:(g + 1) * G] = (mixed * sc_ref[:, g * G:(g + 1) * G]).astype(BF16)

    return pl.pallas_call(
        body, name=name, grid=(B,),
        in_specs=[pl.BlockSpec((S, P), lambda b: (b, 0)), pl.BlockSpec((N_POOL_GROUPS, G, G), lambda b: (0, 0, 0)),
                  pl.BlockSpec((1, P), lambda b: (0, 0))],
        out_specs=pl.BlockSpec((S, P), lambda b: (b, 0)),
        out_shape=jax.ShapeDtypeStruct((T, P), BF16),
        compiler_params=_cparams(("parallel",)),
    )(xp, maps, scale.reshape(1, P))


def pool_bwd(xp, dmixed, maps, scale, B, *, name):
    T, P = xp.shape
    S = T // B
    G = POOL_GROUP_DIM

    def body(x_ref, dm_ref, m_ref, sc_ref, dx_ref, dmaps_ref, dsc_ref):
        @pl.when(pl.program_id(0) == 0)
        def _():
            dmaps_ref[...] = jnp.zeros_like(dmaps_ref)
            dsc_ref[...] = jnp.zeros_like(dsc_ref)

        for g, w in enumerate(POOL_WINDOWS):
            cols = slice(g * G, (g + 1) * G)
            xg = x_ref[:, cols]
            cnt = _pool_count(xg.shape, w)
            pooled = (_window_sum(xg, w, True) / cnt - xg).astype(BF16)
            mixed = jnp.dot(pooled, m_ref[g], preferred_element_type=F32)
            dmx = dm_ref[:, cols]
            dsc_ref[:, cols] += jnp.sum(dmx * mixed, axis=0, keepdims=True)
            dmp = (dmx * sc_ref[:, cols]).astype(BF16)
            dmaps_ref[g] += lax.dot_general(pooled, dmp, (((0,), (0,)), ((), ())), preferred_element_type=F32)
            dpooled = lax.dot_general(dmp, m_ref[g], (((1,), (1,)), ((), ())), preferred_element_type=F32)
            dx_ref[:, cols] = (_window_sum(dpooled / cnt, w, False) - dpooled).astype(BF16)

    blk = pl.BlockSpec((S, P), lambda b: (b, 0))
    mp = pl.BlockSpec((N_POOL_GROUPS, G, G), lambda b: (0, 0, 0))
    vec = pl.BlockSpec((1, P), lambda b: (0, 0))
    return pl.pallas_call(
        body, name=name, grid=(B,), in_specs=[blk, blk, mp, vec], out_specs=[blk, mp, vec],
        out_shape=[jax.ShapeDtypeStruct((T, P), BF16), jax.ShapeDtypeStruct((N_POOL_GROUPS, G, G), F32),
                   jax.ShapeDtypeStruct((1, P), F32)],
        compiler_params=_cparams(("arbitrary",)),
    )(xp, dmixed, maps, scale.reshape(1, P))


def _keys(kv_ref, kr_ref, rows):
    return jnp.concatenate([kv_ref[rows, :QK_NOPE_DIM], kr_ref[rows, :]], axis=1)


def _causal(s):
    row = lax.broadcasted_iota(jnp.int32, s.shape, 0)
    col = lax.broadcasted_iota(jnp.int32, s.shape, 1)
    return jnp.where(row >= col, s, NEG_INF)


def attn_fwd(q, kv, kr, cs, sn, B, *, name):
    T = q.shape[0]
    S = T // B
    blk = min(ATT_BLOCK, S)
    nb = S // blk
    H = N_HEADS
    scale = QK_DIM ** -0.5

    def body(q_ref, kv_ref, kr_ref, c_ref, s_ref, o_ref, lse_ref, qs_ref):
        qs_ref[:, :QK_NOPE_DIM] = (q_ref[:, :QK_NOPE_DIM] * scale).astype(BF16)
        qs_ref[:, QK_NOPE_DIM:] = _rope(q_ref[:, QK_NOPE_DIM:], c_ref[...] * scale, s_ref[...] * scale).astype(BF16)
        for qi in range(nb):
            rows = slice(qi * blk, (qi + 1) * blk)
            qb = qs_ref[rows, :]
            sd = _causal(lax.dot_general(qb, _keys(kv_ref, kr_ref, rows), _NT, preferred_element_type=F32))
            m = jnp.max(sd, axis=-1, keepdims=True)
            if qi > 0:
                prev = slice(0, qi * blk)
                sp = lax.dot_general(qb, _keys(kv_ref, kr_ref, prev), _NT, preferred_element_type=F32)
                m = jnp.maximum(m, jnp.max(sp, axis=-1, keepdims=True))
            pd = jnp.exp(sd - m)
            l = jnp.sum(pd, axis=-1, keepdims=True)
            acc = jnp.dot(pd.astype(BF16), kv_ref[rows, QK_NOPE_DIM:], preferred_element_type=F32)
            if qi > 0:
                pp = jnp.exp(sp - m)
                l = l + jnp.sum(pp, axis=-1, keepdims=True)
                acc = acc + jnp.dot(pp.astype(BF16), kv_ref[prev, QK_NOPE_DIM:], preferred_element_type=F32)
            o_ref[rows, :] = (acc / l).astype(BF16)
            lse_ref[0, rows, :] = m + jnp.log(l)

    head = pl.BlockSpec((S, HEAD_PAD), lambda b, h: (b, h))
    shared = pl.BlockSpec((S, LANES), lambda b, h: (b, 0))
    return pl.pallas_call(
        body, name=name, grid=(B, H),
        in_specs=[head, head, shared, shared, shared],
        out_specs=[pl.BlockSpec((S, V_HEAD_DIM), lambda b, h: (b, h)), pl.BlockSpec((1, S, 1), lambda b, h: (h, b, 0)),
                   head],
        out_shape=[jax.ShapeDtypeStruct((T, H * V_HEAD_DIM), BF16), jax.ShapeDtypeStruct((H, T, 1), F32),
                   jax.ShapeDtypeStruct((T, H * HEAD_PAD), BF16)],
        compiler_params=_cparams(("parallel", "parallel")),
    )(q, kv, kr, cs, sn)


def attn_bwd(q, kv, kr, o, do, lse, cs, sn, B, *, name):
    T = q.shape[0]
    S = T // B
    blk = min(ATT_BLOCK, S)
    nb = S // blk
    H = N_HEADS
    scale = QK_DIM ** -0.5

    def body(q_ref, kv_ref, kr_ref, o_ref, do_ref, lse_ref, c_ref, s_ref, dq_ref, dkv_ref, dkr_ref, dk_s, dv_s):
        dk_s[...] = jnp.zeros_like(dk_s)
        dv_s[...] = jnp.zeros_like(dv_s)

        @pl.when(pl.program_id(1) == 0)
        def _():
            dkr_ref[...] = jnp.zeros_like(dkr_ref)

        for qi in range(nb):
            rows = slice(qi * blk, (qi + 1) * blk)
            qb = q_ref[rows, :]
            dob = do_ref[rows, :]
            delta = jnp.sum(dob.astype(F32) * o_ref[rows, :].astype(F32), axis=-1, keepdims=True)
            lse_b = lse_ref[0, rows, :]

            def part(ks, diagonal):
                k = _keys(kv_ref, kr_ref, ks)
                s = lax.dot_general(qb, k, _NT, preferred_element_type=F32)
                if diagonal:
                    s = _causal(s)
                p = jnp.exp(s - lse_b)
                dp = lax.dot_general(dob, kv_ref[ks, QK_NOPE_DIM:], _NT, preferred_element_type=F32)
                ds = (p * (dp - delta)).astype(BF16)
                dv_s[ks, :] += lax.dot_general(p.astype(BF16), dob, _TN, preferred_element_type=F32)
                dk_s[ks, :] += lax.dot_general(ds, qb, _TN, preferred_element_type=F32)
                return jnp.dot(ds, k, preferred_element_type=F32)

            dq = part(rows, True)
            if qi > 0:
                dq = dq + part(slice(0, qi * blk), False)
            dq_ref[rows, :QK_NOPE_DIM] = (dq[:, :QK_NOPE_DIM] * scale).astype(BF16)
            dq_ref[rows, QK_NOPE_DIM:] = _rope_t(dq[:, QK_NOPE_DIM:], c_ref[rows, :] * scale,
                                                 s_ref[rows, :] * scale).astype(BF16)

        dkv_ref[:, :QK_NOPE_DIM] = dk_s[:, :QK_NOPE_DIM].astype(BF16)
        dkv_ref[:, QK_NOPE_DIM:] = dv_s[...].astype(BF16)
        dkr_ref[...] += dk_s[:, QK_NOPE_DIM:]

    head = lambda w: pl.BlockSpec((S, w), lambda b, h: (b, h))
    shared = pl.BlockSpec((S, LANES), lambda b, h: (b, 0))
    return pl.pallas_call(
        body, name=name, grid=(B, H),
        in_specs=[head(HEAD_PAD), head(HEAD_PAD), shared, head(V_HEAD_DIM), head(V_HEAD_DIM),
                  pl.BlockSpec((1, S, 1), lambda b, h: (h, b, 0)), shared, shared],
        out_specs=[head(HEAD_PAD), head(HEAD_PAD), shared],
        out_shape=[jax.ShapeDtypeStruct((T, H * HEAD_PAD), BF16), jax.ShapeDtypeStruct((T, H * HEAD_PAD), BF16),
                   jax.ShapeDtypeStruct((T, LANES), F32)],
        scratch_shapes=[pltpu.VMEM((S, HEAD_PAD), F32), pltpu.VMEM((S, V_HEAD_DIM), F32)],
        compiler_params=_cparams(("parallel", "arbitrary")),
    )(q, kv, kr, o, do, lse, cs, sn)


def adamw(w, g, m, v, *, name, dep=None, copy_g=False):
    R, C = w.shape
    cap = max(8, (1024 * 1024) // (C * 4))
    tr = _pick(R, tuple(c for c in (1024, 512, 256, 128, 64, 32, 16, 8) if c <= cap))
    c1 = 1.0 - ADAM_B1 ** ADAM_STEP
    c2 = 1.0 - ADAM_B2 ** ADAM_STEP
    nout = 4 if copy_g else 3

    def body(w_ref, g_ref, m_ref, v_ref, *rest):
        d_ref, nm_ref, nv_ref = rest[-3:]
        gv = g_ref[...]
        if copy_g:
            rest[-4][...] = gv
        mv = ADAM_B1 * m_ref[...] + (1.0 - ADAM_B1) * gv
        vv = ADAM_B2 * v_ref[...] + (1.0 - ADAM_B2) * (gv * gv)
        nm_ref[...] = mv
        nv_ref[...] = vv
        d_ref[...] = -ADAM_LR * ((mv / c1) / (jnp.sqrt(vv / c2) + ADAM_EPS) + ADAM_WD * w_ref[...])

    blk = pl.BlockSpec((tr, C), lambda i: (i, 0))
    sh = jax.ShapeDtypeStruct((R, C), F32)
    extra = [] if dep is None else [dep]
    return pl.pallas_call(
        body, name=name, grid=(R // tr,), in_specs=[blk] * 4 + [ANY] * len(extra), out_specs=[blk] * nout,
        out_shape=[sh] * nout, compiler_params=_cparams(("parallel",)),
    )(w, g, m, v, *extra)


ANY = pl.BlockSpec(memory_space=pl.ANY)


def _place():
    x, y, c = lax.axis_index("x"), lax.axis_index("y"), lax.axis_index("c")
    others = [(1 - x, y), (x, 1 - y), (1 - x, 1 - y)]
    return x, y, c, others


def _remote(src, dst, ssem, rsem, dev):
    return pltpu.make_async_remote_copy(src_ref=src, dst_ref=dst, send_sem=ssem, recv_sem=rsem,
                                        device_id=dev, device_id_type=MESH)


def _half(ref_rows, c):
    hr = ref_rows // 2
    return pl.ds(pl.multiple_of(c * hr, 16), hr)


HBM = pl.BlockSpec(memory_space=pltpu.HBM)
SEMS = pl.BlockSpec(memory_space=pltpu.SEMAPHORE)
EFFECT = pltpu.SideEffectType.DATAFLOW_SIDE_EFFECTING


def exchange_begin(name, srcs, land_shapes, plan, ncopies, after=None):
    ns, nl = len(srcs), len(land_shapes)
    nin = ns + nl + (0 if after is None else 1)

    def body(*refs):
        ssem, rsem = refs[nin], refs[nin + 1]
        for k, (s, d, dev) in enumerate(plan(refs[:ns], refs[ns:ns + nl])):
            _remote(s, d, ssem.at[k], rsem.at[k], dev).start()
        refs[-1][...] = jnp.zeros_like(refs[-1])

    bufs = [pltpu.HBM(s.shape, s.dtype) for s in srcs] + [pltpu.HBM(s.shape, s.dtype) for s in land_shapes]
    args = [pltpu.with_memory_space_constraint(s, pltpu.HBM) for s in srcs]
    args += [pltpu.with_memory_space_constraint(lax.empty(s.shape, s.dtype), pltpu.HBM) for s in land_shapes]
    if after is not None:
        args.append(after)
    out = pl.pallas_call(
        body, name=name,
        out_shape=(pltpu.SemaphoreType.DMA((ncopies,)), pltpu.SemaphoreType.DMA((ncopies,)), *bufs,
                   jax.ShapeDtypeStruct((8, LANES), F32)),
        in_specs=[HBM] * (ns + nl) + ([] if after is None else [ANY]),
        out_specs=(SEMS, SEMS, *([HBM] * (ns + nl)), pl.BlockSpec(memory_space=pltpu.VMEM)),
        input_output_aliases={i: 2 + i for i in range(ns + nl)},
        compiler_params=pltpu.CompilerParams(has_side_effects=EFFECT),
    )(*args)
    return (out[0], out[1], out[2:2 + ns], out[2 + ns:2 + ns + nl]), out[-1]


def exchange_end(name, handle, plan, after):
    ssem, rsem, srcs, lands = handle
    ns, nl = len(srcs), len(lands)

    def body(*refs):
        ssem_ref, rsem_ref = refs[ns + nl], refs[ns + nl + 1]
        for k, (s, d, dev) in enumerate(plan(refs[:ns], refs[ns:ns + nl])):
            cp = _remote(s, d, ssem_ref.at[k], rsem_ref.at[k], dev)
            cp.wait_send()
            cp.wait_recv()

    out = pl.pallas_call(
        body, name=name,
        out_shape=tuple(pltpu.HBM(s.shape, s.dtype) for s in (*srcs, *lands)),
        in_specs=[HBM] * (ns + nl) + [SEMS, SEMS, ANY], out_specs=tuple([HBM] * (ns + nl)),
        input_output_aliases={i: i for i in range(ns + nl)},
        compiler_params=pltpu.CompilerParams(has_side_effects=EFFECT),
    )(*srcs, *lands, ssem, rsem, after)
    return list(out[:ns]), list(out[ns:])


def ag_plan(src_refs, land_refs):
    x, y, c, others = _place()
    plan = []
    for s, d in zip(src_refs, land_refs):
        mine = _half(s.shape[0], c)
        for ox, oy in others:
            plan.append((s.at[mine, :], d.at[2 * x + y, mine, :], (ox, oy, c)))
    return plan


def ag_forward_plan(src_refs, land_refs):
    x, y, c, others = _place()
    plan = []
    for s in src_refs:
        mine = _half(s.shape[1], c)
        for ox, oy in others:
            blk = s.at[2 * ox + oy, mine, :]
            plan.append((blk, blk, (x, y, 1 - c)))
    return plan


def rs_swap_plan(src_refs, land_refs):
    x, y, c, _ = _place()
    return [(s.at[:, _half(s.shape[1], 1 - c), :], d, (x, y, 1 - c)) for s, d in zip(src_refs, land_refs)]


def ag_forward(lands, *, name):
    n = len(lands)

    def body(*refs):
        ins, outs = refs[:n], refs[n:2 * n]
        ssem, rsem = refs[2 * n:]
        x, y, c, others = _place()
        sent = []
        for i in range(n):
            mine = _half(ins[i].shape[1], c)
            for j, (ox, oy) in enumerate(others):
                cp = _remote(ins[i].at[2 * ox + oy, mine, :], outs[i].at[2 * ox + oy, mine, :], ssem.at[3 * i + j],
                             rsem.at[3 * i + j], (x, y, 1 - c))
                cp.start()
                sent.append(cp)
        for cp in sent:
            cp.wait()

    return pl.pallas_call(
        body, name=name, in_specs=[ANY] * n, out_specs=[ANY] * n,
        out_shape=[jax.ShapeDtypeStruct(a.shape, a.dtype) for a in lands],
        input_output_aliases={i: i for i in range(n)},
        scratch_shapes=[pltpu.SemaphoreType.DMA((3 * n,)), pltpu.SemaphoreType.DMA((3 * n,))],
        compiler_params=pltpu.CompilerParams(has_side_effects=True),
    )(*lands)


def place_own(lands, own, chip, *, name):
    n = len(lands)
    steps = 4

    def body(chip_ref, *refs):
        for i in range(n):
            refs[2 * n + i][0] = refs[i][...]

    in_specs = [pl.BlockSpec((o.shape[0] // steps, o.shape[1]), lambda t, q: (t, 0)) for o in own] + [ANY] * n
    out_specs = [pl.BlockSpec((1, o.shape[0] // steps, o.shape[1]), lambda t, q: (q[0], t, 0)) for o in own]
    return pl.pallas_call(
        body, name=name,
        grid_spec=pltpu.PrefetchScalarGridSpec(num_scalar_prefetch=1, grid=(steps,), in_specs=in_specs,
                                               out_specs=out_specs),
        out_shape=[jax.ShapeDtypeStruct(a.shape, a.dtype) for a in lands],
        input_output_aliases={1 + n + i: i for i in range(n)},
        compiler_params=_cparams(("parallel",)),
    )(chip, *own, *lands)


def rs_swap_halves(grads, *, name, after=None):
    n = len(grads)
    extra = [] if after is None else [after]
    nin = n + len(extra)

    def body(*refs):
        ins, outs = refs[:n], refs[nin:nin + n]
        ssem, rsem = refs[nin + n:]
        x, y, c, _ = _place()
        cps = []
        for i in range(n):
            theirs = _half(ins[i].shape[1], 1 - c)
            cp = _remote(ins[i].at[:, theirs, :], outs[i], ssem.at[i], rsem.at[i], (x, y, 1 - c))
            cp.start()
            cps.append(cp)
        for cp in cps:
            cp.wait()

    return pl.pallas_call(
        body, name=name, in_specs=[ANY] * nin, out_specs=[ANY] * n,
        out_shape=[jax.ShapeDtypeStruct((4, g.shape[1] // 2, g.shape[2]), g.dtype) for g in grads],
        scratch_shapes=[pltpu.SemaphoreType.DMA((n,)), pltpu.SemaphoreType.DMA((n,))],
        compiler_params=pltpu.CompilerParams(has_side_effects=True),
    )(*grads, *extra)


def rs_chip_sum(g, r1, core, *, name):
    _, r, cdim = g.shape
    hr = r // 2

    def body(c_ref, g_ref, r1_ref, o_ref):
        o_ref[...] = (g_ref[...].astype(F32) + r1_ref[...].astype(F32)).astype(BF16)

    return pl.pallas_call(
        body, name=name,
        grid_spec=pltpu.PrefetchScalarGridSpec(
            num_scalar_prefetch=1, grid=(4,),
            in_specs=[pl.BlockSpec((1, hr, cdim), lambda qq, c_ref: (qq, c_ref[0], 0)),
                      pl.BlockSpec((1, hr, cdim), lambda qq, c_ref: (qq, 0, 0))],
            out_specs=pl.BlockSpec((1, hr, cdim), lambda qq, c_ref: (qq, 0, 0))),
        out_shape=jax.ShapeDtypeStruct((4, hr, cdim), BF16),
        compiler_params=_cparams(("parallel",)),
    )(core, g, r1)


def rs_plan(src_refs, land_refs):
    x, y, c, others = _place()
    plan = []
    for s, d in zip(src_refs, land_refs):
        for j, (ox, oy) in enumerate(others):
            plan.append((s.at[2 * ox + oy], d.at[j], (ox, oy, c)))
    return plan


def rs_final_sum(g, r1, r2, place, acc, l, *, name):
    _, r, cdim = g.shape
    hr = r // 2
    ch = hr // 2

    def body(p_ref, g_ref, r1_ref, a_ref, b_ref, d_ref, acc_in, o_ref):
        s = g_ref[...].astype(F32) + r1_ref[...].astype(F32)
        s = s + a_ref[...].astype(F32)
        s = s + b_ref[...].astype(F32)
        o_ref[...] = s + d_ref[...].astype(F32)

    other = lambda j: pl.BlockSpec((1, ch, cdim), lambda t, p_ref: (j, t, 0))
    return pl.pallas_call(
        body, name=name,
        grid_spec=pltpu.PrefetchScalarGridSpec(
            num_scalar_prefetch=1, grid=(2,),
            in_specs=[pl.BlockSpec((1, ch, cdim), lambda t, p_ref: (p_ref[0], 2 * p_ref[1] + t, 0)),
                      pl.BlockSpec((1, ch, cdim), lambda t, p_ref: (p_ref[0], t, 0)),
                      other(0), other(1), other(2), ANY],
            out_specs=pl.BlockSpec((1, ch, cdim), lambda t, p_ref: (l, 2 * p_ref[1] + t, 0))),
        out_shape=jax.ShapeDtypeStruct(acc.shape, F32),
        input_output_aliases={6: 0},
        compiler_params=_cparams(("parallel",)),
    )(place, g, r1, r2, r2, r2, acc)


def rs_join_halves(grads, *, name):
    n = len(grads)

    def body(*refs):
        ins, outs = refs[:n], refs[n:2 * n]
        ssem, rsem = refs[2 * n:]
        x, y, c, _ = _place()
        cps = []
        for i in range(n):
            mine = _half(ins[i].shape[1], c)
            cp = _remote(ins[i].at[:, mine, :], outs[i].at[:, mine, :], ssem.at[i], rsem.at[i], (x, y, 1 - c))
            cp.start()
            cps.append(cp)
        for cp in cps:
            cp.wait()

    return pl.pallas_call(
        body, name=name, in_specs=[ANY] * n, out_specs=[ANY] * n,
        out_shape=[jax.ShapeDtypeStruct(g.shape, g.dtype) for g in grads],
        input_output_aliases={i: i for i in range(n)},
        scratch_shapes=[pltpu.SemaphoreType.DMA((n,)), pltpu.SemaphoreType.DMA((n,))],
        compiler_params=pltpu.CompilerParams(has_side_effects=True),
    )(*grads)


def all_reduce_small(vs):
    n = len(vs)

    def body(*refs):
        v_refs, o_refs, bufs = refs[:n], refs[n:2 * n], refs[2 * n:3 * n]
        ssem1, rsem1, ssem2, rsem2 = refs[3 * n:]
        x, y, c, _ = _place()
        me = 4 * x + 2 * y + c
        peers = []
        for k in range(1, 8):
            px = jnp.where((k >> 2) & 1 == 1, 1 - x, x)
            py = jnp.where((k >> 1) & 1 == 1, 1 - y, y)
            pc = jnp.where(k & 1 == 1, 1 - c, c)
            peers.append((px, py, pc))
        rows = lambda i, d: pl.ds(pl.multiple_of(d * (vs[i].shape[0] // 8), 8), vs[i].shape[0] // 8)
        cps = []
        for i in range(n):
            bufs[i][me] = v_refs[i][rows(i, me), :]
            for k, (px, py, pc) in enumerate(peers):
                cp = _remote(v_refs[i].at[rows(i, 4 * px + 2 * py + pc), :], bufs[i].at[me], ssem1.at[7 * i + k],
                             rsem1.at[7 * i + k], (px, py, pc))
                cp.start()
                cps.append(cp)
        for cp in cps:
            cp.wait()
        cps = []
        for i in range(n):
            acc = bufs[i][0]
            for d in range(1, 8):
                acc = acc + bufs[i][d]
            o_refs[i][rows(i, me), :] = acc
            for k, peer in enumerate(peers):
                mine = o_refs[i].at[rows(i, me), :]
                cp = _remote(mine, mine, ssem2.at[7 * i + k], rsem2.at[7 * i + k], peer)
                cp.start()
                cps.append(cp)
        for cp in cps:
            cp.wait()

    vm = pl.BlockSpec(memory_space=pltpu.VMEM)
    return pl.pallas_call(
        body, name="all_reduce_small", in_specs=[vm] * n, out_specs=[vm] * n,
        out_shape=[jax.ShapeDtypeStruct(v.shape, F32) for v in vs],
        scratch_shapes=[pltpu.VMEM((8, v.shape[0] // 8, LANES), F32) for v in vs]
        + [pltpu.SemaphoreType.DMA((7 * n,))] * 4,
        compiler_params=pltpu.CompilerParams(vmem_limit_bytes=VMEM_LIMIT, has_side_effects=True),
    )(*vs)


def _to_stacked(name, full):
    R, C = full.shape
    if name in ROW_SHARDED:
        return full.reshape(4, R // 4, C)
    return jnp.transpose(full.reshape(R, 4, C // 4), (1, 0, 2))


def _from_stacked(name, st):
    _, r, c = st.shape
    if name in ROW_SHARDED:
        return st.reshape(4 * r, c)
    return jnp.transpose(st, (1, 0, 2)).reshape(r, 4 * c)


UP_PIECES = ("ffn1_up", "ffn2_up")


def _layer_weights(lands):
    w = {n: lands[n] if n in UP_PIECES else _from_stacked(n, lands[n]) for n in lands}
    if "w_in" not in w:
        return w
    win = w.pop("w_in")
    D = win.shape[0]
    p0, p1, p2, p3 = POOL_DIM, POOL_DIM + Q_LORA_RANK, POOL_DIM + Q_LORA_RANK + KV_LORA_RANK, \
        POOL_DIM + Q_LORA_RANK + KV_LORA_RANK + QK_ROPE_DIM
    w["w_pool"] = win[:, :p0]
    w["w_lat"] = jnp.concatenate([win[:, p0:p3], jnp.zeros((D, LAT_DIM - (p3 - p0)), win.dtype)], axis=1)
    w["w_gate"] = win[:, p3:]
    uq = w["w_uq"].reshape(Q_LORA_RANK, N_HEADS, QK_DIM)
    w["w_uq"] = jnp.concatenate([uq, jnp.zeros((Q_LORA_RANK, N_HEADS, HEAD_PAD - QK_DIM), uq.dtype)],
                                axis=2).reshape(Q_LORA_RANK, N_HEADS * HEAD_PAD)
    return w


def _layer_grads_stacked(dw):
    dw = dict(dw)
    if "w_lat" in dw:
        lat = dw.pop("w_lat")
        dw["w_in"] = jnp.concatenate([dw.pop("w_pool"), lat[:, :Q_LORA_RANK + KV_LORA_RANK + QK_ROPE_DIM],
                                      dw.pop("w_gate")], axis=1)
        dw["w_uq"] = dw["w_uq"].reshape(Q_LORA_RANK, N_HEADS, HEAD_PAD)[:, :, :QK_DIM].reshape(Q_LORA_RANK,
                                                                                                 N_HEADS * QK_DIM)
    return {n: dw[n] if n in UP_PIECES else _to_stacked(n, dw[n]) for n in BIG if n in dw}


def _rope_tables(positions):
    inv_freq = ROPE_THETA ** (-jnp.arange(0, QK_ROPE_DIM, 2, dtype=F32) / QK_ROPE_DIM)
    ang = positions.astype(F32).reshape(-1)[:, None] * inv_freq
    cos, sin = jnp.cos(ang), jnp.sin(ang)
    z = jnp.zeros((ang.shape[0], LANES - QK_ROPE_DIM), F32)
    return jnp.concatenate([cos, cos, z], axis=1), jnp.concatenate([-sin, sin, z], axis=1)


def _pack_small(vals):
    parts, total = [], 0
    for n in PACKED:
        f = vals[n].reshape(-1).astype(F32)
        pad = (-f.shape[0]) % (8 * LANES)
        parts.append(jnp.pad(f, (0, pad)))
        total += f.shape[0] + pad
    parts.append(jnp.zeros(((-total) % (64 * LANES),), F32))
    return jnp.concatenate(parts).reshape(-1, LANES)


def _unpack_small(packed, like):
    flat = packed.reshape(-1)
    out, off = {}, 0
    for n in PACKED:
        size = like[n].size
        out[n] = flat[off:off + size].reshape(like[n].shape)
        off += size + ((-size) % (8 * LANES))
    return out


def _ffn_fwd(x, g, wu4, wd, tag, dep=None):
    h = rms_fwd(x, g, dep=dep, name=f"{tag}_norm")
    gate, up, a = ffn_up_act(h, wu4, name=f"{tag}_up_act")
    y = mm(a, wd, res=x, alpha=0.5, name=f"{tag}_down")
    return y, (x, h, gate, up, a)


def _ffn_bwd(dy, saved, g, wu4, wd, tag, dep=None):
    x, h, gate, up, a = saved
    dgate, dup = ffn_down_dx_act(dy, wd, gate, up, dep=dep, name=f"{tag}_down_dx_act")
    dwd = mm(a, dy, ta=True, alpha=0.5, out_dtype=BF16, name=f"{tag}_down_dw")
    dwu4 = ffn_up_dw(h, dgate, dup, name=f"{tag}_up_dw")
    dx, dg = ffn_up_dx_norm(dgate, dup, wu4, x, g, dy, name=f"{tag}_up_dx_norm")
    return dx, dg, dwu4, dwd


def _mix_fwd(x, p, w, cs, sn, B, dep=None):
    h = rms_fwd(x, p["norm_mix"], dep=dep, name="mix_norm")
    lat, xp, gl = mix_in(h, w["w_lat"], w["w_pool"], w["w_gate"], name="mix_in")
    mixed = pool_fwd(xp, p["pool_maps"].astype(BF16), p["pool_scale"], B, name="pool_fwd")
    ba = mm(mixed, w["w_pool_proj"], out_dtype=BF16, name="mix_pool_proj")
    qn, kvn, kr = lat_fwd(lat, p["q_latent_norm"], p["kv_latent_norm"], cs, sn, name="lat_fwd")
    kv = mm(kvn, w["w_ukv"], out_dtype=BF16, name="mix_ukv")
    o, lse, q = attn_fwd(mm(qn, w["w_uq"], name="mix_uq"), kv, kr, cs, sn, B, name="attn_fwd")
    bb, merged = attn_proj_gate(o, w["w_attn_proj"], gl, p["b_gate"], ba, name="mix_attn_proj_gate")
    y = mm(merged, w["w_out"], res=x, name="mix_out")
    return y, (x, h, lat, xp, gl, mixed, ba, qn, kvn, kr, q, kv, o, lse, bb, merged)


def _mix_bwd(dy, saved, p, w, cs, sn, B, dep=None):
    x, h, lat, xp, gl, mixed, ba, qn, kvn, kr, q, kv, o, lse, bb, merged = saved
    dw, ds = {}, {}
    dw["w_out"] = mm(merged, dy, ta=True, out_dtype=BF16, dep=dep, name="mix_out_dw")
    dba, dbb, dgl, ds["b_gate"] = out_dx_gate(dy, w["w_out"], gl, p["b_gate"], ba, bb, name="mix_out_dx_gate")
    dw["w_attn_proj"] = mm(o, dbb, ta=True, out_dtype=BF16, name="mix_attn_proj_dw")
    do = mm(dbb, w["w_attn_proj"], tb=True, out_dtype=BF16, name="mix_attn_proj_dx")
    dw["w_pool_proj"] = mm(mixed, dba, ta=True, out_dtype=BF16, name="mix_pool_proj_dw")
    dmixed = mm(dba, w["w_pool_proj"], tb=True, name="mix_pool_proj_dx")
    dxp, ds["pool_maps"], ds["pool_scale"] = pool_bwd(xp, dmixed, p["pool_maps"].astype(BF16), p["pool_scale"], B,
                                                      name="pool_bwd")
    dqb, dkv, dkr = attn_bwd(q, kv, kr, o, do, lse, cs, sn, B, name="attn_bwd")
    dw["w_ukv"] = mm(kvn, dkv, ta=True, out_dtype=BF16, name="mix_ukv_dw")
    dkvn = mm(dkv, w["w_ukv"], tb=True, name="mix_ukv_dx")
    dw["w_uq"] = mm(qn, dqb, ta=True, out_dtype=BF16, name="mix_uq_dw")
    dqn = mm(dqb, w["w_uq"], tb=True, name="mix_uq_dx")
    dlat, ds["q_latent_norm"], ds["kv_latent_norm"] = lat_bwd(lat, p["q_latent_norm"], p["kv_latent_norm"], dqn, dkvn,
                                                               dkr, cs, sn, name="lat_bwd")
    dw["w_lat"] = mm(h, dlat, ta=True, out_dtype=BF16, name="mix_lat_dw")
    dw["w_pool"] = mm(h, dxp, ta=True, out_dtype=BF16, name="mix_pool_in_dw")
    dw["w_gate"] = mm(h, dgl, ta=True, out_dtype=BF16, name="mix_gate_in_dw")
    dx, ds["norm_mix"] = mix_in_dx_norm(dlat, dxp, dgl, w["w_lat"], w["w_pool"], w["w_gate"], x, p["norm_mix"], dy,
                                        name="mix_in_dx_norm")
    return dx, dw, ds


def kernel(x, positions, norm_ffn1, ffn1_up, ffn1_down, norm_mix, w_in, b_gate, pool_maps, pool_scale, w_pool_proj, q_latent_norm, w_uq, kv_latent_norm, w_ukv, w_attn_proj, w_out, norm_ffn2, ffn2_up, ffn2_down, final_norm, loss_target, m_norm_ffn1, m_ffn1_up, m_ffn1_down, m_norm_mix, m_w_in, m_b_gate, m_pool_maps, m_pool_scale, m_w_pool_proj, m_q_latent_norm, m_w_uq, m_kv_latent_norm, m_w_ukv, m_w_attn_proj, m_w_out, m_norm_ffn2, m_ffn2_up, m_ffn2_down, m_final_norm, v_norm_ffn1, v_ffn1_up, v_ffn1_down, v_norm_mix, v_w_in, v_b_gate, v_pool_maps, v_pool_scale, v_w_pool_proj, v_q_latent_norm, v_w_uq, v_kv_latent_norm, v_w_ukv, v_w_attn_proj, v_w_out, v_norm_ffn2, v_ffn2_up, v_ffn2_down, v_final_norm):
    given = dict(locals())
    B, S, D = x.shape
    T = B * S
    L = norm_ffn1.shape[0]
    W = {n: given[n] for n in WEIGHTS}
    Mo = {n: given["m_" + n] for n in WEIGHTS}
    Vo = {n: given["v_" + n] for n in WEIGHTS}
    core = lax.axis_index("c").astype(jnp.int32)
    chip = (2 * lax.axis_index("x") + lax.axis_index("y")).astype(jnp.int32)

    core_arr = core.reshape(1)
    chip_arr = chip.reshape(1)
    place = jnp.stack([chip, core])
    first = ("ffn1_up", "ffn1_down")
    rest = tuple(n for n in BIG if n not in first)

    own = [{n: W[n][l].astype(BF16) for n in BIG} for l in range(L)]

    def ag_begin(l, names, tag, after=None):
        lands = [jax.ShapeDtypeStruct((4,) + own[l][n].shape, BF16) for n in names]
        return exchange_begin(f"ag_start_{tag}", [own[l][n] for n in names], lands, ag_plan, 3 * len(names), after)

    def ag_finish(handle, names, tag, after):
        mine, lands = exchange_end(f"ag_wait_{tag}", handle, ag_plan, after)
        lands = ag_forward(lands, name=f"ag_forward_{tag}")
        return _layer_weights(dict(zip(names, place_own(lands, mine, chip_arr, name=f"place_own_{tag}"))))

    h_first, t1 = ag_begin(0, first, "0a")
    cs, sn = _rope_tables(positions)
    xs = x.reshape(T, D) + t1[0, 0]
    saved, ici, handed = [], None, None
    for l in range(L):
        p = {n: W[n][l] for n in SMALL if n != "final_norm"}
        dep = None
        if l == 0:
            w = ag_finish(h_first, first, "0a", xs)
            h_rest, dep = ag_begin(0, rest, "0b", after=w["ffn1_down"])
        else:
            mine, handle = handed
            lands, _ = exchange_end(f"ag_forward_wait_{l}", handle, ag_forward_plan, xs)
            w = _layer_weights(dict(zip(BIG, place_own(lands, mine, chip_arr, name=f"place_own_{l}"))))
        xs, s1 = _ffn_fwd(xs, p["norm_ffn1"], w["ffn1_up"], w["ffn1_down"], "ffn1", dep=dep)
        dep = None
        if l == 0:
            w.update(ag_finish(h_rest, rest, "0b", xs))
            if L > 1:
                ici, dep = ag_begin(1, BIG, "1", after=xs)
        xs, s2 = _mix_fwd(xs, p, w, cs, sn, B, dep=dep)
        dep = None
        if l + 1 < L:
            mine, lands = exchange_end(f"ag_wait_{l + 1}", ici, ag_plan, xs)
            handle, dep = exchange_begin(f"ag_forward_start_{l + 1}", lands, [], ag_forward_plan, 3 * len(BIG))
            handed = (mine, handle)
            if l + 2 < L:
                ici, dep = ag_begin(l + 2, BIG, str(l + 2), after=dep)
        xs, s3 = _ffn_fwd(xs, p["norm_ffn2"], w["ffn2_up"], w["ffn2_down"], "ffn2", dep=dep)
        saved.append((w, p, s1, s2, s3))

    dx, dfinal, loss_tile = loss_head(xs, final_norm, loss_target.reshape(T, D), name="loss_head")
    loss = lax.psum(loss_tile[0, 0], ("x", "y", "c"))

    def rs_begin(dw, tag, after=None):
        stacked = _layer_grads_stacked(dw)
        names = tuple(stacked)
        parts = [stacked[n] for n in names]
        r1 = rs_swap_halves(parts, after=after, name=f"rs_swap_{tag}")
        sums = [rs_chip_sum(g, a, core_arr, name=f"rs_chip_sum_{n}") for n, g, a in zip(names, parts, r1)]
        lands = [jax.ShapeDtypeStruct((3,) + s.shape[1:], BF16) for s in sums]
        handle, token = exchange_begin(f"rs_start_{tag}", sums, lands, rs_plan, 3 * len(names))
        return (names, parts, r1, handle), token

    acc = {n: lax.empty(W[n].shape, F32) for n in BIG}

    def rs_finish(l, pending, tag, after):
        names, parts, r1, handle = pending
        _, r2 = exchange_end(f"rs_wait_{tag}", handle, rs_plan, after)
        for n, g, a, b in zip(names, parts, r1, r2):
            acc[n] = rs_final_sum(g, a, b, place, acc[n], l, name=f"rs_final_sum_{n}")

    small_layers, pending, swapping, dep = [], [], None, None
    for l in reversed(range(L)):
        w, p, s1, s2, s3 = saved[l]
        dx, dg2, dwu2, dwd2 = _ffn_bwd(dx, s3, p["norm_ffn2"], w["ffn2_up"], w["ffn2_down"], "ffn2", dep=dep)
        dep = None
        if swapping is not None:
            above, names, handle = swapping
            parts, r1 = exchange_end(f"rs_swap_wait_{above}", handle, rs_swap_plan, dx)
            sums = [rs_chip_sum(g, a, core_arr, name=f"rs_chip_sum_{n}") for n, g, a in zip(names, parts, r1)]
            lands = [jax.ShapeDtypeStruct((3,) + s.shape[1:], BF16) for s in sums]
            handle, dep = exchange_begin(f"rs_start_{above}", sums, lands, rs_plan, 3 * len(names))
            pending.append((above, (names, parts, r1, handle)))
            swapping = None
        dx, dw, ds = _mix_bwd(dx, s2, p, w, cs, sn, B, dep=dep)
        dw.update(ffn2_up=dwu2, ffn2_down=dwd2)
        dep = None
        if l == 0:
            early, dep = rs_begin(dw, "0b")
            dw = {}
        dx, dg1, dwu1, dwd1 = _ffn_bwd(dx, s1, p["norm_ffn1"], w["ffn1_up"], w["ffn1_down"], "ffn1", dep=dep)
        dw.update(ffn1_up=dwu1, ffn1_down=dwd1)
        ds.update(norm_ffn1=dg1, norm_ffn2=dg2)
        small_layers.append(ds)
        if l == 0:
            last_dw = dw
        else:
            stacked = _layer_grads_stacked(dw)
            halves = [jax.ShapeDtypeStruct((4, g.shape[1] // 2, g.shape[2]), BF16) for g in stacked.values()]
            handle, dep = exchange_begin(f"rs_swap_start_{l}", list(stacked.values()), halves, rs_swap_plan,
                                         len(stacked))
            swapping = (l, tuple(stacked), handle)
    small_layers.reverse()

    small = {n: jnp.stack([small_layers[l][n].reshape(W[n].shape[1:]) for l in range(L)]) for n in SMALL
             if n != "final_norm"}
    small["final_norm"] = dfinal.reshape(final_norm.shape)
    rows = lambda a: a.reshape(-1, LANES)
    reduced, reduced_maps = all_reduce_small([_pack_small(small), rows(small["pool_maps"])])
    grads = _unpack_small(reduced, small)
    grads["pool_maps"] = reduced_maps.reshape(pool_maps.shape)
    last, dep = rs_begin(last_dw, "0a", after=reduced)
    delta, new_m, new_v = {}, {}, {}
    d, nm, nv = adamw(_pack_small(W), reduced, _pack_small(Mo), _pack_small(Vo), dep=dep, name="adamw_small")
    delta.update(_unpack_small(d, W))
    new_m.update(_unpack_small(nm, W))
    new_v.update(_unpack_small(nv, W))
    d, nm, nv = adamw(rows(pool_maps), reduced_maps, rows(m_pool_maps), rows(v_pool_maps), dep=d,
                      name="adamw_pool_maps")
    delta["pool_maps"], new_m["pool_maps"], new_v["pool_maps"] = (a.reshape(pool_maps.shape) for a in (d, nm, nv))

    def update(names, tag, d):
        joined = rs_join_halves([acc[n] for n in names], name=f"rs_join_{tag}")
        for n, g in zip(names, joined):
            flip = (lambda a: jnp.swapaxes(a, 1, 2)) if n == "w_in" else (lambda a: a)
            sh = flip(W[n]).shape
            two = lambda a: flip(a).reshape(sh[0] * sh[1], sh[2])
            back = lambda a: flip(a.reshape(sh))
            gc, d, nm, nv = adamw(two(W[n]), two(g), two(Mo[n]), two(Vo[n]), dep=d, copy_g=True, name=f"adamw_{n}")
            grads[n], delta[n], new_m[n], new_v[n] = back(gc), back(d), back(nm), back(nv)
        return d

    for l, item in pending:
        rs_finish(l, item, str(l), d)
    rs_finish(0, early, "0b", d)
    d = update(rest, "rest", d)
    rs_finish(0, last, "0a", d)
    update(first, "first", d)

    return (loss, dx.reshape(B, S, D), *[grads[n] for n in WEIGHTS], *[delta[n] for n in WEIGHTS],
            *[new_m[n] for n in WEIGHTS], *[new_v[n] for n in WEIGHTS])
```

```python
import functools

import jax
import jax.numpy as jnp
from jax import lax
from jax.experimental import pallas as pl
from jax.experimental.pallas import tpu as pltpu

F32 = jnp.float32
BF16 = jnp.bfloat16

N_HEADS = 8
QK_NOPE_DIM = 128
QK_ROPE_DIM = 64
QK_DIM = QK_NOPE_DIM + QK_ROPE_DIM
V_HEAD_DIM = 128
HEAD_PAD = 256
Q_LORA_RANK = 384
KV_LORA_RANK = 256
ROPE_THETA = 10000.0
POOL_WINDOWS = (2, 4, 8, 16)
N_POOL_GROUPS = 4
POOL_GROUP_DIM = 128
POOL_DIM = N_POOL_GROUPS * POOL_GROUP_DIM
LAT_DIM = 768
NORM_EPS = 1e-6
ADAM_LR = 0.001
ADAM_B1 = 0.9
ADAM_B2 = 0.999
ADAM_EPS = 1e-08
ADAM_WD = 0.01
ADAM_STEP = 10
NEG_INF = -1e30
LANES = 128
ATT_BLOCK = 512
VMEM_LIMIT = 48 * 1024 * 1024
MESH = pl.DeviceIdType.MESH
_NT = (((1,), (1,)), ((), ()))
_TN = (((0,), (0,)), ((), ()))

BIG = ("ffn1_up", "ffn1_down", "w_in", "w_pool_proj", "w_uq", "w_ukv", "w_attn_proj", "w_out",
       "ffn2_up", "ffn2_down")
ROW_SHARDED = ("ffn1_down", "w_attn_proj", "w_out", "ffn2_down")
SMALL = ("norm_ffn1", "norm_mix", "b_gate", "pool_maps", "pool_scale", "q_latent_norm",
         "kv_latent_norm", "norm_ffn2", "final_norm")
PACKED = tuple(n for n in SMALL if n != "pool_maps")
WEIGHTS = ("norm_ffn1", "ffn1_up", "ffn1_down", "norm_mix", "w_in", "b_gate", "pool_maps", "pool_scale",
           "w_pool_proj", "q_latent_norm", "w_uq", "kv_latent_norm", "w_ukv", "w_attn_proj", "w_out",
           "norm_ffn2", "ffn2_up", "ffn2_down", "final_norm")


def _pick(dim, cands):
    for c in cands:
        if c <= dim and dim % c == 0:
            return c
    return dim


def _cparams(sem=None, **kw):
    if sem is not None:
        kw["dimension_semantics"] = sem
    return pltpu.CompilerParams(vmem_limit_bytes=VMEM_LIMIT, **kw)


def _sigmoid(x):
    return 0.5 * jnp.tanh(0.5 * x) + 0.5


MM_TILE_BUDGET = 30 * 1024 * 1024
TILE_SIZES = (1408, 1024, 768, 512, 384, 256, 128)


V7X_MXU_FLOPS = 9.0e14
V7X_HBM_BYTES = 2.5e12
GRID_STEP_S = 0.35e-6


def _mm_tiles(M, N, K, sa, sb, so, sr):
    tks = [K] if K <= 2816 else [t for t in (2816, 2048, 1408, 1024, 512, 256, 128) if K % t == 0]
    best = None
    for tk in tks:
        for tm in [t for t in TILE_SIZES if M % t == 0] or [M]:
            for tn in [t for t in TILE_SIZES if N % t == 0] or [N]:
                need = 2 * (tm * tk * sa + tk * tn * sb + tm * tn * (so + sr)) + (tm * tn * 4 if tk < K else 0)
                if need > MM_TILE_BUDGET:
                    continue
                ni, nj, nk = M // tm, N // tn, K // tk
                a_bytes = M * K * sa * (nj if nk > 1 else 1)
                b_bytes = K * N * sb * (1 if nj == 1 and nk == 1 else ni)
                traffic = a_bytes + b_bytes + M * N * (so + sr) + (M * N * 8 * nk if nk > 1 else 0)
                t = max(2.0 * M * N * K / V7X_MXU_FLOPS, traffic / V7X_HBM_BYTES) + ni * nj * nk * GRID_STEP_S
                if best is None or t < best[0]:
                    best = (t, (tm, tn, tk))
    assert best is not None, (M, N, K)
    return best[1]


def mm(a, b, *, name, ta=False, tb=False, out_dtype=F32, res=None, alpha=1.0, dep=None):
    if ta:
        K, M = a.shape
    else:
        M, K = a.shape
    if tb:
        N, K2 = b.shape
    else:
        K2, N = b.shape
    assert K == K2, (a.shape, b.shape, ta, tb)
    tm, tn, tk = _mm_tiles(M, N, K, a.dtype.itemsize, b.dtype.itemsize, jnp.dtype(out_dtype).itemsize,
                           0 if res is None else res.dtype.itemsize)
    nk = K // tk
    dims = (((0 if ta else 1,), (1 if tb else 0,)), ((), ()))

    def body(*refs):
        a_ref, b_ref = refs[:2]
        r_ref = refs[2] if res is not None else None
        o_ref = refs[-2] if nk > 1 else refs[-1]

        def finish(r):
            if alpha != 1.0:
                r = r * alpha
            if res is not None:
                r = r_ref[...].astype(F32) + r
            o_ref[...] = r.astype(out_dtype)

        part = lax.dot_general(a_ref[...].astype(BF16), b_ref[...].astype(BF16), dims, preferred_element_type=F32)
        if nk == 1:
            finish(part)
            return
        acc = refs[-1]
        k = pl.program_id(2)

        @pl.when(k == 0)
        def _():
            acc[...] = part

        @pl.when(k > 0)
        def _():
            acc[...] += part

        @pl.when(k == nk - 1)
        def _():
            finish(acc[...])

    a_spec = pl.BlockSpec((tk, tm), lambda i, j, k: (k, i)) if ta else pl.BlockSpec((tm, tk), lambda i, j, k: (i, k))
    b_spec = pl.BlockSpec((tn, tk), lambda i, j, k: (j, k)) if tb else pl.BlockSpec((tk, tn), lambda i, j, k: (k, j))
    o_spec = pl.BlockSpec((tm, tn), lambda i, j, k: (i, j))
    in_specs = [a_spec, b_spec]
    args = [a, b]
    if res is not None:
        in_specs.append(o_spec)
        args.append(res)
    if dep is not None:
        in_specs.append(pl.BlockSpec((8, LANES), lambda i, j, k: (0, 0)))
        args.append(dep)
    return pl.pallas_call(
        body, name=name, grid=(M // tm, N // tn, nk), in_specs=in_specs, out_specs=o_spec,
        out_shape=jax.ShapeDtypeStruct((M, N), out_dtype),
        scratch_shapes=[pltpu.VMEM((tm, tn), F32)] if nk > 1 else [],
        compiler_params=_cparams(("parallel", "parallel", "arbitrary")),
    )(*args)


MXU_COLS = 256


def _col_chunks(n):
    return [(lo, min(lo + MXU_COLS, n)) for lo in range(0, n, MXU_COLS)]


def ffn_up_act(h, wu4, *, name):
    T, D = h.shape
    cq = wu4.shape[2]
    Fh = 2 * cq
    tm = _pick(T, (512, 256, 128))

    def body(h_ref, wg_ref, wu_ref, g_ref, u_ref, a_ref):
        hv = h_ref[...]
        for lo, hi in _col_chunks(cq):
            gv = jnp.dot(hv, wg_ref[0, :, lo:hi], preferred_element_type=F32)
            uv = jnp.dot(hv, wu_ref[0, :, lo:hi], preferred_element_type=F32)
            g_ref[:, lo:hi] = gv.astype(BF16)
            u_ref[:, lo:hi] = uv.astype(BF16)
            a_ref[:, lo:hi] = (gv * _sigmoid(gv) * uv).astype(BF16)

    tile = pl.BlockSpec((tm, cq), lambda j, i: (i, j))
    sh = jax.ShapeDtypeStruct((T, Fh), BF16)
    return pl.pallas_call(
        body, name=name, grid=(2, T // tm),
        in_specs=[pl.BlockSpec((tm, D), lambda j, i: (i, 0)), pl.BlockSpec((1, D, cq), lambda j, i: (j, 0, 0)),
                  pl.BlockSpec((1, D, cq), lambda j, i: (2 + j, 0, 0))],
        out_specs=[tile, tile, tile], out_shape=[sh, sh, sh],
        compiler_params=_cparams(("parallel", "parallel")),
    )(h, wu4, wu4)


def ffn_down_dx_act(dy, wd, g, u, *, dep=None, name):
    T, D = dy.shape
    Fh = wd.shape[0]
    cq = Fh // 2
    tm = _pick(T, (512, 256, 128))

    def body(dy_ref, wd_ref, g_ref, u_ref, *rest):
        dg_ref, du_ref = rest[-2:]
        dyv = dy_ref[...].astype(BF16)
        for lo, hi in _col_chunks(cq):
            da = 0.5 * lax.dot_general(dyv, wd_ref[lo:hi, :], _NT, preferred_element_type=F32)
            gv = g_ref[:, lo:hi].astype(F32)
            uv = u_ref[:, lo:hi].astype(F32)
            s = _sigmoid(gv)
            dg_ref[:, lo:hi] = (da * uv * (s * (1.0 + gv * (1.0 - s)))).astype(BF16)
            du_ref[:, lo:hi] = (da * (gv * s)).astype(BF16)

    tile = pl.BlockSpec((tm, cq), lambda j, i: (i, j))
    sh = jax.ShapeDtypeStruct((T, Fh), BF16)
    in_specs = [pl.BlockSpec((tm, D), lambda j, i: (i, 0)), pl.BlockSpec((cq, D), lambda j, i: (j, 0)), tile, tile]
    args = [dy, wd, g, u]
    if dep is not None:
        in_specs.append(pl.BlockSpec((8, LANES), lambda j, i: (0, 0)))
        args.append(dep)
    return pl.pallas_call(
        body, name=name, grid=(2, T // tm), in_specs=in_specs, out_specs=[tile, tile], out_shape=[sh, sh],
        compiler_params=_cparams(("parallel", "parallel")),
    )(*args)


def ffn_up_dw(h, dg, du, *, name):
    T, D = h.shape
    cq = dg.shape[1] // 2
    tk = _pick(T, (1024, 512, 256, 128))
    nk = T // tk

    def body(h_ref, dg_ref, du_ref, o_ref, acc):
        p = pl.program_id(0)
        k = pl.program_id(1)

        @pl.when(k == 0)
        def _():
            acc[...] = jnp.zeros_like(acc)

        @pl.when(p < 2)
        def _():
            acc[...] += lax.dot_general(h_ref[...], dg_ref[...], _TN, preferred_element_type=F32)

        @pl.when(p >= 2)
        def _():
            acc[...] += lax.dot_general(h_ref[...], du_ref[...], _TN, preferred_element_type=F32)

        @pl.when(k == nk - 1)
        def _():
            o_ref[0] = acc[...].astype(BF16)

    return pl.pallas_call(
        body, name=name, grid=(4, nk),
        in_specs=[pl.BlockSpec((tk, D), lambda p, k: (k, 0)),
                  pl.BlockSpec((tk, cq), lambda p, k: (jnp.where(p < 2, k, 0), jnp.minimum(p, 1))),
                  pl.BlockSpec((tk, cq), lambda p, k: (jnp.where(p < 2, 0, k), jnp.maximum(p - 2, 0)))],
        out_specs=pl.BlockSpec((1, D, cq), lambda p, k: (p, 0, 0)),
        out_shape=jax.ShapeDtypeStruct((4, D, cq), BF16),
        scratch_shapes=[pltpu.VMEM((D, cq), F32)],
        compiler_params=_cparams(("parallel", "arbitrary")),
    )(h, dg, du)


def ffn_up_dx_norm(dg, du, wu4, x, g, dy, *, name):
    T = dg.shape[0]
    _, D, cq = wu4.shape
    tm = _pick(T, (512, 256, 128))

    def body(dg_ref, du_ref, wg_ref, wu_ref, x_ref, g_ref, dy_ref, dx_ref, dgain_ref, acc):
        i = pl.program_id(0)
        k = pl.program_id(1)
        part = lax.dot_general(dg_ref[...], wg_ref[0], _NT, preferred_element_type=F32)
        part = part + lax.dot_general(du_ref[...], wu_ref[0], _NT, preferred_element_type=F32)

        @pl.when(jnp.logical_and(i == 0, k == 0))
        def _():
            dgain_ref[...] = jnp.zeros_like(dgain_ref)

        @pl.when(k == 0)
        def _():
            acc[...] = part

        @pl.when(k == 1)
        def _():
            dx, dgain = _rms_bwd_math(x_ref[...], g_ref[...], acc[...] + part)
            dx_ref[...] = dy_ref[...] + dx
            dgain_ref[...] += dgain

    tile = pl.BlockSpec((tm, cq), lambda i, k: (i, k))
    row = pl.BlockSpec((tm, D), lambda i, k: (i, 0))
    vec = pl.BlockSpec((1, D), lambda i, k: (0, 0))
    return pl.pallas_call(
        body, name=name, grid=(T // tm, 2),
        in_specs=[tile, tile, pl.BlockSpec((1, D, cq), lambda i, k: (k, 0, 0)),
                  pl.BlockSpec((1, D, cq), lambda i, k: (2 + k, 0, 0)), row, vec, row],
        out_specs=[row, vec],
        out_shape=[jax.ShapeDtypeStruct((T, D), F32), jax.ShapeDtypeStruct((1, D), F32)],
        scratch_shapes=[pltpu.VMEM((tm, D), F32)],
        compiler_params=_cparams(("arbitrary", "arbitrary")),
    )(dg, du, wu4, wu4, x, g.reshape(1, D), dy)


def mix_in(h, w_lat, w_pool, w_gate, *, name):
    T, D = h.shape
    tm = _pick(T, (512, 256, 128))

    def body(h_ref, wl_ref, wp_ref, wg_ref, lat_ref, xp_ref, gl_ref):
        hv = h_ref[...]
        lat_ref[...] = jnp.dot(hv, wl_ref[...], preferred_element_type=F32)
        xp_ref[...] = jnp.dot(hv, wp_ref[...], preferred_element_type=F32)
        gl_ref[...] = jnp.dot(hv, wg_ref[...], preferred_element_type=F32).astype(BF16)

    whole = lambda a: pl.BlockSpec(a.shape, lambda i: (0, 0))
    out = lambda a: pl.BlockSpec((tm, a.shape[1]), lambda i: (i, 0))
    return pl.pallas_call(
        body, name=name, grid=(T // tm,),
        in_specs=[pl.BlockSpec((tm, D), lambda i: (i, 0)), whole(w_lat), whole(w_pool), whole(w_gate)],
        out_specs=[out(w_lat), out(w_pool), out(w_gate)],
        out_shape=[jax.ShapeDtypeStruct((T, w_lat.shape[1]), F32), jax.ShapeDtypeStruct((T, w_pool.shape[1]), F32),
                   jax.ShapeDtypeStruct((T, w_gate.shape[1]), BF16)],
        compiler_params=_cparams(("parallel",)),
    )(h, w_lat, w_pool, w_gate)


def mix_in_dx_norm(dlat, dxp, dgl, w_lat, w_pool, w_gate, x, g, dy, *, name):
    T, D = x.shape
    tm = _pick(T, (512, 256, 128))

    def body(dlat_ref, dxp_ref, dgl_ref, wl_ref, wp_ref, wg_ref, x_ref, g_ref, dy_ref, dx_ref, dgain_ref):
        @pl.when(pl.program_id(0) == 0)
        def _():
            dgain_ref[...] = jnp.zeros_like(dgain_ref)

        dh = lax.dot_general(dlat_ref[...], wl_ref[...], _NT, preferred_element_type=F32)
        dh = dh + lax.dot_general(dxp_ref[...], wp_ref[...], _NT, preferred_element_type=F32)
        dh = dh + lax.dot_general(dgl_ref[...], wg_ref[...], _NT, preferred_element_type=F32)
        dx, dgain = _rms_bwd_math(x_ref[...], g_ref[...], dh)
        dx_ref[...] = dy_ref[...] + dx
        dgain_ref[...] += dgain

    row = lambda a: pl.BlockSpec((tm, a.shape[1]), lambda i: (i, 0))
    whole = lambda a: pl.BlockSpec(a.shape, lambda i: (0, 0))
    vec = pl.BlockSpec((1, D), lambda i: (0, 0))
    return pl.pallas_call(
        body, name=name, grid=(T // tm,),
        in_specs=[row(dlat), row(dxp), row(dgl), whole(w_lat), whole(w_pool), whole(w_gate), row(x), vec, row(dy)],
        out_specs=[row(x), vec],
        out_shape=[jax.ShapeDtypeStruct((T, D), F32), jax.ShapeDtypeStruct((1, D), F32)],
        compiler_params=_cparams(("arbitrary",)),
    )(dlat, dxp, dgl, w_lat, w_pool, w_gate, x, g.reshape(1, D), dy)


def _rows(T, width_bytes):
    cap = max(8, (2 * 1024 * 1024) // width_bytes)
    return _pick(T, tuple(c for c in (1024, 512, 256, 128, 64, 32, 16) if c <= cap))


def rms_fwd(x, g, *, name, dep=None):
    T, D = x.shape
    tm = _rows(T, D * 4)

    def body(x_ref, g_ref, *rest):
        xv = x_ref[...]
        r = lax.rsqrt(jnp.mean(xv * xv, axis=-1, keepdims=True) + NORM_EPS)
        rest[-1][...] = (xv * r * g_ref[...]).astype(BF16)

    in_specs = [pl.BlockSpec((tm, D), lambda i: (i, 0)), pl.BlockSpec((1, D), lambda i: (0, 0))]
    args = [x, g.reshape(1, D)]
    if dep is not None:
        in_specs.append(pl.BlockSpec((8, LANES), lambda i: (0, 0)))
        args.append(dep)
    return pl.pallas_call(
        body, name=name, grid=(T // tm,), in_specs=in_specs,
        out_specs=pl.BlockSpec((tm, D), lambda i: (i, 0)),
        out_shape=jax.ShapeDtypeStruct((T, D), BF16),
        compiler_params=_cparams(("parallel",)),
    )(*args)


def _rms_bwd_math(xv, gv, dh):
    r = lax.rsqrt(jnp.mean(xv * xv, axis=-1, keepdims=True) + NORM_EPS)
    xn = xv * r
    dg = jnp.sum(dh * xn, axis=0, keepdims=True)
    dxn = dh * gv
    dx = r * (dxn - xn * jnp.mean(dxn * xn, axis=-1, keepdims=True))
    return dx, dg


def _rope(xv, cv, sv):
    half = QK_ROPE_DIM // 2
    lane = lax.broadcasted_iota(jnp.int32, xv.shape, 1)
    swapped = jnp.where(lane < half, pltpu.roll(xv, LANES - half, 1), pltpu.roll(xv, half, 1))
    return xv * cv + swapped * sv


def _rope_t(dy, cv, sv):
    half = QK_ROPE_DIM // 2
    ds = dy * sv
    lane = lax.broadcasted_iota(jnp.int32, dy.shape, 1)
    swapped = jnp.where(lane < half, pltpu.roll(ds, LANES - half, 1), pltpu.roll(ds, half, 1))
    return dy * cv + swapped


def lat_fwd(lat, qn_w, kvn_w, cs, sn, *, name):
    T = lat.shape[0]
    tm = _rows(T, LAT_DIM * 4)
    kv0 = Q_LORA_RANK
    kr0 = Q_LORA_RANK + KV_LORA_RANK

    def body(lat_ref, qw_ref, kw_ref, c_ref, s_ref, qn_ref, kvn_ref, kr_ref):
        ql = lat_ref[:, :kv0]
        r = lax.rsqrt(jnp.mean(ql * ql, axis=-1, keepdims=True) + NORM_EPS)
        qn_ref[...] = (ql * r * qw_ref[...]).astype(BF16)
        kl = lat_ref[:, kv0:kr0]
        r = lax.rsqrt(jnp.mean(kl * kl, axis=-1, keepdims=True) + NORM_EPS)
        kvn_ref[...] = (kl * r * kw_ref[...]).astype(BF16)
        kr_ref[...] = _rope(lat_ref[:, kr0:], c_ref[...], s_ref[...]).astype(BF16)

    return pl.pallas_call(
        body, name=name, grid=(T // tm,),
        in_specs=[pl.BlockSpec((tm, LAT_DIM), lambda i: (i, 0)),
                  pl.BlockSpec((1, Q_LORA_RANK), lambda i: (0, 0)),
                  pl.BlockSpec((1, KV_LORA_RANK), lambda i: (0, 0)),
                  pl.BlockSpec((tm, LANES), lambda i: (i, 0)), pl.BlockSpec((tm, LANES), lambda i: (i, 0))],
        out_specs=[pl.BlockSpec((tm, Q_LORA_RANK), lambda i: (i, 0)),
                   pl.BlockSpec((tm, KV_LORA_RANK), lambda i: (i, 0)),
                   pl.BlockSpec((tm, LANES), lambda i: (i, 0))],
        out_shape=[jax.ShapeDtypeStruct((T, Q_LORA_RANK), BF16), jax.ShapeDtypeStruct((T, KV_LORA_RANK), BF16),
                   jax.ShapeDtypeStruct((T, LANES), BF16)],
        compiler_params=_cparams(("parallel",)),
    )(lat, qn_w.reshape(1, -1), kvn_w.reshape(1, -1), cs, sn)


def lat_bwd(lat, qn_w, kvn_w, dqn, dkvn, dkr, cs, sn, *, name):
    T = lat.shape[0]
    tm = _rows(T, LAT_DIM * 4)
    kv0 = Q_LORA_RANK
    kr0 = Q_LORA_RANK + KV_LORA_RANK

    def body(lat_ref, qw_ref, kw_ref, dqn_ref, dkvn_ref, dkr_ref, c_ref, s_ref, dlat_ref, dqw_ref, dkw_ref):
        @pl.when(pl.program_id(0) == 0)
        def _():
            dqw_ref[...] = jnp.zeros_like(dqw_ref)
            dkw_ref[...] = jnp.zeros_like(dkw_ref)

        dx, dg = _rms_bwd_math(lat_ref[:, :kv0], qw_ref[...], dqn_ref[...])
        dlat_ref[:, :kv0] = dx.astype(BF16)
        dqw_ref[...] += dg
        dx, dg = _rms_bwd_math(lat_ref[:, kv0:kr0], kw_ref[...], dkvn_ref[...])
        dlat_ref[:, kv0:kr0] = dx.astype(BF16)
        dkw_ref[...] += dg
        dlat_ref[:, kr0:] = _rope_t(dkr_ref[...], c_ref[...], s_ref[...]).astype(BF16)

    row = lambda w: pl.BlockSpec((tm, w), lambda i: (i, 0))
    vec = lambda w: pl.BlockSpec((1, w), lambda i: (0, 0))
    return pl.pallas_call(
        body, name=name, grid=(T // tm,),
        in_specs=[row(LAT_DIM), vec(Q_LORA_RANK), vec(KV_LORA_RANK), row(Q_LORA_RANK), row(KV_LORA_RANK),
                  row(LANES), row(LANES), row(LANES)],
        out_specs=[row(LAT_DIM), vec(Q_LORA_RANK), vec(KV_LORA_RANK)],
        out_shape=[jax.ShapeDtypeStruct((T, LAT_DIM), BF16), jax.ShapeDtypeStruct((1, Q_LORA_RANK), F32),
                   jax.ShapeDtypeStruct((1, KV_LORA_RANK), F32)],
        compiler_params=_cparams(("arbitrary",)),
    )(lat, qn_w.reshape(1, -1), kvn_w.reshape(1, -1), dqn, dkvn, dkr, cs, sn)


def attn_proj_gate(o, wap, gl, bg, ba, *, name):
    T, D2 = gl.shape
    D = D2 // 2
    tm = _pick(T, (512, 256, 128))

    def body(o_ref, w_ref, gl_ref, bg_ref, ba_ref, bb_ref, m_ref):
        bb = jnp.dot(o_ref[...], w_ref[...], preferred_element_type=F32)
        bb_ref[...] = bb.astype(BF16)
        ga = _sigmoid(gl_ref[:, :D].astype(F32) + bg_ref[:, :D])
        gb = _sigmoid(gl_ref[:, D:].astype(F32) + bg_ref[:, D:])
        m_ref[...] = (ga * ba_ref[...].astype(F32) + gb * bb).astype(BF16)

    row = lambda w: pl.BlockSpec((tm, w), lambda i: (i, 0))
    return pl.pallas_call(
        body, name=name, grid=(T // tm,),
        in_specs=[row(o.shape[1]), pl.BlockSpec(wap.shape, lambda i: (0, 0)), row(D2),
                  pl.BlockSpec((1, D2), lambda i: (0, 0)), row(D)],
        out_specs=[row(D), row(D)],
        out_shape=[jax.ShapeDtypeStruct((T, D), BF16), jax.ShapeDtypeStruct((T, D), BF16)],
        compiler_params=_cparams(("parallel",)),
    )(o, wap, gl, bg.reshape(1, D2), ba)


def out_dx_gate(dy, wo, gl, bg, ba, bb, *, name):
    T, D2 = gl.shape
    D = D2 // 2
    tm = _pick(T, (512, 256, 128))

    def body(dy_ref, w_ref, gl_ref, bg_ref, ba_ref, bb_ref, dba_ref, dbb_ref, dgl_ref, dbg_ref):
        @pl.when(pl.program_id(0) == 0)
        def _():
            dbg_ref[...] = jnp.zeros_like(dbg_ref)

        dmv = lax.dot_general(dy_ref[...].astype(BF16), w_ref[...], _NT, preferred_element_type=F32)
        ga = _sigmoid(gl_ref[:, :D].astype(F32) + bg_ref[:, :D])
        gb = _sigmoid(gl_ref[:, D:].astype(F32) + bg_ref[:, D:])
        dba_ref[...] = (dmv * ga).astype(BF16)
        dbb_ref[...] = (dmv * gb).astype(BF16)
        dla = dmv * ba_ref[...].astype(F32) * ga * (1.0 - ga)
        dlb = dmv * bb_ref[...].astype(F32) * gb * (1.0 - gb)
        dgl_ref[:, :D] = dla.astype(BF16)
        dgl_ref[:, D:] = dlb.astype(BF16)
        dbg_ref[:, :D] += jnp.sum(dla, axis=0, keepdims=True)
        dbg_ref[:, D:] += jnp.sum(dlb, axis=0, keepdims=True)

    row = lambda w: pl.BlockSpec((tm, w), lambda i: (i, 0))
    vec = pl.BlockSpec((1, D2), lambda i: (0, 0))
    return pl.pallas_call(
        body, name=name, grid=(T // tm,),
        in_specs=[row(D), pl.BlockSpec(wo.shape, lambda i: (0, 0)), row(D2), vec, row(D), row(D)],
        out_specs=[row(D), row(D), row(D2), vec],
        out_shape=[jax.ShapeDtypeStruct((T, D), BF16), jax.ShapeDtypeStruct((T, D), BF16),
                   jax.ShapeDtypeStruct((T, D2), BF16), jax.ShapeDtypeStruct((1, D2), F32)],
        compiler_params=_cparams(("arbitrary",)),
    )(dy, wo, gl, bg.reshape(1, D2), ba, bb)


def loss_head(x, gf, tgt, *, name):
    T, D = x.shape
    tm = _rows(T, D * 4)

    def body(x_ref, g_ref, t_ref, dx_ref, dg_ref, loss_ref):
        @pl.when(pl.program_id(0) == 0)
        def _():
            dg_ref[...] = jnp.zeros_like(dg_ref)
            loss_ref[...] = jnp.zeros_like(loss_ref)

        xv = x_ref[...]
        gv = g_ref[...]
        r = lax.rsqrt(jnp.mean(xv * xv, axis=-1, keepdims=True) + NORM_EPS)
        xn = xv * r
        err = xn * gv - t_ref[...]
        loss_ref[...] += 0.5 * jnp.sum(jnp.mean(err * err, axis=-1, keepdims=True))
        dy = err * (1.0 / D)
        dg_ref[...] += jnp.sum(dy * xn, axis=0, keepdims=True)
        dxn = dy * gv
        dx_ref[...] = r * (dxn - xn * jnp.mean(dxn * xn, axis=-1, keepdims=True))

    row = pl.BlockSpec((tm, D), lambda i: (i, 0))
    vec = pl.BlockSpec((1, D), lambda i: (0, 0))
    return pl.pallas_call(
        body, name=name, grid=(T // tm,), in_specs=[row, vec, row],
        out_specs=[row, vec, pl.BlockSpec((8, LANES), lambda i: (0, 0))],
        out_shape=[jax.ShapeDtypeStruct((T, D), F32), jax.ShapeDtypeStruct((1, D), F32),
                   jax.ShapeDtypeStruct((8, LANES), F32)],
        compiler_params=_cparams(("arbitrary",)),
    )(x, gf.reshape(1, D), tgt)


def _shift_rows(s, k, down):
    n = s.shape[0]
    t = lax.broadcasted_iota(jnp.int32, s.shape, 0)
    if down:
        return jnp.where(t >= k, pltpu.roll(s, k, 0), 0.0)
    return jnp.where(t < n - k, pltpu.roll(s, n - k, 0), 0.0)


def _window_sum(s, w, down):
    k = 1
    while k < w:
        s = s + _shift_rows(s, k, down)
        k *= 2
    return s


def _pool_count(shape, w):
    t = lax.broadcasted_iota(jnp.int32, shape, 0)
    return jnp.minimum(t + 1, w).astype(F32)


def pool_fwd(xp, maps, scale, B, *, name):
    T, P = xp.shape
    S = T // B
    G = POOL_GROUP_DIM

    def body(x_ref, m_ref, sc_ref, o_ref):
        for g, w in enumerate(POOL_WINDOWS):
            xg = x_ref[:, g * G:(g + 1) * G]
            pooled = _window_sum(xg, w, True) / _pool_count(xg.shape, w) - xg
            mixed = jnp.dot(pooled.astype(BF16), m_ref[g], preferred_element_type=F32)
            o_ref[:, g * G:(g + 1) * G] = (mixed * sc_ref[:, g * G:(g + 1) * G]).astype(BF16)

    return pl.pallas_call(
        body, name=name, grid=(B,),
        in_specs=[pl.BlockSpec((S, P), lambda b: (b, 0)), pl.BlockSpec((N_POOL_GROUPS, G, G), lambda b: (0, 0, 0)),
                  pl.BlockSpec((1, P), lambda b: (0, 0))],
        out_specs=pl.BlockSpec((S, P), lambda b: (b, 0)),
        out_shape=jax.ShapeDtypeStruct((T, P), BF16),
        compiler_params=_cparams(("parallel",)),
    )(xp, maps, scale.reshape(1, P))


def pool_bwd(xp, dmixed, maps, scale, B, *, name):
    T, P = xp.shape
    S = T // B
    G = POOL_GROUP_DIM

    def body(x_ref, dm_ref, m_ref, sc_ref, dx_ref, dmaps_ref, dsc_ref):
        @pl.when(pl.program_id(0) == 0)
        def _():
            dmaps_ref[...] = jnp.zeros_like(dmaps_ref)
            dsc_ref[...] = jnp.zeros_like(dsc_ref)

        for g, w in enumerate(POOL_WINDOWS):
            cols = slice(g * G, (g + 1) * G)
            xg = x_ref[:, cols]
            cnt = _pool_count(xg.shape, w)
            pooled = (_window_sum(xg, w, True) / cnt - xg).astype(BF16)
            mixed = jnp.dot(pooled, m_ref[g], preferred_element_type=F32)
            dmx = dm_ref[:, cols]
            dsc_ref[:, cols] += jnp.sum(dmx * mixed, axis=0, keepdims=True)
            dmp = (dmx * sc_ref[:, cols]).astype(BF16)
            dmaps_ref[g] += lax.dot_general(pooled, dmp, (((0,), (0,)), ((), ())), preferred_element_type=F32)
            dpooled = lax.dot_general(dmp, m_ref[g], (((1,), (1,)), ((), ())), preferred_element_type=F32)
            dx_ref[:, cols] = (_window_sum(dpooled / cnt, w, False) - dpooled).astype(BF16)

    blk = pl.BlockSpec((S, P), lambda b: (b, 0))
    mp = pl.BlockSpec((N_POOL_GROUPS, G, G), lambda b: (0, 0, 0))
    vec = pl.BlockSpec((1, P), lambda b: (0, 0))
    return pl.pallas_call(
        body, name=name, grid=(B,), in_specs=[blk, blk, mp, vec], out_specs=[blk, mp, vec],
        out_shape=[jax.ShapeDtypeStruct((T, P), BF16), jax.ShapeDtypeStruct((N_POOL_GROUPS, G, G), F32),
                   jax.ShapeDtypeStruct((1, P), F32)],
        compiler_params=_cparams(("arbitrary",)),
    )(xp, dmixed, maps, scale.reshape(1, P))


def _keys(kv_ref, kr_ref, rows):
    return jnp.concatenate([kv_ref[rows, :QK_NOPE_DIM], kr_ref[rows, :]], axis=1)


def _causal(s):
    row = lax.broadcasted_iota(jnp.int32, s.shape, 0)
    col = lax.broadcasted_iota(jnp.int32, s.shape, 1)
    return jnp.where(row >= col, s, NEG_INF)


def attn_fwd(q, kv, kr, cs, sn, B, *, name):
    T = q.shape[0]
    S = T // B
    blk = min(ATT_BLOCK, S)
    nb = S // blk
    H = N_HEADS
    scale = QK_DIM ** -0.5

    def body(q_ref, kv_ref, kr_ref, c_ref, s_ref, o_ref, lse_ref, qs_ref):
        qs_ref[:, :QK_NOPE_DIM] = (q_ref[:, :QK_NOPE_DIM] * scale).astype(BF16)
        qs_ref[:, QK_NOPE_DIM:] = _rope(q_ref[:, QK_NOPE_DIM:], c_ref[...] * scale, s_ref[...] * scale).astype(BF16)
        for qi in range(nb):
            rows = slice(qi * blk, (qi + 1) * blk)
            qb = qs_ref[rows, :]
            sd = _causal(lax.dot_general(qb, _keys(kv_ref, kr_ref, rows), _NT, preferred_element_type=F32))
            m = jnp.max(sd, axis=-1, keepdims=True)
            if qi > 0:
                prev = slice(0, qi * blk)
                sp = lax.dot_general(qb, _keys(kv_ref, kr_ref, prev), _NT, preferred_element_type=F32)
                m = jnp.maximum(m, jnp.max(sp, axis=-1, keepdims=True))
            pd = jnp.exp(sd - m)
            l = jnp.sum(pd, axis=-1, keepdims=True)
            acc = jnp.dot(pd.astype(BF16), kv_ref[rows, QK_NOPE_DIM:], preferred_element_type=F32)
            if qi > 0:
                pp = jnp.exp(sp - m)
                l = l + jnp.sum(pp, axis=-1, keepdims=True)
                acc = acc + jnp.dot(pp.astype(BF16), kv_ref[prev, QK_NOPE_DIM:], preferred_element_type=F32)
            o_ref[rows, :] = (acc / l).astype(BF16)
            lse_ref[0, rows, :] = m + jnp.log(l)

    head = pl.BlockSpec((S, HEAD_PAD), lambda b, h: (b, h))
    shared = pl.BlockSpec((S, LANES), lambda b, h: (b, 0))
    return pl.pallas_call(
        body, name=name, grid=(B, H),
        in_specs=[head, head, shared, shared, shared],
        out_specs=[pl.BlockSpec((S, V_HEAD_DIM), lambda b, h: (b, h)), pl.BlockSpec((1, S, 1), lambda b, h: (h, b, 0)),
                   head],
        out_shape=[jax.ShapeDtypeStruct((T, H * V_HEAD_DIM), BF16), jax.ShapeDtypeStruct((H, T, 1), F32),
                   jax.ShapeDtypeStruct((T, H * HEAD_PAD), BF16)],
        compiler_params=_cparams(("parallel", "parallel")),
    )(q, kv, kr, cs, sn)


def attn_bwd(q, kv, kr, o, do, lse, cs, sn, B, *, name):
    T = q.shape[0]
    S = T // B
    blk = min(ATT_BLOCK, S)
    nb = S // blk
    H = N_HEADS
    scale = QK_DIM ** -0.5

    def body(q_ref, kv_ref, kr_ref, o_ref, do_ref, lse_ref, c_ref, s_ref, dq_ref, dkv_ref, dkr_ref, dk_s, dv_s):
        dk_s[...] = jnp.zeros_like(dk_s)
        dv_s[...] = jnp.zeros_like(dv_s)

        @pl.when(pl.program_id(1) == 0)
        def _():
            dkr_ref[...] = jnp.zeros_like(dkr_ref)

        for qi in range(nb):
            rows = slice(qi * blk, (qi + 1) * blk)
            qb = q_ref[rows, :]
            dob = do_ref[rows, :]
            delta = jnp.sum(dob.astype(F32) * o_ref[rows, :].astype(F32), axis=-1, keepdims=True)
            lse_b = lse_ref[0, rows, :]

            def part(ks, diagonal):
                k = _keys(kv_ref, kr_ref, ks)
                s = lax.dot_general(qb, k, _NT, preferred_element_type=F32)
                if diagonal:
                    s = _causal(s)
                p = jnp.exp(s - lse_b)
                dp = lax.dot_general(dob, kv_ref[ks, QK_NOPE_DIM:], _NT, preferred_element_type=F32)
                ds = (p * (dp - delta)).astype(BF16)
                dv_s[ks, :] += lax.dot_general(p.astype(BF16), dob, _TN, preferred_element_type=F32)
                dk_s[ks, :] += lax.dot_general(ds, qb, _TN, preferred_element_type=F32)
                return jnp.dot(ds, k, preferred_element_type=F32)

            dq = part(rows, True)
            if qi > 0:
                dq = dq + part(slice(0, qi * blk), False)
            dq_ref[rows, :QK_NOPE_DIM] = (dq[:, :QK_NOPE_DIM] * scale).astype(BF16)
            dq_ref[rows, QK_NOPE_DIM:] = _rope_t(dq[:, QK_NOPE_DIM:], c_ref[rows, :] * scale,
                                                 s_ref[rows, :] * scale).astype(BF16)

        dkv_ref[:, :QK_NOPE_DIM] = dk_s[:, :QK_NOPE_DIM].astype(BF16)
        dkv_ref[:, QK_NOPE_DIM:] = dv_s[...].astype(BF16)
        dkr_ref[...] += dk_s[:, QK_NOPE_DIM:]

    head = lambda w: pl.BlockSpec((S, w), lambda b, h: (b, h))
    shared = pl.BlockSpec((S, LANES), lambda b, h: (b, 0))
    return pl.pallas_call(
        body, name=name, grid=(B, H),
        in_specs=[head(HEAD_PAD), head(HEAD_PAD), shared, head(V_HEAD_DIM), head(V_HEAD_DIM),
                  pl.BlockSpec((1, S, 1), lambda b, h: (h, b, 0)), shared, shared],
        out_specs=[head(HEAD_PAD), head(HEAD_PAD), shared],
        out_shape=[jax.ShapeDtypeStruct((T, H * HEAD_PAD), BF16), jax.ShapeDtypeStruct((T, H * HEAD_PAD), BF16),
                   jax.ShapeDtypeStruct((T, LANES), F32)],
        scratch_shapes=[pltpu.VMEM((S, HEAD_PAD), F32), pltpu.VMEM((S, V_HEAD_DIM), F32)],
        compiler_params=_cparams(("parallel", "arbitrary")),
    )(q, kv, kr, o, do, lse, cs, sn)


def adamw(w, g, m, v, *, name, dep=None, copy_g=False):
    R, C = w.shape
    cap = max(8, (1024 * 1024) // (C * 4))
    tr = _pick(R, tuple(c for c in (1024, 512, 256, 128, 64, 32, 16, 8) if c <= cap))
    c1 = 1.0 - ADAM_B1 ** ADAM_STEP
    c2 = 1.0 - ADAM_B2 ** ADAM_STEP
    nout = 4 if copy_g else 3

    def body(w_ref, g_ref, m_ref, v_ref, *rest):
        d_ref, nm_ref, nv_ref = rest[-3:]
        gv = g_ref[...]
        if copy_g:
            rest[-4][...] = gv
        mv = ADAM_B1 * m_ref[...] + (1.0 - ADAM_B1) * gv
        vv = ADAM_B2 * v_ref[...] + (1.0 - ADAM_B2) * (gv * gv)
        nm_ref[...] = mv
        nv_ref[...] = vv
        d_ref[...] = -ADAM_LR * ((mv / c1) / (jnp.sqrt(vv / c2) + ADAM_EPS) + ADAM_WD * w_ref[...])

    blk = pl.BlockSpec((tr, C), lambda i: (i, 0))
    sh = jax.ShapeDtypeStruct((R, C), F32)
    extra = [] if dep is None else [dep]
    return pl.pallas_call(
        body, name=name, grid=(R // tr,), in_specs=[blk] * 4 + [ANY] * len(extra), out_specs=[blk] * nout,
        out_shape=[sh] * nout, compiler_params=_cparams(("parallel",)),
    )(w, g, m, v, *extra)


ANY = pl.BlockSpec(memory_space=pl.ANY)


def _place():
    x, y, c = lax.axis_index("x"), lax.axis_index("y"), lax.axis_index("c")
    others = [(1 - x, y), (x, 1 - y), (1 - x, 1 - y)]
    return x, y, c, others


def _remote(src, dst, ssem, rsem, dev):
    return pltpu.make_async_remote_copy(src_ref=src, dst_ref=dst, send_sem=ssem, recv_sem=rsem,
                                        device_id=dev, device_id_type=MESH)


def _half(ref_rows, c):
    hr = ref_rows // 2
    return pl.ds(pl.multiple_of(c * hr, 16), hr)


HBM = pl.BlockSpec(memory_space=pltpu.HBM)
SEMS = pl.BlockSpec(memory_space=pltpu.SEMAPHORE)
EFFECT = pltpu.SideEffectType.DATAFLOW_SIDE_EFFECTING


def exchange_begin(name, srcs, land_shapes, plan, ncopies, after=None):
    ns, nl = len(srcs), len(land_shapes)
    nin = ns + nl + (0 if after is None else 1)

    def body(*refs):
        ssem, rsem = refs[nin], refs[nin + 1]
        for k, (s, d, dev) in enumerate(plan(refs[:ns], refs[ns:ns + nl])):
            _remote(s, d, ssem.at[k], rsem.at[k], dev).start()
        refs[-1][...] = jnp.zeros_like(refs[-1])

    bufs = [pltpu.HBM(s.shape, s.dtype) for s in srcs] + [pltpu.HBM(s.shape, s.dtype) for s in land_shapes]
    args = [pltpu.with_memory_space_constraint(s, pltpu.HBM) for s in srcs]
    args += [pltpu.with_memory_space_constraint(lax.empty(s.shape, s.dtype), pltpu.HBM) for s in land_shapes]
    if after is not None:
        args.append(after)
    out = pl.pallas_call(
        body, name=name,
        out_shape=(pltpu.SemaphoreType.DMA((ncopies,)), pltpu.SemaphoreType.DMA((ncopies,)), *bufs,
                   jax.ShapeDtypeStruct((8, LANES), F32)),
        in_specs=[HBM] * (ns + nl) + ([] if after is None else [ANY]),
        out_specs=(SEMS, SEMS, *([HBM] * (ns + nl)), pl.BlockSpec(memory_space=pltpu.VMEM)),
        input_output_aliases={i: 2 + i for i in range(ns + nl)},
        compiler_params=pltpu.CompilerParams(has_side_effects=EFFECT),
    )(*args)
    return (out[0], out[1], out[2:2 + ns], out[2 + ns:2 + ns + nl]), out[-1]


def exchange_end(name, handle, plan, after):
    ssem, rsem, srcs, lands = handle
    ns, nl = len(srcs), len(lands)

    def body(*refs):
        ssem_ref, rsem_ref = refs[ns + nl], refs[ns + nl + 1]
        for k, (s, d, dev) in enumerate(plan(refs[:ns], refs[ns:ns + nl])):
            cp = _remote(s, d, ssem_ref.at[k], rsem_ref.at[k], dev)
            cp.wait_send()
            cp.wait_recv()

    out = pl.pallas_call(
        body, name=name,
        out_shape=tuple(pltpu.HBM(s.shape, s.dtype) for s in (*srcs, *lands)),
        in_specs=[HBM] * (ns + nl) + [SEMS, SEMS, ANY], out_specs=tuple([HBM] * (ns + nl)),
        input_output_aliases={i: i for i in range(ns + nl)},
        compiler_params=pltpu.CompilerParams(has_side_effects=EFFECT),
    )(*srcs, *lands, ssem, rsem, after)
    return list(out[:ns]), list(out[ns:])


def ag_plan(src_refs, land_refs):
    x, y, c, others = _place()
    plan = []
    for s, d in zip(src_refs, land_refs):
        mine = _half(s.shape[0], c)
        for ox, oy in others:
            plan.append((s.at[mine, :], d.at[2 * x + y, mine, :], (ox, oy, c)))
    return plan


def ag_forward_plan(src_refs, land_refs):
    x, y, c, others = _place()
    plan = []
    for s in src_refs:
        mine = _half(s.shape[1], c)
        for ox, oy in others:
            blk = s.at[2 * ox + oy, mine, :]
            plan.append((blk, blk, (x, y, 1 - c)))
    return plan


def rs_swap_plan(src_refs, land_refs):
    x, y, c, _ = _place()
    return [(s.at[:, _half(s.shape[1], 1 - c), :], d, (x, y, 1 - c)) for s, d in zip(src_refs, land_refs)]


def ag_forward(lands, *, name):
    n = len(lands)

    def body(*refs):
        ins, outs = refs[:n], refs[n:2 * n]
        ssem, rsem = refs[2 * n:]
        x, y, c, others = _place()
        sent = []
        for i in range(n):
            mine = _half(ins[i].shape[1], c)
            for j, (ox, oy) in enumerate(others):
                cp = _remote(ins[i].at[2 * ox + oy, mine, :], outs[i].at[2 * ox + oy, mine, :], ssem.at[3 * i + j],
                             rsem.at[3 * i + j], (x, y, 1 - c))
                cp.start()
                sent.append(cp)
        for cp in sent:
            cp.wait()

    return pl.pallas_call(
        body, name=name, in_specs=[ANY] * n, out_specs=[ANY] * n,
        out_shape=[jax.ShapeDtypeStruct(a.shape, a.dtype) for a in lands],
        input_output_aliases={i: i for i in range(n)},
        scratch_shapes=[pltpu.SemaphoreType.DMA((3 * n,)), pltpu.SemaphoreType.DMA((3 * n,))],
        compiler_params=pltpu.CompilerParams(has_side_effects=True),
    )(*lands)


def place_own(lands, own, chip, *, name):
    n = len(lands)
    steps = 4

    def body(chip_ref, *refs):
        for i in range(n):
            refs[2 * n + i][0] = refs[i][...]

    in_specs = [pl.BlockSpec((o.shape[0] // steps, o.shape[1]), lambda t, q: (t, 0)) for o in own] + [ANY] * n
    out_specs = [pl.BlockSpec((1, o.shape[0] // steps, o.shape[1]), lambda t, q: (q[0], t, 0)) for o in own]
    return pl.pallas_call(
        body, name=name,
        grid_spec=pltpu.PrefetchScalarGridSpec(num_scalar_prefetch=1, grid=(steps,), in_specs=in_specs,
                                               out_specs=out_specs),
        out_shape=[jax.ShapeDtypeStruct(a.shape, a.dtype) for a in lands],
        input_output_aliases={1 + n + i: i for i in range(n)},
        compiler_params=_cparams(("parallel",)),
    )(chip, *own, *lands)


def rs_swap_halves(grads, *, name, after=None):
    n = len(grads)
    extra = [] if after is None else [after]
    nin = n + len(extra)

    def body(*refs):
        ins, outs = refs[:n], refs[nin:nin + n]
        ssem, rsem = refs[nin + n:]
        x, y, c, _ = _place()
        cps = []
        for i in range(n):
            theirs = _half(ins[i].shape[1], 1 - c)
            cp = _remote(ins[i].at[:, theirs, :], outs[i], ssem.at[i], rsem.at[i], (x, y, 1 - c))
            cp.start()
            cps.append(cp)
        for cp in cps:
            cp.wait()

    return pl.pallas_call(
        body, name=name, in_specs=[ANY] * nin, out_specs=[ANY] * n,
        out_shape=[jax.ShapeDtypeStruct((4, g.shape[1] // 2, g.shape[2]), g.dtype) for g in grads],
        scratch_shapes=[pltpu.SemaphoreType.DMA((n,)), pltpu.SemaphoreType.DMA((n,))],
        compiler_params=pltpu.CompilerParams(has_side_effects=True),
    )(*grads, *extra)


def rs_chip_sum(g, r1, core, *, name):
    _, r, cdim = g.shape
    hr = r // 2

    def body(c_ref, g_ref, r1_ref, o_ref):
        o_ref[...] = (g_ref[...].astype(F32) + r1_ref[...].astype(F32)).astype(BF16)

    return pl.pallas_call(
        body, name=name,
        grid_spec=pltpu.PrefetchScalarGridSpec(
            num_scalar_prefetch=1, grid=(4,),
            in_specs=[pl.BlockSpec((1, hr, cdim), lambda qq, c_ref: (qq, c_ref[0], 0)),
                      pl.BlockSpec((1, hr, cdim), lambda qq, c_ref: (qq, 0, 0))],
            out_specs=pl.BlockSpec((1, hr, cdim), lambda qq, c_ref: (qq, 0, 0))),
        out_shape=jax.ShapeDtypeStruct((4, hr, cdim), BF16),
        compiler_params=_cparams(("parallel",)),
    )(core, g, r1)


def rs_plan(src_refs, land_refs):
    x, y, c, others = _place()
    plan = []
    for s, d in zip(src_refs, land_refs):
        for j, (ox, oy) in enumerate(others):
            plan.append((s.at[2 * ox + oy], d.at[j], (ox, oy, c)))
    return plan


def rs_final_sum(g, r1, r2, place, acc, l, *, name):
    _, r, cdim = g.shape
    hr = r // 2
    ch = hr // 2

    def body(p_ref, g_ref, r1_ref, a_ref, b_ref, d_ref, acc_in, o_ref):
        s = g_ref[...].astype(F32) + r1_ref[...].astype(F32)
        s = s + a_ref[...].astype(F32)
        s = s + b_ref[...].astype(F32)
        o_ref[...] = s + d_ref[...].astype(F32)

    other = lambda j: pl.BlockSpec((1, ch, cdim), lambda t, p_ref: (j, t, 0))
    return pl.pallas_call(
        body, name=name,
        grid_spec=pltpu.PrefetchScalarGridSpec(
            num_scalar_prefetch=1, grid=(2,),
            in_specs=[pl.BlockSpec((1, ch, cdim), lambda t, p_ref: (p_ref[0], 2 * p_ref[1] + t, 0)),
                      pl.BlockSpec((1, ch, cdim), lambda t, p_ref: (p_ref[0], t, 0)),
                      other(0), other(1), other(2), ANY],
            out_specs=pl.BlockSpec((1, ch, cdim), lambda t, p_ref: (l, 2 * p_ref[1] + t, 0))),
        out_shape=jax.ShapeDtypeStruct(acc.shape, F32),
        input_output_aliases={6: 0},
        compiler_params=_cparams(("parallel",)),
    )(place, g, r1, r2, r2, r2, acc)


def rs_join_halves(grads, *, name):
    n = len(grads)

    def body(*refs):
        ins, outs = refs[:n], refs[n:2 * n]
        ssem, rsem = refs[2 * n:]
        x, y, c, _ = _place()
        cps = []
        for i in range(n):
            mine = _half(ins[i].shape[1], c)
            cp = _remote(ins[i].at[:, mine, :], outs[i].at[:, mine, :], ssem.at[i], rsem.at[i], (x, y, 1 - c))
            cp.start()
            cps.append(cp)
        for cp in cps:
            cp.wait()

    return pl.pallas_call(
        body, name=name, in_specs=[ANY] * n, out_specs=[ANY] * n,
        out_shape=[jax.ShapeDtypeStruct(g.shape, g.dtype) for g in grads],
        input_output_aliases={i: i for i in range(n)},
        scratch_shapes=[pltpu.SemaphoreType.DMA((n,)), pltpu.SemaphoreType.DMA((n,))],
        compiler_params=pltpu.CompilerParams(has_side_effects=True),
    )(*grads)


def all_reduce_small(vs):
    n = len(vs)

    def body(*refs):
        v_refs, o_refs = refs[:n], refs[n:2 * n]
        sibs, sums, gots = refs[2 * n:3 * n], refs[3 * n:4 * n], refs[4 * n:5 * n]
        ssem, rsem = refs[5 * n:]
        x, y, c, others = _place()
        q = 2 * x + y
        sib = (x, y, 1 - c)

        def wait_all(cps):
            for cp in cps:
                cp.wait()

        cps = []
        for i in range(n):
            hr = vs[i].shape[0] // 2
            theirs = pl.ds(pl.multiple_of((1 - c) * hr, 8), hr)
            cps.append(_remote(v_refs[i].at[theirs, :], sibs[i], ssem.at[8 * i], rsem.at[8 * i], sib))
            cps[-1].start()
        wait_all(cps)
        cps = []
        for i in range(n):
            hr, rs = vs[i].shape[0] // 2, vs[i].shape[0] // 8
            sums[i][...] = v_refs[i][pl.ds(pl.multiple_of(c * hr, 8), hr), :] + sibs[i][...]
            for j, (ox, oy) in enumerate(others):
                piece = pl.ds(pl.multiple_of((2 * ox + oy) * rs, 8), rs)
                cps.append(_remote(sums[i].at[piece, :], gots[i].at[j], ssem.at[8 * i + 1 + j], rsem.at[8 * i + 1 + j],
                                   (ox, oy, c)))
                cps[-1].start()
        wait_all(cps)
        cps = []
        for i in range(n):
            hr, rs = vs[i].shape[0] // 2, vs[i].shape[0] // 8
            acc = sums[i][pl.ds(pl.multiple_of(q * rs, 8), rs), :]
            for j in range(3):
                acc = acc + gots[i][j]
            mine = o_refs[i].at[pl.ds(pl.multiple_of(c * hr + q * rs, 8), rs), :]
            o_refs[i][pl.ds(pl.multiple_of(c * hr + q * rs, 8), rs), :] = acc
            for j, (ox, oy) in enumerate(others):
                cps.append(_remote(mine, mine, ssem.at[8 * i + 4 + j], rsem.at[8 * i + 4 + j], (ox, oy, c)))
                cps[-1].start()
        wait_all(cps)
        cps = []
        for i in range(n):
            hr = vs[i].shape[0] // 2
            half = o_refs[i].at[pl.ds(pl.multiple_of(c * hr, 8), hr), :]
            cps.append(_remote(half, half, ssem.at[8 * i + 7], rsem.at[8 * i + 7], sib))
            cps[-1].start()
        wait_all(cps)

    vm = pl.BlockSpec(memory_space=pltpu.VMEM)
    scratch = [pltpu.VMEM((v.shape[0] // 2, LANES), F32) for v in vs] * 2
    scratch += [pltpu.VMEM((3, v.shape[0] // 8, LANES), F32) for v in vs]
    return pl.pallas_call(
        body, name="all_reduce_small", in_specs=[vm] * n, out_specs=[vm] * n,
        out_shape=[jax.ShapeDtypeStruct(v.shape, F32) for v in vs],
        scratch_shapes=scratch + [pltpu.SemaphoreType.DMA((8 * n,))] * 2,
        compiler_params=pltpu.CompilerParams(vmem_limit_bytes=VMEM_LIMIT, has_side_effects=True),
    )(*vs)


def _to_stacked(name, full):
    R, C = full.shape
    if name in ROW_SHARDED:
        return full.reshape(4, R // 4, C)
    return jnp.transpose(full.reshape(R, 4, C // 4), (1, 0, 2))


def _from_stacked(name, st):
    _, r, c = st.shape
    if name in ROW_SHARDED:
        return st.reshape(4 * r, c)
    return jnp.transpose(st, (1, 0, 2)).reshape(r, 4 * c)


UP_PIECES = ("ffn1_up", "ffn2_up")


def _layer_weights(lands):
    w = {n: lands[n] if n in UP_PIECES else _from_stacked(n, lands[n]) for n in lands}
    if "w_in" not in w:
        return w
    win = w.pop("w_in")
    D = win.shape[0]
    p0, p1, p2, p3 = POOL_DIM, POOL_DIM + Q_LORA_RANK, POOL_DIM + Q_LORA_RANK + KV_LORA_RANK, \
        POOL_DIM + Q_LORA_RANK + KV_LORA_RANK + QK_ROPE_DIM
    w["w_pool"] = win[:, :p0]
    w["w_lat"] = jnp.concatenate([win[:, p0:p3], jnp.zeros((D, LAT_DIM - (p3 - p0)), win.dtype)], axis=1)
    w["w_gate"] = win[:, p3:]
    uq = w["w_uq"].reshape(Q_LORA_RANK, N_HEADS, QK_DIM)
    w["w_uq"] = jnp.concatenate([uq, jnp.zeros((Q_LORA_RANK, N_HEADS, HEAD_PAD - QK_DIM), uq.dtype)],
                                axis=2).reshape(Q_LORA_RANK, N_HEADS * HEAD_PAD)
    return w


def _layer_grads_stacked(dw):
    dw = dict(dw)
    if "w_lat" in dw:
        lat = dw.pop("w_lat")
        dw["w_in"] = jnp.concatenate([dw.pop("w_pool"), lat[:, :Q_LORA_RANK + KV_LORA_RANK + QK_ROPE_DIM],
                                      dw.pop("w_gate")], axis=1)
        dw["w_uq"] = dw["w_uq"].reshape(Q_LORA_RANK, N_HEADS, HEAD_PAD)[:, :, :QK_DIM].reshape(Q_LORA_RANK,
                                                                                                 N_HEADS * QK_DIM)
    return {n: dw[n] if n in UP_PIECES else _to_stacked(n, dw[n]) for n in BIG if n in dw}


def _rope_tables(positions):
    inv_freq = ROPE_THETA ** (-jnp.arange(0, QK_ROPE_DIM, 2, dtype=F32) / QK_ROPE_DIM)
    ang = positions.astype(F32).reshape(-1)[:, None] * inv_freq
    cos, sin = jnp.cos(ang), jnp.sin(ang)
    z = jnp.zeros((ang.shape[0], LANES - QK_ROPE_DIM), F32)
    return jnp.concatenate([cos, cos, z], axis=1), jnp.concatenate([-sin, sin, z], axis=1)


def _pack_small(vals):
    parts, total = [], 0
    for n in PACKED:
        f = vals[n].reshape(-1).astype(F32)
        pad = (-f.shape[0]) % (8 * LANES)
        parts.append(jnp.pad(f, (0, pad)))
        total += f.shape[0] + pad
    parts.append(jnp.zeros(((-total) % (64 * LANES),), F32))
    return jnp.concatenate(parts).reshape(-1, LANES)


def _unpack_small(packed, like):
    flat = packed.reshape(-1)
    out, off = {}, 0
    for n in PACKED:
        size = like[n].size
        out[n] = flat[off:off + size].reshape(like[n].shape)
        off += size + ((-size) % (8 * LANES))
    return out


def _ffn_fwd(x, g, wu4, wd, tag, dep=None):
    h = rms_fwd(x, g, dep=dep, name=f"{tag}_norm")
    gate, up, a = ffn_up_act(h, wu4, name=f"{tag}_up_act")
    y = mm(a, wd, res=x, alpha=0.5, name=f"{tag}_down")
    return y, (x, h, gate, up, a)


def _ffn_bwd(dy, saved, g, wu4, wd, tag, dep=None):
    x, h, gate, up, a = saved
    dgate, dup = ffn_down_dx_act(dy, wd, gate, up, dep=dep, name=f"{tag}_down_dx_act")
    dwd = mm(a, dy, ta=True, alpha=0.5, out_dtype=BF16, name=f"{tag}_down_dw")
    dwu4 = ffn_up_dw(h, dgate, dup, name=f"{tag}_up_dw")
    dx, dg = ffn_up_dx_norm(dgate, dup, wu4, x, g, dy, name=f"{tag}_up_dx_norm")
    return dx, dg, dwu4, dwd


def _mix_fwd(x, p, w, cs, sn, B, dep=None):
    h = rms_fwd(x, p["norm_mix"], dep=dep, name="mix_norm")
    lat, xp, gl = mix_in(h, w["w_lat"], w["w_pool"], w["w_gate"], name="mix_in")
    mixed = pool_fwd(xp, p["pool_maps"].astype(BF16), p["pool_scale"], B, name="pool_fwd")
    ba = mm(mixed, w["w_pool_proj"], out_dtype=BF16, name="mix_pool_proj")
    qn, kvn, kr = lat_fwd(lat, p["q_latent_norm"], p["kv_latent_norm"], cs, sn, name="lat_fwd")
    kv = mm(kvn, w["w_ukv"], out_dtype=BF16, name="mix_ukv")
    o, lse, q = attn_fwd(mm(qn, w["w_uq"], name="mix_uq"), kv, kr, cs, sn, B, name="attn_fwd")
    bb, merged = attn_proj_gate(o, w["w_attn_proj"], gl, p["b_gate"], ba, name="mix_attn_proj_gate")
    y = mm(merged, w["w_out"], res=x, name="mix_out")
    return y, (x, h, lat, xp, gl, mixed, ba, qn, kvn, kr, q, kv, o, lse, bb, merged)


def _mix_bwd(dy, saved, p, w, cs, sn, B, dep=None):
    x, h, lat, xp, gl, mixed, ba, qn, kvn, kr, q, kv, o, lse, bb, merged = saved
    dw, ds = {}, {}
    dw["w_out"] = mm(merged, dy, ta=True, out_dtype=BF16, dep=dep, name="mix_out_dw")
    dba, dbb, dgl, ds["b_gate"] = out_dx_gate(dy, w["w_out"], gl, p["b_gate"], ba, bb, name="mix_out_dx_gate")
    dw["w_attn_proj"] = mm(o, dbb, ta=True, out_dtype=BF16, name="mix_attn_proj_dw")
    do = mm(dbb, w["w_attn_proj"], tb=True, out_dtype=BF16, name="mix_attn_proj_dx")
    dw["w_pool_proj"] = mm(mixed, dba, ta=True, out_dtype=BF16, name="mix_pool_proj_dw")
    dmixed = mm(dba, w["w_pool_proj"], tb=True, name="mix_pool_proj_dx")
    dxp, ds["pool_maps"], ds["pool_scale"] = pool_bwd(xp, dmixed, p["pool_maps"].astype(BF16), p["pool_scale"], B,
                                                      name="pool_bwd")
    dqb, dkv, dkr = attn_bwd(q, kv, kr, o, do, lse, cs, sn, B, name="attn_bwd")
    dw["w_ukv"] = mm(kvn, dkv, ta=True, out_dtype=BF16, name="mix_ukv_dw")
    dkvn = mm(dkv, w["w_ukv"], tb=True, name="mix_ukv_dx")
    dw["w_uq"] = mm(qn, dqb, ta=True, out_dtype=BF16, name="mix_uq_dw")
    dqn = mm(dqb, w["w_uq"], tb=True, name="mix_uq_dx")
    dlat, ds["q_latent_norm"], ds["kv_latent_norm"] = lat_bwd(lat, p["q_latent_norm"], p["kv_latent_norm"], dqn, dkvn,
                                                               dkr, cs, sn, name="lat_bwd")
    dw["w_lat"] = mm(h, dlat, ta=True, out_dtype=BF16, name="mix_lat_dw")
    dw["w_pool"] = mm(h, dxp, ta=True, out_dtype=BF16, name="mix_pool_in_dw")
    dw["w_gate"] = mm(h, dgl, ta=True, out_dtype=BF16, name="mix_gate_in_dw")
    dx, ds["norm_mix"] = mix_in_dx_norm(dlat, dxp, dgl, w["w_lat"], w["w_pool"], w["w_gate"], x, p["norm_mix"], dy,
                                        name="mix_in_dx_norm")
    return dx, dw, ds


def kernel(x, positions, norm_ffn1, ffn1_up, ffn1_down, norm_mix, w_in, b_gate, pool_maps, pool_scale, w_pool_proj, q_latent_norm, w_uq, kv_latent_norm, w_ukv, w_attn_proj, w_out, norm_ffn2, ffn2_up, ffn2_down, final_norm, loss_target, m_norm_ffn1, m_ffn1_up, m_ffn1_down, m_norm_mix, m_w_in, m_b_gate, m_pool_maps, m_pool_scale, m_w_pool_proj, m_q_latent_norm, m_w_uq, m_kv_latent_norm, m_w_ukv, m_w_attn_proj, m_w_out, m_norm_ffn2, m_ffn2_up, m_ffn2_down, m_final_norm, v_norm_ffn1, v_ffn1_up, v_ffn1_down, v_norm_mix, v_w_in, v_b_gate, v_pool_maps, v_pool_scale, v_w_pool_proj, v_q_latent_norm, v_w_uq, v_kv_latent_norm, v_w_ukv, v_w_attn_proj, v_w_out, v_norm_ffn2, v_ffn2_up, v_ffn2_down, v_final_norm):
    given = dict(locals())
    B, S, D = x.shape
    T = B * S
    L = norm_ffn1.shape[0]
    W = {n: given[n] for n in WEIGHTS}
    Mo = {n: given["m_" + n] for n in WEIGHTS}
    Vo = {n: given["v_" + n] for n in WEIGHTS}
    core = lax.axis_index("c").astype(jnp.int32)
    chip = (2 * lax.axis_index("x") + lax.axis_index("y")).astype(jnp.int32)

    core_arr = core.reshape(1)
    chip_arr = chip.reshape(1)
    place = jnp.stack([chip, core])
    first = ("ffn1_up", "ffn1_down")
    rest = tuple(n for n in BIG if n not in first)

    own = [{n: W[n][l].astype(BF16) for n in BIG} for l in range(L)]

    def ag_begin(l, names, tag, after=None):
        lands = [jax.ShapeDtypeStruct((4,) + own[l][n].shape, BF16) for n in names]
        return exchange_begin(f"ag_start_{tag}", [own[l][n] for n in names], lands, ag_plan, 3 * len(names), after)

    def ag_finish(handle, names, tag, after):
        mine, lands = exchange_end(f"ag_wait_{tag}", handle, ag_plan, after)
        lands = ag_forward(lands, name=f"ag_forward_{tag}")
        return _layer_weights(dict(zip(names, place_own(lands, mine, chip_arr, name=f"place_own_{tag}"))))

    h_first, t1 = ag_begin(0, first, "0a")
    cs, sn = _rope_tables(positions)
    xs = x.reshape(T, D) + t1[0, 0]
    saved, ici, handed = [], None, None
    for l in range(L):
        p = {n: W[n][l] for n in SMALL if n != "final_norm"}
        dep = None
        if l == 0:
            w = ag_finish(h_first, first, "0a", xs)
            h_rest, dep = ag_begin(0, rest, "0b", after=w["ffn1_down"])
        else:
            mine, handle = handed
            lands, _ = exchange_end(f"ag_forward_wait_{l}", handle, ag_forward_plan, xs)
            w = _layer_weights(dict(zip(BIG, place_own(lands, mine, chip_arr, name=f"place_own_{l}"))))
        xs, s1 = _ffn_fwd(xs, p["norm_ffn1"], w["ffn1_up"], w["ffn1_down"], "ffn1", dep=dep)
        dep = None
        if l == 0:
            w.update(ag_finish(h_rest, rest, "0b", xs))
            if L > 1:
                ici, dep = ag_begin(1, BIG, "1", after=xs)
        xs, s2 = _mix_fwd(xs, p, w, cs, sn, B, dep=dep)
        dep = None
        if l + 1 < L:
            mine, lands = exchange_end(f"ag_wait_{l + 1}", ici, ag_plan, xs)
            handle, dep = exchange_begin(f"ag_forward_start_{l + 1}", lands, [], ag_forward_plan, 3 * len(BIG))
            handed = (mine, handle)
            if l + 2 < L:
                ici, dep = ag_begin(l + 2, BIG, str(l + 2), after=dep)
        xs, s3 = _ffn_fwd(xs, p["norm_ffn2"], w["ffn2_up"], w["ffn2_down"], "ffn2", dep=dep)
        saved.append((w, p, s1, s2, s3))

    dx, dfinal, loss_tile = loss_head(xs, final_norm, loss_target.reshape(T, D), name="loss_head")
    loss = lax.psum(loss_tile[0, 0], ("x", "y", "c"))

    def rs_begin(dw, tag, after=None):
        stacked = _layer_grads_stacked(dw)
        names = tuple(stacked)
        parts = [stacked[n] for n in names]
        r1 = rs_swap_halves(parts, after=after, name=f"rs_swap_{tag}")
        sums = [rs_chip_sum(g, a, core_arr, name=f"rs_chip_sum_{n}") for n, g, a in zip(names, parts, r1)]
        lands = [jax.ShapeDtypeStruct((3,) + s.shape[1:], BF16) for s in sums]
        handle, token = exchange_begin(f"rs_start_{tag}", sums, lands, rs_plan, 3 * len(names))
        return (names, parts, r1, handle), token

    acc = {n: lax.empty(W[n].shape, F32) for n in BIG}

    def rs_finish(l, pending, tag, after):
        names, parts, r1, handle = pending
        _, r2 = exchange_end(f"rs_wait_{tag}", handle, rs_plan, after)
        for n, g, a, b in zip(names, parts, r1, r2):
            acc[n] = rs_final_sum(g, a, b, place, acc[n], l, name=f"rs_final_sum_{n}")

    small_layers, pending, swapping, dep = [], [], None, None
    for l in reversed(range(L)):
        w, p, s1, s2, s3 = saved[l]
        dx, dg2, dwu2, dwd2 = _ffn_bwd(dx, s3, p["norm_ffn2"], w["ffn2_up"], w["ffn2_down"], "ffn2", dep=dep)
        dep = None
        if swapping is not None:
            above, names, handle = swapping
            parts, r1 = exchange_end(f"rs_swap_wait_{above}", handle, rs_swap_plan, dx)
            sums = [rs_chip_sum(g, a, core_arr, name=f"rs_chip_sum_{n}") for n, g, a in zip(names, parts, r1)]
            lands = [jax.ShapeDtypeStruct((3,) + s.shape[1:], BF16) for s in sums]
            handle, dep = exchange_begin(f"rs_start_{above}", sums, lands, rs_plan, 3 * len(names))
            pending.append((above, (names, parts, r1, handle)))
            swapping = None
        dx, dw, ds = _mix_bwd(dx, s2, p, w, cs, sn, B, dep=dep)
        dw.update(ffn2_up=dwu2, ffn2_down=dwd2)
        dep = None
        if l == 0:
            early, dep = rs_begin(dw, "0b")
            dw = {}
        dx, dg1, dwu1, dwd1 = _ffn_bwd(dx, s1, p["norm_ffn1"], w["ffn1_up"], w["ffn1_down"], "ffn1", dep=dep)
        dw.update(ffn1_up=dwu1, ffn1_down=dwd1)
        ds.update(norm_ffn1=dg1, norm_ffn2=dg2)
        small_layers.append(ds)
        if l == 0:
            last_dw = dw
        else:
            stacked = _layer_grads_stacked(dw)
            halves = [jax.ShapeDtypeStruct((4, g.shape[1] // 2, g.shape[2]), BF16) for g in stacked.values()]
            handle, dep = exchange_begin(f"rs_swap_start_{l}", list(stacked.values()), halves, rs_swap_plan,
                                         len(stacked))
            swapping = (l, tuple(stacked), handle)
    small_layers.reverse()

    small = {n: jnp.stack([small_layers[l][n].reshape(W[n].shape[1:]) for l in range(L)]) for n in SMALL
             if n != "final_norm"}
    small["final_norm"] = dfinal.reshape(final_norm.shape)
    rows = lambda a: a.reshape(-1, LANES)
    reduced, reduced_maps = all_reduce_small([_pack_small(small), rows(small["pool_maps"])])
    grads = _unpack_small(reduced, small)
    grads["pool_maps"] = reduced_maps.reshape(pool_maps.shape)
    last, dep = rs_begin(last_dw, "0a", after=reduced)
    delta, new_m, new_v = {}, {}, {}
    d, nm, nv = adamw(_pack_small(W), reduced, _pack_small(Mo), _pack_small(Vo), dep=dep, name="adamw_small")
    delta.update(_unpack_small(d, W))
    new_m.update(_unpack_small(nm, W))
    new_v.update(_unpack_small(nv, W))
    d, nm, nv = adamw(rows(pool_maps), reduced_maps, rows(m_pool_maps), rows(v_pool_maps), dep=d,
                      name="adamw_pool_maps")
    delta["pool_maps"], new_m["pool_maps"], new_v["pool_maps"] = (a.reshape(pool_maps.shape) for a in (d, nm, nv))

    def update(names, tag, d):
        joined = rs_join_halves([acc[n] for n in names], name=f"rs_join_{tag}")
        for n, g in zip(names, joined):
            flip = (lambda a: jnp.swapaxes(a, 1, 2)) if n == "w_in" else (lambda a: a)
            sh = flip(W[n]).shape
            two = lambda a: flip(a).reshape(sh[0] * sh[1], sh[2])
            back = lambda a: flip(a.reshape(sh))
            gc, d, nm, nv = adamw(two(W[n]), two(g), two(Mo[n]), two(Vo[n]), dep=d, copy_g=True, name=f"adamw_{n}")
            grads[n], delta[n], new_m[n], new_v[n] = back(gc), back(d), back(nm), back(nv)
        return d

    for l, item in pending:
        rs_finish(l, item, str(l), d)
    rs_finish(0, early, "0b", d)
    d = update(rest, "rest", d)
    rs_finish(0, last, "0a", d)
    update(first, "first", d)

    return (loss, dx.reshape(B, S, D), *[grads[n] for n in WEIGHTS], *[delta[n] for n in WEIGHTS],
            *[new_m[n] for n in WEIGHTS], *[new_v[n] for n in WEIGHTS])
```

```python
import functools

import jax
import jax.numpy as jnp
from jax import lax
from jax.experimental import pallas as pl
from jax.experimental.pallas import tpu as pltpu

F32 = jnp.float32
BF16 = jnp.bfloat16

N_HEADS = 8
QK_NOPE_DIM = 128
QK_ROPE_DIM = 64
QK_DIM = QK_NOPE_DIM + QK_ROPE_DIM
V_HEAD_DIM = 128
HEAD_PAD = 256
Q_LORA_RANK = 384
KV_LORA_RANK = 256
ROPE_THETA = 10000.0
POOL_WINDOWS = (2, 4, 8, 16)
N_POOL_GROUPS = 4
POOL_GROUP_DIM = 128
POOL_DIM = N_POOL_GROUPS * POOL_GROUP_DIM
LAT_DIM = 768
NORM_EPS = 1e-6
ADAM_LR = 0.001
ADAM_B1 = 0.9
ADAM_B2 = 0.999
ADAM_EPS = 1e-08
ADAM_WD = 0.01
ADAM_STEP = 10
NEG_INF = -1e30
LANES = 128
ATT_BLOCK = 512
VMEM_LIMIT = 48 * 1024 * 1024
MESH = pl.DeviceIdType.MESH
_NT = (((1,), (1,)), ((), ()))
_TN = (((0,), (0,)), ((), ()))

BIG = ("ffn1_up", "ffn1_down", "w_in", "w_pool_proj", "w_uq", "w_ukv", "w_attn_proj", "w_out",
       "ffn2_up", "ffn2_down")
ROW_SHARDED = ("ffn1_down", "w_attn_proj", "w_out", "ffn2_down")
SMALL = ("norm_ffn1", "norm_mix", "b_gate", "pool_maps", "pool_scale", "q_latent_norm",
         "kv_latent_norm", "norm_ffn2", "final_norm")
PACKED = tuple(n for n in SMALL if n != "pool_maps")
WEIGHTS = ("norm_ffn1", "ffn1_up", "ffn1_down", "norm_mix", "w_in", "b_gate", "pool_maps", "pool_scale",
           "w_pool_proj", "q_latent_norm", "w_uq", "kv_latent_norm", "w_ukv", "w_attn_proj", "w_out",
           "norm_ffn2", "ffn2_up", "ffn2_down", "final_norm")


def _pick(dim, cands):
    for c in cands:
        if c <= dim and dim % c == 0:
            return c
    return dim


def _cparams(sem=None, **kw):
    if sem is not None:
        kw["dimension_semantics"] = sem
    return pltpu.CompilerParams(vmem_limit_bytes=VMEM_LIMIT, **kw)


def _sigmoid(x):
    return 0.5 * jnp.tanh(0.5 * x) + 0.5


MM_TILE_BUDGET = 30 * 1024 * 1024
TILE_SIZES = (1408, 1024, 768, 512, 384, 256, 128)


V7X_MXU_FLOPS = 9.0e14
V7X_HBM_BYTES = 2.5e12
GRID_STEP_S = 0.35e-6


def _mm_tiles(M, N, K, sa, sb, so, sr):
    tks = [K] if K <= 2816 else [t for t in (2816, 2048, 1408, 1024, 512, 256, 128) if K % t == 0]
    best = None
    for tk in tks:
        for tm in [t for t in TILE_SIZES if M % t == 0] or [M]:
            for tn in [t for t in TILE_SIZES if N % t == 0] or [N]:
                need = 2 * (tm * tk * sa + tk * tn * sb + tm * tn * (so + sr)) + (tm * tn * 4 if tk < K else 0)
                if need > MM_TILE_BUDGET:
                    continue
                ni, nj, nk = M // tm, N // tn, K // tk
                a_bytes = M * K * sa * (nj if nk > 1 else 1)
                b_bytes = K * N * sb * (1 if nj == 1 and nk == 1 else ni)
                traffic = a_bytes + b_bytes + M * N * (so + sr) + (M * N * 8 * nk if nk > 1 else 0)
                t = max(2.0 * M * N * K / V7X_MXU_FLOPS, traffic / V7X_HBM_BYTES) + ni * nj * nk * GRID_STEP_S
                if best is None or t < best[0]:
                    best = (t, (tm, tn, tk))
    assert best is not None, (M, N, K)
    return best[1]


def mm(a, b, *, name, ta=False, tb=False, out_dtype=F32, res=None, alpha=1.0, dep=None):
    if ta:
        K, M = a.shape
    else:
        M, K = a.shape
    if tb:
        N, K2 = b.shape
    else:
        K2, N = b.shape
    assert K == K2, (a.shape, b.shape, ta, tb)
    tm, tn, tk = _mm_tiles(M, N, K, a.dtype.itemsize, b.dtype.itemsize, jnp.dtype(out_dtype).itemsize,
                           0 if res is None else res.dtype.itemsize)
    nk = K // tk
    dims = (((0 if ta else 1,), (1 if tb else 0,)), ((), ()))

    def body(*refs):
        a_ref, b_ref = refs[:2]
        r_ref = refs[2] if res is not None else None
        o_ref = refs[-2] if nk > 1 else refs[-1]

        def finish(r):
            if alpha != 1.0:
                r = r * alpha
            if res is not None:
                r = r_ref[...].astype(F32) + r
            o_ref[...] = r.astype(out_dtype)

        part = lax.dot_general(a_ref[...].astype(BF16), b_ref[...].astype(BF16), dims, preferred_element_type=F32)
        if nk == 1:
            finish(part)
            return
        acc = refs[-1]
        k = pl.program_id(2)

        @pl.when(k == 0)
        def _():
            acc[...] = part

        @pl.when(k > 0)
        def _():
            acc[...] += part

        @pl.when(k == nk - 1)
        def _():
            finish(acc[...])

    a_spec = pl.BlockSpec((tk, tm), lambda i, j, k: (k, i)) if ta else pl.BlockSpec((tm, tk), lambda i, j, k: (i, k))
    b_spec = pl.BlockSpec((tn, tk), lambda i, j, k: (j, k)) if tb else pl.BlockSpec((tk, tn), lambda i, j, k: (k, j))
    o_spec = pl.BlockSpec((tm, tn), lambda i, j, k: (i, j))
    in_specs = [a_spec, b_spec]
    args = [a, b]
    if res is not None:
        in_specs.append(o_spec)
        args.append(res)
    if dep is not None:
        in_specs.append(pl.BlockSpec((8, LANES), lambda i, j, k: (0, 0)))
        args.append(dep)
    return pl.pallas_call(
        body, name=name, grid=(M // tm, N // tn, nk), in_specs=in_specs, out_specs=o_spec,
        out_shape=jax.ShapeDtypeStruct((M, N), out_dtype),
        scratch_shapes=[pltpu.VMEM((tm, tn), F32)] if nk > 1 else [],
        compiler_params=_cparams(("parallel", "parallel", "arbitrary")),
    )(*args)


MXU_COLS = 256


def _col_chunks(n):
    return [(lo, min(lo + MXU_COLS, n)) for lo in range(0, n, MXU_COLS)]


def ffn_up_act(h, wu4, *, name):
    T, D = h.shape
    cq = wu4.shape[2]
    Fh = 2 * cq
    tm = _pick(T, (512, 256, 128))

    def body(h_ref, wg_ref, wu_ref, g_ref, u_ref, a_ref):
        hv = h_ref[...]
        for lo, hi in _col_chunks(cq):
            gv = jnp.dot(hv, wg_ref[0, :, lo:hi], preferred_element_type=F32)
            uv = jnp.dot(hv, wu_ref[0, :, lo:hi], preferred_element_type=F32)
            g_ref[:, lo:hi] = gv.astype(BF16)
            u_ref[:, lo:hi] = uv.astype(BF16)
            a_ref[:, lo:hi] = (gv * _sigmoid(gv) * uv).astype(BF16)

    tile = pl.BlockSpec((tm, cq), lambda j, i: (i, j))
    sh = jax.ShapeDtypeStruct((T, Fh), BF16)
    return pl.pallas_call(
        body, name=name, grid=(2, T // tm),
        in_specs=[pl.BlockSpec((tm, D), lambda j, i: (i, 0)), pl.BlockSpec((1, D, cq), lambda j, i: (j, 0, 0)),
                  pl.BlockSpec((1, D, cq), lambda j, i: (2 + j, 0, 0))],
        out_specs=[tile, tile, tile], out_shape=[sh, sh, sh],
        compiler_params=_cparams(("parallel", "parallel")),
    )(h, wu4, wu4)


def ffn_down_dx_act(dy, wd, g, u, *, dep=None, name):
    T, D = dy.shape
    Fh = wd.shape[0]
    cq = Fh // 2
    tm = _pick(T, (512, 256, 128))

    def body(dy_ref, wd_ref, g_ref, u_ref, *rest):
        dg_ref, du_ref = rest[-2:]
        dyv = dy_ref[...].astype(BF16)
        for lo, hi in _col_chunks(cq):
            da = 0.5 * lax.dot_general(dyv, wd_ref[lo:hi, :], _NT, preferred_element_type=F32)
            gv = g_ref[:, lo:hi].astype(F32)
            uv = u_ref[:, lo:hi].astype(F32)
            s = _sigmoid(gv)
            dg_ref[:, lo:hi] = (da * uv * (s * (1.0 + gv * (1.0 - s)))).astype(BF16)
            du_ref[:, lo:hi] = (da * (gv * s)).astype(BF16)

    tile = pl.BlockSpec((tm, cq), lambda j, i: (i, j))
    sh = jax.ShapeDtypeStruct((T, Fh), BF16)
    in_specs = [pl.BlockSpec((tm, D), lambda j, i: (i, 0)), pl.BlockSpec((cq, D), lambda j, i: (j, 0)), tile, tile]
    args = [dy, wd, g, u]
    if dep is not None:
        in_specs.append(pl.BlockSpec((8, LANES), lambda j, i: (0, 0)))
        args.append(dep)
    return pl.pallas_call(
        body, name=name, grid=(2, T // tm), in_specs=in_specs, out_specs=[tile, tile], out_shape=[sh, sh],
        compiler_params=_cparams(("parallel", "parallel")),
    )(*args)


def ffn_up_dw(h, dg, du, *, name):
    T, D = h.shape
    cq = dg.shape[1] // 2
    tk = _pick(T, (1024, 512, 256, 128))
    nk = T // tk

    def body(h_ref, dg_ref, du_ref, o_ref, acc):
        p = pl.program_id(0)
        k = pl.program_id(1)

        @pl.when(k == 0)
        def _():
            acc[...] = jnp.zeros_like(acc)

        @pl.when(p < 2)
        def _():
            acc[...] += lax.dot_general(h_ref[...], dg_ref[...], _TN, preferred_element_type=F32)

        @pl.when(p >= 2)
        def _():
            acc[...] += lax.dot_general(h_ref[...], du_ref[...], _TN, preferred_element_type=F32)

        @pl.when(k == nk - 1)
        def _():
            o_ref[0] = acc[...].astype(BF16)

    return pl.pallas_call(
        body, name=name, grid=(4, nk),
        in_specs=[pl.BlockSpec((tk, D), lambda p, k: (k, 0)),
                  pl.BlockSpec((tk, cq), lambda p, k: (jnp.where(p < 2, k, 0), jnp.minimum(p, 1))),
                  pl.BlockSpec((tk, cq), lambda p, k: (jnp.where(p < 2, 0, k), jnp.maximum(p - 2, 0)))],
        out_specs=pl.BlockSpec((1, D, cq), lambda p, k: (p, 0, 0)),
        out_shape=jax.ShapeDtypeStruct((4, D, cq), BF16),
        scratch_shapes=[pltpu.VMEM((D, cq), F32)],
        compiler_params=_cparams(("parallel", "arbitrary")),
    )(h, dg, du)


def ffn_up_dx_norm(dg, du, wu4, x, g, dy, *, name):
    T = dg.shape[0]
    _, D, cq = wu4.shape
    tm = _pick(T, (512, 256, 128))

    def body(dg_ref, du_ref, wg_ref, wu_ref, x_ref, g_ref, dy_ref, dx_ref, dgain_ref, acc):
        i = pl.program_id(0)
        k = pl.program_id(1)
        part = lax.dot_general(dg_ref[...], wg_ref[0], _NT, preferred_element_type=F32)
        part = part + lax.dot_general(du_ref[...], wu_ref[0], _NT, preferred_element_type=F32)

        @pl.when(jnp.logical_and(i == 0, k == 0))
        def _():
            dgain_ref[...] = jnp.zeros_like(dgain_ref)

        @pl.when(k == 0)
        def _():
            acc[...] = part

        @pl.when(k == 1)
        def _():
            dx, dgain = _rms_bwd_math(x_ref[...], g_ref[...], acc[...] + part)
            dx_ref[...] = dy_ref[...] + dx
            dgain_ref[...] += dgain

    tile = pl.BlockSpec((tm, cq), lambda i, k: (i, k))
    row = pl.BlockSpec((tm, D), lambda i, k: (i, 0))
    vec = pl.BlockSpec((1, D), lambda i, k: (0, 0))
    return pl.pallas_call(
        body, name=name, grid=(T // tm, 2),
        in_specs=[tile, tile, pl.BlockSpec((1, D, cq), lambda i, k: (k, 0, 0)),
                  pl.BlockSpec((1, D, cq), lambda i, k: (2 + k, 0, 0)), row, vec, row],
        out_specs=[row, vec],
        out_shape=[jax.ShapeDtypeStruct((T, D), F32), jax.ShapeDtypeStruct((1, D), F32)],
        scratch_shapes=[pltpu.VMEM((tm, D), F32)],
        compiler_params=_cparams(("arbitrary", "arbitrary")),
    )(dg, du, wu4, wu4, x, g.reshape(1, D), dy)


def mix_in(h, w_lat, w_pool, w_gate, *, name):
    T, D = h.shape
    tm = _pick(T, (512, 256, 128))

    def body(h_ref, wl_ref, wp_ref, wg_ref, lat_ref, xp_ref, gl_ref):
        hv = h_ref[...]
        lat_ref[...] = jnp.dot(hv, wl_ref[...], preferred_element_type=F32)
        xp_ref[...] = jnp.dot(hv, wp_ref[...], preferred_element_type=F32)
        gl_ref[...] = jnp.dot(hv, wg_ref[...], preferred_element_type=F32).astype(BF16)

    whole = lambda a: pl.BlockSpec(a.shape, lambda i: (0, 0))
    out = lambda a: pl.BlockSpec((tm, a.shape[1]), lambda i: (i, 0))
    return pl.pallas_call(
        body, name=name, grid=(T // tm,),
        in_specs=[pl.BlockSpec((tm, D), lambda i: (i, 0)), whole(w_lat), whole(w_pool), whole(w_gate)],
        out_specs=[out(w_lat), out(w_pool), out(w_gate)],
        out_shape=[jax.ShapeDtypeStruct((T, w_lat.shape[1]), F32), jax.ShapeDtypeStruct((T, w_pool.shape[1]), F32),
                   jax.ShapeDtypeStruct((T, w_gate.shape[1]), BF16)],
        compiler_params=_cparams(("parallel",)),
    )(h, w_lat, w_pool, w_gate)


def mix_in_dx_norm(dlat, dxp, dgl, w_lat, w_pool, w_gate, x, g, dy, *, name):
    T, D = x.shape
    tm = _pick(T, (512, 256, 128))

    def body(dlat_ref, dxp_ref, dgl_ref, wl_ref, wp_ref, wg_ref, x_ref, g_ref, dy_ref, dx_ref, dgain_ref):
        @pl.when(pl.program_id(0) == 0)
        def _():
            dgain_ref[...] = jnp.zeros_like(dgain_ref)

        dh = lax.dot_general(dlat_ref[...], wl_ref[...], _NT, preferred_element_type=F32)
        dh = dh + lax.dot_general(dxp_ref[...], wp_ref[...], _NT, preferred_element_type=F32)
        dh = dh + lax.dot_general(dgl_ref[...], wg_ref[...], _NT, preferred_element_type=F32)
        dx, dgain = _rms_bwd_math(x_ref[...], g_ref[...], dh)
        dx_ref[...] = dy_ref[...] + dx
        dgain_ref[...] += dgain

    row = lambda a: pl.BlockSpec((tm, a.shape[1]), lambda i: (i, 0))
    whole = lambda a: pl.BlockSpec(a.shape, lambda i: (0, 0))
    vec = pl.BlockSpec((1, D), lambda i: (0, 0))
    return pl.pallas_call(
        body, name=name, grid=(T // tm,),
        in_specs=[row(dlat), row(dxp), row(dgl), whole(w_lat), whole(w_pool), whole(w_gate), row(x), vec, row(dy)],
        out_specs=[row(x), vec],
        out_shape=[jax.ShapeDtypeStruct((T, D), F32), jax.ShapeDtypeStruct((1, D), F32)],
        compiler_params=_cparams(("arbitrary",)),
    )(dlat, dxp, dgl, w_lat, w_pool, w_gate, x, g.reshape(1, D), dy)


def _rows(T, width_bytes):
    cap = max(8, (2 * 1024 * 1024) // width_bytes)
    return _pick(T, tuple(c for c in (1024, 512, 256, 128, 64, 32, 16) if c <= cap))


def rms_fwd(x, g, *, name, dep=None):
    T, D = x.shape
    tm = _rows(T, D * 4)

    def body(x_ref, g_ref, *rest):
        xv = x_ref[...]
        r = lax.rsqrt(jnp.mean(xv * xv, axis=-1, keepdims=True) + NORM_EPS)
        rest[-1][...] = (xv * r * g_ref[...]).astype(BF16)

    in_specs = [pl.BlockSpec((tm, D), lambda i: (i, 0)), pl.BlockSpec((1, D), lambda i: (0, 0))]
    args = [x, g.reshape(1, D)]
    if dep is not None:
        in_specs.append(pl.BlockSpec((8, LANES), lambda i: (0, 0)))
        args.append(dep)
    return pl.pallas_call(
        body, name=name, grid=(T // tm,), in_specs=in_specs,
        out_specs=pl.BlockSpec((tm, D), lambda i: (i, 0)),
        out_shape=jax.ShapeDtypeStruct((T, D), BF16),
        compiler_params=_cparams(("parallel",)),
    )(*args)


def _rms_bwd_math(xv, gv, dh):
    r = lax.rsqrt(jnp.mean(xv * xv, axis=-1, keepdims=True) + NORM_EPS)
    xn = xv * r
    dg = jnp.sum(dh * xn, axis=0, keepdims=True)
    dxn = dh * gv
    dx = r * (dxn - xn * jnp.mean(dxn * xn, axis=-1, keepdims=True))
    return dx, dg


def _rope(xv, cv, sv):
    half = QK_ROPE_DIM // 2
    lane = lax.broadcasted_iota(jnp.int32, xv.shape, 1)
    swapped = jnp.where(lane < half, pltpu.roll(xv, LANES - half, 1), pltpu.roll(xv, half, 1))
    return xv * cv + swapped * sv


def _rope_t(dy, cv, sv):
    half = QK_ROPE_DIM // 2
    ds = dy * sv
    lane = lax.broadcasted_iota(jnp.int32, dy.shape, 1)
    swapped = jnp.where(lane < half, pltpu.roll(ds, LANES - half, 1), pltpu.roll(ds, half, 1))
    return dy * cv + swapped


def lat_fwd(lat, qn_w, kvn_w, cs, sn, *, name):
    T = lat.shape[0]
    tm = _rows(T, LAT_DIM * 4)
    kv0 = Q_LORA_RANK
    kr0 = Q_LORA_RANK + KV_LORA_RANK

    def body(lat_ref, qw_ref, kw_ref, c_ref, s_ref, qn_ref, kvn_ref, kr_ref):
        ql = lat_ref[:, :kv0]
        r = lax.rsqrt(jnp.mean(ql * ql, axis=-1, keepdims=True) + NORM_EPS)
        qn_ref[...] = (ql * r * qw_ref[...]).astype(BF16)
        kl = lat_ref[:, kv0:kr0]
        r = lax.rsqrt(jnp.mean(kl * kl, axis=-1, keepdims=True) + NORM_EPS)
        kvn_ref[...] = (kl * r * kw_ref[...]).astype(BF16)
        kr_ref[...] = _rope(lat_ref[:, kr0:], c_ref[...], s_ref[...]).astype(BF16)

    return pl.pallas_call(
        body, name=name, grid=(T // tm,),
        in_specs=[pl.BlockSpec((tm, LAT_DIM), lambda i: (i, 0)),
                  pl.BlockSpec((1, Q_LORA_RANK), lambda i: (0, 0)),
                  pl.BlockSpec((1, KV_LORA_RANK), lambda i: (0, 0)),
                  pl.BlockSpec((tm, LANES), lambda i: (i, 0)), pl.BlockSpec((tm, LANES), lambda i: (i, 0))],
        out_specs=[pl.BlockSpec((tm, Q_LORA_RANK), lambda i: (i, 0)),
                   pl.BlockSpec((tm, KV_LORA_RANK), lambda i: (i, 0)),
                   pl.BlockSpec((tm, LANES), lambda i: (i, 0))],
        out_shape=[jax.ShapeDtypeStruct((T, Q_LORA_RANK), BF16), jax.ShapeDtypeStruct((T, KV_LORA_RANK), BF16),
                   jax.ShapeDtypeStruct((T, LANES), BF16)],
        compiler_params=_cparams(("parallel",)),
    )(lat, qn_w.reshape(1, -1), kvn_w.reshape(1, -1), cs, sn)


def lat_bwd(lat, qn_w, kvn_w, dqn, dkvn, dkr, cs, sn, *, name):
    T = lat.shape[0]
    tm = _rows(T, LAT_DIM * 4)
    kv0 = Q_LORA_RANK
    kr0 = Q_LORA_RANK + KV_LORA_RANK

    def body(lat_ref, qw_ref, kw_ref, dqn_ref, dkvn_ref, dkr_ref, c_ref, s_ref, dlat_ref, dqw_ref, dkw_ref):
        @pl.when(pl.program_id(0) == 0)
        def _():
            dqw_ref[...] = jnp.zeros_like(dqw_ref)
            dkw_ref[...] = jnp.zeros_like(dkw_ref)

        dx, dg = _rms_bwd_math(lat_ref[:, :kv0], qw_ref[...], dqn_ref[...])
        dlat_ref[:, :kv0] = dx.astype(BF16)
        dqw_ref[...] += dg
        dx, dg = _rms_bwd_math(lat_ref[:, kv0:kr0], kw_ref[...], dkvn_ref[...])
        dlat_ref[:, kv0:kr0] = dx.astype(BF16)
        dkw_ref[...] += dg
        dlat_ref[:, kr0:] = _rope_t(dkr_ref[...], c_ref[...], s_ref[...]).astype(BF16)

    row = lambda w: pl.BlockSpec((tm, w), lambda i: (i, 0))
    vec = lambda w: pl.BlockSpec((1, w), lambda i: (0, 0))
    return pl.pallas_call(
        body, name=name, grid=(T // tm,),
        in_specs=[row(LAT_DIM), vec(Q_LORA_RANK), vec(KV_LORA_RANK), row(Q_LORA_RANK), row(KV_LORA_RANK),
                  row(LANES), row(LANES), row(LANES)],
        out_specs=[row(LAT_DIM), vec(Q_LORA_RANK), vec(KV_LORA_RANK)],
        out_shape=[jax.ShapeDtypeStruct((T, LAT_DIM), BF16), jax.ShapeDtypeStruct((1, Q_LORA_RANK), F32),
                   jax.ShapeDtypeStruct((1, KV_LORA_RANK), F32)],
        compiler_params=_cparams(("arbitrary",)),
    )(lat, qn_w.reshape(1, -1), kvn_w.reshape(1, -1), dqn, dkvn, dkr, cs, sn)


def attn_proj_gate(o, wap, gl, bg, ba, *, name):
    T, D2 = gl.shape
    D = D2 // 2
    tm = _pick(T, (512, 256, 128))

    def body(o_ref, w_ref, gl_ref, bg_ref, ba_ref, bb_ref, m_ref):
        bb = jnp.dot(o_ref[...], w_ref[...], preferred_element_type=F32)
        bb_ref[...] = bb.astype(BF16)
        ga = _sigmoid(gl_ref[:, :D].astype(F32) + bg_ref[:, :D])
        gb = _sigmoid(gl_ref[:, D:].astype(F32) + bg_ref[:, D:])
        m_ref[...] = (ga * ba_ref[...].astype(F32) + gb * bb).astype(BF16)

    row = lambda w: pl.BlockSpec((tm, w), lambda i: (i, 0))
    return pl.pallas_call(
        body, name=name, grid=(T // tm,),
        in_specs=[row(o.shape[1]), pl.BlockSpec(wap.shape, lambda i: (0, 0)), row(D2),
                  pl.BlockSpec((1, D2), lambda i: (0, 0)), row(D)],
        out_specs=[row(D), row(D)],
        out_shape=[jax.ShapeDtypeStruct((T, D), BF16), jax.ShapeDtypeStruct((T, D), BF16)],
        compiler_params=_cparams(("parallel",)),
    )(o, wap, gl, bg.reshape(1, D2), ba)


def out_dx_gate(dy, wo, gl, bg, ba, bb, *, name, dep=None):
    T, D2 = gl.shape
    D = D2 // 2
    tm = _pick(T, (512, 256, 128))

    def body(dy_ref, w_ref, gl_ref, bg_ref, ba_ref, bb_ref, *rest):
        dba_ref, dbb_ref, dgl_ref, dbg_ref = rest[-4:]

        @pl.when(pl.program_id(0) == 0)
        def _():
            dbg_ref[...] = jnp.zeros_like(dbg_ref)

        dmv = lax.dot_general(dy_ref[...].astype(BF16), w_ref[...], _NT, preferred_element_type=F32)
        ga = _sigmoid(gl_ref[:, :D].astype(F32) + bg_ref[:, :D])
        gb = _sigmoid(gl_ref[:, D:].astype(F32) + bg_ref[:, D:])
        dba_ref[...] = (dmv * ga).astype(BF16)
        dbb_ref[...] = (dmv * gb).astype(BF16)
        dla = dmv * ba_ref[...].astype(F32) * ga * (1.0 - ga)
        dlb = dmv * bb_ref[...].astype(F32) * gb * (1.0 - gb)
        dgl_ref[:, :D] = dla.astype(BF16)
        dgl_ref[:, D:] = dlb.astype(BF16)
        dbg_ref[:, :D] += jnp.sum(dla, axis=0, keepdims=True)
        dbg_ref[:, D:] += jnp.sum(dlb, axis=0, keepdims=True)

    row = lambda w: pl.BlockSpec((tm, w), lambda i: (i, 0))
    vec = pl.BlockSpec((1, D2), lambda i: (0, 0))
    return pl.pallas_call(
        body, name=name, grid=(T // tm,),
        in_specs=[row(D), pl.BlockSpec(wo.shape, lambda i: (0, 0)), row(D2), vec, row(D), row(D)]
        + ([] if dep is None else [pl.BlockSpec((8, LANES), lambda i: (0, 0))]),
        out_specs=[row(D), row(D), row(D2), vec],
        out_shape=[jax.ShapeDtypeStruct((T, D), BF16), jax.ShapeDtypeStruct((T, D), BF16),
                   jax.ShapeDtypeStruct((T, D2), BF16), jax.ShapeDtypeStruct((1, D2), F32)],
        compiler_params=_cparams(("arbitrary",)),
    )(dy, wo, gl, bg.reshape(1, D2), ba, bb, *([] if dep is None else [dep]))


def loss_head(x, gf, tgt, *, name):
    T, D = x.shape
    tm = _rows(T, D * 4)

    def body(x_ref, g_ref, t_ref, dx_ref, dg_ref, loss_ref):
        @pl.when(pl.program_id(0) == 0)
        def _():
            dg_ref[...] = jnp.zeros_like(dg_ref)
            loss_ref[...] = jnp.zeros_like(loss_ref)

        xv = x_ref[...]
        gv = g_ref[...]
        r = lax.rsqrt(jnp.mean(xv * xv, axis=-1, keepdims=True) + NORM_EPS)
        xn = xv * r
        err = xn * gv - t_ref[...]
        loss_ref[...] += 0.5 * jnp.sum(jnp.mean(err * err, axis=-1, keepdims=True))
        dy = err * (1.0 / D)
        dg_ref[...] += jnp.sum(dy * xn, axis=0, keepdims=True)
        dxn = dy * gv
        dx_ref[...] = r * (dxn - xn * jnp.mean(dxn * xn, axis=-1, keepdims=True))

    row = pl.BlockSpec((tm, D), lambda i: (i, 0))
    vec = pl.BlockSpec((1, D), lambda i: (0, 0))
    return pl.pallas_call(
        body, name=name, grid=(T // tm,), in_specs=[row, vec, row],
        out_specs=[row, vec, pl.BlockSpec((8, LANES), lambda i: (0, 0))],
        out_shape=[jax.ShapeDtypeStruct((T, D), F32), jax.ShapeDtypeStruct((1, D), F32),
                   jax.ShapeDtypeStruct((8, LANES), F32)],
        compiler_params=_cparams(("arbitrary",)),
    )(x, gf.reshape(1, D), tgt)


def _shift_rows(s, k, down):
    n = s.shape[0]
    t = lax.broadcasted_iota(jnp.int32, s.shape, 0)
    if down:
        return jnp.where(t >= k, pltpu.roll(s, k, 0), 0.0)
    return jnp.where(t < n - k, pltpu.roll(s, n - k, 0), 0.0)


def _window_sum(s, w, down):
    k = 1
    while k < w:
        s = s + _shift_rows(s, k, down)
        k *= 2
    return s


def _pool_count(shape, w):
    t = lax.broadcasted_iota(jnp.int32, shape, 0)
    return jnp.minimum(t + 1, w).astype(F32)


def pool_fwd(xp, maps, scale, B, *, name):
    T, P = xp.shape
    S = T // B
    G = POOL_GROUP_DIM

    def body(x_ref, m_ref, sc_ref, o_ref):
        for g, w in enumerate(POOL_WINDOWS):
            xg = x_ref[:, g * G:(g + 1) * G]
            pooled = _window_sum(xg, w, True) / _pool_count(xg.shape, w) - xg
            mixed = jnp.dot(pooled.astype(BF16), m_ref[g], preferred_element_type=F32)
            o_ref[:, g * G:(g + 1) * G] = (mixed * sc_ref[:, g * G:(g + 1) * G]).astype(BF16)

    return pl.pallas_call(
        body, name=name, grid=(B,),
        in_specs=[pl.BlockSpec((S, P), lambda b: (b, 0)), pl.BlockSpec((N_POOL_GROUPS, G, G), lambda b: (0, 0, 0)),
                  pl.BlockSpec((1, P), lambda b: (0, 0))],
        out_specs=pl.BlockSpec((S, P), lambda b: (b, 0)),
        out_shape=jax.ShapeDtypeStruct((T, P), BF16),
        compiler_params=_cparams(("parallel",)),
    )(xp, maps, scale.reshape(1, P))


def pool_bwd(xp, dmixed, maps, scale, B, *, name):
    T, P = xp.shape
    S = T // B
    G = POOL_GROUP_DIM

    def body(x_ref, dm_ref, m_ref, sc_ref, dx_ref, dmaps_ref, dsc_ref):
        @pl.when(pl.program_id(0) == 0)
        def _():
            dmaps_ref[...] = jnp.zeros_like(dmaps_ref)
            dsc_ref[...] = jnp.zeros_like(dsc_ref)

        for g, w in enumerate(POOL_WINDOWS):
            cols = slice(g * G, (g + 1) * G)
            xg = x_ref[:, cols]
            cnt = _pool_count(xg.shape, w)
            pooled = (_window_sum(xg, w, True) / cnt - xg).astype(BF16)
            mixed = jnp.dot(pooled, m_ref[g], preferred_element_type=F32)
            dmx = dm_ref[:, cols]
            dsc_ref[:, cols] += jnp.sum(dmx * mixed, axis=0, keepdims=True)
            dmp = (dmx * sc_ref[:, cols]).astype(BF16)
            dmaps_ref[g] += lax.dot_general(pooled, dmp, (((0,), (0,)), ((), ())), preferred_element_type=F32)
            dpooled = lax.dot_general(dmp, m_ref[g], (((1,), (1,)), ((), ())), preferred_element_type=F32)
            dx_ref[:, cols] = (_window_sum(dpooled / cnt, w, False) - dpooled).astype(BF16)

    blk = pl.BlockSpec((S, P), lambda b: (b, 0))
    mp = pl.BlockSpec((N_POOL_GROUPS, G, G), lambda b: (0, 0, 0))
    vec = pl.BlockSpec((1, P), lambda b: (0, 0))
    return pl.pallas_call(
        body, name=name, grid=(B,), in_specs=[blk, blk, mp, vec], out_specs=[blk, mp, vec],
        out_shape=[jax.ShapeDtypeStruct((T, P), BF16), jax.ShapeDtypeStruct((N_POOL_GROUPS, G, G), F32),
                   jax.ShapeDtypeStruct((1, P), F32)],
        compiler_params=_cparams(("arbitrary",)),
    )(xp, dmixed, maps, scale.reshape(1, P))


def _keys(kv_ref, kr_ref, rows):
    return jnp.concatenate([kv_ref[rows, :QK_NOPE_DIM], kr_ref[rows, :]], axis=1)


def _causal(s):
    row = lax.broadcasted_iota(jnp.int32, s.shape, 0)
    col = lax.broadcasted_iota(jnp.int32, s.shape, 1)
    return jnp.where(row >= col, s, NEG_INF)


def attn_fwd(q, kv, kr, cs, sn, B, *, name):
    T = q.shape[0]
    S = T // B
    blk = min(ATT_BLOCK, S)
    nb = S // blk
    H = N_HEADS
    scale = QK_DIM ** -0.5

    def body(q_ref, kv_ref, kr_ref, c_ref, s_ref, o_ref, lse_ref, qs_ref):
        qs_ref[:, :QK_NOPE_DIM] = (q_ref[:, :QK_NOPE_DIM] * scale).astype(BF16)
        qs_ref[:, QK_NOPE_DIM:] = _rope(q_ref[:, QK_NOPE_DIM:], c_ref[...] * scale, s_ref[...] * scale).astype(BF16)
        for qi in range(nb):
            rows = slice(qi * blk, (qi + 1) * blk)
            qb = qs_ref[rows, :]
            sd = _causal(lax.dot_general(qb, _keys(kv_ref, kr_ref, rows), _NT, preferred_element_type=F32))
            m = jnp.max(sd, axis=-1, keepdims=True)
            if qi > 0:
                prev = slice(0, qi * blk)
                sp = lax.dot_general(qb, _keys(kv_ref, kr_ref, prev), _NT, preferred_element_type=F32)
                m = jnp.maximum(m, jnp.max(sp, axis=-1, keepdims=True))
            pd = jnp.exp(sd - m)
            l = jnp.sum(pd, axis=-1, keepdims=True)
            acc = jnp.dot(pd.astype(BF16), kv_ref[rows, QK_NOPE_DIM:], preferred_element_type=F32)
            if qi > 0:
                pp = jnp.exp(sp - m)
                l = l + jnp.sum(pp, axis=-1, keepdims=True)
                acc = acc + jnp.dot(pp.astype(BF16), kv_ref[prev, QK_NOPE_DIM:], preferred_element_type=F32)
            o_ref[rows, :] = (acc / l).astype(BF16)
            lse_ref[0, rows, :] = m + jnp.log(l)

    head = pl.BlockSpec((S, HEAD_PAD), lambda b, h: (b, h))
    shared = pl.BlockSpec((S, LANES), lambda b, h: (b, 0))
    return pl.pallas_call(
        body, name=name, grid=(B, H),
        in_specs=[head, head, shared, shared, shared],
        out_specs=[pl.BlockSpec((S, V_HEAD_DIM), lambda b, h: (b, h)), pl.BlockSpec((1, S, 1), lambda b, h: (h, b, 0)),
                   head],
        out_shape=[jax.ShapeDtypeStruct((T, H * V_HEAD_DIM), BF16), jax.ShapeDtypeStruct((H, T, 1), F32),
                   jax.ShapeDtypeStruct((T, H * HEAD_PAD), BF16)],
        compiler_params=_cparams(("parallel", "parallel")),
    )(q, kv, kr, cs, sn)


def attn_bwd(q, kv, kr, o, do, lse, cs, sn, B, *, name):
    T = q.shape[0]
    S = T // B
    blk = min(ATT_BLOCK, S)
    nb = S // blk
    H = N_HEADS
    scale = QK_DIM ** -0.5

    def body(q_ref, kv_ref, kr_ref, o_ref, do_ref, lse_ref, c_ref, s_ref, dq_ref, dkv_ref, dkr_ref, dk_s, dv_s):
        dk_s[...] = jnp.zeros_like(dk_s)
        dv_s[...] = jnp.zeros_like(dv_s)

        @pl.when(pl.program_id(1) == 0)
        def _():
            dkr_ref[...] = jnp.zeros_like(dkr_ref)

        for qi in range(nb):
            rows = slice(qi * blk, (qi + 1) * blk)
            qb = q_ref[rows, :]
            dob = do_ref[rows, :]
            delta = jnp.sum(dob.astype(F32) * o_ref[rows, :].astype(F32), axis=-1, keepdims=True)
            lse_b = lse_ref[0, rows, :]

            def part(ks, diagonal):
                k = _keys(kv_ref, kr_ref, ks)
                s = lax.dot_general(qb, k, _NT, preferred_element_type=F32)
                if diagonal:
                    s = _causal(s)
                p = jnp.exp(s - lse_b)
                dp = lax.dot_general(dob, kv_ref[ks, QK_NOPE_DIM:], _NT, preferred_element_type=F32)
                ds = (p * (dp - delta)).astype(BF16)
                dv_s[ks, :] += lax.dot_general(p.astype(BF16), dob, _TN, preferred_element_type=F32)
                dk_s[ks, :] += lax.dot_general(ds, qb, _TN, preferred_element_type=F32)
                return jnp.dot(ds, k, preferred_element_type=F32)

            dq = part(rows, True)
            if qi > 0:
                dq = dq + part(slice(0, qi * blk), False)
            dq_ref[rows, :QK_NOPE_DIM] = (dq[:, :QK_NOPE_DIM] * scale).astype(BF16)
            dq_ref[rows, QK_NOPE_DIM:] = _rope_t(dq[:, QK_NOPE_DIM:], c_ref[rows, :] * scale,
                                                 s_ref[rows, :] * scale).astype(BF16)

        dkv_ref[:, :QK_NOPE_DIM] = dk_s[:, :QK_NOPE_DIM].astype(BF16)
        dkv_ref[:, QK_NOPE_DIM:] = dv_s[...].astype(BF16)
        dkr_ref[...] += dk_s[:, QK_NOPE_DIM:]

    head = lambda w: pl.BlockSpec((S, w), lambda b, h: (b, h))
    shared = pl.BlockSpec((S, LANES), lambda b, h: (b, 0))
    return pl.pallas_call(
        body, name=name, grid=(B, H),
        in_specs=[head(HEAD_PAD), head(HEAD_PAD), shared, head(V_HEAD_DIM), head(V_HEAD_DIM),
                  pl.BlockSpec((1, S, 1), lambda b, h: (h, b, 0)), shared, shared],
        out_specs=[head(HEAD_PAD), head(HEAD_PAD), shared],
        out_shape=[jax.ShapeDtypeStruct((T, H * HEAD_PAD), BF16), jax.ShapeDtypeStruct((T, H * HEAD_PAD), BF16),
                   jax.ShapeDtypeStruct((T, LANES), F32)],
        scratch_shapes=[pltpu.VMEM((S, HEAD_PAD), F32), pltpu.VMEM((S, V_HEAD_DIM), F32)],
        compiler_params=_cparams(("parallel", "arbitrary")),
    )(q, kv, kr, o, do, lse, cs, sn)


def adamw(w, g, m, v, *, name, dep=None, copy_g=False):
    R, C = w.shape
    cap = max(8, (1024 * 1024) // (C * 4))
    tr = _pick(R, tuple(c for c in (1024, 512, 256, 128, 64, 32, 16, 8) if c <= cap))
    c1 = 1.0 - ADAM_B1 ** ADAM_STEP
    c2 = 1.0 - ADAM_B2 ** ADAM_STEP
    nout = 4 if copy_g else 3

    def body(w_ref, g_ref, m_ref, v_ref, *rest):
        d_ref, nm_ref, nv_ref = rest[-3:]
        gv = g_ref[...]
        if copy_g:
            rest[-4][...] = gv
        mv = ADAM_B1 * m_ref[...] + (1.0 - ADAM_B1) * gv
        vv = ADAM_B2 * v_ref[...] + (1.0 - ADAM_B2) * (gv * gv)
        nm_ref[...] = mv
        nv_ref[...] = vv
        d_ref[...] = -ADAM_LR * ((mv / c1) / (jnp.sqrt(vv / c2) + ADAM_EPS) + ADAM_WD * w_ref[...])

    blk = pl.BlockSpec((tr, C), lambda i: (i, 0))
    sh = jax.ShapeDtypeStruct((R, C), F32)
    extra = [] if dep is None else [dep]
    return pl.pallas_call(
        body, name=name, grid=(R // tr,), in_specs=[blk] * 4 + [ANY] * len(extra), out_specs=[blk] * nout,
        out_shape=[sh] * nout, compiler_params=_cparams(("parallel",)),
    )(w, g, m, v, *extra)


ANY = pl.BlockSpec(memory_space=pl.ANY)


def _place():
    x, y, c = lax.axis_index("x"), lax.axis_index("y"), lax.axis_index("c")
    others = [(1 - x, y), (x, 1 - y), (1 - x, 1 - y)]
    return x, y, c, others


def _remote(src, dst, ssem, rsem, dev):
    return pltpu.make_async_remote_copy(src_ref=src, dst_ref=dst, send_sem=ssem, recv_sem=rsem,
                                        device_id=dev, device_id_type=MESH)


def _half(ref_rows, c):
    hr = ref_rows // 2
    return pl.ds(pl.multiple_of(c * hr, 16), hr)


HBM = pl.BlockSpec(memory_space=pltpu.HBM)
SEMS = pl.BlockSpec(memory_space=pltpu.SEMAPHORE)
EFFECT = pltpu.SideEffectType.DATAFLOW_SIDE_EFFECTING


def exchange_begin(name, srcs, land_shapes, plan, ncopies, after=None):
    ns, nl = len(srcs), len(land_shapes)
    nin = ns + nl + (0 if after is None else 1)

    def body(*refs):
        ssem, rsem = refs[nin], refs[nin + 1]
        for k, (s, d, dev) in enumerate(plan(refs[:ns], refs[ns:ns + nl])):
            _remote(s, d, ssem.at[k], rsem.at[k], dev).start()
        refs[-1][...] = jnp.zeros_like(refs[-1])

    bufs = [pltpu.HBM(s.shape, s.dtype) for s in srcs] + [pltpu.HBM(s.shape, s.dtype) for s in land_shapes]
    args = [pltpu.with_memory_space_constraint(s, pltpu.HBM) for s in srcs]
    args += [pltpu.with_memory_space_constraint(lax.empty(s.shape, s.dtype), pltpu.HBM) for s in land_shapes]
    if after is not None:
        args.append(after)
    out = pl.pallas_call(
        body, name=name,
        out_shape=(pltpu.SemaphoreType.DMA((ncopies,)), pltpu.SemaphoreType.DMA((ncopies,)), *bufs,
                   jax.ShapeDtypeStruct((8, LANES), F32)),
        in_specs=[HBM] * (ns + nl) + ([] if after is None else [ANY]),
        out_specs=(SEMS, SEMS, *([HBM] * (ns + nl)), pl.BlockSpec(memory_space=pltpu.VMEM)),
        input_output_aliases={i: 2 + i for i in range(ns + nl)},
        compiler_params=pltpu.CompilerParams(has_side_effects=EFFECT),
    )(*args)
    return (out[0], out[1], out[2:2 + ns], out[2 + ns:2 + ns + nl]), out[-1]


def exchange_end(name, handle, plan, after):
    ssem, rsem, srcs, lands = handle
    ns, nl = len(srcs), len(lands)

    def body(*refs):
        ssem_ref, rsem_ref = refs[ns + nl], refs[ns + nl + 1]
        for k, (s, d, dev) in enumerate(plan(refs[:ns], refs[ns:ns + nl])):
            cp = _remote(s, d, ssem_ref.at[k], rsem_ref.at[k], dev)
            cp.wait_send()
            cp.wait_recv()

    out = pl.pallas_call(
        body, name=name,
        out_shape=tuple(pltpu.HBM(s.shape, s.dtype) for s in (*srcs, *lands)),
        in_specs=[HBM] * (ns + nl) + [SEMS, SEMS, ANY], out_specs=tuple([HBM] * (ns + nl)),
        input_output_aliases={i: i for i in range(ns + nl)},
        compiler_params=pltpu.CompilerParams(has_side_effects=EFFECT),
    )(*srcs, *lands, ssem, rsem, after)
    return list(out[:ns]), list(out[ns:])


def ag_plan(src_refs, land_refs):
    x, y, c, others = _place()
    plan = []
    for s, d in zip(src_refs, land_refs):
        mine = _half(s.shape[0], c)
        for ox, oy in others:
            plan.append((s.at[mine, :], d.at[2 * x + y, mine, :], (ox, oy, c)))
    return plan


def ag_forward_plan(src_refs, land_refs):
    x, y, c, others = _place()
    plan = []
    for s in src_refs:
        mine = _half(s.shape[1], c)
        for ox, oy in others:
            blk = s.at[2 * ox + oy, mine, :]
            plan.append((blk, blk, (x, y, 1 - c)))
    return plan


def rs_swap_plan(src_refs, land_refs):
    x, y, c, _ = _place()
    return [(s.at[:, _half(s.shape[1], 1 - c), :], d, (x, y, 1 - c)) for s, d in zip(src_refs, land_refs)]


def ag_forward(lands, *, name):
    n = len(lands)

    def body(*refs):
        ins, outs = refs[:n], refs[n:2 * n]
        ssem, rsem = refs[2 * n:]
        x, y, c, others = _place()
        sent = []
        for i in range(n):
            mine = _half(ins[i].shape[1], c)
            for j, (ox, oy) in enumerate(others):
                cp = _remote(ins[i].at[2 * ox + oy, mine, :], outs[i].at[2 * ox + oy, mine, :], ssem.at[3 * i + j],
                             rsem.at[3 * i + j], (x, y, 1 - c))
                cp.start()
                sent.append(cp)
        for cp in sent:
            cp.wait()

    return pl.pallas_call(
        body, name=name, in_specs=[ANY] * n, out_specs=[ANY] * n,
        out_shape=[jax.ShapeDtypeStruct(a.shape, a.dtype) for a in lands],
        input_output_aliases={i: i for i in range(n)},
        scratch_shapes=[pltpu.SemaphoreType.DMA((3 * n,)), pltpu.SemaphoreType.DMA((3 * n,))],
        compiler_params=pltpu.CompilerParams(has_side_effects=True),
    )(*lands)


def place_own(lands, own, chip, *, name):
    n = len(lands)
    steps = 4

    def body(chip_ref, *refs):
        for i in range(n):
            refs[2 * n + i][0] = refs[i][...]

    in_specs = [pl.BlockSpec((o.shape[0] // steps, o.shape[1]), lambda t, q: (t, 0)) for o in own] + [ANY] * n
    out_specs = [pl.BlockSpec((1, o.shape[0] // steps, o.shape[1]), lambda t, q: (q[0], t, 0)) for o in own]
    return pl.pallas_call(
        body, name=name,
        grid_spec=pltpu.PrefetchScalarGridSpec(num_scalar_prefetch=1, grid=(steps,), in_specs=in_specs,
                                               out_specs=out_specs),
        out_shape=[jax.ShapeDtypeStruct(a.shape, a.dtype) for a in lands],
        input_output_aliases={1 + n + i: i for i in range(n)},
        compiler_params=_cparams(("parallel",)),
    )(chip, *own, *lands)


def rs_swap_halves(grads, *, name, after=None):
    n = len(grads)
    extra = [] if after is None else [after]
    nin = n + len(extra)

    def body(*refs):
        ins, outs = refs[:n], refs[nin:nin + n]
        ssem, rsem = refs[nin + n:]
        x, y, c, _ = _place()
        cps = []
        for i in range(n):
            theirs = _half(ins[i].shape[1], 1 - c)
            cp = _remote(ins[i].at[:, theirs, :], outs[i], ssem.at[i], rsem.at[i], (x, y, 1 - c))
            cp.start()
            cps.append(cp)
        for cp in cps:
            cp.wait()

    return pl.pallas_call(
        body, name=name, in_specs=[ANY] * nin, out_specs=[ANY] * n,
        out_shape=[jax.ShapeDtypeStruct((4, g.shape[1] // 2, g.shape[2]), g.dtype) for g in grads],
        scratch_shapes=[pltpu.SemaphoreType.DMA((n,)), pltpu.SemaphoreType.DMA((n,))],
        compiler_params=pltpu.CompilerParams(has_side_effects=True),
    )(*grads, *extra)


def rs_chip_sum(g, r1, core, *, name):
    _, r, cdim = g.shape
    hr = r // 2

    def body(c_ref, g_ref, r1_ref, o_ref):
        o_ref[...] = (g_ref[...].astype(F32) + r1_ref[...].astype(F32)).astype(BF16)

    return pl.pallas_call(
        body, name=name,
        grid_spec=pltpu.PrefetchScalarGridSpec(
            num_scalar_prefetch=1, grid=(4,),
            in_specs=[pl.BlockSpec((1, hr, cdim), lambda qq, c_ref: (qq, c_ref[0], 0)),
                      pl.BlockSpec((1, hr, cdim), lambda qq, c_ref: (qq, 0, 0))],
            out_specs=pl.BlockSpec((1, hr, cdim), lambda qq, c_ref: (qq, 0, 0))),
        out_shape=jax.ShapeDtypeStruct((4, hr, cdim), BF16),
        compiler_params=_cparams(("parallel",)),
    )(core, g, r1)


def rs_plan(src_refs, land_refs):
    x, y, c, others = _place()
    plan = []
    for s, d in zip(src_refs, land_refs):
        for j, (ox, oy) in enumerate(others):
            plan.append((s.at[2 * ox + oy], d.at[j], (ox, oy, c)))
    return plan


def rs_final_sum(g, r1, r2, place, acc, l, *, name):
    _, r, cdim = g.shape
    hr = r // 2
    ch = hr // 2

    def body(p_ref, g_ref, r1_ref, a_ref, b_ref, d_ref, acc_in, o_ref):
        s = g_ref[...].astype(F32) + r1_ref[...].astype(F32)
        s = s + a_ref[...].astype(F32)
        s = s + b_ref[...].astype(F32)
        o_ref[...] = s + d_ref[...].astype(F32)

    other = lambda j: pl.BlockSpec((1, ch, cdim), lambda t, p_ref: (j, t, 0))
    return pl.pallas_call(
        body, name=name,
        grid_spec=pltpu.PrefetchScalarGridSpec(
            num_scalar_prefetch=1, grid=(2,),
            in_specs=[pl.BlockSpec((1, ch, cdim), lambda t, p_ref: (p_ref[0], 2 * p_ref[1] + t, 0)),
                      pl.BlockSpec((1, ch, cdim), lambda t, p_ref: (p_ref[0], t, 0)),
                      other(0), other(1), other(2), ANY],
            out_specs=pl.BlockSpec((1, ch, cdim), lambda t, p_ref: (l, 2 * p_ref[1] + t, 0))),
        out_shape=jax.ShapeDtypeStruct(acc.shape, F32),
        input_output_aliases={6: 0},
        compiler_params=_cparams(("parallel",)),
    )(place, g, r1, r2, r2, r2, acc)


def rs_join_halves(grads, *, name):
    n = len(grads)

    def body(*refs):
        ins, outs = refs[:n], refs[n:2 * n]
        ssem, rsem = refs[2 * n:]
        x, y, c, _ = _place()
        cps = []
        for i in range(n):
            mine = _half(ins[i].shape[1], c)
            cp = _remote(ins[i].at[:, mine, :], outs[i].at[:, mine, :], ssem.at[i], rsem.at[i], (x, y, 1 - c))
            cp.start()
            cps.append(cp)
        for cp in cps:
            cp.wait()

    return pl.pallas_call(
        body, name=name, in_specs=[ANY] * n, out_specs=[ANY] * n,
        out_shape=[jax.ShapeDtypeStruct(g.shape, g.dtype) for g in grads],
        input_output_aliases={i: i for i in range(n)},
        scratch_shapes=[pltpu.SemaphoreType.DMA((n,)), pltpu.SemaphoreType.DMA((n,))],
        compiler_params=pltpu.CompilerParams(has_side_effects=True),
    )(*grads)


def all_reduce_small(vs):
    n = len(vs)

    def body(*refs):
        v_refs, o_refs = refs[:n], refs[n:2 * n]
        sibs, sums, gots = refs[2 * n:3 * n], refs[3 * n:4 * n], refs[4 * n:5 * n]
        ssem, rsem = refs[5 * n:]
        x, y, c, others = _place()
        q = 2 * x + y
        sib = (x, y, 1 - c)

        def wait_all(cps):
            for cp in cps:
                cp.wait()

        cps = []
        for i in range(n):
            hr = vs[i].shape[0] // 2
            theirs = pl.ds(pl.multiple_of((1 - c) * hr, 8), hr)
            cps.append(_remote(v_refs[i].at[theirs, :], sibs[i], ssem.at[8 * i], rsem.at[8 * i], sib))
            cps[-1].start()
        wait_all(cps)
        cps = []
        for i in range(n):
            hr, rs = vs[i].shape[0] // 2, vs[i].shape[0] // 8
            sums[i][...] = v_refs[i][pl.ds(pl.multiple_of(c * hr, 8), hr), :] + sibs[i][...]
            for j, (ox, oy) in enumerate(others):
                piece = pl.ds(pl.multiple_of((2 * ox + oy) * rs, 8), rs)
                cps.append(_remote(sums[i].at[piece, :], gots[i].at[j], ssem.at[8 * i + 1 + j], rsem.at[8 * i + 1 + j],
                                   (ox, oy, c)))
                cps[-1].start()
        wait_all(cps)
        cps = []
        for i in range(n):
            hr, rs = vs[i].shape[0] // 2, vs[i].shape[0] // 8
            acc = sums[i][pl.ds(pl.multiple_of(q * rs, 8), rs), :]
            for j in range(3):
                acc = acc + gots[i][j]
            mine = o_refs[i].at[pl.ds(pl.multiple_of(c * hr + q * rs, 8), rs), :]
            o_refs[i][pl.ds(pl.multiple_of(c * hr + q * rs, 8), rs), :] = acc
            for j, (ox, oy) in enumerate(others):
                cps.append(_remote(mine, mine, ssem.at[8 * i + 4 + j], rsem.at[8 * i + 4 + j], (ox, oy, c)))
                cps[-1].start()
        wait_all(cps)
        cps = []
        for i in range(n):
            hr = vs[i].shape[0] // 2
            half = o_refs[i].at[pl.ds(pl.multiple_of(c * hr, 8), hr), :]
            cps.append(_remote(half, half, ssem.at[8 * i + 7], rsem.at[8 * i + 7], sib))
            cps[-1].start()
        wait_all(cps)

    vm = pl.BlockSpec(memory_space=pltpu.VMEM)
    scratch = [pltpu.VMEM((v.shape[0] // 2, LANES), F32) for v in vs] * 2
    scratch += [pltpu.VMEM((3, v.shape[0] // 8, LANES), F32) for v in vs]
    return pl.pallas_call(
        body, name="all_reduce_small", in_specs=[vm] * n, out_specs=[vm] * n,
        out_shape=[jax.ShapeDtypeStruct(v.shape, F32) for v in vs],
        scratch_shapes=scratch + [pltpu.SemaphoreType.DMA((8 * n,))] * 2,
        compiler_params=pltpu.CompilerParams(vmem_limit_bytes=VMEM_LIMIT, has_side_effects=True),
    )(*vs)


def _to_stacked(name, full):
    R, C = full.shape
    if name in ROW_SHARDED:
        return full.reshape(4, R // 4, C)
    return jnp.transpose(full.reshape(R, 4, C // 4), (1, 0, 2))


def _from_stacked(name, st):
    _, r, c = st.shape
    if name in ROW_SHARDED:
        return st.reshape(4 * r, c)
    return jnp.transpose(st, (1, 0, 2)).reshape(r, 4 * c)


UP_PIECES = ("ffn1_up", "ffn2_up")


def _layer_weights(lands):
    w = {n: lands[n] if n in UP_PIECES else _from_stacked(n, lands[n]) for n in lands}
    if "w_in" not in w:
        return w
    win = w.pop("w_in")
    D = win.shape[0]
    p0, p1, p2, p3 = POOL_DIM, POOL_DIM + Q_LORA_RANK, POOL_DIM + Q_LORA_RANK + KV_LORA_RANK, \
        POOL_DIM + Q_LORA_RANK + KV_LORA_RANK + QK_ROPE_DIM
    w["w_pool"] = win[:, :p0]
    w["w_lat"] = jnp.concatenate([win[:, p0:p3], jnp.zeros((D, LAT_DIM - (p3 - p0)), win.dtype)], axis=1)
    w["w_gate"] = win[:, p3:]
    uq = w["w_uq"].reshape(Q_LORA_RANK, N_HEADS, QK_DIM)
    w["w_uq"] = jnp.concatenate([uq, jnp.zeros((Q_LORA_RANK, N_HEADS, HEAD_PAD - QK_DIM), uq.dtype)],
                                axis=2).reshape(Q_LORA_RANK, N_HEADS * HEAD_PAD)
    return w


def _layer_grads_stacked(dw):
    dw = dict(dw)
    if "w_lat" in dw:
        lat = dw.pop("w_lat")
        dw["w_in"] = jnp.concatenate([dw.pop("w_pool"), lat[:, :Q_LORA_RANK + KV_LORA_RANK + QK_ROPE_DIM],
                                      dw.pop("w_gate")], axis=1)
        dw["w_uq"] = dw["w_uq"].reshape(Q_LORA_RANK, N_HEADS, HEAD_PAD)[:, :, :QK_DIM].reshape(Q_LORA_RANK,
                                                                                                 N_HEADS * QK_DIM)
    return {n: dw[n] if n in UP_PIECES else _to_stacked(n, dw[n]) for n in BIG if n in dw}


def _rope_tables(positions):
    inv_freq = ROPE_THETA ** (-jnp.arange(0, QK_ROPE_DIM, 2, dtype=F32) / QK_ROPE_DIM)
    ang = positions.astype(F32).reshape(-1)[:, None] * inv_freq
    cos, sin = jnp.cos(ang), jnp.sin(ang)
    z = jnp.zeros((ang.shape[0], LANES - QK_ROPE_DIM), F32)
    return jnp.concatenate([cos, cos, z], axis=1), jnp.concatenate([-sin, sin, z], axis=1)


def _pack_small(vals):
    parts, total = [], 0
    for n in PACKED:
        f = vals[n].reshape(-1).astype(F32)
        pad = (-f.shape[0]) % (8 * LANES)
        parts.append(jnp.pad(f, (0, pad)))
        total += f.shape[0] + pad
    parts.append(jnp.zeros(((-total) % (64 * LANES),), F32))
    return jnp.concatenate(parts).reshape(-1, LANES)


def _unpack_small(packed, like):
    flat = packed.reshape(-1)
    out, off = {}, 0
    for n in PACKED:
        size = like[n].size
        out[n] = flat[off:off + size].reshape(like[n].shape)
        off += size + ((-size) % (8 * LANES))
    return out


def _ffn_fwd(x, g, wu4, wd, tag, dep=None):
    h = rms_fwd(x, g, dep=dep, name=f"{tag}_norm")
    gate, up, a = ffn_up_act(h, wu4, name=f"{tag}_up_act")
    y = mm(a, wd, res=x, alpha=0.5, name=f"{tag}_down")
    return y, (x, h, gate, up, a)


def _ffn_bwd(dy, saved, g, wu4, wd, tag, dep=None):
    x, h, gate, up, a = saved
    dgate, dup = ffn_down_dx_act(dy, wd, gate, up, dep=dep, name=f"{tag}_down_dx_act")
    dwd = mm(a, dy, ta=True, alpha=0.5, out_dtype=BF16, name=f"{tag}_down_dw")
    dwu4 = ffn_up_dw(h, dgate, dup, name=f"{tag}_up_dw")
    dx, dg = ffn_up_dx_norm(dgate, dup, wu4, x, g, dy, name=f"{tag}_up_dx_norm")
    return dx, dg, dwu4, dwd


def _mix_fwd(x, p, w, cs, sn, B, dep=None):
    h = rms_fwd(x, p["norm_mix"], dep=dep, name="mix_norm")
    lat, xp, gl = mix_in(h, w["w_lat"], w["w_pool"], w["w_gate"], name="mix_in")
    mixed = pool_fwd(xp, p["pool_maps"].astype(BF16), p["pool_scale"], B, name="pool_fwd")
    ba = mm(mixed, w["w_pool_proj"], out_dtype=BF16, name="mix_pool_proj")
    qn, kvn, kr = lat_fwd(lat, p["q_latent_norm"], p["kv_latent_norm"], cs, sn, name="lat_fwd")
    kv = mm(kvn, w["w_ukv"], out_dtype=BF16, name="mix_ukv")
    o, lse, q = attn_fwd(mm(qn, w["w_uq"], name="mix_uq"), kv, kr, cs, sn, B, name="attn_fwd")
    bb, merged = attn_proj_gate(o, w["w_attn_proj"], gl, p["b_gate"], ba, name="mix_attn_proj_gate")
    y = mm(merged, w["w_out"], res=x, name="mix_out")
    return y, (x, h, lat, xp, gl, mixed, ba, qn, kvn, kr, q, kv, o, lse, bb, merged)


def _mix_bwd(dy, saved, p, w, cs, sn, B, dep=None):
    x, h, lat, xp, gl, mixed, ba, qn, kvn, kr, q, kv, o, lse, bb, merged = saved
    dw, ds = {}, {}
    dw["w_out"] = mm(merged, dy, ta=True, out_dtype=BF16, name="mix_out_dw")
    dba, dbb, dgl, ds["b_gate"] = out_dx_gate(dy, w["w_out"], gl, p["b_gate"], ba, bb, dep=dep,
                                              name="mix_out_dx_gate")
    dw["w_attn_proj"] = mm(o, dbb, ta=True, out_dtype=BF16, name="mix_attn_proj_dw")
    do = mm(dbb, w["w_attn_proj"], tb=True, out_dtype=BF16, name="mix_attn_proj_dx")
    dw["w_pool_proj"] = mm(mixed, dba, ta=True, out_dtype=BF16, name="mix_pool_proj_dw")
    dmixed = mm(dba, w["w_pool_proj"], tb=True, name="mix_pool_proj_dx")
    dxp, ds["pool_maps"], ds["pool_scale"] = pool_bwd(xp, dmixed, p["pool_maps"].astype(BF16), p["pool_scale"], B,
                                                      name="pool_bwd")
    dqb, dkv, dkr = attn_bwd(q, kv, kr, o, do, lse, cs, sn, B, name="attn_bwd")
    dw["w_ukv"] = mm(kvn, dkv, ta=True, out_dtype=BF16, name="mix_ukv_dw")
    dkvn = mm(dkv, w["w_ukv"], tb=True, name="mix_ukv_dx")
    dw["w_uq"] = mm(qn, dqb, ta=True, out_dtype=BF16, name="mix_uq_dw")
    dqn = mm(dqb, w["w_uq"], tb=True, name="mix_uq_dx")
    dlat, ds["q_latent_norm"], ds["kv_latent_norm"] = lat_bwd(lat, p["q_latent_norm"], p["kv_latent_norm"], dqn, dkvn,
                                                               dkr, cs, sn, name="lat_bwd")
    dw["w_lat"] = mm(h, dlat, ta=True, out_dtype=BF16, name="mix_lat_dw")
    dw["w_pool"] = mm(h, dxp, ta=True, out_dtype=BF16, name="mix_pool_in_dw")
    dw["w_gate"] = mm(h, dgl, ta=True, out_dtype=BF16, name="mix_gate_in_dw")
    dx, ds["norm_mix"] = mix_in_dx_norm(dlat, dxp, dgl, w["w_lat"], w["w_pool"], w["w_gate"], x, p["norm_mix"], dy,
                                        name="mix_in_dx_norm")
    return dx, dw, ds


def kernel(x, positions, norm_ffn1, ffn1_up, ffn1_down, norm_mix, w_in, b_gate, pool_maps, pool_scale, w_pool_proj, q_latent_norm, w_uq, kv_latent_norm, w_ukv, w_attn_proj, w_out, norm_ffn2, ffn2_up, ffn2_down, final_norm, loss_target, m_norm_ffn1, m_ffn1_up, m_ffn1_down, m_norm_mix, m_w_in, m_b_gate, m_pool_maps, m_pool_scale, m_w_pool_proj, m_q_latent_norm, m_w_uq, m_kv_latent_norm, m_w_ukv, m_w_attn_proj, m_w_out, m_norm_ffn2, m_ffn2_up, m_ffn2_down, m_final_norm, v_norm_ffn1, v_ffn1_up, v_ffn1_down, v_norm_mix, v_w_in, v_b_gate, v_pool_maps, v_pool_scale, v_w_pool_proj, v_q_latent_norm, v_w_uq, v_kv_latent_norm, v_w_ukv, v_w_attn_proj, v_w_out, v_norm_ffn2, v_ffn2_up, v_ffn2_down, v_final_norm):
    given = dict(locals())
    B, S, D = x.shape
    T = B * S
    L = norm_ffn1.shape[0]
    W = {n: given[n] for n in WEIGHTS}
    Mo = {n: given["m_" + n] for n in WEIGHTS}
    Vo = {n: given["v_" + n] for n in WEIGHTS}
    core = lax.axis_index("c").astype(jnp.int32)
    chip = (2 * lax.axis_index("x") + lax.axis_index("y")).astype(jnp.int32)

    core_arr = core.reshape(1)
    chip_arr = chip.reshape(1)
    place = jnp.stack([chip, core])
    first = ("ffn1_up", "ffn1_down")
    rest = tuple(n for n in BIG if n not in first)

    own = [{n: W[n][l].astype(BF16) for n in BIG} for l in range(L)]

    def ag_begin(l, names, tag, after=None):
        lands = [jax.ShapeDtypeStruct((4,) + own[l][n].shape, BF16) for n in names]
        return exchange_begin(f"ag_start_{tag}", [own[l][n] for n in names], lands, ag_plan, 3 * len(names), after)

    def ag_finish(handle, names, tag, after):
        mine, lands = exchange_end(f"ag_wait_{tag}", handle, ag_plan, after)
        lands = ag_forward(lands, name=f"ag_forward_{tag}")
        return _layer_weights(dict(zip(names, place_own(lands, mine, chip_arr, name=f"place_own_{tag}"))))

    h_first, t1 = ag_begin(0, first, "0a")
    cs, sn = _rope_tables(positions)
    xs = x.reshape(T, D) + t1[0, 0]
    saved, ici, handed = [], None, None
    for l in range(L):
        p = {n: W[n][l] for n in SMALL if n != "final_norm"}
        dep = None
        if l == 0:
            w = ag_finish(h_first, first, "0a", xs)
            h_rest, dep = ag_begin(0, rest, "0b", after=w["ffn1_down"])
        else:
            mine, handle = handed
            lands, _ = exchange_end(f"ag_forward_wait_{l}", handle, ag_forward_plan, xs)
            w = _layer_weights(dict(zip(BIG, place_own(lands, mine, chip_arr, name=f"place_own_{l}"))))
        xs, s1 = _ffn_fwd(xs, p["norm_ffn1"], w["ffn1_up"], w["ffn1_down"], "ffn1", dep=dep)
        dep = None
        if l == 0:
            w.update(ag_finish(h_rest, rest, "0b", xs))
            if L > 1:
                ici, dep = ag_begin(1, BIG, "1", after=xs)
        xs, s2 = _mix_fwd(xs, p, w, cs, sn, B, dep=dep)
        dep = None
        if l + 1 < L:
            mine, lands = exchange_end(f"ag_wait_{l + 1}", ici, ag_plan, xs)
            handle, dep = exchange_begin(f"ag_forward_start_{l + 1}", lands, [], ag_forward_plan, 3 * len(BIG))
            handed = (mine, handle)
            if l + 2 < L:
                ici, dep = ag_begin(l + 2, BIG, str(l + 2), after=dep)
        xs, s3 = _ffn_fwd(xs, p["norm_ffn2"], w["ffn2_up"], w["ffn2_down"], "ffn2", dep=dep)
        saved.append((w, p, s1, s2, s3))

    dx, dfinal, loss_tile = loss_head(xs, final_norm, loss_target.reshape(T, D), name="loss_head")
    loss = lax.psum(loss_tile[0, 0], ("x", "y", "c"))

    def rs_begin(dw, tag, after=None):
        stacked = _layer_grads_stacked(dw)
        names = tuple(stacked)
        parts = [stacked[n] for n in names]
        r1 = rs_swap_halves(parts, after=after, name=f"rs_swap_{tag}")
        sums = [rs_chip_sum(g, a, core_arr, name=f"rs_chip_sum_{n}") for n, g, a in zip(names, parts, r1)]
        lands = [jax.ShapeDtypeStruct((3,) + s.shape[1:], BF16) for s in sums]
        handle, token = exchange_begin(f"rs_start_{tag}", sums, lands, rs_plan, 3 * len(names))
        return (names, parts, r1, handle), token

    acc = {n: lax.empty(W[n].shape, F32) for n in BIG}

    def rs_finish(l, pending, tag, after):
        names, parts, r1, handle = pending
        _, r2 = exchange_end(f"rs_wait_{tag}", handle, rs_plan, after)
        for n, g, a, b in zip(names, parts, r1, r2):
            acc[n] = rs_final_sum(g, a, b, place, acc[n], l, name=f"rs_final_sum_{n}")

    small_layers, pending, swapping, dep = [], [], None, None
    for l in reversed(range(L)):
        w, p, s1, s2, s3 = saved[l]
        dx, dg2, dwu2, dwd2 = _ffn_bwd(dx, s3, p["norm_ffn2"], w["ffn2_up"], w["ffn2_down"], "ffn2", dep=dep)
        dep = None
        if swapping is not None:
            above, names, handle = swapping
            parts, r1 = exchange_end(f"rs_swap_wait_{above}", handle, rs_swap_plan, dx)
            sums = [rs_chip_sum(g, a, core_arr, name=f"rs_chip_sum_{n}") for n, g, a in zip(names, parts, r1)]
            lands = [jax.ShapeDtypeStruct((3,) + s.shape[1:], BF16) for s in sums]
            handle, dep = exchange_begin(f"rs_start_{above}", sums, lands, rs_plan, 3 * len(names))
            pending.append((above, (names, parts, r1, handle)))
            swapping = None
        dx, dw, ds = _mix_bwd(dx, s2, p, w, cs, sn, B, dep=dep)
        dw.update(ffn2_up=dwu2, ffn2_down=dwd2)
        dep = None
        if l == 0:
            early, dep = rs_begin(dw, "0b")
            dw = {}
        dx, dg1, dwu1, dwd1 = _ffn_bwd(dx, s1, p["norm_ffn1"], w["ffn1_up"], w["ffn1_down"], "ffn1", dep=dep)
        dw.update(ffn1_up=dwu1, ffn1_down=dwd1)
        ds.update(norm_ffn1=dg1, norm_ffn2=dg2)
        small_layers.append(ds)
        if l == 0:
            last_dw = dw
        else:
            stacked = _layer_grads_stacked(dw)
            halves = [jax.ShapeDtypeStruct((4, g.shape[1] // 2, g.shape[2]), BF16) for g in stacked.values()]
            handle, dep = exchange_begin(f"rs_swap_start_{l}", list(stacked.values()), halves, rs_swap_plan,
                                         len(stacked))
            swapping = (l, tuple(stacked), handle)
    small_layers.reverse()

    small = {n: jnp.stack([small_layers[l][n].reshape(W[n].shape[1:]) for l in range(L)]) for n in SMALL
             if n != "final_norm"}
    small["final_norm"] = dfinal.reshape(final_norm.shape)
    rows = lambda a: a.reshape(-1, LANES)
    reduced, reduced_maps = all_reduce_small([_pack_small(small), rows(small["pool_maps"])])
    grads = _unpack_small(reduced, small)
    grads["pool_maps"] = reduced_maps.reshape(pool_maps.shape)
    last, dep = rs_begin(last_dw, "0a", after=reduced)
    delta, new_m, new_v = {}, {}, {}
    d, nm, nv = adamw(_pack_small(W), reduced, _pack_small(Mo), _pack_small(Vo), dep=dep, name="adamw_small")
    delta.update(_unpack_small(d, W))
    new_m.update(_unpack_small(nm, W))
    new_v.update(_unpack_small(nv, W))
    d, nm, nv = adamw(rows(pool_maps), reduced_maps, rows(m_pool_maps), rows(v_pool_maps), dep=d,
                      name="adamw_pool_maps")
    delta["pool_maps"], new_m["pool_maps"], new_v["pool_maps"] = (a.reshape(pool_maps.shape) for a in (d, nm, nv))

    def update(names, tag, d):
        joined = rs_join_halves([acc[n] for n in names], name=f"rs_join_{tag}")
        for n, g in zip(names, joined):
            flip = (lambda a: jnp.swapaxes(a, 1, 2)) if n == "w_in" else (lambda a: a)
            sh = flip(W[n]).shape
            two = lambda a: flip(a).reshape(sh[0] * sh[1], sh[2])
            back = lambda a: flip(a.reshape(sh))
            gc, d, nm, nv = adamw(two(W[n]), two(g), two(Mo[n]), two(Vo[n]), dep=d, copy_g=True, name=f"adamw_{n}")
            grads[n], delta[n], new_m[n], new_v[n] = back(gc), back(d), back(nm), back(nv)
        return d

    for l, item in pending:
        rs_finish(l, item, str(l), d)
    rs_finish(0, early, "0b", d)
    d = update(rest, "rest", d)
    rs_finish(0, last, "0a", d)
    update(first, "first", d)

    return (loss, dx.reshape(B, S, D), *[grads[n] for n in WEIGHTS], *[delta[n] for n in WEIGHTS],
            *[new_m[n] for n in WEIGHTS], *[new_v[n] for n in WEIGHTS])
```

```python
import functools

import jax
import jax.numpy as jnp
from jax import lax
from jax.experimental import pallas as pl
from jax.experimental.pallas import tpu as pltpu

F32 = jnp.float32
BF16 = jnp.bfloat16

N_HEADS = 8
QK_NOPE_DIM = 128
QK_ROPE_DIM = 64
QK_DIM = QK_NOPE_DIM + QK_ROPE_DIM
V_HEAD_DIM = 128
HEAD_PAD = 256
Q_LORA_RANK = 384
KV_LORA_RANK = 256
ROPE_THETA = 10000.0
POOL_WINDOWS = (2, 4, 8, 16)
N_POOL_GROUPS = 4
POOL_GROUP_DIM = 128
POOL_DIM = N_POOL_GROUPS * POOL_GROUP_DIM
LAT_DIM = 768
NORM_EPS = 1e-6
ADAM_LR = 0.001
ADAM_B1 = 0.9
ADAM_B2 = 0.999
ADAM_EPS = 1e-08
ADAM_WD = 0.01
ADAM_STEP = 10
NEG_INF = -1e30
LANES = 128
ATT_BLOCK = 512
ATT_BLOCK_FWD = 256
VMEM_LIMIT = 48 * 1024 * 1024
MESH = pl.DeviceIdType.MESH
_NT = (((1,), (1,)), ((), ()))
_TN = (((0,), (0,)), ((), ()))

BIG = ("ffn1_up", "ffn1_down", "w_in", "w_pool_proj", "w_uq", "w_ukv", "w_attn_proj", "w_out",
       "ffn2_up", "ffn2_down")
ROW_SHARDED = ("ffn1_down", "w_attn_proj", "w_out", "ffn2_down")
SMALL = ("norm_ffn1", "norm_mix", "b_gate", "pool_maps", "pool_scale", "q_latent_norm",
         "kv_latent_norm", "norm_ffn2", "final_norm")
PACKED = tuple(n for n in SMALL if n != "pool_maps")
WEIGHTS = ("norm_ffn1", "ffn1_up", "ffn1_down", "norm_mix", "w_in", "b_gate", "pool_maps", "pool_scale",
           "w_pool_proj", "q_latent_norm", "w_uq", "kv_latent_norm", "w_ukv", "w_attn_proj", "w_out",
           "norm_ffn2", "ffn2_up", "ffn2_down", "final_norm")


def _pick(dim, cands):
    for c in cands:
        if c <= dim and dim % c == 0:
            return c
    return dim


def _cparams(sem=None, **kw):
    if sem is not None:
        kw["dimension_semantics"] = sem
    return pltpu.CompilerParams(vmem_limit_bytes=VMEM_LIMIT, **kw)


def _sigmoid(x):
    return 0.5 * jnp.tanh(0.5 * x) + 0.5


MM_TILE_BUDGET = 30 * 1024 * 1024
TILE_SIZES = (1408, 1024, 768, 512, 384, 256, 128)


V7X_MXU_FLOPS = 9.0e14
V7X_HBM_BYTES = 2.5e12
GRID_STEP_S = 0.35e-6


def _mm_tiles(M, N, K, sa, sb, so, sr):
    tks = [K] if K <= 2816 else [t for t in (2816, 2048, 1408, 1024, 512, 256, 128) if K % t == 0]
    best = None
    for tk in tks:
        for tm in [t for t in TILE_SIZES if M % t == 0] or [M]:
            for tn in [t for t in TILE_SIZES if N % t == 0] or [N]:
                need = 2 * (tm * tk * sa + tk * tn * sb + tm * tn * (so + sr)) + (tm * tn * 4 if tk < K else 0)
                if need > MM_TILE_BUDGET:
                    continue
                ni, nj, nk = M // tm, N // tn, K // tk
                a_bytes = M * K * sa * (nj if nk > 1 else 1)
                b_bytes = K * N * sb * (1 if nj == 1 and nk == 1 else ni)
                traffic = a_bytes + b_bytes + M * N * (so + sr) + (M * N * 8 * nk if nk > 1 else 0)
                t = max(2.0 * M * N * K / V7X_MXU_FLOPS, traffic / V7X_HBM_BYTES) + ni * nj * nk * GRID_STEP_S
                if best is None or t < best[0]:
                    best = (t, (tm, tn, tk))
    assert best is not None, (M, N, K)
    return best[1]


def mm(a, b, *, name, ta=False, tb=False, out_dtype=F32, res=None, alpha=1.0, dep=None):
    if ta:
        K, M = a.shape
    else:
        M, K = a.shape
    if tb:
        N, K2 = b.shape
    else:
        K2, N = b.shape
    assert K == K2, (a.shape, b.shape, ta, tb)
    tm, tn, tk = _mm_tiles(M, N, K, a.dtype.itemsize, b.dtype.itemsize, jnp.dtype(out_dtype).itemsize,
                           0 if res is None else res.dtype.itemsize)
    nk = K // tk
    dims = (((0 if ta else 1,), (1 if tb else 0,)), ((), ()))

    def body(*refs):
        a_ref, b_ref = refs[:2]
        r_ref = refs[2] if res is not None else None
        o_ref = refs[-2] if nk > 1 else refs[-1]

        def finish(r):
            if alpha != 1.0:
                r = r * alpha
            if res is not None:
                r = r_ref[...].astype(F32) + r
            o_ref[...] = r.astype(out_dtype)

        part = lax.dot_general(a_ref[...].astype(BF16), b_ref[...].astype(BF16), dims, preferred_element_type=F32)
        if nk == 1:
            finish(part)
            return
        acc = refs[-1]
        k = pl.program_id(2)

        @pl.when(k == 0)
        def _():
            acc[...] = part

        @pl.when(k > 0)
        def _():
            acc[...] += part

        @pl.when(k == nk - 1)
        def _():
            finish(acc[...])

    a_spec = pl.BlockSpec((tk, tm), lambda i, j, k: (k, i)) if ta else pl.BlockSpec((tm, tk), lambda i, j, k: (i, k))
    b_spec = pl.BlockSpec((tn, tk), lambda i, j, k: (j, k)) if tb else pl.BlockSpec((tk, tn), lambda i, j, k: (k, j))
    o_spec = pl.BlockSpec((tm, tn), lambda i, j, k: (i, j))
    in_specs = [a_spec, b_spec]
    args = [a, b]
    if res is not None:
        in_specs.append(o_spec)
        args.append(res)
    if dep is not None:
        in_specs.append(pl.BlockSpec((8, LANES), lambda i, j, k: (0, 0)))
        args.append(dep)
    return pl.pallas_call(
        body, name=name, grid=(M // tm, N // tn, nk), in_specs=in_specs, out_specs=o_spec,
        out_shape=jax.ShapeDtypeStruct((M, N), out_dtype),
        scratch_shapes=[pltpu.VMEM((tm, tn), F32)] if nk > 1 else [],
        compiler_params=_cparams(("parallel", "parallel", "arbitrary")),
    )(*args)


MXU_COLS = 256


def _col_chunks(n):
    return [(lo, min(lo + MXU_COLS, n)) for lo in range(0, n, MXU_COLS)]


def ffn_up_act(h, wu4, *, name):
    T, D = h.shape
    cq = wu4.shape[2]
    Fh = 2 * cq
    tm = _pick(T, (512, 256, 128))

    def body(h_ref, wg_ref, wu_ref, g_ref, u_ref, a_ref):
        hv = h_ref[...]
        for lo, hi in _col_chunks(cq):
            gv = jnp.dot(hv, wg_ref[0, :, lo:hi], preferred_element_type=F32)
            uv = jnp.dot(hv, wu_ref[0, :, lo:hi], preferred_element_type=F32)
            g_ref[:, lo:hi] = gv.astype(BF16)
            u_ref[:, lo:hi] = uv.astype(BF16)
            a_ref[:, lo:hi] = (gv * _sigmoid(gv) * uv).astype(BF16)

    tile = pl.BlockSpec((tm, cq), lambda j, i: (i, j))
    sh = jax.ShapeDtypeStruct((T, Fh), BF16)
    return pl.pallas_call(
        body, name=name, grid=(2, T // tm),
        in_specs=[pl.BlockSpec((tm, D), lambda j, i: (i, 0)), pl.BlockSpec((1, D, cq), lambda j, i: (j, 0, 0)),
                  pl.BlockSpec((1, D, cq), lambda j, i: (2 + j, 0, 0))],
        out_specs=[tile, tile, tile], out_shape=[sh, sh, sh],
        compiler_params=_cparams(("parallel", "parallel")),
    )(h, wu4, wu4)


def ffn_down_dx_act(dy, wd, g, u, *, dep=None, name):
    T, D = dy.shape
    Fh = wd.shape[0]
    cq = Fh // 2
    tm = _pick(T, (512, 256, 128))

    def body(dy_ref, wd_ref, g_ref, u_ref, *rest):
        dg_ref, du_ref = rest[-2:]
        dyv = dy_ref[...].astype(BF16)
        for lo, hi in _col_chunks(cq):
            da = 0.5 * lax.dot_general(dyv, wd_ref[lo:hi, :], _NT, preferred_element_type=F32)
            gv = g_ref[:, lo:hi].astype(F32)
            uv = u_ref[:, lo:hi].astype(F32)
            s = _sigmoid(gv)
            dg_ref[:, lo:hi] = (da * uv * (s * (1.0 + gv * (1.0 - s)))).astype(BF16)
            du_ref[:, lo:hi] = (da * (gv * s)).astype(BF16)

    tile = pl.BlockSpec((tm, cq), lambda j, i: (i, j))
    sh = jax.ShapeDtypeStruct((T, Fh), BF16)
    in_specs = [pl.BlockSpec((tm, D), lambda j, i: (i, 0)), pl.BlockSpec((cq, D), lambda j, i: (j, 0)), tile, tile]
    args = [dy, wd, g, u]
    if dep is not None:
        in_specs.append(pl.BlockSpec((8, LANES), lambda j, i: (0, 0)))
        args.append(dep)
    return pl.pallas_call(
        body, name=name, grid=(2, T // tm), in_specs=in_specs, out_specs=[tile, tile], out_shape=[sh, sh],
        compiler_params=_cparams(("parallel", "parallel")),
    )(*args)


def ffn_up_dw(h, dg, du, *, name):
    T, D = h.shape
    cq = dg.shape[1] // 2
    tk = _pick(T, (1024, 512, 256, 128))
    nk = T // tk

    def body(h_ref, dg_ref, du_ref, o_ref, acc):
        p = pl.program_id(0)
        k = pl.program_id(1)

        @pl.when(k == 0)
        def _():
            acc[...] = jnp.zeros_like(acc)

        @pl.when(p < 2)
        def _():
            acc[...] += lax.dot_general(h_ref[...], dg_ref[...], _TN, preferred_element_type=F32)

        @pl.when(p >= 2)
        def _():
            acc[...] += lax.dot_general(h_ref[...], du_ref[...], _TN, preferred_element_type=F32)

        @pl.when(k == nk - 1)
        def _():
            o_ref[0] = acc[...].astype(BF16)

    return pl.pallas_call(
        body, name=name, grid=(4, nk),
        in_specs=[pl.BlockSpec((tk, D), lambda p, k: (k, 0)),
                  pl.BlockSpec((tk, cq), lambda p, k: (jnp.where(p < 2, k, 0), jnp.minimum(p, 1))),
                  pl.BlockSpec((tk, cq), lambda p, k: (jnp.where(p < 2, 0, k), jnp.maximum(p - 2, 0)))],
        out_specs=pl.BlockSpec((1, D, cq), lambda p, k: (p, 0, 0)),
        out_shape=jax.ShapeDtypeStruct((4, D, cq), BF16),
        scratch_shapes=[pltpu.VMEM((D, cq), F32)],
        compiler_params=_cparams(("parallel", "arbitrary")),
    )(h, dg, du)


def ffn_up_dx_norm(dg, du, wu4, x, g, dy, *, name):
    T = dg.shape[0]
    _, D, cq = wu4.shape
    tm = _pick(T, (512, 256, 128))

    def body(dg_ref, du_ref, wg_ref, wu_ref, x_ref, g_ref, dy_ref, dx_ref, dgain_ref, acc):
        i = pl.program_id(0)
        k = pl.program_id(1)
        part = lax.dot_general(dg_ref[...], wg_ref[0], _NT, preferred_element_type=F32)
        part = part + lax.dot_general(du_ref[...], wu_ref[0], _NT, preferred_element_type=F32)

        @pl.when(jnp.logical_and(i == 0, k == 0))
        def _():
            dgain_ref[...] = jnp.zeros_like(dgain_ref)

        @pl.when(k == 0)
        def _():
            acc[...] = part

        @pl.when(k == 1)
        def _():
            dx, dgain = _rms_bwd_math(x_ref[...], g_ref[...], acc[...] + part)
            dx_ref[...] = dy_ref[...] + dx
            dgain_ref[...] += dgain

    tile = pl.BlockSpec((tm, cq), lambda i, k: (i, k))
    row = pl.BlockSpec((tm, D), lambda i, k: (i, 0))
    vec = pl.BlockSpec((1, D), lambda i, k: (0, 0))
    return pl.pallas_call(
        body, name=name, grid=(T // tm, 2),
        in_specs=[tile, tile, pl.BlockSpec((1, D, cq), lambda i, k: (k, 0, 0)),
                  pl.BlockSpec((1, D, cq), lambda i, k: (2 + k, 0, 0)), row, vec, row],
        out_specs=[row, vec],
        out_shape=[jax.ShapeDtypeStruct((T, D), F32), jax.ShapeDtypeStruct((1, D), F32)],
        scratch_shapes=[pltpu.VMEM((tm, D), F32)],
        compiler_params=_cparams(("arbitrary", "arbitrary")),
    )(dg, du, wu4, wu4, x, g.reshape(1, D), dy)


def mix_in(h, w_lat, w_pool, w_gate, *, name):
    T, D = h.shape
    tm = _pick(T, (512, 256, 128))

    def body(h_ref, wl_ref, wp_ref, wg_ref, lat_ref, xp_ref, gl_ref):
        hv = h_ref[...]
        lat_ref[...] = jnp.dot(hv, wl_ref[...], preferred_element_type=F32)
        xp_ref[...] = jnp.dot(hv, wp_ref[...], preferred_element_type=F32)
        gl_ref[...] = jnp.dot(hv, wg_ref[...], preferred_element_type=F32).astype(BF16)

    whole = lambda a: pl.BlockSpec(a.shape, lambda i: (0, 0))
    out = lambda a: pl.BlockSpec((tm, a.shape[1]), lambda i: (i, 0))
    return pl.pallas_call(
        body, name=name, grid=(T // tm,),
        in_specs=[pl.BlockSpec((tm, D), lambda i: (i, 0)), whole(w_lat), whole(w_pool), whole(w_gate)],
        out_specs=[out(w_lat), out(w_pool), out(w_gate)],
        out_shape=[jax.ShapeDtypeStruct((T, w_lat.shape[1]), F32), jax.ShapeDtypeStruct((T, w_pool.shape[1]), F32),
                   jax.ShapeDtypeStruct((T, w_gate.shape[1]), BF16)],
        compiler_params=_cparams(("parallel",)),
    )(h, w_lat, w_pool, w_gate)


def mix_in_dx_norm(dlat, dxp, dgl, w_lat, w_pool, w_gate, x, g, dy, *, name):
    T, D = x.shape
    tm = _pick(T, (512, 256, 128))

    def body(dlat_ref, dxp_ref, dgl_ref, wl_ref, wp_ref, wg_ref, x_ref, g_ref, dy_ref, dx_ref, dgain_ref):
        @pl.when(pl.program_id(0) == 0)
        def _():
            dgain_ref[...] = jnp.zeros_like(dgain_ref)

        dh = lax.dot_general(dlat_ref[...], wl_ref[...], _NT, preferred_element_type=F32)
        dh = dh + lax.dot_general(dxp_ref[...], wp_ref[...], _NT, preferred_element_type=F32)
        dh = dh + lax.dot_general(dgl_ref[...], wg_ref[...], _NT, preferred_element_type=F32)
        dx, dgain = _rms_bwd_math(x_ref[...], g_ref[...], dh)
        dx_ref[...] = dy_ref[...] + dx
        dgain_ref[...] += dgain

    row = lambda a: pl.BlockSpec((tm, a.shape[1]), lambda i: (i, 0))
    whole = lambda a: pl.BlockSpec(a.shape, lambda i: (0, 0))
    vec = pl.BlockSpec((1, D), lambda i: (0, 0))
    return pl.pallas_call(
        body, name=name, grid=(T // tm,),
        in_specs=[row(dlat), row(dxp), row(dgl), whole(w_lat), whole(w_pool), whole(w_gate), row(x), vec, row(dy)],
        out_specs=[row(x), vec],
        out_shape=[jax.ShapeDtypeStruct((T, D), F32), jax.ShapeDtypeStruct((1, D), F32)],
        compiler_params=_cparams(("arbitrary",)),
    )(dlat, dxp, dgl, w_lat, w_pool, w_gate, x, g.reshape(1, D), dy)


def _rows(T, width_bytes):
    cap = max(8, (2 * 1024 * 1024) // width_bytes)
    return _pick(T, tuple(c for c in (1024, 512, 256, 128, 64, 32, 16) if c <= cap))


def rms_fwd(x, g, *, name, dep=None):
    T, D = x.shape
    tm = _rows(T, D * 4)

    def body(x_ref, g_ref, *rest):
        xv = x_ref[...]
        r = lax.rsqrt(jnp.mean(xv * xv, axis=-1, keepdims=True) + NORM_EPS)
        rest[-1][...] = (xv * r * g_ref[...]).astype(BF16)

    in_specs = [pl.BlockSpec((tm, D), lambda i: (i, 0)), pl.BlockSpec((1, D), lambda i: (0, 0))]
    args = [x, g.reshape(1, D)]
    if dep is not None:
        in_specs.append(pl.BlockSpec((8, LANES), lambda i: (0, 0)))
        args.append(dep)
    return pl.pallas_call(
        body, name=name, grid=(T // tm,), in_specs=in_specs,
        out_specs=pl.BlockSpec((tm, D), lambda i: (i, 0)),
        out_shape=jax.ShapeDtypeStruct((T, D), BF16),
        compiler_params=_cparams(("parallel",)),
    )(*args)


def _rms_bwd_math(xv, gv, dh):
    r = lax.rsqrt(jnp.mean(xv * xv, axis=-1, keepdims=True) + NORM_EPS)
    xn = xv * r
    dg = jnp.sum(dh * xn, axis=0, keepdims=True)
    dxn = dh * gv
    dx = r * (dxn - xn * jnp.mean(dxn * xn, axis=-1, keepdims=True))
    return dx, dg


def _rope(xv, cv, sv):
    half = QK_ROPE_DIM // 2
    lane = lax.broadcasted_iota(jnp.int32, xv.shape, 1)
    swapped = jnp.where(lane < half, pltpu.roll(xv, LANES - half, 1), pltpu.roll(xv, half, 1))
    return xv * cv + swapped * sv


def _rope_t(dy, cv, sv):
    half = QK_ROPE_DIM // 2
    ds = dy * sv
    lane = lax.broadcasted_iota(jnp.int32, dy.shape, 1)
    swapped = jnp.where(lane < half, pltpu.roll(ds, LANES - half, 1), pltpu.roll(ds, half, 1))
    return dy * cv + swapped


def lat_fwd(lat, qn_w, kvn_w, cs, sn, *, name):
    T = lat.shape[0]
    tm = _rows(T, LAT_DIM * 4)
    kv0 = Q_LORA_RANK
    kr0 = Q_LORA_RANK + KV_LORA_RANK

    def body(lat_ref, qw_ref, kw_ref, c_ref, s_ref, qn_ref, kvn_ref, kr_ref):
        ql = lat_ref[:, :kv0]
        r = lax.rsqrt(jnp.mean(ql * ql, axis=-1, keepdims=True) + NORM_EPS)
        qn_ref[...] = (ql * r * qw_ref[...]).astype(BF16)
        kl = lat_ref[:, kv0:kr0]
        r = lax.rsqrt(jnp.mean(kl * kl, axis=-1, keepdims=True) + NORM_EPS)
        kvn_ref[...] = (kl * r * kw_ref[...]).astype(BF16)
        kr_ref[...] = _rope(lat_ref[:, kr0:], c_ref[...], s_ref[...]).astype(BF16)

    return pl.pallas_call(
        body, name=name, grid=(T // tm,),
        in_specs=[pl.BlockSpec((tm, LAT_DIM), lambda i: (i, 0)),
                  pl.BlockSpec((1, Q_LORA_RANK), lambda i: (0, 0)),
                  pl.BlockSpec((1, KV_LORA_RANK), lambda i: (0, 0)),
                  pl.BlockSpec((tm, LANES), lambda i: (i, 0)), pl.BlockSpec((tm, LANES), lambda i: (i, 0))],
        out_specs=[pl.BlockSpec((tm, Q_LORA_RANK), lambda i: (i, 0)),
                   pl.BlockSpec((tm, KV_LORA_RANK), lambda i: (i, 0)),
                   pl.BlockSpec((tm, LANES), lambda i: (i, 0))],
        out_shape=[jax.ShapeDtypeStruct((T, Q_LORA_RANK), BF16), jax.ShapeDtypeStruct((T, KV_LORA_RANK), BF16),
                   jax.ShapeDtypeStruct((T, LANES), BF16)],
        compiler_params=_cparams(("parallel",)),
    )(lat, qn_w.reshape(1, -1), kvn_w.reshape(1, -1), cs, sn)


def lat_bwd(lat, qn_w, kvn_w, dqn, dkvn, dkr, cs, sn, *, name):
    T = lat.shape[0]
    tm = _rows(T, LAT_DIM * 4)
    kv0 = Q_LORA_RANK
    kr0 = Q_LORA_RANK + KV_LORA_RANK

    def body(lat_ref, qw_ref, kw_ref, dqn_ref, dkvn_ref, dkr_ref, c_ref, s_ref, dlat_ref, dqw_ref, dkw_ref):
        @pl.when(pl.program_id(0) == 0)
        def _():
            dqw_ref[...] = jnp.zeros_like(dqw_ref)
            dkw_ref[...] = jnp.zeros_like(dkw_ref)

        dx, dg = _rms_bwd_math(lat_ref[:, :kv0], qw_ref[...], dqn_ref[...])
        dlat_ref[:, :kv0] = dx.astype(BF16)
        dqw_ref[...] += dg
        dx, dg = _rms_bwd_math(lat_ref[:, kv0:kr0], kw_ref[...], dkvn_ref[...])
        dlat_ref[:, kv0:kr0] = dx.astype(BF16)
        dkw_ref[...] += dg
        dlat_ref[:, kr0:] = _rope_t(dkr_ref[...], c_ref[...], s_ref[...]).astype(BF16)

    row = lambda w: pl.BlockSpec((tm, w), lambda i: (i, 0))
    vec = lambda w: pl.BlockSpec((1, w), lambda i: (0, 0))
    return pl.pallas_call(
        body, name=name, grid=(T // tm,),
        in_specs=[row(LAT_DIM), vec(Q_LORA_RANK), vec(KV_LORA_RANK), row(Q_LORA_RANK), row(KV_LORA_RANK),
                  row(LANES), row(LANES), row(LANES)],
        out_specs=[row(LAT_DIM), vec(Q_LORA_RANK), vec(KV_LORA_RANK)],
        out_shape=[jax.ShapeDtypeStruct((T, LAT_DIM), BF16), jax.ShapeDtypeStruct((1, Q_LORA_RANK), F32),
                   jax.ShapeDtypeStruct((1, KV_LORA_RANK), F32)],
        compiler_params=_cparams(("arbitrary",)),
    )(lat, qn_w.reshape(1, -1), kvn_w.reshape(1, -1), dqn, dkvn, dkr, cs, sn)


def attn_proj_gate(o, wap, gl, bg, ba, *, name):
    T, D2 = gl.shape
    D = D2 // 2
    tm = _pick(T, (512, 256, 128))

    def body(o_ref, w_ref, gl_ref, bg_ref, ba_ref, bb_ref, m_ref):
        bb = jnp.dot(o_ref[...], w_ref[...], preferred_element_type=F32)
        bb_ref[...] = bb.astype(BF16)
        ga = _sigmoid(gl_ref[:, :D].astype(F32) + bg_ref[:, :D])
        gb = _sigmoid(gl_ref[:, D:].astype(F32) + bg_ref[:, D:])
        m_ref[...] = (ga * ba_ref[...].astype(F32) + gb * bb).astype(BF16)

    row = lambda w: pl.BlockSpec((tm, w), lambda i: (i, 0))
    return pl.pallas_call(
        body, name=name, grid=(T // tm,),
        in_specs=[row(o.shape[1]), pl.BlockSpec(wap.shape, lambda i: (0, 0)), row(D2),
                  pl.BlockSpec((1, D2), lambda i: (0, 0)), row(D)],
        out_specs=[row(D), row(D)],
        out_shape=[jax.ShapeDtypeStruct((T, D), BF16), jax.ShapeDtypeStruct((T, D), BF16)],
        compiler_params=_cparams(("parallel",)),
    )(o, wap, gl, bg.reshape(1, D2), ba)


def out_dx_gate(dy, wo, gl, bg, ba, bb, *, name, dep=None):
    T, D2 = gl.shape
    D = D2 // 2
    tm = _pick(T, (512, 256, 128))

    def body(dy_ref, w_ref, gl_ref, bg_ref, ba_ref, bb_ref, *rest):
        dba_ref, dbb_ref, dgl_ref, dbg_ref = rest[-4:]

        @pl.when(pl.program_id(0) == 0)
        def _():
            dbg_ref[...] = jnp.zeros_like(dbg_ref)

        dmv = lax.dot_general(dy_ref[...].astype(BF16), w_ref[...], _NT, preferred_element_type=F32)
        ga = _sigmoid(gl_ref[:, :D].astype(F32) + bg_ref[:, :D])
        gb = _sigmoid(gl_ref[:, D:].astype(F32) + bg_ref[:, D:])
        dba_ref[...] = (dmv * ga).astype(BF16)
        dbb_ref[...] = (dmv * gb).astype(BF16)
        dla = dmv * ba_ref[...].astype(F32) * ga * (1.0 - ga)
        dlb = dmv * bb_ref[...].astype(F32) * gb * (1.0 - gb)
        dgl_ref[:, :D] = dla.astype(BF16)
        dgl_ref[:, D:] = dlb.astype(BF16)
        dbg_ref[:, :D] += jnp.sum(dla, axis=0, keepdims=True)
        dbg_ref[:, D:] += jnp.sum(dlb, axis=0, keepdims=True)

    row = lambda w: pl.BlockSpec((tm, w), lambda i: (i, 0))
    vec = pl.BlockSpec((1, D2), lambda i: (0, 0))
    return pl.pallas_call(
        body, name=name, grid=(T // tm,),
        in_specs=[row(D), pl.BlockSpec(wo.shape, lambda i: (0, 0)), row(D2), vec, row(D), row(D)]
        + ([] if dep is None else [pl.BlockSpec((8, LANES), lambda i: (0, 0))]),
        out_specs=[row(D), row(D), row(D2), vec],
        out_shape=[jax.ShapeDtypeStruct((T, D), BF16), jax.ShapeDtypeStruct((T, D), BF16),
                   jax.ShapeDtypeStruct((T, D2), BF16), jax.ShapeDtypeStruct((1, D2), F32)],
        compiler_params=_cparams(("arbitrary",)),
    )(dy, wo, gl, bg.reshape(1, D2), ba, bb, *([] if dep is None else [dep]))


def loss_head(x, gf, tgt, *, name):
    T, D = x.shape
    tm = _rows(T, D * 4)

    def body(x_ref, g_ref, t_ref, dx_ref, dg_ref, loss_ref):
        @pl.when(pl.program_id(0) == 0)
        def _():
            dg_ref[...] = jnp.zeros_like(dg_ref)
            loss_ref[...] = jnp.zeros_like(loss_ref)

        xv = x_ref[...]
        gv = g_ref[...]
        r = lax.rsqrt(jnp.mean(xv * xv, axis=-1, keepdims=True) + NORM_EPS)
        xn = xv * r
        err = xn * gv - t_ref[...]
        loss_ref[...] += 0.5 * jnp.sum(jnp.mean(err * err, axis=-1, keepdims=True))
        dy = err * (1.0 / D)
        dg_ref[...] += jnp.sum(dy * xn, axis=0, keepdims=True)
        dxn = dy * gv
        dx_ref[...] = r * (dxn - xn * jnp.mean(dxn * xn, axis=-1, keepdims=True))

    row = pl.BlockSpec((tm, D), lambda i: (i, 0))
    vec = pl.BlockSpec((1, D), lambda i: (0, 0))
    return pl.pallas_call(
        body, name=name, grid=(T // tm,), in_specs=[row, vec, row],
        out_specs=[row, vec, pl.BlockSpec((8, LANES), lambda i: (0, 0))],
        out_shape=[jax.ShapeDtypeStruct((T, D), F32), jax.ShapeDtypeStruct((1, D), F32),
                   jax.ShapeDtypeStruct((8, LANES), F32)],
        compiler_params=_cparams(("arbitrary",)),
    )(x, gf.reshape(1, D), tgt)


def _shift_rows(s, k, down):
    n = s.shape[0]
    t = lax.broadcasted_iota(jnp.int32, s.shape, 0)
    if down:
        return jnp.where(t >= k, pltpu.roll(s, k, 0), 0.0)
    return jnp.where(t < n - k, pltpu.roll(s, n - k, 0), 0.0)


def _window_sum(s, w, down):
    k = 1
    while k < w:
        s = s + _shift_rows(s, k, down)
        k *= 2
    return s


def _pool_count(shape, w):
    t = lax.broadcasted_iota(jnp.int32, shape, 0)
    return jnp.minimum(t + 1, w).astype(F32)


def pool_fwd(xp, maps, scale, B, *, name):
    T, P = xp.shape
    S = T // B
    G = POOL_GROUP_DIM

    def body(x_ref, m_ref, sc_ref, o_ref):
        for g, w in enumerate(POOL_WINDOWS):
            xg = x_ref[:, g * G:(g + 1) * G]
            pooled = _window_sum(xg, w, True) / _pool_count(xg.shape, w) - xg
            mixed = jnp.dot(pooled.astype(BF16), m_ref[g], preferred_element_type=F32)
            o_ref[:, g * G:(g + 1) * G] = (mixed * sc_ref[:, g * G:(g + 1) * G]).astype(BF16)

    return pl.pallas_call(
        body, name=name, grid=(B,),
        in_specs=[pl.BlockSpec((S, P), lambda b: (b, 0)), pl.BlockSpec((N_POOL_GROUPS, G, G), lambda b: (0, 0, 0)),
                  pl.BlockSpec((1, P), lambda b: (0, 0))],
        out_specs=pl.BlockSpec((S, P), lambda b: (b, 0)),
        out_shape=jax.ShapeDtypeStruct((T, P), BF16),
        compiler_params=_cparams(("parallel",)),
    )(xp, maps, scale.reshape(1, P))


def pool_bwd(xp, dmixed, maps, scale, B, *, name):
    T, P = xp.shape
    S = T // B
    G = POOL_GROUP_DIM

    def body(x_ref, dm_ref, m_ref, sc_ref, dx_ref, dmaps_ref, dsc_ref):
        @pl.when(pl.program_id(0) == 0)
        def _():
            dmaps_ref[...] = jnp.zeros_like(dmaps_ref)
            dsc_ref[...] = jnp.zeros_like(dsc_ref)

        for g, w in enumerate(POOL_WINDOWS):
            cols = slice(g * G, (g + 1) * G)
            xg = x_ref[:, cols]
            cnt = _pool_count(xg.shape, w)
            pooled = (_window_sum(xg, w, True) / cnt - xg).astype(BF16)
            mixed = jnp.dot(pooled, m_ref[g], preferred_element_type=F32)
            dmx = dm_ref[:, cols]
            dsc_ref[:, cols] += jnp.sum(dmx * mixed, axis=0, keepdims=True)
            dmp = (dmx * sc_ref[:, cols]).astype(BF16)
            dmaps_ref[g] += lax.dot_general(pooled, dmp, (((0,), (0,)), ((), ())), preferred_element_type=F32)
            dpooled = lax.dot_general(dmp, m_ref[g], (((1,), (1,)), ((), ())), preferred_element_type=F32)
            dx_ref[:, cols] = (_window_sum(dpooled / cnt, w, False) - dpooled).astype(BF16)

    blk = pl.BlockSpec((S, P), lambda b: (b, 0))
    mp = pl.BlockSpec((N_POOL_GROUPS, G, G), lambda b: (0, 0, 0))
    vec = pl.BlockSpec((1, P), lambda b: (0, 0))
    return pl.pallas_call(
        body, name=name, grid=(B,), in_specs=[blk, blk, mp, vec], out_specs=[blk, mp, vec],
        out_shape=[jax.ShapeDtypeStruct((T, P), BF16), jax.ShapeDtypeStruct((N_POOL_GROUPS, G, G), F32),
                   jax.ShapeDtypeStruct((1, P), F32)],
        compiler_params=_cparams(("arbitrary",)),
    )(xp, dmixed, maps, scale.reshape(1, P))


def _keys(kv_ref, kr_ref, rows):
    return jnp.concatenate([kv_ref[rows, :QK_NOPE_DIM], kr_ref[rows, :]], axis=1)


def _causal(s):
    row = lax.broadcasted_iota(jnp.int32, s.shape, 0)
    col = lax.broadcasted_iota(jnp.int32, s.shape, 1)
    return jnp.where(row >= col, s, NEG_INF)


def attn_fwd(q, kv, kr, cs, sn, B, *, name):
    T = q.shape[0]
    S = T // B
    blk = min(ATT_BLOCK_FWD, S)
    nb = S // blk
    H = N_HEADS
    scale = QK_DIM ** -0.5

    def body(q_ref, kv_ref, kr_ref, c_ref, s_ref, o_ref, lse_ref, qs_ref):
        qs_ref[:, :QK_NOPE_DIM] = (q_ref[:, :QK_NOPE_DIM] * scale).astype(BF16)
        qs_ref[:, QK_NOPE_DIM:] = _rope(q_ref[:, QK_NOPE_DIM:], c_ref[...] * scale, s_ref[...] * scale).astype(BF16)
        for qi in range(nb):
            rows = slice(qi * blk, (qi + 1) * blk)
            qb = qs_ref[rows, :]
            sd = _causal(lax.dot_general(qb, _keys(kv_ref, kr_ref, rows), _NT, preferred_element_type=F32))
            m = jnp.max(sd, axis=-1, keepdims=True)
            if qi > 0:
                prev = slice(0, qi * blk)
                sp = lax.dot_general(qb, _keys(kv_ref, kr_ref, prev), _NT, preferred_element_type=F32)
                m = jnp.maximum(m, jnp.max(sp, axis=-1, keepdims=True))
            pd = jnp.exp(sd - m)
            l = jnp.sum(pd, axis=-1, keepdims=True)
            acc = jnp.dot(pd.astype(BF16), kv_ref[rows, QK_NOPE_DIM:], preferred_element_type=F32)
            if qi > 0:
                pp = jnp.exp(sp - m)
                l = l + jnp.sum(pp, axis=-1, keepdims=True)
                acc = acc + jnp.dot(pp.astype(BF16), kv_ref[prev, QK_NOPE_DIM:], preferred_element_type=F32)
            o_ref[rows, :] = (acc / l).astype(BF16)
            lse_ref[0, rows, :] = m + jnp.log(l)

    head = pl.BlockSpec((S, HEAD_PAD), lambda b, h: (b, h))
    shared = pl.BlockSpec((S, LANES), lambda b, h: (b, 0))
    return pl.pallas_call(
        body, name=name, grid=(B, H),
        in_specs=[head, head, shared, shared, shared],
        out_specs=[pl.BlockSpec((S, V_HEAD_DIM), lambda b, h: (b, h)), pl.BlockSpec((1, S, 1), lambda b, h: (h, b, 0)),
                   head],
        out_shape=[jax.ShapeDtypeStruct((T, H * V_HEAD_DIM), BF16), jax.ShapeDtypeStruct((H, T, 1), F32),
                   jax.ShapeDtypeStruct((T, H * HEAD_PAD), BF16)],
        compiler_params=_cparams(("parallel", "parallel")),
    )(q, kv, kr, cs, sn)


def attn_bwd(q, kv, kr, o, do, lse, cs, sn, B, *, name):
    T = q.shape[0]
    S = T // B
    blk = min(ATT_BLOCK, S)
    nb = S // blk
    H = N_HEADS
    scale = QK_DIM ** -0.5

    def body(q_ref, kv_ref, kr_ref, o_ref, do_ref, lse_ref, c_ref, s_ref, dq_ref, dkv_ref, dkr_ref, dk_s, dv_s):
        dk_s[...] = jnp.zeros_like(dk_s)
        dv_s[...] = jnp.zeros_like(dv_s)

        @pl.when(pl.program_id(1) == 0)
        def _():
            dkr_ref[...] = jnp.zeros_like(dkr_ref)

        for qi in range(nb):
            rows = slice(qi * blk, (qi + 1) * blk)
            qb = q_ref[rows, :]
            dob = do_ref[rows, :]
            delta = jnp.sum(dob.astype(F32) * o_ref[rows, :].astype(F32), axis=-1, keepdims=True)
            lse_b = lse_ref[0, rows, :]

            def part(ks, diagonal):
                k = _keys(kv_ref, kr_ref, ks)
                s = lax.dot_general(qb, k, _NT, preferred_element_type=F32)
                if diagonal:
                    s = _causal(s)
                p = jnp.exp(s - lse_b)
                dp = lax.dot_general(dob, kv_ref[ks, QK_NOPE_DIM:], _NT, preferred_element_type=F32)
                ds = (p * (dp - delta)).astype(BF16)
                dv_s[ks, :] += lax.dot_general(p.astype(BF16), dob, _TN, preferred_element_type=F32)
                dk_s[ks, :] += lax.dot_general(ds, qb, _TN, preferred_element_type=F32)
                return jnp.dot(ds, k, preferred_element_type=F32)

            dq = part(rows, True)
            if qi > 0:
                dq = dq + part(slice(0, qi * blk), False)
            dq_ref[rows, :QK_NOPE_DIM] = (dq[:, :QK_NOPE_DIM] * scale).astype(BF16)
            dq_ref[rows, QK_NOPE_DIM:] = _rope_t(dq[:, QK_NOPE_DIM:], c_ref[rows, :] * scale,
                                                 s_ref[rows, :] * scale).astype(BF16)

        dkv_ref[:, :QK_NOPE_DIM] = dk_s[:, :QK_NOPE_DIM].astype(BF16)
        dkv_ref[:, QK_NOPE_DIM:] = dv_s[...].astype(BF16)
        dkr_ref[...] += dk_s[:, QK_NOPE_DIM:]

    head = lambda w: pl.BlockSpec((S, w), lambda b, h: (b, h))
    shared = pl.BlockSpec((S, LANES), lambda b, h: (b, 0))
    return pl.pallas_call(
        body, name=name, grid=(B, H),
        in_specs=[head(HEAD_PAD), head(HEAD_PAD), shared, head(V_HEAD_DIM), head(V_HEAD_DIM),
                  pl.BlockSpec((1, S, 1), lambda b, h: (h, b, 0)), shared, shared],
        out_specs=[head(HEAD_PAD), head(HEAD_PAD), shared],
        out_shape=[jax.ShapeDtypeStruct((T, H * HEAD_PAD), BF16), jax.ShapeDtypeStruct((T, H * HEAD_PAD), BF16),
                   jax.ShapeDtypeStruct((T, LANES), F32)],
        scratch_shapes=[pltpu.VMEM((S, HEAD_PAD), F32), pltpu.VMEM((S, V_HEAD_DIM), F32)],
        compiler_params=_cparams(("parallel", "arbitrary")),
    )(q, kv, kr, o, do, lse, cs, sn)


def adamw(w, g, m, v, *, name, dep=None, copy_g=False):
    R, C = w.shape
    cap = max(8, (1024 * 1024) // (C * 4))
    tr = _pick(R, tuple(c for c in (1024, 512, 256, 128, 64, 32, 16, 8) if c <= cap))
    c1 = 1.0 - ADAM_B1 ** ADAM_STEP
    c2 = 1.0 - ADAM_B2 ** ADAM_STEP
    nout = 4 if copy_g else 3

    def body(w_ref, g_ref, m_ref, v_ref, *rest):
        d_ref, nm_ref, nv_ref = rest[-3:]
        gv = g_ref[...]
        if copy_g:
            rest[-4][...] = gv
        mv = ADAM_B1 * m_ref[...] + (1.0 - ADAM_B1) * gv
        vv = ADAM_B2 * v_ref[...] + (1.0 - ADAM_B2) * (gv * gv)
        nm_ref[...] = mv
        nv_ref[...] = vv
        d_ref[...] = -ADAM_LR * ((mv / c1) / (jnp.sqrt(vv / c2) + ADAM_EPS) + ADAM_WD * w_ref[...])

    blk = pl.BlockSpec((tr, C), lambda i: (i, 0))
    sh = jax.ShapeDtypeStruct((R, C), F32)
    extra = [] if dep is None else [dep]
    return pl.pallas_call(
        body, name=name, grid=(R // tr,), in_specs=[blk] * 4 + [ANY] * len(extra), out_specs=[blk] * nout,
        out_shape=[sh] * nout, compiler_params=_cparams(("parallel",)),
    )(w, g, m, v, *extra)


ANY = pl.BlockSpec(memory_space=pl.ANY)


def _place():
    x, y, c = lax.axis_index("x"), lax.axis_index("y"), lax.axis_index("c")
    others = [(1 - x, y), (x, 1 - y), (1 - x, 1 - y)]
    return x, y, c, others


def _remote(src, dst, ssem, rsem, dev):
    return pltpu.make_async_remote_copy(src_ref=src, dst_ref=dst, send_sem=ssem, recv_sem=rsem,
                                        device_id=dev, device_id_type=MESH)


def _half(ref_rows, c):
    hr = ref_rows // 2
    return pl.ds(pl.multiple_of(c * hr, 16), hr)


HBM = pl.BlockSpec(memory_space=pltpu.HBM)
SEMS = pl.BlockSpec(memory_space=pltpu.SEMAPHORE)
EFFECT = pltpu.SideEffectType.DATAFLOW_SIDE_EFFECTING


def exchange_begin(name, srcs, land_shapes, plan, ncopies, after=None):
    ns, nl = len(srcs), len(land_shapes)
    nin = ns + nl + (0 if after is None else 1)

    def body(*refs):
        ssem, rsem = refs[nin], refs[nin + 1]
        for k, (s, d, dev) in enumerate(plan(refs[:ns], refs[ns:ns + nl])):
            _remote(s, d, ssem.at[k], rsem.at[k], dev).start()
        refs[-1][...] = jnp.zeros_like(refs[-1])

    bufs = [pltpu.HBM(s.shape, s.dtype) for s in srcs] + [pltpu.HBM(s.shape, s.dtype) for s in land_shapes]
    args = [pltpu.with_memory_space_constraint(s, pltpu.HBM) for s in srcs]
    args += [pltpu.with_memory_space_constraint(lax.empty(s.shape, s.dtype), pltpu.HBM) for s in land_shapes]
    if after is not None:
        args.append(after)
    out = pl.pallas_call(
        body, name=name,
        out_shape=(pltpu.SemaphoreType.DMA((ncopies,)), pltpu.SemaphoreType.DMA((ncopies,)), *bufs,
                   jax.ShapeDtypeStruct((8, LANES), F32)),
        in_specs=[HBM] * (ns + nl) + ([] if after is None else [ANY]),
        out_specs=(SEMS, SEMS, *([HBM] * (ns + nl)), pl.BlockSpec(memory_space=pltpu.VMEM)),
        input_output_aliases={i: 2 + i for i in range(ns + nl)},
        compiler_params=pltpu.CompilerParams(has_side_effects=EFFECT),
    )(*args)
    return (out[0], out[1], out[2:2 + ns], out[2 + ns:2 + ns + nl]), out[-1]


def exchange_end(name, handle, plan, after):
    ssem, rsem, srcs, lands = handle
    ns, nl = len(srcs), len(lands)

    def body(*refs):
        ssem_ref, rsem_ref = refs[ns + nl], refs[ns + nl + 1]
        for k, (s, d, dev) in enumerate(plan(refs[:ns], refs[ns:ns + nl])):
            cp = _remote(s, d, ssem_ref.at[k], rsem_ref.at[k], dev)
            cp.wait_send()
            cp.wait_recv()

    out = pl.pallas_call(
        body, name=name,
        out_shape=tuple(pltpu.HBM(s.shape, s.dtype) for s in (*srcs, *lands)),
        in_specs=[HBM] * (ns + nl) + [SEMS, SEMS, ANY], out_specs=tuple([HBM] * (ns + nl)),
        input_output_aliases={i: i for i in range(ns + nl)},
        compiler_params=pltpu.CompilerParams(has_side_effects=EFFECT),
    )(*srcs, *lands, ssem, rsem, after)
    return list(out[:ns]), list(out[ns:])


def ag_plan(src_refs, land_refs):
    x, y, c, others = _place()
    plan = []
    for s, d in zip(src_refs, land_refs):
        mine = _half(s.shape[0], c)
        for ox, oy in others:
            plan.append((s.at[mine, :], d.at[2 * x + y, mine, :], (ox, oy, c)))
    return plan


def ag_forward_plan(src_refs, land_refs):
    x, y, c, others = _place()
    plan = []
    for s in src_refs:
        mine = _half(s.shape[1], c)
        for ox, oy in others:
            blk = s.at[2 * ox + oy, mine, :]
            plan.append((blk, blk, (x, y, 1 - c)))
    return plan


def rs_swap_plan(src_refs, land_refs):
    x, y, c, _ = _place()
    return [(s.at[:, _half(s.shape[1], 1 - c), :], d, (x, y, 1 - c)) for s, d in zip(src_refs, land_refs)]


def ag_forward(lands, *, name):
    n = len(lands)

    def body(*refs):
        ins, outs = refs[:n], refs[n:2 * n]
        ssem, rsem = refs[2 * n:]
        x, y, c, others = _place()
        sent = []
        for i in range(n):
            mine = _half(ins[i].shape[1], c)
            for j, (ox, oy) in enumerate(others):
                cp = _remote(ins[i].at[2 * ox + oy, mine, :], outs[i].at[2 * ox + oy, mine, :], ssem.at[3 * i + j],
                             rsem.at[3 * i + j], (x, y, 1 - c))
                cp.start()
                sent.append(cp)
        for cp in sent:
            cp.wait()

    return pl.pallas_call(
        body, name=name, in_specs=[ANY] * n, out_specs=[ANY] * n,
        out_shape=[jax.ShapeDtypeStruct(a.shape, a.dtype) for a in lands],
        input_output_aliases={i: i for i in range(n)},
        scratch_shapes=[pltpu.SemaphoreType.DMA((3 * n,)), pltpu.SemaphoreType.DMA((3 * n,))],
        compiler_params=pltpu.CompilerParams(has_side_effects=True),
    )(*lands)


def place_own(lands, own, chip, *, name):
    n = len(lands)
    steps = 4

    def body(chip_ref, *refs):
        for i in range(n):
            refs[2 * n + i][0] = refs[i][...]

    in_specs = [pl.BlockSpec((o.shape[0] // steps, o.shape[1]), lambda t, q: (t, 0)) for o in own] + [ANY] * n
    out_specs = [pl.BlockSpec((1, o.shape[0] // steps, o.shape[1]), lambda t, q: (q[0], t, 0)) for o in own]
    return pl.pallas_call(
        body, name=name,
        grid_spec=pltpu.PrefetchScalarGridSpec(num_scalar_prefetch=1, grid=(steps,), in_specs=in_specs,
                                               out_specs=out_specs),
        out_shape=[jax.ShapeDtypeStruct(a.shape, a.dtype) for a in lands],
        input_output_aliases={1 + n + i: i for i in range(n)},
        compiler_params=_cparams(("parallel",)),
    )(chip, *own, *lands)


def rs_swap_halves(grads, *, name, after=None):
    n = len(grads)
    extra = [] if after is None else [after]
    nin = n + len(extra)

    def body(*refs):
        ins, outs = refs[:n], refs[nin:nin + n]
        ssem, rsem = refs[nin + n:]
        x, y, c, _ = _place()
        cps = []
        for i in range(n):
            theirs = _half(ins[i].shape[1], 1 - c)
            cp = _remote(ins[i].at[:, theirs, :], outs[i], ssem.at[i], rsem.at[i], (x, y, 1 - c))
            cp.start()
            cps.append(cp)
        for cp in cps:
            cp.wait()

    return pl.pallas_call(
        body, name=name, in_specs=[ANY] * nin, out_specs=[ANY] * n,
        out_shape=[jax.ShapeDtypeStruct((4, g.shape[1] // 2, g.shape[2]), g.dtype) for g in grads],
        scratch_shapes=[pltpu.SemaphoreType.DMA((n,)), pltpu.SemaphoreType.DMA((n,))],
        compiler_params=pltpu.CompilerParams(has_side_effects=True),
    )(*grads, *extra)


def rs_chip_sum(g, r1, core, *, name):
    _, r, cdim = g.shape
    hr = r // 2

    def body(c_ref, g_ref, r1_ref, o_ref):
        o_ref[...] = (g_ref[...].astype(F32) + r1_ref[...].astype(F32)).astype(BF16)

    return pl.pallas_call(
        body, name=name,
        grid_spec=pltpu.PrefetchScalarGridSpec(
            num_scalar_prefetch=1, grid=(4,),
            in_specs=[pl.BlockSpec((1, hr, cdim), lambda qq, c_ref: (qq, c_ref[0], 0)),
                      pl.BlockSpec((1, hr, cdim), lambda qq, c_ref: (qq, 0, 0))],
            out_specs=pl.BlockSpec((1, hr, cdim), lambda qq, c_ref: (qq, 0, 0))),
        out_shape=jax.ShapeDtypeStruct((4, hr, cdim), BF16),
        compiler_params=_cparams(("parallel",)),
    )(core, g, r1)


def rs_plan(src_refs, land_refs):
    x, y, c, others = _place()
    plan = []
    for s, d in zip(src_refs, land_refs):
        for j, (ox, oy) in enumerate(others):
            plan.append((s.at[2 * ox + oy], d.at[j], (ox, oy, c)))
    return plan


def rs_final_sum(g, r1, r2, place, acc, l, *, name):
    _, r, cdim = g.shape
    hr = r // 2
    ch = hr // 2

    def body(p_ref, g_ref, r1_ref, a_ref, b_ref, d_ref, acc_in, o_ref):
        s = g_ref[...].astype(F32) + r1_ref[...].astype(F32)
        s = s + a_ref[...].astype(F32)
        s = s + b_ref[...].astype(F32)
        o_ref[...] = s + d_ref[...].astype(F32)

    other = lambda j: pl.BlockSpec((1, ch, cdim), lambda t, p_ref: (j, t, 0))
    return pl.pallas_call(
        body, name=name,
        grid_spec=pltpu.PrefetchScalarGridSpec(
            num_scalar_prefetch=1, grid=(2,),
            in_specs=[pl.BlockSpec((1, ch, cdim), lambda t, p_ref: (p_ref[0], 2 * p_ref[1] + t, 0)),
                      pl.BlockSpec((1, ch, cdim), lambda t, p_ref: (p_ref[0], t, 0)),
                      other(0), other(1), other(2), ANY],
            out_specs=pl.BlockSpec((1, ch, cdim), lambda t, p_ref: (l, 2 * p_ref[1] + t, 0))),
        out_shape=jax.ShapeDtypeStruct(acc.shape, F32),
        input_output_aliases={6: 0},
        compiler_params=_cparams(("parallel",)),
    )(place, g, r1, r2, r2, r2, acc)


def rs_join_halves(grads, *, name):
    n = len(grads)

    def body(*refs):
        ins, outs = refs[:n], refs[n:2 * n]
        ssem, rsem = refs[2 * n:]
        x, y, c, _ = _place()
        cps = []
        for i in range(n):
            mine = _half(ins[i].shape[1], c)
            cp = _remote(ins[i].at[:, mine, :], outs[i].at[:, mine, :], ssem.at[i], rsem.at[i], (x, y, 1 - c))
            cp.start()
            cps.append(cp)
        for cp in cps:
            cp.wait()

    return pl.pallas_call(
        body, name=name, in_specs=[ANY] * n, out_specs=[ANY] * n,
        out_shape=[jax.ShapeDtypeStruct(g.shape, g.dtype) for g in grads],
        input_output_aliases={i: i for i in range(n)},
        scratch_shapes=[pltpu.SemaphoreType.DMA((n,)), pltpu.SemaphoreType.DMA((n,))],
        compiler_params=pltpu.CompilerParams(has_side_effects=True),
    )(*grads)


def all_reduce_small(vs):
    n = len(vs)

    def body(*refs):
        v_refs, o_refs = refs[:n], refs[n:2 * n]
        sibs, sums, gots = refs[2 * n:3 * n], refs[3 * n:4 * n], refs[4 * n:5 * n]
        ssem, rsem = refs[5 * n:]
        x, y, c, others = _place()
        q = 2 * x + y
        sib = (x, y, 1 - c)

        def wait_all(cps):
            for cp in cps:
                cp.wait()

        cps = []
        for i in range(n):
            hr = vs[i].shape[0] // 2
            theirs = pl.ds(pl.multiple_of((1 - c) * hr, 8), hr)
            cps.append(_remote(v_refs[i].at[theirs, :], sibs[i], ssem.at[8 * i], rsem.at[8 * i], sib))
            cps[-1].start()
        wait_all(cps)
        cps = []
        for i in range(n):
            hr, rs = vs[i].shape[0] // 2, vs[i].shape[0] // 8
            sums[i][...] = v_refs[i][pl.ds(pl.multiple_of(c * hr, 8), hr), :] + sibs[i][...]
            for j, (ox, oy) in enumerate(others):
                piece = pl.ds(pl.multiple_of((2 * ox + oy) * rs, 8), rs)
                cps.append(_remote(sums[i].at[piece, :], gots[i].at[j], ssem.at[8 * i + 1 + j], rsem.at[8 * i + 1 + j],
                                   (ox, oy, c)))
                cps[-1].start()
        wait_all(cps)
        cps = []
        for i in range(n):
            hr, rs = vs[i].shape[0] // 2, vs[i].shape[0] // 8
            acc = sums[i][pl.ds(pl.multiple_of(q * rs, 8), rs), :]
            for j in range(3):
                acc = acc + gots[i][j]
            mine = o_refs[i].at[pl.ds(pl.multiple_of(c * hr + q * rs, 8), rs), :]
            o_refs[i][pl.ds(pl.multiple_of(c * hr + q * rs, 8), rs), :] = acc
            for j, (ox, oy) in enumerate(others):
                cps.append(_remote(mine, mine, ssem.at[8 * i + 4 + j], rsem.at[8 * i + 4 + j], (ox, oy, c)))
                cps[-1].start()
        wait_all(cps)
        cps = []
        for i in range(n):
            hr = vs[i].shape[0] // 2
            half = o_refs[i].at[pl.ds(pl.multiple_of(c * hr, 8), hr), :]
            cps.append(_remote(half, half, ssem.at[8 * i + 7], rsem.at[8 * i + 7], sib))
            cps[-1].start()
        wait_all(cps)

    vm = pl.BlockSpec(memory_space=pltpu.VMEM)
    scratch = [pltpu.VMEM((v.shape[0] // 2, LANES), F32) for v in vs] * 2
    scratch += [pltpu.VMEM((3, v.shape[0] // 8, LANES), F32) for v in vs]
    return pl.pallas_call(
        body, name="all_reduce_small", in_specs=[vm] * n, out_specs=[vm] * n,
        out_shape=[jax.ShapeDtypeStruct(v.shape, F32) for v in vs],
        scratch_shapes=scratch + [pltpu.SemaphoreType.DMA((8 * n,))] * 2,
        compiler_params=pltpu.CompilerParams(vmem_limit_bytes=VMEM_LIMIT, has_side_effects=True),
    )(*vs)


def _to_stacked(name, full):
    R, C = full.shape
    if name in ROW_SHARDED:
        return full.reshape(4, R // 4, C)
    return jnp.transpose(full.reshape(R, 4, C // 4), (1, 0, 2))


def _from_stacked(name, st):
    _, r, c = st.shape
    if name in ROW_SHARDED:
        return st.reshape(4 * r, c)
    return jnp.transpose(st, (1, 0, 2)).reshape(r, 4 * c)


UP_PIECES = ("ffn1_up", "ffn2_up")


def _layer_weights(lands):
    w = {n: lands[n] if n in UP_PIECES else _from_stacked(n, lands[n]) for n in lands}
    if "w_in" not in w:
        return w
    win = w.pop("w_in")
    D = win.shape[0]
    p0, p1, p2, p3 = POOL_DIM, POOL_DIM + Q_LORA_RANK, POOL_DIM + Q_LORA_RANK + KV_LORA_RANK, \
        POOL_DIM + Q_LORA_RANK + KV_LORA_RANK + QK_ROPE_DIM
    w["w_pool"] = win[:, :p0]
    w["w_lat"] = jnp.concatenate([win[:, p0:p3], jnp.zeros((D, LAT_DIM - (p3 - p0)), win.dtype)], axis=1)
    w["w_gate"] = win[:, p3:]
    uq = w["w_uq"].reshape(Q_LORA_RANK, N_HEADS, QK_DIM)
    w["w_uq"] = jnp.concatenate([uq, jnp.zeros((Q_LORA_RANK, N_HEADS, HEAD_PAD - QK_DIM), uq.dtype)],
                                axis=2).reshape(Q_LORA_RANK, N_HEADS * HEAD_PAD)
    return w


def _layer_grads_stacked(dw):
    dw = dict(dw)
    if "w_lat" in dw:
        lat = dw.pop("w_lat")
        dw["w_in"] = jnp.concatenate([dw.pop("w_pool"), lat[:, :Q_LORA_RANK + KV_LORA_RANK + QK_ROPE_DIM],
                                      dw.pop("w_gate")], axis=1)
        dw["w_uq"] = dw["w_uq"].reshape(Q_LORA_RANK, N_HEADS, HEAD_PAD)[:, :, :QK_DIM].reshape(Q_LORA_RANK,
                                                                                                 N_HEADS * QK_DIM)
    return {n: dw[n] if n in UP_PIECES else _to_stacked(n, dw[n]) for n in BIG if n in dw}


def _rope_tables(positions):
    inv_freq = ROPE_THETA ** (-jnp.arange(0, QK_ROPE_DIM, 2, dtype=F32) / QK_ROPE_DIM)
    ang = positions.astype(F32).reshape(-1)[:, None] * inv_freq
    cos, sin = jnp.cos(ang), jnp.sin(ang)
    z = jnp.zeros((ang.shape[0], LANES - QK_ROPE_DIM), F32)
    return jnp.concatenate([cos, cos, z], axis=1), jnp.concatenate([-sin, sin, z], axis=1)


def _pack_small(vals):
    parts, total = [], 0
    for n in PACKED:
        f = vals[n].reshape(-1).astype(F32)
        pad = (-f.shape[0]) % (8 * LANES)
        parts.append(jnp.pad(f, (0, pad)))
        total += f.shape[0] + pad
    parts.append(jnp.zeros(((-total) % (64 * LANES),), F32))
    return jnp.concatenate(parts).reshape(-1, LANES)


def _unpack_small(packed, like):
    flat = packed.reshape(-1)
    out, off = {}, 0
    for n in PACKED:
        size = like[n].size
        out[n] = flat[off:off + size].reshape(like[n].shape)
        off += size + ((-size) % (8 * LANES))
    return out


def _ffn_fwd(x, g, wu4, wd, tag, dep=None):
    h = rms_fwd(x, g, dep=dep, name=f"{tag}_norm")
    gate, up, a = ffn_up_act(h, wu4, name=f"{tag}_up_act")
    y = mm(a, wd, res=x, alpha=0.5, name=f"{tag}_down")
    return y, (x, h, gate, up, a)


def _ffn_bwd(dy, saved, g, wu4, wd, tag, dep=None):
    x, h, gate, up, a = saved
    dgate, dup = ffn_down_dx_act(dy, wd, gate, up, dep=dep, name=f"{tag}_down_dx_act")
    dwd = mm(a, dy, ta=True, alpha=0.5, out_dtype=BF16, name=f"{tag}_down_dw")
    dwu4 = ffn_up_dw(h, dgate, dup, name=f"{tag}_up_dw")
    dx, dg = ffn_up_dx_norm(dgate, dup, wu4, x, g, dy, name=f"{tag}_up_dx_norm")
    return dx, dg, dwu4, dwd


def _mix_fwd(x, p, w, cs, sn, B, dep=None):
    h = rms_fwd(x, p["norm_mix"], dep=dep, name="mix_norm")
    lat, xp, gl = mix_in(h, w["w_lat"], w["w_pool"], w["w_gate"], name="mix_in")
    mixed = pool_fwd(xp, p["pool_maps"].astype(BF16), p["pool_scale"], B, name="pool_fwd")
    ba = mm(mixed, w["w_pool_proj"], out_dtype=BF16, name="mix_pool_proj")
    qn, kvn, kr = lat_fwd(lat, p["q_latent_norm"], p["kv_latent_norm"], cs, sn, name="lat_fwd")
    kv = mm(kvn, w["w_ukv"], out_dtype=BF16, name="mix_ukv")
    o, lse, q = attn_fwd(mm(qn, w["w_uq"], name="mix_uq"), kv, kr, cs, sn, B, name="attn_fwd")
    bb, merged = attn_proj_gate(o, w["w_attn_proj"], gl, p["b_gate"], ba, name="mix_attn_proj_gate")
    y = mm(merged, w["w_out"], res=x, name="mix_out")
    return y, (x, h, lat, xp, gl, mixed, ba, qn, kvn, kr, q, kv, o, lse, bb, merged)


def _mix_bwd(dy, saved, p, w, cs, sn, B, dep=None):
    x, h, lat, xp, gl, mixed, ba, qn, kvn, kr, q, kv, o, lse, bb, merged = saved
    dw, ds = {}, {}
    dw["w_out"] = mm(merged, dy, ta=True, out_dtype=BF16, name="mix_out_dw")
    dba, dbb, dgl, ds["b_gate"] = out_dx_gate(dy, w["w_out"], gl, p["b_gate"], ba, bb, dep=dep,
                                              name="mix_out_dx_gate")
    dw["w_attn_proj"] = mm(o, dbb, ta=True, out_dtype=BF16, name="mix_attn_proj_dw")
    do = mm(dbb, w["w_attn_proj"], tb=True, out_dtype=BF16, name="mix_attn_proj_dx")
    dw["w_pool_proj"] = mm(mixed, dba, ta=True, out_dtype=BF16, name="mix_pool_proj_dw")
    dmixed = mm(dba, w["w_pool_proj"], tb=True, name="mix_pool_proj_dx")
    dxp, ds["pool_maps"], ds["pool_scale"] = pool_bwd(xp, dmixed, p["pool_maps"].astype(BF16), p["pool_scale"], B,
                                                      name="pool_bwd")
    dqb, dkv, dkr = attn_bwd(q, kv, kr, o, do, lse, cs, sn, B, name="attn_bwd")
    dw["w_ukv"] = mm(kvn, dkv, ta=True, out_dtype=BF16, name="mix_ukv_dw")
    dkvn = mm(dkv, w["w_ukv"], tb=True, name="mix_ukv_dx")
    dw["w_uq"] = mm(qn, dqb, ta=True, out_dtype=BF16, name="mix_uq_dw")
    dqn = mm(dqb, w["w_uq"], tb=True, name="mix_uq_dx")
    dlat, ds["q_latent_norm"], ds["kv_latent_norm"] = lat_bwd(lat, p["q_latent_norm"], p["kv_latent_norm"], dqn, dkvn,
                                                               dkr, cs, sn, name="lat_bwd")
    dw["w_lat"] = mm(h, dlat, ta=True, out_dtype=BF16, name="mix_lat_dw")
    dw["w_pool"] = mm(h, dxp, ta=True, out_dtype=BF16, name="mix_pool_in_dw")
    dw["w_gate"] = mm(h, dgl, ta=True, out_dtype=BF16, name="mix_gate_in_dw")
    dx, ds["norm_mix"] = mix_in_dx_norm(dlat, dxp, dgl, w["w_lat"], w["w_pool"], w["w_gate"], x, p["norm_mix"], dy,
                                        name="mix_in_dx_norm")
    return dx, dw, ds


def kernel(x, positions, norm_ffn1, ffn1_up, ffn1_down, norm_mix, w_in, b_gate, pool_maps, pool_scale, w_pool_proj, q_latent_norm, w_uq, kv_latent_norm, w_ukv, w_attn_proj, w_out, norm_ffn2, ffn2_up, ffn2_down, final_norm, loss_target, m_norm_ffn1, m_ffn1_up, m_ffn1_down, m_norm_mix, m_w_in, m_b_gate, m_pool_maps, m_pool_scale, m_w_pool_proj, m_q_latent_norm, m_w_uq, m_kv_latent_norm, m_w_ukv, m_w_attn_proj, m_w_out, m_norm_ffn2, m_ffn2_up, m_ffn2_down, m_final_norm, v_norm_ffn1, v_ffn1_up, v_ffn1_down, v_norm_mix, v_w_in, v_b_gate, v_pool_maps, v_pool_scale, v_w_pool_proj, v_q_latent_norm, v_w_uq, v_kv_latent_norm, v_w_ukv, v_w_attn_proj, v_w_out, v_norm_ffn2, v_ffn2_up, v_ffn2_down, v_final_norm):
    given = dict(locals())
    B, S, D = x.shape
    T = B * S
    L = norm_ffn1.shape[0]
    W = {n: given[n] for n in WEIGHTS}
    Mo = {n: given["m_" + n] for n in WEIGHTS}
    Vo = {n: given["v_" + n] for n in WEIGHTS}
    core = lax.axis_index("c").astype(jnp.int32)
    chip = (2 * lax.axis_index("x") + lax.axis_index("y")).astype(jnp.int32)

    core_arr = core.reshape(1)
    chip_arr = chip.reshape(1)
    place = jnp.stack([chip, core])
    first = ("ffn1_up", "ffn1_down")
    rest = tuple(n for n in BIG if n not in first)

    own = [{n: W[n][l].astype(BF16) for n in BIG} for l in range(L)]

    def ag_begin(l, names, tag, after=None):
        lands = [jax.ShapeDtypeStruct((4,) + own[l][n].shape, BF16) for n in names]
        return exchange_begin(f"ag_start_{tag}", [own[l][n] for n in names], lands, ag_plan, 3 * len(names), after)

    def ag_finish(handle, names, tag, after):
        mine, lands = exchange_end(f"ag_wait_{tag}", handle, ag_plan, after)
        lands = ag_forward(lands, name=f"ag_forward_{tag}")
        return _layer_weights(dict(zip(names, place_own(lands, mine, chip_arr, name=f"place_own_{tag}"))))

    h_first, t1 = ag_begin(0, first, "0a")
    cs, sn = _rope_tables(positions)
    xs = x.reshape(T, D) + t1[0, 0]
    saved, ici, handed = [], None, None
    for l in range(L):
        p = {n: W[n][l] for n in SMALL if n != "final_norm"}
        dep = None
        if l == 0:
            w = ag_finish(h_first, first, "0a", xs)
            h_rest, dep = ag_begin(0, rest, "0b", after=w["ffn1_down"])
        else:
            mine, handle = handed
            lands, _ = exchange_end(f"ag_forward_wait_{l}", handle, ag_forward_plan, xs)
            w = _layer_weights(dict(zip(BIG, place_own(lands, mine, chip_arr, name=f"place_own_{l}"))))
        xs, s1 = _ffn_fwd(xs, p["norm_ffn1"], w["ffn1_up"], w["ffn1_down"], "ffn1", dep=dep)
        dep = None
        if l == 0:
            w.update(ag_finish(h_rest, rest, "0b", xs))
            if L > 1:
                ici, dep = ag_begin(1, BIG, "1", after=xs)
        xs, s2 = _mix_fwd(xs, p, w, cs, sn, B, dep=dep)
        dep = None
        if l + 1 < L:
            mine, lands = exchange_end(f"ag_wait_{l + 1}", ici, ag_plan, xs)
            handle, dep = exchange_begin(f"ag_forward_start_{l + 1}", lands, [], ag_forward_plan, 3 * len(BIG))
            handed = (mine, handle)
            if l + 2 < L:
                ici, dep = ag_begin(l + 2, BIG, str(l + 2), after=dep)
        xs, s3 = _ffn_fwd(xs, p["norm_ffn2"], w["ffn2_up"], w["ffn2_down"], "ffn2", dep=dep)
        saved.append((w, p, s1, s2, s3))

    dx, dfinal, loss_tile = loss_head(xs, final_norm, loss_target.reshape(T, D), name="loss_head")
    loss = lax.psum(loss_tile[0, 0], ("x", "y", "c"))

    def rs_begin(dw, tag, after=None):
        stacked = _layer_grads_stacked(dw)
        names = tuple(stacked)
        parts = [stacked[n] for n in names]
        r1 = rs_swap_halves(parts, after=after, name=f"rs_swap_{tag}")
        sums = [rs_chip_sum(g, a, core_arr, name=f"rs_chip_sum_{n}") for n, g, a in zip(names, parts, r1)]
        lands = [jax.ShapeDtypeStruct((3,) + s.shape[1:], BF16) for s in sums]
        handle, token = exchange_begin(f"rs_start_{tag}", sums, lands, rs_plan, 3 * len(names))
        return (names, parts, r1, handle), token

    acc = {n: lax.empty(W[n].shape, F32) for n in BIG}

    def rs_finish(l, pending, tag, after):
        names, parts, r1, handle = pending
        _, r2 = exchange_end(f"rs_wait_{tag}", handle, rs_plan, after)
        for n, g, a, b in zip(names, parts, r1, r2):
            acc[n] = rs_final_sum(g, a, b, place, acc[n], l, name=f"rs_final_sum_{n}")

    small_layers, pending, swapping, dep = [], [], None, None
    for l in reversed(range(L)):
        w, p, s1, s2, s3 = saved[l]
        dx, dg2, dwu2, dwd2 = _ffn_bwd(dx, s3, p["norm_ffn2"], w["ffn2_up"], w["ffn2_down"], "ffn2", dep=dep)
        dep = None
        if swapping is not None:
            above, names, handle = swapping
            parts, r1 = exchange_end(f"rs_swap_wait_{above}", handle, rs_swap_plan, dx)
            sums = [rs_chip_sum(g, a, core_arr, name=f"rs_chip_sum_{n}") for n, g, a in zip(names, parts, r1)]
            lands = [jax.ShapeDtypeStruct((3,) + s.shape[1:], BF16) for s in sums]
            handle, dep = exchange_begin(f"rs_start_{above}", sums, lands, rs_plan, 3 * len(names))
            pending.append((above, (names, parts, r1, handle)))
            swapping = None
        dx, dw, ds = _mix_bwd(dx, s2, p, w, cs, sn, B, dep=dep)
        dw.update(ffn2_up=dwu2, ffn2_down=dwd2)
        dep = None
        if l == 0:
            early, dep = rs_begin(dw, "0b")
            dw = {}
        dx, dg1, dwu1, dwd1 = _ffn_bwd(dx, s1, p["norm_ffn1"], w["ffn1_up"], w["ffn1_down"], "ffn1", dep=dep)
        dw.update(ffn1_up=dwu1, ffn1_down=dwd1)
        ds.update(norm_ffn1=dg1, norm_ffn2=dg2)
        small_layers.append(ds)
        if l == 0:
            last_dw = dw
        else:
            stacked = _layer_grads_stacked(dw)
            halves = [jax.ShapeDtypeStruct((4, g.shape[1] // 2, g.shape[2]), BF16) for g in stacked.values()]
            handle, dep = exchange_begin(f"rs_swap_start_{l}", list(stacked.values()), halves, rs_swap_plan,
                                         len(stacked))
            swapping = (l, tuple(stacked), handle)
    small_layers.reverse()

    small = {n: jnp.stack([small_layers[l][n].reshape(W[n].shape[1:]) for l in range(L)]) for n in SMALL
             if n != "final_norm"}
    small["final_norm"] = dfinal.reshape(final_norm.shape)
    rows = lambda a: a.reshape(-1, LANES)
    reduced, reduced_maps = all_reduce_small([_pack_small(small), rows(small["pool_maps"])])
    grads = _unpack_small(reduced, small)
    grads["pool_maps"] = reduced_maps.reshape(pool_maps.shape)
    last, dep = rs_begin(last_dw, "0a", after=reduced)
    delta, new_m, new_v = {}, {}, {}
    d, nm, nv = adamw(_pack_small(W), reduced, _pack_small(Mo), _pack_small(Vo), dep=dep, name="adamw_small")
    delta.update(_unpack_small(d, W))
    new_m.update(_unpack_small(nm, W))
    new_v.update(_unpack_small(nv, W))
    d, nm, nv = adamw(rows(pool_maps), reduced_maps, rows(m_pool_maps), rows(v_pool_maps), dep=d,
                      name="adamw_pool_maps")
    delta["pool_maps"], new_m["pool_maps"], new_v["pool_maps"] = (a.reshape(pool_maps.shape) for a in (d, nm, nv))

    def update(names, tag, d):
        joined = rs_join_halves([acc[n] for n in names], name=f"rs_join_{tag}")
        for n, g in zip(names, joined):
            flip = (lambda a: jnp.swapaxes(a, 1, 2)) if n == "w_in" else (lambda a: a)
            sh = flip(W[n]).shape
            two = lambda a: flip(a).reshape(sh[0] * sh[1], sh[2])
            back = lambda a: flip(a.reshape(sh))
            gc, d, nm, nv = adamw(two(W[n]), two(g), two(Mo[n]), two(Vo[n]), dep=d, copy_g=True, name=f"adamw_{n}")
            grads[n], delta[n], new_m[n], new_v[n] = back(gc), back(d), back(nm), back(nv)
        return d

    for l, item in pending:
        rs_finish(l, item, str(l), d)
    rs_finish(0, early, "0b", d)
    d = update(rest, "rest", d)
    rs_finish(0, last, "0a", d)
    update(first, "first", d)

    return (loss, dx.reshape(B, S, D), *[grads[n] for n in WEIGHTS], *[delta[n] for n in WEIGHTS],
            *[new_m[n] for n in WEIGHTS], *[new_v[n] for n in WEIGHTS])
```
